```python
import math
import jax
import jax.numpy as jnp
from jax import lax
import numpy as np

D_MODEL = 1024
BATCH = 4
SEQ = 4096
DEPTH = 2

CTX_LEN = 256
GRID_W = 64
HEAD_DIM = 64
D_MIX = D_MODEL
SSD_WIDTH = 3 * D_MIX // 8
SSD_HEADS = SSD_WIDTH // HEAD_DIM
SSD_GROUPS = 2
SSD_STATE = 64
SSD_CHUNK = 128
HYENA_WIDTH = D_MIX // 4
HYENA_ORDER = 2
HYENA_BANDS = 16
HYENA_POS_DIM = 1 + 2 * HYENA_BANDS
HYENA_FILTER_HIDDEN = 64
MLSTM_WIDTH = D_MIX - SSD_WIDTH - HYENA_WIDTH
MLSTM_HEADS = MLSTM_WIDTH // HEAD_DIM
MLSTM_CHUNK = 128
SHORT_CONV = 3
N_GROUPS = 4
EXPERTS_PER_GROUP = 8
N_EXPERTS = N_GROUPS * EXPERTS_PER_GROUP
TOP_K_IN_GROUP = 2
EXPERT_HIDDEN = 256
N_MOD = 6
EPS = 1e-6
SSD_CONV_CH = SSD_WIDTH + 2 * SSD_GROUPS * SSD_STATE
IN_SIZES = (SSD_WIDTH, SSD_CONV_CH, 2 * SSD_HEADS, (HYENA_ORDER + 1) * HYENA_WIDTH,
            2 * MLSTM_WIDTH, MLSTM_WIDTH, MLSTM_WIDTH, 4 * MLSTM_HEADS)
IN_COLS = sum(IN_SIZES)

kernel_name = 'hybrid_ssd_hyena_mlstm_hmoe_dit'


def split_sizes(u, sizes):
    cuts, s = [], 0
    for n in sizes[:-1]:
        s += n
        cuts.append(s)
    return jnp.split(u, cuts, axis=-1)


def rmsnorm(x, w):
    xf = x.astype(jnp.float32)
    y = xf * lax.rsqrt(jnp.mean(xf * xf, axis=-1, keepdims=True) + EPS)
    return (y * w.astype(jnp.float32)).astype(x.dtype)


def modulated_norm(x, w, shift, scale):
    return rmsnorm(x, w) * (1 + scale[:, None, :]) + shift[:, None, :]


def short_conv(x, w, b):
    L = x.shape[1]
    pad = w.shape[0] // 2
    xp = jnp.pad(x, ((0, 0), (pad, pad), (0, 0)))
    return sum(xp[:, k:k + L] * w[k] for k in range(w.shape[0])) + b


def flip(t):
    return t[:, ::-1]


def to_col_major(u):
    b, L, ch = u.shape
    rows = L // GRID_W
    return u.reshape(b, rows, GRID_W, ch).transpose(0, 2, 1, 3).reshape(b, L, ch)


def to_row_major(u):
    b, L, ch = u.shape
    rows = L // GRID_W
    return u.reshape(b, GRID_W, rows, ch).transpose(0, 2, 1, 3).reshape(b, L, ch)


def zero_states(b):
    f32 = jnp.float32
    ssd0 = jnp.zeros((b, SSD_HEADS, HEAD_DIM, SSD_STATE), f32)
    ml0 = (jnp.zeros((b, MLSTM_HEADS, HEAD_DIM, HEAD_DIM), f32),
           jnp.zeros((b, MLSTM_HEADS, HEAD_DIM), f32),
           jnp.zeros((b, MLSTM_HEADS), f32))
    return (ssd0, ssd0), (ml0, ml0)


def ssd_scan(xh, dt, a, bh, ch, init, want_y):
    b, L, h, pd = xh.shape
    n = bh.shape[-1]
    q = SSD_CHUNK
    nc = L // q
    cum = jnp.cumsum((dt * a).reshape(b, nc, q, h).transpose(0, 1, 3, 2), axis=-1)
    xdt = (xh * dt[..., None]).reshape(b, nc, q, h, pd)
    bc = bh.reshape(b, nc, q, h, n)
    cc = ch.reshape(b, nc, q, h, n)
    to_end = jnp.exp(cum[..., -1:] - cum)
    chunk_states = jnp.einsum('bcqhn,bchq,bcqhp->bchpn', bc, to_end, xdt)
    chunk_decay = jnp.exp(cum[..., -1])

    def step(state, inp):
        s_c, d_c = inp
        return state * d_c[..., None, None] + s_c, state

    final, entering = lax.scan(step, init, (jnp.moveaxis(chunk_states, 1, 0), jnp.moveaxis(chunk_decay, 1, 0)))
    if not want_y:
        return None, final
    entering = jnp.moveaxis(entering, 0, 1)
    causal = jnp.tril(jnp.ones((q, q), dtype=bool))
    seg = jnp.exp(jnp.where(causal, cum[..., :, None] - cum[..., None, :], -jnp.inf))
    scores = jnp.einsum('bclhn,bcshn->bchls', cc, bc) * seg
    y = jnp.einsum('bchls,bcshp->bclhp', scores, xdt)
    y = y + jnp.einsum('bclhn,bchpn->bclhp', cc, entering) * jnp.exp(cum).transpose(0, 1, 3, 2)[..., None]
    return y.reshape(b, L, h, pd), final


def ssd_mixer(z, xbc, dt_raw, p, init, want_y):
    b, L, _ = z.shape
    f32 = jnp.float32
    xbc = jax.nn.silu(short_conv(xbc, p['ssd_conv_w'], p['ssd_conv_b']))
    xs, bs, cs = split_sizes(xbc, (SSD_WIDTH, SSD_GROUPS * SSD_STATE, SSD_GROUPS * SSD_STATE))
    rep = SSD_HEADS // SSD_GROUPS
    xh = xs.reshape(b, L, SSD_HEADS, HEAD_DIM).astype(f32)
    bh = jnp.repeat(bs.reshape(b, L, SSD_GROUPS, SSD_STATE), rep, axis=2).astype(f32)
    chh = jnp.repeat(cs.reshape(b, L, SSD_GROUPS, SSD_STATE), rep, axis=2).astype(f32)
    dt = jax.nn.softplus(dt_raw.reshape(b, L, 2, SSD_HEADS).astype(f32) + p['ssd_dt_bias'])
    a = -jnp.exp(p['ssd_a_log'].astype(f32))
    y_f, s_f = ssd_scan(xh, dt[:, :, 0], a[0], bh, chh, init[0], want_y)
    y_b, s_b = ssd_scan(flip(xh), flip(dt[:, :, 1]), a[1], flip(bh), flip(chh), init[1], want_y)
    if not want_y:
        return None, (s_f, s_b)
    y = y_f + flip(y_b) + xh * p['ssd_d'][:, None]
    y = y.reshape(b, L, SSD_WIDTH).astype(z.dtype)
    return rmsnorm(y * jax.nn.silu(z), p['ssd_norm_w']), (s_f, s_b)


def mlstm_scan(q, k, v, i_pre, f_pre, init, want_y):
    b, L, h, d = q.shape
    qn = MLSTM_CHUNK
    nc = L // qn
    chunk = lambda t: t.reshape(b, nc, qn, h, d).transpose(0, 1, 3, 2, 4)
    gate = lambda t: t.reshape(b, nc, qn, h).transpose(0, 1, 3, 2)
    qc, kc, vc = chunk(q), chunk(k) * (d ** -0.5), chunk(v)
    ig = gate(i_pre)
    cum_f = jnp.cumsum(gate(jax.nn.log_sigmoid(f_pre)), axis=-1)
    tot_f = cum_f[..., -1]
    w_end = tot_f[..., None] - cum_f + ig
    m_loc = jnp.max(w_end, axis=-1)
    e_end = jnp.exp(w_end - m_loc[..., None])
    c_loc = jnp.einsum('bchq,bchqv,bchqk->bchvk', e_end, vc, kc)
    n_loc = jnp.einsum('bchq,bchqk->bchk', e_end, kc)

    def step(carry, inp):
        c_prev, n_prev, m_prev = carry
        c_l, n_l, m_l, g_l = inp
        m_new = jnp.maximum(g_l + m_prev, m_l)
        a_prev = jnp.exp(g_l + m_prev - m_new)
        a_loc = jnp.exp(m_l - m_new)
        new = (a_prev[..., None, None] * c_prev + a_loc[..., None, None] * c_l,
               a_prev[..., None] * n_prev + a_loc[..., None] * n_l,
               m_new)
        return new, (c_prev, n_prev, m_prev)

    mv = lambda t: jnp.moveaxis(t, 1, 0)
    final, (c_in, n_in, m_in) = lax.scan(step, init, (mv(c_loc), mv(n_loc), mv(m_loc), mv(tot_f)))
    if not want_y:
        return None, final
    c_in, n_in, m_in = mv(c_in), mv(n_in), mv(m_in)
    causal = jnp.tril(jnp.ones((qn, qn), dtype=bool))
    log_w = jnp.where(causal, cum_f[..., :, None] - cum_f[..., None, :] + ig[..., None, :], -jnp.inf)
    inter = cum_f + m_in[..., None]
    m_t = jnp.maximum(inter, jnp.max(log_w, axis=-1))
    s = jnp.einsum('bchld,bchsd->bchls', qc, kc) * jnp.exp(log_w - m_t[..., None])
    a_inter = jnp.exp(inter - m_t)
    num = jnp.einsum('bchls,bchsv->bchlv', s, vc) + a_inter[..., None] * jnp.einsum('bchld,bchvd->bchlv', qc, c_in)
    den = jnp.sum(s, axis=-1) + a_inter * jnp.einsum('bchld,bchd->bchl', qc, n_in)
    hid = num / jnp.maximum(jnp.abs(den), jnp.exp(-m_t))[..., None]
    return hid.transpose(0, 1, 3, 2, 4).reshape(b, L, h, d), final


def head_layernorm(h, w):
    mu = jnp.mean(h, axis=-1, keepdims=True)
    hc = h - mu
    return hc * lax.rsqrt(jnp.mean(hc * hc, axis=-1, keepdims=True) + EPS) * w


def mlstm_mixer(qk, v, o, g, p, init, want_y):
    b, L, _ = v.shape
    f32 = jnp.float32
    qk = jax.nn.silu(short_conv(qk, p['ml_conv_w'], p['ml_conv_b']))
    q, k = jnp.split(qk, 2, axis=-1)
    heads = lambda t: t.reshape(b, L, MLSTM_HEADS, HEAD_DIM).astype(f32)
    q, k, vh = heads(q), heads(k), heads(v)
    gates = g.reshape(b, L, 2, 2, MLSTM_HEADS).astype(f32) + p['ml_gate_b']
    h_f, s_f = mlstm_scan(q, k, vh, gates[:, :, 0, 0], gates[:, :, 0, 1], init[0], want_y)
    h_b, s_b = mlstm_scan(flip(q), flip(k), flip(vh), flip(gates[:, :, 1, 0]), flip(gates[:, :, 1, 1]), init[1], want_y)
    if not want_y:
        return None, (s_f, s_b)
    h = head_layernorm(h_f + flip(h_b), p['ml_norm_w'].reshape(MLSTM_HEADS, HEAD_DIM))
    out = jax.nn.sigmoid(o.astype(f32)) * h.reshape(b, L, MLSTM_WIDTH)
    return out.astype(v.dtype), (s_f, s_b)


def hyena_filters(L, p):
    t = jnp.arange(L, dtype=jnp.float32)
    t_unit = t / float(max(L - 1, 1))
    bands = jnp.linspace(1e-4, HYENA_BANDS - 1, HYENA_BANDS, dtype=jnp.float32)
    ang = (2 * math.pi / L) * t[:, None] * bands[None, :]
    feats = jnp.concatenate([t_unit[:, None], jnp.cos(ang), -jnp.sin(ang)], axis=-1)
    freq = p['hy_freq']
    hdn = jnp.sin(freq * (feats @ p['hy_pos_w1'] + p['hy_pos_b1']))
    hdn = jnp.sin(freq * (hdn @ p['hy_pos_w2'] + p['hy_pos_b2']))
    filt = (hdn @ p['hy_pos_w3']).reshape(L, 2, HYENA_ORDER, HYENA_WIDTH)
    window = jnp.exp(-t_unit[:, None, None, None] * jnp.abs(p['hy_decay']))
    return (filt * window).astype(jnp.float32)


def bidir_long_conv(z, h_fwd, h_bwd):
    L, ch = h_fwd.shape
    two_sided = jnp.concatenate([h_fwd, jnp.zeros((1, ch), h_fwd.dtype), h_bwd[1:][::-1]], axis=0)
    two_sided = two_sided / jnp.sum(jnp.abs(two_sided), axis=0, keepdims=True)
    zf = jnp.fft.rfft(z.astype(jnp.float32), n=2 * L, axis=1)
    hf = jnp.fft.rfft(two_sided, n=2 * L, axis=0)
    return jnp.fft.irfft(zf * hf[None], n=2 * L, axis=1)[:, :L]


def hyena_mixer(u, p):
    uc = short_conv(u, p['hy_conv_w'], p['hy_conv_b'])
    v, *gates = jnp.split(uc.astype(jnp.float32), HYENA_ORDER + 1, axis=-1)
    filt = hyena_filters(u.shape[1], p)
    z = v
    for order, gate in enumerate(gates):
        z = gate * (bidir_long_conv(z, filt[:, 0, order], filt[:, 1, order]) + p['hy_skip'][order] * z)
    return z.astype(u.dtype)


def token_mixers(u, p, ssd_init, ml_init, want_out):
    z, xbc, dt_raw, hy_u, ml_qk, ml_v, ml_o, ml_g = split_sizes(u, IN_SIZES)
    y_ssd, ssd_fin = ssd_mixer(z, xbc, dt_raw, p, ssd_init, want_out)
    y_ml, ml_fin = mlstm_mixer(ml_qk, ml_v, ml_o, ml_g, p, ml_init, want_out)
    if not want_out:
        return None, ssd_fin, ml_fin
    y_hy = hyena_mixer(hy_u, p)
    return jnp.concatenate([y_ssd, y_hy, y_ml], axis=-1), ssd_fin, ml_fin


def hier_moe(h, p):
    b, L, dm = h.shape
    t = h.reshape(b * L, dm)
    grp_prob = jax.nn.softmax((t @ p['grp_router_w'] + p['grp_router_b']).astype(jnp.float32), axis=-1)
    grp_p, grp_idx = lax.top_k(grp_prob, 1)
    exp_logits = (t @ p['exp_router_w'] + p['exp_router_b']).astype(jnp.float32)
    exp_logits = exp_logits.reshape(-1, N_GROUPS, EXPERTS_PER_GROUP)
    in_grp = jnp.take_along_axis(exp_logits, grp_idx[:, :, None], axis=1)[:, 0]
    top_p, top_idx = lax.top_k(jax.nn.softmax(in_grp, axis=-1), TOP_K_IN_GROUP)
    top_w = grp_p * top_p / jnp.sum(top_p, axis=-1, keepdims=True)
    expert_w = jnp.sum(jax.nn.one_hot(top_idx, EXPERTS_PER_GROUP) * top_w[..., None], axis=1)
    group_mask = jax.nn.one_hot(grp_idx[:, 0], N_GROUPS)
    out = jnp.zeros_like(t)
    for g in range(N_GROUPS):
        gate_w = (group_mask[:, g:g + 1] * expert_w).astype(t.dtype)
        hid = jax.nn.silu(jnp.einsum('td,edf->tef', t, p['moe_w_gate'][g])) * jnp.einsum('td,edf->tef', t, p['moe_w_up'][g])
        out = out + jnp.einsum('tef,efd->td', hid * gate_w[..., None], p['moe_w_down'][g])
    return out.reshape(b, L, dm)


def setup_inputs(seed: int = 0) -> dict:
    key = jax.random.key(seed)
    keys = list(jax.random.split(key, 48))
    f32 = jnp.float32

    def nrm(shape, scale):
        return jax.random.normal(keys.pop(), shape, f32) * scale

    def gain(shape):
        return 1.0 + nrm(shape, 0.02)

    def unif(shape, lo, hi):
        return jax.random.uniform(keys.pop(), shape, f32, minval=lo, maxval=hi)

    D, FH, HW = D_MODEL, HYENA_FILTER_HIDDEN, HYENA_WIDTH
    dt0 = jnp.exp(unif((DEPTH, 2, SSD_HEADS), math.log(1e-3), math.log(1e-1)))
    decay_lo, decay_hi = -math.log(1e-2) / 1.5, -math.log(1e-2) / 0.3
    decay = jnp.broadcast_to(jnp.linspace(decay_lo, decay_hi, HW, dtype=f32), (DEPTH, 2, HYENA_ORDER, HW))
    f_bias = jnp.linspace(3.0, 6.0, MLSTM_HEADS, dtype=f32) + nrm((DEPTH, 2, MLSTM_HEADS), 0.02)
    i_bias = nrm((DEPTH, 2, MLSTM_HEADS), 0.1)
    return {
        'x': nrm((BATCH, SEQ, D), 1.0),
        'c': nrm((BATCH, D), 1.0),
        'ctx': nrm((BATCH, CTX_LEN, D), 1.0),
        'c_ctx': nrm((D,), 1.0),
        'w_mod': nrm((DEPTH, D, N_MOD * D), 0.5 * D ** -0.5),
        'b_mod': nrm((DEPTH, N_MOD * D), 0.02),
        'norm1_w': gain((DEPTH, D)),
        'norm2_w': gain((DEPTH, D)),
        'w_in': nrm((DEPTH, D, IN_COLS), D ** -0.5),
        'w_out': nrm((DEPTH, D_MIX, D), D_MIX ** -0.5),
        'ssd_conv_w': nrm((DEPTH, SHORT_CONV, SSD_CONV_CH), SHORT_CONV ** -0.5),
        'ssd_conv_b': nrm((DEPTH, SSD_CONV_CH), 0.02),
        'ssd_dt_bias': dt0 + jnp.log(-jnp.expm1(-dt0)),
        'ssd_a_log': jnp.log(unif((DEPTH, 2, SSD_HEADS), 1.0, 16.0)),
        'ssd_d': gain((DEPTH, SSD_HEADS)),
        'ssd_norm_w': gain((DEPTH, SSD_WIDTH)),
        'hy_conv_w': nrm((DEPTH, SHORT_CONV, (HYENA_ORDER + 1) * HW), SHORT_CONV ** -0.5),
        'hy_conv_b': nrm((DEPTH, (HYENA_ORDER + 1) * HW), 0.02),
        'hy_pos_w1': nrm((DEPTH, HYENA_POS_DIM, FH), HYENA_POS_DIM ** -0.5),
        'hy_pos_b1': nrm((DEPTH, FH), 0.02),
        'hy_pos_w2': nrm((DEPTH, FH, FH), FH ** -0.5),
        'hy_pos_b2': nrm((DEPTH, FH), 0.02),
        'hy_pos_w3': nrm((DEPTH, FH, 2 * HYENA_ORDER * HW), FH ** -0.5),
        'hy_freq': gain((DEPTH, FH)),
        'hy_decay': decay + nrm((DEPTH, 2, HYENA_ORDER, HW), 0.02),
        'hy_skip': nrm((DEPTH, HYENA_ORDER, HW), 1.0),
        'ml_conv_w': nrm((DEPTH, SHORT_CONV, 2 * MLSTM_WIDTH), SHORT_CONV ** -0.5),
        'ml_conv_b': nrm((DEPTH, 2 * MLSTM_WIDTH), 0.02),
        'ml_gate_b': jnp.stack([i_bias, f_bias], axis=2),
        'ml_norm_w': gain((DEPTH, MLSTM_WIDTH)),
        'grp_router_w': nrm((DEPTH, D, N_GROUPS), D ** -0.5),
        'grp_router_b': nrm((DEPTH, N_GROUPS), 0.01),
        'exp_router_w': nrm((DEPTH, D, N_EXPERTS), D ** -0.5),
        'exp_router_b': nrm((DEPTH, N_EXPERTS), 0.01),
        'moe_w_gate': nrm((DEPTH, N_GROUPS, EXPERTS_PER_GROUP, D, EXPERT_HIDDEN), D ** -0.5),
        'moe_w_up': nrm((DEPTH, N_GROUPS, EXPERTS_PER_GROUP, D, EXPERT_HIDDEN), D ** -0.5),
        'moe_w_down': nrm((DEPTH, N_GROUPS, EXPERTS_PER_GROUP, EXPERT_HIDDEN, D), EXPERT_HIDDEN ** -0.5),
        'final_norm_w': gain((D,)),
    }


def reference(x, c, ctx, c_ctx, w_mod, b_mod, norm1_w, norm2_w, w_in, w_out,
              ssd_conv_w, ssd_conv_b, ssd_dt_bias, ssd_a_log, ssd_d, ssd_norm_w,
              hy_conv_w, hy_conv_b, hy_pos_w1, hy_pos_b1, hy_pos_w2, hy_pos_b2, hy_pos_w3,
              hy_freq, hy_decay, hy_skip, ml_conv_w, ml_conv_b, ml_gate_b, ml_norm_w,
              grp_router_w, grp_router_b, exp_router_w, exp_router_b,
              moe_w_gate, moe_w_up, moe_w_down, final_norm_w):
    layer_params = (
        ('ssd_conv_w', ssd_conv_w), ('ssd_conv_b', ssd_conv_b), ('ssd_dt_bias', ssd_dt_bias),
        ('ssd_a_log', ssd_a_log), ('ssd_d', ssd_d), ('ssd_norm_w', ssd_norm_w),
        ('hy_conv_w', hy_conv_w), ('hy_conv_b', hy_conv_b), ('hy_pos_w1', hy_pos_w1),
        ('hy_pos_b1', hy_pos_b1), ('hy_pos_w2', hy_pos_w2), ('hy_pos_b2', hy_pos_b2),
        ('hy_pos_w3', hy_pos_w3), ('hy_freq', hy_freq), ('hy_decay', hy_decay), ('hy_skip', hy_skip),
        ('ml_conv_w', ml_conv_w), ('ml_conv_b', ml_conv_b), ('ml_gate_b', ml_gate_b), ('ml_norm_w', ml_norm_w),
        ('grp_router_w', grp_router_w), ('grp_router_b', grp_router_b),
        ('exp_router_w', exp_router_w), ('exp_router_b', exp_router_b),
        ('moe_w_gate', moe_w_gate), ('moe_w_up', moe_w_up), ('moe_w_down', moe_w_down),
    )
    xl, xc = x, ctx
    ssd_zero, ml_zero = zero_states(ctx.shape[0])
    for i in range(DEPTH):
        last = i == DEPTH - 1
        p = {name: arr[i] for name, arr in layer_params}
        mod_l = jnp.split(jax.nn.silu(c) @ w_mod[i] + b_mod[i], N_MOD, axis=-1)
        mod_c = jnp.split(jax.nn.silu(c_ctx)[None, :] @ w_mod[i] + b_mod[i], N_MOD, axis=-1)
        uc = modulated_norm(xc, norm1_w[i], mod_c[0], mod_c[1]) @ w_in[i]
        yc, ssd_ctx, ml_ctx = token_mixers(uc, p, ssd_zero, ml_zero, not last)
        ul = modulated_norm(xl, norm1_w[i], mod_l[0], mod_l[1]) @ w_in[i]
        col_major = i % 2 == 1
        if col_major:
            ul = to_col_major(ul)
        yl, _, _ = token_mixers(ul, p, ssd_ctx, ml_ctx, True)
        if col_major:
            yl = to_row_major(yl)
        xl = xl + mod_l[2][:, None, :] * (yl @ w_out[i])
        xl = xl + mod_l[5][:, None, :] * hier_moe(modulated_norm(xl, norm2_w[i], mod_l[3], mod_l[4]), p)
        if not last:
            xc = xc + mod_c[2][:, None, :] * (yc @ w_out[i])
            xc = xc + mod_c[5][:, None, :] * hier_moe(modulated_norm(xc, norm2_w[i], mod_c[3], mod_c[4]), p)
    return rmsnorm(xl, final_norm_w)
```

```python
import functools
import math

import jax
import jax.numpy as jnp
from jax import lax
from jax.experimental import pallas as pl
from jax.experimental.pallas import tpu as pltpu

D_MODEL = 1024
DEPTH = 2
GRID_W = 64
HEAD_DIM = 64
D_MIX = D_MODEL
SSD_WIDTH = 3 * D_MIX // 8
SSD_HEADS = SSD_WIDTH // HEAD_DIM
SSD_GROUPS = 2
SSD_STATE = 64
SSD_CHUNK = 128
HYENA_WIDTH = D_MIX // 4
HYENA_ORDER = 2
HYENA_BANDS = 16
MLSTM_WIDTH = D_MIX - SSD_WIDTH - HYENA_WIDTH
MLSTM_HEADS = MLSTM_WIDTH // HEAD_DIM
MLSTM_CHUNK = 128
N_GROUPS = 4
EXPERTS_PER_GROUP = 8
N_EXPERTS = N_GROUPS * EXPERTS_PER_GROUP
EXPERT_HIDDEN = 256
N_MOD = 6
EPS = 1e-6
SSD_CONV_CH = SSD_WIDTH + 2 * SSD_GROUPS * SSD_STATE
IN_SIZES = (SSD_WIDTH, SSD_CONV_CH, 2 * SSD_HEADS, (HYENA_ORDER + 1) * HYENA_WIDTH,
            2 * MLSTM_WIDTH, MLSTM_WIDTH, MLSTM_WIDTH, 4 * MLSTM_HEADS)

LANE = 128
IN_PAD_SIZES = tuple(-(-n // LANE) * LANE for n in IN_SIZES)
IN_PAD_COLS = sum(IN_PAD_SIZES)
ROUTER_COLS = LANE
VMEM_LIMIT = 48 * 1024 * 1024

F32 = jnp.float32
BF16 = jnp.bfloat16


def _params(*sem):
    return pltpu.CompilerParams(dimension_semantics=sem, vmem_limit_bytes=VMEM_LIMIT)


def _mod_kernel(c_ref, w_ref, b_ref, o_ref):
    c = c_ref[...]
    h = c * jax.nn.sigmoid(c)
    o_ref[...] = jnp.dot(h, w_ref[...], preferred_element_type=F32,
                         precision=lax.Precision.HIGHEST) + b_ref[...]


def modulation(c_rows, w_mod, b_mod):
    n = w_mod.shape[1]
    tn = 1536
    return pl.pallas_call(
        _mod_kernel,
        out_shape=jax.ShapeDtypeStruct((c_rows.shape[0], n), F32),
        grid=(n // tn,),
        in_specs=[pl.BlockSpec(c_rows.shape, lambda j: (0, 0)),
                  pl.BlockSpec((w_mod.shape[0], tn), lambda j: (0, j)),
                  pl.BlockSpec((1, tn), lambda j: (0, j))],
        out_specs=pl.BlockSpec((c_rows.shape[0], tn), lambda j: (0, j)),
        compiler_params=_params("arbitrary"),
        name="adaln_modulation",
    )(c_rows, w_mod, b_mod.reshape(1, n))


def _modnorm(x, nw, shift, scale):
    y = x * lax.rsqrt(jnp.mean(x * x, axis=-1, keepdims=True) + EPS) * nw
    return y * (1.0 + scale) + shift


def _norm_proj_kernel(x_ref, nw_ref, sh_ref, sc_ref, w_ref, o_ref):
    h = _modnorm(x_ref[0], nw_ref[...], sh_ref[0], sc_ref[0])
    o_ref[0] = jnp.dot(h.astype(BF16), w_ref[...], preferred_element_type=F32)


def norm_proj(x, nw, shift, scale, w_bf16, tm=256):
    b, L, d = x.shape
    n = w_bf16.shape[1]
    per_batch = scale.shape[0] == b
    mod_map = (lambda bi, i: (bi, 0, 0)) if per_batch else (lambda bi, i: (0, 0, 0))
    return pl.pallas_call(
        _norm_proj_kernel,
        out_shape=jax.ShapeDtypeStruct((b, L, n), F32),
        grid=(b, L // tm),
        in_specs=[pl.BlockSpec((1, tm, d), lambda bi, i: (bi, i, 0)),
                  pl.BlockSpec((1, d), lambda bi, i: (0, 0)),
                  pl.BlockSpec((1, 1, d), mod_map),
                  pl.BlockSpec((1, 1, d), mod_map),
                  pl.BlockSpec((d, n), lambda bi, i: (0, 0))],
        out_specs=pl.BlockSpec((1, tm, n), lambda bi, i: (bi, i, 0)),
        compiler_params=_params("parallel", "arbitrary"),
        name="norm_in_proj",
    )(x, nw.reshape(1, d), shift, scale, w_bf16)


def _out_proj_kernel(y_ref, x_ref, g_ref, w_ref, o_ref):
    r = jnp.dot(y_ref[0].astype(BF16), w_ref[...], preferred_element_type=F32)
    o_ref[0] = x_ref[0] + g_ref[0] * r


def out_proj(y, x, gate, w_bf16, tm=512):
    b, L, d = x.shape
    per_batch = gate.shape[0] == b
    mod_map = (lambda bi, i: (bi, 0, 0)) if per_batch else (lambda bi, i: (0, 0, 0))
    tm = min(tm, L)
    return pl.pallas_call(
        _out_proj_kernel,
        out_shape=jax.ShapeDtypeStruct((b, L, d), F32),
        grid=(b, L // tm),
        in_specs=[pl.BlockSpec((1, tm, y.shape[-1]), lambda bi, i: (bi, i, 0)),
                  pl.BlockSpec((1, tm, d), lambda bi, i: (bi, i, 0)),
                  pl.BlockSpec((1, 1, d), mod_map),
                  pl.BlockSpec(w_bf16.shape, lambda bi, i: (0, 0))],
        out_specs=pl.BlockSpec((1, tm, d), lambda bi, i: (bi, i, 0)),
        compiler_params=_params("parallel", "arbitrary"),
        name="out_proj_residual",
    )(y, x, gate, w_bf16)


def _norm_router_kernel(x_ref, nw_ref, sh_ref, sc_ref, wr_ref, br_ref, h_ref, lg_ref):
    h = _modnorm(x_ref[0], nw_ref[...], sh_ref[0], sc_ref[0])
    h_ref[0] = h.astype(BF16)
    lg_ref[0] = jnp.dot(h, wr_ref[...], preferred_element_type=F32,
                        precision=lax.Precision.HIGHEST) + br_ref[...]


def norm_router(x, nw, shift, scale, w_router, b_router, tm=512):
    b, L, d = x.shape
    per_batch = scale.shape[0] == b
    mod_map = (lambda bi, i: (bi, 0, 0)) if per_batch else (lambda bi, i: (0, 0, 0))
    tm = min(tm, L)
    return pl.pallas_call(
        _norm_router_kernel,
        out_shape=(jax.ShapeDtypeStruct((b, L, d), BF16),
                   jax.ShapeDtypeStruct((b, L, ROUTER_COLS), F32)),
        grid=(b, L // tm),
        in_specs=[pl.BlockSpec((1, tm, d), lambda bi, i: (bi, i, 0)),
                  pl.BlockSpec((1, d), lambda bi, i: (0, 0)),
                  pl.BlockSpec((1, 1, d), mod_map),
                  pl.BlockSpec((1, 1, d), mod_map),
                  pl.BlockSpec((d, ROUTER_COLS), lambda bi, i: (0, 0)),
                  pl.BlockSpec((1, ROUTER_COLS), lambda bi, i: (0, 0))],
        out_specs=(pl.BlockSpec((1, tm, d), lambda bi, i: (bi, i, 0)),
                   pl.BlockSpec((1, tm, ROUTER_COLS), lambda bi, i: (bi, i, 0))),
        compiler_params=_params("parallel", "arbitrary"),
        name="moe_norm_router",
    )(x, nw.reshape(1, d), shift, scale, w_router, b_router)


def _expert_ffn_kernel(te_ref, tv_ref, x_ref, rw_ref, wg_ref, wu_ref, wd_ref, o_ref):
    i = pl.program_id(0)

    @pl.when(tv_ref[i] > 0)
    def _():
        x = x_ref[...]
        g = jnp.dot(x, wg_ref[0].astype(BF16), preferred_element_type=F32)
        u = jnp.dot(x, wu_ref[0].astype(BF16), preferred_element_type=F32)
        hid = (g * jax.nn.sigmoid(g)) * u * rw_ref[...]
        o_ref[...] = jnp.dot(hid.astype(BF16), wd_ref[0].astype(BF16), preferred_element_type=F32)

    @pl.when(tv_ref[i] == 0)
    def _():
        o_ref[...] = jnp.zeros_like(o_ref)


def expert_ffn(x_sorted, row_w, tile_expert, tile_valid, w_gate, w_up, w_down, tm):
    r, d = x_sorted.shape
    f = w_gate.shape[-1]
    grid_spec = pltpu.PrefetchScalarGridSpec(
        num_scalar_prefetch=2,
        grid=(r // tm,),
        in_specs=[pl.BlockSpec((tm, d), lambda i, te, tv: (i, 0)),
                  pl.BlockSpec((tm, 1), lambda i, te, tv: (i, 0)),
                  pl.BlockSpec((1, d, f), lambda i, te, tv: (te[i], 0, 0)),
                  pl.BlockSpec((1, d, f), lambda i, te, tv: (te[i], 0, 0)),
                  pl.BlockSpec((1, f, d), lambda i, te, tv: (te[i], 0, 0))],
        out_specs=pl.BlockSpec((tm, d), lambda i, te, tv: (i, 0)),
    )
    return pl.pallas_call(
        _expert_ffn_kernel,
        out_shape=jax.ShapeDtypeStruct((r, d), F32),
        grid_spec=grid_spec,
        compiler_params=_params("arbitrary"),
        name="moe_expert_ffn",
    )(tile_expert, tile_valid, x_sorted, row_w, w_gate, w_up, w_down)


def route(logits):
    grp_prob = jax.nn.softmax(logits[:, :N_GROUPS], axis=-1)
    grp_p, grp_idx = lax.top_k(grp_prob, 1)
    exp_logits = logits[:, N_GROUPS:N_GROUPS + N_EXPERTS].reshape(-1, N_GROUPS, EXPERTS_PER_GROUP)
    in_grp = jnp.take_along_axis(exp_logits, grp_idx[:, :, None], axis=1)[:, 0]
    top_p, top_idx = lax.top_k(jax.nn.softmax(in_grp, axis=-1), 2)
    top_w = grp_p * top_p / jnp.sum(top_p, axis=-1, keepdims=True)
    return grp_idx * EXPERTS_PER_GROUP + top_idx, top_w


def moe_apply(h_tokens, logits, w_gate, w_up, w_down, tm=256):
    t, d = h_tokens.shape
    e_idx, e_w = route(logits)
    flat_e = e_idx.reshape(-1).astype(jnp.int32)
    order = jnp.argsort(flat_e, stable=True).astype(jnp.int32)
    sorted_e = flat_e[order]
    counts = jnp.zeros((N_EXPERTS,), jnp.int32).at[flat_e].add(1)
    padded = (counts + tm - 1) // tm * tm
    seg_start = jnp.cumsum(counts) - counts
    pad_start = jnp.cumsum(padded) - padded
    dest = pad_start[sorted_e] + (jnp.arange(2 * t, dtype=jnp.int32) - seg_start[sorted_e])
    n_rows = 2 * t + N_EXPERTS * tm
    row_token = jnp.zeros((n_rows,), jnp.int32).at[dest].set(order // 2)
    row_w = jnp.zeros((n_rows,), F32).at[dest].set(e_w.reshape(-1)[order])
    pos = jnp.zeros((2 * t,), jnp.int32).at[order].set(dest).reshape(t, 2)
    tile_start = jnp.arange(n_rows // tm, dtype=jnp.int32) * tm
    pad_end = jnp.cumsum(padded)
    tile_expert = jnp.minimum(jnp.searchsorted(pad_end, tile_start, side="right"), N_EXPERTS - 1).astype(jnp.int32)
    tile_valid = (tile_start < pad_end[-1]).astype(jnp.int32)
    x_sorted = h_tokens[row_token]
    wg = w_gate.reshape(N_EXPERTS, d, EXPERT_HIDDEN)
    wu = w_up.reshape(N_EXPERTS, d, EXPERT_HIDDEN)
    wd = w_down.reshape(N_EXPERTS, EXPERT_HIDDEN, d)
    y_sorted = expert_ffn(x_sorted, row_w[:, None], tile_expert, tile_valid, wg, wu, wd, tm)
    return y_sorted[pos[:, 0]] + y_sorted[pos[:, 1]]


def _rmsnorm_kernel(x_ref, w_ref, o_ref):
    x = x_ref[...]
    o_ref[...] = x * lax.rsqrt(jnp.mean(x * x, axis=-1, keepdims=True) + EPS) * w_ref[...]


def final_norm(x, w, tm=1024):
    t, d = x.shape
    return pl.pallas_call(
        _rmsnorm_kernel,
        out_shape=jax.ShapeDtypeStruct((t, d), F32),
        grid=(t // tm,),
        in_specs=[pl.BlockSpec((tm, d), lambda i: (i, 0)), pl.BlockSpec((1, d), lambda i: (0, 0))],
        out_specs=pl.BlockSpec((tm, d), lambda i: (i, 0)),
        compiler_params=_params("parallel"),
        name="final_rmsnorm",
    )(x, w.reshape(1, d))


def _rmsnorm(x, w):
    return x * lax.rsqrt(jnp.mean(x * x, axis=-1, keepdims=True) + EPS) * w


def _short_conv(x, w, b):
    L = x.shape[1]
    pad = w.shape[0] // 2
    xp = jnp.pad(x, ((0, 0), (pad, pad), (0, 0)))
    return sum(xp[:, k:k + L] * w[k] for k in range(w.shape[0])) + b


def _flip(t):
    return t[:, ::-1]


def _to_col_major(u):
    b, L, ch = u.shape
    rows = L // GRID_W
    return u.reshape(b, rows, GRID_W, ch).transpose(0, 2, 1, 3).reshape(b, L, ch)


def _to_row_major(u):
    b, L, ch = u.shape
    rows = L // GRID_W
    return u.reshape(b, GRID_W, rows, ch).transpose(0, 2, 1, 3).reshape(b, L, ch)


def _ssd_scan(xh, dt, a, bh, ch, init, want_y):
    b, L, h, pd = xh.shape
    n = bh.shape[-1]
    q = SSD_CHUNK
    nc = L // q
    cum = jnp.cumsum((dt * a).reshape(b, nc, q, h).transpose(0, 1, 3, 2), axis=-1)
    xdt = (xh * dt[..., None]).reshape(b, nc, q, h, pd)
    bc = bh.reshape(b, nc, q, h, n)
    cc = ch.reshape(b, nc, q, h, n)
    to_end = jnp.exp(cum[..., -1:] - cum)
    chunk_states = jnp.einsum('bcqhn,bchq,bcqhp->bchpn', bc, to_end, xdt)
    chunk_decay = jnp.exp(cum[..., -1])

    def step(state, inp):
        s_c, d_c = inp
        return state * d_c[..., None, None] + s_c, state

    final, entering = lax.scan(step, init, (jnp.moveaxis(chunk_states, 1, 0), jnp.moveaxis(chunk_decay, 1, 0)))
    if not want_y:
        return None, final
    entering = jnp.moveaxis(entering, 0, 1)
    causal = jnp.tril(jnp.ones((q, q), dtype=bool))
    seg = jnp.exp(jnp.where(causal, cum[..., :, None] - cum[..., None, :], -jnp.inf))
    scores = jnp.einsum('bclhn,bcshn->bchls', cc, bc) * seg
    y = jnp.einsum('bchls,bcshp->bclhp', scores, xdt)
    y = y + jnp.einsum('bclhn,bchpn->bclhp', cc, entering) * jnp.exp(cum).transpose(0, 1, 3, 2)[..., None]
    return y.reshape(b, L, h, pd), final


def _ssd_mixer(z, xbc, dt_raw, p, init, want_y):
    b, L, _ = z.shape
    xbc = jax.nn.silu(_short_conv(xbc, p['ssd_conv_w'], p['ssd_conv_b']))
    xs = xbc[..., :SSD_WIDTH]
    bs = xbc[..., SSD_WIDTH:SSD_WIDTH + SSD_GROUPS * SSD_STATE]
    cs = xbc[..., SSD_WIDTH + SSD_GROUPS * SSD_STATE:]
    rep = SSD_HEADS // SSD_GROUPS
    xh = xs.reshape(b, L, SSD_HEADS, HEAD_DIM)
    bh = jnp.repeat(bs.reshape(b, L, SSD_GROUPS, SSD_STATE), rep, axis=2)
    chh = jnp.repeat(cs.reshape(b, L, SSD_GROUPS, SSD_STATE), rep, axis=2)
    dt = jax.nn.softplus(dt_raw.reshape(b, L, 2, SSD_HEADS) + p['ssd_dt_bias'])
    a = -jnp.exp(p['ssd_a_log'])
    y_f, s_f = _ssd_scan(xh, dt[:, :, 0], a[0], bh, chh, init[0], want_y)
    y_b, s_b = _ssd_scan(_flip(xh), _flip(dt[:, :, 1]), a[1], _flip(bh), _flip(chh), init[1], want_y)
    if not want_y:
        return None, (s_f, s_b)
    y = y_f + _flip(y_b) + xh * p['ssd_d'][:, None]
    y = y.reshape(b, L, SSD_WIDTH)
    return _rmsnorm(y * jax.nn.silu(z), p['ssd_norm_w']), (s_f, s_b)


def _mlstm_scan(q, k, v, i_pre, f_pre, init, want_y):
    b, L, h, d = q.shape
    qn = MLSTM_CHUNK
    nc = L // qn
    chunk = lambda t: t.reshape(b, nc, qn, h, d).transpose(0, 1, 3, 2, 4)
    gate = lambda t: t.reshape(b, nc, qn, h).transpose(0, 1, 3, 2)
    qc, kc, vc = chunk(q), chunk(k) * (d ** -0.5), chunk(v)
    ig = gate(i_pre)
    cum_f = jnp.cumsum(gate(jax.nn.log_sigmoid(f_pre)), axis=-1)
    tot_f = cum_f[..., -1]
    w_end = tot_f[..., None] - cum_f + ig
    m_loc = jnp.max(w_end, axis=-1)
    e_end = jnp.exp(w_end - m_loc[..., None])
    c_loc = jnp.einsum('bchq,bchqv,bchqk->bchvk', e_end, vc, kc)
    n_loc = jnp.einsum('bchq,bchqk->bchk', e_end, kc)

    def step(carry, inp):
        c_prev, n_prev, m_prev = carry
        c_l, n_l, m_l, g_l = inp
        m_new = jnp.maximum(g_l + m_prev, m_l)
        a_prev = jnp.exp(g_l + m_prev - m_new)
        a_loc = jnp.exp(m_l - m_new)
        new = (a_prev[..., None, None] * c_prev + a_loc[..., None, None] * c_l,
               a_prev[..., None] * n_prev + a_loc[..., None] * n_l,
               m_new)
        return new, (c_prev, n_prev, m_prev)

    mv = lambda t: jnp.moveaxis(t, 1, 0)
    final, (c_in, n_in, m_in) = lax.scan(step, init, (mv(c_loc), mv(n_loc), mv(m_loc), mv(tot_f)))
    if not want_y:
        return None, final
    c_in, n_in, m_in = mv(c_in), mv(n_in), mv(m_in)
    causal = jnp.tril(jnp.ones((qn, qn), dtype=bool))
    log_w = jnp.where(causal, cum_f[..., :, None] - cum_f[..., None, :] + ig[..., None, :], -jnp.inf)
    inter = cum_f + m_in[..., None]
    m_t = jnp.maximum(inter, jnp.max(log_w, axis=-1))
    s = jnp.einsum('bchld,bchsd->bchls', qc, kc) * jnp.exp(log_w - m_t[..., None])
    a_inter = jnp.exp(inter - m_t)
    num = jnp.einsum('bchls,bchsv->bchlv', s, vc) + a_inter[..., None] * jnp.einsum('bchld,bchvd->bchlv', qc, c_in)
    den = jnp.sum(s, axis=-1) + a_inter * jnp.einsum('bchld,bchd->bchl', qc, n_in)
    hid = num / jnp.maximum(jnp.abs(den), jnp.exp(-m_t))[..., None]
    return hid.transpose(0, 1, 3, 2, 4).reshape(b, L, h, d), final


def _head_layernorm(h, w):
    mu = jnp.mean(h, axis=-1, keepdims=True)
    hc = h - mu
    return hc * lax.rsqrt(jnp.mean(hc * hc, axis=-1, keepdims=True) + EPS) * w


def _mlstm_mixer(qk, v, o, g, p, init, want_y):
    b, L, _ = v.shape
    qk = jax.nn.silu(_short_conv(qk, p['ml_conv_w'], p['ml_conv_b']))
    q, k = jnp.split(qk, 2, axis=-1)
    heads = lambda t: t.reshape(b, L, MLSTM_HEADS, HEAD_DIM)
    q, k, vh = heads(q), heads(k), heads(v)
    gates = g.reshape(b, L, 2, 2, MLSTM_HEADS) + p['ml_gate_b']
    h_f, s_f = _mlstm_scan(q, k, vh, gates[:, :, 0, 0], gates[:, :, 0, 1], init[0], want_y)
    h_b, s_b = _mlstm_scan(_flip(q), _flip(k), _flip(vh), _flip(gates[:, :, 1, 0]), _flip(gates[:, :, 1, 1]),
                           init[1], want_y)
    if not want_y:
        return None, (s_f, s_b)
    h = _head_layernorm(h_f + _flip(h_b), p['ml_norm_w'].reshape(MLSTM_HEADS, HEAD_DIM))
    out = jax.nn.sigmoid(o) * h.reshape(b, L, MLSTM_WIDTH)
    return out, (s_f, s_b)


def _hyena_filters(L, p):
    t = jnp.arange(L, dtype=F32)
    t_unit = t / float(max(L - 1, 1))
    bands = jnp.linspace(1e-4, HYENA_BANDS - 1, HYENA_BANDS, dtype=F32)
    ang = (2 * math.pi / L) * t[:, None] * bands[None, :]
    feats = jnp.concatenate([t_unit[:, None], jnp.cos(ang), -jnp.sin(ang)], axis=-1)
    freq = p['hy_freq']
    hp = lax.Precision.HIGHEST
    hdn = jnp.sin(freq * (jnp.dot(feats, p['hy_pos_w1'], precision=hp) + p['hy_pos_b1']))
    hdn = jnp.sin(freq * (jnp.dot(hdn, p['hy_pos_w2'], precision=hp) + p['hy_pos_b2']))
    filt = jnp.dot(hdn, p['hy_pos_w3'], precision=hp).reshape(L, 2, HYENA_ORDER, HYENA_WIDTH)
    window = jnp.exp(-t_unit[:, None, None, None] * jnp.abs(p['hy_decay']))
    return filt * window


def _bidir_long_conv(z, h_fwd, h_bwd):
    L, ch = h_fwd.shape
    two_sided = jnp.concatenate([h_fwd, jnp.zeros((1, ch), h_fwd.dtype), h_bwd[1:][::-1]], axis=0)
    two_sided = two_sided / jnp.sum(jnp.abs(two_sided), axis=0, keepdims=True)
    zf = jnp.fft.rfft(z, n=2 * L, axis=1)
    hf = jnp.fft.rfft(two_sided, n=2 * L, axis=0)
    return jnp.fft.irfft(zf * hf[None], n=2 * L, axis=1)[:, :L]


def _hyena_mixer(u, p):
    uc = _short_conv(u, p['hy_conv_w'], p['hy_conv_b'])
    v, *gates = jnp.split(uc, HYENA_ORDER + 1, axis=-1)
    filt = _hyena_filters(u.shape[1], p)
    z = v
    for order, gate in enumerate(gates):
        z = gate * (_bidir_long_conv(z, filt[:, 0, order], filt[:, 1, order]) + p['hy_skip'][order] * z)
    return z


def _split_cols(u):
    outs, s = [], 0
    for n, npad in zip(IN_SIZES, IN_PAD_SIZES):
        outs.append(u[..., s:s + n])
        s += npad
    return outs


def _token_mixers(u, p, ssd_init, ml_init, want_out):
    z, xbc, dt_raw, hy_u, ml_qk, ml_v, ml_o, ml_g = _split_cols(u)
    y_ssd, ssd_fin = _ssd_mixer(z, xbc, dt_raw, p, ssd_init, want_out)
    y_ml, ml_fin = _mlstm_mixer(ml_qk, ml_v, ml_o, ml_g, p, ml_init, want_out)
    if not want_out:
        return None, ssd_fin, ml_fin
    y_hy = _hyena_mixer(hy_u, p)
    return jnp.concatenate([y_ssd, y_hy, y_ml], axis=-1), ssd_fin, ml_fin


def _pad_in_weight(w_in):
    parts, s = [], 0
    for n, npad in zip(IN_SIZES, IN_PAD_SIZES):
        parts.append(jnp.pad(w_in[:, s:s + n], ((0, 0), (0, npad - n))))
        s += n
    return jnp.concatenate(parts, axis=1).astype(BF16)


def kernel(x, c, ctx, c_ctx, w_mod, b_mod, norm1_w, norm2_w, w_in, w_out, ssd_conv_w, ssd_conv_b, ssd_dt_bias, ssd_a_log, ssd_d, ssd_norm_w, hy_conv_w, hy_conv_b, hy_pos_w1, hy_pos_b1, hy_pos_w2, hy_pos_b2, hy_pos_w3, hy_freq, hy_decay, hy_skip, ml_conv_w, ml_conv_b, ml_gate_b, ml_norm_w, grp_router_w, grp_router_b, exp_router_w, exp_router_b, moe_w_gate, moe_w_up, moe_w_down, final_norm_w):
    layer_params = dict(
        ssd_conv_w=ssd_conv_w, ssd_conv_b=ssd_conv_b, ssd_dt_bias=ssd_dt_bias, ssd_a_log=ssd_a_log,
        ssd_d=ssd_d, ssd_norm_w=ssd_norm_w, hy_conv_w=hy_conv_w, hy_conv_b=hy_conv_b,
        hy_pos_w1=hy_pos_w1, hy_pos_b1=hy_pos_b1, hy_pos_w2=hy_pos_w2, hy_pos_b2=hy_pos_b2,
        hy_pos_w3=hy_pos_w3, hy_freq=hy_freq, hy_decay=hy_decay, hy_skip=hy_skip,
        ml_conv_w=ml_conv_w, ml_conv_b=ml_conv_b, ml_gate_b=ml_gate_b, ml_norm_w=ml_norm_w)
    bsz, seq, d = x.shape
    n_ctx = ctx.shape[1]
    xl, xc = x, ctx
    ssd0 = jnp.zeros((bsz, SSD_HEADS, HEAD_DIM, SSD_STATE), F32)
    ml0 = (jnp.zeros((bsz, MLSTM_HEADS, HEAD_DIM, HEAD_DIM), F32),
           jnp.zeros((bsz, MLSTM_HEADS, HEAD_DIM), F32),
           jnp.zeros((bsz, MLSTM_HEADS), F32))
    c_rows = jnp.concatenate([c, c_ctx[None, :], jnp.zeros((8 - bsz - 1, d), F32)], axis=0)
    for i in range(DEPTH):
        last = i == DEPTH - 1
        p = {name: arr[i] for name, arr in layer_params.items()}
        mod = modulation(c_rows, w_mod[i], b_mod[i]).reshape(8, N_MOD, 1, d)
        mod_l = [mod[:bsz, k] for k in range(N_MOD)]
        mod_c = [mod[bsz:bsz + 1, k] for k in range(N_MOD)]
        w_in_p = _pad_in_weight(w_in[i])
        w_out_b = w_out[i].astype(BF16)
        w_router = jnp.pad(jnp.concatenate([grp_router_w[i], exp_router_w[i]], axis=1),
                           ((0, 0), (0, ROUTER_COLS - N_GROUPS - N_EXPERTS)))
        b_router = jnp.pad(jnp.concatenate([grp_router_b[i], exp_router_b[i]]),
                           (0, ROUTER_COLS - N_GROUPS - N_EXPERTS)).reshape(1, ROUTER_COLS)

        uc = norm_proj(xc, norm1_w[i], mod_c[0], mod_c[1], w_in_p)
        yc, ssd_ctx, ml_ctx = _token_mixers(uc, p, (ssd0, ssd0), (ml0, ml0), not last)
        col_major = i % 2 == 1
        xin = _to_col_major(xl) if col_major else xl
        ul = norm_proj(xin, norm1_w[i], mod_l[0], mod_l[1], w_in_p)
        yl, _, _ = _token_mixers(ul, p, ssd_ctx, ml_ctx, True)
        if col_major:
            yl = _to_row_major(yl)
        xl = out_proj(yl, xl, mod_l[2], w_out_b)
        hl, lgl = norm_router(xl, norm2_w[i], mod_l[3], mod_l[4], w_router, b_router)
        if not last:
            xc = out_proj(yc, xc, mod_c[2], w_out_b)
            hc, lgc = norm_router(xc, norm2_w[i], mod_c[3], mod_c[4], w_router, b_router)
            h_all = jnp.concatenate([hl.reshape(-1, d), hc.reshape(-1, d)], axis=0)
            lg_all = jnp.concatenate([lgl.reshape(-1, ROUTER_COLS), lgc.reshape(-1, ROUTER_COLS)], axis=0)
        else:
            h_all, lg_all = hl.reshape(-1, d), lgl.reshape(-1, ROUTER_COLS)
        moe = moe_apply(h_all, lg_all, moe_w_gate[i], moe_w_up[i], moe_w_down[i])
        xl = xl + mod_l[5] * moe[:bsz * seq].reshape(bsz, seq, d)
        if not last:
            xc = xc + mod_c[5] * moe[bsz * seq:].reshape(bsz, n_ctx, d)
    return final_norm(xl.reshape(-1, d), final_norm_w).reshape(bsz, seq, d)
```

```python
import functools
import math

import jax
import jax.numpy as jnp
from jax import lax
from jax.experimental import pallas as pl
from jax.experimental.pallas import tpu as pltpu

D_MODEL = 1024
DEPTH = 2
GRID_W = 64
HEAD_DIM = 64
SSD_WIDTH = 384
SSD_HEADS = SSD_WIDTH // HEAD_DIM
SSD_GROUPS = 2
SSD_STATE = 64
HYENA_WIDTH = 256
HYENA_ORDER = 2
HYENA_BANDS = 16
ML_WIDTH = 384
ML_HEADS = ML_WIDTH // HEAD_DIM
N_GROUPS = 4
EXPERTS_PER_GROUP = 8
N_EXPERTS = N_GROUPS * EXPERTS_PER_GROUP
EXPERT_HIDDEN = 256
N_MOD = 6
EPS = 1e-6

LANE = 128
SUBLANE = 8
VMEM_LIMIT = 48 * 1024 * 1024

SSD_CONV_CH = SSD_WIDTH + 2 * SSD_GROUPS * SSD_STATE
SSD_XBC0 = SSD_WIDTH
SSD_DT0 = SSD_XBC0 + SSD_CONV_CH
SSD_COLS = SSD_DT0 + LANE
HY_COLS = (HYENA_ORDER + 1) * HYENA_WIDTH
ML_V0 = 2 * ML_WIDTH
ML_O0 = ML_V0 + ML_WIDTH
ML_G0 = ML_O0 + ML_WIDTH
ML_COLS = ML_G0 + LANE
ROUTER_COLS = LANE

F32 = jnp.float32
BF16 = jnp.bfloat16
HI = lax.Precision.HIGHEST
NT = (((1,), (1,)), ((), ()))
TN = (((0,), (0,)), ((), ()))


def _params(*sem):
    return pltpu.CompilerParams(dimension_semantics=sem, vmem_limit_bytes=VMEM_LIMIT)


def _silu(x):
    return x * jax.nn.sigmoid(x)


def _softplus(x):
    return jnp.maximum(x, 0.0) + jnp.log(1.0 + jnp.exp(-jnp.abs(x)))


def _log_sigmoid(x):
    return jnp.minimum(x, 0.0) - jnp.log(1.0 + jnp.exp(-jnp.abs(x)))


def _mod_kernel(c_ref, w_ref, b_ref, o_ref):
    o_ref[...] = jnp.dot(_silu(c_ref[...]), w_ref[...], preferred_element_type=F32, precision=HI) + b_ref[...]


def modulation(c_rows, w_mod, b_mod):
    n = w_mod.shape[1]
    tn = 1536
    return pl.pallas_call(
        _mod_kernel,
        out_shape=jax.ShapeDtypeStruct((c_rows.shape[0], n), F32),
        grid=(n // tn,),
        in_specs=[pl.BlockSpec(c_rows.shape, lambda j: (0, 0)),
                  pl.BlockSpec((w_mod.shape[0], tn), lambda j: (0, j)),
                  pl.BlockSpec((1, tn), lambda j: (0, j))],
        out_specs=pl.BlockSpec((c_rows.shape[0], tn), lambda j: (0, j)),
        compiler_params=_params("arbitrary"),
        name="adaln_modulation",
    )(c_rows, w_mod, b_mod.reshape(1, n))


def _modnorm(x, nw, shift, scale):
    y = x * lax.rsqrt(jnp.mean(x * x, axis=-1, keepdims=True) + EPS) * nw
    return y * (1.0 + scale) + shift


def _mod_map(mod, b):
    return (lambda bi, i: (bi, 0, 0)) if mod.shape[0] == b else (lambda bi, i: (0, 0, 0))


def _norm_proj_kernel(x_ref, nw_ref, sh_ref, sc_ref, w_ref, ssd_ref, hy_ref, ml_ref):
    h = _modnorm(x_ref[0], nw_ref[...], sh_ref[0], sc_ref[0])
    u = jnp.dot(h.astype(BF16), w_ref[...], preferred_element_type=F32)
    ssd_ref[0] = u[:, 0:SSD_COLS]
    hy_ref[0] = u[:, SSD_COLS:SSD_COLS + HY_COLS]
    ml_ref[0] = u[:, SSD_COLS + HY_COLS:]


def norm_proj(x, nw, shift, scale, w_bf16, tm=256):
    b, L, d = x.shape
    n = w_bf16.shape[1]
    row = lambda bi, i: (bi, i, 0)
    return pl.pallas_call(
        _norm_proj_kernel,
        out_shape=(jax.ShapeDtypeStruct((b, L, SSD_COLS), F32),
                   jax.ShapeDtypeStruct((b, L, HY_COLS), F32),
                   jax.ShapeDtypeStruct((b, L, ML_COLS), F32)),
        grid=(b, L // tm),
        in_specs=[pl.BlockSpec((1, tm, d), row),
                  pl.BlockSpec((1, d), lambda bi, i: (0, 0)),
                  pl.BlockSpec((1, 1, d), _mod_map(shift, b)),
                  pl.BlockSpec((1, 1, d), _mod_map(scale, b)),
                  pl.BlockSpec((d, n), lambda bi, i: (0, 0))],
        out_specs=(pl.BlockSpec((1, tm, SSD_COLS), row), pl.BlockSpec((1, tm, HY_COLS), row),
                   pl.BlockSpec((1, tm, ML_COLS), row)),
        compiler_params=_params("parallel", "arbitrary"),
        name="norm_in_proj",
    )(x, nw.reshape(1, d), shift, scale, w_bf16)


def _out_proj_kernel(ys_ref, yh_ref, ym_ref, x_ref, g_ref, w_ref, o_ref):
    y = jnp.concatenate([ys_ref[0], yh_ref[0], ym_ref[0]], axis=-1).astype(BF16)
    o_ref[0] = x_ref[0] + g_ref[0] * jnp.dot(y, w_ref[...], preferred_element_type=F32)


def out_proj(y_ssd, y_hy, y_ml, x, gate, w_bf16, tm=512):
    b, L, d = x.shape
    tm = min(tm, L)
    row = lambda bi, i: (bi, i, 0)
    return pl.pallas_call(
        _out_proj_kernel,
        out_shape=jax.ShapeDtypeStruct((b, L, d), F32),
        grid=(b, L // tm),
        in_specs=[pl.BlockSpec((1, tm, SSD_WIDTH), row), pl.BlockSpec((1, tm, HYENA_WIDTH), row),
                  pl.BlockSpec((1, tm, ML_WIDTH), row), pl.BlockSpec((1, tm, d), row),
                  pl.BlockSpec((1, 1, d), _mod_map(gate, b)),
                  pl.BlockSpec(w_bf16.shape, lambda bi, i: (0, 0))],
        out_specs=pl.BlockSpec((1, tm, d), row),
        compiler_params=_params("parallel", "arbitrary"),
        name="out_proj_residual",
    )(y_ssd, y_hy, y_ml, x, gate, w_bf16)


def _conv3(xr, prev_row, next_row, cw, cb, q):
    rid = lax.broadcasted_iota(jnp.int32, (q, 1), 0)
    x_prev = jnp.where(rid == 0, prev_row, pltpu.roll(xr, 1, axis=0))
    x_next = jnp.where(rid == q - 1, next_row, pltpu.roll(xr, q - 1, axis=0))
    return x_prev * cw[0:1] + xr * cw[1:2] + x_next * cw[2:3] + cb


def _scan_mask(q, direction):
    li = lax.broadcasted_iota(jnp.int32, (q, q), 0)
    si = lax.broadcasted_iota(jnp.int32, (q, q), 1)
    return (si <= li) if direction == 0 else (si >= li)


def _scan_specs(L, q, nc, cols, direction):
    hb = q // SUBLANE
    nrb = L // SUBLANE
    cidx = (lambda j: j) if direction == 0 else (lambda j: nc - 1 - j)
    specs = [pl.BlockSpec((1, q, cols), lambda bi, j: (bi, cidx(j), 0)),
             pl.BlockSpec((1, SUBLANE, cols), lambda bi, j: (bi, jnp.maximum(cidx(j) * hb - 1, 0), 0)),
             pl.BlockSpec((1, SUBLANE, cols), lambda bi, j: (bi, jnp.minimum((cidx(j) + 1) * hb, nrb - 1), 0))]
    return specs, cidx


def _ssd_kernel(*refs, direction, finalize, q, nc):
    if finalize:
        (u_ref, prev_ref, next_ref, yb_ref, init_ref, cw_ref, cb_ref, dtb_ref, a_ref, d_ref, nw_ref,
         y_ref, fin_ref, state_ref) = refs
    else:
        (u_ref, prev_ref, next_ref, init_ref, cw_ref, cb_ref, dtb_ref, a_ref,
         y_ref, fin_ref, state_ref) = refs
    j = pl.program_id(1)
    c = j if direction == 0 else nc - 1 - j

    @pl.when(j == 0)
    def _():
        state_ref[...] = init_ref[0]

    prev_row = jnp.where(c > 0, prev_ref[0, SUBLANE - 1:SUBLANE, SSD_XBC0:SSD_DT0], 0.0)
    next_row = jnp.where(c < nc - 1, next_ref[0, 0:1, SSD_XBC0:SSD_DT0], 0.0)
    xc = _silu(_conv3(u_ref[0, :, SSD_XBC0:SSD_DT0], prev_row, next_row, cw_ref[...], cb_ref[...], q))

    dt = _softplus(u_ref[0, :, SSD_DT0:SSD_COLS] + dtb_ref[...])
    mask = _scan_mask(q, direction)
    cum = jnp.dot(mask.astype(F32), dt * a_ref[...], preferred_element_type=F32, precision=HI)
    cum_t = cum.T
    end = q - 1 if direction == 0 else 0

    ys = []
    for g in range(SSD_GROUPS):
        b0 = SSD_WIDTH + g * SSD_STATE
        c0 = SSD_WIDTH + (SSD_GROUPS + g) * SSD_STATE
        bm = xc[:, b0:b0 + SSD_STATE].astype(BF16)
        cm = xc[:, c0:c0 + SSD_STATE].astype(BF16)
        scores = lax.dot_general(cm, bm, NT, preferred_element_type=F32)
        for h in range(g * (SSD_HEADS // SSD_GROUPS), (g + 1) * (SSD_HEADS // SSD_GROUPS)):
            hl = direction * SSD_HEADS + h
            col = cum[:, hl:hl + 1]
            seg = jnp.exp(jnp.where(mask, col - cum_t[hl:hl + 1, :], -jnp.inf))
            xdt = xc[:, h * HEAD_DIM:(h + 1) * HEAD_DIM] * dt[:, hl:hl + 1]
            y = jnp.dot((scores * seg).astype(BF16), xdt.astype(BF16), preferred_element_type=F32)
            st = state_ref[h]
            y = y + lax.dot_general(cm, st.astype(BF16), NT, preferred_element_type=F32) * jnp.exp(col)
            tot = cum[end:end + 1, hl:hl + 1]
            upd = lax.dot_general((xdt * jnp.exp(tot - col)).astype(BF16), bm, TN, preferred_element_type=F32)
            state_ref[h] = st * jnp.exp(tot) + upd
            ys.append(y)
    y_all = jnp.concatenate(ys, axis=-1)
    if finalize:
        t = (y_all + yb_ref[0] + xc[:, 0:SSD_WIDTH] * d_ref[...]) * _silu(u_ref[0, :, 0:SSD_WIDTH])
        y_all = t * lax.rsqrt(jnp.mean(t * t, axis=-1, keepdims=True) + EPS) * nw_ref[...]
    y_ref[0] = y_all

    @pl.when(j == nc - 1)
    def _():
        fin_ref[0] = state_ref[...]


def ssd_pass(u, y_other, init, sp, direction, q):
    b, L, _ = u.shape
    q = min(q, L)
    nc = L // q
    finalize = y_other is not None
    in_specs, cidx = _scan_specs(L, q, nc, SSD_COLS, direction)
    const2 = lambda bi, j: (0, 0)
    st_spec = pl.BlockSpec((1, SSD_HEADS, HEAD_DIM, SSD_STATE), lambda bi, j: (bi, 0, 0, 0))
    y_spec = pl.BlockSpec((1, q, SSD_WIDTH), lambda bi, j: (bi, cidx(j), 0))
    args = [u, u, u]
    if finalize:
        in_specs.append(y_spec)
        args.append(y_other)
    consts = [sp['cw'], sp['cb'], sp['dtb'], sp['a']] + ([sp['d'], sp['nw']] if finalize else [])
    in_specs += [st_spec] + [pl.BlockSpec(t.shape, const2) for t in consts]
    args += [init] + consts
    return pl.pallas_call(
        functools.partial(_ssd_kernel, direction=direction, finalize=finalize, q=q, nc=nc),
        out_shape=(jax.ShapeDtypeStruct((b, L, SSD_WIDTH), F32),
                   jax.ShapeDtypeStruct((b, SSD_HEADS, HEAD_DIM, SSD_STATE), F32)),
        grid=(b, nc),
        in_specs=in_specs,
        out_specs=(y_spec, st_spec),
        scratch_shapes=[pltpu.VMEM((SSD_HEADS, HEAD_DIM, SSD_STATE), F32)],
        compiler_params=_params("parallel", "arbitrary"),
        name="ssd_scan_%s" % ("fwd" if direction == 0 else "bwd"),
    )(*args)


def ssd_prepare(p):
    pad = lambda v: jnp.pad(v.reshape(1, -1), ((0, 0), (0, LANE - v.size)))
    return dict(cw=p['ssd_conv_w'], cb=p['ssd_conv_b'].reshape(1, -1),
                dtb=pad(p['ssd_dt_bias']), a=pad(-jnp.exp(p['ssd_a_log'])),
                d=jnp.repeat(p['ssd_d'], HEAD_DIM).reshape(1, -1), nw=p['ssd_norm_w'].reshape(1, -1))


def ssd_mixer(u, sp, init_f, init_b, want_y, q=256):
    yb, fin_b = ssd_pass(u, None, init_b, sp, 1, q)
    y, fin_f = ssd_pass(u, yb if want_y else None, init_f, sp, 0, q)
    return y, fin_f, fin_b


def _ml_kernel(*refs, direction, finalize, q, nc):
    if finalize:
        (u_ref, prev_ref, next_ref, hb_ref, s_init_ref, m_init_ref, cw_ref, cb_ref, gb_ref, nw_ref,
         y_ref, s_fin_ref, m_fin_ref, s_ref, m_ref) = refs
    else:
        (u_ref, prev_ref, next_ref, s_init_ref, m_init_ref, cw_ref, cb_ref, gb_ref,
         y_ref, s_fin_ref, m_fin_ref, s_ref, m_ref) = refs
    j = pl.program_id(1)
    c = j if direction == 0 else nc - 1 - j

    @pl.when(j == 0)
    def _():
        s_ref[...] = s_init_ref[0]
        m_ref[...] = m_init_ref[0]

    prev_row = jnp.where(c > 0, prev_ref[0, SUBLANE - 1:SUBLANE, 0:ML_V0], 0.0)
    next_row = jnp.where(c < nc - 1, next_ref[0, 0:1, 0:ML_V0], 0.0)
    qk = _silu(_conv3(u_ref[0, :, 0:ML_V0], prev_row, next_row, cw_ref[...], cb_ref[...], q))
    v = u_ref[0, :, ML_V0:ML_O0]

    gb = u_ref[0, :, ML_G0:ML_COLS] + gb_ref[...]
    mask = _scan_mask(q, direction)
    cum = jnp.dot(mask.astype(F32), _log_sigmoid(gb), preferred_element_type=F32, precision=HI)
    ig = pltpu.roll(gb, ML_HEADS, axis=1)
    end = q - 1 if direction == 0 else 0
    m_prev = m_ref[0:1, :]
    tot = cum[end:end + 1, :]
    w_end = tot - cum + ig
    m_loc = jnp.max(w_end, axis=0, keepdims=True)
    e_end = jnp.exp(w_end - m_loc)
    m_new = jnp.maximum(tot + m_prev, m_loc)
    a_prev = jnp.exp(tot + m_prev - m_new)
    a_loc = jnp.exp(m_loc - m_new)
    inter = cum + m_prev
    cum_t = cum.T
    ig_t = ig.T
    one_col = (lax.broadcasted_iota(jnp.int32, (q, HEAD_DIM), 1) == 0).astype(F32)

    ys = []
    for h in range(ML_HEADS):
        fl = direction * 2 * ML_HEADS + ML_HEADS + h
        qh = qk[:, h * HEAD_DIM:(h + 1) * HEAD_DIM].astype(BF16)
        kh = qk[:, ML_WIDTH + h * HEAD_DIM:ML_WIDTH + (h + 1) * HEAD_DIM] * (HEAD_DIM ** -0.5)
        v_ext = jnp.concatenate([v[:, h * HEAD_DIM:(h + 1) * HEAD_DIM], one_col], axis=-1).astype(BF16)
        log_w = jnp.where(mask, cum[:, fl:fl + 1] - cum_t[fl:fl + 1, :] + ig_t[fl:fl + 1, :], -jnp.inf)
        inter_c = inter[:, fl:fl + 1]
        m_t = jnp.maximum(inter_c, jnp.max(log_w, axis=-1, keepdims=True))
        scores = lax.dot_general(qh, kh.astype(BF16), NT, preferred_element_type=F32)
        sm = scores * jnp.exp(log_w - m_t)
        nd = jnp.dot(sm.astype(BF16), v_ext, preferred_element_type=F32)
        st = s_ref[h]
        nd = nd + jnp.exp(inter_c - m_t) * jnp.dot(qh, st.astype(BF16), preferred_element_type=F32)
        den = nd[:, HEAD_DIM:HEAD_DIM + 1]
        hid = nd[:, 0:HEAD_DIM] / jnp.maximum(jnp.abs(den), jnp.exp(-m_t))
        upd = lax.dot_general((kh * e_end[:, fl:fl + 1]).astype(BF16), v_ext, TN, preferred_element_type=F32)
        s_ref[h] = a_prev[:, fl:fl + 1] * st + a_loc[:, fl:fl + 1] * upd
        if finalize:
            hs = hid + hb_ref[0, :, h * HEAD_DIM:(h + 1) * HEAD_DIM]
            hc = hs - jnp.mean(hs, axis=-1, keepdims=True)
            hid = hc * lax.rsqrt(jnp.mean(hc * hc, axis=-1, keepdims=True) + EPS)
        ys.append(hid)
    m_ref[...] = jnp.broadcast_to(m_new, m_ref.shape)
    y_all = jnp.concatenate(ys, axis=-1)
    if finalize:
        y_all = y_all * nw_ref[...] * jax.nn.sigmoid(u_ref[0, :, ML_O0:ML_G0])
    y_ref[0] = y_all

    @pl.when(j == nc - 1)
    def _():
        s_fin_ref[0] = s_ref[...]
        m_fin_ref[0] = m_ref[...]


def ml_pass(u, h_other, init, mp, direction, q):
    b, L, _ = u.shape
    q = min(q, L)
    nc = L // q
    finalize = h_other is not None
    in_specs, cidx = _scan_specs(L, q, nc, ML_COLS, direction)
    const2 = lambda bi, j: (0, 0)
    s_spec = pl.BlockSpec((1, ML_HEADS, HEAD_DIM, LANE), lambda bi, j: (bi, 0, 0, 0))
    m_spec = pl.BlockSpec((1, SUBLANE, LANE), lambda bi, j: (bi, 0, 0))
    y_spec = pl.BlockSpec((1, q, ML_WIDTH), lambda bi, j: (bi, cidx(j), 0))
    args = [u, u, u]
    if finalize:
        in_specs.append(y_spec)
        args.append(h_other)
    consts = [mp['cw'], mp['cb'], mp['gb']] + ([mp['nw']] if finalize else [])
    in_specs += [s_spec, m_spec] + [pl.BlockSpec(t.shape, const2) for t in consts]
    args += [init[0], init[1]] + consts
    y, s_fin, m_fin = pl.pallas_call(
        functools.partial(_ml_kernel, direction=direction, finalize=finalize, q=q, nc=nc),
        out_shape=(jax.ShapeDtypeStruct((b, L, ML_WIDTH), F32),
                   jax.ShapeDtypeStruct((b, ML_HEADS, HEAD_DIM, LANE), F32),
                   jax.ShapeDtypeStruct((b, SUBLANE, LANE), F32)),
        grid=(b, nc),
        in_specs=in_specs,
        out_specs=(y_spec, s_spec, m_spec),
        scratch_shapes=[pltpu.VMEM((ML_HEADS, HEAD_DIM, LANE), F32), pltpu.VMEM((SUBLANE, LANE), F32)],
        compiler_params=_params("parallel", "arbitrary"),
        name="mlstm_scan_%s" % ("fwd" if direction == 0 else "bwd"),
    )(*args)
    return y, (s_fin, m_fin)


def ml_prepare(p):
    gb = p['ml_gate_b'].reshape(1, -1)
    return dict(cw=p['ml_conv_w'], cb=p['ml_conv_b'].reshape(1, -1),
                gb=jnp.pad(gb, ((0, 0), (0, LANE - gb.shape[1]))), nw=p['ml_norm_w'].reshape(1, -1))


def ml_mixer(u, mp, init_f, init_b, want_y, q=256):
    hb, fin_b = ml_pass(u, None, init_b, mp, 1, q)
    y, fin_f = ml_pass(u, hb if want_y else None, init_f, mp, 0, q)
    return y, fin_f, fin_b


def _norm_router_kernel(x_ref, nw_ref, sh_ref, sc_ref, wr_ref, br_ref, h_ref, lg_ref):
    h = _modnorm(x_ref[0], nw_ref[...], sh_ref[0], sc_ref[0])
    h_ref[0] = h.astype(BF16)
    lg_ref[0] = jnp.dot(h, wr_ref[...], preferred_element_type=F32, precision=HI) + br_ref[...]


def norm_router(x, nw, shift, scale, w_router, b_router, tm=512):
    b, L, d = x.shape
    tm = min(tm, L)
    row = lambda bi, i: (bi, i, 0)
    return pl.pallas_call(
        _norm_router_kernel,
        out_shape=(jax.ShapeDtypeStruct((b, L, d), BF16),
                   jax.ShapeDtypeStruct((b, L, ROUTER_COLS), F32)),
        grid=(b, L // tm),
        in_specs=[pl.BlockSpec((1, tm, d), row),
                  pl.BlockSpec((1, d), lambda bi, i: (0, 0)),
                  pl.BlockSpec((1, 1, d), _mod_map(shift, b)),
                  pl.BlockSpec((1, 1, d), _mod_map(scale, b)),
                  pl.BlockSpec((d, ROUTER_COLS), lambda bi, i: (0, 0)),
                  pl.BlockSpec((1, ROUTER_COLS), lambda bi, i: (0, 0))],
        out_specs=(pl.BlockSpec((1, tm, d), row), pl.BlockSpec((1, tm, ROUTER_COLS), row)),
        compiler_params=_params("parallel", "arbitrary"),
        name="moe_norm_router",
    )(x, nw.reshape(1, d), shift, scale, w_router, b_router)


def _expert_ffn_kernel(te_ref, tv_ref, x_ref, rw_ref, wg_ref, wu_ref, wd_ref, o_ref):
    i = pl.program_id(0)

    @pl.when(tv_ref[i] > 0)
    def _():
        x = x_ref[...]
        g = jnp.dot(x, wg_ref[0].astype(BF16), preferred_element_type=F32)
        u = jnp.dot(x, wu_ref[0].astype(BF16), preferred_element_type=F32)
        hid = _silu(g) * u * rw_ref[...]
        o_ref[...] = jnp.dot(hid.astype(BF16), wd_ref[0].astype(BF16), preferred_element_type=F32)

    @pl.when(tv_ref[i] == 0)
    def _():
        o_ref[...] = jnp.zeros_like(o_ref)


def expert_ffn(x_sorted, row_w, tile_expert, tile_valid, w_gate, w_up, w_down, tm):
    r, d = x_sorted.shape
    f = w_gate.shape[-1]
    grid_spec = pltpu.PrefetchScalarGridSpec(
        num_scalar_prefetch=2,
        grid=(r // tm,),
        in_specs=[pl.BlockSpec((tm, d), lambda i, te, tv: (i, 0)),
                  pl.BlockSpec((tm, 1), lambda i, te, tv: (i, 0)),
                  pl.BlockSpec((1, d, f), lambda i, te, tv: (te[i], 0, 0)),
                  pl.BlockSpec((1, d, f), lambda i, te, tv: (te[i], 0, 0)),
                  pl.BlockSpec((1, f, d), lambda i, te, tv: (te[i], 0, 0))],
        out_specs=pl.BlockSpec((tm, d), lambda i, te, tv: (i, 0)),
    )
    return pl.pallas_call(
        _expert_ffn_kernel,
        out_shape=jax.ShapeDtypeStruct((r, d), F32),
        grid_spec=grid_spec,
        compiler_params=_params("arbitrary"),
        name="moe_expert_ffn",
    )(tile_expert, tile_valid, x_sorted, row_w, w_gate, w_up, w_down)


def route(logits):
    grp_prob = jax.nn.softmax(logits[:, :N_GROUPS], axis=-1)
    grp_p, grp_idx = lax.top_k(grp_prob, 1)
    exp_logits = logits[:, N_GROUPS:N_GROUPS + N_EXPERTS].reshape(-1, N_GROUPS, EXPERTS_PER_GROUP)
    in_grp = jnp.take_along_axis(exp_logits, grp_idx[:, :, None], axis=1)[:, 0]
    top_p, top_idx = lax.top_k(jax.nn.softmax(in_grp, axis=-1), 2)
    top_w = grp_p * top_p / jnp.sum(top_p, axis=-1, keepdims=True)
    return grp_idx * EXPERTS_PER_GROUP + top_idx, top_w


def moe_apply(h_tokens, logits, w_gate, w_up, w_down, tm=256):
    t, d = h_tokens.shape
    e_idx, e_w = route(logits)
    flat_e = e_idx.reshape(-1).astype(jnp.int32)
    order = jnp.argsort(flat_e, stable=True).astype(jnp.int32)
    sorted_e = flat_e[order]
    counts = jnp.zeros((N_EXPERTS,), jnp.int32).at[flat_e].add(1)
    padded = (counts + tm - 1) // tm * tm
    seg_start = jnp.cumsum(counts) - counts
    pad_start = jnp.cumsum(padded) - padded
    dest = pad_start[sorted_e] + (jnp.arange(2 * t, dtype=jnp.int32) - seg_start[sorted_e])
    n_rows = 2 * t + N_EXPERTS * tm
    row_token = jnp.zeros((n_rows,), jnp.int32).at[dest].set(order // 2)
    row_w = jnp.zeros((n_rows,), F32).at[dest].set(e_w.reshape(-1)[order])
    pos = jnp.zeros((2 * t,), jnp.int32).at[order].set(dest).reshape(t, 2)
    tile_start = jnp.arange(n_rows // tm, dtype=jnp.int32) * tm
    pad_end = jnp.cumsum(padded)
    tile_expert = jnp.minimum(jnp.searchsorted(pad_end, tile_start, side="right"), N_EXPERTS - 1).astype(jnp.int32)
    tile_valid = (tile_start < pad_end[-1]).astype(jnp.int32)
    x_sorted = h_tokens[row_token]
    wg = w_gate.reshape(N_EXPERTS, d, EXPERT_HIDDEN)
    wu = w_up.reshape(N_EXPERTS, d, EXPERT_HIDDEN)
    wd = w_down.reshape(N_EXPERTS, EXPERT_HIDDEN, d)
    y_sorted = expert_ffn(x_sorted, row_w[:, None], tile_expert, tile_valid, wg, wu, wd, tm)
    return y_sorted[pos[:, 0]] + y_sorted[pos[:, 1]]


def _rmsnorm_kernel(x_ref, w_ref, o_ref):
    x = x_ref[...]
    o_ref[...] = x * lax.rsqrt(jnp.mean(x * x, axis=-1, keepdims=True) + EPS) * w_ref[...]


def final_norm(x, w, tm=1024):
    t, d = x.shape
    return pl.pallas_call(
        _rmsnorm_kernel,
        out_shape=jax.ShapeDtypeStruct((t, d), F32),
        grid=(t // tm,),
        in_specs=[pl.BlockSpec((tm, d), lambda i: (i, 0)), pl.BlockSpec((1, d), lambda i: (0, 0))],
        out_specs=pl.BlockSpec((tm, d), lambda i: (i, 0)),
        compiler_params=_params("parallel"),
        name="final_rmsnorm",
    )(x, w.reshape(1, d))


def _short_conv(x, w, b):
    L = x.shape[1]
    pad = w.shape[0] // 2
    xp = jnp.pad(x, ((0, 0), (pad, pad), (0, 0)))
    return sum(xp[:, k:k + L] * w[k] for k in range(w.shape[0])) + b


def _to_col_major(u):
    b, L, ch = u.shape
    rows = L // GRID_W
    return u.reshape(b, rows, GRID_W, ch).transpose(0, 2, 1, 3).reshape(b, L, ch)


def _to_row_major(u):
    b, L, ch = u.shape
    rows = L // GRID_W
    return u.reshape(b, GRID_W, rows, ch).transpose(0, 2, 1, 3).reshape(b, L, ch)


def _hyena_filters(L, p):
    t = jnp.arange(L, dtype=F32)
    t_unit = t / float(max(L - 1, 1))
    bands = jnp.linspace(1e-4, HYENA_BANDS - 1, HYENA_BANDS, dtype=F32)
    ang = (2 * math.pi / L) * t[:, None] * bands[None, :]
    feats = jnp.concatenate([t_unit[:, None], jnp.cos(ang), -jnp.sin(ang)], axis=-1)
    freq = p['hy_freq']
    hdn = jnp.sin(freq * (jnp.dot(feats, p['hy_pos_w1'], precision=HI) + p['hy_pos_b1']))
    hdn = jnp.sin(freq * (jnp.dot(hdn, p['hy_pos_w2'], precision=HI) + p['hy_pos_b2']))
    filt = jnp.dot(hdn, p['hy_pos_w3'], precision=HI).reshape(L, 2, HYENA_ORDER, HYENA_WIDTH)
    window = jnp.exp(-t_unit[:, None, None, None] * jnp.abs(p['hy_decay']))
    return filt * window


def _bidir_long_conv(z, h_fwd, h_bwd):
    L, ch = h_fwd.shape
    two_sided = jnp.concatenate([h_fwd, jnp.zeros((1, ch), h_fwd.dtype), h_bwd[1:][::-1]], axis=0)
    two_sided = two_sided / jnp.sum(jnp.abs(two_sided), axis=0, keepdims=True)
    zf = jnp.fft.rfft(z, n=2 * L, axis=1)
    hf = jnp.fft.rfft(two_sided, n=2 * L, axis=0)
    return jnp.fft.irfft(zf * hf[None], n=2 * L, axis=1)[:, :L]


def _hyena_mixer(u, p):
    uc = _short_conv(u, p['hy_conv_w'], p['hy_conv_b'])
    v, *gates = jnp.split(uc, HYENA_ORDER + 1, axis=-1)
    filt = _hyena_filters(u.shape[1], p)
    z = v
    for order, gate in enumerate(gates):
        z = gate * (_bidir_long_conv(z, filt[:, 0, order], filt[:, 1, order]) + p['hy_skip'][order] * z)
    return z


def _regroup_in_weight(w_in):
    sizes = (SSD_WIDTH, SSD_CONV_CH, 2 * SSD_HEADS, HY_COLS, 2 * ML_WIDTH, ML_WIDTH, ML_WIDTH, 4 * ML_HEADS)
    parts, s = [], 0
    for n in sizes:
        parts.append(jnp.pad(w_in[:, s:s + n], ((0, 0), (0, -n % LANE))))
        s += n
    return jnp.concatenate(parts, axis=1).astype(BF16)


def kernel(x, c, ctx, c_ctx, w_mod, b_mod, norm1_w, norm2_w, w_in, w_out, ssd_conv_w, ssd_conv_b, ssd_dt_bias, ssd_a_log, ssd_d, ssd_norm_w, hy_conv_w, hy_conv_b, hy_pos_w1, hy_pos_b1, hy_pos_w2, hy_pos_b2, hy_pos_w3, hy_freq, hy_decay, hy_skip, ml_conv_w, ml_conv_b, ml_gate_b, ml_norm_w, grp_router_w, grp_router_b, exp_router_w, exp_router_b, moe_w_gate, moe_w_up, moe_w_down, final_norm_w):
    layer_params = dict(
        ssd_conv_w=ssd_conv_w, ssd_conv_b=ssd_conv_b, ssd_dt_bias=ssd_dt_bias, ssd_a_log=ssd_a_log,
        ssd_d=ssd_d, ssd_norm_w=ssd_norm_w, hy_conv_w=hy_conv_w, hy_conv_b=hy_conv_b,
        hy_pos_w1=hy_pos_w1, hy_pos_b1=hy_pos_b1, hy_pos_w2=hy_pos_w2, hy_pos_b2=hy_pos_b2,
        hy_pos_w3=hy_pos_w3, hy_freq=hy_freq, hy_decay=hy_decay, hy_skip=hy_skip,
        ml_conv_w=ml_conv_w, ml_conv_b=ml_conv_b, ml_gate_b=ml_gate_b, ml_norm_w=ml_norm_w)
    bsz, seq, d = x.shape
    n_ctx = ctx.shape[1]
    xl, xc = x, ctx
    ssd0 = jnp.zeros((bsz, SSD_HEADS, HEAD_DIM, SSD_STATE), F32)
    ml0 = (jnp.zeros((bsz, ML_HEADS, HEAD_DIM, LANE), F32), jnp.zeros((bsz, SUBLANE, LANE), F32))
    c_rows = jnp.concatenate([c, c_ctx[None, :], jnp.zeros((SUBLANE - bsz - 1, d), F32)], axis=0)
    for i in range(DEPTH):
        last = i == DEPTH - 1
        p = {name: arr[i] for name, arr in layer_params.items()}
        sp, mp = ssd_prepare(p), ml_prepare(p)
        mod = modulation(c_rows, w_mod[i], b_mod[i]).reshape(SUBLANE, N_MOD, 1, d)
        mod_l = [mod[:bsz, k] for k in range(N_MOD)]
        mod_c = [mod[bsz:bsz + 1, k] for k in range(N_MOD)]
        w_in_p = _regroup_in_weight(w_in[i])
        w_out_b = w_out[i].astype(BF16)
        w_router = jnp.pad(jnp.concatenate([grp_router_w[i], exp_router_w[i]], axis=1),
                           ((0, 0), (0, ROUTER_COLS - N_GROUPS - N_EXPERTS)))
        b_router = jnp.pad(jnp.concatenate([grp_router_b[i], exp_router_b[i]]),
                           (0, ROUTER_COLS - N_GROUPS - N_EXPERTS)).reshape(1, ROUTER_COLS)

        uc_ssd, uc_hy, uc_ml = norm_proj(xc, norm1_w[i], mod_c[0], mod_c[1], w_in_p)
        yc_ssd, ssd_f, ssd_b = ssd_mixer(uc_ssd, sp, ssd0, ssd0, not last)
        yc_ml, ml_f, ml_b = ml_mixer(uc_ml, mp, ml0, ml0, not last)
        col_major = i % 2 == 1
        xin = _to_col_major(xl) if col_major else xl
        ul_ssd, ul_hy, ul_ml = norm_proj(xin, norm1_w[i], mod_l[0], mod_l[1], w_in_p)
        yl_ssd, _, _ = ssd_mixer(ul_ssd, sp, ssd_f, ssd_b, True)
        yl_ml, _, _ = ml_mixer(ul_ml, mp, ml_f, ml_b, True)
        yl_hy = _hyena_mixer(ul_hy, p)
        if col_major:
            yl_ssd, yl_hy, yl_ml = _to_row_major(yl_ssd), _to_row_major(yl_hy), _to_row_major(yl_ml)
        xl = out_proj(yl_ssd, yl_hy, yl_ml, xl, mod_l[2], w_out_b)
        hl, lgl = norm_router(xl, norm2_w[i], mod_l[3], mod_l[4], w_router, b_router)
        if not last:
            yc_hy = _hyena_mixer(uc_hy, p)
            xc = out_proj(yc_ssd, yc_hy, yc_ml, xc, mod_c[2], w_out_b)
            hc, lgc = norm_router(xc, norm2_w[i], mod_c[3], mod_c[4], w_router, b_router)
            h_all = jnp.concatenate([hl.reshape(-1, d), hc.reshape(-1, d)], axis=0)
            lg_all = jnp.concatenate([lgl.reshape(-1, ROUTER_COLS), lgc.reshape(-1, ROUTER_COLS)], axis=0)
        else:
            h_all, lg_all = hl.reshape(-1, d), lgl.reshape(-1, ROUTER_COLS)
        moe = moe_apply(h_all, lg_all, moe_w_gate[i], moe_w_up[i], moe_w_down[i])
        xl = xl + mod_l[5] * moe[:bsz * seq].reshape(bsz, seq, d)
        if not last:
            xc = xc + mod_c[5] * moe[bsz * seq:].reshape(bsz, n_ctx, d)
    return final_norm(xl.reshape(-1, d), final_norm_w).reshape(bsz, seq, d)
```

```python
import functools
import math

import jax
import jax.numpy as jnp
from jax import lax
from jax.experimental import pallas as pl
from jax.experimental.pallas import tpu as pltpu

D_MODEL = 1024
DEPTH = 2
GRID_W = 64
HEAD_DIM = 64
SSD_WIDTH = 384
SSD_HEADS = SSD_WIDTH // HEAD_DIM
SSD_GROUPS = 2
SSD_STATE = 64
HYENA_WIDTH = 256
HYENA_ORDER = 2
HYENA_BANDS = 16
ML_WIDTH = 384
ML_HEADS = ML_WIDTH // HEAD_DIM
N_GROUPS = 4
EXPERTS_PER_GROUP = 8
N_EXPERTS = N_GROUPS * EXPERTS_PER_GROUP
EXPERT_HIDDEN = 256
N_MOD = 6
EPS = 1e-6

LANE = 128
SUBLANE = 8
VMEM_LIMIT = 48 * 1024 * 1024

SSD_CONV_CH = SSD_WIDTH + 2 * SSD_GROUPS * SSD_STATE
SSD_XBC0 = SSD_WIDTH
SSD_DT0 = SSD_XBC0 + SSD_CONV_CH
SSD_COLS = SSD_DT0 + LANE
HY_COLS = (HYENA_ORDER + 1) * HYENA_WIDTH
ML_V0 = 2 * ML_WIDTH
ML_O0 = ML_V0 + ML_WIDTH
ML_G0 = ML_O0 + ML_WIDTH
ML_COLS = ML_G0 + LANE
ROUTER_COLS = LANE

F32 = jnp.float32
BF16 = jnp.bfloat16
HI = lax.Precision.HIGHEST
NT = (((1,), (1,)), ((), ()))
TN = (((0,), (0,)), ((), ()))


def _params(*sem):
    return pltpu.CompilerParams(dimension_semantics=sem, vmem_limit_bytes=VMEM_LIMIT)


def _silu(x):
    return x * jax.nn.sigmoid(x)


def _softplus(x):
    return jnp.maximum(x, 0.0) + jnp.log(1.0 + jnp.exp(-jnp.abs(x)))


def _log_sigmoid(x):
    return jnp.minimum(x, 0.0) - jnp.log(1.0 + jnp.exp(-jnp.abs(x)))


def _mod_kernel(c_ref, w_ref, b_ref, o_ref):
    o_ref[...] = jnp.dot(_silu(c_ref[...]), w_ref[...], preferred_element_type=F32, precision=HI) + b_ref[...]


def modulation(c_rows, w_mod, b_mod):
    n = w_mod.shape[1]
    tn = 1536
    return pl.pallas_call(
        _mod_kernel,
        out_shape=jax.ShapeDtypeStruct((c_rows.shape[0], n), F32),
        grid=(n // tn,),
        in_specs=[pl.BlockSpec(c_rows.shape, lambda j: (0, 0)),
                  pl.BlockSpec((w_mod.shape[0], tn), lambda j: (0, j)),
                  pl.BlockSpec((1, tn), lambda j: (0, j))],
        out_specs=pl.BlockSpec((c_rows.shape[0], tn), lambda j: (0, j)),
        compiler_params=_params("arbitrary"),
        name="adaln_modulation",
    )(c_rows, w_mod, b_mod.reshape(1, n))


def _modnorm(x, nw, shift, scale):
    y = x * lax.rsqrt(jnp.mean(x * x, axis=-1, keepdims=True) + EPS) * nw
    return y * (1.0 + scale) + shift


def _mod_map(mod, b):
    return (lambda bi, i: (bi, 0, 0)) if mod.shape[0] == b else (lambda bi, i: (0, 0, 0))


def _norm_proj_kernel(x_ref, nw_ref, sh_ref, sc_ref, w_ref, ssd_ref, hy_ref, ml_ref):
    h = _modnorm(x_ref[0], nw_ref[...], sh_ref[0], sc_ref[0])
    u = jnp.dot(h.astype(BF16), w_ref[...], preferred_element_type=F32)
    ssd_ref[0] = u[:, 0:SSD_COLS]
    hy_ref[0] = u[:, SSD_COLS:SSD_COLS + HY_COLS]
    ml_ref[0] = u[:, SSD_COLS + HY_COLS:]


def norm_proj(x, nw, shift, scale, w_bf16, tm=256):
    b, L, d = x.shape
    n = w_bf16.shape[1]
    row = lambda bi, i: (bi, i, 0)
    return pl.pallas_call(
        _norm_proj_kernel,
        out_shape=(jax.ShapeDtypeStruct((b, L, SSD_COLS), F32),
                   jax.ShapeDtypeStruct((b, L, HY_COLS), F32),
                   jax.ShapeDtypeStruct((b, L, ML_COLS), F32)),
        grid=(b, L // tm),
        in_specs=[pl.BlockSpec((1, tm, d), row),
                  pl.BlockSpec((1, d), lambda bi, i: (0, 0)),
                  pl.BlockSpec((1, 1, d), _mod_map(shift, b)),
                  pl.BlockSpec((1, 1, d), _mod_map(scale, b)),
                  pl.BlockSpec((d, n), lambda bi, i: (0, 0))],
        out_specs=(pl.BlockSpec((1, tm, SSD_COLS), row), pl.BlockSpec((1, tm, HY_COLS), row),
                   pl.BlockSpec((1, tm, ML_COLS), row)),
        compiler_params=_params("parallel", "arbitrary"),
        name="norm_in_proj",
    )(x, nw.reshape(1, d), shift, scale, w_bf16)


def _out_proj_kernel(ys_ref, yh_ref, ym_ref, x_ref, g_ref, w_ref, o_ref):
    y = jnp.concatenate([ys_ref[0], yh_ref[0], ym_ref[0]], axis=-1).astype(BF16)
    o_ref[0] = x_ref[0] + g_ref[0] * jnp.dot(y, w_ref[...], preferred_element_type=F32)


def out_proj(y_ssd, y_hy, y_ml, x, gate, w_bf16, tm=512):
    b, L, d = x.shape
    tm = min(tm, L)
    row = lambda bi, i: (bi, i, 0)
    return pl.pallas_call(
        _out_proj_kernel,
        out_shape=jax.ShapeDtypeStruct((b, L, d), F32),
        grid=(b, L // tm),
        in_specs=[pl.BlockSpec((1, tm, SSD_WIDTH), row), pl.BlockSpec((1, tm, HYENA_WIDTH), row),
                  pl.BlockSpec((1, tm, ML_WIDTH), row), pl.BlockSpec((1, tm, d), row),
                  pl.BlockSpec((1, 1, d), _mod_map(gate, b)),
                  pl.BlockSpec(w_bf16.shape, lambda bi, i: (0, 0))],
        out_specs=pl.BlockSpec((1, tm, d), row),
        compiler_params=_params("parallel", "arbitrary"),
        name="out_proj_residual",
    )(y_ssd, y_hy, y_ml, x, gate, w_bf16)


def _conv3(xr, prev_row, next_row, cw, cb, q):
    rid = lax.broadcasted_iota(jnp.int32, (q, 1), 0)
    x_prev = jnp.where(rid == 0, prev_row, pltpu.roll(xr, 1, axis=0))
    x_next = jnp.where(rid == q - 1, next_row, pltpu.roll(xr, q - 1, axis=0))
    return x_prev * cw[0:1] + xr * cw[1:2] + x_next * cw[2:3] + cb


def _scan_mask(q, direction):
    li = lax.broadcasted_iota(jnp.int32, (q, q), 0)
    si = lax.broadcasted_iota(jnp.int32, (q, q), 1)
    return (si <= li) if direction == 0 else (si >= li)


def _scan_specs(L, q, nc, cols, direction):
    hb = q // SUBLANE
    nrb = L // SUBLANE
    cidx = (lambda j: j) if direction == 0 else (lambda j: nc - 1 - j)
    specs = [pl.BlockSpec((1, q, cols), lambda bi, j: (bi, cidx(j), 0)),
             pl.BlockSpec((1, SUBLANE, cols), lambda bi, j: (bi, jnp.maximum(cidx(j) * hb - 1, 0), 0)),
             pl.BlockSpec((1, SUBLANE, cols), lambda bi, j: (bi, jnp.minimum((cidx(j) + 1) * hb, nrb - 1), 0))]
    return specs, cidx


def _ssd_kernel(*refs, direction, finalize, q, nc):
    if finalize:
        (u_ref, prev_ref, next_ref, yb_ref, init_ref, cw_ref, cb_ref, dtb_ref, a_ref, d_ref, nw_ref,
         y_ref, fin_ref, state_ref) = refs
    else:
        (u_ref, prev_ref, next_ref, init_ref, cw_ref, cb_ref, dtb_ref, a_ref,
         y_ref, fin_ref, state_ref) = refs
    j = pl.program_id(1)
    c = j if direction == 0 else nc - 1 - j

    @pl.when(j == 0)
    def _():
        state_ref[...] = init_ref[0]

    prev_row = jnp.where(c > 0, prev_ref[0, SUBLANE - 1:SUBLANE, SSD_XBC0:SSD_DT0], 0.0)
    next_row = jnp.where(c < nc - 1, next_ref[0, 0:1, SSD_XBC0:SSD_DT0], 0.0)
    xc = _silu(_conv3(u_ref[0, :, SSD_XBC0:SSD_DT0], prev_row, next_row, cw_ref[...], cb_ref[...], q))

    dt = _softplus(u_ref[0, :, SSD_DT0:SSD_COLS] + dtb_ref[...])
    mask = _scan_mask(q, direction)
    cum = jnp.dot(mask.astype(F32), dt * a_ref[...], preferred_element_type=F32, precision=HI)
    cum_t = cum.T
    end = q - 1 if direction == 0 else 0

    ys = []
    for g in range(SSD_GROUPS):
        b0 = SSD_WIDTH + g * SSD_STATE
        c0 = SSD_WIDTH + (SSD_GROUPS + g) * SSD_STATE
        bm = xc[:, b0:b0 + SSD_STATE].astype(BF16)
        cm = xc[:, c0:c0 + SSD_STATE].astype(BF16)
        scores = lax.dot_general(cm, bm, NT, preferred_element_type=F32)
        for h in range(g * (SSD_HEADS // SSD_GROUPS), (g + 1) * (SSD_HEADS // SSD_GROUPS)):
            hl = direction * SSD_HEADS + h
            col = cum[:, hl:hl + 1]
            seg = jnp.exp(jnp.where(mask, col - cum_t[hl:hl + 1, :], -jnp.inf))
            xdt = xc[:, h * HEAD_DIM:(h + 1) * HEAD_DIM] * dt[:, hl:hl + 1]
            y = jnp.dot((scores * seg).astype(BF16), xdt.astype(BF16), preferred_element_type=F32)
            st = state_ref[h]
            y = y + lax.dot_general(cm, st.astype(BF16), NT, preferred_element_type=F32) * jnp.exp(col)
            tot = cum[end:end + 1, hl:hl + 1]
            upd = lax.dot_general((xdt * jnp.exp(tot - col)).astype(BF16), bm, TN, preferred_element_type=F32)
            state_ref[h] = st * jnp.exp(tot) + upd
            ys.append(y)
    y_all = jnp.concatenate(ys, axis=-1)
    if finalize:
        t = (y_all + yb_ref[0] + xc[:, 0:SSD_WIDTH] * d_ref[...]) * _silu(u_ref[0, :, 0:SSD_WIDTH])
        y_all = t * lax.rsqrt(jnp.mean(t * t, axis=-1, keepdims=True) + EPS) * nw_ref[...]
    y_ref[0] = y_all

    @pl.when(j == nc - 1)
    def _():
        fin_ref[0] = state_ref[...]


def ssd_pass(u, y_other, init, sp, direction, q):
    b, L, _ = u.shape
    q = min(q, L)
    nc = L // q
    finalize = y_other is not None
    in_specs, cidx = _scan_specs(L, q, nc, SSD_COLS, direction)
    const2 = lambda bi, j: (0, 0)
    st_spec = pl.BlockSpec((1, SSD_HEADS, HEAD_DIM, SSD_STATE), lambda bi, j: (bi, 0, 0, 0))
    y_spec = pl.BlockSpec((1, q, SSD_WIDTH), lambda bi, j: (bi, cidx(j), 0))
    args = [u, u, u]
    if finalize:
        in_specs.append(y_spec)
        args.append(y_other)
    consts = [sp['cw'], sp['cb'], sp['dtb'], sp['a']] + ([sp['d'], sp['nw']] if finalize else [])
    in_specs += [st_spec] + [pl.BlockSpec(t.shape, const2) for t in consts]
    args += [init] + consts
    return pl.pallas_call(
        functools.partial(_ssd_kernel, direction=direction, finalize=finalize, q=q, nc=nc),
        out_shape=(jax.ShapeDtypeStruct((b, L, SSD_WIDTH), F32),
                   jax.ShapeDtypeStruct((b, SSD_HEADS, HEAD_DIM, SSD_STATE), F32)),
        grid=(b, nc),
        in_specs=in_specs,
        out_specs=(y_spec, st_spec),
        scratch_shapes=[pltpu.VMEM((SSD_HEADS, HEAD_DIM, SSD_STATE), F32)],
        compiler_params=_params("parallel", "arbitrary"),
        name="ssd_scan_%s" % ("fwd" if direction == 0 else "bwd"),
    )(*args)


def ssd_prepare(p):
    pad = lambda v: jnp.pad(v.reshape(1, -1), ((0, 0), (0, LANE - v.size)))
    return dict(cw=p['ssd_conv_w'], cb=p['ssd_conv_b'].reshape(1, -1),
                dtb=pad(p['ssd_dt_bias']), a=pad(-jnp.exp(p['ssd_a_log'])),
                d=jnp.repeat(p['ssd_d'], HEAD_DIM).reshape(1, -1), nw=p['ssd_norm_w'].reshape(1, -1))


def ssd_mixer(u, sp, init_f, init_b, want_y, q=256):
    yb, fin_b = ssd_pass(u, None, init_b, sp, 1, q)
    y, fin_f = ssd_pass(u, yb if want_y else None, init_f, sp, 0, q)
    return y, fin_f, fin_b


def _ml_kernel(*refs, direction, finalize, q, nc):
    if finalize:
        (u_ref, prev_ref, next_ref, hb_ref, s_init_ref, m_init_ref, cw_ref, cb_ref, gb_ref, nw_ref,
         y_ref, s_fin_ref, m_fin_ref, s_ref, m_ref) = refs
    else:
        (u_ref, prev_ref, next_ref, s_init_ref, m_init_ref, cw_ref, cb_ref, gb_ref,
         y_ref, s_fin_ref, m_fin_ref, s_ref, m_ref) = refs
    j = pl.program_id(1)
    c = j if direction == 0 else nc - 1 - j

    @pl.when(j == 0)
    def _():
        s_ref[...] = s_init_ref[0]
        m_ref[...] = m_init_ref[0]

    prev_row = jnp.where(c > 0, prev_ref[0, SUBLANE - 1:SUBLANE, 0:ML_V0], 0.0)
    next_row = jnp.where(c < nc - 1, next_ref[0, 0:1, 0:ML_V0], 0.0)
    qk = _silu(_conv3(u_ref[0, :, 0:ML_V0], prev_row, next_row, cw_ref[...], cb_ref[...], q))
    v = u_ref[0, :, ML_V0:ML_O0]

    gb = u_ref[0, :, ML_G0:ML_COLS] + gb_ref[...]
    mask = _scan_mask(q, direction)
    cum = jnp.dot(mask.astype(F32), _log_sigmoid(gb), preferred_element_type=F32, precision=HI)
    ig = pltpu.roll(gb, ML_HEADS, axis=1)
    end = q - 1 if direction == 0 else 0
    m_prev = m_ref[0:1, :]
    tot = cum[end:end + 1, :]
    w_end = tot - cum + ig
    m_loc = jnp.max(w_end, axis=0, keepdims=True)
    e_end = jnp.exp(w_end - m_loc)
    m_new = jnp.maximum(tot + m_prev, m_loc)
    a_prev = jnp.exp(tot + m_prev - m_new)
    a_loc = jnp.exp(m_loc - m_new)
    inter = cum + m_prev
    cum_t = cum.T
    ig_t = ig.T
    one_col = (lax.broadcasted_iota(jnp.int32, (q, HEAD_DIM), 1) == 0).astype(F32)

    ys = []
    for h in range(ML_HEADS):
        fl = direction * 2 * ML_HEADS + ML_HEADS + h
        qh = qk[:, h * HEAD_DIM:(h + 1) * HEAD_DIM].astype(BF16)
        kh = qk[:, ML_WIDTH + h * HEAD_DIM:ML_WIDTH + (h + 1) * HEAD_DIM] * (HEAD_DIM ** -0.5)
        v_ext = jnp.concatenate([v[:, h * HEAD_DIM:(h + 1) * HEAD_DIM], one_col], axis=-1).astype(BF16)
        log_w = jnp.where(mask, cum[:, fl:fl + 1] - cum_t[fl:fl + 1, :] + ig_t[fl:fl + 1, :], -jnp.inf)
        inter_c = inter[:, fl:fl + 1]
        m_t = jnp.maximum(inter_c, jnp.max(log_w, axis=-1, keepdims=True))
        scores = lax.dot_general(qh, kh.astype(BF16), NT, preferred_element_type=F32)
        sm = scores * jnp.exp(log_w - m_t)
        nd = jnp.dot(sm.astype(BF16), v_ext, preferred_element_type=F32)
        st = s_ref[h]
        nd = nd + jnp.exp(inter_c - m_t) * jnp.dot(qh, st.astype(BF16), preferred_element_type=F32)
        den = nd[:, HEAD_DIM:HEAD_DIM + 1]
        hid = nd[:, 0:HEAD_DIM] / jnp.maximum(jnp.abs(den), jnp.exp(-m_t))
        upd = lax.dot_general((kh * e_end[:, fl:fl + 1]).astype(BF16), v_ext, TN, preferred_element_type=F32)
        s_ref[h] = a_prev[:, fl:fl + 1] * st + a_loc[:, fl:fl + 1] * upd
        if finalize:
            hs = hid + hb_ref[0, :, h * HEAD_DIM:(h + 1) * HEAD_DIM]
            hc = hs - jnp.mean(hs, axis=-1, keepdims=True)
            hid = hc * lax.rsqrt(jnp.mean(hc * hc, axis=-1, keepdims=True) + EPS)
        ys.append(hid)
    m_ref[...] = jnp.broadcast_to(m_new, m_ref.shape)
    y_all = jnp.concatenate(ys, axis=-1)
    if finalize:
        y_all = y_all * nw_ref[...] * jax.nn.sigmoid(u_ref[0, :, ML_O0:ML_G0])
    y_ref[0] = y_all

    @pl.when(j == nc - 1)
    def _():
        s_fin_ref[0] = s_ref[...]
        m_fin_ref[0] = m_ref[...]


def ml_pass(u, h_other, init, mp, direction, q):
    b, L, _ = u.shape
    q = min(q, L)
    nc = L // q
    finalize = h_other is not None
    in_specs, cidx = _scan_specs(L, q, nc, ML_COLS, direction)
    const2 = lambda bi, j: (0, 0)
    s_spec = pl.BlockSpec((1, ML_HEADS, HEAD_DIM, LANE), lambda bi, j: (bi, 0, 0, 0))
    m_spec = pl.BlockSpec((1, SUBLANE, LANE), lambda bi, j: (bi, 0, 0))
    y_spec = pl.BlockSpec((1, q, ML_WIDTH), lambda bi, j: (bi, cidx(j), 0))
    args = [u, u, u]
    if finalize:
        in_specs.append(y_spec)
        args.append(h_other)
    consts = [mp['cw'], mp['cb'], mp['gb']] + ([mp['nw']] if finalize else [])
    in_specs += [s_spec, m_spec] + [pl.BlockSpec(t.shape, const2) for t in consts]
    args += [init[0], init[1]] + consts
    y, s_fin, m_fin = pl.pallas_call(
        functools.partial(_ml_kernel, direction=direction, finalize=finalize, q=q, nc=nc),
        out_shape=(jax.ShapeDtypeStruct((b, L, ML_WIDTH), F32),
                   jax.ShapeDtypeStruct((b, ML_HEADS, HEAD_DIM, LANE), F32),
                   jax.ShapeDtypeStruct((b, SUBLANE, LANE), F32)),
        grid=(b, nc),
        in_specs=in_specs,
        out_specs=(y_spec, s_spec, m_spec),
        scratch_shapes=[pltpu.VMEM((ML_HEADS, HEAD_DIM, LANE), F32), pltpu.VMEM((SUBLANE, LANE), F32)],
        compiler_params=_params("parallel", "arbitrary"),
        name="mlstm_scan_%s" % ("fwd" if direction == 0 else "bwd"),
    )(*args)
    return y, (s_fin, m_fin)


def ml_prepare(p):
    gb = p['ml_gate_b'].reshape(1, -1)
    return dict(cw=p['ml_conv_w'], cb=p['ml_conv_b'].reshape(1, -1),
                gb=jnp.pad(gb, ((0, 0), (0, LANE - gb.shape[1]))), nw=p['ml_norm_w'].reshape(1, -1))


def ml_mixer(u, mp, init_f, init_b, want_y, q=256):
    hb, fin_b = ml_pass(u, None, init_b, mp, 1, q)
    y, fin_f = ml_pass(u, hb if want_y else None, init_f, mp, 0, q)
    return y, fin_f, fin_b


def _norm_router_kernel(x_ref, nw_ref, sh_ref, sc_ref, wr_ref, br_ref, h_ref, lg_ref):
    h = _modnorm(x_ref[0], nw_ref[...], sh_ref[0], sc_ref[0])
    h_ref[0] = h
    lg_ref[0] = jnp.dot(h, wr_ref[...], preferred_element_type=F32, precision=HI) + br_ref[...]


def norm_router(x, nw, shift, scale, w_router, b_router, tm=512):
    b, L, d = x.shape
    tm = min(tm, L)
    row = lambda bi, i: (bi, i, 0)
    return pl.pallas_call(
        _norm_router_kernel,
        out_shape=(jax.ShapeDtypeStruct((b, L, d), F32),
                   jax.ShapeDtypeStruct((b, L, ROUTER_COLS), F32)),
        grid=(b, L // tm),
        in_specs=[pl.BlockSpec((1, tm, d), row),
                  pl.BlockSpec((1, d), lambda bi, i: (0, 0)),
                  pl.BlockSpec((1, 1, d), _mod_map(shift, b)),
                  pl.BlockSpec((1, 1, d), _mod_map(scale, b)),
                  pl.BlockSpec((d, ROUTER_COLS), lambda bi, i: (0, 0)),
                  pl.BlockSpec((1, ROUTER_COLS), lambda bi, i: (0, 0))],
        out_specs=(pl.BlockSpec((1, tm, d), row), pl.BlockSpec((1, tm, ROUTER_COLS), row)),
        compiler_params=_params("parallel", "arbitrary"),
        name="moe_norm_router",
    )(x, nw.reshape(1, d), shift, scale, w_router, b_router)


def _expert_ffn_kernel(te_ref, tv_ref, x_ref, rw_ref, wg_ref, wu_ref, wd_ref, o_ref):
    i = pl.program_id(0)

    @pl.when(tv_ref[i] > 0)
    def _():
        x = x_ref[...].astype(BF16)
        g = jnp.dot(x, wg_ref[0].astype(BF16), preferred_element_type=F32)
        u = jnp.dot(x, wu_ref[0].astype(BF16), preferred_element_type=F32)
        hid = _silu(g) * u * rw_ref[...]
        o_ref[...] = jnp.dot(hid.astype(BF16), wd_ref[0].astype(BF16), preferred_element_type=F32)

    @pl.when(tv_ref[i] == 0)
    def _():
        o_ref[...] = jnp.zeros_like(o_ref)


def expert_ffn(x_sorted, row_w, tile_expert, tile_valid, w_gate, w_up, w_down, tm):
    r, d = x_sorted.shape
    f = w_gate.shape[-1]
    grid_spec = pltpu.PrefetchScalarGridSpec(
        num_scalar_prefetch=2,
        grid=(r // tm,),
        in_specs=[pl.BlockSpec((tm, d), lambda i, te, tv: (i, 0)),
                  pl.BlockSpec((tm, 1), lambda i, te, tv: (i, 0)),
                  pl.BlockSpec((1, d, f), lambda i, te, tv: (te[i], 0, 0)),
                  pl.BlockSpec((1, d, f), lambda i, te, tv: (te[i], 0, 0)),
                  pl.BlockSpec((1, f, d), lambda i, te, tv: (te[i], 0, 0))],
        out_specs=pl.BlockSpec((tm, d), lambda i, te, tv: (i, 0)),
    )
    return pl.pallas_call(
        _expert_ffn_kernel,
        out_shape=jax.ShapeDtypeStruct((r, d), F32),
        grid_spec=grid_spec,
        compiler_params=_params("arbitrary"),
        name="moe_expert_ffn",
    )(tile_expert, tile_valid, x_sorted, row_w, w_gate, w_up, w_down)


def route(logits):
    grp_prob = jax.nn.softmax(logits[:, :N_GROUPS], axis=-1)
    grp_p, grp_idx = lax.top_k(grp_prob, 1)
    exp_logits = logits[:, N_GROUPS:N_GROUPS + N_EXPERTS].reshape(-1, N_GROUPS, EXPERTS_PER_GROUP)
    in_grp = jnp.take_along_axis(exp_logits, grp_idx[:, :, None], axis=1)[:, 0]
    top_p, top_idx = lax.top_k(jax.nn.softmax(in_grp, axis=-1), 2)
    top_w = grp_p * top_p / jnp.sum(top_p, axis=-1, keepdims=True)
    return grp_idx * EXPERTS_PER_GROUP + top_idx, top_w


def moe_apply(h_tokens, logits, w_gate, w_up, w_down, tm=256):
    t, d = h_tokens.shape
    e_idx, e_w = route(logits)
    flat_e = e_idx.reshape(-1).astype(jnp.int32)
    n_pairs = 2 * t
    sorted_e, order = lax.sort((flat_e, jnp.arange(n_pairs, dtype=jnp.int32)), num_keys=1, is_stable=True)
    seg_end = jnp.searchsorted(sorted_e, jnp.arange(N_EXPERTS, dtype=jnp.int32), side="right").astype(jnp.int32)
    counts = seg_end - jnp.concatenate([jnp.zeros((1,), jnp.int32), seg_end[:-1]])
    seg_start = seg_end - counts
    padded = (counts + tm - 1) // tm * tm
    pad_end = jnp.cumsum(padded)
    pad_start = pad_end - padded
    n_rows = n_pairs + N_EXPERTS * tm
    tile_start = jnp.arange(n_rows // tm, dtype=jnp.int32) * tm
    tile_expert = jnp.minimum(jnp.searchsorted(pad_end, tile_start, side="right"), N_EXPERTS - 1).astype(jnp.int32)
    tile_valid = (tile_start < pad_end[-1]).astype(jnp.int32)
    off = (tile_start - pad_start[tile_expert])[:, None] + jnp.arange(tm, dtype=jnp.int32)[None, :]
    valid = (off < counts[tile_expert][:, None]) & (tile_valid[:, None] > 0)
    pair = jnp.where(valid, seg_start[tile_expert][:, None] + off, 0).reshape(-1)
    src = order[pair]
    row_token = src // 2
    row_w = jnp.where(valid.reshape(-1), e_w.reshape(-1)[src], 0.0)
    dest = pad_start[sorted_e] + (jnp.arange(n_pairs, dtype=jnp.int32) - seg_start[sorted_e])
    _, pos = lax.sort((order, dest), num_keys=1)
    pos = pos.reshape(t, 2)
    x_sorted = h_tokens[row_token]
    wg = w_gate.reshape(N_EXPERTS, d, EXPERT_HIDDEN)
    wu = w_up.reshape(N_EXPERTS, d, EXPERT_HIDDEN)
    wd = w_down.reshape(N_EXPERTS, EXPERT_HIDDEN, d)
    y_sorted = expert_ffn(x_sorted, row_w[:, None], tile_expert, tile_valid, wg, wu, wd, tm)
    return y_sorted[pos[:, 0]] + y_sorted[pos[:, 1]]


def _rmsnorm_kernel(x_ref, w_ref, o_ref):
    x = x_ref[...]
    o_ref[...] = x * lax.rsqrt(jnp.mean(x * x, axis=-1, keepdims=True) + EPS) * w_ref[...]


def final_norm(x, w, tm=1024):
    t, d = x.shape
    return pl.pallas_call(
        _rmsnorm_kernel,
        out_shape=jax.ShapeDtypeStruct((t, d), F32),
        grid=(t // tm,),
        in_specs=[pl.BlockSpec((tm, d), lambda i: (i, 0)), pl.BlockSpec((1, d), lambda i: (0, 0))],
        out_specs=pl.BlockSpec((tm, d), lambda i: (i, 0)),
        compiler_params=_params("parallel"),
        name="final_rmsnorm",
    )(x, w.reshape(1, d))


def _short_conv(x, w, b):
    L = x.shape[1]
    pad = w.shape[0] // 2
    xp = jnp.pad(x, ((0, 0), (pad, pad), (0, 0)))
    return sum(xp[:, k:k + L] * w[k] for k in range(w.shape[0])) + b


def _to_col_major(u):
    b, L, ch = u.shape
    rows = L // GRID_W
    return u.reshape(b, rows, GRID_W, ch).transpose(0, 2, 1, 3).reshape(b, L, ch)


def _to_row_major(u):
    b, L, ch = u.shape
    rows = L // GRID_W
    return u.reshape(b, GRID_W, rows, ch).transpose(0, 2, 1, 3).reshape(b, L, ch)


def _hyena_filters(L, p):
    t = jnp.arange(L, dtype=F32)
    t_unit = t / float(max(L - 1, 1))
    bands = jnp.linspace(1e-4, HYENA_BANDS - 1, HYENA_BANDS, dtype=F32)
    ang = (2 * math.pi / L) * t[:, None] * bands[None, :]
    feats = jnp.concatenate([t_unit[:, None], jnp.cos(ang), -jnp.sin(ang)], axis=-1)
    freq = p['hy_freq']
    hdn = jnp.sin(freq * (jnp.dot(feats, p['hy_pos_w1'], precision=HI) + p['hy_pos_b1']))
    hdn = jnp.sin(freq * (jnp.dot(hdn, p['hy_pos_w2'], precision=HI) + p['hy_pos_b2']))
    filt = jnp.dot(hdn, p['hy_pos_w3'], precision=HI).reshape(L, 2, HYENA_ORDER, HYENA_WIDTH)
    window = jnp.exp(-t_unit[:, None, None, None] * jnp.abs(p['hy_decay']))
    return filt * window


def _bidir_long_conv(z, h_fwd, h_bwd):
    L, ch = h_fwd.shape
    two_sided = jnp.concatenate([h_fwd, jnp.zeros((1, ch), h_fwd.dtype), h_bwd[1:][::-1]], axis=0)
    two_sided = two_sided / jnp.sum(jnp.abs(two_sided), axis=0, keepdims=True)
    zf = jnp.fft.rfft(z, n=2 * L, axis=1)
    hf = jnp.fft.rfft(two_sided, n=2 * L, axis=0)
    return jnp.fft.irfft(zf * hf[None], n=2 * L, axis=1)[:, :L]


def _hyena_mixer(u, p):
    uc = _short_conv(u, p['hy_conv_w'], p['hy_conv_b'])
    v, *gates = jnp.split(uc, HYENA_ORDER + 1, axis=-1)
    filt = _hyena_filters(u.shape[1], p)
    z = v
    for order, gate in enumerate(gates):
        z = gate * (_bidir_long_conv(z, filt[:, 0, order], filt[:, 1, order]) + p['hy_skip'][order] * z)
    return z


def _regroup_in_weight(w_in):
    sizes = (SSD_WIDTH, SSD_CONV_CH, 2 * SSD_HEADS, HY_COLS, 2 * ML_WIDTH, ML_WIDTH, ML_WIDTH, 4 * ML_HEADS)
    parts, s = [], 0
    for n in sizes:
        parts.append(jnp.pad(w_in[:, s:s + n], ((0, 0), (0, -n % LANE))))
        s += n
    return jnp.concatenate(parts, axis=1).astype(BF16)


def kernel(x, c, ctx, c_ctx, w_mod, b_mod, norm1_w, norm2_w, w_in, w_out, ssd_conv_w, ssd_conv_b, ssd_dt_bias, ssd_a_log, ssd_d, ssd_norm_w, hy_conv_w, hy_conv_b, hy_pos_w1, hy_pos_b1, hy_pos_w2, hy_pos_b2, hy_pos_w3, hy_freq, hy_decay, hy_skip, ml_conv_w, ml_conv_b, ml_gate_b, ml_norm_w, grp_router_w, grp_router_b, exp_router_w, exp_router_b, moe_w_gate, moe_w_up, moe_w_down, final_norm_w):
    layer_params = dict(
        ssd_conv_w=ssd_conv_w, ssd_conv_b=ssd_conv_b, ssd_dt_bias=ssd_dt_bias, ssd_a_log=ssd_a_log,
        ssd_d=ssd_d, ssd_norm_w=ssd_norm_w, hy_conv_w=hy_conv_w, hy_conv_b=hy_conv_b,
        hy_pos_w1=hy_pos_w1, hy_pos_b1=hy_pos_b1, hy_pos_w2=hy_pos_w2, hy_pos_b2=hy_pos_b2,
        hy_pos_w3=hy_pos_w3, hy_freq=hy_freq, hy_decay=hy_decay, hy_skip=hy_skip,
        ml_conv_w=ml_conv_w, ml_conv_b=ml_conv_b, ml_gate_b=ml_gate_b, ml_norm_w=ml_norm_w)
    bsz, seq, d = x.shape
    n_ctx = ctx.shape[1]
    xl, xc = x, ctx
    ssd0 = jnp.zeros((bsz, SSD_HEADS, HEAD_DIM, SSD_STATE), F32)
    ml0 = (jnp.zeros((bsz, ML_HEADS, HEAD_DIM, LANE), F32), jnp.zeros((bsz, SUBLANE, LANE), F32))
    c_rows = jnp.concatenate([c, c_ctx[None, :], jnp.zeros((SUBLANE - bsz - 1, d), F32)], axis=0)
    for i in range(DEPTH):
        last = i == DEPTH - 1
        p = {name: arr[i] for name, arr in layer_params.items()}
        sp, mp = ssd_prepare(p), ml_prepare(p)
        mod = modulation(c_rows, w_mod[i], b_mod[i]).reshape(SUBLANE, N_MOD, 1, d)
        mod_l = [mod[:bsz, k] for k in range(N_MOD)]
        mod_c = [mod[bsz:bsz + 1, k] for k in range(N_MOD)]
        w_in_p = _regroup_in_weight(w_in[i])
        w_out_b = w_out[i].astype(BF16)
        w_router = jnp.pad(jnp.concatenate([grp_router_w[i], exp_router_w[i]], axis=1),
                           ((0, 0), (0, ROUTER_COLS - N_GROUPS - N_EXPERTS)))
        b_router = jnp.pad(jnp.concatenate([grp_router_b[i], exp_router_b[i]]),
                           (0, ROUTER_COLS - N_GROUPS - N_EXPERTS)).reshape(1, ROUTER_COLS)

        uc_ssd, uc_hy, uc_ml = norm_proj(xc, norm1_w[i], mod_c[0], mod_c[1], w_in_p)
        yc_ssd, ssd_f, ssd_b = ssd_mixer(uc_ssd, sp, ssd0, ssd0, not last)
        yc_ml, ml_f, ml_b = ml_mixer(uc_ml, mp, ml0, ml0, not last)
        col_major = i % 2 == 1
        xin = _to_col_major(xl) if col_major else xl
        ul_ssd, ul_hy, ul_ml = norm_proj(xin, norm1_w[i], mod_l[0], mod_l[1], w_in_p)
        yl_ssd, _, _ = ssd_mixer(ul_ssd, sp, ssd_f, ssd_b, True)
        yl_ml, _, _ = ml_mixer(ul_ml, mp, ml_f, ml_b, True)
        yl_hy = _hyena_mixer(ul_hy, p)
        if col_major:
            yl_ssd, yl_hy, yl_ml = _to_row_major(yl_ssd), _to_row_major(yl_hy), _to_row_major(yl_ml)
        xl = out_proj(yl_ssd, yl_hy, yl_ml, xl, mod_l[2], w_out_b)
        hl, lgl = norm_router(xl, norm2_w[i], mod_l[3], mod_l[4], w_router, b_router)
        if not last:
            yc_hy = _hyena_mixer(uc_hy, p)
            xc = out_proj(yc_ssd, yc_hy, yc_ml, xc, mod_c[2], w_out_b)
            hc, lgc = norm_router(xc, norm2_w[i], mod_c[3], mod_c[4], w_router, b_router)
            h_all = jnp.concatenate([hl.reshape(-1, d), hc.reshape(-1, d)], axis=0)
            lg_all = jnp.concatenate([lgl.reshape(-1, ROUTER_COLS), lgc.reshape(-1, ROUTER_COLS)], axis=0)
        else:
            h_all, lg_all = hl.reshape(-1, d), lgl.reshape(-1, ROUTER_COLS)
        moe = moe_apply(h_all, lg_all, moe_w_gate[i], moe_w_up[i], moe_w_down[i])
        xl = xl + mod_l[5] * moe[:bsz * seq].reshape(bsz, seq, d)
        if not last:
            xc = xc + mod_c[5] * moe[bsz * seq:].reshape(bsz, n_ctx, d)
    return final_norm(xl.reshape(-1, d), final_norm_w).reshape(bsz, seq, d)
```

```python
import functools
import math

import jax
import jax.numpy as jnp
import numpy as np
from jax import lax
from jax.experimental import pallas as pl
from jax.experimental.pallas import tpu as pltpu

D_MODEL = 1024
DEPTH = 2
GRID_W = 64
HEAD_DIM = 64
SSD_WIDTH = 384
SSD_HEADS = SSD_WIDTH // HEAD_DIM
SSD_GROUPS = 2
SSD_STATE = 64
HYENA_WIDTH = 256
HYENA_ORDER = 2
HYENA_BANDS = 16
ML_WIDTH = 384
ML_HEADS = ML_WIDTH // HEAD_DIM
N_GROUPS = 4
EXPERTS_PER_GROUP = 8
N_EXPERTS = N_GROUPS * EXPERTS_PER_GROUP
EXPERT_HIDDEN = 256
N_MOD = 6
EPS = 1e-6

LANE = 128
SUBLANE = 8
VMEM_LIMIT = 48 * 1024 * 1024
VMEM_LIMIT_HY = 56 * 1024 * 1024

SSD_CONV_CH = SSD_WIDTH + 2 * SSD_GROUPS * SSD_STATE
SSD_XBC0 = SSD_WIDTH
SSD_DT0 = SSD_XBC0 + SSD_CONV_CH
SSD_COLS = SSD_DT0 + LANE
HY_COLS = (HYENA_ORDER + 1) * HYENA_WIDTH
ML_V0 = 2 * ML_WIDTH
ML_O0 = ML_V0 + ML_WIDTH
ML_G0 = ML_O0 + ML_WIDTH
ML_COLS = ML_G0 + LANE
ROUTER_COLS = LANE

F32 = jnp.float32
BF16 = jnp.bfloat16
HI = lax.Precision.HIGHEST
NT = (((1,), (1,)), ((), ()))
TN = (((0,), (0,)), ((), ()))


def _params(*sem):
    return pltpu.CompilerParams(dimension_semantics=sem, vmem_limit_bytes=VMEM_LIMIT)


def _silu(x):
    return x * jax.nn.sigmoid(x)


def _softplus(x):
    return jnp.maximum(x, 0.0) + jnp.log(1.0 + jnp.exp(-jnp.abs(x)))


def _log_sigmoid(x):
    return jnp.minimum(x, 0.0) - jnp.log(1.0 + jnp.exp(-jnp.abs(x)))


def _mod_kernel(c_ref, w_ref, b_ref, o_ref):
    o_ref[...] = jnp.dot(_silu(c_ref[...]), w_ref[...], preferred_element_type=F32, precision=HI) + b_ref[...]


def modulation(c_rows, w_mod, b_mod):
    n = w_mod.shape[1]
    tn = 1536
    return pl.pallas_call(
        _mod_kernel,
        out_shape=jax.ShapeDtypeStruct((c_rows.shape[0], n), F32),
        grid=(n // tn,),
        in_specs=[pl.BlockSpec(c_rows.shape, lambda j: (0, 0)),
                  pl.BlockSpec((w_mod.shape[0], tn), lambda j: (0, j)),
                  pl.BlockSpec((1, tn), lambda j: (0, j))],
        out_specs=pl.BlockSpec((c_rows.shape[0], tn), lambda j: (0, j)),
        compiler_params=_params("arbitrary"),
        name="adaln_modulation",
    )(c_rows, w_mod, b_mod.reshape(1, n))


def _modnorm(x, nw, shift, scale):
    y = x * lax.rsqrt(jnp.mean(x * x, axis=-1, keepdims=True) + EPS) * nw
    return y * (1.0 + scale) + shift


def _mod_map(mod, b):
    return (lambda bi, i: (bi, 0, 0)) if mod.shape[0] == b else (lambda bi, i: (0, 0, 0))


def _norm_proj_kernel(x_ref, nw_ref, sh_ref, sc_ref, w_ref, ssd_ref, hy_ref, ml_ref):
    h = _modnorm(x_ref[0], nw_ref[...], sh_ref[0], sc_ref[0])
    u = jnp.dot(h.astype(BF16), w_ref[...], preferred_element_type=F32)
    ssd_ref[0] = u[:, 0:SSD_COLS]
    hy_ref[0] = u[:, SSD_COLS:SSD_COLS + HY_COLS]
    ml_ref[0] = u[:, SSD_COLS + HY_COLS:]


def norm_proj(x, nw, shift, scale, w_bf16, tm=256):
    b, L, d = x.shape
    n = w_bf16.shape[1]
    row = lambda bi, i: (bi, i, 0)
    return pl.pallas_call(
        _norm_proj_kernel,
        out_shape=(jax.ShapeDtypeStruct((b, L, SSD_COLS), F32),
                   jax.ShapeDtypeStruct((b, L, HY_COLS), F32),
                   jax.ShapeDtypeStruct((b, L, ML_COLS), F32)),
        grid=(b, L // tm),
        in_specs=[pl.BlockSpec((1, tm, d), row),
                  pl.BlockSpec((1, d), lambda bi, i: (0, 0)),
                  pl.BlockSpec((1, 1, d), _mod_map(shift, b)),
                  pl.BlockSpec((1, 1, d), _mod_map(scale, b)),
                  pl.BlockSpec((d, n), lambda bi, i: (0, 0))],
        out_specs=(pl.BlockSpec((1, tm, SSD_COLS), row), pl.BlockSpec((1, tm, HY_COLS), row),
                   pl.BlockSpec((1, tm, ML_COLS), row)),
        compiler_params=_params("parallel", "arbitrary"),
        name="norm_in_proj",
    )(x, nw.reshape(1, d), shift, scale, w_bf16)


def _out_proj_kernel(ys_ref, yh_ref, ym_ref, x_ref, g_ref, w_ref, o_ref):
    y = jnp.concatenate([ys_ref[0], yh_ref[0], ym_ref[0]], axis=-1).astype(BF16)
    o_ref[0] = x_ref[0] + g_ref[0] * jnp.dot(y, w_ref[...], preferred_element_type=F32)


def out_proj(y_ssd, y_hy, y_ml, x, gate, w_bf16, tm=512):
    b, L, d = x.shape
    tm = min(tm, L)
    row = lambda bi, i: (bi, i, 0)
    return pl.pallas_call(
        _out_proj_kernel,
        out_shape=jax.ShapeDtypeStruct((b, L, d), F32),
        grid=(b, L // tm),
        in_specs=[pl.BlockSpec((1, tm, SSD_WIDTH), row), pl.BlockSpec((1, tm, HYENA_WIDTH), row),
                  pl.BlockSpec((1, tm, ML_WIDTH), row), pl.BlockSpec((1, tm, d), row),
                  pl.BlockSpec((1, 1, d), _mod_map(gate, b)),
                  pl.BlockSpec(w_bf16.shape, lambda bi, i: (0, 0))],
        out_specs=pl.BlockSpec((1, tm, d), row),
        compiler_params=_params("parallel", "arbitrary"),
        name="out_proj_residual",
    )(y_ssd, y_hy, y_ml, x, gate, w_bf16)


def _conv3(xr, prev_row, next_row, cw, cb, q):
    rid = lax.broadcasted_iota(jnp.int32, (q, 1), 0)
    x_prev = jnp.where(rid == 0, prev_row, pltpu.roll(xr, 1, axis=0))
    x_next = jnp.where(rid == q - 1, next_row, pltpu.roll(xr, q - 1, axis=0))
    return x_prev * cw[0:1] + xr * cw[1:2] + x_next * cw[2:3] + cb


def _scan_mask(q, direction):
    li = lax.broadcasted_iota(jnp.int32, (q, q), 0)
    si = lax.broadcasted_iota(jnp.int32, (q, q), 1)
    return (si <= li) if direction == 0 else (si >= li)


def _scan_specs(L, q, nc, cols, direction):
    hb = q // SUBLANE
    nrb = L // SUBLANE
    cidx = (lambda j: j) if direction == 0 else (lambda j: nc - 1 - j)
    specs = [pl.BlockSpec((1, q, cols), lambda bi, j: (bi, cidx(j), 0)),
             pl.BlockSpec((1, SUBLANE, cols), lambda bi, j: (bi, jnp.maximum(cidx(j) * hb - 1, 0), 0)),
             pl.BlockSpec((1, SUBLANE, cols), lambda bi, j: (bi, jnp.minimum((cidx(j) + 1) * hb, nrb - 1), 0))]
    return specs, cidx


def _ssd_kernel(*refs, direction, finalize, q, nc):
    if finalize:
        (u_ref, prev_ref, next_ref, yb_ref, init_ref, cw_ref, cb_ref, dtb_ref, a_ref, d_ref, nw_ref,
         y_ref, fin_ref, state_ref) = refs
    else:
        (u_ref, prev_ref, next_ref, init_ref, cw_ref, cb_ref, dtb_ref, a_ref,
         y_ref, fin_ref, state_ref) = refs
    j = pl.program_id(1)
    c = j if direction == 0 else nc - 1 - j

    @pl.when(j == 0)
    def _():
        state_ref[...] = init_ref[0]

    prev_row = jnp.where(c > 0, prev_ref[0, SUBLANE - 1:SUBLANE, SSD_XBC0:SSD_DT0], 0.0)
    next_row = jnp.where(c < nc - 1, next_ref[0, 0:1, SSD_XBC0:SSD_DT0], 0.0)
    xc = _silu(_conv3(u_ref[0, :, SSD_XBC0:SSD_DT0], prev_row, next_row, cw_ref[...], cb_ref[...], q))

    dt = _softplus(u_ref[0, :, SSD_DT0:SSD_COLS] + dtb_ref[...])
    mask = _scan_mask(q, direction)
    cum = jnp.dot(mask.astype(F32), dt * a_ref[...], preferred_element_type=F32, precision=HI)
    cum_t = cum.T
    end = q - 1 if direction == 0 else 0

    ys = []
    for g in range(SSD_GROUPS):
        b0 = SSD_WIDTH + g * SSD_STATE
        c0 = SSD_WIDTH + (SSD_GROUPS + g) * SSD_STATE
        bm = xc[:, b0:b0 + SSD_STATE].astype(BF16)
        cm = xc[:, c0:c0 + SSD_STATE].astype(BF16)
        scores = lax.dot_general(cm, bm, NT, preferred_element_type=F32)
        for h in range(g * (SSD_HEADS // SSD_GROUPS), (g + 1) * (SSD_HEADS // SSD_GROUPS)):
            hl = direction * SSD_HEADS + h
            col = cum[:, hl:hl + 1]
            seg = jnp.exp(jnp.where(mask, col - cum_t[hl:hl + 1, :], -jnp.inf))
            xdt = xc[:, h * HEAD_DIM:(h + 1) * HEAD_DIM] * dt[:, hl:hl + 1]
            y = jnp.dot((scores * seg).astype(BF16), xdt.astype(BF16), preferred_element_type=F32)
            st = state_ref[h]
            y = y + lax.dot_general(cm, st.astype(BF16), NT, preferred_element_type=F32) * jnp.exp(col)
            tot = cum[end:end + 1, hl:hl + 1]
            upd = lax.dot_general((xdt * jnp.exp(tot - col)).astype(BF16), bm, TN, preferred_element_type=F32)
            state_ref[h] = st * jnp.exp(tot) + upd
            ys.append(y)
    y_all = jnp.concatenate(ys, axis=-1)
    if finalize:
        t = (y_all + yb_ref[0] + xc[:, 0:SSD_WIDTH] * d_ref[...]) * _silu(u_ref[0, :, 0:SSD_WIDTH])
        y_all = t * lax.rsqrt(jnp.mean(t * t, axis=-1, keepdims=True) + EPS) * nw_ref[...]
    y_ref[0] = y_all

    @pl.when(j == nc - 1)
    def _():
        fin_ref[0] = state_ref[...]


def ssd_pass(u, y_other, init, sp, direction, q):
    b, L, _ = u.shape
    q = min(q, L)
    nc = L // q
    finalize = y_other is not None
    in_specs, cidx = _scan_specs(L, q, nc, SSD_COLS, direction)
    const2 = lambda bi, j: (0, 0)
    st_spec = pl.BlockSpec((1, SSD_HEADS, HEAD_DIM, SSD_STATE), lambda bi, j: (bi, 0, 0, 0))
    y_spec = pl.BlockSpec((1, q, SSD_WIDTH), lambda bi, j: (bi, cidx(j), 0))
    args = [u, u, u]
    if finalize:
        in_specs.append(y_spec)
        args.append(y_other)
    consts = [sp['cw'], sp['cb'], sp['dtb'], sp['a']] + ([sp['d'], sp['nw']] if finalize else [])
    in_specs += [st_spec] + [pl.BlockSpec(t.shape, const2) for t in consts]
    args += [init] + consts
    return pl.pallas_call(
        functools.partial(_ssd_kernel, direction=direction, finalize=finalize, q=q, nc=nc),
        out_shape=(jax.ShapeDtypeStruct((b, L, SSD_WIDTH), F32),
                   jax.ShapeDtypeStruct((b, SSD_HEADS, HEAD_DIM, SSD_STATE), F32)),
        grid=(b, nc),
        in_specs=in_specs,
        out_specs=(y_spec, st_spec),
        scratch_shapes=[pltpu.VMEM((SSD_HEADS, HEAD_DIM, SSD_STATE), F32)],
        compiler_params=_params("parallel", "arbitrary"),
        name="ssd_scan_%s" % ("fwd" if direction == 0 else "bwd"),
    )(*args)


def ssd_prepare(p):
    pad = lambda v: jnp.pad(v.reshape(1, -1), ((0, 0), (0, LANE - v.size)))
    return dict(cw=p['ssd_conv_w'], cb=p['ssd_conv_b'].reshape(1, -1),
                dtb=pad(p['ssd_dt_bias']), a=pad(-jnp.exp(p['ssd_a_log'])),
                d=jnp.repeat(p['ssd_d'], HEAD_DIM).reshape(1, -1), nw=p['ssd_norm_w'].reshape(1, -1))


def ssd_mixer(u, sp, init_f, init_b, want_y, q=256):
    yb, fin_b = ssd_pass(u, None, init_b, sp, 1, q)
    y, fin_f = ssd_pass(u, yb if want_y else None, init_f, sp, 0, q)
    return y, fin_f, fin_b


def _ml_kernel(*refs, direction, finalize, q, nc):
    if finalize:
        (u_ref, prev_ref, next_ref, hb_ref, s_init_ref, m_init_ref, cw_ref, cb_ref, gb_ref, nw_ref,
         y_ref, s_fin_ref, m_fin_ref, s_ref, m_ref) = refs
    else:
        (u_ref, prev_ref, next_ref, s_init_ref, m_init_ref, cw_ref, cb_ref, gb_ref,
         y_ref, s_fin_ref, m_fin_ref, s_ref, m_ref) = refs
    j = pl.program_id(1)
    c = j if direction == 0 else nc - 1 - j

    @pl.when(j == 0)
    def _():
        s_ref[...] = s_init_ref[0]
        m_ref[...] = m_init_ref[0]

    prev_row = jnp.where(c > 0, prev_ref[0, SUBLANE - 1:SUBLANE, 0:ML_V0], 0.0)
    next_row = jnp.where(c < nc - 1, next_ref[0, 0:1, 0:ML_V0], 0.0)
    qk = _silu(_conv3(u_ref[0, :, 0:ML_V0], prev_row, next_row, cw_ref[...], cb_ref[...], q))
    v = u_ref[0, :, ML_V0:ML_O0]

    gb = u_ref[0, :, ML_G0:ML_COLS] + gb_ref[...]
    mask = _scan_mask(q, direction)
    cum = jnp.dot(mask.astype(F32), _log_sigmoid(gb), preferred_element_type=F32, precision=HI)
    ig = pltpu.roll(gb, ML_HEADS, axis=1)
    end = q - 1 if direction == 0 else 0
    m_prev = m_ref[0:1, :]
    tot = cum[end:end + 1, :]
    w_end = tot - cum + ig
    m_loc = jnp.max(w_end, axis=0, keepdims=True)
    e_end = jnp.exp(w_end - m_loc)
    m_new = jnp.maximum(tot + m_prev, m_loc)
    a_prev = jnp.exp(tot + m_prev - m_new)
    a_loc = jnp.exp(m_loc - m_new)
    inter = cum + m_prev
    cum_t = cum.T
    ig_t = ig.T
    one_col = (lax.broadcasted_iota(jnp.int32, (q, HEAD_DIM), 1) == 0).astype(F32)

    ys = []
    for h in range(ML_HEADS):
        fl = direction * 2 * ML_HEADS + ML_HEADS + h
        qh = qk[:, h * HEAD_DIM:(h + 1) * HEAD_DIM].astype(BF16)
        kh = qk[:, ML_WIDTH + h * HEAD_DIM:ML_WIDTH + (h + 1) * HEAD_DIM] * (HEAD_DIM ** -0.5)
        v_ext = jnp.concatenate([v[:, h * HEAD_DIM:(h + 1) * HEAD_DIM], one_col], axis=-1).astype(BF16)
        log_w = jnp.where(mask, cum[:, fl:fl + 1] - cum_t[fl:fl + 1, :] + ig_t[fl:fl + 1, :], -jnp.inf)
        inter_c = inter[:, fl:fl + 1]
        m_t = jnp.maximum(inter_c, jnp.max(log_w, axis=-1, keepdims=True))
        scores = lax.dot_general(qh, kh.astype(BF16), NT, preferred_element_type=F32)
        sm = scores * jnp.exp(log_w - m_t)
        nd = jnp.dot(sm.astype(BF16), v_ext, preferred_element_type=F32)
        st = s_ref[h]
        nd = nd + jnp.exp(inter_c - m_t) * jnp.dot(qh, st.astype(BF16), preferred_element_type=F32)
        den = nd[:, HEAD_DIM:HEAD_DIM + 1]
        hid = nd[:, 0:HEAD_DIM] / jnp.maximum(jnp.abs(den), jnp.exp(-m_t))
        upd = lax.dot_general((kh * e_end[:, fl:fl + 1]).astype(BF16), v_ext, TN, preferred_element_type=F32)
        s_ref[h] = a_prev[:, fl:fl + 1] * st + a_loc[:, fl:fl + 1] * upd
        if finalize:
            hs = hid + hb_ref[0, :, h * HEAD_DIM:(h + 1) * HEAD_DIM]
            hc = hs - jnp.mean(hs, axis=-1, keepdims=True)
            hid = hc * lax.rsqrt(jnp.mean(hc * hc, axis=-1, keepdims=True) + EPS)
        ys.append(hid)
    m_ref[...] = jnp.broadcast_to(m_new, m_ref.shape)
    y_all = jnp.concatenate(ys, axis=-1)
    if finalize:
        y_all = y_all * nw_ref[...] * jax.nn.sigmoid(u_ref[0, :, ML_O0:ML_G0])
    y_ref[0] = y_all

    @pl.when(j == nc - 1)
    def _():
        s_fin_ref[0] = s_ref[...]
        m_fin_ref[0] = m_ref[...]


def ml_pass(u, h_other, init, mp, direction, q):
    b, L, _ = u.shape
    q = min(q, L)
    nc = L // q
    finalize = h_other is not None
    in_specs, cidx = _scan_specs(L, q, nc, ML_COLS, direction)
    const2 = lambda bi, j: (0, 0)
    s_spec = pl.BlockSpec((1, ML_HEADS, HEAD_DIM, LANE), lambda bi, j: (bi, 0, 0, 0))
    m_spec = pl.BlockSpec((1, SUBLANE, LANE), lambda bi, j: (bi, 0, 0))
    y_spec = pl.BlockSpec((1, q, ML_WIDTH), lambda bi, j: (bi, cidx(j), 0))
    args = [u, u, u]
    if finalize:
        in_specs.append(y_spec)
        args.append(h_other)
    consts = [mp['cw'], mp['cb'], mp['gb']] + ([mp['nw']] if finalize else [])
    in_specs += [s_spec, m_spec] + [pl.BlockSpec(t.shape, const2) for t in consts]
    args += [init[0], init[1]] + consts
    y, s_fin, m_fin = pl.pallas_call(
        functools.partial(_ml_kernel, direction=direction, finalize=finalize, q=q, nc=nc),
        out_shape=(jax.ShapeDtypeStruct((b, L, ML_WIDTH), F32),
                   jax.ShapeDtypeStruct((b, ML_HEADS, HEAD_DIM, LANE), F32),
                   jax.ShapeDtypeStruct((b, SUBLANE, LANE), F32)),
        grid=(b, nc),
        in_specs=in_specs,
        out_specs=(y_spec, s_spec, m_spec),
        scratch_shapes=[pltpu.VMEM((ML_HEADS, HEAD_DIM, LANE), F32), pltpu.VMEM((SUBLANE, LANE), F32)],
        compiler_params=_params("parallel", "arbitrary"),
        name="mlstm_scan_%s" % ("fwd" if direction == 0 else "bwd"),
    )(*args)
    return y, (s_fin, m_fin)


def ml_prepare(p):
    gb = p['ml_gate_b'].reshape(1, -1)
    return dict(cw=p['ml_conv_w'], cb=p['ml_conv_b'].reshape(1, -1),
                gb=jnp.pad(gb, ((0, 0), (0, LANE - gb.shape[1]))), nw=p['ml_norm_w'].reshape(1, -1))


def ml_mixer(u, mp, init_f, init_b, want_y, q=256):
    hb, fin_b = ml_pass(u, None, init_b, mp, 1, q)
    y, fin_f = ml_pass(u, hb if want_y else None, init_f, mp, 0, q)
    return y, fin_f, fin_b


FFT_L = 4096
FFT_N = 2 * FFT_L
FFT_N2 = 128
FFT_N1 = FFT_N // FFT_N2
FFT_N1H = FFT_L // FFT_N2
FFT_K1 = FFT_N1 // 2 + 1
FFT_R = 80


def _fft_tables():
    n2 = np.arange(FFT_N2)[:, None, None]
    k1 = np.arange(FFT_K1)[None, :, None]
    n1 = np.arange(FFT_N1H)[None, None, :]
    th = 2 * np.pi * (((FFT_N2 * n1 + n2) * k1) % FFT_N) / FFT_N
    f1 = np.zeros((FFT_N2, FFT_R, FFT_N1H))
    f1[:, 0:2 * FFT_K1:2, :] = np.cos(th)
    f1[:, 1:2 * FFT_K1:2, :] = -np.sin(th)
    wgt = np.where((np.arange(FFT_K1) == 0) | (np.arange(FFT_K1) == FFT_N1 // 2), 1.0, 2.0)[None, :, None] / FFT_N
    g1 = np.zeros((FFT_N2, FFT_N1H, FFT_R))
    g1[:, :, 0:2 * FFT_K1:2] = np.transpose(wgt * np.cos(th), (0, 2, 1))
    g1[:, :, 1:2 * FFT_K1:2] = np.transpose(-wgt * np.sin(th), (0, 2, 1))
    ph = 2 * np.pi * ((np.arange(FFT_N2)[:, None] * np.arange(FFT_N2)[None, :]) % FFT_N2) / FFT_N2
    c, s = np.cos(ph), np.sin(ph)
    f2 = np.block([[c, s], [-s, c]])
    as_bf = lambda a: jnp.asarray(a, F32).astype(BF16)
    return as_bf(f1), as_bf(f2), as_bf(f2.T), as_bf(g1)


def _sld(ref, n2, count):
    rows = pl.ds(n2, count, stride=FFT_N2)
    return jnp.concatenate([ref[0, rows, :], ref[1, rows, :]], axis=-1)


def _sst(ref, n2, count, val):
    rows = pl.ds(n2, count, stride=FFT_N2)
    ref[0, rows, :] = val[:, 0:LANE]
    ref[1, rows, :] = val[:, LANE:2 * LANE]


def _dense_ld(ref, r0, count):
    return jnp.concatenate([ref[0, pl.ds(r0, count), :], ref[1, pl.ds(r0, count), :]], axis=-1)


def _dense_st(ref, r0, count, val):
    ref[0, pl.ds(r0, count), :] = val[:, 0:LANE]
    ref[1, pl.ds(r0, count), :] = val[:, LANE:2 * LANE]


def _fft_stage1(z_ref, a_ref, f1_ref):
    def body(n2, carry):
        xs = _sld(z_ref, n2, FFT_N1H).astype(BF16)
        _sst(a_ref, n2, FFT_R, jnp.dot(f1_ref[n2], xs, preferred_element_type=F32))
        return carry
    lax.fori_loop(0, FFT_N2, body, 0)


def _load_slab(a_ref, k1):
    return _dense_ld(a_ref, pl.multiple_of(k1 * 2 * FFT_N2, 2 * FFT_N2), 2 * FFT_N2)


def _spectrum_kernel(x_ref, f1_ref, f2_ref, o_ref, z_ref, a_ref):
    z_ref[0] = x_ref[0, :, 0:LANE]
    z_ref[1] = x_ref[0, :, LANE:2 * LANE]
    _fft_stage1(z_ref, a_ref, f1_ref)

    def body(k1, carry):
        slab = _load_slab(a_ref, k1).astype(BF16)
        o_ref[0, k1] = jnp.dot(f2_ref[...], slab, preferred_element_type=F32)
        return carry
    lax.fori_loop(0, FFT_K1, body, 0)


def spectrum(sig):
    s, L, c = sig.shape
    f1, f2, _, _ = _fft_tables()
    one = pl.Buffered(1)
    return pl.pallas_call(
        _spectrum_kernel,
        out_shape=jax.ShapeDtypeStruct((s, FFT_K1, 2 * FFT_N2, c), F32),
        grid=(s,),
        in_specs=[pl.BlockSpec((1, L, c), lambda i: (i, 0, 0)),
                  pl.BlockSpec(f1.shape, lambda i: (0, 0, 0), pipeline_mode=one),
                  pl.BlockSpec(f2.shape, lambda i: (0, 0), pipeline_mode=one)],
        out_specs=pl.BlockSpec((1, FFT_K1, 2 * FFT_N2, c), lambda i: (i, 0, 0, 0)),
        scratch_shapes=[pltpu.VMEM((2, L, LANE), F32), pltpu.VMEM((2, FFT_R * FFT_N2, LANE), F32)],
        compiler_params=pltpu.CompilerParams(dimension_semantics=("arbitrary",), vmem_limit_bytes=VMEM_LIMIT_HY),
        name="hyena_filter_spectrum",
    )(sig, f1, f2)


def _conv3_rows(src, dst, cw, cb, L, rows=256):
    nchunk = L // rows

    def body(i, carry):
        r0 = pl.multiple_of(i * rows, rows)
        prev_row = jnp.where(i > 0, src[pl.ds(jnp.maximum(r0 - 1, 0), 1), :], 0.0)
        next_row = jnp.where(i < nchunk - 1, src[pl.ds(jnp.minimum(r0 + rows, L - 1), 1), :], 0.0)
        _dense_st(dst, r0, rows, _conv3(src[pl.ds(r0, rows), :], prev_row, next_row, cw, cb, rows))
        return carry
    lax.fori_loop(0, nchunk, body, 0)


def _hyena_kernel(v_ref, g_ref, h_ref, cwv_ref, cbv_ref, cwg_ref, cbg_ref, skip_ref,
                  f1_ref, f2_ref, f2t_ref, g1_ref, o_ref, z_ref, gc_ref, a_ref):
    order = pl.program_id(1)

    @pl.when(order == 0)
    def _():
        _conv3_rows(v_ref.at[0], z_ref, cwv_ref[...], cbv_ref[...], FFT_L)

    _conv3_rows(g_ref.at[0], gc_ref, cwg_ref[0], cbg_ref[0], FFT_L)
    _fft_stage1(z_ref, a_ref, f1_ref)

    def mid(k1, carry):
        x = jnp.dot(f2_ref[...], _load_slab(a_ref, k1).astype(BF16), preferred_element_type=F32)
        h = h_ref[0, k1].astype(F32)
        xr, xi, hr, hi = x[:FFT_N2], x[FFT_N2:], h[:FFT_N2], h[FFT_N2:]
        y = jnp.concatenate([xr * hr - xi * hi, xr * hi + xi * hr], axis=0).astype(BF16)
        r0 = pl.multiple_of(k1 * 2 * FFT_N2, 2 * FFT_N2)
        _dense_st(a_ref, r0, 2 * FFT_N2, jnp.dot(f2t_ref[...], y, preferred_element_type=F32))
        return carry
    lax.fori_loop(0, FFT_K1, mid, 0)

    skip = skip_ref[0]

    def last(n2, carry):
        bs = _sld(a_ref, n2, FFT_R).astype(BF16)
        y = jnp.dot(g1_ref[n2], bs, preferred_element_type=F32)
        _sst(z_ref, n2, FFT_N1H, _sld(gc_ref, n2, FFT_N1H) * (y + skip * _sld(z_ref, n2, FFT_N1H)))
        return carry
    lax.fori_loop(0, FFT_N2, last, 0)

    @pl.when(order == 1)
    def _():
        o_ref[0, :, 0:LANE] = z_ref[0]
        o_ref[0, :, LANE:2 * LANE] = z_ref[1]


def hyena_long(u, h_spec, cw, cb, skip):
    b, L, _ = u.shape
    c = HYENA_WIDTH
    f1, f2, f2t, g1 = _fft_tables()
    one = pl.Buffered(1)
    cw3 = cw.reshape(3, 3, c).transpose(1, 0, 2)
    cb3 = cb.reshape(3, 1, c)
    return pl.pallas_call(
        _hyena_kernel,
        out_shape=jax.ShapeDtypeStruct((b, L, c), F32),
        grid=(b, 2),
        in_specs=[pl.BlockSpec((1, L, c), lambda bi, o: (bi, 0, 0), pipeline_mode=one),
                  pl.BlockSpec((1, L, c), lambda bi, o: (bi, 0, 1 + o)),
                  pl.BlockSpec((1, FFT_K1, 2 * FFT_N2, c), lambda bi, o: (o, 0, 0, 0)),
                  pl.BlockSpec((3, c), lambda bi, o: (0, 0)),
                  pl.BlockSpec((1, c), lambda bi, o: (0, 0)),
                  pl.BlockSpec((1, 3, c), lambda bi, o: (1 + o, 0, 0)),
                  pl.BlockSpec((1, 1, c), lambda bi, o: (1 + o, 0, 0)),
                  pl.BlockSpec((1, 1, c), lambda bi, o: (o, 0, 0)),
                  pl.BlockSpec(f1.shape, lambda bi, o: (0, 0, 0), pipeline_mode=one),
                  pl.BlockSpec(f2.shape, lambda bi, o: (0, 0), pipeline_mode=one),
                  pl.BlockSpec(f2t.shape, lambda bi, o: (0, 0), pipeline_mode=one),
                  pl.BlockSpec(g1.shape, lambda bi, o: (0, 0, 0), pipeline_mode=one)],
        out_specs=pl.BlockSpec((1, L, c), lambda bi, o: (bi, 0, 0)),
        scratch_shapes=[pltpu.VMEM((2, L, LANE), F32), pltpu.VMEM((2, L, LANE), F32),
                        pltpu.VMEM((2, FFT_R * FFT_N2, LANE), F32)],
        compiler_params=pltpu.CompilerParams(dimension_semantics=("parallel", "arbitrary"),
                                             vmem_limit_bytes=VMEM_LIMIT_HY),
        name="hyena_long_conv",
    )(u, u, h_spec, cw3[0], cb3[0], cw3, cb3, skip.reshape(2, 1, c), f1, f2, f2t, g1)


def hyena_filter_spectra(filt):
    sigs = []
    for order in range(HYENA_ORDER):
        h_fwd, h_bwd = filt[:, 0, order], filt[:, 1, order]
        second = jnp.concatenate([jnp.zeros((1, h_fwd.shape[1]), F32), h_bwd[1:][::-1]], axis=0)
        norm = jnp.sum(jnp.abs(h_fwd), axis=0, keepdims=True) + jnp.sum(jnp.abs(second), axis=0, keepdims=True)
        sigs += [h_fwd / norm, second / norm]
    spec = spectrum(jnp.stack(sigs))
    sign = jnp.where(jnp.arange(FFT_K1) % 2 == 0, 1.0, -1.0)[None, :, None, None]
    spec = spec.reshape(HYENA_ORDER, 2, *spec.shape[1:])
    return (spec[:, 0] + sign * spec[:, 1]).astype(BF16)


def _norm_router_kernel(x_ref, nw_ref, sh_ref, sc_ref, wr_ref, br_ref, h_ref, lg_ref):
    h = _modnorm(x_ref[0], nw_ref[...], sh_ref[0], sc_ref[0])
    h_ref[0] = h
    lg = jnp.dot(h, wr_ref[...], preferred_element_type=F32, precision=HI) + br_ref[...]
    lane = lax.broadcasted_iota(jnp.int32, lg.shape, 1)
    first = lambda hit: jnp.min(jnp.where(hit, lane, ROUTER_COLS), axis=-1, keepdims=True)
    gl = jnp.where(lane < N_GROUPS, lg, -jnp.inf)
    gmax = jnp.max(gl, axis=-1, keepdims=True)
    grp = first(gl == gmax)
    grp_p = 1.0 / jnp.sum(jnp.exp(gl - gmax), axis=-1, keepdims=True)
    lo = N_GROUPS + grp * EXPERTS_PER_GROUP
    el = jnp.where((lane >= lo) & (lane < lo + EXPERTS_PER_GROUP), lg, -jnp.inf)
    e1 = jnp.max(el, axis=-1, keepdims=True)
    i1 = first(el == e1)
    el2 = jnp.where(lane == i1, -jnp.inf, el)
    e2 = jnp.max(el2, axis=-1, keepdims=True)
    i2 = first(el2 == e2)
    r = jnp.exp(e2 - e1)
    w1 = grp_p / (1.0 + r)
    w2 = w1 * r
    vals = [(i1 - N_GROUPS).astype(F32), (i2 - N_GROUPS).astype(F32), w1, w2]
    out = jnp.zeros(lg.shape, F32)
    for k, val in enumerate(vals):
        out = jnp.where(lane == k, val, out)
    lg_ref[0] = out


def norm_router(x, nw, shift, scale, w_router, b_router, tm=512):
    b, L, d = x.shape
    tm = min(tm, L)
    row = lambda bi, i: (bi, i, 0)
    return pl.pallas_call(
        _norm_router_kernel,
        out_shape=(jax.ShapeDtypeStruct((b, L, d), F32),
                   jax.ShapeDtypeStruct((b, L, ROUTER_COLS), F32)),
        grid=(b, L // tm),
        in_specs=[pl.BlockSpec((1, tm, d), row),
                  pl.BlockSpec((1, d), lambda bi, i: (0, 0)),
                  pl.BlockSpec((1, 1, d), _mod_map(shift, b)),
                  pl.BlockSpec((1, 1, d), _mod_map(scale, b)),
                  pl.BlockSpec((d, ROUTER_COLS), lambda bi, i: (0, 0)),
                  pl.BlockSpec((1, ROUTER_COLS), lambda bi, i: (0, 0))],
        out_specs=(pl.BlockSpec((1, tm, d), row), pl.BlockSpec((1, tm, ROUTER_COLS), row)),
        compiler_params=_params("parallel", "arbitrary"),
        name="moe_norm_router",
    )(x, nw.reshape(1, d), shift, scale, w_router, b_router)


def _expert_ffn_kernel(te_ref, tv_ref, x_ref, rw_ref, wg_ref, wu_ref, wd_ref, o_ref):
    i = pl.program_id(0)

    @pl.when(tv_ref[i] > 0)
    def _():
        x = x_ref[...].astype(BF16)
        g = jnp.dot(x, wg_ref[0].astype(BF16), preferred_element_type=F32)
        u = jnp.dot(x, wu_ref[0].astype(BF16), preferred_element_type=F32)
        hid = _silu(g) * u * rw_ref[...]
        o_ref[...] = jnp.dot(hid.astype(BF16), wd_ref[0].astype(BF16), preferred_element_type=F32)

    @pl.when(tv_ref[i] == 0)
    def _():
        o_ref[...] = jnp.zeros_like(o_ref)


def expert_ffn(x_sorted, row_w, tile_expert, tile_valid, w_gate, w_up, w_down, tm):
    r, d = x_sorted.shape
    f = w_gate.shape[-1]
    grid_spec = pltpu.PrefetchScalarGridSpec(
        num_scalar_prefetch=2,
        grid=(r // tm,),
        in_specs=[pl.BlockSpec((tm, d), lambda i, te, tv: (i, 0)),
                  pl.BlockSpec((tm, 1), lambda i, te, tv: (i, 0)),
                  pl.BlockSpec((1, d, f), lambda i, te, tv: (te[i], 0, 0)),
                  pl.BlockSpec((1, d, f), lambda i, te, tv: (te[i], 0, 0)),
                  pl.BlockSpec((1, f, d), lambda i, te, tv: (te[i], 0, 0))],
        out_specs=pl.BlockSpec((tm, d), lambda i, te, tv: (i, 0)),
    )
    return pl.pallas_call(
        _expert_ffn_kernel,
        out_shape=jax.ShapeDtypeStruct((r, d), F32),
        grid_spec=grid_spec,
        compiler_params=_params("arbitrary"),
        name="moe_expert_ffn",
    )(tile_expert, tile_valid, x_sorted, row_w, w_gate, w_up, w_down)


def moe_apply(h_tokens, routed, w_gate, w_up, w_down, tm=256):
    t, d = h_tokens.shape
    e_idx, e_w = routed[:, 0:2].astype(jnp.int32), routed[:, 2:4]
    flat_e = e_idx.reshape(-1).astype(jnp.int32)
    n_pairs = 2 * t
    sorted_e, order = lax.sort((flat_e, jnp.arange(n_pairs, dtype=jnp.int32)), num_keys=1, is_stable=True)
    seg_end = jnp.searchsorted(sorted_e, jnp.arange(N_EXPERTS, dtype=jnp.int32), side="right").astype(jnp.int32)
    counts = seg_end - jnp.concatenate([jnp.zeros((1,), jnp.int32), seg_end[:-1]])
    seg_start = seg_end - counts
    padded = (counts + tm - 1) // tm * tm
    pad_end = jnp.cumsum(padded)
    pad_start = pad_end - padded
    n_rows = n_pairs + N_EXPERTS * tm
    tile_start = jnp.arange(n_rows // tm, dtype=jnp.int32) * tm
    tile_expert = jnp.minimum(jnp.searchsorted(pad_end, tile_start, side="right"), N_EXPERTS - 1).astype(jnp.int32)
    tile_valid = (tile_start < pad_end[-1]).astype(jnp.int32)
    off = (tile_start - pad_start[tile_expert])[:, None] + jnp.arange(tm, dtype=jnp.int32)[None, :]
    valid = (off < counts[tile_expert][:, None]) & (tile_valid[:, None] > 0)
    pair = jnp.where(valid, seg_start[tile_expert][:, None] + off, 0).reshape(-1)
    src = order[pair]
    row_token = src // 2
    row_w = jnp.where(valid.reshape(-1), e_w.reshape(-1)[src], 0.0)
    dest = pad_start[sorted_e] + (jnp.arange(n_pairs, dtype=jnp.int32) - seg_start[sorted_e])
    _, pos = lax.sort((order, dest), num_keys=1)
    pos = pos.reshape(t, 2)
    x_sorted = h_tokens[row_token]
    wg = w_gate.reshape(N_EXPERTS, d, EXPERT_HIDDEN)
    wu = w_up.reshape(N_EXPERTS, d, EXPERT_HIDDEN)
    wd = w_down.reshape(N_EXPERTS, EXPERT_HIDDEN, d)
    y_sorted = expert_ffn(x_sorted, row_w[:, None], tile_expert, tile_valid, wg, wu, wd, tm)
    return y_sorted[pos[:, 0]] + y_sorted[pos[:, 1]]


def _rmsnorm_kernel(x_ref, w_ref, o_ref):
    x = x_ref[...]
    o_ref[...] = x * lax.rsqrt(jnp.mean(x * x, axis=-1, keepdims=True) + EPS) * w_ref[...]


def final_norm(x, w, tm=1024):
    t, d = x.shape
    return pl.pallas_call(
        _rmsnorm_kernel,
        out_shape=jax.ShapeDtypeStruct((t, d), F32),
        grid=(t // tm,),
        in_specs=[pl.BlockSpec((tm, d), lambda i: (i, 0)), pl.BlockSpec((1, d), lambda i: (0, 0))],
        out_specs=pl.BlockSpec((tm, d), lambda i: (i, 0)),
        compiler_params=_params("parallel"),
        name="final_rmsnorm",
    )(x, w.reshape(1, d))


def _short_conv(x, w, b):
    L = x.shape[1]
    pad = w.shape[0] // 2
    xp = jnp.pad(x, ((0, 0), (pad, pad), (0, 0)))
    return sum(xp[:, k:k + L] * w[k] for k in range(w.shape[0])) + b


def _to_col_major(u):
    b, L, ch = u.shape
    rows = L // GRID_W
    return u.reshape(b, rows, GRID_W, ch).transpose(0, 2, 1, 3).reshape(b, L, ch)


def _to_row_major(u):
    b, L, ch = u.shape
    rows = L // GRID_W
    return u.reshape(b, GRID_W, rows, ch).transpose(0, 2, 1, 3).reshape(b, L, ch)


def _hyena_filters(L, p):
    t = jnp.arange(L, dtype=F32)
    t_unit = t / float(max(L - 1, 1))
    bands = jnp.linspace(1e-4, HYENA_BANDS - 1, HYENA_BANDS, dtype=F32)
    ang = (2 * math.pi / L) * t[:, None] * bands[None, :]
    feats = jnp.concatenate([t_unit[:, None], jnp.cos(ang), -jnp.sin(ang)], axis=-1)
    freq = p['hy_freq']
    hdn = jnp.sin(freq * (jnp.dot(feats, p['hy_pos_w1'], precision=HI) + p['hy_pos_b1']))
    hdn = jnp.sin(freq * (jnp.dot(hdn, p['hy_pos_w2'], precision=HI) + p['hy_pos_b2']))
    filt = jnp.dot(hdn, p['hy_pos_w3'], precision=HI).reshape(L, 2, HYENA_ORDER, HYENA_WIDTH)
    window = jnp.exp(-t_unit[:, None, None, None] * jnp.abs(p['hy_decay']))
    return filt * window


def _bidir_long_conv(z, h_fwd, h_bwd):
    L, ch = h_fwd.shape
    two_sided = jnp.concatenate([h_fwd, jnp.zeros((1, ch), h_fwd.dtype), h_bwd[1:][::-1]], axis=0)
    two_sided = two_sided / jnp.sum(jnp.abs(two_sided), axis=0, keepdims=True)
    zf = jnp.fft.rfft(z, n=2 * L, axis=1)
    hf = jnp.fft.rfft(two_sided, n=2 * L, axis=0)
    return jnp.fft.irfft(zf * hf[None], n=2 * L, axis=1)[:, :L]


def _hyena_mixer(u, p):
    uc = _short_conv(u, p['hy_conv_w'], p['hy_conv_b'])
    v, *gates = jnp.split(uc, HYENA_ORDER + 1, axis=-1)
    filt = _hyena_filters(u.shape[1], p)
    z = v
    for order, gate in enumerate(gates):
        z = gate * (_bidir_long_conv(z, filt[:, 0, order], filt[:, 1, order]) + p['hy_skip'][order] * z)
    return z


def _regroup_in_weight(w_in):
    sizes = (SSD_WIDTH, SSD_CONV_CH, 2 * SSD_HEADS, HY_COLS, 2 * ML_WIDTH, ML_WIDTH, ML_WIDTH, 4 * ML_HEADS)
    parts, s = [], 0
    for n in sizes:
        parts.append(jnp.pad(w_in[:, s:s + n], ((0, 0), (0, -n % LANE))))
        s += n
    return jnp.concatenate(parts, axis=1).astype(BF16)


def kernel(x, c, ctx, c_ctx, w_mod, b_mod, norm1_w, norm2_w, w_in, w_out, ssd_conv_w, ssd_conv_b, ssd_dt_bias, ssd_a_log, ssd_d, ssd_norm_w, hy_conv_w, hy_conv_b, hy_pos_w1, hy_pos_b1, hy_pos_w2, hy_pos_b2, hy_pos_w3, hy_freq, hy_decay, hy_skip, ml_conv_w, ml_conv_b, ml_gate_b, ml_norm_w, grp_router_w, grp_router_b, exp_router_w, exp_router_b, moe_w_gate, moe_w_up, moe_w_down, final_norm_w):
    layer_params = dict(
        ssd_conv_w=ssd_conv_w, ssd_conv_b=ssd_conv_b, ssd_dt_bias=ssd_dt_bias, ssd_a_log=ssd_a_log,
        ssd_d=ssd_d, ssd_norm_w=ssd_norm_w, hy_conv_w=hy_conv_w, hy_conv_b=hy_conv_b,
        hy_pos_w1=hy_pos_w1, hy_pos_b1=hy_pos_b1, hy_pos_w2=hy_pos_w2, hy_pos_b2=hy_pos_b2,
        hy_pos_w3=hy_pos_w3, hy_freq=hy_freq, hy_decay=hy_decay, hy_skip=hy_skip,
        ml_conv_w=ml_conv_w, ml_conv_b=ml_conv_b, ml_gate_b=ml_gate_b, ml_norm_w=ml_norm_w)
    bsz, seq, d = x.shape
    n_ctx = ctx.shape[1]
    xl, xc = x, ctx
    ssd0 = jnp.zeros((bsz, SSD_HEADS, HEAD_DIM, SSD_STATE), F32)
    ml0 = (jnp.zeros((bsz, ML_HEADS, HEAD_DIM, LANE), F32), jnp.zeros((bsz, SUBLANE, LANE), F32))
    c_rows = jnp.concatenate([c, c_ctx[None, :], jnp.zeros((SUBLANE - bsz - 1, d), F32)], axis=0)
    for i in range(DEPTH):
        last = i == DEPTH - 1
        p = {name: arr[i] for name, arr in layer_params.items()}
        sp, mp = ssd_prepare(p), ml_prepare(p)
        mod = modulation(c_rows, w_mod[i], b_mod[i]).reshape(SUBLANE, N_MOD, 1, d)
        mod_l = [mod[:bsz, k] for k in range(N_MOD)]
        mod_c = [mod[bsz:bsz + 1, k] for k in range(N_MOD)]
        w_in_p = _regroup_in_weight(w_in[i])
        w_out_b = w_out[i].astype(BF16)
        w_router = jnp.pad(jnp.concatenate([grp_router_w[i], exp_router_w[i]], axis=1),
                           ((0, 0), (0, ROUTER_COLS - N_GROUPS - N_EXPERTS)))
        b_router = jnp.pad(jnp.concatenate([grp_router_b[i], exp_router_b[i]]),
                           (0, ROUTER_COLS - N_GROUPS - N_EXPERTS)).reshape(1, ROUTER_COLS)

        uc_ssd, uc_hy, uc_ml = norm_proj(xc, norm1_w[i], mod_c[0], mod_c[1], w_in_p)
        yc_ssd, ssd_f, ssd_b = ssd_mixer(uc_ssd, sp, ssd0, ssd0, not last)
        yc_ml, ml_f, ml_b = ml_mixer(uc_ml, mp, ml0, ml0, not last)
        col_major = i % 2 == 1
        xin = _to_col_major(xl) if col_major else xl
        ul_ssd, ul_hy, ul_ml = norm_proj(xin, norm1_w[i], mod_l[0], mod_l[1], w_in_p)
        yl_ssd, _, _ = ssd_mixer(ul_ssd, sp, ssd_f, ssd_b, True)
        yl_ml, _, _ = ml_mixer(ul_ml, mp, ml_f, ml_b, True)
        h_spec = hyena_filter_spectra(_hyena_filters(seq, p))
        yl_hy = hyena_long(ul_hy, h_spec, p['hy_conv_w'], p['hy_conv_b'], p['hy_skip'])
        if col_major:
            yl_ssd, yl_hy, yl_ml = _to_row_major(yl_ssd), _to_row_major(yl_hy), _to_row_major(yl_ml)
        xl = out_proj(yl_ssd, yl_hy, yl_ml, xl, mod_l[2], w_out_b)
        hl, lgl = norm_router(xl, norm2_w[i], mod_l[3], mod_l[4], w_router, b_router)
        if not last:
            yc_hy = _hyena_mixer(uc_hy, p)
            xc = out_proj(yc_ssd, yc_hy, yc_ml, xc, mod_c[2], w_out_b)
            hc, lgc = norm_router(xc, norm2_w[i], mod_c[3], mod_c[4], w_router, b_router)
            h_all = jnp.concatenate([hl.reshape(-1, d), hc.reshape(-1, d)], axis=0)
            lg_all = jnp.concatenate([lgl.reshape(-1, ROUTER_COLS), lgc.reshape(-1, ROUTER_COLS)], axis=0)
        else:
            h_all, lg_all = hl.reshape(-1, d), lgl.reshape(-1, ROUTER_COLS)
        moe = moe_apply(h_all, lg_all, moe_w_gate[i], moe_w_up[i], moe_w_down[i])
        xl = xl + mod_l[5] * moe[:bsz * seq].reshape(bsz, seq, d)
        if not last:
            xc = xc + mod_c[5] * moe[bsz * seq:].reshape(bsz, n_ctx, d)
    return final_norm(xl.reshape(-1, d), final_norm_w).reshape(bsz, seq, d)
```

```python
import functools
import math

import jax
import jax.numpy as jnp
import numpy as np
from jax import lax
from jax.experimental import pallas as pl
from jax.experimental.pallas import tpu as pltpu

D_MODEL = 1024
DEPTH = 2
GRID_W = 64
HEAD_DIM = 64
SSD_WIDTH = 384
SSD_HEADS = SSD_WIDTH // HEAD_DIM
SSD_GROUPS = 2
SSD_STATE = 64
HYENA_WIDTH = 256
HYENA_ORDER = 2
HYENA_BANDS = 16
ML_WIDTH = 384
ML_HEADS = ML_WIDTH // HEAD_DIM
N_GROUPS = 4
EXPERTS_PER_GROUP = 8
N_EXPERTS = N_GROUPS * EXPERTS_PER_GROUP
EXPERT_HIDDEN = 256
N_MOD = 6
EPS = 1e-6

LANE = 128
SUBLANE = 8
VMEM_LIMIT = 48 * 1024 * 1024
VMEM_LIMIT_HY = 56 * 1024 * 1024

SSD_CONV_CH = SSD_WIDTH + 2 * SSD_GROUPS * SSD_STATE
SSD_XBC0 = SSD_WIDTH
SSD_DT0 = SSD_XBC0 + SSD_CONV_CH
SSD_COLS = SSD_DT0 + LANE
HY_COLS = (HYENA_ORDER + 1) * HYENA_WIDTH
ML_V0 = 2 * ML_WIDTH
ML_O0 = ML_V0 + ML_WIDTH
ML_G0 = ML_O0 + ML_WIDTH
ML_COLS = ML_G0 + LANE
ROUTER_COLS = LANE

F32 = jnp.float32
BF16 = jnp.bfloat16
HI = lax.Precision.HIGHEST
NT = (((1,), (1,)), ((), ()))
TN = (((0,), (0,)), ((), ()))


def _params(*sem):
    return pltpu.CompilerParams(dimension_semantics=sem, vmem_limit_bytes=VMEM_LIMIT)


def _silu(x):
    return x * jax.nn.sigmoid(x)


def _softplus(x):
    return jnp.maximum(x, 0.0) + jnp.log(1.0 + jnp.exp(-jnp.abs(x)))


def _log_sigmoid(x):
    return jnp.minimum(x, 0.0) - jnp.log(1.0 + jnp.exp(-jnp.abs(x)))


def _mod_kernel(c_ref, w_ref, b_ref, o_ref):
    o_ref[...] = jnp.dot(_silu(c_ref[...]), w_ref[...], preferred_element_type=F32, precision=HI) + b_ref[...]


def modulation(c_rows, w_mod, b_mod):
    n = w_mod.shape[1]
    tn = 1536
    return pl.pallas_call(
        _mod_kernel,
        out_shape=jax.ShapeDtypeStruct((c_rows.shape[0], n), F32),
        grid=(n // tn,),
        in_specs=[pl.BlockSpec(c_rows.shape, lambda j: (0, 0)),
                  pl.BlockSpec((w_mod.shape[0], tn), lambda j: (0, j)),
                  pl.BlockSpec((1, tn), lambda j: (0, j))],
        out_specs=pl.BlockSpec((c_rows.shape[0], tn), lambda j: (0, j)),
        compiler_params=_params("arbitrary"),
        name="adaln_modulation",
    )(c_rows, w_mod, b_mod.reshape(1, n))


def _modnorm(x, nw, shift, scale):
    y = x * lax.rsqrt(jnp.mean(x * x, axis=-1, keepdims=True) + EPS) * nw
    return y * (1.0 + scale) + shift


def _mod_map(mod, b):
    return (lambda bi, i: (bi, 0, 0)) if mod.shape[0] == b else (lambda bi, i: (0, 0, 0))


def _norm_proj_kernel(x_ref, nw_ref, sh_ref, sc_ref, w_ref, ssd_ref, hy_ref, ml_ref):
    h = _modnorm(x_ref[0], nw_ref[...], sh_ref[0], sc_ref[0])
    u = jnp.dot(h.astype(BF16), w_ref[...], preferred_element_type=F32)
    ssd_ref[0] = u[:, 0:SSD_COLS]
    hy_ref[0] = u[:, SSD_COLS:SSD_COLS + HY_COLS]
    ml_ref[0] = u[:, SSD_COLS + HY_COLS:]


def norm_proj(x, nw, shift, scale, w_bf16, tm=256):
    b, L, d = x.shape
    n = w_bf16.shape[1]
    row = lambda bi, i: (bi, i, 0)
    return pl.pallas_call(
        _norm_proj_kernel,
        out_shape=(jax.ShapeDtypeStruct((b, L, SSD_COLS), F32),
                   jax.ShapeDtypeStruct((b, L, HY_COLS), F32),
                   jax.ShapeDtypeStruct((b, L, ML_COLS), F32)),
        grid=(b, L // tm),
        in_specs=[pl.BlockSpec((1, tm, d), row),
                  pl.BlockSpec((1, d), lambda bi, i: (0, 0)),
                  pl.BlockSpec((1, 1, d), _mod_map(shift, b)),
                  pl.BlockSpec((1, 1, d), _mod_map(scale, b)),
                  pl.BlockSpec((d, n), lambda bi, i: (0, 0))],
        out_specs=(pl.BlockSpec((1, tm, SSD_COLS), row), pl.BlockSpec((1, tm, HY_COLS), row),
                   pl.BlockSpec((1, tm, ML_COLS), row)),
        compiler_params=_params("parallel", "arbitrary"),
        name="norm_in_proj",
    )(x, nw.reshape(1, d), shift, scale, w_bf16)


def _out_proj_kernel(ys_ref, yh_ref, ym_ref, x_ref, g_ref, w_ref, o_ref):
    y = jnp.concatenate([ys_ref[0], yh_ref[0], ym_ref[0]], axis=-1).astype(BF16)
    o_ref[0] = x_ref[0] + g_ref[0] * jnp.dot(y, w_ref[...], preferred_element_type=F32)


def out_proj(y_ssd, y_hy, y_ml, x, gate, w_bf16, tm=512):
    b, L, d = x.shape
    tm = min(tm, L)
    row = lambda bi, i: (bi, i, 0)
    return pl.pallas_call(
        _out_proj_kernel,
        out_shape=jax.ShapeDtypeStruct((b, L, d), F32),
        grid=(b, L // tm),
        in_specs=[pl.BlockSpec((1, tm, SSD_WIDTH), row), pl.BlockSpec((1, tm, HYENA_WIDTH), row),
                  pl.BlockSpec((1, tm, ML_WIDTH), row), pl.BlockSpec((1, tm, d), row),
                  pl.BlockSpec((1, 1, d), _mod_map(gate, b)),
                  pl.BlockSpec(w_bf16.shape, lambda bi, i: (0, 0))],
        out_specs=pl.BlockSpec((1, tm, d), row),
        compiler_params=_params("parallel", "arbitrary"),
        name="out_proj_residual",
    )(y_ssd, y_hy, y_ml, x, gate, w_bf16)


def _conv3(xr, prev_row, next_row, cw, cb, q):
    rid = lax.broadcasted_iota(jnp.int32, (q, 1), 0)
    x_prev = jnp.where(rid == 0, prev_row, pltpu.roll(xr, 1, axis=0))
    x_next = jnp.where(rid == q - 1, next_row, pltpu.roll(xr, q - 1, axis=0))
    return x_prev * cw[0:1] + xr * cw[1:2] + x_next * cw[2:3] + cb


def _scan_mask(q, direction):
    li = lax.broadcasted_iota(jnp.int32, (q, q), 0)
    si = lax.broadcasted_iota(jnp.int32, (q, q), 1)
    return (si <= li) if direction == 0 else (si >= li)


def _running_max(x, direction, q):
    rid = lax.broadcasted_iota(jnp.int32, (q, 1), 0)
    s = 1
    while s < q:
        if direction == 0:
            x = jnp.where(rid >= s, jnp.maximum(x, pltpu.roll(x, s, axis=0)), x)
        else:
            x = jnp.where(rid < q - s, jnp.maximum(x, pltpu.roll(x, q - s, axis=0)), x)
        s *= 2
    return x


def _scan_specs(L, q, nc, cols, direction):
    hb = q // SUBLANE
    nrb = L // SUBLANE
    cidx = (lambda j: j) if direction == 0 else (lambda j: nc - 1 - j)
    specs = [pl.BlockSpec((1, q, cols), lambda bi, j: (bi, cidx(j), 0)),
             pl.BlockSpec((1, SUBLANE, cols), lambda bi, j: (bi, jnp.maximum(cidx(j) * hb - 1, 0), 0)),
             pl.BlockSpec((1, SUBLANE, cols), lambda bi, j: (bi, jnp.minimum((cidx(j) + 1) * hb, nrb - 1), 0))]
    return specs, cidx


def _ssd_kernel(*refs, direction, finalize, q, nc):
    if finalize:
        (u_ref, prev_ref, next_ref, yb_ref, init_ref, cw_ref, cb_ref, dtb_ref, a_ref, d_ref, nw_ref,
         y_ref, fin_ref, state_ref) = refs
    else:
        (u_ref, prev_ref, next_ref, init_ref, cw_ref, cb_ref, dtb_ref, a_ref,
         y_ref, fin_ref, state_ref) = refs
    j = pl.program_id(1)
    c = j if direction == 0 else nc - 1 - j

    @pl.when(j == 0)
    def _():
        state_ref[...] = init_ref[0]

    prev_row = jnp.where(c > 0, prev_ref[0, SUBLANE - 1:SUBLANE, SSD_XBC0:SSD_DT0], 0.0)
    next_row = jnp.where(c < nc - 1, next_ref[0, 0:1, SSD_XBC0:SSD_DT0], 0.0)
    xc = _silu(_conv3(u_ref[0, :, SSD_XBC0:SSD_DT0], prev_row, next_row, cw_ref[...], cb_ref[...], q))

    dt = _softplus(u_ref[0, :, SSD_DT0:SSD_COLS] + dtb_ref[...])
    mask = _scan_mask(q, direction)
    cum = jnp.dot(mask.astype(F32), dt * a_ref[...], preferred_element_type=F32, precision=HI)
    cum_t = cum.T
    end = q - 1 if direction == 0 else 0

    ys = []
    for g in range(SSD_GROUPS):
        b0 = SSD_WIDTH + g * SSD_STATE
        c0 = SSD_WIDTH + (SSD_GROUPS + g) * SSD_STATE
        bm = xc[:, b0:b0 + SSD_STATE].astype(BF16)
        cm = xc[:, c0:c0 + SSD_STATE].astype(BF16)
        scores = lax.dot_general(cm, bm, NT, preferred_element_type=F32)
        for h in range(g * (SSD_HEADS // SSD_GROUPS), (g + 1) * (SSD_HEADS // SSD_GROUPS)):
            hl = direction * SSD_HEADS + h
            col = cum[:, hl:hl + 1]
            seg = jnp.exp(jnp.where(mask, col - cum_t[hl:hl + 1, :], -jnp.inf))
            xdt = xc[:, h * HEAD_DIM:(h + 1) * HEAD_DIM] * dt[:, hl:hl + 1]
            y = jnp.dot((scores * seg).astype(BF16), xdt.astype(BF16), preferred_element_type=F32)
            st = state_ref[h]
            y = y + lax.dot_general(cm, st.astype(BF16), NT, preferred_element_type=F32) * jnp.exp(col)
            tot = cum[end:end + 1, hl:hl + 1]
            upd = lax.dot_general((xdt * jnp.exp(tot - col)).astype(BF16), bm, TN, preferred_element_type=F32)
            state_ref[h] = st * jnp.exp(tot) + upd
            ys.append(y)
    y_all = jnp.concatenate(ys, axis=-1)
    if finalize:
        t = (y_all + yb_ref[0] + xc[:, 0:SSD_WIDTH] * d_ref[...]) * _silu(u_ref[0, :, 0:SSD_WIDTH])
        y_all = t * lax.rsqrt(jnp.mean(t * t, axis=-1, keepdims=True) + EPS) * nw_ref[...]
    y_ref[0] = y_all

    @pl.when(j == nc - 1)
    def _():
        fin_ref[0] = state_ref[...]


def ssd_pass(u, y_other, init, sp, direction, q):
    b, L, _ = u.shape
    q = min(q, L)
    nc = L // q
    finalize = y_other is not None
    in_specs, cidx = _scan_specs(L, q, nc, SSD_COLS, direction)
    const2 = lambda bi, j: (0, 0)
    st_spec = pl.BlockSpec((1, SSD_HEADS, HEAD_DIM, SSD_STATE), lambda bi, j: (bi, 0, 0, 0))
    y_spec = pl.BlockSpec((1, q, SSD_WIDTH), lambda bi, j: (bi, cidx(j), 0))
    args = [u, u, u]
    if finalize:
        in_specs.append(y_spec)
        args.append(y_other)
    consts = [sp['cw'], sp['cb'], sp['dtb'], sp['a']] + ([sp['d'], sp['nw']] if finalize else [])
    in_specs += [st_spec] + [pl.BlockSpec(t.shape, const2) for t in consts]
    args += [init] + consts
    return pl.pallas_call(
        functools.partial(_ssd_kernel, direction=direction, finalize=finalize, q=q, nc=nc),
        out_shape=(jax.ShapeDtypeStruct((b, L, SSD_WIDTH), F32),
                   jax.ShapeDtypeStruct((b, SSD_HEADS, HEAD_DIM, SSD_STATE), F32)),
        grid=(b, nc),
        in_specs=in_specs,
        out_specs=(y_spec, st_spec),
        scratch_shapes=[pltpu.VMEM((SSD_HEADS, HEAD_DIM, SSD_STATE), F32)],
        compiler_params=_params("parallel", "arbitrary"),
        name="ssd_scan_%s" % ("fwd" if direction == 0 else "bwd"),
    )(*args)


def ssd_prepare(p):
    pad = lambda v: jnp.pad(v.reshape(1, -1), ((0, 0), (0, LANE - v.size)))
    return dict(cw=p['ssd_conv_w'], cb=p['ssd_conv_b'].reshape(1, -1),
                dtb=pad(p['ssd_dt_bias']), a=pad(-jnp.exp(p['ssd_a_log'])),
                d=jnp.repeat(p['ssd_d'], HEAD_DIM).reshape(1, -1), nw=p['ssd_norm_w'].reshape(1, -1))


def ssd_mixer(u, sp, init_f, init_b, want_y, q=256):
    yb, fin_b = ssd_pass(u, None, init_b, sp, 1, q)
    y, fin_f = ssd_pass(u, yb if want_y else None, init_f, sp, 0, q)
    return y, fin_f, fin_b


def _ml_kernel(*refs, direction, finalize, q, nc):
    if finalize:
        (u_ref, prev_ref, next_ref, hb_ref, s_init_ref, m_init_ref, cw_ref, cb_ref, gb_ref, nw_ref, pool_ref,
         y_ref, s_fin_ref, m_fin_ref, s_ref, m_ref) = refs
    else:
        (u_ref, prev_ref, next_ref, s_init_ref, m_init_ref, cw_ref, cb_ref, gb_ref,
         y_ref, s_fin_ref, m_fin_ref, s_ref, m_ref) = refs
    j = pl.program_id(1)
    c = j if direction == 0 else nc - 1 - j

    @pl.when(j == 0)
    def _():
        s_ref[...] = s_init_ref[0]
        m_ref[...] = m_init_ref[0]

    prev_row = jnp.where(c > 0, prev_ref[0, SUBLANE - 1:SUBLANE, 0:ML_V0], 0.0)
    next_row = jnp.where(c < nc - 1, next_ref[0, 0:1, 0:ML_V0], 0.0)
    qk = _silu(_conv3(u_ref[0, :, 0:ML_V0], prev_row, next_row, cw_ref[...], cb_ref[...], q))
    v = u_ref[0, :, ML_V0:ML_O0]

    gb = u_ref[0, :, ML_G0:ML_COLS] + gb_ref[...]
    mask = _scan_mask(q, direction)
    cum = jnp.dot(mask.astype(F32), _log_sigmoid(gb), preferred_element_type=F32, precision=HI)
    ig = pltpu.roll(gb, ML_HEADS, axis=1)
    end = q - 1 if direction == 0 else 0
    m_prev = m_ref[0:1, :]
    tot = cum[end:end + 1, :]
    w_end = tot - cum + ig
    m_loc = jnp.max(w_end, axis=0, keepdims=True)
    e_end = jnp.exp(w_end - m_loc)
    m_new = jnp.maximum(tot + m_prev, m_loc)
    a_prev = jnp.exp(tot + m_prev - m_new)
    a_loc = jnp.exp(m_loc - m_new)
    inter = cum + m_prev
    rel = ig - cum
    m_t = jnp.maximum(inter, cum + _running_max(rel, direction, q))
    col_a = cum - m_t
    a_inter = jnp.exp(inter - m_t)
    floor = jnp.exp(-m_t)
    rel_t = rel.T
    one_col = (lax.broadcasted_iota(jnp.int32, (q, HEAD_DIM), 1) == 0).astype(F32)

    ys = []
    for h in range(ML_HEADS):
        fl = direction * 2 * ML_HEADS + ML_HEADS + h
        qh = qk[:, h * HEAD_DIM:(h + 1) * HEAD_DIM].astype(BF16)
        kh = qk[:, ML_WIDTH + h * HEAD_DIM:ML_WIDTH + (h + 1) * HEAD_DIM] * (HEAD_DIM ** -0.5)
        v_ext = jnp.concatenate([v[:, h * HEAD_DIM:(h + 1) * HEAD_DIM], one_col], axis=-1).astype(BF16)
        pw = jnp.exp(jnp.where(mask, col_a[:, fl:fl + 1] + rel_t[fl:fl + 1, :], -jnp.inf))
        scores = lax.dot_general(qh, kh.astype(BF16), NT, preferred_element_type=F32)
        nd = jnp.dot((scores * pw).astype(BF16), v_ext, preferred_element_type=F32)
        st = s_ref[h]
        nd = nd + a_inter[:, fl:fl + 1] * jnp.dot(qh, st.astype(BF16), preferred_element_type=F32)
        den = nd[:, HEAD_DIM:HEAD_DIM + 1]
        ys.append(nd[:, 0:HEAD_DIM] / jnp.maximum(jnp.abs(den), floor[:, fl:fl + 1]))
        upd = lax.dot_general((kh * e_end[:, fl:fl + 1]).astype(BF16), v_ext, TN, preferred_element_type=F32)
        s_ref[h] = a_prev[:, fl:fl + 1] * st + a_loc[:, fl:fl + 1] * upd
    m_ref[...] = jnp.broadcast_to(m_new, m_ref.shape)
    y_all = jnp.concatenate(ys, axis=-1)
    if finalize:
        hs = y_all + hb_ref[0]
        hc = hs - jnp.dot(hs.astype(BF16), pool_ref[...], preferred_element_type=F32)
        var = jnp.dot((hc * hc).astype(BF16), pool_ref[...], preferred_element_type=F32)
        y_all = hc * lax.rsqrt(var + EPS) * nw_ref[...] * jax.nn.sigmoid(u_ref[0, :, ML_O0:ML_G0])
    y_ref[0] = y_all

    @pl.when(j == nc - 1)
    def _():
        s_fin_ref[0] = s_ref[...]
        m_fin_ref[0] = m_ref[...]


def ml_pass(u, h_other, init, mp, direction, q):
    b, L, _ = u.shape
    q = min(q, L)
    nc = L // q
    finalize = h_other is not None
    in_specs, cidx = _scan_specs(L, q, nc, ML_COLS, direction)
    const2 = lambda bi, j: (0, 0)
    s_spec = pl.BlockSpec((1, ML_HEADS, HEAD_DIM, LANE), lambda bi, j: (bi, 0, 0, 0))
    m_spec = pl.BlockSpec((1, SUBLANE, LANE), lambda bi, j: (bi, 0, 0))
    y_spec = pl.BlockSpec((1, q, ML_WIDTH), lambda bi, j: (bi, cidx(j), 0))
    args = [u, u, u]
    if finalize:
        in_specs.append(y_spec)
        args.append(h_other)
    consts = [mp['cw'], mp['cb'], mp['gb']] + ([mp['nw'], mp['pool']] if finalize else [])
    in_specs += [s_spec, m_spec] + [pl.BlockSpec(t.shape, const2) for t in consts]
    args += [init[0], init[1]] + consts
    y, s_fin, m_fin = pl.pallas_call(
        functools.partial(_ml_kernel, direction=direction, finalize=finalize, q=q, nc=nc),
        out_shape=(jax.ShapeDtypeStruct((b, L, ML_WIDTH), F32),
                   jax.ShapeDtypeStruct((b, ML_HEADS, HEAD_DIM, LANE), F32),
                   jax.ShapeDtypeStruct((b, SUBLANE, LANE), F32)),
        grid=(b, nc),
        in_specs=in_specs,
        out_specs=(y_spec, s_spec, m_spec),
        scratch_shapes=[pltpu.VMEM((ML_HEADS, HEAD_DIM, LANE), F32), pltpu.VMEM((SUBLANE, LANE), F32)],
        compiler_params=_params("parallel", "arbitrary"),
        name="mlstm_scan_%s" % ("fwd" if direction == 0 else "bwd"),
    )(*args)
    return y, (s_fin, m_fin)


def ml_prepare(p):
    gb = p['ml_gate_b'].reshape(1, -1)
    head = np.arange(ML_WIDTH) // HEAD_DIM
    pool = jnp.asarray((head[:, None] == head[None, :]) / HEAD_DIM, BF16)
    return dict(cw=p['ml_conv_w'], cb=p['ml_conv_b'].reshape(1, -1),
                gb=jnp.pad(gb, ((0, 0), (0, LANE - gb.shape[1]))), nw=p['ml_norm_w'].reshape(1, -1), pool=pool)


def ml_mixer(u, mp, init_f, init_b, want_y, q=256):
    hb, fin_b = ml_pass(u, None, init_b, mp, 1, q)
    y, fin_f = ml_pass(u, hb if want_y else None, init_f, mp, 0, q)
    return y, fin_f, fin_b


FFT_L = 4096
FFT_N = 2 * FFT_L
FFT_N2 = 128
FFT_N1 = FFT_N // FFT_N2
FFT_N1H = FFT_L // FFT_N2
FFT_K1 = FFT_N1 // 2 + 1
FFT_R = 80
FFT_UNROLL = 8
FFT_PITCH = FFT_N2 + SUBLANE


def _fft_tables():
    n2 = np.arange(FFT_N2)[:, None, None]
    k1 = np.arange(FFT_K1)[None, :, None]
    n1 = np.arange(FFT_N1H)[None, None, :]
    th = 2 * np.pi * (((FFT_N2 * n1 + n2) * k1) % FFT_N) / FFT_N
    f1 = np.zeros((FFT_N2, FFT_R, FFT_N1H))
    f1[:, 0:2 * FFT_K1:2, :] = np.cos(th)
    f1[:, 1:2 * FFT_K1:2, :] = -np.sin(th)
    wgt = np.where((np.arange(FFT_K1) == 0) | (np.arange(FFT_K1) == FFT_N1 // 2), 1.0, 2.0)[None, :, None] / FFT_N
    g1 = np.zeros((FFT_N2, FFT_N1H, FFT_R))
    g1[:, :, 0:2 * FFT_K1:2] = np.transpose(wgt * np.cos(th), (0, 2, 1))
    g1[:, :, 1:2 * FFT_K1:2] = np.transpose(-wgt * np.sin(th), (0, 2, 1))
    ph = 2 * np.pi * ((np.arange(FFT_N2)[:, None] * np.arange(FFT_N2)[None, :]) % FFT_N2) / FFT_N2
    c, s = np.cos(ph), np.sin(ph)
    f2 = np.block([[c, s], [-s, c]])
    as_bf = lambda a: jnp.asarray(a, F32).astype(BF16)
    return as_bf(f1), as_bf(f2), as_bf(f2.T), as_bf(g1)


def _sld(ref, n2, count):
    rows = pl.ds(n2, count, stride=FFT_PITCH)
    return jnp.concatenate([ref[0, rows, :], ref[1, rows, :]], axis=-1)


def _sst(ref, n2, count, val):
    rows = pl.ds(n2, count, stride=FFT_PITCH)
    ref[0, rows, :] = val[:, 0:LANE]
    ref[1, rows, :] = val[:, LANE:2 * LANE]


def _blk_ld(ref, blk, nblk):
    parts = []
    for k in range(nblk):
        rows = pl.ds(pl.multiple_of((blk + k) * FFT_PITCH, SUBLANE), FFT_N2)
        parts.append(jnp.concatenate([ref[0, rows, :], ref[1, rows, :]], axis=-1))
    return parts[0] if nblk == 1 else jnp.concatenate(parts, axis=0)


def _blk_st(ref, blk, nblk, val):
    for k in range(nblk):
        rows = pl.ds(pl.multiple_of((blk + k) * FFT_PITCH, SUBLANE), FFT_N2)
        ref[0, rows, :] = val[k * FFT_N2:(k + 1) * FFT_N2, 0:LANE]
        ref[1, rows, :] = val[k * FFT_N2:(k + 1) * FFT_N2, LANE:2 * LANE]


def _fft_stage1(z_ref, a_ref, f1_ref):
    def body(n2, carry):
        xs = _sld(z_ref, n2, FFT_N1H).astype(BF16)
        _sst(a_ref, n2, FFT_R, jnp.dot(f1_ref[n2], xs, preferred_element_type=F32))
        return carry
    lax.fori_loop(0, FFT_N2, body, 0, unroll=FFT_UNROLL)


def _spectrum_kernel(x_ref, f1_ref, f2_ref, o_ref, z_ref, a_ref):
    for n1 in range(FFT_N1H):
        _blk_st(z_ref, n1, 1, x_ref[0, n1 * FFT_N2:(n1 + 1) * FFT_N2, :])
    _fft_stage1(z_ref, a_ref, f1_ref)

    def body(k1, carry):
        slab = _blk_ld(a_ref, 2 * k1, 2).astype(BF16)
        o_ref[0, k1] = jnp.dot(f2_ref[...], slab, preferred_element_type=F32)
        return carry
    lax.fori_loop(0, FFT_K1, body, 0, unroll=3)


def spectrum(sig):
    s, L, c = sig.shape
    f1, f2, _, _ = _fft_tables()
    one = pl.Buffered(1)
    return pl.pallas_call(
        _spectrum_kernel,
        out_shape=jax.ShapeDtypeStruct((s, FFT_K1, 2 * FFT_N2, c), F32),
        grid=(s,),
        in_specs=[pl.BlockSpec((1, L, c), lambda i: (i, 0, 0)),
                  pl.BlockSpec(f1.shape, lambda i: (0, 0, 0), pipeline_mode=one),
                  pl.BlockSpec(f2.shape, lambda i: (0, 0), pipeline_mode=one)],
        out_specs=pl.BlockSpec((1, FFT_K1, 2 * FFT_N2, c), lambda i: (i, 0, 0, 0)),
        scratch_shapes=[pltpu.VMEM((2, FFT_N1H * FFT_PITCH, LANE), F32),
                        pltpu.VMEM((2, FFT_R * FFT_PITCH, LANE), F32)],
        compiler_params=pltpu.CompilerParams(dimension_semantics=("arbitrary",), vmem_limit_bytes=VMEM_LIMIT_HY),
        name="hyena_filter_spectrum",
    )(sig, f1, f2)


def _conv3_rows(src, dst, cw, cb, L):
    rows = 2 * FFT_N2
    nchunk = L // rows

    def body(i, carry):
        r0 = pl.multiple_of(i * rows, rows)
        prev_row = jnp.where(i > 0, src[pl.ds(jnp.maximum(r0 - 1, 0), 1), :], 0.0)
        next_row = jnp.where(i < nchunk - 1, src[pl.ds(jnp.minimum(r0 + rows, L - 1), 1), :], 0.0)
        _blk_st(dst, 2 * i, 2, _conv3(src[pl.ds(r0, rows), :], prev_row, next_row, cw, cb, rows))
        return carry
    lax.fori_loop(0, nchunk, body, 0, unroll=2)


def _hyena_kernel(v_ref, g_ref, h_ref, cwv_ref, cbv_ref, cwg_ref, cbg_ref, skip_ref,
                  f1_ref, f2_ref, f2t_ref, g1_ref, o_ref, z_ref, gc_ref, a_ref):
    order = pl.program_id(1)

    @pl.when(order == 0)
    def _():
        _conv3_rows(v_ref.at[0], z_ref, cwv_ref[...], cbv_ref[...], FFT_L)

    _conv3_rows(g_ref.at[0], gc_ref, cwg_ref[0], cbg_ref[0], FFT_L)
    _fft_stage1(z_ref, a_ref, f1_ref)

    def mid(k1, carry):
        x = jnp.dot(f2_ref[...], _blk_ld(a_ref, 2 * k1, 2).astype(BF16), preferred_element_type=F32)
        h = h_ref[0, k1].astype(F32)
        xr, xi, hr, hi = x[:FFT_N2], x[FFT_N2:], h[:FFT_N2], h[FFT_N2:]
        y = jnp.concatenate([xr * hr - xi * hi, xr * hi + xi * hr], axis=0).astype(BF16)
        _blk_st(a_ref, 2 * k1, 2, jnp.dot(f2t_ref[...], y, preferred_element_type=F32))
        return carry
    lax.fori_loop(0, FFT_K1, mid, 0, unroll=3)

    skip = skip_ref[0]

    def last(n2, carry):
        bs = _sld(a_ref, n2, FFT_R).astype(BF16)
        y = jnp.dot(g1_ref[n2], bs, preferred_element_type=F32)
        _sst(z_ref, n2, FFT_N1H, _sld(gc_ref, n2, FFT_N1H) * (y + skip * _sld(z_ref, n2, FFT_N1H)))
        return carry
    lax.fori_loop(0, FFT_N2, last, 0, unroll=FFT_UNROLL)

    @pl.when(order == 1)
    def _():
        for n1 in range(FFT_N1H):
            o_ref[0, n1 * FFT_N2:(n1 + 1) * FFT_N2, :] = _blk_ld(z_ref, n1, 1)


def hyena_long(u, h_spec, cw, cb, skip):
    b, L, _ = u.shape
    c = HYENA_WIDTH
    f1, f2, f2t, g1 = _fft_tables()
    one = pl.Buffered(1)
    cw3 = cw.reshape(3, 3, c).transpose(1, 0, 2)
    cb3 = cb.reshape(3, 1, c)
    return pl.pallas_call(
        _hyena_kernel,
        out_shape=jax.ShapeDtypeStruct((b, L, c), F32),
        grid=(b, 2),
        in_specs=[pl.BlockSpec((1, L, c), lambda bi, o: (bi, 0, 0), pipeline_mode=one),
                  pl.BlockSpec((1, L, c), lambda bi, o: (bi, 0, 1 + o)),
                  pl.BlockSpec((1, FFT_K1, 2 * FFT_N2, c), lambda bi, o: (o, 0, 0, 0)),
                  pl.BlockSpec((3, c), lambda bi, o: (0, 0)),
                  pl.BlockSpec((1, c), lambda bi, o: (0, 0)),
                  pl.BlockSpec((1, 3, c), lambda bi, o: (1 + o, 0, 0)),
                  pl.BlockSpec((1, 1, c), lambda bi, o: (1 + o, 0, 0)),
                  pl.BlockSpec((1, 1, c), lambda bi, o: (o, 0, 0)),
                  pl.BlockSpec(f1.shape, lambda bi, o: (0, 0, 0), pipeline_mode=one),
                  pl.BlockSpec(f2.shape, lambda bi, o: (0, 0), pipeline_mode=one),
                  pl.BlockSpec(f2t.shape, lambda bi, o: (0, 0), pipeline_mode=one),
                  pl.BlockSpec(g1.shape, lambda bi, o: (0, 0, 0), pipeline_mode=one)],
        out_specs=pl.BlockSpec((1, L, c), lambda bi, o: (bi, 0, 0)),
        scratch_shapes=[pltpu.VMEM((2, FFT_N1H * FFT_PITCH, LANE), F32),
                        pltpu.VMEM((2, FFT_N1H * FFT_PITCH, LANE), F32),
                        pltpu.VMEM((2, FFT_R * FFT_PITCH, LANE), F32)],
        compiler_params=pltpu.CompilerParams(dimension_semantics=("parallel", "arbitrary"),
                                             vmem_limit_bytes=VMEM_LIMIT_HY),
        name="hyena_long_conv",
    )(u, u, h_spec, cw3[0], cb3[0], cw3, cb3, skip.reshape(2, 1, c), f1, f2, f2t, g1)


def hyena_filter_spectra(filt):
    sigs = []
    for order in range(HYENA_ORDER):
        h_fwd, h_bwd = filt[:, 0, order], filt[:, 1, order]
        second = jnp.concatenate([jnp.zeros((1, h_fwd.shape[1]), F32), h_bwd[1:][::-1]], axis=0)
        norm = jnp.sum(jnp.abs(h_fwd), axis=0, keepdims=True) + jnp.sum(jnp.abs(second), axis=0, keepdims=True)
        sigs += [h_fwd / norm, second / norm]
    spec = spectrum(jnp.stack(sigs))
    sign = jnp.where(jnp.arange(FFT_K1) % 2 == 0, 1.0, -1.0)[None, :, None, None]
    spec = spec.reshape(HYENA_ORDER, 2, *spec.shape[1:])
    return (spec[:, 0] + sign * spec[:, 1]).astype(BF16)


def _norm_router_kernel(x_ref, nw_ref, sh_ref, sc_ref, wr_ref, br_ref, h_ref, lg_ref):
    h = _modnorm(x_ref[0], nw_ref[...], sh_ref[0], sc_ref[0])
    h_ref[0] = h
    lg = jnp.dot(h, wr_ref[...], preferred_element_type=F32, precision=HI) + br_ref[...]
    lane = lax.broadcasted_iota(jnp.int32, lg.shape, 1)
    first = lambda hit: jnp.min(jnp.where(hit, lane, ROUTER_COLS), axis=-1, keepdims=True)
    gl = jnp.where(lane < N_GROUPS, lg, -jnp.inf)
    gmax = jnp.max(gl, axis=-1, keepdims=True)
    grp = first(gl == gmax)
    grp_p = 1.0 / jnp.sum(jnp.exp(gl - gmax), axis=-1, keepdims=True)
    lo = N_GROUPS + grp * EXPERTS_PER_GROUP
    el = jnp.where((lane >= lo) & (lane < lo + EXPERTS_PER_GROUP), lg, -jnp.inf)
    e1 = jnp.max(el, axis=-1, keepdims=True)
    i1 = first(el == e1)
    el2 = jnp.where(lane == i1, -jnp.inf, el)
    e2 = jnp.max(el2, axis=-1, keepdims=True)
    i2 = first(el2 == e2)
    r = jnp.exp(e2 - e1)
    w1 = grp_p / (1.0 + r)
    w2 = w1 * r
    vals = [(i1 - N_GROUPS).astype(F32), (i2 - N_GROUPS).astype(F32), w1, w2]
    out = jnp.zeros(lg.shape, F32)
    for k, val in enumerate(vals):
        out = jnp.where(lane == k, val, out)
    lg_ref[0] = out


def norm_router(x, nw, shift, scale, w_router, b_router, tm=512):
    b, L, d = x.shape
    tm = min(tm, L)
    row = lambda bi, i: (bi, i, 0)
    return pl.pallas_call(
        _norm_router_kernel,
        out_shape=(jax.ShapeDtypeStruct((b, L, d), F32),
                   jax.ShapeDtypeStruct((b, L, ROUTER_COLS), F32)),
        grid=(b, L // tm),
        in_specs=[pl.BlockSpec((1, tm, d), row),
                  pl.BlockSpec((1, d), lambda bi, i: (0, 0)),
                  pl.BlockSpec((1, 1, d), _mod_map(shift, b)),
                  pl.BlockSpec((1, 1, d), _mod_map(scale, b)),
                  pl.BlockSpec((d, ROUTER_COLS), lambda bi, i: (0, 0)),
                  pl.BlockSpec((1, ROUTER_COLS), lambda bi, i: (0, 0))],
        out_specs=(pl.BlockSpec((1, tm, d), row), pl.BlockSpec((1, tm, ROUTER_COLS), row)),
        compiler_params=_params("parallel", "arbitrary"),
        name="moe_norm_router",
    )(x, nw.reshape(1, d), shift, scale, w_router, b_router)


def _expert_ffn_kernel(te_ref, tv_ref, x_ref, rw_ref, wg_ref, wu_ref, wd_ref, o_ref):
    i = pl.program_id(0)

    @pl.when(tv_ref[i] > 0)
    def _():
        x = x_ref[...].astype(BF16)
        g = jnp.dot(x, wg_ref[0].astype(BF16), preferred_element_type=F32)
        u = jnp.dot(x, wu_ref[0].astype(BF16), preferred_element_type=F32)
        hid = _silu(g) * u * rw_ref[...]
        o_ref[...] = jnp.dot(hid.astype(BF16), wd_ref[0].astype(BF16), preferred_element_type=F32)

    @pl.when(tv_ref[i] == 0)
    def _():
        o_ref[...] = jnp.zeros_like(o_ref)


def expert_ffn(x_sorted, row_w, tile_expert, tile_valid, w_gate, w_up, w_down, tm):
    r, d = x_sorted.shape
    f = w_gate.shape[-1]
    grid_spec = pltpu.PrefetchScalarGridSpec(
        num_scalar_prefetch=2,
        grid=(r // tm,),
        in_specs=[pl.BlockSpec((tm, d), lambda i, te, tv: (i, 0)),
                  pl.BlockSpec((tm, 1), lambda i, te, tv: (i, 0)),
                  pl.BlockSpec((1, d, f), lambda i, te, tv: (te[i], 0, 0)),
                  pl.BlockSpec((1, d, f), lambda i, te, tv: (te[i], 0, 0)),
                  pl.BlockSpec((1, f, d), lambda i, te, tv: (te[i], 0, 0))],
        out_specs=pl.BlockSpec((tm, d), lambda i, te, tv: (i, 0)),
    )
    return pl.pallas_call(
        _expert_ffn_kernel,
        out_shape=jax.ShapeDtypeStruct((r, d), F32),
        grid_spec=grid_spec,
        compiler_params=_params("arbitrary"),
        name="moe_expert_ffn",
    )(tile_expert, tile_valid, x_sorted, row_w, w_gate, w_up, w_down)


def moe_apply(h_tokens, routed, w_gate, w_up, w_down, tm=256):
    t, d = h_tokens.shape
    e_idx, e_w = routed[:, 0:2].astype(jnp.int32), routed[:, 2:4]
    flat_e = e_idx.reshape(-1).astype(jnp.int32)
    n_pairs = 2 * t
    sorted_e, order = lax.sort((flat_e, jnp.arange(n_pairs, dtype=jnp.int32)), num_keys=1, is_stable=True)
    experts = jnp.arange(N_EXPERTS, dtype=jnp.int32)
    seg_end = jnp.sum((flat_e[None, :] <= experts[:, None]).astype(jnp.int32), axis=1)
    counts = seg_end - jnp.concatenate([jnp.zeros((1,), jnp.int32), seg_end[:-1]])
    seg_start = seg_end - counts
    padded = (counts + tm - 1) // tm * tm
    pad_end = jnp.cumsum(padded)
    pad_start = pad_end - padded
    n_rows = n_pairs + N_EXPERTS * tm
    tile_start = jnp.arange(n_rows // tm, dtype=jnp.int32) * tm
    tile_expert = jnp.minimum(jnp.sum((pad_end[None, :] <= tile_start[:, None]).astype(jnp.int32), axis=1),
                              N_EXPERTS - 1)
    tile_valid = (tile_start < pad_end[-1]).astype(jnp.int32)
    off = (tile_start - pad_start[tile_expert])[:, None] + jnp.arange(tm, dtype=jnp.int32)[None, :]
    valid = (off < counts[tile_expert][:, None]) & (tile_valid[:, None] > 0)
    pair = jnp.where(valid, seg_start[tile_expert][:, None] + off, 0).reshape(-1)
    src = order[pair]
    row_token = src // 2
    row_w = jnp.where(valid.reshape(-1), e_w.reshape(-1)[src], 0.0)
    dest = pad_start[sorted_e] + (jnp.arange(n_pairs, dtype=jnp.int32) - seg_start[sorted_e])
    _, pos = lax.sort((order, dest), num_keys=1)
    pos = pos.reshape(t, 2)
    x_sorted = h_tokens[row_token]
    wg = w_gate.reshape(N_EXPERTS, d, EXPERT_HIDDEN)
    wu = w_up.reshape(N_EXPERTS, d, EXPERT_HIDDEN)
    wd = w_down.reshape(N_EXPERTS, EXPERT_HIDDEN, d)
    y_sorted = expert_ffn(x_sorted, row_w[:, None], tile_expert, tile_valid, wg, wu, wd, tm)
    return y_sorted[pos[:, 0]] + y_sorted[pos[:, 1]]


def _rmsnorm_kernel(x_ref, w_ref, o_ref):
    x = x_ref[...]
    o_ref[...] = x * lax.rsqrt(jnp.mean(x * x, axis=-1, keepdims=True) + EPS) * w_ref[...]


def final_norm(x, w, tm=1024):
    t, d = x.shape
    return pl.pallas_call(
        _rmsnorm_kernel,
        out_shape=jax.ShapeDtypeStruct((t, d), F32),
        grid=(t // tm,),
        in_specs=[pl.BlockSpec((tm, d), lambda i: (i, 0)), pl.BlockSpec((1, d), lambda i: (0, 0))],
        out_specs=pl.BlockSpec((tm, d), lambda i: (i, 0)),
        compiler_params=_params("parallel"),
        name="final_rmsnorm",
    )(x, w.reshape(1, d))


def _short_conv(x, w, b):
    L = x.shape[1]
    pad = w.shape[0] // 2
    xp = jnp.pad(x, ((0, 0), (pad, pad), (0, 0)))
    return sum(xp[:, k:k + L] * w[k] for k in range(w.shape[0])) + b


def _to_col_major(u):
    b, L, ch = u.shape
    rows = L // GRID_W
    return u.reshape(b, rows, GRID_W, ch).transpose(0, 2, 1, 3).reshape(b, L, ch)


def _to_row_major(u):
    b, L, ch = u.shape
    rows = L // GRID_W
    return u.reshape(b, GRID_W, rows, ch).transpose(0, 2, 1, 3).reshape(b, L, ch)


def _hyena_filters(L, p):
    t = jnp.arange(L, dtype=F32)
    t_unit = t / float(max(L - 1, 1))
    bands = jnp.linspace(1e-4, HYENA_BANDS - 1, HYENA_BANDS, dtype=F32)
    ang = (2 * math.pi / L) * t[:, None] * bands[None, :]
    feats = jnp.concatenate([t_unit[:, None], jnp.cos(ang), -jnp.sin(ang)], axis=-1)
    freq = p['hy_freq']
    hdn = jnp.sin(freq * (jnp.dot(feats, p['hy_pos_w1'], precision=HI) + p['hy_pos_b1']))
    hdn = jnp.sin(freq * (jnp.dot(hdn, p['hy_pos_w2'], precision=HI) + p['hy_pos_b2']))
    filt = jnp.dot(hdn, p['hy_pos_w3'], precision=HI).reshape(L, 2, HYENA_ORDER, HYENA_WIDTH)
    window = jnp.exp(-t_unit[:, None, None, None] * jnp.abs(p['hy_decay']))
    return filt * window


def _bidir_long_conv(z, h_fwd, h_bwd):
    L, ch = h_fwd.shape
    two_sided = jnp.concatenate([h_fwd, jnp.zeros((1, ch), h_fwd.dtype), h_bwd[1:][::-1]], axis=0)
    two_sided = two_sided / jnp.sum(jnp.abs(two_sided), axis=0, keepdims=True)
    zf = jnp.fft.rfft(z, n=2 * L, axis=1)
    hf = jnp.fft.rfft(two_sided, n=2 * L, axis=0)
    return jnp.fft.irfft(zf * hf[None], n=2 * L, axis=1)[:, :L]


def _hyena_mixer(u, p):
    uc = _short_conv(u, p['hy_conv_w'], p['hy_conv_b'])
    v, *gates = jnp.split(uc, HYENA_ORDER + 1, axis=-1)
    filt = _hyena_filters(u.shape[1], p)
    z = v
    for order, gate in enumerate(gates):
        z = gate * (_bidir_long_conv(z, filt[:, 0, order], filt[:, 1, order]) + p['hy_skip'][order] * z)
    return z


def _regroup_in_weight(w_in):
    sizes = (SSD_WIDTH, SSD_CONV_CH, 2 * SSD_HEADS, HY_COLS, 2 * ML_WIDTH, ML_WIDTH, ML_WIDTH, 4 * ML_HEADS)
    parts, s = [], 0
    for n in sizes:
        parts.append(jnp.pad(w_in[:, s:s + n], ((0, 0), (0, -n % LANE))))
        s += n
    return jnp.concatenate(parts, axis=1).astype(BF16)


def kernel(x, c, ctx, c_ctx, w_mod, b_mod, norm1_w, norm2_w, w_in, w_out, ssd_conv_w, ssd_conv_b, ssd_dt_bias, ssd_a_log, ssd_d, ssd_norm_w, hy_conv_w, hy_conv_b, hy_pos_w1, hy_pos_b1, hy_pos_w2, hy_pos_b2, hy_pos_w3, hy_freq, hy_decay, hy_skip, ml_conv_w, ml_conv_b, ml_gate_b, ml_norm_w, grp_router_w, grp_router_b, exp_router_w, exp_router_b, moe_w_gate, moe_w_up, moe_w_down, final_norm_w):
    layer_params = dict(
        ssd_conv_w=ssd_conv_w, ssd_conv_b=ssd_conv_b, ssd_dt_bias=ssd_dt_bias, ssd_a_log=ssd_a_log,
        ssd_d=ssd_d, ssd_norm_w=ssd_norm_w, hy_conv_w=hy_conv_w, hy_conv_b=hy_conv_b,
        hy_pos_w1=hy_pos_w1, hy_pos_b1=hy_pos_b1, hy_pos_w2=hy_pos_w2, hy_pos_b2=hy_pos_b2,
        hy_pos_w3=hy_pos_w3, hy_freq=hy_freq, hy_decay=hy_decay, hy_skip=hy_skip,
        ml_conv_w=ml_conv_w, ml_conv_b=ml_conv_b, ml_gate_b=ml_gate_b, ml_norm_w=ml_norm_w)
    bsz, seq, d = x.shape
    n_ctx = ctx.shape[1]
    xl, xc = x, ctx
    ssd0 = jnp.zeros((bsz, SSD_HEADS, HEAD_DIM, SSD_STATE), F32)
    ml0 = (jnp.zeros((bsz, ML_HEADS, HEAD_DIM, LANE), F32), jnp.zeros((bsz, SUBLANE, LANE), F32))
    c_rows = jnp.concatenate([c, c_ctx[None, :], jnp.zeros((SUBLANE - bsz - 1, d), F32)], axis=0)
    for i in range(DEPTH):
        last = i == DEPTH - 1
        p = {name: arr[i] for name, arr in layer_params.items()}
        sp, mp = ssd_prepare(p), ml_prepare(p)
        mod = modulation(c_rows, w_mod[i], b_mod[i]).reshape(SUBLANE, N_MOD, 1, d)
        mod_l = [mod[:bsz, k] for k in range(N_MOD)]
        mod_c = [mod[bsz:bsz + 1, k] for k in range(N_MOD)]
        w_in_p = _regroup_in_weight(w_in[i])
        w_out_b = w_out[i].astype(BF16)
        w_router = jnp.pad(jnp.concatenate([grp_router_w[i], exp_router_w[i]], axis=1),
                           ((0, 0), (0, ROUTER_COLS - N_GROUPS - N_EXPERTS)))
        b_router = jnp.pad(jnp.concatenate([grp_router_b[i], exp_router_b[i]]),
                           (0, ROUTER_COLS - N_GROUPS - N_EXPERTS)).reshape(1, ROUTER_COLS)

        uc_ssd, uc_hy, uc_ml = norm_proj(xc, norm1_w[i], mod_c[0], mod_c[1], w_in_p)
        yc_ssd, ssd_f, ssd_b = ssd_mixer(uc_ssd, sp, ssd0, ssd0, not last)
        yc_ml, ml_f, ml_b = ml_mixer(uc_ml, mp, ml0, ml0, not last)
        col_major = i % 2 == 1
        xin = _to_col_major(xl) if col_major else xl
        ul_ssd, ul_hy, ul_ml = norm_proj(xin, norm1_w[i], mod_l[0], mod_l[1], w_in_p)
        yl_ssd, _, _ = ssd_mixer(ul_ssd, sp, ssd_f, ssd_b, True)
        yl_ml, _, _ = ml_mixer(ul_ml, mp, ml_f, ml_b, True)
        h_spec = hyena_filter_spectra(_hyena_filters(seq, p))
        yl_hy = hyena_long(ul_hy, h_spec, p['hy_conv_w'], p['hy_conv_b'], p['hy_skip'])
        if col_major:
            yl_ssd, yl_hy, yl_ml = _to_row_major(yl_ssd), _to_row_major(yl_hy), _to_row_major(yl_ml)
        xl = out_proj(yl_ssd, yl_hy, yl_ml, xl, mod_l[2], w_out_b)
        hl, lgl = norm_router(xl, norm2_w[i], mod_l[3], mod_l[4], w_router, b_router)
        if not last:
            yc_hy = _hyena_mixer(uc_hy, p)
            xc = out_proj(yc_ssd, yc_hy, yc_ml, xc, mod_c[2], w_out_b)
            hc, lgc = norm_router(xc, norm2_w[i], mod_c[3], mod_c[4], w_router, b_router)
            h_all = jnp.concatenate([hl.reshape(-1, d), hc.reshape(-1, d)], axis=0)
            lg_all = jnp.concatenate([lgl.reshape(-1, ROUTER_COLS), lgc.reshape(-1, ROUTER_COLS)], axis=0)
        else:
            h_all, lg_all = hl.reshape(-1, d), lgl.reshape(-1, ROUTER_COLS)
        moe = moe_apply(h_all, lg_all, moe_w_gate[i], moe_w_up[i], moe_w_down[i])
        xl = xl + mod_l[5] * moe[:bsz * seq].reshape(bsz, seq, d)
        if not last:
            xc = xc + mod_c[5] * moe[bsz * seq:].reshape(bsz, n_ctx, d)
    return final_norm(xl.reshape(-1, d), final_norm_w).reshape(bsz, seq, d)
```

```python
import functools
import math

import jax
import jax.numpy as jnp
import numpy as np
from jax import lax
from jax.experimental import pallas as pl
from jax.experimental.pallas import tpu as pltpu

D_MODEL = 1024
DEPTH = 2
GRID_W = 64
HEAD_DIM = 64
SSD_WIDTH = 384
SSD_HEADS = SSD_WIDTH // HEAD_DIM
SSD_GROUPS = 2
SSD_STATE = 64
HYENA_WIDTH = 256
HYENA_ORDER = 2
HYENA_BANDS = 16
ML_WIDTH = 384
ML_HEADS = ML_WIDTH // HEAD_DIM
N_GROUPS = 4
EXPERTS_PER_GROUP = 8
N_EXPERTS = N_GROUPS * EXPERTS_PER_GROUP
EXPERT_HIDDEN = 256
N_MOD = 6
EPS = 1e-6

LANE = 128
SUBLANE = 8
VMEM_LIMIT = 48 * 1024 * 1024
VMEM_LIMIT_HY = 56 * 1024 * 1024

SSD_CONV_CH = SSD_WIDTH + 2 * SSD_GROUPS * SSD_STATE
SSD_XBC0 = SSD_WIDTH
SSD_DT0 = SSD_XBC0 + SSD_CONV_CH
SSD_COLS = SSD_DT0 + LANE
HY_COLS = (HYENA_ORDER + 1) * HYENA_WIDTH
ML_V0 = 2 * ML_WIDTH
ML_O0 = ML_V0 + ML_WIDTH
ML_G0 = ML_O0 + ML_WIDTH
ML_COLS = ML_G0 + LANE
ROUTER_COLS = LANE

F32 = jnp.float32
BF16 = jnp.bfloat16
HI = lax.Precision.HIGHEST
NT = (((1,), (1,)), ((), ()))
TN = (((0,), (0,)), ((), ()))


def _params(*sem):
    return pltpu.CompilerParams(dimension_semantics=sem, vmem_limit_bytes=VMEM_LIMIT)


def _silu(x):
    return x * jax.nn.sigmoid(x)


def _softplus(x):
    return jnp.maximum(x, 0.0) + jnp.log(1.0 + jnp.exp(-jnp.abs(x)))


def _log_sigmoid(x):
    return jnp.minimum(x, 0.0) - jnp.log(1.0 + jnp.exp(-jnp.abs(x)))


def _mod_kernel(c_ref, w_ref, b_ref, o_ref):
    o_ref[...] = jnp.dot(_silu(c_ref[...]), w_ref[...], preferred_element_type=F32, precision=HI) + b_ref[...]


def modulation(c_rows, w_mod, b_mod):
    n = w_mod.shape[1]
    tn = 1536
    return pl.pallas_call(
        _mod_kernel,
        out_shape=jax.ShapeDtypeStruct((c_rows.shape[0], n), F32),
        grid=(n // tn,),
        in_specs=[pl.BlockSpec(c_rows.shape, lambda j: (0, 0)),
                  pl.BlockSpec((w_mod.shape[0], tn), lambda j: (0, j)),
                  pl.BlockSpec((1, tn), lambda j: (0, j))],
        out_specs=pl.BlockSpec((c_rows.shape[0], tn), lambda j: (0, j)),
        compiler_params=_params("arbitrary"),
        name="adaln_modulation",
    )(c_rows, w_mod, b_mod.reshape(1, n))


def _modnorm(x, nw, shift, scale):
    y = x * lax.rsqrt(jnp.mean(x * x, axis=-1, keepdims=True) + EPS) * nw
    return y * (1.0 + scale) + shift


def _mod_map(mod, b):
    return (lambda bi, i: (bi, 0, 0)) if mod.shape[0] == b else (lambda bi, i: (0, 0, 0))


def _tok_view(x, col_major):
    b, L, d = x.shape
    return x.reshape(b, L // GRID_W, GRID_W * d) if col_major else x


def _tok_spec(L, d, tm, col_major):
    if col_major:
        return pl.BlockSpec((1, L // GRID_W, (tm // (L // GRID_W)) * d), lambda bi, i: (bi, 0, i))
    return pl.BlockSpec((1, tm, d), lambda bi, i: (bi, i, 0))


def _tok_load(ref, d, col_major):
    if not col_major:
        return ref[0]
    return jnp.concatenate([ref[0, :, j * d:(j + 1) * d] for j in range(ref.shape[2] // d)], axis=0)


def _tok_store(ref, val, d, col_major):
    if not col_major:
        ref[0] = val
        return
    rows = ref.shape[1]
    for j in range(ref.shape[2] // d):
        ref[0, :, j * d:(j + 1) * d] = val[j * rows:(j + 1) * rows]


def _norm_proj_kernel(*refs, col_major, fuse_moe):
    if fuse_moe:
        x_ref, ya_ref, yb_ref, g_ref, nw_ref, sh_ref, sc_ref, w_ref, ssd_ref, hy_ref, ml_ref, xo_ref = refs
    else:
        x_ref, nw_ref, sh_ref, sc_ref, w_ref, ssd_ref, hy_ref, ml_ref = refs
    d = nw_ref.shape[1]
    x = _tok_load(x_ref, d, col_major)
    if fuse_moe:
        x = x + g_ref[0] * (_tok_load(ya_ref, d, col_major) + _tok_load(yb_ref, d, col_major))
        _tok_store(xo_ref, x, d, col_major)
    h = _modnorm(x, nw_ref[...], sh_ref[0], sc_ref[0])
    u = jnp.dot(h.astype(BF16), w_ref[...], preferred_element_type=F32)
    ssd_ref[0] = u[:, 0:SSD_COLS]
    hy_ref[0] = u[:, SSD_COLS:SSD_COLS + HY_COLS]
    ml_ref[0] = u[:, SSD_COLS + HY_COLS:]


def norm_proj(x, moe, nw, shift, scale, w_bf16, col_major=False, tm=256):
    b, L, d = x.shape
    n = w_bf16.shape[1]
    fuse_moe = moe is not None
    tok = _tok_spec(L, d, tm, col_major)
    row = lambda bi, i: (bi, i, 0)
    const2 = lambda bi, i: (0, 0)
    args, in_specs = [_tok_view(x, col_major)], [tok]
    if fuse_moe:
        ya, yb, gate = moe
        args += [_tok_view(ya, col_major), _tok_view(yb, col_major), gate]
        in_specs += [tok, tok, pl.BlockSpec((1, 1, d), _mod_map(gate, b))]
    args += [nw.reshape(1, d), shift, scale, w_bf16]
    in_specs += [pl.BlockSpec((1, d), const2), pl.BlockSpec((1, 1, d), _mod_map(shift, b)),
                 pl.BlockSpec((1, 1, d), _mod_map(scale, b)), pl.BlockSpec((d, n), const2)]
    out_shape = [jax.ShapeDtypeStruct((b, L, SSD_COLS), F32), jax.ShapeDtypeStruct((b, L, HY_COLS), F32),
                 jax.ShapeDtypeStruct((b, L, ML_COLS), F32)]
    out_specs = [pl.BlockSpec((1, tm, SSD_COLS), row), pl.BlockSpec((1, tm, HY_COLS), row),
                 pl.BlockSpec((1, tm, ML_COLS), row)]
    if fuse_moe:
        out_shape.append(jax.ShapeDtypeStruct(args[0].shape, F32))
        out_specs.append(tok)
    outs = pl.pallas_call(
        functools.partial(_norm_proj_kernel, col_major=col_major, fuse_moe=fuse_moe),
        out_shape=tuple(out_shape),
        grid=(b, L // tm),
        in_specs=in_specs,
        out_specs=tuple(out_specs),
        compiler_params=_params("parallel", "arbitrary"),
        name="norm_in_proj",
    )(*args)
    return (*outs[:3], outs[3].reshape(b, L, d) if fuse_moe else x)


def _out_proj_kernel(ys_ref, yh_ref, ym_ref, x_ref, g_ref, w_ref, o_ref, *, col_major):
    d = w_ref.shape[1]
    y = jnp.concatenate([ys_ref[0], yh_ref[0], ym_ref[0]], axis=-1).astype(BF16)
    r = _tok_load(x_ref, d, col_major) + g_ref[0] * jnp.dot(y, w_ref[...], preferred_element_type=F32)
    _tok_store(o_ref, r, d, col_major)


def out_proj(y_ssd, y_hy, y_ml, x, gate, w_bf16, col_major=False, tm=512):
    b, L, d = x.shape
    tm = min(tm, L)
    row = lambda bi, i: (bi, i, 0)
    tok = _tok_spec(L, d, tm, col_major)
    xv = _tok_view(x, col_major)
    return pl.pallas_call(
        functools.partial(_out_proj_kernel, col_major=col_major),
        out_shape=jax.ShapeDtypeStruct(xv.shape, F32),
        grid=(b, L // tm),
        in_specs=[pl.BlockSpec((1, tm, SSD_WIDTH), row), pl.BlockSpec((1, tm, HYENA_WIDTH), row),
                  pl.BlockSpec((1, tm, ML_WIDTH), row), tok,
                  pl.BlockSpec((1, 1, d), _mod_map(gate, b)),
                  pl.BlockSpec(w_bf16.shape, lambda bi, i: (0, 0))],
        out_specs=tok,
        compiler_params=_params("parallel", "arbitrary"),
        name="out_proj_residual",
    )(y_ssd, y_hy, y_ml, xv, gate, w_bf16).reshape(b, L, d)


def _conv3(xr, prev_row, next_row, cw, cb, q):
    rid = lax.broadcasted_iota(jnp.int32, (q, 1), 0)
    x_prev = jnp.where(rid == 0, prev_row, pltpu.roll(xr, 1, axis=0))
    x_next = jnp.where(rid == q - 1, next_row, pltpu.roll(xr, q - 1, axis=0))
    return x_prev * cw[0:1] + xr * cw[1:2] + x_next * cw[2:3] + cb


def _scan_mask(q, direction):
    li = lax.broadcasted_iota(jnp.int32, (q, q), 0)
    si = lax.broadcasted_iota(jnp.int32, (q, q), 1)
    return (si <= li) if direction == 0 else (si >= li)


def _running_max(x, direction, q):
    rid = lax.broadcasted_iota(jnp.int32, (q, 1), 0)
    s = 1
    while s < q:
        if direction == 0:
            x = jnp.where(rid >= s, jnp.maximum(x, pltpu.roll(x, s, axis=0)), x)
        else:
            x = jnp.where(rid < q - s, jnp.maximum(x, pltpu.roll(x, q - s, axis=0)), x)
        s *= 2
    return x


def _scan_specs(L, q, nc, cols, direction):
    hb = q // SUBLANE
    nrb = L // SUBLANE
    cidx = (lambda j: j) if direction == 0 else (lambda j: nc - 1 - j)
    specs = [pl.BlockSpec((1, q, cols), lambda bi, j: (bi, cidx(j), 0)),
             pl.BlockSpec((1, SUBLANE, cols), lambda bi, j: (bi, jnp.maximum(cidx(j) * hb - 1, 0), 0)),
             pl.BlockSpec((1, SUBLANE, cols), lambda bi, j: (bi, jnp.minimum((cidx(j) + 1) * hb, nrb - 1), 0))]
    return specs, cidx


def _ssd_kernel(*refs, direction, finalize, q, nc):
    if finalize:
        (u_ref, prev_ref, next_ref, yb_ref, init_ref, cw_ref, cb_ref, dtb_ref, a_ref, d_ref, nw_ref,
         y_ref, fin_ref, state_ref) = refs
    else:
        (u_ref, prev_ref, next_ref, init_ref, cw_ref, cb_ref, dtb_ref, a_ref,
         y_ref, fin_ref, state_ref) = refs
    j = pl.program_id(1)
    c = j if direction == 0 else nc - 1 - j

    @pl.when(j == 0)
    def _():
        state_ref[...] = init_ref[0]

    prev_row = jnp.where(c > 0, prev_ref[0, SUBLANE - 1:SUBLANE, SSD_XBC0:SSD_DT0], 0.0)
    next_row = jnp.where(c < nc - 1, next_ref[0, 0:1, SSD_XBC0:SSD_DT0], 0.0)
    xc = _silu(_conv3(u_ref[0, :, SSD_XBC0:SSD_DT0], prev_row, next_row, cw_ref[...], cb_ref[...], q))

    dt = _softplus(u_ref[0, :, SSD_DT0:SSD_COLS] + dtb_ref[...])
    mask = _scan_mask(q, direction)
    cum = jnp.dot(mask.astype(F32), dt * a_ref[...], preferred_element_type=F32, precision=HI)
    cum_t = cum.T
    end = q - 1 if direction == 0 else 0

    ys = []
    for g in range(SSD_GROUPS):
        b0 = SSD_WIDTH + g * SSD_STATE
        c0 = SSD_WIDTH + (SSD_GROUPS + g) * SSD_STATE
        bm = xc[:, b0:b0 + SSD_STATE].astype(BF16)
        cm = xc[:, c0:c0 + SSD_STATE].astype(BF16)
        scores = lax.dot_general(cm, bm, NT, preferred_element_type=F32)
        for h in range(g * (SSD_HEADS // SSD_GROUPS), (g + 1) * (SSD_HEADS // SSD_GROUPS)):
            hl = direction * SSD_HEADS + h
            col = cum[:, hl:hl + 1]
            seg = jnp.exp(jnp.where(mask, col - cum_t[hl:hl + 1, :], -jnp.inf))
            xdt = xc[:, h * HEAD_DIM:(h + 1) * HEAD_DIM] * dt[:, hl:hl + 1]
            y = jnp.dot((scores * seg).astype(BF16), xdt.astype(BF16), preferred_element_type=F32)
            st = state_ref[h]
            y = y + lax.dot_general(cm, st.astype(BF16), NT, preferred_element_type=F32) * jnp.exp(col)
            tot = cum[end:end + 1, hl:hl + 1]
            upd = lax.dot_general((xdt * jnp.exp(tot - col)).astype(BF16), bm, TN, preferred_element_type=F32)
            state_ref[h] = st * jnp.exp(tot) + upd
            ys.append(y)
    y_all = jnp.concatenate(ys, axis=-1)
    if finalize:
        t = (y_all + yb_ref[0] + xc[:, 0:SSD_WIDTH] * d_ref[...]) * _silu(u_ref[0, :, 0:SSD_WIDTH])
        y_all = t * lax.rsqrt(jnp.mean(t * t, axis=-1, keepdims=True) + EPS) * nw_ref[...]
    y_ref[0] = y_all

    @pl.when(j == nc - 1)
    def _():
        fin_ref[0] = state_ref[...]


def ssd_pass(u, y_other, init, sp, direction, q):
    b, L, _ = u.shape
    q = min(q, L)
    nc = L // q
    finalize = y_other is not None
    in_specs, cidx = _scan_specs(L, q, nc, SSD_COLS, direction)
    const2 = lambda bi, j: (0, 0)
    st_spec = pl.BlockSpec((1, SSD_HEADS, HEAD_DIM, SSD_STATE), lambda bi, j: (bi, 0, 0, 0))
    y_spec = pl.BlockSpec((1, q, SSD_WIDTH), lambda bi, j: (bi, cidx(j), 0))
    args = [u, u, u]
    if finalize:
        in_specs.append(y_spec)
        args.append(y_other)
    consts = [sp['cw'], sp['cb'], sp['dtb'], sp['a']] + ([sp['d'], sp['nw']] if finalize else [])
    in_specs += [st_spec] + [pl.BlockSpec(t.shape, const2) for t in consts]
    args += [init] + consts
    return pl.pallas_call(
        functools.partial(_ssd_kernel, direction=direction, finalize=finalize, q=q, nc=nc),
        out_shape=(jax.ShapeDtypeStruct((b, L, SSD_WIDTH), F32),
                   jax.ShapeDtypeStruct((b, SSD_HEADS, HEAD_DIM, SSD_STATE), F32)),
        grid=(b, nc),
        in_specs=in_specs,
        out_specs=(y_spec, st_spec),
        scratch_shapes=[pltpu.VMEM((SSD_HEADS, HEAD_DIM, SSD_STATE), F32)],
        compiler_params=_params("parallel", "arbitrary"),
        name="ssd_scan_%s" % ("fwd" if direction == 0 else "bwd"),
    )(*args)


def ssd_prepare(p):
    pad = lambda v: jnp.pad(v.reshape(1, -1), ((0, 0), (0, LANE - v.size)))
    return dict(cw=p['ssd_conv_w'], cb=p['ssd_conv_b'].reshape(1, -1),
                dtb=pad(p['ssd_dt_bias']), a=pad(-jnp.exp(p['ssd_a_log'])),
                d=jnp.repeat(p['ssd_d'], HEAD_DIM).reshape(1, -1), nw=p['ssd_norm_w'].reshape(1, -1))


def ssd_mixer(u, sp, init_f, init_b, want_y, q=256):
    yb, fin_b = ssd_pass(u, None, init_b, sp, 1, q)
    y, fin_f = ssd_pass(u, yb if want_y else None, init_f, sp, 0, q)
    return y, fin_f, fin_b


def _ml_kernel(*refs, direction, finalize, q, nc):
    if finalize:
        (u_ref, prev_ref, next_ref, hb_ref, s_init_ref, m_init_ref, cw_ref, cb_ref, gb_ref, nw_ref, pool_ref,
         y_ref, s_fin_ref, m_fin_ref, s_ref, m_ref) = refs
    else:
        (u_ref, prev_ref, next_ref, s_init_ref, m_init_ref, cw_ref, cb_ref, gb_ref,
         y_ref, s_fin_ref, m_fin_ref, s_ref, m_ref) = refs
    j = pl.program_id(1)
    c = j if direction == 0 else nc - 1 - j

    @pl.when(j == 0)
    def _():
        s_ref[...] = s_init_ref[0]
        m_ref[...] = m_init_ref[0]

    prev_row = jnp.where(c > 0, prev_ref[0, SUBLANE - 1:SUBLANE, 0:ML_V0], 0.0)
    next_row = jnp.where(c < nc - 1, next_ref[0, 0:1, 0:ML_V0], 0.0)
    qk = _silu(_conv3(u_ref[0, :, 0:ML_V0], prev_row, next_row, cw_ref[...], cb_ref[...], q))
    v = u_ref[0, :, ML_V0:ML_O0]

    gb = u_ref[0, :, ML_G0:ML_COLS] + gb_ref[...]
    mask = _scan_mask(q, direction)
    cum = jnp.dot(mask.astype(F32), _log_sigmoid(gb), preferred_element_type=F32, precision=HI)
    ig = pltpu.roll(gb, ML_HEADS, axis=1)
    end = q - 1 if direction == 0 else 0
    m_prev = m_ref[0:1, :]
    tot = cum[end:end + 1, :]
    w_end = tot - cum + ig
    m_loc = jnp.max(w_end, axis=0, keepdims=True)
    e_end = jnp.exp(w_end - m_loc)
    m_new = jnp.maximum(tot + m_prev, m_loc)
    a_prev = jnp.exp(tot + m_prev - m_new)
    a_loc = jnp.exp(m_loc - m_new)
    inter = cum + m_prev
    rel = ig - cum
    m_t = jnp.maximum(inter, cum + _running_max(rel, direction, q))
    col_a = cum - m_t
    a_inter = jnp.exp(inter - m_t)
    floor = jnp.exp(-m_t)
    rel_t = rel.T
    one_col = (lax.broadcasted_iota(jnp.int32, (q, HEAD_DIM), 1) == 0).astype(F32)

    ys = []
    for h in range(ML_HEADS):
        fl = direction * 2 * ML_HEADS + ML_HEADS + h
        qh = qk[:, h * HEAD_DIM:(h + 1) * HEAD_DIM].astype(BF16)
        kh = qk[:, ML_WIDTH + h * HEAD_DIM:ML_WIDTH + (h + 1) * HEAD_DIM] * (HEAD_DIM ** -0.5)
        v_ext = jnp.concatenate([v[:, h * HEAD_DIM:(h + 1) * HEAD_DIM], one_col], axis=-1).astype(BF16)
        pw = jnp.exp(jnp.where(mask, col_a[:, fl:fl + 1] + rel_t[fl:fl + 1, :], -jnp.inf))
        scores = lax.dot_general(qh, kh.astype(BF16), NT, preferred_element_type=F32)
        nd = jnp.dot((scores * pw).astype(BF16), v_ext, preferred_element_type=F32)
        st = s_ref[h]
        nd = nd + a_inter[:, fl:fl + 1] * jnp.dot(qh, st.astype(BF16), preferred_element_type=F32)
        den = nd[:, HEAD_DIM:HEAD_DIM + 1]
        ys.append(nd[:, 0:HEAD_DIM] / jnp.maximum(jnp.abs(den), floor[:, fl:fl + 1]))
        upd = lax.dot_general((kh * e_end[:, fl:fl + 1]).astype(BF16), v_ext, TN, preferred_element_type=F32)
        s_ref[h] = a_prev[:, fl:fl + 1] * st + a_loc[:, fl:fl + 1] * upd
    m_ref[...] = jnp.broadcast_to(m_new, m_ref.shape)
    y_all = jnp.concatenate(ys, axis=-1)
    if finalize:
        hs = y_all + hb_ref[0]
        hc = hs - jnp.dot(hs.astype(BF16), pool_ref[...], preferred_element_type=F32)
        var = jnp.dot((hc * hc).astype(BF16), pool_ref[...], preferred_element_type=F32)
        y_all = hc * lax.rsqrt(var + EPS) * nw_ref[...] * jax.nn.sigmoid(u_ref[0, :, ML_O0:ML_G0])
    y_ref[0] = y_all

    @pl.when(j == nc - 1)
    def _():
        s_fin_ref[0] = s_ref[...]
        m_fin_ref[0] = m_ref[...]


def ml_pass(u, h_other, init, mp, direction, q):
    b, L, _ = u.shape
    q = min(q, L)
    nc = L // q
    finalize = h_other is not None
    in_specs, cidx = _scan_specs(L, q, nc, ML_COLS, direction)
    const2 = lambda bi, j: (0, 0)
    s_spec = pl.BlockSpec((1, ML_HEADS, HEAD_DIM, LANE), lambda bi, j: (bi, 0, 0, 0))
    m_spec = pl.BlockSpec((1, SUBLANE, LANE), lambda bi, j: (bi, 0, 0))
    y_spec = pl.BlockSpec((1, q, ML_WIDTH), lambda bi, j: (bi, cidx(j), 0))
    args = [u, u, u]
    if finalize:
        in_specs.append(y_spec)
        args.append(h_other)
    consts = [mp['cw'], mp['cb'], mp['gb']] + ([mp['nw'], mp['pool']] if finalize else [])
    in_specs += [s_spec, m_spec] + [pl.BlockSpec(t.shape, const2) for t in consts]
    args += [init[0], init[1]] + consts
    y, s_fin, m_fin = pl.pallas_call(
        functools.partial(_ml_kernel, direction=direction, finalize=finalize, q=q, nc=nc),
        out_shape=(jax.ShapeDtypeStruct((b, L, ML_WIDTH), F32),
                   jax.ShapeDtypeStruct((b, ML_HEADS, HEAD_DIM, LANE), F32),
                   jax.ShapeDtypeStruct((b, SUBLANE, LANE), F32)),
        grid=(b, nc),
        in_specs=in_specs,
        out_specs=(y_spec, s_spec, m_spec),
        scratch_shapes=[pltpu.VMEM((ML_HEADS, HEAD_DIM, LANE), F32), pltpu.VMEM((SUBLANE, LANE), F32)],
        compiler_params=_params("parallel", "arbitrary"),
        name="mlstm_scan_%s" % ("fwd" if direction == 0 else "bwd"),
    )(*args)
    return y, (s_fin, m_fin)


def ml_prepare(p):
    gb = p['ml_gate_b'].reshape(1, -1)
    head = np.arange(ML_WIDTH) // HEAD_DIM
    pool = jnp.asarray((head[:, None] == head[None, :]) / HEAD_DIM, BF16)
    return dict(cw=p['ml_conv_w'], cb=p['ml_conv_b'].reshape(1, -1),
                gb=jnp.pad(gb, ((0, 0), (0, LANE - gb.shape[1]))), nw=p['ml_norm_w'].reshape(1, -1), pool=pool)


def ml_mixer(u, mp, init_f, init_b, want_y, q=256):
    hb, fin_b = ml_pass(u, None, init_b, mp, 1, q)
    y, fin_f = ml_pass(u, hb if want_y else None, init_f, mp, 0, q)
    return y, fin_f, fin_b


FFT_L = 4096
FFT_N = 2 * FFT_L
FFT_N2 = 128
FFT_N1 = FFT_N // FFT_N2
FFT_N1H = FFT_L // FFT_N2
FFT_K1 = FFT_N1 // 2 + 1
FFT_R = 80
FFT_UNROLL = 8
FFT_PITCH = FFT_N2 + SUBLANE


def _fft_tables():
    n2 = np.arange(FFT_N2)[:, None, None]
    k1 = np.arange(FFT_K1)[None, :, None]
    n1 = np.arange(FFT_N1H)[None, None, :]
    th = 2 * np.pi * (((FFT_N2 * n1 + n2) * k1) % FFT_N) / FFT_N
    f1 = np.zeros((FFT_N2, FFT_R, FFT_N1H))
    f1[:, 0:2 * FFT_K1:2, :] = np.cos(th)
    f1[:, 1:2 * FFT_K1:2, :] = -np.sin(th)
    wgt = np.where((np.arange(FFT_K1) == 0) | (np.arange(FFT_K1) == FFT_N1 // 2), 1.0, 2.0)[None, :, None] / FFT_N
    g1 = np.zeros((FFT_N2, FFT_N1H, FFT_R))
    g1[:, :, 0:2 * FFT_K1:2] = np.transpose(wgt * np.cos(th), (0, 2, 1))
    g1[:, :, 1:2 * FFT_K1:2] = np.transpose(-wgt * np.sin(th), (0, 2, 1))
    ph = 2 * np.pi * ((np.arange(FFT_N2)[:, None] * np.arange(FFT_N2)[None, :]) % FFT_N2) / FFT_N2
    c, s = np.cos(ph), np.sin(ph)
    f2 = np.block([[c, s], [-s, c]])
    as_bf = lambda a: jnp.asarray(a, F32).astype(BF16)
    return as_bf(f1), as_bf(f2), as_bf(f2.T), as_bf(g1)


def _sld(ref, n2, count):
    rows = pl.ds(n2, count, stride=FFT_PITCH)
    return jnp.concatenate([ref[0, rows, :], ref[1, rows, :]], axis=-1)


def _sst(ref, n2, count, val):
    rows = pl.ds(n2, count, stride=FFT_PITCH)
    ref[0, rows, :] = val[:, 0:LANE]
    ref[1, rows, :] = val[:, LANE:2 * LANE]


def _blk_ld(ref, blk, nblk):
    parts = []
    for k in range(nblk):
        rows = pl.ds(pl.multiple_of((blk + k) * FFT_PITCH, SUBLANE), FFT_N2)
        parts.append(jnp.concatenate([ref[0, rows, :], ref[1, rows, :]], axis=-1))
    return parts[0] if nblk == 1 else jnp.concatenate(parts, axis=0)


def _blk_st(ref, blk, nblk, val):
    for k in range(nblk):
        rows = pl.ds(pl.multiple_of((blk + k) * FFT_PITCH, SUBLANE), FFT_N2)
        ref[0, rows, :] = val[k * FFT_N2:(k + 1) * FFT_N2, 0:LANE]
        ref[1, rows, :] = val[k * FFT_N2:(k + 1) * FFT_N2, LANE:2 * LANE]


def _fft_stage1(z_ref, a_ref, f1_ref):
    def body(n2, carry):
        xs = _sld(z_ref, n2, FFT_N1H).astype(BF16)
        _sst(a_ref, n2, FFT_R, jnp.dot(f1_ref[n2], xs, preferred_element_type=F32))
        return carry
    lax.fori_loop(0, FFT_N2, body, 0, unroll=FFT_UNROLL)


def _spectrum_kernel(x_ref, f1_ref, f2_ref, o_ref, z_ref, a_ref):
    for n1 in range(FFT_N1H):
        _blk_st(z_ref, n1, 1, x_ref[0, n1 * FFT_N2:(n1 + 1) * FFT_N2, :])
    _fft_stage1(z_ref, a_ref, f1_ref)

    def body(k1, carry):
        slab = _blk_ld(a_ref, 2 * k1, 2).astype(BF16)
        o_ref[0, k1] = jnp.dot(f2_ref[...], slab, preferred_element_type=F32)
        return carry
    lax.fori_loop(0, FFT_K1, body, 0, unroll=3)


def spectrum(sig):
    s, L, c = sig.shape
    f1, f2, _, _ = _fft_tables()
    one = pl.Buffered(1)
    return pl.pallas_call(
        _spectrum_kernel,
        out_shape=jax.ShapeDtypeStruct((s, FFT_K1, 2 * FFT_N2, c), F32),
        grid=(s,),
        in_specs=[pl.BlockSpec((1, L, c), lambda i: (i, 0, 0)),
                  pl.BlockSpec(f1.shape, lambda i: (0, 0, 0), pipeline_mode=one),
                  pl.BlockSpec(f2.shape, lambda i: (0, 0), pipeline_mode=one)],
        out_specs=pl.BlockSpec((1, FFT_K1, 2 * FFT_N2, c), lambda i: (i, 0, 0, 0)),
        scratch_shapes=[pltpu.VMEM((2, FFT_N1H * FFT_PITCH, LANE), F32),
                        pltpu.VMEM((2, FFT_R * FFT_PITCH, LANE), F32)],
        compiler_params=pltpu.CompilerParams(dimension_semantics=("arbitrary",), vmem_limit_bytes=VMEM_LIMIT_HY),
        name="hyena_filter_spectrum",
    )(sig, f1, f2)


def _conv3_rows(src, dst, cw, cb, L):
    rows = 2 * FFT_N2
    nchunk = L // rows

    def body(i, carry):
        r0 = pl.multiple_of(i * rows, rows)
        prev_row = jnp.where(i > 0, src[pl.ds(jnp.maximum(r0 - 1, 0), 1), :], 0.0)
        next_row = jnp.where(i < nchunk - 1, src[pl.ds(jnp.minimum(r0 + rows, L - 1), 1), :], 0.0)
        _blk_st(dst, 2 * i, 2, _conv3(src[pl.ds(r0, rows), :], prev_row, next_row, cw, cb, rows))
        return carry
    lax.fori_loop(0, nchunk, body, 0, unroll=2)


def _hyena_kernel(v_ref, g_ref, h_ref, cwv_ref, cbv_ref, cwg_ref, cbg_ref, skip_ref,
                  f1_ref, f2_ref, f2t_ref, g1_ref, o_ref, z_ref, gc_ref, a_ref):
    order = pl.program_id(1)

    @pl.when(order == 0)
    def _():
        _conv3_rows(v_ref.at[0], z_ref, cwv_ref[...], cbv_ref[...], FFT_L)

    _conv3_rows(g_ref.at[0], gc_ref, cwg_ref[0], cbg_ref[0], FFT_L)
    _fft_stage1(z_ref, a_ref, f1_ref)

    def mid(k1, carry):
        x = jnp.dot(f2_ref[...], _blk_ld(a_ref, 2 * k1, 2).astype(BF16), preferred_element_type=F32)
        h = h_ref[0, k1].astype(F32)
        xr, xi, hr, hi = x[:FFT_N2], x[FFT_N2:], h[:FFT_N2], h[FFT_N2:]
        y = jnp.concatenate([xr * hr - xi * hi, xr * hi + xi * hr], axis=0).astype(BF16)
        _blk_st(a_ref, 2 * k1, 2, jnp.dot(f2t_ref[...], y, preferred_element_type=F32))
        return carry
    lax.fori_loop(0, FFT_K1, mid, 0, unroll=3)

    skip = skip_ref[0]

    def last(n2, carry):
        bs = _sld(a_ref, n2, FFT_R).astype(BF16)
        y = jnp.dot(g1_ref[n2], bs, preferred_element_type=F32)
        _sst(z_ref, n2, FFT_N1H, _sld(gc_ref, n2, FFT_N1H) * (y + skip * _sld(z_ref, n2, FFT_N1H)))
        return carry
    lax.fori_loop(0, FFT_N2, last, 0, unroll=FFT_UNROLL)

    @pl.when(order == 1)
    def _():
        for n1 in range(FFT_N1H):
            o_ref[0, n1 * FFT_N2:(n1 + 1) * FFT_N2, :] = _blk_ld(z_ref, n1, 1)


def hyena_long(u, h_spec, cw, cb, skip):
    b, L, _ = u.shape
    c = HYENA_WIDTH
    f1, f2, f2t, g1 = _fft_tables()
    one = pl.Buffered(1)
    cw3 = cw.reshape(3, 3, c).transpose(1, 0, 2)
    cb3 = cb.reshape(3, 1, c)
    return pl.pallas_call(
        _hyena_kernel,
        out_shape=jax.ShapeDtypeStruct((b, L, c), F32),
        grid=(b, 2),
        in_specs=[pl.BlockSpec((1, L, c), lambda bi, o: (bi, 0, 0), pipeline_mode=one),
                  pl.BlockSpec((1, L, c), lambda bi, o: (bi, 0, 1 + o)),
                  pl.BlockSpec((1, FFT_K1, 2 * FFT_N2, c), lambda bi, o: (o, 0, 0, 0)),
                  pl.BlockSpec((3, c), lambda bi, o: (0, 0)),
                  pl.BlockSpec((1, c), lambda bi, o: (0, 0)),
                  pl.BlockSpec((1, 3, c), lambda bi, o: (1 + o, 0, 0)),
                  pl.BlockSpec((1, 1, c), lambda bi, o: (1 + o, 0, 0)),
                  pl.BlockSpec((1, 1, c), lambda bi, o: (o, 0, 0)),
                  pl.BlockSpec(f1.shape, lambda bi, o: (0, 0, 0), pipeline_mode=one),
                  pl.BlockSpec(f2.shape, lambda bi, o: (0, 0), pipeline_mode=one),
                  pl.BlockSpec(f2t.shape, lambda bi, o: (0, 0), pipeline_mode=one),
                  pl.BlockSpec(g1.shape, lambda bi, o: (0, 0, 0), pipeline_mode=one)],
        out_specs=pl.BlockSpec((1, L, c), lambda bi, o: (bi, 0, 0)),
        scratch_shapes=[pltpu.VMEM((2, FFT_N1H * FFT_PITCH, LANE), F32),
                        pltpu.VMEM((2, FFT_N1H * FFT_PITCH, LANE), F32),
                        pltpu.VMEM((2, FFT_R * FFT_PITCH, LANE), F32)],
        compiler_params=pltpu.CompilerParams(dimension_semantics=("parallel", "arbitrary"),
                                             vmem_limit_bytes=VMEM_LIMIT_HY),
        name="hyena_long_conv",
    )(u, u, h_spec, cw3[0], cb3[0], cw3, cb3, skip.reshape(2, 1, c), f1, f2, f2t, g1)


def hyena_filter_spectra(filt):
    sigs = []
    for order in range(HYENA_ORDER):
        h_fwd, h_bwd = filt[:, 0, order], filt[:, 1, order]
        second = jnp.concatenate([jnp.zeros((1, h_fwd.shape[1]), F32), h_bwd[1:][::-1]], axis=0)
        norm = jnp.sum(jnp.abs(h_fwd), axis=0, keepdims=True) + jnp.sum(jnp.abs(second), axis=0, keepdims=True)
        sigs += [h_fwd / norm, second / norm]
    spec = spectrum(jnp.stack(sigs))
    sign = jnp.where(jnp.arange(FFT_K1) % 2 == 0, 1.0, -1.0)[None, :, None, None]
    spec = spec.reshape(HYENA_ORDER, 2, *spec.shape[1:])
    return (spec[:, 0] + sign * spec[:, 1]).astype(BF16)


def _norm_router_kernel(x_ref, nw_ref, sh_ref, sc_ref, wr_ref, br_ref, h_ref, lg_ref):
    h = _modnorm(x_ref[0], nw_ref[...], sh_ref[0], sc_ref[0])
    h_ref[0] = h
    lg = jnp.dot(h, wr_ref[...], preferred_element_type=F32, precision=HI) + br_ref[...]
    lane = lax.broadcasted_iota(jnp.int32, lg.shape, 1)
    first = lambda hit: jnp.min(jnp.where(hit, lane, ROUTER_COLS), axis=-1, keepdims=True)
    gl = jnp.where(lane < N_GROUPS, lg, -jnp.inf)
    gmax = jnp.max(gl, axis=-1, keepdims=True)
    grp = first(gl == gmax)
    grp_p = 1.0 / jnp.sum(jnp.exp(gl - gmax), axis=-1, keepdims=True)
    lo = N_GROUPS + grp * EXPERTS_PER_GROUP
    el = jnp.where((lane >= lo) & (lane < lo + EXPERTS_PER_GROUP), lg, -jnp.inf)
    e1 = jnp.max(el, axis=-1, keepdims=True)
    i1 = first(el == e1)
    el2 = jnp.where(lane == i1, -jnp.inf, el)
    e2 = jnp.max(el2, axis=-1, keepdims=True)
    i2 = first(el2 == e2)
    r = jnp.exp(e2 - e1)
    w1 = grp_p / (1.0 + r)
    w2 = w1 * r
    vals = [(i1 - N_GROUPS).astype(F32), (i2 - N_GROUPS).astype(F32), w1, w2]
    out = jnp.zeros(lg.shape, F32)
    for k, val in enumerate(vals):
        out = jnp.where(lane == k, val, out)
    lg_ref[0] = out


def norm_router(x, nw, shift, scale, w_router, b_router, tm=512):
    b, L, d = x.shape
    tm = min(tm, L)
    row = lambda bi, i: (bi, i, 0)
    return pl.pallas_call(
        _norm_router_kernel,
        out_shape=(jax.ShapeDtypeStruct((b, L, d), F32),
                   jax.ShapeDtypeStruct((b, L, ROUTER_COLS), F32)),
        grid=(b, L // tm),
        in_specs=[pl.BlockSpec((1, tm, d), row),
                  pl.BlockSpec((1, d), lambda bi, i: (0, 0)),
                  pl.BlockSpec((1, 1, d), _mod_map(shift, b)),
                  pl.BlockSpec((1, 1, d), _mod_map(scale, b)),
                  pl.BlockSpec((d, ROUTER_COLS), lambda bi, i: (0, 0)),
                  pl.BlockSpec((1, ROUTER_COLS), lambda bi, i: (0, 0))],
        out_specs=(pl.BlockSpec((1, tm, d), row), pl.BlockSpec((1, tm, ROUTER_COLS), row)),
        compiler_params=_params("parallel", "arbitrary"),
        name="moe_norm_router",
    )(x, nw.reshape(1, d), shift, scale, w_router, b_router)


def _expert_ffn_kernel(te_ref, tv_ref, x_ref, rw_ref, wg_ref, wu_ref, wd_ref, o_ref):
    i = pl.program_id(0)

    @pl.when(tv_ref[i] > 0)
    def _():
        x = x_ref[...].astype(BF16)
        g = jnp.dot(x, wg_ref[0].astype(BF16), preferred_element_type=F32)
        u = jnp.dot(x, wu_ref[0].astype(BF16), preferred_element_type=F32)
        hid = _silu(g) * u * rw_ref[...]
        o_ref[...] = jnp.dot(hid.astype(BF16), wd_ref[0].astype(BF16), preferred_element_type=F32)

    @pl.when(tv_ref[i] == 0)
    def _():
        o_ref[...] = jnp.zeros_like(o_ref)


def expert_ffn(x_sorted, row_w, tile_expert, tile_valid, w_gate, w_up, w_down, tm):
    r, d = x_sorted.shape
    f = w_gate.shape[-1]
    grid_spec = pltpu.PrefetchScalarGridSpec(
        num_scalar_prefetch=2,
        grid=(r // tm,),
        in_specs=[pl.BlockSpec((tm, d), lambda i, te, tv: (i, 0)),
                  pl.BlockSpec((tm, 1), lambda i, te, tv: (i, 0)),
                  pl.BlockSpec((1, d, f), lambda i, te, tv: (te[i], 0, 0)),
                  pl.BlockSpec((1, d, f), lambda i, te, tv: (te[i], 0, 0)),
                  pl.BlockSpec((1, f, d), lambda i, te, tv: (te[i], 0, 0))],
        out_specs=pl.BlockSpec((tm, d), lambda i, te, tv: (i, 0)),
    )
    return pl.pallas_call(
        _expert_ffn_kernel,
        out_shape=jax.ShapeDtypeStruct((r, d), F32),
        grid_spec=grid_spec,
        compiler_params=_params("arbitrary"),
        name="moe_expert_ffn",
    )(tile_expert, tile_valid, x_sorted, row_w, w_gate, w_up, w_down)


def moe_apply(h_tokens, routed, w_gate, w_up, w_down, tm=256):
    t, d = h_tokens.shape
    e_idx, e_w = routed[:, 0:2].astype(jnp.int32), routed[:, 2:4]
    flat_e = e_idx.reshape(-1).astype(jnp.int32)
    n_pairs = 2 * t
    sorted_e, order = lax.sort((flat_e, jnp.arange(n_pairs, dtype=jnp.int32)), num_keys=1, is_stable=True)
    experts = jnp.arange(N_EXPERTS, dtype=jnp.int32)
    seg_end = jnp.sum((flat_e[None, :] <= experts[:, None]).astype(jnp.int32), axis=1)
    counts = seg_end - jnp.concatenate([jnp.zeros((1,), jnp.int32), seg_end[:-1]])
    seg_start = seg_end - counts
    padded = (counts + tm - 1) // tm * tm
    pad_end = jnp.cumsum(padded)
    pad_start = pad_end - padded
    n_rows = n_pairs + N_EXPERTS * tm
    tile_start = jnp.arange(n_rows // tm, dtype=jnp.int32) * tm
    tile_expert = jnp.minimum(jnp.sum((pad_end[None, :] <= tile_start[:, None]).astype(jnp.int32), axis=1),
                              N_EXPERTS - 1)
    tile_valid = (tile_start < pad_end[-1]).astype(jnp.int32)
    off = (tile_start - pad_start[tile_expert])[:, None] + jnp.arange(tm, dtype=jnp.int32)[None, :]
    valid = (off < counts[tile_expert][:, None]) & (tile_valid[:, None] > 0)
    spread = jnp.arange(n_rows, dtype=jnp.int32).reshape(-1, tm) % n_pairs
    pair = jnp.where(valid, seg_start[tile_expert][:, None] + off, spread).reshape(-1)
    src = order[pair]
    row_token = src // 2
    row_w = jnp.where(valid.reshape(-1), e_w.reshape(-1)[src], 0.0)
    dest = pad_start[sorted_e] + (jnp.arange(n_pairs, dtype=jnp.int32) - seg_start[sorted_e])
    _, pos = lax.sort((order, dest), num_keys=1)
    pos = pos.reshape(t, 2)
    x_sorted = h_tokens[row_token]
    wg = w_gate.reshape(N_EXPERTS, d, EXPERT_HIDDEN)
    wu = w_up.reshape(N_EXPERTS, d, EXPERT_HIDDEN)
    wd = w_down.reshape(N_EXPERTS, EXPERT_HIDDEN, d)
    y_sorted = expert_ffn(x_sorted, row_w[:, None], tile_expert, tile_valid, wg, wu, wd, tm)
    return y_sorted[pos[:, 0]], y_sorted[pos[:, 1]]


def _final_kernel(x_ref, ya_ref, yb_ref, g_ref, w_ref, o_ref):
    x = x_ref[0] + g_ref[0] * (ya_ref[0] + yb_ref[0])
    o_ref[0] = x * lax.rsqrt(jnp.mean(x * x, axis=-1, keepdims=True) + EPS) * w_ref[...]


def final_norm(x, ya, yb, gate, w, tm=512):
    b, L, d = x.shape
    tok = pl.BlockSpec((1, tm, d), lambda bi, i: (bi, i, 0))
    return pl.pallas_call(
        _final_kernel,
        out_shape=jax.ShapeDtypeStruct((b, L, d), F32),
        grid=(b, L // tm),
        in_specs=[tok, tok, tok, pl.BlockSpec((1, 1, d), _mod_map(gate, b)),
                  pl.BlockSpec((1, d), lambda bi, i: (0, 0))],
        out_specs=tok,
        compiler_params=_params("parallel", "arbitrary"),
        name="final_rmsnorm",
    )(x, ya, yb, gate, w.reshape(1, d))


def _short_conv(x, w, b):
    L = x.shape[1]
    pad = w.shape[0] // 2
    xp = jnp.pad(x, ((0, 0), (pad, pad), (0, 0)))
    return sum(xp[:, k:k + L] * w[k] for k in range(w.shape[0])) + b


def _hyena_filters(L, p):
    t = jnp.arange(L, dtype=F32)
    t_unit = t / float(max(L - 1, 1))
    bands = jnp.linspace(1e-4, HYENA_BANDS - 1, HYENA_BANDS, dtype=F32)
    ang = (2 * math.pi / L) * t[:, None] * bands[None, :]
    feats = jnp.concatenate([t_unit[:, None], jnp.cos(ang), -jnp.sin(ang)], axis=-1)
    freq = p['hy_freq']
    hdn = jnp.sin(freq * (jnp.dot(feats, p['hy_pos_w1'], precision=HI) + p['hy_pos_b1']))
    hdn = jnp.sin(freq * (jnp.dot(hdn, p['hy_pos_w2'], precision=HI) + p['hy_pos_b2']))
    filt = jnp.dot(hdn, p['hy_pos_w3'], precision=HI).reshape(L, 2, HYENA_ORDER, HYENA_WIDTH)
    window = jnp.exp(-t_unit[:, None, None, None] * jnp.abs(p['hy_decay']))
    return filt * window


def _bidir_long_conv(z, h_fwd, h_bwd):
    L, ch = h_fwd.shape
    two_sided = jnp.concatenate([h_fwd, jnp.zeros((1, ch), h_fwd.dtype), h_bwd[1:][::-1]], axis=0)
    two_sided = two_sided / jnp.sum(jnp.abs(two_sided), axis=0, keepdims=True)
    zf = jnp.fft.rfft(z, n=2 * L, axis=1)
    hf = jnp.fft.rfft(two_sided, n=2 * L, axis=0)
    return jnp.fft.irfft(zf * hf[None], n=2 * L, axis=1)[:, :L]


def _hyena_mixer(u, p):
    uc = _short_conv(u, p['hy_conv_w'], p['hy_conv_b'])
    v, *gates = jnp.split(uc, HYENA_ORDER + 1, axis=-1)
    filt = _hyena_filters(u.shape[1], p)
    z = v
    for order, gate in enumerate(gates):
        z = gate * (_bidir_long_conv(z, filt[:, 0, order], filt[:, 1, order]) + p['hy_skip'][order] * z)
    return z


def _regroup_in_weight(w_in):
    sizes = (SSD_WIDTH, SSD_CONV_CH, 2 * SSD_HEADS, HY_COLS, 2 * ML_WIDTH, ML_WIDTH, ML_WIDTH, 4 * ML_HEADS)
    parts, s = [], 0
    for n in sizes:
        parts.append(jnp.pad(w_in[:, s:s + n], ((0, 0), (0, -n % LANE))))
        s += n
    return jnp.concatenate(parts, axis=1).astype(BF16)


def kernel(x, c, ctx, c_ctx, w_mod, b_mod, norm1_w, norm2_w, w_in, w_out, ssd_conv_w, ssd_conv_b, ssd_dt_bias, ssd_a_log, ssd_d, ssd_norm_w, hy_conv_w, hy_conv_b, hy_pos_w1, hy_pos_b1, hy_pos_w2, hy_pos_b2, hy_pos_w3, hy_freq, hy_decay, hy_skip, ml_conv_w, ml_conv_b, ml_gate_b, ml_norm_w, grp_router_w, grp_router_b, exp_router_w, exp_router_b, moe_w_gate, moe_w_up, moe_w_down, final_norm_w):
    layer_params = dict(
        ssd_conv_w=ssd_conv_w, ssd_conv_b=ssd_conv_b, ssd_dt_bias=ssd_dt_bias, ssd_a_log=ssd_a_log,
        ssd_d=ssd_d, ssd_norm_w=ssd_norm_w, hy_conv_w=hy_conv_w, hy_conv_b=hy_conv_b,
        hy_pos_w1=hy_pos_w1, hy_pos_b1=hy_pos_b1, hy_pos_w2=hy_pos_w2, hy_pos_b2=hy_pos_b2,
        hy_pos_w3=hy_pos_w3, hy_freq=hy_freq, hy_decay=hy_decay, hy_skip=hy_skip,
        ml_conv_w=ml_conv_w, ml_conv_b=ml_conv_b, ml_gate_b=ml_gate_b, ml_norm_w=ml_norm_w)
    bsz, seq, d = x.shape
    n_ctx = ctx.shape[1]
    xl, xc = x, ctx
    moe_l = moe_c = None
    ssd0 = jnp.zeros((bsz, SSD_HEADS, HEAD_DIM, SSD_STATE), F32)
    ml0 = (jnp.zeros((bsz, ML_HEADS, HEAD_DIM, LANE), F32), jnp.zeros((bsz, SUBLANE, LANE), F32))
    c_rows = jnp.concatenate([c, c_ctx[None, :], jnp.zeros((SUBLANE - bsz - 1, d), F32)], axis=0)
    for i in range(DEPTH):
        last = i == DEPTH - 1
        p = {name: arr[i] for name, arr in layer_params.items()}
        sp, mp = ssd_prepare(p), ml_prepare(p)
        mod = modulation(c_rows, w_mod[i], b_mod[i]).reshape(SUBLANE, N_MOD, 1, d)
        mod_l = [mod[:bsz, k] for k in range(N_MOD)]
        mod_c = [mod[bsz:bsz + 1, k] for k in range(N_MOD)]
        w_in_p = _regroup_in_weight(w_in[i])
        w_out_b = w_out[i].astype(BF16)
        w_router = jnp.pad(jnp.concatenate([grp_router_w[i], exp_router_w[i]], axis=1),
                           ((0, 0), (0, ROUTER_COLS - N_GROUPS - N_EXPERTS)))
        b_router = jnp.pad(jnp.concatenate([grp_router_b[i], exp_router_b[i]]),
                           (0, ROUTER_COLS - N_GROUPS - N_EXPERTS)).reshape(1, ROUTER_COLS)

        uc_ssd, uc_hy, uc_ml, xc = norm_proj(xc, moe_c, norm1_w[i], mod_c[0], mod_c[1], w_in_p)
        yc_ssd, ssd_f, ssd_b = ssd_mixer(uc_ssd, sp, ssd0, ssd0, not last)
        yc_ml, ml_f, ml_b = ml_mixer(uc_ml, mp, ml0, ml0, not last)
        col_major = i % 2 == 1
        ul_ssd, ul_hy, ul_ml, xl = norm_proj(xl, moe_l, norm1_w[i], mod_l[0], mod_l[1], w_in_p, col_major)
        yl_ssd, _, _ = ssd_mixer(ul_ssd, sp, ssd_f, ssd_b, True)
        yl_ml, _, _ = ml_mixer(ul_ml, mp, ml_f, ml_b, True)
        h_spec = hyena_filter_spectra(_hyena_filters(seq, p))
        yl_hy = hyena_long(ul_hy, h_spec, p['hy_conv_w'], p['hy_conv_b'], p['hy_skip'])
        xl = out_proj(yl_ssd, yl_hy, yl_ml, xl, mod_l[2], w_out_b, col_major)
        hl, lgl = norm_router(xl, norm2_w[i], mod_l[3], mod_l[4], w_router, b_router)
        if not last:
            yc_hy = _hyena_mixer(uc_hy, p)
            xc = out_proj(yc_ssd, yc_hy, yc_ml, xc, mod_c[2], w_out_b)
            hc, lgc = norm_router(xc, norm2_w[i], mod_c[3], mod_c[4], w_router, b_router)
            h_all = jnp.concatenate([hl.reshape(-1, d), hc.reshape(-1, d)], axis=0)
            lg_all = jnp.concatenate([lgl.reshape(-1, ROUTER_COLS), lgc.reshape(-1, ROUTER_COLS)], axis=0)
        else:
            h_all, lg_all = hl.reshape(-1, d), lgl.reshape(-1, ROUTER_COLS)
        ya, yb = moe_apply(h_all, lg_all, moe_w_gate[i], moe_w_up[i], moe_w_down[i])
        moe_l = (ya[:bsz * seq].reshape(bsz, seq, d), yb[:bsz * seq].reshape(bsz, seq, d), mod_l[5])
        if not last:
            moe_c = (ya[bsz * seq:].reshape(bsz, n_ctx, d), yb[bsz * seq:].reshape(bsz, n_ctx, d), mod_c[5])
    return final_norm(xl, *moe_l, final_norm_w)
```

```python
import functools
import math

import jax
import jax.numpy as jnp
import numpy as np
from jax import lax
from jax.experimental import pallas as pl
from jax.experimental.pallas import tpu as pltpu

D_MODEL = 1024
DEPTH = 2
GRID_W = 64
HEAD_DIM = 64
SSD_WIDTH = 384
SSD_HEADS = SSD_WIDTH // HEAD_DIM
SSD_GROUPS = 2
SSD_STATE = 64
HYENA_WIDTH = 256
HYENA_ORDER = 2
HYENA_BANDS = 16
ML_WIDTH = 384
ML_HEADS = ML_WIDTH // HEAD_DIM
N_GROUPS = 4
EXPERTS_PER_GROUP = 8
N_EXPERTS = N_GROUPS * EXPERTS_PER_GROUP
EXPERT_HIDDEN = 256
N_MOD = 6
EPS = 1e-6

LANE = 128
SUBLANE = 8
VMEM_LIMIT = 48 * 1024 * 1024
VMEM_LIMIT_HY = 56 * 1024 * 1024

SSD_CONV_CH = SSD_WIDTH + 2 * SSD_GROUPS * SSD_STATE
SSD_XBC0 = SSD_WIDTH
SSD_DT0 = SSD_XBC0 + SSD_CONV_CH
SSD_COLS = SSD_DT0 + LANE
HY_COLS = (HYENA_ORDER + 1) * HYENA_WIDTH
ML_V0 = 2 * ML_WIDTH
ML_O0 = ML_V0 + ML_WIDTH
ML_G0 = ML_O0 + ML_WIDTH
ML_COLS = ML_G0 + LANE
ROUTER_COLS = LANE

F32 = jnp.float32
BF16 = jnp.bfloat16
HI = lax.Precision.HIGHEST
NT = (((1,), (1,)), ((), ()))
TN = (((0,), (0,)), ((), ()))


def _params(*sem):
    return pltpu.CompilerParams(dimension_semantics=sem, vmem_limit_bytes=VMEM_LIMIT)


def _silu(x):
    return x * jax.nn.sigmoid(x)


def _softplus(x):
    return jnp.maximum(x, 0.0) + jnp.log(1.0 + jnp.exp(-jnp.abs(x)))


def _log_sigmoid(x):
    return jnp.minimum(x, 0.0) - jnp.log(1.0 + jnp.exp(-jnp.abs(x)))


def _mod_kernel(c_ref, w_ref, b_ref, o_ref):
    o_ref[...] = jnp.dot(_silu(c_ref[...]), w_ref[...], preferred_element_type=F32, precision=HI) + b_ref[...]


def modulation(c_rows, w_mod, b_mod):
    n = w_mod.shape[1]
    tn = 1536
    return pl.pallas_call(
        _mod_kernel,
        out_shape=jax.ShapeDtypeStruct((c_rows.shape[0], n), F32),
        grid=(n // tn,),
        in_specs=[pl.BlockSpec(c_rows.shape, lambda j: (0, 0)),
                  pl.BlockSpec((w_mod.shape[0], tn), lambda j: (0, j)),
                  pl.BlockSpec((1, tn), lambda j: (0, j))],
        out_specs=pl.BlockSpec((c_rows.shape[0], tn), lambda j: (0, j)),
        compiler_params=_params("arbitrary"),
        name="adaln_modulation",
    )(c_rows, w_mod, b_mod.reshape(1, n))


def _modnorm(x, nw, shift, scale):
    y = x * lax.rsqrt(jnp.mean(x * x, axis=-1, keepdims=True) + EPS) * nw
    return y * (1.0 + scale) + shift


def _mod_map(mod, b):
    return (lambda bi, i: (bi, 0, 0)) if mod.shape[0] == b else (lambda bi, i: (0, 0, 0))


def _tok_view(x, col_major):
    b, L, d = x.shape
    return x.reshape(b, L // GRID_W, GRID_W, d) if col_major else x


def _tok_spec(L, d, tm, col_major):
    if col_major:
        assert tm == (L // GRID_W) * SUBLANE
        return pl.BlockSpec((1, L // GRID_W, SUBLANE, d), lambda bi, i: (bi, 0, i, 0))
    return pl.BlockSpec((1, tm, d), lambda bi, i: (bi, i, 0))


def _tok_load(ref, col_major):
    if not col_major:
        return ref[0]
    return jnp.concatenate([ref[0, :, j, :] for j in range(ref.shape[2])], axis=0)


def _tok_store(ref, val, col_major):
    if not col_major:
        ref[0] = val
        return
    rows = ref.shape[1]
    for j in range(ref.shape[2]):
        ref[0, :, j, :] = val[j * rows:(j + 1) * rows]


def _norm_proj_kernel(*refs, col_major, fuse_moe):
    if fuse_moe:
        x_ref, ya_ref, yb_ref, g_ref, nw_ref, sh_ref, sc_ref, w_ref, ssd_ref, hy_ref, ml_ref, xo_ref = refs
    else:
        x_ref, nw_ref, sh_ref, sc_ref, w_ref, ssd_ref, hy_ref, ml_ref = refs
    x = _tok_load(x_ref, col_major)
    if fuse_moe:
        x = x + g_ref[0] * (_tok_load(ya_ref, col_major) + _tok_load(yb_ref, col_major))
        _tok_store(xo_ref, x, col_major)
    h = _modnorm(x, nw_ref[...], sh_ref[0], sc_ref[0])
    u = jnp.dot(h.astype(BF16), w_ref[...], preferred_element_type=F32)
    ssd_ref[0] = u[:, 0:SSD_COLS]
    hy_ref[0] = u[:, SSD_COLS:SSD_COLS + HY_COLS]
    ml_ref[0] = u[:, SSD_COLS + HY_COLS:]


def norm_proj(x, moe, nw, shift, scale, w_bf16, col_major=False, tm=256):
    b, L, d = x.shape
    n = w_bf16.shape[1]
    fuse_moe = moe is not None
    tm = (L // GRID_W) * SUBLANE if col_major else tm
    tok = _tok_spec(L, d, tm, col_major)
    row = lambda bi, i: (bi, i, 0)
    const2 = lambda bi, i: (0, 0)
    args, in_specs = [_tok_view(x, col_major)], [tok]
    if fuse_moe:
        ya, yb, gate = moe
        args += [_tok_view(ya, col_major), _tok_view(yb, col_major), gate]
        in_specs += [tok, tok, pl.BlockSpec((1, 1, d), _mod_map(gate, b))]
    args += [nw.reshape(1, d), shift, scale, w_bf16]
    in_specs += [pl.BlockSpec((1, d), const2), pl.BlockSpec((1, 1, d), _mod_map(shift, b)),
                 pl.BlockSpec((1, 1, d), _mod_map(scale, b)), pl.BlockSpec((d, n), const2)]
    out_shape = [jax.ShapeDtypeStruct((b, L, SSD_COLS), F32), jax.ShapeDtypeStruct((b, L, HY_COLS), F32),
                 jax.ShapeDtypeStruct((b, L, ML_COLS), F32)]
    out_specs = [pl.BlockSpec((1, tm, SSD_COLS), row), pl.BlockSpec((1, tm, HY_COLS), row),
                 pl.BlockSpec((1, tm, ML_COLS), row)]
    if fuse_moe:
        out_shape.append(jax.ShapeDtypeStruct(args[0].shape, F32))
        out_specs.append(tok)
    outs = pl.pallas_call(
        functools.partial(_norm_proj_kernel, col_major=col_major, fuse_moe=fuse_moe),
        out_shape=tuple(out_shape),
        grid=(b, L // tm),
        in_specs=in_specs,
        out_specs=tuple(out_specs),
        compiler_params=_params("parallel", "arbitrary"),
        name="norm_in_proj",
    )(*args)
    return (*outs[:3], outs[3].reshape(b, L, d) if fuse_moe else x)


def _out_proj_kernel(ys_ref, yh_ref, ym_ref, x_ref, g_ref, w_ref, o_ref, *, col_major):
    y = jnp.concatenate([ys_ref[0], yh_ref[0], ym_ref[0]], axis=-1).astype(BF16)
    r = _tok_load(x_ref, col_major) + g_ref[0] * jnp.dot(y, w_ref[...], preferred_element_type=F32)
    _tok_store(o_ref, r, col_major)


def out_proj(y_ssd, y_hy, y_ml, x, gate, w_bf16, col_major=False, tm=512):
    b, L, d = x.shape
    tm = (L // GRID_W) * SUBLANE if col_major else min(tm, L)
    row = lambda bi, i: (bi, i, 0)
    tok = _tok_spec(L, d, tm, col_major)
    xv = _tok_view(x, col_major)
    return pl.pallas_call(
        functools.partial(_out_proj_kernel, col_major=col_major),
        out_shape=jax.ShapeDtypeStruct(xv.shape, F32),
        grid=(b, L // tm),
        in_specs=[pl.BlockSpec((1, tm, SSD_WIDTH), row), pl.BlockSpec((1, tm, HYENA_WIDTH), row),
                  pl.BlockSpec((1, tm, ML_WIDTH), row), tok,
                  pl.BlockSpec((1, 1, d), _mod_map(gate, b)),
                  pl.BlockSpec(w_bf16.shape, lambda bi, i: (0, 0))],
        out_specs=tok,
        compiler_params=_params("parallel", "arbitrary"),
        name="out_proj_residual",
    )(y_ssd, y_hy, y_ml, xv, gate, w_bf16).reshape(b, L, d)


def _conv3(xr, prev_row, next_row, cw, cb, q):
    rid = lax.broadcasted_iota(jnp.int32, (q, 1), 0)
    x_prev = jnp.where(rid == 0, prev_row, pltpu.roll(xr, 1, axis=0))
    x_next = jnp.where(rid == q - 1, next_row, pltpu.roll(xr, q - 1, axis=0))
    return x_prev * cw[0:1] + xr * cw[1:2] + x_next * cw[2:3] + cb


def _scan_mask(q, direction):
    li = lax.broadcasted_iota(jnp.int32, (q, q), 0)
    si = lax.broadcasted_iota(jnp.int32, (q, q), 1)
    return (si <= li) if direction == 0 else (si >= li)


def _running_max(x, direction, q):
    rid = lax.broadcasted_iota(jnp.int32, (q, 1), 0)
    s = 1
    while s < q:
        if direction == 0:
            x = jnp.where(rid >= s, jnp.maximum(x, pltpu.roll(x, s, axis=0)), x)
        else:
            x = jnp.where(rid < q - s, jnp.maximum(x, pltpu.roll(x, q - s, axis=0)), x)
        s *= 2
    return x


def _scan_specs(L, q, nc, cols, direction):
    hb = q // SUBLANE
    nrb = L // SUBLANE
    cidx = (lambda j: j) if direction == 0 else (lambda j: nc - 1 - j)
    specs = [pl.BlockSpec((1, q, cols), lambda bi, j: (bi, cidx(j), 0)),
             pl.BlockSpec((1, SUBLANE, cols), lambda bi, j: (bi, jnp.maximum(cidx(j) * hb - 1, 0), 0)),
             pl.BlockSpec((1, SUBLANE, cols), lambda bi, j: (bi, jnp.minimum((cidx(j) + 1) * hb, nrb - 1), 0))]
    return specs, cidx


def _ssd_kernel(*refs, direction, finalize, q, nc):
    if finalize:
        (u_ref, prev_ref, next_ref, yb_ref, init_ref, cw_ref, cb_ref, dtb_ref, a_ref, d_ref, nw_ref,
         y_ref, fin_ref, state_ref) = refs
    else:
        (u_ref, prev_ref, next_ref, init_ref, cw_ref, cb_ref, dtb_ref, a_ref,
         y_ref, fin_ref, state_ref) = refs
    j = pl.program_id(1)
    c = j if direction == 0 else nc - 1 - j

    @pl.when(j == 0)
    def _():
        state_ref[...] = init_ref[0]

    prev_row = jnp.where(c > 0, prev_ref[0, SUBLANE - 1:SUBLANE, SSD_XBC0:SSD_DT0], 0.0)
    next_row = jnp.where(c < nc - 1, next_ref[0, 0:1, SSD_XBC0:SSD_DT0], 0.0)
    xc = _silu(_conv3(u_ref[0, :, SSD_XBC0:SSD_DT0], prev_row, next_row, cw_ref[...], cb_ref[...], q))

    dt = _softplus(u_ref[0, :, SSD_DT0:SSD_COLS] + dtb_ref[...])
    mask = _scan_mask(q, direction)
    cum = jnp.dot(mask.astype(F32), dt * a_ref[...], preferred_element_type=F32, precision=HI)
    cum_t = cum.T
    end = q - 1 if direction == 0 else 0

    ys = []
    for g in range(SSD_GROUPS):
        b0 = SSD_WIDTH + g * SSD_STATE
        c0 = SSD_WIDTH + (SSD_GROUPS + g) * SSD_STATE
        bm = xc[:, b0:b0 + SSD_STATE].astype(BF16)
        cm = xc[:, c0:c0 + SSD_STATE].astype(BF16)
        scores = lax.dot_general(cm, bm, NT, preferred_element_type=F32)
        for h in range(g * (SSD_HEADS // SSD_GROUPS), (g + 1) * (SSD_HEADS // SSD_GROUPS)):
            hl = direction * SSD_HEADS + h
            col = cum[:, hl:hl + 1]
            seg = jnp.exp(jnp.where(mask, col - cum_t[hl:hl + 1, :], -jnp.inf))
            xdt = xc[:, h * HEAD_DIM:(h + 1) * HEAD_DIM] * dt[:, hl:hl + 1]
            y = jnp.dot((scores * seg).astype(BF16), xdt.astype(BF16), preferred_element_type=F32)
            st = state_ref[h]
            y = y + lax.dot_general(cm, st.astype(BF16), NT, preferred_element_type=F32) * jnp.exp(col)
            tot = cum[end:end + 1, hl:hl + 1]
            upd = lax.dot_general((xdt * jnp.exp(tot - col)).astype(BF16), bm, TN, preferred_element_type=F32)
            state_ref[h] = st * jnp.exp(tot) + upd
            ys.append(y)
    y_all = jnp.concatenate(ys, axis=-1)
    if finalize:
        t = (y_all + yb_ref[0] + xc[:, 0:SSD_WIDTH] * d_ref[...]) * _silu(u_ref[0, :, 0:SSD_WIDTH])
        y_all = t * lax.rsqrt(jnp.mean(t * t, axis=-1, keepdims=True) + EPS) * nw_ref[...]
    y_ref[0] = y_all

    @pl.when(j == nc - 1)
    def _():
        fin_ref[0] = state_ref[...]


def ssd_pass(u, y_other, init, sp, direction, q):
    b, L, _ = u.shape
    q = min(q, L)
    nc = L // q
    finalize = y_other is not None
    in_specs, cidx = _scan_specs(L, q, nc, SSD_COLS, direction)
    const2 = lambda bi, j: (0, 0)
    st_spec = pl.BlockSpec((1, SSD_HEADS, HEAD_DIM, SSD_STATE), lambda bi, j: (bi, 0, 0, 0))
    y_spec = pl.BlockSpec((1, q, SSD_WIDTH), lambda bi, j: (bi, cidx(j), 0))
    args = [u, u, u]
    if finalize:
        in_specs.append(y_spec)
        args.append(y_other)
    consts = [sp['cw'], sp['cb'], sp['dtb'], sp['a']] + ([sp['d'], sp['nw']] if finalize else [])
    in_specs += [st_spec] + [pl.BlockSpec(t.shape, const2) for t in consts]
    args += [init] + consts
    return pl.pallas_call(
        functools.partial(_ssd_kernel, direction=direction, finalize=finalize, q=q, nc=nc),
        out_shape=(jax.ShapeDtypeStruct((b, L, SSD_WIDTH), F32),
                   jax.ShapeDtypeStruct((b, SSD_HEADS, HEAD_DIM, SSD_STATE), F32)),
        grid=(b, nc),
        in_specs=in_specs,
        out_specs=(y_spec, st_spec),
        scratch_shapes=[pltpu.VMEM((SSD_HEADS, HEAD_DIM, SSD_STATE), F32)],
        compiler_params=_params("parallel", "arbitrary"),
        name="ssd_scan_%s" % ("fwd" if direction == 0 else "bwd"),
    )(*args)


def ssd_prepare(p):
    pad = lambda v: jnp.pad(v.reshape(1, -1), ((0, 0), (0, LANE - v.size)))
    return dict(cw=p['ssd_conv_w'], cb=p['ssd_conv_b'].reshape(1, -1),
                dtb=pad(p['ssd_dt_bias']), a=pad(-jnp.exp(p['ssd_a_log'])),
                d=jnp.repeat(p['ssd_d'], HEAD_DIM).reshape(1, -1), nw=p['ssd_norm_w'].reshape(1, -1))


def ssd_mixer(u, sp, init_f, init_b, want_y, q=256):
    yb, fin_b = ssd_pass(u, None, init_b, sp, 1, q)
    y, fin_f = ssd_pass(u, yb if want_y else None, init_f, sp, 0, q)
    return y, fin_f, fin_b


def _ml_kernel(*refs, direction, finalize, q, nc):
    if finalize:
        (u_ref, prev_ref, next_ref, hb_ref, s_init_ref, m_init_ref, cw_ref, cb_ref, gb_ref, nw_ref, pool_ref,
         y_ref, s_fin_ref, m_fin_ref, s_ref, m_ref) = refs
    else:
        (u_ref, prev_ref, next_ref, s_init_ref, m_init_ref, cw_ref, cb_ref, gb_ref,
         y_ref, s_fin_ref, m_fin_ref, s_ref, m_ref) = refs
    j = pl.program_id(1)
    c = j if direction == 0 else nc - 1 - j

    @pl.when(j == 0)
    def _():
        s_ref[...] = s_init_ref[0]
        m_ref[...] = m_init_ref[0]

    prev_row = jnp.where(c > 0, prev_ref[0, SUBLANE - 1:SUBLANE, 0:ML_V0], 0.0)
    next_row = jnp.where(c < nc - 1, next_ref[0, 0:1, 0:ML_V0], 0.0)
    qk = _silu(_conv3(u_ref[0, :, 0:ML_V0], prev_row, next_row, cw_ref[...], cb_ref[...], q))
    v = u_ref[0, :, ML_V0:ML_O0]

    gb = u_ref[0, :, ML_G0:ML_COLS] + gb_ref[...]
    mask = _scan_mask(q, direction)
    cum = jnp.dot(mask.astype(F32), _log_sigmoid(gb), preferred_element_type=F32, precision=HI)
    ig = pltpu.roll(gb, ML_HEADS, axis=1)
    end = q - 1 if direction == 0 else 0
    m_prev = m_ref[0:1, :]
    tot = cum[end:end + 1, :]
    w_end = tot - cum + ig
    m_loc = jnp.max(w_end, axis=0, keepdims=True)
    e_end = jnp.exp(w_end - m_loc)
    m_new = jnp.maximum(tot + m_prev, m_loc)
    a_prev = jnp.exp(tot + m_prev - m_new)
    a_loc = jnp.exp(m_loc - m_new)
    inter = cum + m_prev
    rel = ig - cum
    m_t = jnp.maximum(inter, cum + _running_max(rel, direction, q))
    col_a = cum - m_t
    a_inter = jnp.exp(inter - m_t)
    floor = jnp.exp(-m_t)
    rel_t = rel.T
    one_col = (lax.broadcasted_iota(jnp.int32, (q, HEAD_DIM), 1) == 0).astype(F32)

    ys = []
    for h in range(ML_HEADS):
        fl = direction * 2 * ML_HEADS + ML_HEADS + h
        qh = qk[:, h * HEAD_DIM:(h + 1) * HEAD_DIM].astype(BF16)
        kh = qk[:, ML_WIDTH + h * HEAD_DIM:ML_WIDTH + (h + 1) * HEAD_DIM] * (HEAD_DIM ** -0.5)
        v_ext = jnp.concatenate([v[:, h * HEAD_DIM:(h + 1) * HEAD_DIM], one_col], axis=-1).astype(BF16)
        pw = jnp.exp(jnp.where(mask, col_a[:, fl:fl + 1] + rel_t[fl:fl + 1, :], -jnp.inf))
        scores = lax.dot_general(qh, kh.astype(BF16), NT, preferred_element_type=F32)
        nd = jnp.dot((scores * pw).astype(BF16), v_ext, preferred_element_type=F32)
        st = s_ref[h]
        nd = nd + a_inter[:, fl:fl + 1] * jnp.dot(qh, st.astype(BF16), preferred_element_type=F32)
        den = nd[:, HEAD_DIM:HEAD_DIM + 1]
        ys.append(nd[:, 0:HEAD_DIM] / jnp.maximum(jnp.abs(den), floor[:, fl:fl + 1]))
        upd = lax.dot_general((kh * e_end[:, fl:fl + 1]).astype(BF16), v_ext, TN, preferred_element_type=F32)
        s_ref[h] = a_prev[:, fl:fl + 1] * st + a_loc[:, fl:fl + 1] * upd
    m_ref[...] = jnp.broadcast_to(m_new, m_ref.shape)
    y_all = jnp.concatenate(ys, axis=-1)
    if finalize:
        hs = y_all + hb_ref[0]
        hc = hs - jnp.dot(hs.astype(BF16), pool_ref[...], preferred_element_type=F32)
        var = jnp.dot((hc * hc).astype(BF16), pool_ref[...], preferred_element_type=F32)
        y_all = hc * lax.rsqrt(var + EPS) * nw_ref[...] * jax.nn.sigmoid(u_ref[0, :, ML_O0:ML_G0])
    y_ref[0] = y_all

    @pl.when(j == nc - 1)
    def _():
        s_fin_ref[0] = s_ref[...]
        m_fin_ref[0] = m_ref[...]


def ml_pass(u, h_other, init, mp, direction, q):
    b, L, _ = u.shape
    q = min(q, L)
    nc = L // q
    finalize = h_other is not None
    in_specs, cidx = _scan_specs(L, q, nc, ML_COLS, direction)
    const2 = lambda bi, j: (0, 0)
    s_spec = pl.BlockSpec((1, ML_HEADS, HEAD_DIM, LANE), lambda bi, j: (bi, 0, 0, 0))
    m_spec = pl.BlockSpec((1, SUBLANE, LANE), lambda bi, j: (bi, 0, 0))
    y_spec = pl.BlockSpec((1, q, ML_WIDTH), lambda bi, j: (bi, cidx(j), 0))
    args = [u, u, u]
    if finalize:
        in_specs.append(y_spec)
        args.append(h_other)
    consts = [mp['cw'], mp['cb'], mp['gb']] + ([mp['nw'], mp['pool']] if finalize else [])
    in_specs += [s_spec, m_spec] + [pl.BlockSpec(t.shape, const2) for t in consts]
    args += [init[0], init[1]] + consts
    y, s_fin, m_fin = pl.pallas_call(
        functools.partial(_ml_kernel, direction=direction, finalize=finalize, q=q, nc=nc),
        out_shape=(jax.ShapeDtypeStruct((b, L, ML_WIDTH), F32),
                   jax.ShapeDtypeStruct((b, ML_HEADS, HEAD_DIM, LANE), F32),
                   jax.ShapeDtypeStruct((b, SUBLANE, LANE), F32)),
        grid=(b, nc),
        in_specs=in_specs,
        out_specs=(y_spec, s_spec, m_spec),
        scratch_shapes=[pltpu.VMEM((ML_HEADS, HEAD_DIM, LANE), F32), pltpu.VMEM((SUBLANE, LANE), F32)],
        compiler_params=_params("parallel", "arbitrary"),
        name="mlstm_scan_%s" % ("fwd" if direction == 0 else "bwd"),
    )(*args)
    return y, (s_fin, m_fin)


def ml_prepare(p):
    gb = p['ml_gate_b'].reshape(1, -1)
    head = np.arange(ML_WIDTH) // HEAD_DIM
    pool = jnp.asarray((head[:, None] == head[None, :]) / HEAD_DIM, BF16)
    return dict(cw=p['ml_conv_w'], cb=p['ml_conv_b'].reshape(1, -1),
                gb=jnp.pad(gb, ((0, 0), (0, LANE - gb.shape[1]))), nw=p['ml_norm_w'].reshape(1, -1), pool=pool)


def ml_mixer(u, mp, init_f, init_b, want_y, q=256):
    hb, fin_b = ml_pass(u, None, init_b, mp, 1, q)
    y, fin_f = ml_pass(u, hb if want_y else None, init_f, mp, 0, q)
    return y, fin_f, fin_b


FFT_L = 4096
FFT_N = 2 * FFT_L
FFT_N2 = 128
FFT_N1 = FFT_N // FFT_N2
FFT_N1H = FFT_L // FFT_N2
FFT_K1 = FFT_N1 // 2 + 1
FFT_R = 80
FFT_UNROLL = 8
FFT_PITCH = FFT_N2 + SUBLANE


def _fft_tables():
    n2 = np.arange(FFT_N2)[:, None, None]
    k1 = np.arange(FFT_K1)[None, :, None]
    n1 = np.arange(FFT_N1H)[None, None, :]
    th = 2 * np.pi * (((FFT_N2 * n1 + n2) * k1) % FFT_N) / FFT_N
    f1 = np.zeros((FFT_N2, FFT_R, FFT_N1H))
    f1[:, 0:2 * FFT_K1:2, :] = np.cos(th)
    f1[:, 1:2 * FFT_K1:2, :] = -np.sin(th)
    wgt = np.where((np.arange(FFT_K1) == 0) | (np.arange(FFT_K1) == FFT_N1 // 2), 1.0, 2.0)[None, :, None] / FFT_N
    g1 = np.zeros((FFT_N2, FFT_N1H, FFT_R))
    g1[:, :, 0:2 * FFT_K1:2] = np.transpose(wgt * np.cos(th), (0, 2, 1))
    g1[:, :, 1:2 * FFT_K1:2] = np.transpose(-wgt * np.sin(th), (0, 2, 1))
    ph = 2 * np.pi * ((np.arange(FFT_N2)[:, None] * np.arange(FFT_N2)[None, :]) % FFT_N2) / FFT_N2
    c, s = np.cos(ph), np.sin(ph)
    f2 = np.block([[c, s], [-s, c]])
    as_bf = lambda a: jnp.asarray(a, F32).astype(BF16)
    return as_bf(f1), as_bf(f2), as_bf(f2.T), as_bf(g1)


def _sld(ref, n2, count):
    rows = pl.ds(n2, count, stride=FFT_PITCH)
    return jnp.concatenate([ref[0, rows, :], ref[1, rows, :]], axis=-1)


def _sst(ref, n2, count, val):
    rows = pl.ds(n2, count, stride=FFT_PITCH)
    ref[0, rows, :] = val[:, 0:LANE]
    ref[1, rows, :] = val[:, LANE:2 * LANE]


def _blk_ld(ref, blk, nblk):
    parts = []
    for k in range(nblk):
        rows = pl.ds(pl.multiple_of((blk + k) * FFT_PITCH, SUBLANE), FFT_N2)
        parts.append(jnp.concatenate([ref[0, rows, :], ref[1, rows, :]], axis=-1))
    return parts[0] if nblk == 1 else jnp.concatenate(parts, axis=0)


def _blk_st(ref, blk, nblk, val):
    for k in range(nblk):
        rows = pl.ds(pl.multiple_of((blk + k) * FFT_PITCH, SUBLANE), FFT_N2)
        ref[0, rows, :] = val[k * FFT_N2:(k + 1) * FFT_N2, 0:LANE]
        ref[1, rows, :] = val[k * FFT_N2:(k + 1) * FFT_N2, LANE:2 * LANE]


def _fft_stage1(z_ref, a_ref, f1_ref):
    def body(n2, carry):
        xs = _sld(z_ref, n2, FFT_N1H).astype(BF16)
        _sst(a_ref, n2, FFT_R, jnp.dot(f1_ref[n2], xs, preferred_element_type=F32))
        return carry
    lax.fori_loop(0, FFT_N2, body, 0, unroll=FFT_UNROLL)


def _spectrum_kernel(x_ref, f1_ref, f2_ref, o_ref, z_ref, a_ref):
    for n1 in range(FFT_N1H):
        _blk_st(z_ref, n1, 1, x_ref[0, n1 * FFT_N2:(n1 + 1) * FFT_N2, :])
    _fft_stage1(z_ref, a_ref, f1_ref)

    def body(k1, carry):
        slab = _blk_ld(a_ref, 2 * k1, 2).astype(BF16)
        o_ref[0, k1] = jnp.dot(f2_ref[...], slab, preferred_element_type=F32)
        return carry
    lax.fori_loop(0, FFT_K1, body, 0, unroll=3)


def spectrum(sig):
    s, L, c = sig.shape
    f1, f2, _, _ = _fft_tables()
    one = pl.Buffered(1)
    return pl.pallas_call(
        _spectrum_kernel,
        out_shape=jax.ShapeDtypeStruct((s, FFT_K1, 2 * FFT_N2, c), F32),
        grid=(s,),
        in_specs=[pl.BlockSpec((1, L, c), lambda i: (i, 0, 0)),
                  pl.BlockSpec(f1.shape, lambda i: (0, 0, 0), pipeline_mode=one),
                  pl.BlockSpec(f2.shape, lambda i: (0, 0), pipeline_mode=one)],
        out_specs=pl.BlockSpec((1, FFT_K1, 2 * FFT_N2, c), lambda i: (i, 0, 0, 0)),
        scratch_shapes=[pltpu.VMEM((2, FFT_N1H * FFT_PITCH, LANE), F32),
                        pltpu.VMEM((2, FFT_R * FFT_PITCH, LANE), F32)],
        compiler_params=pltpu.CompilerParams(dimension_semantics=("arbitrary",), vmem_limit_bytes=VMEM_LIMIT_HY),
        name="hyena_filter_spectrum",
    )(sig, f1, f2)


def _conv3_rows(src, dst, cw, cb, L):
    rows = 2 * FFT_N2
    nchunk = L // rows

    def body(i, carry):
        r0 = pl.multiple_of(i * rows, rows)
        prev_row = jnp.where(i > 0, src[pl.ds(jnp.maximum(r0 - 1, 0), 1), :], 0.0)
        next_row = jnp.where(i < nchunk - 1, src[pl.ds(jnp.minimum(r0 + rows, L - 1), 1), :], 0.0)
        _blk_st(dst, 2 * i, 2, _conv3(src[pl.ds(r0, rows), :], prev_row, next_row, cw, cb, rows))
        return carry
    lax.fori_loop(0, nchunk, body, 0, unroll=2)


def _hyena_kernel(v_ref, g_ref, h_ref, cwv_ref, cbv_ref, cwg_ref, cbg_ref, skip_ref,
                  f1_ref, f2_ref, f2t_ref, g1_ref, o_ref, z_ref, gc_ref, a_ref):
    order = pl.program_id(1)

    @pl.when(order == 0)
    def _():
        _conv3_rows(v_ref.at[0], z_ref, cwv_ref[...], cbv_ref[...], FFT_L)

    _conv3_rows(g_ref.at[0], gc_ref, cwg_ref[0], cbg_ref[0], FFT_L)
    _fft_stage1(z_ref, a_ref, f1_ref)

    def mid(k1, carry):
        x = jnp.dot(f2_ref[...], _blk_ld(a_ref, 2 * k1, 2).astype(BF16), preferred_element_type=F32)
        h = h_ref[0, k1].astype(F32)
        xr, xi, hr, hi = x[:FFT_N2], x[FFT_N2:], h[:FFT_N2], h[FFT_N2:]
        y = jnp.concatenate([xr * hr - xi * hi, xr * hi + xi * hr], axis=0).astype(BF16)
        _blk_st(a_ref, 2 * k1, 2, jnp.dot(f2t_ref[...], y, preferred_element_type=F32))
        return carry
    lax.fori_loop(0, FFT_K1, mid, 0, unroll=3)

    skip = skip_ref[0]

    def last(n2, carry):
        bs = _sld(a_ref, n2, FFT_R).astype(BF16)
        y = jnp.dot(g1_ref[n2], bs, preferred_element_type=F32)
        _sst(z_ref, n2, FFT_N1H, _sld(gc_ref, n2, FFT_N1H) * (y + skip * _sld(z_ref, n2, FFT_N1H)))
        return carry
    lax.fori_loop(0, FFT_N2, last, 0, unroll=FFT_UNROLL)

    @pl.when(order == 1)
    def _():
        for n1 in range(FFT_N1H):
            o_ref[0, n1 * FFT_N2:(n1 + 1) * FFT_N2, :] = _blk_ld(z_ref, n1, 1)


def hyena_long(u, h_spec, cw, cb, skip):
    b, L, _ = u.shape
    c = HYENA_WIDTH
    f1, f2, f2t, g1 = _fft_tables()
    one = pl.Buffered(1)
    cw3 = cw.reshape(3, 3, c).transpose(1, 0, 2)
    cb3 = cb.reshape(3, 1, c)
    return pl.pallas_call(
        _hyena_kernel,
        out_shape=jax.ShapeDtypeStruct((b, L, c), F32),
        grid=(b, 2),
        in_specs=[pl.BlockSpec((1, L, c), lambda bi, o: (bi, 0, 0), pipeline_mode=one),
                  pl.BlockSpec((1, L, c), lambda bi, o: (bi, 0, 1 + o)),
                  pl.BlockSpec((1, FFT_K1, 2 * FFT_N2, c), lambda bi, o: (o, 0, 0, 0)),
                  pl.BlockSpec((3, c), lambda bi, o: (0, 0)),
                  pl.BlockSpec((1, c), lambda bi, o: (0, 0)),
                  pl.BlockSpec((1, 3, c), lambda bi, o: (1 + o, 0, 0)),
                  pl.BlockSpec((1, 1, c), lambda bi, o: (1 + o, 0, 0)),
                  pl.BlockSpec((1, 1, c), lambda bi, o: (o, 0, 0)),
                  pl.BlockSpec(f1.shape, lambda bi, o: (0, 0, 0), pipeline_mode=one),
                  pl.BlockSpec(f2.shape, lambda bi, o: (0, 0), pipeline_mode=one),
                  pl.BlockSpec(f2t.shape, lambda bi, o: (0, 0), pipeline_mode=one),
                  pl.BlockSpec(g1.shape, lambda bi, o: (0, 0, 0), pipeline_mode=one)],
        out_specs=pl.BlockSpec((1, L, c), lambda bi, o: (bi, 0, 0)),
        scratch_shapes=[pltpu.VMEM((2, FFT_N1H * FFT_PITCH, LANE), F32),
                        pltpu.VMEM((2, FFT_N1H * FFT_PITCH, LANE), F32),
                        pltpu.VMEM((2, FFT_R * FFT_PITCH, LANE), F32)],
        compiler_params=pltpu.CompilerParams(dimension_semantics=("parallel", "arbitrary"),
                                             vmem_limit_bytes=VMEM_LIMIT_HY),
        name="hyena_long_conv",
    )(u, u, h_spec, cw3[0], cb3[0], cw3, cb3, skip.reshape(2, 1, c), f1, f2, f2t, g1)


def hyena_filter_spectra(filt):
    sigs = []
    for order in range(HYENA_ORDER):
        h_fwd, h_bwd = filt[:, 0, order], filt[:, 1, order]
        second = jnp.concatenate([jnp.zeros((1, h_fwd.shape[1]), F32), h_bwd[1:][::-1]], axis=0)
        norm = jnp.sum(jnp.abs(h_fwd), axis=0, keepdims=True) + jnp.sum(jnp.abs(second), axis=0, keepdims=True)
        sigs += [h_fwd / norm, second / norm]
    spec = spectrum(jnp.stack(sigs))
    sign = jnp.where(jnp.arange(FFT_K1) % 2 == 0, 1.0, -1.0)[None, :, None, None]
    spec = spec.reshape(HYENA_ORDER, 2, *spec.shape[1:])
    return (spec[:, 0] + sign * spec[:, 1]).astype(BF16)


def _split_bf16(a):
    hi = a.astype(BF16)
    return hi, (a - hi.astype(F32)).astype(BF16)


def _norm_router_kernel(*refs, n_lat, two_src):
    if two_src:
        xl_ref, xc_ref, nw_ref, sh_ref, sc_ref, whi_ref, wlo_ref, br_ref, h_ref, lg_ref = refs
        x = jnp.where(pl.program_id(0) < n_lat, xl_ref[...], xc_ref[...])
    else:
        xl_ref, nw_ref, sh_ref, sc_ref, whi_ref, wlo_ref, br_ref, h_ref, lg_ref = refs
        x = xl_ref[...]
    h = _modnorm(x, nw_ref[...], sh_ref[0], sc_ref[0])
    h_ref[...] = h
    h_hi, h_lo = _split_bf16(h)
    lg = (jnp.dot(h_hi, whi_ref[...], preferred_element_type=F32)
          + jnp.dot(h_hi, wlo_ref[...], preferred_element_type=F32)
          + jnp.dot(h_lo, whi_ref[...], preferred_element_type=F32)) + br_ref[...]
    lane = lax.broadcasted_iota(jnp.int32, lg.shape, 1)
    first = lambda hit: jnp.min(jnp.where(hit, lane, ROUTER_COLS), axis=-1, keepdims=True)
    gl = jnp.where(lane < N_GROUPS, lg, -jnp.inf)
    gmax = jnp.max(gl, axis=-1, keepdims=True)
    grp = first(gl == gmax)
    grp_p = 1.0 / jnp.sum(jnp.exp(gl - gmax), axis=-1, keepdims=True)
    lo = N_GROUPS + grp * EXPERTS_PER_GROUP
    el = jnp.where((lane >= lo) & (lane < lo + EXPERTS_PER_GROUP), lg, -jnp.inf)
    e1 = jnp.max(el, axis=-1, keepdims=True)
    i1 = first(el == e1)
    el2 = jnp.where(lane == i1, -jnp.inf, el)
    e2 = jnp.max(el2, axis=-1, keepdims=True)
    i2 = first(el2 == e2)
    r = jnp.exp(e2 - e1)
    w1 = grp_p / (1.0 + r)
    w2 = w1 * r
    vals = [(i1 - N_GROUPS).astype(F32), (i2 - N_GROUPS).astype(F32), w1, w2]
    out = jnp.zeros(lg.shape, F32)
    for k, val in enumerate(vals):
        out = jnp.where(lane == k, val, out)
    lg_ref[...] = out


def norm_router(xl, xc, nw, mod_l, mod_c, w_router, b_router, tm=512):
    b, L, d = xl.shape
    two_src = xc is not None
    n_lat = b * L // tm
    per_batch = L // tm
    n_ctx = (xc.shape[0] * xc.shape[1]) // tm if two_src else 0
    w_hi, w_lo = _split_bf16(w_router)
    const = lambda i: (0, 0)
    if two_src:
        shift = jnp.concatenate([mod_l[0], mod_c[0]], axis=0)
        scale = jnp.concatenate([mod_l[1], mod_c[1]], axis=0)
        mod_map = lambda i: (jnp.where(i < n_lat, i // per_batch, b), 0, 0)
        srcs = [xl.reshape(b * L, d), xc.reshape(-1, d)]
        src_specs = [pl.BlockSpec((tm, d), lambda i: (jnp.minimum(i, n_lat - 1), 0)),
                     pl.BlockSpec((tm, d), lambda i: (jnp.maximum(i - n_lat, 0), 0))]
    else:
        shift, scale = mod_l
        mod_map = lambda i: (i // per_batch, 0, 0)
        srcs = [xl.reshape(b * L, d)]
        src_specs = [pl.BlockSpec((tm, d), lambda i: (i, 0))]
    n_tok = (n_lat + n_ctx) * tm
    return pl.pallas_call(
        functools.partial(_norm_router_kernel, n_lat=n_lat, two_src=two_src),
        out_shape=(jax.ShapeDtypeStruct((n_tok, d), F32), jax.ShapeDtypeStruct((n_tok, ROUTER_COLS), F32)),
        grid=(n_lat + n_ctx,),
        in_specs=src_specs + [pl.BlockSpec((1, d), const), pl.BlockSpec((1, 1, d), mod_map),
                              pl.BlockSpec((1, 1, d), mod_map), pl.BlockSpec((d, ROUTER_COLS), const),
                              pl.BlockSpec((d, ROUTER_COLS), const), pl.BlockSpec((1, ROUTER_COLS), const)],
        out_specs=(pl.BlockSpec((tm, d), lambda i: (i, 0)), pl.BlockSpec((tm, ROUTER_COLS), lambda i: (i, 0))),
        compiler_params=_params("arbitrary"),
        name="moe_norm_router",
    )(*srcs, nw.reshape(1, d), shift, scale, w_hi, w_lo, b_router)


def _expert_ffn_kernel(te_ref, tv_ref, x_ref, rw_ref, wg_ref, wu_ref, wd_ref, o_ref):
    i = pl.program_id(0)

    @pl.when(tv_ref[i] > 0)
    def _():
        x = x_ref[...].astype(BF16)
        g = jnp.dot(x, wg_ref[0].astype(BF16), preferred_element_type=F32)
        u = jnp.dot(x, wu_ref[0].astype(BF16), preferred_element_type=F32)
        hid = _silu(g) * u * rw_ref[...]
        o_ref[...] = jnp.dot(hid.astype(BF16), wd_ref[0].astype(BF16), preferred_element_type=F32)

    @pl.when(tv_ref[i] == 0)
    def _():
        o_ref[...] = jnp.zeros_like(o_ref)


def expert_ffn(x_sorted, row_w, tile_expert, tile_valid, w_gate, w_up, w_down, tm):
    r, d = x_sorted.shape
    f = w_gate.shape[-1]
    grid_spec = pltpu.PrefetchScalarGridSpec(
        num_scalar_prefetch=2,
        grid=(r // tm,),
        in_specs=[pl.BlockSpec((tm, d), lambda i, te, tv: (i, 0)),
                  pl.BlockSpec((tm, 1), lambda i, te, tv: (i, 0)),
                  pl.BlockSpec((1, d, f), lambda i, te, tv: (te[i], 0, 0)),
                  pl.BlockSpec((1, d, f), lambda i, te, tv: (te[i], 0, 0)),
                  pl.BlockSpec((1, f, d), lambda i, te, tv: (te[i], 0, 0))],
        out_specs=pl.BlockSpec((tm, d), lambda i, te, tv: (i, 0)),
    )
    return pl.pallas_call(
        _expert_ffn_kernel,
        out_shape=jax.ShapeDtypeStruct((r, d), F32),
        grid_spec=grid_spec,
        compiler_params=_params("arbitrary"),
        name="moe_expert_ffn",
    )(tile_expert, tile_valid, x_sorted, row_w, w_gate, w_up, w_down)


def moe_apply(h_tokens, routed, w_gate, w_up, w_down, layer, n_lat, tm=256):
    t, d = h_tokens.shape
    e_idx, e_w = routed[:, 0:2].astype(jnp.int32), routed[:, 2:4]
    flat_e = e_idx.reshape(-1).astype(jnp.int32)
    n_pairs = 2 * t
    sorted_e, order = lax.sort((flat_e, jnp.arange(n_pairs, dtype=jnp.int32)), num_keys=1, is_stable=True)
    experts = jnp.arange(N_EXPERTS, dtype=jnp.int32)
    seg_end = jnp.sum((flat_e[None, :] <= experts[:, None]).astype(jnp.int32), axis=1)
    counts = seg_end - jnp.concatenate([jnp.zeros((1,), jnp.int32), seg_end[:-1]])
    seg_start = seg_end - counts
    padded = (counts + tm - 1) // tm * tm
    pad_end = jnp.cumsum(padded)
    pad_start = pad_end - padded
    n_rows = n_pairs + N_EXPERTS * tm
    tile_start = jnp.arange(n_rows // tm, dtype=jnp.int32) * tm
    tile_expert = jnp.minimum(jnp.sum((pad_end[None, :] <= tile_start[:, None]).astype(jnp.int32), axis=1),
                              N_EXPERTS - 1)
    tile_valid = (tile_start < pad_end[-1]).astype(jnp.int32)
    off = (tile_start - pad_start[tile_expert])[:, None] + jnp.arange(tm, dtype=jnp.int32)[None, :]
    valid = (off < counts[tile_expert][:, None]) & (tile_valid[:, None] > 0)
    spread = jnp.arange(n_rows, dtype=jnp.int32).reshape(-1, tm) % n_pairs
    pair = jnp.where(valid, seg_start[tile_expert][:, None] + off, spread).reshape(-1)
    src = order[pair]
    row_token = src // 2
    row_w = jnp.where(valid.reshape(-1), e_w.reshape(-1)[src], 0.0)
    dest = pad_start[sorted_e] + (jnp.arange(n_pairs, dtype=jnp.int32) - seg_start[sorted_e])
    _, pos = lax.sort((order, dest), num_keys=1)
    pos = pos.reshape(t, 2)
    x_sorted = h_tokens[row_token]
    wg = w_gate.reshape(-1, d, EXPERT_HIDDEN)
    wu = w_up.reshape(-1, d, EXPERT_HIDDEN)
    wd = w_down.reshape(-1, EXPERT_HIDDEN, d)
    y_sorted = expert_ffn(x_sorted, row_w[:, None], tile_expert + layer * N_EXPERTS, tile_valid, wg, wu, wd, tm)
    lat = (y_sorted[pos[:n_lat, 0]], y_sorted[pos[:n_lat, 1]])
    rest = (y_sorted[pos[n_lat:, 0]], y_sorted[pos[n_lat:, 1]]) if t > n_lat else None
    return lat, rest


def _final_kernel(x_ref, ya_ref, yb_ref, g_ref, w_ref, o_ref):
    x = x_ref[0] + g_ref[0] * (ya_ref[0] + yb_ref[0])
    o_ref[0] = x * lax.rsqrt(jnp.mean(x * x, axis=-1, keepdims=True) + EPS) * w_ref[...]


def final_norm(x, ya, yb, gate, w, tm=512):
    b, L, d = x.shape
    tok = pl.BlockSpec((1, tm, d), lambda bi, i: (bi, i, 0))
    return pl.pallas_call(
        _final_kernel,
        out_shape=jax.ShapeDtypeStruct((b, L, d), F32),
        grid=(b, L // tm),
        in_specs=[tok, tok, tok, pl.BlockSpec((1, 1, d), _mod_map(gate, b)),
                  pl.BlockSpec((1, d), lambda bi, i: (0, 0))],
        out_specs=tok,
        compiler_params=_params("parallel", "arbitrary"),
        name="final_rmsnorm",
    )(x, ya, yb, gate, w.reshape(1, d))


def _short_conv(x, w, b):
    L = x.shape[1]
    pad = w.shape[0] // 2
    xp = jnp.pad(x, ((0, 0), (pad, pad), (0, 0)))
    return sum(xp[:, k:k + L] * w[k] for k in range(w.shape[0])) + b


def _hyena_filters(L, p):
    t = jnp.arange(L, dtype=F32)
    t_unit = t / float(max(L - 1, 1))
    bands = jnp.linspace(1e-4, HYENA_BANDS - 1, HYENA_BANDS, dtype=F32)
    ang = (2 * math.pi / L) * t[:, None] * bands[None, :]
    feats = jnp.concatenate([t_unit[:, None], jnp.cos(ang), -jnp.sin(ang)], axis=-1)
    freq = p['hy_freq']
    hdn = jnp.sin(freq * (jnp.dot(feats, p['hy_pos_w1'], precision=HI) + p['hy_pos_b1']))
    hdn = jnp.sin(freq * (jnp.dot(hdn, p['hy_pos_w2'], precision=HI) + p['hy_pos_b2']))
    filt = jnp.dot(hdn, p['hy_pos_w3'], precision=HI).reshape(L, 2, HYENA_ORDER, HYENA_WIDTH)
    window = jnp.exp(-t_unit[:, None, None, None] * jnp.abs(p['hy_decay']))
    return filt * window


def _bidir_long_conv(z, h_fwd, h_bwd):
    L, ch = h_fwd.shape
    two_sided = jnp.concatenate([h_fwd, jnp.zeros((1, ch), h_fwd.dtype), h_bwd[1:][::-1]], axis=0)
    two_sided = two_sided / jnp.sum(jnp.abs(two_sided), axis=0, keepdims=True)
    zf = jnp.fft.rfft(z, n=2 * L, axis=1)
    hf = jnp.fft.rfft(two_sided, n=2 * L, axis=0)
    return jnp.fft.irfft(zf * hf[None], n=2 * L, axis=1)[:, :L]


def _hyena_mixer(u, p):
    uc = _short_conv(u, p['hy_conv_w'], p['hy_conv_b'])
    v, *gates = jnp.split(uc, HYENA_ORDER + 1, axis=-1)
    filt = _hyena_filters(u.shape[1], p)
    z = v
    for order, gate in enumerate(gates):
        z = gate * (_bidir_long_conv(z, filt[:, 0, order], filt[:, 1, order]) + p['hy_skip'][order] * z)
    return z


def _regroup_in_weight(w_in):
    sizes = (SSD_WIDTH, SSD_CONV_CH, 2 * SSD_HEADS, HY_COLS, 2 * ML_WIDTH, ML_WIDTH, ML_WIDTH, 4 * ML_HEADS)
    parts, s = [], 0
    for n in sizes:
        parts.append(jnp.pad(w_in[:, s:s + n], ((0, 0), (0, -n % LANE))))
        s += n
    return jnp.concatenate(parts, axis=1).astype(BF16)


def kernel(x, c, ctx, c_ctx, w_mod, b_mod, norm1_w, norm2_w, w_in, w_out, ssd_conv_w, ssd_conv_b, ssd_dt_bias, ssd_a_log, ssd_d, ssd_norm_w, hy_conv_w, hy_conv_b, hy_pos_w1, hy_pos_b1, hy_pos_w2, hy_pos_b2, hy_pos_w3, hy_freq, hy_decay, hy_skip, ml_conv_w, ml_conv_b, ml_gate_b, ml_norm_w, grp_router_w, grp_router_b, exp_router_w, exp_router_b, moe_w_gate, moe_w_up, moe_w_down, final_norm_w):
    layer_params = dict(
        ssd_conv_w=ssd_conv_w, ssd_conv_b=ssd_conv_b, ssd_dt_bias=ssd_dt_bias, ssd_a_log=ssd_a_log,
        ssd_d=ssd_d, ssd_norm_w=ssd_norm_w, hy_conv_w=hy_conv_w, hy_conv_b=hy_conv_b,
        hy_pos_w1=hy_pos_w1, hy_pos_b1=hy_pos_b1, hy_pos_w2=hy_pos_w2, hy_pos_b2=hy_pos_b2,
        hy_pos_w3=hy_pos_w3, hy_freq=hy_freq, hy_decay=hy_decay, hy_skip=hy_skip,
        ml_conv_w=ml_conv_w, ml_conv_b=ml_conv_b, ml_gate_b=ml_gate_b, ml_norm_w=ml_norm_w)
    bsz, seq, d = x.shape
    n_ctx = ctx.shape[1]
    xl, xc = x, ctx
    moe_l = moe_c = None
    ssd0 = jnp.zeros((bsz, SSD_HEADS, HEAD_DIM, SSD_STATE), F32)
    ml0 = (jnp.zeros((bsz, ML_HEADS, HEAD_DIM, LANE), F32), jnp.zeros((bsz, SUBLANE, LANE), F32))
    c_rows = jnp.concatenate([c, c_ctx[None, :], jnp.zeros((SUBLANE - bsz - 1, d), F32)], axis=0)
    for i in range(DEPTH):
        last = i == DEPTH - 1
        p = {name: arr[i] for name, arr in layer_params.items()}
        sp, mp = ssd_prepare(p), ml_prepare(p)
        mod = modulation(c_rows, w_mod[i], b_mod[i]).reshape(SUBLANE, N_MOD, 1, d)
        mod_l = [mod[:bsz, k] for k in range(N_MOD)]
        mod_c = [mod[bsz:bsz + 1, k] for k in range(N_MOD)]
        w_in_p = _regroup_in_weight(w_in[i])
        w_out_b = w_out[i].astype(BF16)
        w_router = jnp.pad(jnp.concatenate([grp_router_w[i], exp_router_w[i]], axis=1),
                           ((0, 0), (0, ROUTER_COLS - N_GROUPS - N_EXPERTS)))
        b_router = jnp.pad(jnp.concatenate([grp_router_b[i], exp_router_b[i]]),
                           (0, ROUTER_COLS - N_GROUPS - N_EXPERTS)).reshape(1, ROUTER_COLS)

        uc_ssd, uc_hy, uc_ml, xc = norm_proj(xc, moe_c, norm1_w[i], mod_c[0], mod_c[1], w_in_p)
        yc_ssd, ssd_f, ssd_b = ssd_mixer(uc_ssd, sp, ssd0, ssd0, not last)
        yc_ml, ml_f, ml_b = ml_mixer(uc_ml, mp, ml0, ml0, not last)
        col_major = i % 2 == 1
        ul_ssd, ul_hy, ul_ml, xl = norm_proj(xl, moe_l, norm1_w[i], mod_l[0], mod_l[1], w_in_p, col_major)
        yl_ssd, _, _ = ssd_mixer(ul_ssd, sp, ssd_f, ssd_b, True)
        yl_ml, _, _ = ml_mixer(ul_ml, mp, ml_f, ml_b, True)
        h_spec = hyena_filter_spectra(_hyena_filters(seq, p))
        yl_hy = hyena_long(ul_hy, h_spec, p['hy_conv_w'], p['hy_conv_b'], p['hy_skip'])
        xl = out_proj(yl_ssd, yl_hy, yl_ml, xl, mod_l[2], w_out_b, col_major)
        if not last:
            yc_hy = _hyena_mixer(uc_hy, p)
            xc = out_proj(yc_ssd, yc_hy, yc_ml, xc, mod_c[2], w_out_b)
        h_all, routed = norm_router(xl, None if last else xc, norm2_w[i], (mod_l[3], mod_l[4]), (mod_c[3], mod_c[4]),
                                    w_router, b_router)
        lat, rest = moe_apply(h_all, routed, moe_w_gate, moe_w_up, moe_w_down, i, bsz * seq)
        moe_l = (lat[0].reshape(bsz, seq, d), lat[1].reshape(bsz, seq, d), mod_l[5])
        if not last:
            moe_c = (rest[0].reshape(bsz, n_ctx, d), rest[1].reshape(bsz, n_ctx, d), mod_c[5])
    return final_norm(xl, *moe_l, final_norm_w)
```

```python
import functools
import math

import jax
import jax.numpy as jnp
import numpy as np
from jax import lax
from jax.experimental import pallas as pl
from jax.experimental.pallas import tpu as pltpu

D_MODEL = 1024
DEPTH = 2
GRID_W = 64
HEAD_DIM = 64
SSD_WIDTH = 384
SSD_HEADS = SSD_WIDTH // HEAD_DIM
SSD_GROUPS = 2
SSD_STATE = 64
HYENA_WIDTH = 256
HYENA_ORDER = 2
HYENA_BANDS = 16
ML_WIDTH = 384
ML_HEADS = ML_WIDTH // HEAD_DIM
N_GROUPS = 4
EXPERTS_PER_GROUP = 8
N_EXPERTS = N_GROUPS * EXPERTS_PER_GROUP
EXPERT_HIDDEN = 256
N_MOD = 6
EPS = 1e-6

LANE = 128
SUBLANE = 8
VMEM_LIMIT = 48 * 1024 * 1024
VMEM_LIMIT_HY = 56 * 1024 * 1024

SSD_CONV_CH = SSD_WIDTH + 2 * SSD_GROUPS * SSD_STATE
SSD_XBC0 = SSD_WIDTH
SSD_DT0 = SSD_XBC0 + SSD_CONV_CH
SSD_COLS = SSD_DT0 + LANE
HY_COLS = (HYENA_ORDER + 1) * HYENA_WIDTH
ML_V0 = 2 * ML_WIDTH
ML_O0 = ML_V0 + ML_WIDTH
ML_G0 = ML_O0 + ML_WIDTH
ML_COLS = ML_G0 + LANE
ROUTER_COLS = LANE

F32 = jnp.float32
BF16 = jnp.bfloat16
HI = lax.Precision.HIGHEST


def _params(*sem):
    return pltpu.CompilerParams(dimension_semantics=sem, vmem_limit_bytes=VMEM_LIMIT)


def _silu(x):
    return x * jax.nn.sigmoid(x)


def _softplus(x):
    return jnp.maximum(x, 0.0) + jnp.log(1.0 + jnp.exp(-jnp.abs(x)))


def _log_sigmoid(x):
    return jnp.minimum(x, 0.0) - jnp.log(1.0 + jnp.exp(-jnp.abs(x)))


def _mod_kernel(c_ref, w_ref, b_ref, o_ref):
    o_ref[...] = jnp.dot(_silu(c_ref[...]), w_ref[...], preferred_element_type=F32, precision=HI) + b_ref[...]


def modulation(c_rows, w_mod, b_mod):
    n = w_mod.shape[1]
    tn = 1536
    return pl.pallas_call(
        _mod_kernel,
        out_shape=jax.ShapeDtypeStruct((c_rows.shape[0], n), F32),
        grid=(n // tn,),
        in_specs=[pl.BlockSpec(c_rows.shape, lambda j: (0, 0)),
                  pl.BlockSpec((w_mod.shape[0], tn), lambda j: (0, j)),
                  pl.BlockSpec((1, tn), lambda j: (0, j))],
        out_specs=pl.BlockSpec((c_rows.shape[0], tn), lambda j: (0, j)),
        compiler_params=_params("arbitrary"),
        name="adaln_modulation",
    )(c_rows, w_mod, b_mod.reshape(1, n))


def _modnorm(x, nw, shift, scale):
    y = x * lax.rsqrt(jnp.mean(x * x, axis=-1, keepdims=True) + EPS) * nw
    return y * (1.0 + scale) + shift


def _mod_map(mod, b):
    return (lambda bi, i: (bi, 0, 0)) if mod.shape[0] == b else (lambda bi, i: (0, 0, 0))


def _tok_view(x, col_major):
    b, L, d = x.shape
    return x.reshape(b, L // GRID_W, GRID_W, d) if col_major else x


def _tok_spec(L, d, tm, col_major):
    if col_major:
        assert tm == (L // GRID_W) * SUBLANE
        return pl.BlockSpec((1, L // GRID_W, SUBLANE, d), lambda bi, i: (bi, 0, i, 0))
    return pl.BlockSpec((1, tm, d), lambda bi, i: (bi, i, 0))


def _tok_load(ref, col_major):
    if not col_major:
        return ref[0]
    return jnp.concatenate([ref[0, :, j, :] for j in range(ref.shape[2])], axis=0)


def _tok_store(ref, val, col_major):
    if not col_major:
        ref[0] = val
        return
    rows = ref.shape[1]
    for j in range(ref.shape[2]):
        ref[0, :, j, :] = val[j * rows:(j + 1) * rows]


def _norm_proj_kernel(*refs, col_major, fuse_moe):
    if fuse_moe:
        x_ref, ya_ref, yb_ref, g_ref, nw_ref, sh_ref, sc_ref, w_ref, ssd_ref, hy_ref, ml_ref, xo_ref = refs
    else:
        x_ref, nw_ref, sh_ref, sc_ref, w_ref, ssd_ref, hy_ref, ml_ref = refs
    x = _tok_load(x_ref, col_major)
    if fuse_moe:
        x = x + g_ref[0] * (_tok_load(ya_ref, col_major) + _tok_load(yb_ref, col_major))
        _tok_store(xo_ref, x, col_major)
    h = _modnorm(x, nw_ref[...], sh_ref[0], sc_ref[0])
    u = jnp.dot(h.astype(BF16), w_ref[...], preferred_element_type=F32)
    ssd_ref[0] = u[:, 0:SSD_COLS]
    hy_ref[0] = u[:, SSD_COLS:SSD_COLS + HY_COLS]
    ml_ref[0] = u[:, SSD_COLS + HY_COLS:]


def norm_proj(x, moe, nw, shift, scale, w_bf16, col_major=False, tm=256):
    b, L, d = x.shape
    n = w_bf16.shape[1]
    fuse_moe = moe is not None
    tm = (L // GRID_W) * SUBLANE if col_major else tm
    tok = _tok_spec(L, d, tm, col_major)
    row = lambda bi, i: (bi, i, 0)
    const2 = lambda bi, i: (0, 0)
    args, in_specs = [_tok_view(x, col_major)], [tok]
    if fuse_moe:
        ya, yb, gate = moe
        args += [_tok_view(ya, col_major), _tok_view(yb, col_major), gate]
        in_specs += [tok, tok, pl.BlockSpec((1, 1, d), _mod_map(gate, b))]
    args += [nw.reshape(1, d), shift, scale, w_bf16]
    in_specs += [pl.BlockSpec((1, d), const2), pl.BlockSpec((1, 1, d), _mod_map(shift, b)),
                 pl.BlockSpec((1, 1, d), _mod_map(scale, b)), pl.BlockSpec((d, n), const2)]
    out_shape = [jax.ShapeDtypeStruct((b, L, SSD_COLS), F32), jax.ShapeDtypeStruct((b, L, HY_COLS), F32),
                 jax.ShapeDtypeStruct((b, L, ML_COLS), F32)]
    out_specs = [pl.BlockSpec((1, tm, SSD_COLS), row), pl.BlockSpec((1, tm, HY_COLS), row),
                 pl.BlockSpec((1, tm, ML_COLS), row)]
    if fuse_moe:
        out_shape.append(jax.ShapeDtypeStruct(args[0].shape, F32))
        out_specs.append(tok)
    outs = pl.pallas_call(
        functools.partial(_norm_proj_kernel, col_major=col_major, fuse_moe=fuse_moe),
        out_shape=tuple(out_shape),
        grid=(b, L // tm),
        in_specs=in_specs,
        out_specs=tuple(out_specs),
        compiler_params=_params("parallel", "arbitrary"),
        name="norm_in_proj",
    )(*args)
    return (*outs[:3], outs[3].reshape(b, L, d) if fuse_moe else x)


def _out_proj_kernel(ys_ref, yh_ref, ym_ref, x_ref, g_ref, w_ref, o_ref, *, col_major):
    y = jnp.concatenate([ys_ref[0], yh_ref[0], ym_ref[0]], axis=-1).astype(BF16)
    r = _tok_load(x_ref, col_major) + g_ref[0] * jnp.dot(y, w_ref[...], preferred_element_type=F32)
    _tok_store(o_ref, r, col_major)


def out_proj(y_ssd, y_hy, y_ml, x, gate, w_bf16, col_major=False, tm=512):
    b, L, d = x.shape
    tm = (L // GRID_W) * SUBLANE if col_major else min(tm, L)
    row = lambda bi, i: (bi, i, 0)
    tok = _tok_spec(L, d, tm, col_major)
    xv = _tok_view(x, col_major)
    return pl.pallas_call(
        functools.partial(_out_proj_kernel, col_major=col_major),
        out_shape=jax.ShapeDtypeStruct(xv.shape, F32),
        grid=(b, L // tm),
        in_specs=[pl.BlockSpec((1, tm, SSD_WIDTH), row), pl.BlockSpec((1, tm, HYENA_WIDTH), row),
                  pl.BlockSpec((1, tm, ML_WIDTH), row), tok,
                  pl.BlockSpec((1, 1, d), _mod_map(gate, b)),
                  pl.BlockSpec(w_bf16.shape, lambda bi, i: (0, 0))],
        out_specs=tok,
        compiler_params=_params("parallel", "arbitrary"),
        name="out_proj_residual",
    )(y_ssd, y_hy, y_ml, xv, gate, w_bf16).reshape(b, L, d)


def _conv3(xr, prev_row, next_row, cw, cb, q):
    rid = lax.broadcasted_iota(jnp.int32, (q, 1), 0)
    x_prev = jnp.where(rid == 0, prev_row, pltpu.roll(xr, 1, axis=0))
    x_next = jnp.where(rid == q - 1, next_row, pltpu.roll(xr, q - 1, axis=0))
    return x_prev * cw[0:1] + xr * cw[1:2] + x_next * cw[2:3] + cb


def _scan_mask(q, direction):
    li = lax.broadcasted_iota(jnp.int32, (q, q), 0)
    si = lax.broadcasted_iota(jnp.int32, (q, q), 1)
    return (si <= li) if direction == 0 else (si >= li)


def _running_max(x, direction, q):
    rid = lax.broadcasted_iota(jnp.int32, (q, 1), 0)
    s = 1
    while s < q:
        if direction == 0:
            x = jnp.where(rid >= s, jnp.maximum(x, pltpu.roll(x, s, axis=0)), x)
        else:
            x = jnp.where(rid < q - s, jnp.maximum(x, pltpu.roll(x, q - s, axis=0)), x)
        s *= 2
    return x


def _scan_specs(L, q, nc, cols, direction):
    hb = q // SUBLANE
    nrb = L // SUBLANE
    cidx = (lambda j: j) if direction == 0 else (lambda j: nc - 1 - j)
    specs = [pl.BlockSpec((1, q, cols), lambda bi, j: (bi, cidx(j), 0)),
             pl.BlockSpec((1, SUBLANE, cols), lambda bi, j: (bi, jnp.maximum(cidx(j) * hb - 1, 0), 0)),
             pl.BlockSpec((1, SUBLANE, cols), lambda bi, j: (bi, jnp.minimum((cidx(j) + 1) * hb, nrb - 1), 0))]
    return specs, cidx


def _ssd_kernel(*refs, direction, finalize, q, nc):
    if finalize:
        (u_ref, prev_ref, next_ref, yb_ref, init_ref, cw_ref, cb_ref, dtb_ref, a_ref, d_ref, nw_ref,
         y_ref, fin_ref, state_ref) = refs
    else:
        (u_ref, prev_ref, next_ref, init_ref, cw_ref, cb_ref, dtb_ref, a_ref,
         y_ref, fin_ref, state_ref) = refs
    j = pl.program_id(1)
    c = j if direction == 0 else nc - 1 - j

    @pl.when(j == 0)
    def _():
        state_ref[...] = init_ref[0]

    prev_row = jnp.where(c > 0, prev_ref[0, SUBLANE - 1:SUBLANE, SSD_XBC0:SSD_DT0], 0.0)
    next_row = jnp.where(c < nc - 1, next_ref[0, 0:1, SSD_XBC0:SSD_DT0], 0.0)
    xc = _silu(_conv3(u_ref[0, :, SSD_XBC0:SSD_DT0], prev_row, next_row, cw_ref[...], cb_ref[...], q))

    dt = _softplus(u_ref[0, :, SSD_DT0:SSD_COLS] + dtb_ref[...])
    mask = _scan_mask(q, direction)
    cum = jnp.dot(mask.astype(F32), dt * a_ref[...], preferred_element_type=F32, precision=HI)
    cum_t = cum.T
    end = q - 1 if direction == 0 else 0

    ys = []
    for g in range(SSD_GROUPS):
        b0 = SSD_WIDTH + g * SSD_STATE
        c0 = SSD_WIDTH + (SSD_GROUPS + g) * SSD_STATE
        bm_t = xc[:, b0:b0 + SSD_STATE].T
        cm = xc[:, c0:c0 + SSD_STATE].astype(BF16)
        scores = jnp.dot(cm, bm_t.astype(BF16), preferred_element_type=F32)
        for h in range(g * (SSD_HEADS // SSD_GROUPS), (g + 1) * (SSD_HEADS // SSD_GROUPS)):
            hl = direction * SSD_HEADS + h
            col = cum[:, hl:hl + 1]
            row = cum_t[hl:hl + 1, :]
            seg = jnp.exp(jnp.where(mask, col - row, -jnp.inf))
            xdt = (xc[:, h * HEAD_DIM:(h + 1) * HEAD_DIM] * dt[:, hl:hl + 1]).astype(BF16)
            y = jnp.dot((scores * seg).astype(BF16), xdt, preferred_element_type=F32)
            st = state_ref[h]
            y = y + jnp.dot(cm, st.astype(BF16), preferred_element_type=F32) * jnp.exp(col)
            tot = cum[end:end + 1, hl:hl + 1]
            upd = jnp.dot((bm_t * jnp.exp(tot - row)).astype(BF16), xdt, preferred_element_type=F32)
            state_ref[h] = st * jnp.exp(tot) + upd
            ys.append(y)
    y_all = jnp.concatenate(ys, axis=-1)
    if finalize:
        t = (y_all + yb_ref[0] + xc[:, 0:SSD_WIDTH] * d_ref[...]) * _silu(u_ref[0, :, 0:SSD_WIDTH])
        y_all = t * lax.rsqrt(jnp.mean(t * t, axis=-1, keepdims=True) + EPS) * nw_ref[...]
    y_ref[0] = y_all

    @pl.when(j == nc - 1)
    def _():
        fin_ref[0] = state_ref[...]


def ssd_pass(u, y_other, init, sp, direction, q):
    b, L, _ = u.shape
    q = min(q, L)
    nc = L // q
    finalize = y_other is not None
    in_specs, cidx = _scan_specs(L, q, nc, SSD_COLS, direction)
    const2 = lambda bi, j: (0, 0)
    st_spec = pl.BlockSpec((1, SSD_HEADS, HEAD_DIM, SSD_STATE), lambda bi, j: (bi, 0, 0, 0))
    y_spec = pl.BlockSpec((1, q, SSD_WIDTH), lambda bi, j: (bi, cidx(j), 0))
    args = [u, u, u]
    if finalize:
        in_specs.append(y_spec)
        args.append(y_other)
    consts = [sp['cw'], sp['cb'], sp['dtb'], sp['a']] + ([sp['d'], sp['nw']] if finalize else [])
    in_specs += [st_spec] + [pl.BlockSpec(t.shape, const2) for t in consts]
    args += [init] + consts
    return pl.pallas_call(
        functools.partial(_ssd_kernel, direction=direction, finalize=finalize, q=q, nc=nc),
        out_shape=(jax.ShapeDtypeStruct((b, L, SSD_WIDTH), F32),
                   jax.ShapeDtypeStruct((b, SSD_HEADS, HEAD_DIM, SSD_STATE), F32)),
        grid=(b, nc),
        in_specs=in_specs,
        out_specs=(y_spec, st_spec),
        scratch_shapes=[pltpu.VMEM((SSD_HEADS, HEAD_DIM, SSD_STATE), F32)],
        compiler_params=_params("parallel", "arbitrary"),
        name="ssd_scan_%s" % ("fwd" if direction == 0 else "bwd"),
    )(*args)


def ssd_prepare(p):
    pad = lambda v: jnp.pad(v.reshape(1, -1), ((0, 0), (0, LANE - v.size)))
    return dict(cw=p['ssd_conv_w'], cb=p['ssd_conv_b'].reshape(1, -1),
                dtb=pad(p['ssd_dt_bias']), a=pad(-jnp.exp(p['ssd_a_log'])),
                d=jnp.repeat(p['ssd_d'], HEAD_DIM).reshape(1, -1), nw=p['ssd_norm_w'].reshape(1, -1))


def ssd_mixer(u, sp, init_f, init_b, want_y, q=256):
    yb, fin_b = ssd_pass(u, None, init_b, sp, 1, q)
    y, fin_f = ssd_pass(u, yb if want_y else None, init_f, sp, 0, q)
    return y, fin_f, fin_b


def _ml_kernel(*refs, direction, finalize, q, nc):
    if finalize:
        (u_ref, prev_ref, next_ref, hb_ref, s_init_ref, m_init_ref, cw_ref, cb_ref, gb_ref, nw_ref, pool_ref,
         y_ref, s_fin_ref, m_fin_ref, s_ref, m_ref) = refs
    else:
        (u_ref, prev_ref, next_ref, s_init_ref, m_init_ref, cw_ref, cb_ref, gb_ref,
         y_ref, s_fin_ref, m_fin_ref, s_ref, m_ref) = refs
    j = pl.program_id(1)
    c = j if direction == 0 else nc - 1 - j

    @pl.when(j == 0)
    def _():
        s_ref[...] = s_init_ref[0]
        m_ref[...] = m_init_ref[0]

    prev_row = jnp.where(c > 0, prev_ref[0, SUBLANE - 1:SUBLANE, 0:ML_V0], 0.0)
    next_row = jnp.where(c < nc - 1, next_ref[0, 0:1, 0:ML_V0], 0.0)
    qk = _silu(_conv3(u_ref[0, :, 0:ML_V0], prev_row, next_row, cw_ref[...], cb_ref[...], q))
    v = u_ref[0, :, ML_V0:ML_O0]

    gb = u_ref[0, :, ML_G0:ML_COLS] + gb_ref[...]
    mask = _scan_mask(q, direction)
    cum = jnp.dot(mask.astype(F32), _log_sigmoid(gb), preferred_element_type=F32, precision=HI)
    ig = pltpu.roll(gb, ML_HEADS, axis=1)
    end = q - 1 if direction == 0 else 0
    m_prev = m_ref[0:1, :]
    tot = cum[end:end + 1, :]
    w_end = tot - cum + ig
    m_loc = jnp.max(w_end, axis=0, keepdims=True)
    e_end = jnp.exp(w_end - m_loc)
    m_new = jnp.maximum(tot + m_prev, m_loc)
    a_prev = jnp.exp(tot + m_prev - m_new)
    a_loc = jnp.exp(m_loc - m_new)
    inter = cum + m_prev
    rel = ig - cum
    m_t = jnp.maximum(inter, cum + _running_max(rel, direction, q))
    col_a = cum - m_t
    a_inter = jnp.exp(inter - m_t)
    floor = jnp.exp(-m_t)
    rel_t = rel.T
    e_end_t = e_end.T
    k_t = (qk[:, ML_WIDTH:2 * ML_WIDTH] * (HEAD_DIM ** -0.5)).T
    one_col = (lax.broadcasted_iota(jnp.int32, (q, HEAD_DIM), 1) == 0).astype(F32)

    ys = []
    for h in range(ML_HEADS):
        fl = direction * 2 * ML_HEADS + ML_HEADS + h
        qh = qk[:, h * HEAD_DIM:(h + 1) * HEAD_DIM].astype(BF16)
        kh_t = k_t[h * HEAD_DIM:(h + 1) * HEAD_DIM, :]
        v_ext = jnp.concatenate([v[:, h * HEAD_DIM:(h + 1) * HEAD_DIM], one_col], axis=-1).astype(BF16)
        pw = jnp.exp(jnp.where(mask, col_a[:, fl:fl + 1] + rel_t[fl:fl + 1, :], -jnp.inf))
        scores = jnp.dot(qh, kh_t.astype(BF16), preferred_element_type=F32)
        nd = jnp.dot((scores * pw).astype(BF16), v_ext, preferred_element_type=F32)
        st = s_ref[h]
        nd = nd + a_inter[:, fl:fl + 1] * jnp.dot(qh, st.astype(BF16), preferred_element_type=F32)
        den = nd[:, HEAD_DIM:HEAD_DIM + 1]
        ys.append(nd[:, 0:HEAD_DIM] / jnp.maximum(jnp.abs(den), floor[:, fl:fl + 1]))
        upd = jnp.dot((kh_t * e_end_t[fl:fl + 1, :]).astype(BF16), v_ext, preferred_element_type=F32)
        s_ref[h] = a_prev[:, fl:fl + 1] * st + a_loc[:, fl:fl + 1] * upd
    m_ref[...] = jnp.broadcast_to(m_new, m_ref.shape)
    y_all = jnp.concatenate(ys, axis=-1)
    if finalize:
        hs = y_all + hb_ref[0]
        hc = hs - jnp.dot(hs.astype(BF16), pool_ref[...], preferred_element_type=F32)
        var = jnp.dot((hc * hc).astype(BF16), pool_ref[...], preferred_element_type=F32)
        y_all = hc * lax.rsqrt(var + EPS) * nw_ref[...] * jax.nn.sigmoid(u_ref[0, :, ML_O0:ML_G0])
    y_ref[0] = y_all

    @pl.when(j == nc - 1)
    def _():
        s_fin_ref[0] = s_ref[...]
        m_fin_ref[0] = m_ref[...]


def ml_pass(u, h_other, init, mp, direction, q):
    b, L, _ = u.shape
    q = min(q, L)
    nc = L // q
    finalize = h_other is not None
    in_specs, cidx = _scan_specs(L, q, nc, ML_COLS, direction)
    const2 = lambda bi, j: (0, 0)
    s_spec = pl.BlockSpec((1, ML_HEADS, HEAD_DIM, LANE), lambda bi, j: (bi, 0, 0, 0))
    m_spec = pl.BlockSpec((1, SUBLANE, LANE), lambda bi, j: (bi, 0, 0))
    y_spec = pl.BlockSpec((1, q, ML_WIDTH), lambda bi, j: (bi, cidx(j), 0))
    args = [u, u, u]
    if finalize:
        in_specs.append(y_spec)
        args.append(h_other)
    consts = [mp['cw'], mp['cb'], mp['gb']] + ([mp['nw'], mp['pool']] if finalize else [])
    in_specs += [s_spec, m_spec] + [pl.BlockSpec(t.shape, const2) for t in consts]
    args += [init[0], init[1]] + consts
    y, s_fin, m_fin = pl.pallas_call(
        functools.partial(_ml_kernel, direction=direction, finalize=finalize, q=q, nc=nc),
        out_shape=(jax.ShapeDtypeStruct((b, L, ML_WIDTH), F32),
                   jax.ShapeDtypeStruct((b, ML_HEADS, HEAD_DIM, LANE), F32),
                   jax.ShapeDtypeStruct((b, SUBLANE, LANE), F32)),
        grid=(b, nc),
        in_specs=in_specs,
        out_specs=(y_spec, s_spec, m_spec),
        scratch_shapes=[pltpu.VMEM((ML_HEADS, HEAD_DIM, LANE), F32), pltpu.VMEM((SUBLANE, LANE), F32)],
        compiler_params=_params("parallel", "arbitrary"),
        name="mlstm_scan_%s" % ("fwd" if direction == 0 else "bwd"),
    )(*args)
    return y, (s_fin, m_fin)


def ml_prepare(p):
    gb = p['ml_gate_b'].reshape(1, -1)
    head = np.arange(ML_WIDTH) // HEAD_DIM
    pool = jnp.asarray((head[:, None] == head[None, :]) / HEAD_DIM, BF16)
    return dict(cw=p['ml_conv_w'], cb=p['ml_conv_b'].reshape(1, -1),
                gb=jnp.pad(gb, ((0, 0), (0, LANE - gb.shape[1]))), nw=p['ml_norm_w'].reshape(1, -1), pool=pool)


def ml_mixer(u, mp, init_f, init_b, want_y, q=256):
    hb, fin_b = ml_pass(u, None, init_b, mp, 1, q)
    y, fin_f = ml_pass(u, hb if want_y else None, init_f, mp, 0, q)
    return y, fin_f, fin_b


FFT_L = 4096
FFT_N = 2 * FFT_L
FFT_N2 = 128
FFT_N1 = FFT_N // FFT_N2
FFT_N1H = FFT_L // FFT_N2
FFT_K1 = FFT_N1 // 2 + 1
FFT_R = 80
FFT_UNROLL = 8
FFT_PITCH = FFT_N2 + SUBLANE


def _fft_tables():
    n2 = np.arange(FFT_N2)[:, None, None]
    k1 = np.arange(FFT_K1)[None, :, None]
    n1 = np.arange(FFT_N1H)[None, None, :]
    th = 2 * np.pi * (((FFT_N2 * n1 + n2) * k1) % FFT_N) / FFT_N
    f1 = np.zeros((FFT_N2, FFT_R, FFT_N1H))
    f1[:, 0:2 * FFT_K1:2, :] = np.cos(th)
    f1[:, 1:2 * FFT_K1:2, :] = -np.sin(th)
    wgt = np.where((np.arange(FFT_K1) == 0) | (np.arange(FFT_K1) == FFT_N1 // 2), 1.0, 2.0)[None, :, None] / FFT_N
    g1 = np.zeros((FFT_N2, FFT_N1H, FFT_R))
    g1[:, :, 0:2 * FFT_K1:2] = np.transpose(wgt * np.cos(th), (0, 2, 1))
    g1[:, :, 1:2 * FFT_K1:2] = np.transpose(-wgt * np.sin(th), (0, 2, 1))
    ph = 2 * np.pi * ((np.arange(FFT_N2)[:, None] * np.arange(FFT_N2)[None, :]) % FFT_N2) / FFT_N2
    c, s = np.cos(ph), np.sin(ph)
    f2 = np.block([[c, s], [-s, c]])
    as_bf = lambda a: jnp.asarray(a, F32).astype(BF16)
    return as_bf(f1), as_bf(f2), as_bf(f2.T), as_bf(g1)


def _sld(ref, n2, count):
    rows = pl.ds(n2, count, stride=FFT_PITCH)
    return jnp.concatenate([ref[0, rows, :], ref[1, rows, :]], axis=-1)


def _sst(ref, n2, count, val):
    rows = pl.ds(n2, count, stride=FFT_PITCH)
    ref[0, rows, :] = val[:, 0:LANE]
    ref[1, rows, :] = val[:, LANE:2 * LANE]


def _blk_ld(ref, blk, nblk):
    parts = []
    for k in range(nblk):
        rows = pl.ds(pl.multiple_of((blk + k) * FFT_PITCH, SUBLANE), FFT_N2)
        parts.append(jnp.concatenate([ref[0, rows, :], ref[1, rows, :]], axis=-1))
    return parts[0] if nblk == 1 else jnp.concatenate(parts, axis=0)


def _blk_st(ref, blk, nblk, val):
    for k in range(nblk):
        rows = pl.ds(pl.multiple_of((blk + k) * FFT_PITCH, SUBLANE), FFT_N2)
        ref[0, rows, :] = val[k * FFT_N2:(k + 1) * FFT_N2, 0:LANE]
        ref[1, rows, :] = val[k * FFT_N2:(k + 1) * FFT_N2, LANE:2 * LANE]


def _fft_stage1(z_ref, a_ref, f1_ref):
    def body(n2, carry):
        xs = _sld(z_ref, n2, FFT_N1H).astype(BF16)
        _sst(a_ref, n2, FFT_R, jnp.dot(f1_ref[n2], xs, preferred_element_type=F32))
        return carry
    lax.fori_loop(0, FFT_N2, body, 0, unroll=FFT_UNROLL)


def _spectrum_kernel(x_ref, f1_ref, f2_ref, o_ref, z_ref, a_ref):
    for n1 in range(FFT_N1H):
        _blk_st(z_ref, n1, 1, x_ref[0, n1 * FFT_N2:(n1 + 1) * FFT_N2, :])
    _fft_stage1(z_ref, a_ref, f1_ref)

    def body(k1, carry):
        slab = _blk_ld(a_ref, 2 * k1, 2).astype(BF16)
        o_ref[0, k1] = jnp.dot(f2_ref[...], slab, preferred_element_type=F32)
        return carry
    lax.fori_loop(0, FFT_K1, body, 0, unroll=3)


def spectrum(sig):
    s, L, c = sig.shape
    f1, f2, _, _ = _fft_tables()
    one = pl.Buffered(1)
    return pl.pallas_call(
        _spectrum_kernel,
        out_shape=jax.ShapeDtypeStruct((s, FFT_K1, 2 * FFT_N2, c), F32),
        grid=(s,),
        in_specs=[pl.BlockSpec((1, L, c), lambda i: (i, 0, 0)),
                  pl.BlockSpec(f1.shape, lambda i: (0, 0, 0), pipeline_mode=one),
                  pl.BlockSpec(f2.shape, lambda i: (0, 0), pipeline_mode=one)],
        out_specs=pl.BlockSpec((1, FFT_K1, 2 * FFT_N2, c), lambda i: (i, 0, 0, 0)),
        scratch_shapes=[pltpu.VMEM((2, FFT_N1H * FFT_PITCH, LANE), F32),
                        pltpu.VMEM((2, FFT_R * FFT_PITCH, LANE), F32)],
        compiler_params=pltpu.CompilerParams(dimension_semantics=("arbitrary",), vmem_limit_bytes=VMEM_LIMIT_HY),
        name="hyena_filter_spectrum",
    )(sig, f1, f2)


def _conv3_rows(src, dst, cw, cb, L):
    rows = 2 * FFT_N2
    nchunk = L // rows

    def body(i, carry):
        r0 = pl.multiple_of(i * rows, rows)
        prev_row = jnp.where(i > 0, src[pl.ds(jnp.maximum(r0 - 1, 0), 1), :], 0.0)
        next_row = jnp.where(i < nchunk - 1, src[pl.ds(jnp.minimum(r0 + rows, L - 1), 1), :], 0.0)
        _blk_st(dst, 2 * i, 2, _conv3(src[pl.ds(r0, rows), :], prev_row, next_row, cw, cb, rows))
        return carry
    lax.fori_loop(0, nchunk, body, 0, unroll=2)


def _hyena_kernel(v_ref, g_ref, h_ref, cwv_ref, cbv_ref, cwg_ref, cbg_ref, skip_ref,
                  f1_ref, f2_ref, f2t_ref, g1_ref, o_ref, z_ref, gc_ref, a_ref):
    order = pl.program_id(1)

    @pl.when(order == 0)
    def _():
        _conv3_rows(v_ref.at[0], z_ref, cwv_ref[...], cbv_ref[...], FFT_L)

    _conv3_rows(g_ref.at[0], gc_ref, cwg_ref[0], cbg_ref[0], FFT_L)
    _fft_stage1(z_ref, a_ref, f1_ref)

    def mid(k1, carry):
        x = jnp.dot(f2_ref[...], _blk_ld(a_ref, 2 * k1, 2).astype(BF16), preferred_element_type=F32)
        h = h_ref[0, k1].astype(F32)
        xr, xi, hr, hi = x[:FFT_N2], x[FFT_N2:], h[:FFT_N2], h[FFT_N2:]
        y = jnp.concatenate([xr * hr - xi * hi, xr * hi + xi * hr], axis=0).astype(BF16)
        _blk_st(a_ref, 2 * k1, 2, jnp.dot(f2t_ref[...], y, preferred_element_type=F32))
        return carry
    lax.fori_loop(0, FFT_K1, mid, 0, unroll=3)

    skip = skip_ref[0]

    def last(n2, carry):
        bs = _sld(a_ref, n2, FFT_R).astype(BF16)
        y = jnp.dot(g1_ref[n2], bs, preferred_element_type=F32)
        _sst(z_ref, n2, FFT_N1H, _sld(gc_ref, n2, FFT_N1H) * (y + skip * _sld(z_ref, n2, FFT_N1H)))
        return carry
    lax.fori_loop(0, FFT_N2, last, 0, unroll=FFT_UNROLL)

    @pl.when(order == 1)
    def _():
        for n1 in range(FFT_N1H):
            o_ref[0, n1 * FFT_N2:(n1 + 1) * FFT_N2, :] = _blk_ld(z_ref, n1, 1)


def hyena_long(u, h_spec, cw, cb, skip):
    b, L, _ = u.shape
    c = HYENA_WIDTH
    f1, f2, f2t, g1 = _fft_tables()
    one = pl.Buffered(1)
    cw3 = cw.reshape(3, 3, c).transpose(1, 0, 2)
    cb3 = cb.reshape(3, 1, c)
    return pl.pallas_call(
        _hyena_kernel,
        out_shape=jax.ShapeDtypeStruct((b, L, c), F32),
        grid=(b, 2),
        in_specs=[pl.BlockSpec((1, L, c), lambda bi, o: (bi, 0, 0), pipeline_mode=one),
                  pl.BlockSpec((1, L, c), lambda bi, o: (bi, 0, 1 + o)),
                  pl.BlockSpec((1, FFT_K1, 2 * FFT_N2, c), lambda bi, o: (o, 0, 0, 0)),
                  pl.BlockSpec((3, c), lambda bi, o: (0, 0)),
                  pl.BlockSpec((1, c), lambda bi, o: (0, 0)),
                  pl.BlockSpec((1, 3, c), lambda bi, o: (1 + o, 0, 0)),
                  pl.BlockSpec((1, 1, c), lambda bi, o: (1 + o, 0, 0)),
                  pl.BlockSpec((1, 1, c), lambda bi, o: (o, 0, 0)),
                  pl.BlockSpec(f1.shape, lambda bi, o: (0, 0, 0), pipeline_mode=one),
                  pl.BlockSpec(f2.shape, lambda bi, o: (0, 0), pipeline_mode=one),
                  pl.BlockSpec(f2t.shape, lambda bi, o: (0, 0), pipeline_mode=one),
                  pl.BlockSpec(g1.shape, lambda bi, o: (0, 0, 0), pipeline_mode=one)],
        out_specs=pl.BlockSpec((1, L, c), lambda bi, o: (bi, 0, 0)),
        scratch_shapes=[pltpu.VMEM((2, FFT_N1H * FFT_PITCH, LANE), F32),
                        pltpu.VMEM((2, FFT_N1H * FFT_PITCH, LANE), F32),
                        pltpu.VMEM((2, FFT_R * FFT_PITCH, LANE), F32)],
        compiler_params=pltpu.CompilerParams(dimension_semantics=("parallel", "arbitrary"),
                                             vmem_limit_bytes=VMEM_LIMIT_HY),
        name="hyena_long_conv",
    )(u, u, h_spec, cw3[0], cb3[0], cw3, cb3, skip.reshape(2, 1, c), f1, f2, f2t, g1)


def hyena_filter_spectra(sig, inv_norm):
    spec = spectrum(sig)
    sign = jnp.where(jnp.arange(FFT_K1) % 2 == 0, 1.0, -1.0)[None, :, None, None]
    spec = spec.reshape(HYENA_ORDER, 2, *spec.shape[1:])
    return ((spec[:, 0] + sign * spec[:, 1]) * inv_norm[:, None]).astype(BF16)


HY_FILT = 2 * HYENA_ORDER * HYENA_WIDTH
HY_HALF = HYENA_ORDER * HYENA_WIDTH


def _filter_kernel(wt_ref, wc_ref, ws_ref, b1_ref, w2_ref, b2_ref, w3_ref, freq_ref, decay_ref,
                   sig_ref, asum_ref, *, L, rows):
    i = pl.program_id(0)
    n = (i * rows + lax.broadcasted_iota(jnp.int32, (rows, 1), 0)).astype(F32)
    band = lax.broadcasted_iota(jnp.int32, (1, HYENA_BANDS), 1).astype(F32)
    bands = 1e-4 + band * ((HYENA_BANDS - 1 - 1e-4) / (HYENA_BANDS - 1))
    freq = freq_ref[...]

    @pl.when(i == 0)
    def _():
        asum_ref[...] = jnp.zeros_like(asum_ref)

    for side, t in enumerate((n, L - n)):
        t_unit = t / float(max(L - 1, 1))
        ang = (2 * math.pi / L) * t * bands
        pre = (t_unit * wt_ref[...] + jnp.dot(jnp.cos(ang), wc_ref[...], preferred_element_type=F32, precision=HI)
               - jnp.dot(jnp.sin(ang), ws_ref[...], preferred_element_type=F32, precision=HI) + b1_ref[...])
        hdn = jnp.sin(freq * pre)
        hdn = jnp.sin(freq * (jnp.dot(hdn, w2_ref[...], preferred_element_type=F32, precision=HI) + b2_ref[...]))
        cols = slice(side * HY_HALF, (side + 1) * HY_HALF)
        val = jnp.dot(hdn, w3_ref[:, cols], preferred_element_type=F32, precision=HI)
        val = val * jnp.exp(-t_unit * jnp.abs(decay_ref[:, cols]))
        if side == 1:
            val = jnp.where(n > 0, val, 0.0)
        for o in range(HYENA_ORDER):
            sig_ref[2 * o + side] = val[:, o * HYENA_WIDTH:(o + 1) * HYENA_WIDTH]
        asum_ref[side:side + 1, :] += jnp.sum(jnp.abs(val), axis=0, keepdims=True)


def hyena_filter_signals(L, p):
    rows = min(L, 512)
    fh = p['hy_pos_w1'].shape[1]
    w1 = p['hy_pos_w1']
    consts = [w1[0:1], w1[1:1 + HYENA_BANDS], w1[1 + HYENA_BANDS:], p['hy_pos_b1'].reshape(1, fh), p['hy_pos_w2'],
              p['hy_pos_b2'].reshape(1, fh), p['hy_pos_w3'], p['hy_freq'].reshape(1, fh),
              p['hy_decay'].reshape(1, HY_FILT)]
    sig, asum = pl.pallas_call(
        functools.partial(_filter_kernel, L=L, rows=rows),
        out_shape=(jax.ShapeDtypeStruct((2 * HYENA_ORDER, L, HYENA_WIDTH), F32),
                   jax.ShapeDtypeStruct((SUBLANE, HY_HALF), F32)),
        grid=(L // rows,),
        in_specs=[pl.BlockSpec(c.shape, lambda i: (0, 0)) for c in consts],
        out_specs=(pl.BlockSpec((2 * HYENA_ORDER, rows, HYENA_WIDTH), lambda i: (0, i, 0)),
                   pl.BlockSpec((SUBLANE, HY_HALF), lambda i: (0, 0))),
        compiler_params=_params("arbitrary"),
        name="hyena_filters",
    )(*consts)
    inv_norm = 1.0 / (asum[0] + asum[1]).reshape(HYENA_ORDER, 1, HYENA_WIDTH)
    return sig, inv_norm


def _rdft_tables(L):
    n_bins = L + 1
    half = -(-n_bins // 16) * 16
    k = np.arange(n_bins)[:, None]
    n = np.arange(L)[None, :]
    th = 2 * np.pi * ((k * n) % (2 * L)) / (2 * L)
    f = np.zeros((2 * half, L))
    f[:n_bins] = np.cos(th)
    f[half:half + n_bins] = -np.sin(th)
    sign = np.where(np.arange(n_bins) % 2 == 0, 1.0, -1.0)[:, None]
    fs = np.zeros_like(f)
    fs[:n_bins] = f[:n_bins] * sign
    fs[half:half + n_bins] = f[half:half + n_bins] * sign
    wgt = np.where((np.arange(n_bins) == 0) | (np.arange(n_bins) == L), 1.0, 2.0)[None, :] / (2 * L)
    g = np.zeros((L, 2 * half))
    g[:, :n_bins] = wgt * np.cos(th).T
    g[:, half:half + n_bins] = -wgt * np.sin(th).T
    as_bf = lambda a: jnp.asarray(a, F32).astype(BF16)
    return as_bf(f), as_bf(fs), as_bf(g), half


def _hyena_ctx_kernel(u_ref, sig_ref, inorm_ref, cw_ref, cb_ref, skip_ref, f_ref, fs_ref, g_ref, o_ref, *, L, half):
    zero_row = jnp.zeros((1, u_ref.shape[2]), F32)
    uc = _conv3(u_ref[0], zero_row, zero_row, cw_ref[...], cb_ref[...], L)
    z = uc[:, 0:HYENA_WIDTH]
    for o in range(HYENA_ORDER):
        h = (jnp.dot(f_ref[...], sig_ref[2 * o].astype(BF16), preferred_element_type=F32)
             + jnp.dot(fs_ref[...], sig_ref[2 * o + 1].astype(BF16), preferred_element_type=F32)) * inorm_ref[o]
        x = jnp.dot(f_ref[...], z.astype(BF16), preferred_element_type=F32)
        xr, xi, hr, hi = x[:half], x[half:], h[:half], h[half:]
        y = jnp.concatenate([xr * hr - xi * hi, xr * hi + xi * hr], axis=0).astype(BF16)
        conv = jnp.dot(g_ref[...], y, preferred_element_type=F32)
        z = uc[:, (o + 1) * HYENA_WIDTH:(o + 2) * HYENA_WIDTH] * (conv + skip_ref[o] * z)
    o_ref[0] = z


def hyena_short(u, sig, inv_norm, cw, cb, skip):
    b, L, cols = u.shape
    f, fs, g, half = _rdft_tables(L)
    const2 = lambda bi: (0, 0)
    const3 = lambda bi: (0, 0, 0)
    return pl.pallas_call(
        functools.partial(_hyena_ctx_kernel, L=L, half=half),
        out_shape=jax.ShapeDtypeStruct((b, L, HYENA_WIDTH), F32),
        grid=(b,),
        in_specs=[pl.BlockSpec((1, L, cols), lambda bi: (bi, 0, 0)),
                  pl.BlockSpec(sig.shape, const3), pl.BlockSpec(inv_norm.shape, const3),
                  pl.BlockSpec(cw.shape, const2), pl.BlockSpec((1, cols), const2),
                  pl.BlockSpec((HYENA_ORDER, 1, HYENA_WIDTH), const3),
                  pl.BlockSpec(f.shape, const2), pl.BlockSpec(fs.shape, const2), pl.BlockSpec(g.shape, const2)],
        out_specs=pl.BlockSpec((1, L, HYENA_WIDTH), lambda bi: (bi, 0, 0)),
        compiler_params=_params("parallel"),
        name="hyena_context",
    )(u, sig, inv_norm, cw, cb.reshape(1, cols), skip.reshape(HYENA_ORDER, 1, HYENA_WIDTH), f, fs, g)


def _split_bf16(a):
    hi = a.astype(BF16)
    return hi, (a - hi.astype(F32)).astype(BF16)


def _norm_router_kernel(*refs, n_lat, two_src):
    if two_src:
        xl_ref, xc_ref, nw_ref, sh_ref, sc_ref, whi_ref, wlo_ref, br_ref, h_ref, lg_ref = refs
        x = jnp.where(pl.program_id(0) < n_lat, xl_ref[...], xc_ref[...])
    else:
        xl_ref, nw_ref, sh_ref, sc_ref, whi_ref, wlo_ref, br_ref, h_ref, lg_ref = refs
        x = xl_ref[...]
    h = _modnorm(x, nw_ref[...], sh_ref[0], sc_ref[0])
    h_ref[...] = h
    h_hi, h_lo = _split_bf16(h)
    lg = (jnp.dot(h_hi, whi_ref[...], preferred_element_type=F32)
          + jnp.dot(h_hi, wlo_ref[...], preferred_element_type=F32)
          + jnp.dot(h_lo, whi_ref[...], preferred_element_type=F32)) + br_ref[...]
    lane = lax.broadcasted_iota(jnp.int32, lg.shape, 1)
    first = lambda hit: jnp.min(jnp.where(hit, lane, ROUTER_COLS), axis=-1, keepdims=True)
    gl = jnp.where(lane < N_GROUPS, lg, -jnp.inf)
    gmax = jnp.max(gl, axis=-1, keepdims=True)
    grp = first(gl == gmax)
    grp_p = 1.0 / jnp.sum(jnp.exp(gl - gmax), axis=-1, keepdims=True)
    lo = N_GROUPS + grp * EXPERTS_PER_GROUP
    el = jnp.where((lane >= lo) & (lane < lo + EXPERTS_PER_GROUP), lg, -jnp.inf)
    e1 = jnp.max(el, axis=-1, keepdims=True)
    i1 = first(el == e1)
    el2 = jnp.where(lane == i1, -jnp.inf, el)
    e2 = jnp.max(el2, axis=-1, keepdims=True)
    i2 = first(el2 == e2)
    r = jnp.exp(e2 - e1)
    w1 = grp_p / (1.0 + r)
    w2 = w1 * r
    vals = [(i1 - N_GROUPS).astype(F32), (i2 - N_GROUPS).astype(F32), w1, w2]
    out = jnp.zeros(lg.shape, F32)
    for k, val in enumerate(vals):
        out = jnp.where(lane == k, val, out)
    lg_ref[...] = out


def norm_router(xl, xc, nw, mod_l, mod_c, w_router, b_router, tm=512):
    b, L, d = xl.shape
    two_src = xc is not None
    n_lat = b * L // tm
    per_batch = L // tm
    n_ctx = (xc.shape[0] * xc.shape[1]) // tm if two_src else 0
    w_hi, w_lo = _split_bf16(w_router)
    const = lambda i: (0, 0)
    if two_src:
        shift = jnp.concatenate([mod_l[0], mod_c[0]], axis=0)
        scale = jnp.concatenate([mod_l[1], mod_c[1]], axis=0)
        mod_map = lambda i: (jnp.where(i < n_lat, i // per_batch, b), 0, 0)
        srcs = [xl.reshape(b * L, d), xc.reshape(-1, d)]
        src_specs = [pl.BlockSpec((tm, d), lambda i: (jnp.minimum(i, n_lat - 1), 0)),
                     pl.BlockSpec((tm, d), lambda i: (jnp.maximum(i - n_lat, 0), 0))]
    else:
        shift, scale = mod_l
        mod_map = lambda i: (i // per_batch, 0, 0)
        srcs = [xl.reshape(b * L, d)]
        src_specs = [pl.BlockSpec((tm, d), lambda i: (i, 0))]
    n_tok = (n_lat + n_ctx) * tm
    return pl.pallas_call(
        functools.partial(_norm_router_kernel, n_lat=n_lat, two_src=two_src),
        out_shape=(jax.ShapeDtypeStruct((n_tok, d), F32), jax.ShapeDtypeStruct((n_tok, ROUTER_COLS), F32)),
        grid=(n_lat + n_ctx,),
        in_specs=src_specs + [pl.BlockSpec((1, d), const), pl.BlockSpec((1, 1, d), mod_map),
                              pl.BlockSpec((1, 1, d), mod_map), pl.BlockSpec((d, ROUTER_COLS), const),
                              pl.BlockSpec((d, ROUTER_COLS), const), pl.BlockSpec((1, ROUTER_COLS), const)],
        out_specs=(pl.BlockSpec((tm, d), lambda i: (i, 0)), pl.BlockSpec((tm, ROUTER_COLS), lambda i: (i, 0))),
        compiler_params=_params("arbitrary"),
        name="moe_norm_router",
    )(*srcs, nw.reshape(1, d), shift, scale, w_hi, w_lo, b_router)


def _expert_ffn_kernel(te_ref, tv_ref, x_ref, rw_ref, wg_ref, wu_ref, wd_ref, o_ref):
    i = pl.program_id(0)

    @pl.when(tv_ref[i] > 0)
    def _():
        x = x_ref[...].astype(BF16)
        g = jnp.dot(x, wg_ref[0].astype(BF16), preferred_element_type=F32)
        u = jnp.dot(x, wu_ref[0].astype(BF16), preferred_element_type=F32)
        hid = _silu(g) * u * rw_ref[...]
        o_ref[...] = jnp.dot(hid.astype(BF16), wd_ref[0].astype(BF16), preferred_element_type=F32)

    @pl.when(tv_ref[i] == 0)
    def _():
        o_ref[...] = jnp.zeros_like(o_ref)


def expert_ffn(x_sorted, row_w, tile_expert, tile_valid, w_gate, w_up, w_down, tm):
    r, d = x_sorted.shape
    f = w_gate.shape[-1]
    grid_spec = pltpu.PrefetchScalarGridSpec(
        num_scalar_prefetch=2,
        grid=(r // tm,),
        in_specs=[pl.BlockSpec((tm, d), lambda i, te, tv: (i, 0)),
                  pl.BlockSpec((tm, 1), lambda i, te, tv: (i, 0)),
                  pl.BlockSpec((1, d, f), lambda i, te, tv: (te[i], 0, 0)),
                  pl.BlockSpec((1, d, f), lambda i, te, tv: (te[i], 0, 0)),
                  pl.BlockSpec((1, f, d), lambda i, te, tv: (te[i], 0, 0))],
        out_specs=pl.BlockSpec((tm, d), lambda i, te, tv: (i, 0)),
    )
    return pl.pallas_call(
        _expert_ffn_kernel,
        out_shape=jax.ShapeDtypeStruct((r, d), F32),
        grid_spec=grid_spec,
        compiler_params=_params("arbitrary"),
        name="moe_expert_ffn",
    )(tile_expert, tile_valid, x_sorted, row_w, w_gate, w_up, w_down)


def moe_apply(h_tokens, routed, w_gate, w_up, w_down, layer, n_lat, tm=256):
    t, d = h_tokens.shape
    e_idx, e_w = routed[:, 0:2].astype(jnp.int32), routed[:, 2:4]
    flat_e = e_idx.reshape(-1).astype(jnp.int32)
    n_pairs = 2 * t
    sorted_e, order = lax.sort((flat_e, jnp.arange(n_pairs, dtype=jnp.int32)), num_keys=1, is_stable=True)
    experts = jnp.arange(N_EXPERTS, dtype=jnp.int32)
    seg_end = jnp.sum((flat_e[None, :] <= experts[:, None]).astype(jnp.int32), axis=1)
    counts = seg_end - jnp.concatenate([jnp.zeros((1,), jnp.int32), seg_end[:-1]])
    seg_start = seg_end - counts
    padded = (counts + tm - 1) // tm * tm
    pad_end = jnp.cumsum(padded)
    pad_start = pad_end - padded
    n_rows = n_pairs + N_EXPERTS * tm
    tile_start = jnp.arange(n_rows // tm, dtype=jnp.int32) * tm
    tile_expert = jnp.minimum(jnp.sum((pad_end[None, :] <= tile_start[:, None]).astype(jnp.int32), axis=1),
                              N_EXPERTS - 1)
    tile_valid = (tile_start < pad_end[-1]).astype(jnp.int32)
    off = (tile_start - pad_start[tile_expert])[:, None] + jnp.arange(tm, dtype=jnp.int32)[None, :]
    valid = (off < counts[tile_expert][:, None]) & (tile_valid[:, None] > 0)
    spread = jnp.arange(n_rows, dtype=jnp.int32).reshape(-1, tm) % n_pairs
    pair = jnp.where(valid, seg_start[tile_expert][:, None] + off, spread).reshape(-1)
    src = order[pair]
    row_token = src // 2
    row_w = jnp.where(valid.reshape(-1), e_w.reshape(-1)[src], 0.0)
    dest = pad_start[sorted_e] + (jnp.arange(n_pairs, dtype=jnp.int32) - seg_start[sorted_e])
    _, pos = lax.sort((order, dest), num_keys=1)
    pos = pos.reshape(t, 2)
    x_sorted = h_tokens[row_token]
    wg = w_gate.reshape(-1, d, EXPERT_HIDDEN)
    wu = w_up.reshape(-1, d, EXPERT_HIDDEN)
    wd = w_down.reshape(-1, EXPERT_HIDDEN, d)
    y_sorted = expert_ffn(x_sorted, row_w[:, None], tile_expert + layer * N_EXPERTS, tile_valid, wg, wu, wd, tm)
    lat = (y_sorted[pos[:n_lat, 0]], y_sorted[pos[:n_lat, 1]])
    rest = (y_sorted[pos[n_lat:, 0]], y_sorted[pos[n_lat:, 1]]) if t > n_lat else None
    return lat, rest


def _final_kernel(x_ref, ya_ref, yb_ref, g_ref, w_ref, o_ref):
    x = x_ref[0] + g_ref[0] * (ya_ref[0] + yb_ref[0])
    o_ref[0] = x * lax.rsqrt(jnp.mean(x * x, axis=-1, keepdims=True) + EPS) * w_ref[...]


def final_norm(x, ya, yb, gate, w, tm=512):
    b, L, d = x.shape
    tok = pl.BlockSpec((1, tm, d), lambda bi, i: (bi, i, 0))
    return pl.pallas_call(
        _final_kernel,
        out_shape=jax.ShapeDtypeStruct((b, L, d), F32),
        grid=(b, L // tm),
        in_specs=[tok, tok, tok, pl.BlockSpec((1, 1, d), _mod_map(gate, b)),
                  pl.BlockSpec((1, d), lambda bi, i: (0, 0))],
        out_specs=tok,
        compiler_params=_params("parallel", "arbitrary"),
        name="final_rmsnorm",
    )(x, ya, yb, gate, w.reshape(1, d))


def _regroup_in_weight(w_in):
    sizes = (SSD_WIDTH, SSD_CONV_CH, 2 * SSD_HEADS, HY_COLS, 2 * ML_WIDTH, ML_WIDTH, ML_WIDTH, 4 * ML_HEADS)
    parts, s = [], 0
    for n in sizes:
        parts.append(jnp.pad(w_in[:, s:s + n], ((0, 0), (0, -n % LANE))))
        s += n
    return jnp.concatenate(parts, axis=1).astype(BF16)


def kernel(x, c, ctx, c_ctx, w_mod, b_mod, norm1_w, norm2_w, w_in, w_out, ssd_conv_w, ssd_conv_b, ssd_dt_bias, ssd_a_log, ssd_d, ssd_norm_w, hy_conv_w, hy_conv_b, hy_pos_w1, hy_pos_b1, hy_pos_w2, hy_pos_b2, hy_pos_w3, hy_freq, hy_decay, hy_skip, ml_conv_w, ml_conv_b, ml_gate_b, ml_norm_w, grp_router_w, grp_router_b, exp_router_w, exp_router_b, moe_w_gate, moe_w_up, moe_w_down, final_norm_w):
    layer_params = dict(
        ssd_conv_w=ssd_conv_w, ssd_conv_b=ssd_conv_b, ssd_dt_bias=ssd_dt_bias, ssd_a_log=ssd_a_log,
        ssd_d=ssd_d, ssd_norm_w=ssd_norm_w, hy_conv_w=hy_conv_w, hy_conv_b=hy_conv_b,
        hy_pos_w1=hy_pos_w1, hy_pos_b1=hy_pos_b1, hy_pos_w2=hy_pos_w2, hy_pos_b2=hy_pos_b2,
        hy_pos_w3=hy_pos_w3, hy_freq=hy_freq, hy_decay=hy_decay, hy_skip=hy_skip,
        ml_conv_w=ml_conv_w, ml_conv_b=ml_conv_b, ml_gate_b=ml_gate_b, ml_norm_w=ml_norm_w)
    bsz, seq, d = x.shape
    n_ctx = ctx.shape[1]
    xl, xc = x, ctx
    moe_l = moe_c = None
    ssd0 = jnp.zeros((bsz, SSD_HEADS, HEAD_DIM, SSD_STATE), F32)
    ml0 = (jnp.zeros((bsz, ML_HEADS, HEAD_DIM, LANE), F32), jnp.zeros((bsz, SUBLANE, LANE), F32))
    c_rows = jnp.concatenate([c, c_ctx[None, :], jnp.zeros((SUBLANE - bsz - 1, d), F32)], axis=0)
    for i in range(DEPTH):
        last = i == DEPTH - 1
        p = {name: arr[i] for name, arr in layer_params.items()}
        sp, mp = ssd_prepare(p), ml_prepare(p)
        mod = modulation(c_rows, w_mod[i], b_mod[i]).reshape(SUBLANE, N_MOD, 1, d)
        mod_l = [mod[:bsz, k] for k in range(N_MOD)]
        mod_c = [mod[bsz:bsz + 1, k] for k in range(N_MOD)]
        w_in_p = _regroup_in_weight(w_in[i])
        w_out_b = w_out[i].astype(BF16)
        w_router = jnp.pad(jnp.concatenate([grp_router_w[i], exp_router_w[i]], axis=1),
                           ((0, 0), (0, ROUTER_COLS - N_GROUPS - N_EXPERTS)))
        b_router = jnp.pad(jnp.concatenate([grp_router_b[i], exp_router_b[i]]),
                           (0, ROUTER_COLS - N_GROUPS - N_EXPERTS)).reshape(1, ROUTER_COLS)

        uc_ssd, uc_hy, uc_ml, xc = norm_proj(xc, moe_c, norm1_w[i], mod_c[0], mod_c[1], w_in_p)
        yc_ssd, ssd_f, ssd_b = ssd_mixer(uc_ssd, sp, ssd0, ssd0, not last)
        yc_ml, ml_f, ml_b = ml_mixer(uc_ml, mp, ml0, ml0, not last)
        col_major = i % 2 == 1
        ul_ssd, ul_hy, ul_ml, xl = norm_proj(xl, moe_l, norm1_w[i], mod_l[0], mod_l[1], w_in_p, col_major)
        yl_ssd, _, _ = ssd_mixer(ul_ssd, sp, ssd_f, ssd_b, True)
        yl_ml, _, _ = ml_mixer(ul_ml, mp, ml_f, ml_b, True)
        h_spec = hyena_filter_spectra(*hyena_filter_signals(seq, p))
        yl_hy = hyena_long(ul_hy, h_spec, p['hy_conv_w'], p['hy_conv_b'], p['hy_skip'])
        xl = out_proj(yl_ssd, yl_hy, yl_ml, xl, mod_l[2], w_out_b, col_major)
        if not last:
            sig_c, inorm_c = hyena_filter_signals(n_ctx, p)
            yc_hy = hyena_short(uc_hy, sig_c, inorm_c, p['hy_conv_w'], p['hy_conv_b'], p['hy_skip'])
            xc = out_proj(yc_ssd, yc_hy, yc_ml, xc, mod_c[2], w_out_b)
        h_all, routed = norm_router(xl, None if last else xc, norm2_w[i], (mod_l[3], mod_l[4]), (mod_c[3], mod_c[4]),
                                    w_router, b_router)
        lat, rest = moe_apply(h_all, routed, moe_w_gate, moe_w_up, moe_w_down, i, bsz * seq)
        moe_l = (lat[0].reshape(bsz, seq, d), lat[1].reshape(bsz, seq, d), mod_l[5])
        if not last:
            moe_c = (rest[0].reshape(bsz, n_ctx, d), rest[1].reshape(bsz, n_ctx, d), mod_c[5])
    return final_norm(xl, *moe_l, final_norm_w)
```

```python
import functools
import math

import jax
import jax.numpy as jnp
import numpy as np
from jax import lax
from jax.experimental import pallas as pl
from jax.experimental.pallas import tpu as pltpu

D_MODEL = 1024
DEPTH = 2
GRID_W = 64
HEAD_DIM = 64
SSD_WIDTH = 384
SSD_HEADS = SSD_WIDTH // HEAD_DIM
SSD_GROUPS = 2
SSD_STATE = 64
HYENA_WIDTH = 256
HYENA_ORDER = 2
HYENA_BANDS = 16
ML_WIDTH = 384
ML_HEADS = ML_WIDTH // HEAD_DIM
N_GROUPS = 4
EXPERTS_PER_GROUP = 8
N_EXPERTS = N_GROUPS * EXPERTS_PER_GROUP
EXPERT_HIDDEN = 256
N_MOD = 6
EPS = 1e-6

LANE = 128
SUBLANE = 8
VMEM_LIMIT = 48 * 1024 * 1024
VMEM_LIMIT_HY = 56 * 1024 * 1024

SSD_CONV_CH = SSD_WIDTH + 2 * SSD_GROUPS * SSD_STATE
SSD_XBC0 = SSD_WIDTH
SSD_DT0 = SSD_XBC0 + SSD_CONV_CH
SSD_COLS = SSD_DT0 + LANE
HY_COLS = (HYENA_ORDER + 1) * HYENA_WIDTH
ML_V0 = 2 * ML_WIDTH
ML_O0 = ML_V0 + ML_WIDTH
ML_G0 = ML_O0 + ML_WIDTH
ML_COLS = ML_G0 + LANE
ROUTER_COLS = LANE

F32 = jnp.float32
BF16 = jnp.bfloat16
HI = lax.Precision.HIGHEST


def _params(*sem):
    return pltpu.CompilerParams(dimension_semantics=sem, vmem_limit_bytes=VMEM_LIMIT)


def _silu(x):
    return x * jax.nn.sigmoid(x)


def _softplus(x):
    return jnp.maximum(x, 0.0) + jnp.log(1.0 + jnp.exp(-jnp.abs(x)))


def _log_sigmoid(x):
    return jnp.minimum(x, 0.0) - jnp.log(1.0 + jnp.exp(-jnp.abs(x)))


def _mod_kernel(c_ref, w_ref, b_ref, o_ref):
    o_ref[...] = jnp.dot(_silu(c_ref[...]), w_ref[...], preferred_element_type=F32, precision=HI) + b_ref[...]


def modulation(c_rows, w_mod, b_mod):
    n = w_mod.shape[1]
    tn = 1536
    return pl.pallas_call(
        _mod_kernel,
        out_shape=jax.ShapeDtypeStruct((c_rows.shape[0], n), F32),
        grid=(n // tn,),
        in_specs=[pl.BlockSpec(c_rows.shape, lambda j: (0, 0)),
                  pl.BlockSpec((w_mod.shape[0], tn), lambda j: (0, j)),
                  pl.BlockSpec((1, tn), lambda j: (0, j))],
        out_specs=pl.BlockSpec((c_rows.shape[0], tn), lambda j: (0, j)),
        compiler_params=_params("arbitrary"),
        name="adaln_modulation",
    )(c_rows, w_mod, b_mod.reshape(1, n))


def _modnorm(x, nw, shift, scale):
    y = x * lax.rsqrt(jnp.mean(x * x, axis=-1, keepdims=True) + EPS) * nw
    return y * (1.0 + scale) + shift


def _mod_map(mod, b):
    return (lambda bi, i: (bi, 0, 0)) if mod.shape[0] == b else (lambda bi, i: (0, 0, 0))


def _tok_view(x, col_major):
    b, L, d = x.shape
    return x.reshape(b, L // GRID_W, GRID_W, d) if col_major else x


def _tok_spec(L, d, tm, col_major):
    if col_major:
        assert tm == (L // GRID_W) * SUBLANE
        return pl.BlockSpec((1, L // GRID_W, SUBLANE, d), lambda bi, i: (bi, 0, i, 0))
    return pl.BlockSpec((1, tm, d), lambda bi, i: (bi, i, 0))


def _tok_load(ref, col_major):
    if not col_major:
        return ref[0]
    return jnp.concatenate([ref[0, :, j, :] for j in range(ref.shape[2])], axis=0)


def _tok_store(ref, val, col_major):
    if not col_major:
        ref[0] = val
        return
    rows = ref.shape[1]
    for j in range(ref.shape[2]):
        ref[0, :, j, :] = val[j * rows:(j + 1) * rows]


def _norm_proj_kernel(*refs, col_major, fuse_moe):
    if fuse_moe:
        x_ref, ya_ref, yb_ref, g_ref, nw_ref, sh_ref, sc_ref, w_ref, ssd_ref, hy_ref, ml_ref, xo_ref = refs
    else:
        x_ref, nw_ref, sh_ref, sc_ref, w_ref, ssd_ref, hy_ref, ml_ref = refs
    x = _tok_load(x_ref, col_major)
    if fuse_moe:
        x = x + g_ref[0] * (_tok_load(ya_ref, col_major) + _tok_load(yb_ref, col_major))
        _tok_store(xo_ref, x, col_major)
    h = _modnorm(x, nw_ref[...], sh_ref[0], sc_ref[0])
    u = jnp.dot(h.astype(BF16), w_ref[...], preferred_element_type=F32)
    ssd_ref[0] = u[:, 0:SSD_COLS]
    hy_ref[0] = u[:, SSD_COLS:SSD_COLS + HY_COLS]
    ml_ref[0] = u[:, SSD_COLS + HY_COLS:]


def norm_proj(x, moe, nw, shift, scale, w_bf16, col_major=False, tm=512):
    b, L, d = x.shape
    n = w_bf16.shape[1]
    fuse_moe = moe is not None
    tm = (L // GRID_W) * SUBLANE if col_major else min(tm, L)
    tok = _tok_spec(L, d, tm, col_major)
    row = lambda bi, i: (bi, i, 0)
    const2 = lambda bi, i: (0, 0)
    args, in_specs = [_tok_view(x, col_major)], [tok]
    if fuse_moe:
        ya, yb, gate = moe
        args += [_tok_view(ya, col_major), _tok_view(yb, col_major), gate]
        in_specs += [tok, tok, pl.BlockSpec((1, 1, d), _mod_map(gate, b))]
    args += [nw.reshape(1, d), shift, scale, w_bf16]
    in_specs += [pl.BlockSpec((1, d), const2), pl.BlockSpec((1, 1, d), _mod_map(shift, b)),
                 pl.BlockSpec((1, 1, d), _mod_map(scale, b)), pl.BlockSpec((d, n), const2)]
    out_shape = [jax.ShapeDtypeStruct((b, L, SSD_COLS), F32), jax.ShapeDtypeStruct((b, L, HY_COLS), F32),
                 jax.ShapeDtypeStruct((b, L, ML_COLS), F32)]
    out_specs = [pl.BlockSpec((1, tm, SSD_COLS), row), pl.BlockSpec((1, tm, HY_COLS), row),
                 pl.BlockSpec((1, tm, ML_COLS), row)]
    if fuse_moe:
        out_shape.append(jax.ShapeDtypeStruct(args[0].shape, F32))
        out_specs.append(tok)
    outs = pl.pallas_call(
        functools.partial(_norm_proj_kernel, col_major=col_major, fuse_moe=fuse_moe),
        out_shape=tuple(out_shape),
        grid=(b, L // tm),
        in_specs=in_specs,
        out_specs=tuple(out_specs),
        compiler_params=_params("parallel", "arbitrary"),
        name="norm_in_proj",
    )(*args)
    return (*outs[:3], outs[3].reshape(b, L, d) if fuse_moe else x)


def _out_proj_kernel(ys_ref, yh_ref, ym_ref, x_ref, g_ref, w_ref, o_ref, *, col_major):
    y = jnp.concatenate([ys_ref[0], yh_ref[0], ym_ref[0]], axis=-1).astype(BF16)
    r = _tok_load(x_ref, col_major) + g_ref[0] * jnp.dot(y, w_ref[...], preferred_element_type=F32)
    _tok_store(o_ref, r, col_major)


def out_proj(y_ssd, y_hy, y_ml, x, gate, w_bf16, col_major=False, tm=512):
    b, L, d = x.shape
    tm = (L // GRID_W) * SUBLANE if col_major else min(tm, L)
    row = lambda bi, i: (bi, i, 0)
    tok = _tok_spec(L, d, tm, col_major)
    xv = _tok_view(x, col_major)
    return pl.pallas_call(
        functools.partial(_out_proj_kernel, col_major=col_major),
        out_shape=jax.ShapeDtypeStruct(xv.shape, F32),
        grid=(b, L // tm),
        in_specs=[pl.BlockSpec((1, tm, SSD_WIDTH), row), pl.BlockSpec((1, tm, HYENA_WIDTH), row),
                  pl.BlockSpec((1, tm, ML_WIDTH), row), tok,
                  pl.BlockSpec((1, 1, d), _mod_map(gate, b)),
                  pl.BlockSpec(w_bf16.shape, lambda bi, i: (0, 0))],
        out_specs=tok,
        compiler_params=_params("parallel", "arbitrary"),
        name="out_proj_residual",
    )(y_ssd, y_hy, y_ml, xv, gate, w_bf16).reshape(b, L, d)


def _conv3(xr, prev_row, next_row, cw, cb, q):
    rid = lax.broadcasted_iota(jnp.int32, (q, 1), 0)
    x_prev = jnp.where(rid == 0, prev_row, pltpu.roll(xr, 1, axis=0))
    x_next = jnp.where(rid == q - 1, next_row, pltpu.roll(xr, q - 1, axis=0))
    return x_prev * cw[0:1] + xr * cw[1:2] + x_next * cw[2:3] + cb


def _scan_mask(q, direction):
    li = lax.broadcasted_iota(jnp.int32, (q, q), 0)
    si = lax.broadcasted_iota(jnp.int32, (q, q), 1)
    return (si <= li) if direction == 0 else (si >= li)


def _running_max(x, direction, q):
    rid = lax.broadcasted_iota(jnp.int32, (q, 1), 0)
    s = 1
    while s < q:
        if direction == 0:
            x = jnp.where(rid >= s, jnp.maximum(x, pltpu.roll(x, s, axis=0)), x)
        else:
            x = jnp.where(rid < q - s, jnp.maximum(x, pltpu.roll(x, q - s, axis=0)), x)
        s *= 2
    return x


def _scan_specs(L, q, nc, cols, direction):
    hb = q // SUBLANE
    nrb = L // SUBLANE
    cidx = (lambda j: j) if direction == 0 else (lambda j: nc - 1 - j)
    specs = [pl.BlockSpec((1, q, cols), lambda bi, j: (bi, cidx(j), 0)),
             pl.BlockSpec((1, SUBLANE, cols), lambda bi, j: (bi, jnp.maximum(cidx(j) * hb - 1, 0), 0)),
             pl.BlockSpec((1, SUBLANE, cols), lambda bi, j: (bi, jnp.minimum((cidx(j) + 1) * hb, nrb - 1), 0))]
    return specs, cidx


def _ssd_kernel(*refs, direction, finalize, q, nc):
    if finalize:
        (u_ref, prev_ref, next_ref, yb_ref, init_ref, cw_ref, cb_ref, dtb_ref, a_ref, d_ref, nw_ref,
         y_ref, fin_ref, state_ref) = refs
    else:
        (u_ref, prev_ref, next_ref, init_ref, cw_ref, cb_ref, dtb_ref, a_ref,
         y_ref, fin_ref, state_ref) = refs
    j = pl.program_id(1)
    c = j if direction == 0 else nc - 1 - j

    @pl.when(j == 0)
    def _():
        state_ref[...] = init_ref[0]

    prev_row = jnp.where(c > 0, prev_ref[0, SUBLANE - 1:SUBLANE, SSD_XBC0:SSD_DT0], 0.0)
    next_row = jnp.where(c < nc - 1, next_ref[0, 0:1, SSD_XBC0:SSD_DT0], 0.0)
    xc = _silu(_conv3(u_ref[0, :, SSD_XBC0:SSD_DT0], prev_row, next_row, cw_ref[...], cb_ref[...], q))

    dt = _softplus(u_ref[0, :, SSD_DT0:SSD_COLS] + dtb_ref[...])
    mask = _scan_mask(q, direction)
    cum = jnp.dot(mask.astype(F32), dt * a_ref[...], preferred_element_type=F32, precision=HI)
    cum_t = cum.T
    end = q - 1 if direction == 0 else 0

    ys = []
    for g in range(SSD_GROUPS):
        b0 = SSD_WIDTH + g * SSD_STATE
        c0 = SSD_WIDTH + (SSD_GROUPS + g) * SSD_STATE
        bm_t = xc[:, b0:b0 + SSD_STATE].T
        cm = xc[:, c0:c0 + SSD_STATE].astype(BF16)
        scores = jnp.dot(cm, bm_t.astype(BF16), preferred_element_type=F32)
        for h in range(g * (SSD_HEADS // SSD_GROUPS), (g + 1) * (SSD_HEADS // SSD_GROUPS)):
            hl = direction * SSD_HEADS + h
            col = cum[:, hl:hl + 1]
            row = cum_t[hl:hl + 1, :]
            seg = jnp.exp(jnp.where(mask, col - row, -jnp.inf))
            xdt = (xc[:, h * HEAD_DIM:(h + 1) * HEAD_DIM] * dt[:, hl:hl + 1]).astype(BF16)
            y = jnp.dot((scores * seg).astype(BF16), xdt, preferred_element_type=F32)
            st = state_ref[h]
            y = y + jnp.dot(cm, st.astype(BF16), preferred_element_type=F32) * jnp.exp(col)
            tot = cum[end:end + 1, hl:hl + 1]
            upd = jnp.dot((bm_t * jnp.exp(tot - row)).astype(BF16), xdt, preferred_element_type=F32)
            state_ref[h] = st * jnp.exp(tot) + upd
            ys.append(y)
    y_all = jnp.concatenate(ys, axis=-1)
    if finalize:
        t = (y_all + yb_ref[0] + xc[:, 0:SSD_WIDTH] * d_ref[...]) * _silu(u_ref[0, :, 0:SSD_WIDTH])
        y_all = t * lax.rsqrt(jnp.mean(t * t, axis=-1, keepdims=True) + EPS) * nw_ref[...]
    y_ref[0] = y_all

    @pl.when(j == nc - 1)
    def _():
        fin_ref[0] = state_ref[...]


def ssd_pass(u, y_other, init, sp, direction, q):
    b, L, _ = u.shape
    q = min(q, L)
    nc = L // q
    finalize = y_other is not None
    in_specs, cidx = _scan_specs(L, q, nc, SSD_COLS, direction)
    const2 = lambda bi, j: (0, 0)
    st_spec = pl.BlockSpec((1, SSD_HEADS, HEAD_DIM, SSD_STATE), lambda bi, j: (bi, 0, 0, 0))
    y_spec = pl.BlockSpec((1, q, SSD_WIDTH), lambda bi, j: (bi, cidx(j), 0))
    args = [u, u, u]
    if finalize:
        in_specs.append(y_spec)
        args.append(y_other)
    consts = [sp['cw'], sp['cb'], sp['dtb'], sp['a']] + ([sp['d'], sp['nw']] if finalize else [])
    in_specs += [st_spec] + [pl.BlockSpec(t.shape, const2) for t in consts]
    args += [init] + consts
    return pl.pallas_call(
        functools.partial(_ssd_kernel, direction=direction, finalize=finalize, q=q, nc=nc),
        out_shape=(jax.ShapeDtypeStruct((b, L, SSD_WIDTH), F32),
                   jax.ShapeDtypeStruct((b, SSD_HEADS, HEAD_DIM, SSD_STATE), F32)),
        grid=(b, nc),
        in_specs=in_specs,
        out_specs=(y_spec, st_spec),
        scratch_shapes=[pltpu.VMEM((SSD_HEADS, HEAD_DIM, SSD_STATE), F32)],
        compiler_params=_params("parallel", "arbitrary"),
        name="ssd_scan_%s" % ("fwd" if direction == 0 else "bwd"),
    )(*args)


def ssd_prepare(p):
    pad = lambda v: jnp.pad(v.reshape(1, -1), ((0, 0), (0, LANE - v.size)))
    return dict(cw=p['ssd_conv_w'], cb=p['ssd_conv_b'].reshape(1, -1),
                dtb=pad(p['ssd_dt_bias']), a=pad(-jnp.exp(p['ssd_a_log'])),
                d=jnp.repeat(p['ssd_d'], HEAD_DIM).reshape(1, -1), nw=p['ssd_norm_w'].reshape(1, -1))


def ssd_mixer(u, sp, init_f, init_b, want_y, q=256):
    yb, fin_b = ssd_pass(u, None, init_b, sp, 1, q)
    y, fin_f = ssd_pass(u, yb if want_y else None, init_f, sp, 0, q)
    return y, fin_f, fin_b


def _ml_kernel(*refs, direction, finalize, q, nc):
    if finalize:
        (u_ref, prev_ref, next_ref, hb_ref, s_init_ref, m_init_ref, cw_ref, cb_ref, gb_ref, nw_ref, pool_ref,
         y_ref, s_fin_ref, m_fin_ref, s_ref, m_ref) = refs
    else:
        (u_ref, prev_ref, next_ref, s_init_ref, m_init_ref, cw_ref, cb_ref, gb_ref,
         y_ref, s_fin_ref, m_fin_ref, s_ref, m_ref) = refs
    j = pl.program_id(1)
    c = j if direction == 0 else nc - 1 - j

    @pl.when(j == 0)
    def _():
        s_ref[...] = s_init_ref[0]
        m_ref[...] = m_init_ref[0]

    prev_row = jnp.where(c > 0, prev_ref[0, SUBLANE - 1:SUBLANE, 0:ML_V0], 0.0)
    next_row = jnp.where(c < nc - 1, next_ref[0, 0:1, 0:ML_V0], 0.0)
    qk = _silu(_conv3(u_ref[0, :, 0:ML_V0], prev_row, next_row, cw_ref[...], cb_ref[...], q))
    v = u_ref[0, :, ML_V0:ML_O0]

    gb = u_ref[0, :, ML_G0:ML_COLS] + gb_ref[...]
    mask = _scan_mask(q, direction)
    cum = jnp.dot(mask.astype(F32), _log_sigmoid(gb), preferred_element_type=F32, precision=HI)
    ig = pltpu.roll(gb, ML_HEADS, axis=1)
    end = q - 1 if direction == 0 else 0
    m_prev = m_ref[0:1, :]
    tot = cum[end:end + 1, :]
    w_end = tot - cum + ig
    m_loc = jnp.max(w_end, axis=0, keepdims=True)
    e_end = jnp.exp(w_end - m_loc)
    m_new = jnp.maximum(tot + m_prev, m_loc)
    a_prev = jnp.exp(tot + m_prev - m_new)
    a_loc = jnp.exp(m_loc - m_new)
    inter = cum + m_prev
    rel = ig - cum
    m_t = jnp.maximum(inter, cum + _running_max(rel, direction, q))
    col_a = cum - m_t
    a_inter = jnp.exp(inter - m_t)
    floor = jnp.exp(-m_t)
    rel_t = rel.T
    e_end_t = e_end.T
    k_t = (qk[:, ML_WIDTH:2 * ML_WIDTH] * (HEAD_DIM ** -0.5)).T
    one_col = (lax.broadcasted_iota(jnp.int32, (q, HEAD_DIM), 1) == 0).astype(F32)

    ys = []
    for h in range(ML_HEADS):
        fl = direction * 2 * ML_HEADS + ML_HEADS + h
        qh = qk[:, h * HEAD_DIM:(h + 1) * HEAD_DIM].astype(BF16)
        kh_t = k_t[h * HEAD_DIM:(h + 1) * HEAD_DIM, :]
        v_ext = jnp.concatenate([v[:, h * HEAD_DIM:(h + 1) * HEAD_DIM], one_col], axis=-1).astype(BF16)
        pw = jnp.exp(jnp.where(mask, col_a[:, fl:fl + 1] + rel_t[fl:fl + 1, :], -jnp.inf))
        scores = jnp.dot(qh, kh_t.astype(BF16), preferred_element_type=F32)
        nd = jnp.dot((scores * pw).astype(BF16), v_ext, preferred_element_type=F32)
        st = s_ref[h]
        nd = nd + a_inter[:, fl:fl + 1] * jnp.dot(qh, st.astype(BF16), preferred_element_type=F32)
        den = nd[:, HEAD_DIM:HEAD_DIM + 1]
        ys.append(nd[:, 0:HEAD_DIM] / jnp.maximum(jnp.abs(den), floor[:, fl:fl + 1]))
        upd = jnp.dot((kh_t * e_end_t[fl:fl + 1, :]).astype(BF16), v_ext, preferred_element_type=F32)
        s_ref[h] = a_prev[:, fl:fl + 1] * st + a_loc[:, fl:fl + 1] * upd
    m_ref[...] = jnp.broadcast_to(m_new, m_ref.shape)
    y_all = jnp.concatenate(ys, axis=-1)
    if finalize:
        hs = y_all + hb_ref[0]
        hc = hs - jnp.dot(hs.astype(BF16), pool_ref[...], preferred_element_type=F32)
        var = jnp.dot((hc * hc).astype(BF16), pool_ref[...], preferred_element_type=F32)
        y_all = hc * lax.rsqrt(var + EPS) * nw_ref[...] * jax.nn.sigmoid(u_ref[0, :, ML_O0:ML_G0])
    y_ref[0] = y_all

    @pl.when(j == nc - 1)
    def _():
        s_fin_ref[0] = s_ref[...]
        m_fin_ref[0] = m_ref[...]


def ml_pass(u, h_other, init, mp, direction, q):
    b, L, _ = u.shape
    q = min(q, L)
    nc = L // q
    finalize = h_other is not None
    in_specs, cidx = _scan_specs(L, q, nc, ML_COLS, direction)
    const2 = lambda bi, j: (0, 0)
    s_spec = pl.BlockSpec((1, ML_HEADS, HEAD_DIM, LANE), lambda bi, j: (bi, 0, 0, 0))
    m_spec = pl.BlockSpec((1, SUBLANE, LANE), lambda bi, j: (bi, 0, 0))
    y_spec = pl.BlockSpec((1, q, ML_WIDTH), lambda bi, j: (bi, cidx(j), 0))
    args = [u, u, u]
    if finalize:
        in_specs.append(y_spec)
        args.append(h_other)
    consts = [mp['cw'], mp['cb'], mp['gb']] + ([mp['nw'], mp['pool']] if finalize else [])
    in_specs += [s_spec, m_spec] + [pl.BlockSpec(t.shape, const2) for t in consts]
    args += [init[0], init[1]] + consts
    y, s_fin, m_fin = pl.pallas_call(
        functools.partial(_ml_kernel, direction=direction, finalize=finalize, q=q, nc=nc),
        out_shape=(jax.ShapeDtypeStruct((b, L, ML_WIDTH), F32),
                   jax.ShapeDtypeStruct((b, ML_HEADS, HEAD_DIM, LANE), F32),
                   jax.ShapeDtypeStruct((b, SUBLANE, LANE), F32)),
        grid=(b, nc),
        in_specs=in_specs,
        out_specs=(y_spec, s_spec, m_spec),
        scratch_shapes=[pltpu.VMEM((ML_HEADS, HEAD_DIM, LANE), F32), pltpu.VMEM((SUBLANE, LANE), F32)],
        compiler_params=_params("parallel", "arbitrary"),
        name="mlstm_scan_%s" % ("fwd" if direction == 0 else "bwd"),
    )(*args)
    return y, (s_fin, m_fin)


def ml_prepare(p):
    gb = p['ml_gate_b'].reshape(1, -1)
    head = np.arange(ML_WIDTH) // HEAD_DIM
    pool = jnp.asarray((head[:, None] == head[None, :]) / HEAD_DIM, BF16)
    return dict(cw=p['ml_conv_w'], cb=p['ml_conv_b'].reshape(1, -1),
                gb=jnp.pad(gb, ((0, 0), (0, LANE - gb.shape[1]))), nw=p['ml_norm_w'].reshape(1, -1), pool=pool)


def ml_mixer(u, mp, init_f, init_b, want_y, q=256):
    hb, fin_b = ml_pass(u, None, init_b, mp, 1, q)
    y, fin_f = ml_pass(u, hb if want_y else None, init_f, mp, 0, q)
    return y, fin_f, fin_b


FFT_L = 4096
FFT_N = 2 * FFT_L
FFT_N2 = 128
FFT_N1 = FFT_N // FFT_N2
FFT_N1H = FFT_L // FFT_N2
FFT_K1 = FFT_N1 // 2 + 1
FFT_R = 80
FFT_UNROLL = 8
FFT_PITCH = FFT_N2 + SUBLANE


def _fft_tables():
    n2 = np.arange(FFT_N2)[:, None, None]
    k1 = np.arange(FFT_K1)[None, :, None]
    n1 = np.arange(FFT_N1H)[None, None, :]
    th = 2 * np.pi * (((FFT_N2 * n1 + n2) * k1) % FFT_N) / FFT_N
    f1 = np.zeros((FFT_N2, FFT_R, FFT_N1H))
    f1[:, 0:2 * FFT_K1:2, :] = np.cos(th)
    f1[:, 1:2 * FFT_K1:2, :] = -np.sin(th)
    wgt = np.where((np.arange(FFT_K1) == 0) | (np.arange(FFT_K1) == FFT_N1 // 2), 1.0, 2.0)[None, :, None] / FFT_N
    g1 = np.zeros((FFT_N2, FFT_N1H, FFT_R))
    g1[:, :, 0:2 * FFT_K1:2] = np.transpose(wgt * np.cos(th), (0, 2, 1))
    g1[:, :, 1:2 * FFT_K1:2] = np.transpose(-wgt * np.sin(th), (0, 2, 1))
    ph = 2 * np.pi * ((np.arange(FFT_N2)[:, None] * np.arange(FFT_N2)[None, :]) % FFT_N2) / FFT_N2
    c, s = np.cos(ph), np.sin(ph)
    f2 = np.block([[c, s], [-s, c]])
    as_bf = lambda a: jnp.asarray(a, F32).astype(BF16)
    return as_bf(f1), as_bf(f2), as_bf(f2.T), as_bf(g1)


def _sld(ref, n2, count):
    rows = pl.ds(n2, count, stride=FFT_PITCH)
    return jnp.concatenate([ref[0, rows, :], ref[1, rows, :]], axis=-1)


def _sst(ref, n2, count, val):
    rows = pl.ds(n2, count, stride=FFT_PITCH)
    ref[0, rows, :] = val[:, 0:LANE]
    ref[1, rows, :] = val[:, LANE:2 * LANE]


def _blk_ld(ref, blk, nblk):
    parts = []
    for k in range(nblk):
        rows = pl.ds(pl.multiple_of((blk + k) * FFT_PITCH, SUBLANE), FFT_N2)
        parts.append(jnp.concatenate([ref[0, rows, :], ref[1, rows, :]], axis=-1))
    return parts[0] if nblk == 1 else jnp.concatenate(parts, axis=0)


def _blk_st(ref, blk, nblk, val):
    for k in range(nblk):
        rows = pl.ds(pl.multiple_of((blk + k) * FFT_PITCH, SUBLANE), FFT_N2)
        ref[0, rows, :] = val[k * FFT_N2:(k + 1) * FFT_N2, 0:LANE]
        ref[1, rows, :] = val[k * FFT_N2:(k + 1) * FFT_N2, LANE:2 * LANE]


def _fft_stage1(z_ref, a_ref, f1_ref):
    def body(n2, carry):
        xs = _sld(z_ref, n2, FFT_N1H).astype(BF16)
        _sst(a_ref, n2, FFT_R, jnp.dot(f1_ref[n2], xs, preferred_element_type=F32))
        return carry
    lax.fori_loop(0, FFT_N2, body, 0, unroll=FFT_UNROLL)


def _spectrum_kernel(x_ref, f1_ref, f2_ref, o_ref, z_ref, a_ref):
    for n1 in range(FFT_N1H):
        _blk_st(z_ref, n1, 1, x_ref[0, n1 * FFT_N2:(n1 + 1) * FFT_N2, :])
    _fft_stage1(z_ref, a_ref, f1_ref)

    def body(k1, carry):
        slab = _blk_ld(a_ref, 2 * k1, 2).astype(BF16)
        o_ref[0, k1] = jnp.dot(f2_ref[...], slab, preferred_element_type=F32)
        return carry
    lax.fori_loop(0, FFT_K1, body, 0, unroll=3)


def spectrum(sig):
    s, L, c = sig.shape
    f1, f2, _, _ = _fft_tables()
    one = pl.Buffered(1)
    return pl.pallas_call(
        _spectrum_kernel,
        out_shape=jax.ShapeDtypeStruct((s, FFT_K1, 2 * FFT_N2, c), F32),
        grid=(s,),
        in_specs=[pl.BlockSpec((1, L, c), lambda i: (i, 0, 0)),
                  pl.BlockSpec(f1.shape, lambda i: (0, 0, 0), pipeline_mode=one),
                  pl.BlockSpec(f2.shape, lambda i: (0, 0), pipeline_mode=one)],
        out_specs=pl.BlockSpec((1, FFT_K1, 2 * FFT_N2, c), lambda i: (i, 0, 0, 0)),
        scratch_shapes=[pltpu.VMEM((2, FFT_N1H * FFT_PITCH, LANE), F32),
                        pltpu.VMEM((2, FFT_R * FFT_PITCH, LANE), F32)],
        compiler_params=pltpu.CompilerParams(dimension_semantics=("arbitrary",), vmem_limit_bytes=VMEM_LIMIT_HY),
        name="hyena_filter_spectrum",
    )(sig, f1, f2)


def _conv3_rows(src, dst, cw, cb, L):
    rows = 2 * FFT_N2
    nchunk = L // rows

    def body(i, carry):
        r0 = pl.multiple_of(i * rows, rows)
        prev_row = jnp.where(i > 0, src[pl.ds(jnp.maximum(r0 - 1, 0), 1), :], 0.0)
        next_row = jnp.where(i < nchunk - 1, src[pl.ds(jnp.minimum(r0 + rows, L - 1), 1), :], 0.0)
        _blk_st(dst, 2 * i, 2, _conv3(src[pl.ds(r0, rows), :], prev_row, next_row, cw, cb, rows))
        return carry
    lax.fori_loop(0, nchunk, body, 0, unroll=2)


def _hyena_kernel(v_ref, g_ref, h_ref, cwv_ref, cbv_ref, cwg_ref, cbg_ref, skip_ref,
                  f1_ref, f2_ref, f2t_ref, g1_ref, o_ref, z_ref, gc_ref, a_ref):
    order = pl.program_id(1)

    @pl.when(order == 0)
    def _():
        _conv3_rows(v_ref.at[0], z_ref, cwv_ref[...], cbv_ref[...], FFT_L)

    _conv3_rows(g_ref.at[0], gc_ref, cwg_ref[0], cbg_ref[0], FFT_L)
    _fft_stage1(z_ref, a_ref, f1_ref)

    def mid(k1, carry):
        x = jnp.dot(f2_ref[...], _blk_ld(a_ref, 2 * k1, 2).astype(BF16), preferred_element_type=F32)
        h = h_ref[0, k1].astype(F32)
        xr, xi, hr, hi = x[:FFT_N2], x[FFT_N2:], h[:FFT_N2], h[FFT_N2:]
        y = jnp.concatenate([xr * hr - xi * hi, xr * hi + xi * hr], axis=0).astype(BF16)
        _blk_st(a_ref, 2 * k1, 2, jnp.dot(f2t_ref[...], y, preferred_element_type=F32))
        return carry
    lax.fori_loop(0, FFT_K1, mid, 0, unroll=3)

    skip = skip_ref[0]

    def last(n2, carry):
        bs = _sld(a_ref, n2, FFT_R).astype(BF16)
        y = jnp.dot(g1_ref[n2], bs, preferred_element_type=F32)
        _sst(z_ref, n2, FFT_N1H, _sld(gc_ref, n2, FFT_N1H) * (y + skip * _sld(z_ref, n2, FFT_N1H)))
        return carry
    lax.fori_loop(0, FFT_N2, last, 0, unroll=FFT_UNROLL)

    @pl.when(order == 1)
    def _():
        for n1 in range(FFT_N1H):
            o_ref[0, n1 * FFT_N2:(n1 + 1) * FFT_N2, :] = _blk_ld(z_ref, n1, 1)


def hyena_long(u, h_spec, cw, cb, skip):
    b, L, _ = u.shape
    c = HYENA_WIDTH
    f1, f2, f2t, g1 = _fft_tables()
    one = pl.Buffered(1)
    cw3 = cw.reshape(3, 3, c).transpose(1, 0, 2)
    cb3 = cb.reshape(3, 1, c)
    return pl.pallas_call(
        _hyena_kernel,
        out_shape=jax.ShapeDtypeStruct((b, L, c), F32),
        grid=(b, 2),
        in_specs=[pl.BlockSpec((1, L, c), lambda bi, o: (bi, 0, 0), pipeline_mode=one),
                  pl.BlockSpec((1, L, c), lambda bi, o: (bi, 0, 1 + o)),
                  pl.BlockSpec((1, FFT_K1, 2 * FFT_N2, c), lambda bi, o: (o, 0, 0, 0)),
                  pl.BlockSpec((3, c), lambda bi, o: (0, 0)),
                  pl.BlockSpec((1, c), lambda bi, o: (0, 0)),
                  pl.BlockSpec((1, 3, c), lambda bi, o: (1 + o, 0, 0)),
                  pl.BlockSpec((1, 1, c), lambda bi, o: (1 + o, 0, 0)),
                  pl.BlockSpec((1, 1, c), lambda bi, o: (o, 0, 0)),
                  pl.BlockSpec(f1.shape, lambda bi, o: (0, 0, 0), pipeline_mode=one),
                  pl.BlockSpec(f2.shape, lambda bi, o: (0, 0), pipeline_mode=one),
                  pl.BlockSpec(f2t.shape, lambda bi, o: (0, 0), pipeline_mode=one),
                  pl.BlockSpec(g1.shape, lambda bi, o: (0, 0, 0), pipeline_mode=one)],
        out_specs=pl.BlockSpec((1, L, c), lambda bi, o: (bi, 0, 0)),
        scratch_shapes=[pltpu.VMEM((2, FFT_N1H * FFT_PITCH, LANE), F32),
                        pltpu.VMEM((2, FFT_N1H * FFT_PITCH, LANE), F32),
                        pltpu.VMEM((2, FFT_R * FFT_PITCH, LANE), F32)],
        compiler_params=pltpu.CompilerParams(dimension_semantics=("parallel", "arbitrary"),
                                             vmem_limit_bytes=VMEM_LIMIT_HY),
        name="hyena_long_conv",
    )(u, u, h_spec, cw3[0], cb3[0], cw3, cb3, skip.reshape(2, 1, c), f1, f2, f2t, g1)


def hyena_filter_spectra(sig, inv_norm):
    spec = spectrum(sig)
    sign = jnp.where(jnp.arange(FFT_K1) % 2 == 0, 1.0, -1.0)[None, :, None, None]
    spec = spec.reshape(HYENA_ORDER, 2, *spec.shape[1:])
    return ((spec[:, 0] + sign * spec[:, 1]) * inv_norm[:, None]).astype(BF16)


HY_FILT = 2 * HYENA_ORDER * HYENA_WIDTH
HY_HALF = HYENA_ORDER * HYENA_WIDTH


def _filter_kernel(wt_ref, wc_ref, ws_ref, b1_ref, w2_ref, b2_ref, w3_ref, freq_ref, decay_ref,
                   sig_ref, asum_ref, *, L, rows):
    i = pl.program_id(0)
    n = (i * rows + lax.broadcasted_iota(jnp.int32, (rows, 1), 0)).astype(F32)
    band = lax.broadcasted_iota(jnp.int32, (1, HYENA_BANDS), 1).astype(F32)
    bands = 1e-4 + band * ((HYENA_BANDS - 1 - 1e-4) / (HYENA_BANDS - 1))
    freq = freq_ref[...]

    @pl.when(i == 0)
    def _():
        asum_ref[...] = jnp.zeros_like(asum_ref)

    for side, t in enumerate((n, L - n)):
        t_unit = t / float(max(L - 1, 1))
        ang = (2 * math.pi / L) * t * bands
        pre = (t_unit * wt_ref[...] + jnp.dot(jnp.cos(ang), wc_ref[...], preferred_element_type=F32, precision=HI)
               - jnp.dot(jnp.sin(ang), ws_ref[...], preferred_element_type=F32, precision=HI) + b1_ref[...])
        hdn = jnp.sin(freq * pre)
        hdn = jnp.sin(freq * (jnp.dot(hdn, w2_ref[...], preferred_element_type=F32, precision=HI) + b2_ref[...]))
        cols = slice(side * HY_HALF, (side + 1) * HY_HALF)
        val = jnp.dot(hdn, w3_ref[:, cols], preferred_element_type=F32, precision=HI)
        val = val * jnp.exp(-t_unit * jnp.abs(decay_ref[:, cols]))
        if side == 1:
            val = jnp.where(n > 0, val, 0.0)
        for o in range(HYENA_ORDER):
            sig_ref[2 * o + side] = val[:, o * HYENA_WIDTH:(o + 1) * HYENA_WIDTH]
        asum_ref[side:side + 1, :] += jnp.sum(jnp.abs(val), axis=0, keepdims=True)


def hyena_filter_signals(L, p):
    rows = min(L, 512)
    fh = p['hy_pos_w1'].shape[1]
    w1 = p['hy_pos_w1']
    consts = [w1[0:1], w1[1:1 + HYENA_BANDS], w1[1 + HYENA_BANDS:], p['hy_pos_b1'].reshape(1, fh), p['hy_pos_w2'],
              p['hy_pos_b2'].reshape(1, fh), p['hy_pos_w3'], p['hy_freq'].reshape(1, fh),
              p['hy_decay'].reshape(1, HY_FILT)]
    sig, asum = pl.pallas_call(
        functools.partial(_filter_kernel, L=L, rows=rows),
        out_shape=(jax.ShapeDtypeStruct((2 * HYENA_ORDER, L, HYENA_WIDTH), F32),
                   jax.ShapeDtypeStruct((SUBLANE, HY_HALF), F32)),
        grid=(L // rows,),
        in_specs=[pl.BlockSpec(c.shape, lambda i: (0, 0)) for c in consts],
        out_specs=(pl.BlockSpec((2 * HYENA_ORDER, rows, HYENA_WIDTH), lambda i: (0, i, 0)),
                   pl.BlockSpec((SUBLANE, HY_HALF), lambda i: (0, 0))),
        compiler_params=_params("arbitrary"),
        name="hyena_filters",
    )(*consts)
    inv_norm = 1.0 / (asum[0] + asum[1]).reshape(HYENA_ORDER, 1, HYENA_WIDTH)
    return sig, inv_norm


def _rdft_tables(L):
    n_bins = L + 1
    half = -(-n_bins // 16) * 16
    k = np.arange(n_bins)[:, None]
    n = np.arange(L)[None, :]
    th = 2 * np.pi * ((k * n) % (2 * L)) / (2 * L)
    f = np.zeros((2 * half, L))
    f[:n_bins] = np.cos(th)
    f[half:half + n_bins] = -np.sin(th)
    sign = np.where(np.arange(n_bins) % 2 == 0, 1.0, -1.0)[:, None]
    fs = np.zeros_like(f)
    fs[:n_bins] = f[:n_bins] * sign
    fs[half:half + n_bins] = f[half:half + n_bins] * sign
    wgt = np.where((np.arange(n_bins) == 0) | (np.arange(n_bins) == L), 1.0, 2.0)[None, :] / (2 * L)
    g = np.zeros((L, 2 * half))
    g[:, :n_bins] = wgt * np.cos(th).T
    g[:, half:half + n_bins] = -wgt * np.sin(th).T
    as_bf = lambda a: jnp.asarray(a, F32).astype(BF16)
    return as_bf(f), as_bf(fs), as_bf(g), half


def _hyena_ctx_kernel(u_ref, sig_ref, inorm_ref, cw_ref, cb_ref, skip_ref, f_ref, fs_ref, g_ref, o_ref, *, L, half):
    zero_row = jnp.zeros((1, u_ref.shape[2]), F32)
    uc = _conv3(u_ref[0], zero_row, zero_row, cw_ref[...], cb_ref[...], L)
    z = uc[:, 0:HYENA_WIDTH]
    for o in range(HYENA_ORDER):
        h = (jnp.dot(f_ref[...], sig_ref[2 * o].astype(BF16), preferred_element_type=F32)
             + jnp.dot(fs_ref[...], sig_ref[2 * o + 1].astype(BF16), preferred_element_type=F32)) * inorm_ref[o]
        x = jnp.dot(f_ref[...], z.astype(BF16), preferred_element_type=F32)
        xr, xi, hr, hi = x[:half], x[half:], h[:half], h[half:]
        y = jnp.concatenate([xr * hr - xi * hi, xr * hi + xi * hr], axis=0).astype(BF16)
        conv = jnp.dot(g_ref[...], y, preferred_element_type=F32)
        z = uc[:, (o + 1) * HYENA_WIDTH:(o + 2) * HYENA_WIDTH] * (conv + skip_ref[o] * z)
    o_ref[0] = z


def hyena_short(u, sig, inv_norm, cw, cb, skip):
    b, L, cols = u.shape
    f, fs, g, half = _rdft_tables(L)
    const2 = lambda bi: (0, 0)
    const3 = lambda bi: (0, 0, 0)
    return pl.pallas_call(
        functools.partial(_hyena_ctx_kernel, L=L, half=half),
        out_shape=jax.ShapeDtypeStruct((b, L, HYENA_WIDTH), F32),
        grid=(b,),
        in_specs=[pl.BlockSpec((1, L, cols), lambda bi: (bi, 0, 0)),
                  pl.BlockSpec(sig.shape, const3), pl.BlockSpec(inv_norm.shape, const3),
                  pl.BlockSpec(cw.shape, const2), pl.BlockSpec((1, cols), const2),
                  pl.BlockSpec((HYENA_ORDER, 1, HYENA_WIDTH), const3),
                  pl.BlockSpec(f.shape, const2), pl.BlockSpec(fs.shape, const2), pl.BlockSpec(g.shape, const2)],
        out_specs=pl.BlockSpec((1, L, HYENA_WIDTH), lambda bi: (bi, 0, 0)),
        compiler_params=_params("parallel"),
        name="hyena_context",
    )(u, sig, inv_norm, cw, cb.reshape(1, cols), skip.reshape(HYENA_ORDER, 1, HYENA_WIDTH), f, fs, g)


def _split_bf16(a):
    hi = a.astype(BF16)
    return hi, (a - hi.astype(F32)).astype(BF16)


def _norm_router_kernel(*refs, n_lat, two_src):
    if two_src:
        xl_ref, xc_ref, nw_ref, sh_ref, sc_ref, whi_ref, wlo_ref, br_ref, tri_ref, h_ref, lg_ref, cnt_ref = refs
        x = jnp.where(pl.program_id(0) < n_lat, xl_ref[...], xc_ref[...])
    else:
        xl_ref, nw_ref, sh_ref, sc_ref, whi_ref, wlo_ref, br_ref, tri_ref, h_ref, lg_ref, cnt_ref = refs
        x = xl_ref[...]
    h = _modnorm(x, nw_ref[...], sh_ref[0], sc_ref[0])
    h_ref[...] = h
    h_hi, h_lo = _split_bf16(h)
    lg = (jnp.dot(h_hi, whi_ref[...], preferred_element_type=F32)
          + jnp.dot(h_hi, wlo_ref[...], preferred_element_type=F32)
          + jnp.dot(h_lo, whi_ref[...], preferred_element_type=F32)) + br_ref[...]
    lane = lax.broadcasted_iota(jnp.int32, lg.shape, 1)
    first = lambda hit: jnp.min(jnp.where(hit, lane, ROUTER_COLS), axis=-1, keepdims=True)
    gl = jnp.where(lane < N_GROUPS, lg, -jnp.inf)
    gmax = jnp.max(gl, axis=-1, keepdims=True)
    grp = first(gl == gmax)
    grp_p = 1.0 / jnp.sum(jnp.exp(gl - gmax), axis=-1, keepdims=True)
    lo = N_GROUPS + grp * EXPERTS_PER_GROUP
    el = jnp.where((lane >= lo) & (lane < lo + EXPERTS_PER_GROUP), lg, -jnp.inf)
    e1 = jnp.max(el, axis=-1, keepdims=True)
    i1 = first(el == e1)
    el2 = jnp.where(lane == i1, -jnp.inf, el)
    e2 = jnp.max(el2, axis=-1, keepdims=True)
    i2 = first(el2 == e2)
    r = jnp.exp(e2 - e1)
    w1 = grp_p / (1.0 + r)
    w2 = w1 * r
    @pl.when(pl.program_id(0) == 0)
    def _():
        cnt_ref[...] = jnp.zeros_like(cnt_ref)

    hit1, hit2 = lane == i1, lane == i2
    picks = jnp.where(hit1 | hit2, 1.0, 0.0)
    before = cnt_ref[0:1, :] + jnp.dot(tri_ref[...], picks.astype(BF16), preferred_element_type=F32)
    rank1 = jnp.sum(jnp.where(hit1, before, 0.0), axis=-1, keepdims=True)
    rank2 = jnp.sum(jnp.where(hit2, before, 0.0), axis=-1, keepdims=True)
    cnt_ref[...] = jnp.broadcast_to(cnt_ref[0:1, :] + jnp.sum(picks, axis=0, keepdims=True), cnt_ref.shape)
    vals = [(i1 - N_GROUPS).astype(F32), (i2 - N_GROUPS).astype(F32), w1, w2, rank1, rank2]
    out = jnp.zeros(lg.shape, F32)
    for k, val in enumerate(vals):
        out = jnp.where(lane == k, val, out)
    lg_ref[...] = out


def norm_router(xl, xc, nw, mod_l, mod_c, w_router, b_router, tm=512):
    b, L, d = xl.shape
    two_src = xc is not None
    n_lat = b * L // tm
    per_batch = L // tm
    n_ctx = (xc.shape[0] * xc.shape[1]) // tm if two_src else 0
    w_hi, w_lo = _split_bf16(w_router)
    const = lambda i: (0, 0)
    if two_src:
        shift = jnp.concatenate([mod_l[0], mod_c[0]], axis=0)
        scale = jnp.concatenate([mod_l[1], mod_c[1]], axis=0)
        mod_map = lambda i: (jnp.where(i < n_lat, i // per_batch, b), 0, 0)
        srcs = [xl.reshape(b * L, d), xc.reshape(-1, d)]
        src_specs = [pl.BlockSpec((tm, d), lambda i: (jnp.minimum(i, n_lat - 1), 0)),
                     pl.BlockSpec((tm, d), lambda i: (jnp.maximum(i - n_lat, 0), 0))]
    else:
        shift, scale = mod_l
        mod_map = lambda i: (i // per_batch, 0, 0)
        srcs = [xl.reshape(b * L, d)]
        src_specs = [pl.BlockSpec((tm, d), lambda i: (i, 0))]
    n_tok = (n_lat + n_ctx) * tm
    tri = jnp.asarray(np.tril(np.ones((tm, tm)), -1), BF16)
    return pl.pallas_call(
        functools.partial(_norm_router_kernel, n_lat=n_lat, two_src=two_src),
        out_shape=(jax.ShapeDtypeStruct((n_tok, d), F32), jax.ShapeDtypeStruct((n_tok, ROUTER_COLS), F32),
                   jax.ShapeDtypeStruct((SUBLANE, ROUTER_COLS), F32)),
        grid=(n_lat + n_ctx,),
        in_specs=src_specs + [pl.BlockSpec((1, d), const), pl.BlockSpec((1, 1, d), mod_map),
                              pl.BlockSpec((1, 1, d), mod_map), pl.BlockSpec((d, ROUTER_COLS), const),
                              pl.BlockSpec((d, ROUTER_COLS), const), pl.BlockSpec((1, ROUTER_COLS), const),
                              pl.BlockSpec((tm, tm), const)],
        out_specs=(pl.BlockSpec((tm, d), lambda i: (i, 0)), pl.BlockSpec((tm, ROUTER_COLS), lambda i: (i, 0)),
                   pl.BlockSpec((SUBLANE, ROUTER_COLS), const)),
        compiler_params=_params("arbitrary"),
        name="moe_norm_router",
    )(*srcs, nw.reshape(1, d), shift, scale, w_hi, w_lo, b_router, tri)


def _expert_ffn_kernel(te_ref, tv_ref, x_ref, rw_ref, wg_ref, wu_ref, wd_ref, o_ref):
    i = pl.program_id(0)

    @pl.when(tv_ref[i] > 0)
    def _():
        x = x_ref[...].astype(BF16)
        g = jnp.dot(x, wg_ref[0].astype(BF16), preferred_element_type=F32)
        u = jnp.dot(x, wu_ref[0].astype(BF16), preferred_element_type=F32)
        hid = _silu(g) * u * rw_ref[...]
        o_ref[...] = jnp.dot(hid.astype(BF16), wd_ref[0].astype(BF16), preferred_element_type=F32)

    @pl.when(tv_ref[i] == 0)
    def _():
        o_ref[...] = jnp.zeros_like(o_ref)


def expert_ffn(x_sorted, row_w, tile_expert, tile_valid, w_gate, w_up, w_down, tm):
    r, d = x_sorted.shape
    f = w_gate.shape[-1]
    grid_spec = pltpu.PrefetchScalarGridSpec(
        num_scalar_prefetch=2,
        grid=(r // tm,),
        in_specs=[pl.BlockSpec((tm, d), lambda i, te, tv: (i, 0)),
                  pl.BlockSpec((tm, 1), lambda i, te, tv: (i, 0)),
                  pl.BlockSpec((1, d, f), lambda i, te, tv: (te[i], 0, 0)),
                  pl.BlockSpec((1, d, f), lambda i, te, tv: (te[i], 0, 0)),
                  pl.BlockSpec((1, f, d), lambda i, te, tv: (te[i], 0, 0))],
        out_specs=pl.BlockSpec((tm, d), lambda i, te, tv: (i, 0)),
    )
    return pl.pallas_call(
        _expert_ffn_kernel,
        out_shape=jax.ShapeDtypeStruct((r, d), F32),
        grid_spec=grid_spec,
        compiler_params=_params("arbitrary"),
        name="moe_expert_ffn",
    )(tile_expert, tile_valid, x_sorted, row_w, w_gate, w_up, w_down)


def moe_apply(h_tokens, routed, counts_row, w_gate, w_up, w_down, layer, n_lat, tm=256):
    t, d = h_tokens.shape
    e_idx, e_w, rank = routed[:, 0:2].astype(jnp.int32), routed[:, 2:4], routed[:, 4:6].astype(jnp.int32)
    counts = counts_row[0, N_GROUPS:N_GROUPS + N_EXPERTS].astype(jnp.int32)
    n_pairs = 2 * t
    padded = (counts + tm - 1) // tm * tm
    pad_end = jnp.cumsum(padded)
    pad_start = pad_end - padded
    n_rows = n_pairs + N_EXPERTS * tm
    tile_start = jnp.arange(n_rows // tm, dtype=jnp.int32) * tm
    tile_expert = jnp.minimum(jnp.sum((pad_end[None, :] <= tile_start[:, None]).astype(jnp.int32), axis=1),
                              N_EXPERTS - 1)
    tile_valid = (tile_start < pad_end[-1]).astype(jnp.int32)
    onehot = (e_idx[:, :, None] == jnp.arange(N_EXPERTS, dtype=jnp.int32)).astype(jnp.int32)
    pos = jnp.sum(onehot * pad_start, axis=-1) + rank
    j = jnp.arange(tm, dtype=jnp.int32)[None, :]
    fill_key = jnp.where(j < (padded - counts)[:, None], (pad_start + counts)[:, None] + j, n_rows)
    keys = jnp.concatenate([pos.reshape(-1), fill_key.reshape(-1)])
    toks = jnp.concatenate([jnp.arange(n_pairs, dtype=jnp.int32) // 2, jnp.arange(N_EXPERTS * tm, dtype=jnp.int32) % t])
    wts = jnp.concatenate([e_w.reshape(-1), jnp.zeros((N_EXPERTS * tm,), F32)])
    _, row_token, row_w = lax.sort((keys, toks, wts), num_keys=1)
    x_sorted = h_tokens[row_token]
    wg = w_gate.reshape(-1, d, EXPERT_HIDDEN)
    wu = w_up.reshape(-1, d, EXPERT_HIDDEN)
    wd = w_down.reshape(-1, EXPERT_HIDDEN, d)
    y_sorted = expert_ffn(x_sorted, row_w[:, None], tile_expert + layer * N_EXPERTS, tile_valid, wg, wu, wd, tm)
    lat = (y_sorted[pos[:n_lat, 0]], y_sorted[pos[:n_lat, 1]])
    rest = (y_sorted[pos[n_lat:, 0]], y_sorted[pos[n_lat:, 1]]) if t > n_lat else None
    return lat, rest


def _final_kernel(x_ref, ya_ref, yb_ref, g_ref, w_ref, o_ref):
    x = x_ref[0] + g_ref[0] * (ya_ref[0] + yb_ref[0])
    o_ref[0] = x * lax.rsqrt(jnp.mean(x * x, axis=-1, keepdims=True) + EPS) * w_ref[...]


def final_norm(x, ya, yb, gate, w, tm=512):
    b, L, d = x.shape
    tok = pl.BlockSpec((1, tm, d), lambda bi, i: (bi, i, 0))
    return pl.pallas_call(
        _final_kernel,
        out_shape=jax.ShapeDtypeStruct((b, L, d), F32),
        grid=(b, L // tm),
        in_specs=[tok, tok, tok, pl.BlockSpec((1, 1, d), _mod_map(gate, b)),
                  pl.BlockSpec((1, d), lambda bi, i: (0, 0))],
        out_specs=tok,
        compiler_params=_params("parallel", "arbitrary"),
        name="final_rmsnorm",
    )(x, ya, yb, gate, w.reshape(1, d))


def _regroup_in_weight(w_in):
    sizes = (SSD_WIDTH, SSD_CONV_CH, 2 * SSD_HEADS, HY_COLS, 2 * ML_WIDTH, ML_WIDTH, ML_WIDTH, 4 * ML_HEADS)
    parts, s = [], 0
    for n in sizes:
        parts.append(jnp.pad(w_in[:, s:s + n], ((0, 0), (0, -n % LANE))))
        s += n
    return jnp.concatenate(parts, axis=1).astype(BF16)


def kernel(x, c, ctx, c_ctx, w_mod, b_mod, norm1_w, norm2_w, w_in, w_out, ssd_conv_w, ssd_conv_b, ssd_dt_bias, ssd_a_log, ssd_d, ssd_norm_w, hy_conv_w, hy_conv_b, hy_pos_w1, hy_pos_b1, hy_pos_w2, hy_pos_b2, hy_pos_w3, hy_freq, hy_decay, hy_skip, ml_conv_w, ml_conv_b, ml_gate_b, ml_norm_w, grp_router_w, grp_router_b, exp_router_w, exp_router_b, moe_w_gate, moe_w_up, moe_w_down, final_norm_w):
    layer_params = dict(
        ssd_conv_w=ssd_conv_w, ssd_conv_b=ssd_conv_b, ssd_dt_bias=ssd_dt_bias, ssd_a_log=ssd_a_log,
        ssd_d=ssd_d, ssd_norm_w=ssd_norm_w, hy_conv_w=hy_conv_w, hy_conv_b=hy_conv_b,
        hy_pos_w1=hy_pos_w1, hy_pos_b1=hy_pos_b1, hy_pos_w2=hy_pos_w2, hy_pos_b2=hy_pos_b2,
        hy_pos_w3=hy_pos_w3, hy_freq=hy_freq, hy_decay=hy_decay, hy_skip=hy_skip,
        ml_conv_w=ml_conv_w, ml_conv_b=ml_conv_b, ml_gate_b=ml_gate_b, ml_norm_w=ml_norm_w)
    bsz, seq, d = x.shape
    n_ctx = ctx.shape[1]
    xl, xc = x, ctx
    moe_l = moe_c = None
    ssd0 = jnp.zeros((bsz, SSD_HEADS, HEAD_DIM, SSD_STATE), F32)
    ml0 = (jnp.zeros((bsz, ML_HEADS, HEAD_DIM, LANE), F32), jnp.zeros((bsz, SUBLANE, LANE), F32))
    c_rows = jnp.concatenate([c, c_ctx[None, :], jnp.zeros((SUBLANE - bsz - 1, d), F32)], axis=0)
    for i in range(DEPTH):
        last = i == DEPTH - 1
        p = {name: arr[i] for name, arr in layer_params.items()}
        sp, mp = ssd_prepare(p), ml_prepare(p)
        mod = modulation(c_rows, w_mod[i], b_mod[i]).reshape(SUBLANE, N_MOD, 1, d)
        mod_l = [mod[:bsz, k] for k in range(N_MOD)]
        mod_c = [mod[bsz:bsz + 1, k] for k in range(N_MOD)]
        w_in_p = _regroup_in_weight(w_in[i])
        w_out_b = w_out[i].astype(BF16)
        w_router = jnp.pad(jnp.concatenate([grp_router_w[i], exp_router_w[i]], axis=1),
                           ((0, 0), (0, ROUTER_COLS - N_GROUPS - N_EXPERTS)))
        b_router = jnp.pad(jnp.concatenate([grp_router_b[i], exp_router_b[i]]),
                           (0, ROUTER_COLS - N_GROUPS - N_EXPERTS)).reshape(1, ROUTER_COLS)

        uc_ssd, uc_hy, uc_ml, xc = norm_proj(xc, moe_c, norm1_w[i], mod_c[0], mod_c[1], w_in_p)
        yc_ssd, ssd_f, ssd_b = ssd_mixer(uc_ssd, sp, ssd0, ssd0, not last)
        yc_ml, ml_f, ml_b = ml_mixer(uc_ml, mp, ml0, ml0, not last)
        col_major = i % 2 == 1
        ul_ssd, ul_hy, ul_ml, xl = norm_proj(xl, moe_l, norm1_w[i], mod_l[0], mod_l[1], w_in_p, col_major)
        yl_ssd, _, _ = ssd_mixer(ul_ssd, sp, ssd_f, ssd_b, True)
        yl_ml, _, _ = ml_mixer(ul_ml, mp, ml_f, ml_b, True)
        h_spec = hyena_filter_spectra(*hyena_filter_signals(seq, p))
        yl_hy = hyena_long(ul_hy, h_spec, p['hy_conv_w'], p['hy_conv_b'], p['hy_skip'])
        xl = out_proj(yl_ssd, yl_hy, yl_ml, xl, mod_l[2], w_out_b, col_major)
        if not last:
            sig_c, inorm_c = hyena_filter_signals(n_ctx, p)
            yc_hy = hyena_short(uc_hy, sig_c, inorm_c, p['hy_conv_w'], p['hy_conv_b'], p['hy_skip'])
            xc = out_proj(yc_ssd, yc_hy, yc_ml, xc, mod_c[2], w_out_b)
        h_all, routed, counts_row = norm_router(xl, None if last else xc, norm2_w[i], (mod_l[3], mod_l[4]), (mod_c[3], mod_c[4]),
                                    w_router, b_router)
        lat, rest = moe_apply(h_all, routed, counts_row, moe_w_gate, moe_w_up, moe_w_down, i, bsz * seq)
        moe_l = (lat[0].reshape(bsz, seq, d), lat[1].reshape(bsz, seq, d), mod_l[5])
        if not last:
            moe_c = (rest[0].reshape(bsz, n_ctx, d), rest[1].reshape(bsz, n_ctx, d), mod_c[5])
    return final_norm(xl, *moe_l, final_norm_w)
```

```python
import functools
import math

import jax
import jax.numpy as jnp
import numpy as np
from jax import lax
from jax.experimental import pallas as pl
from jax.experimental.pallas import tpu as pltpu

D_MODEL = 1024
DEPTH = 2
GRID_W = 64
HEAD_DIM = 64
SSD_WIDTH = 384
SSD_HEADS = SSD_WIDTH // HEAD_DIM
SSD_GROUPS = 2
SSD_STATE = 64
HYENA_WIDTH = 256
HYENA_ORDER = 2
HYENA_BANDS = 16
ML_WIDTH = 384
ML_HEADS = ML_WIDTH // HEAD_DIM
N_GROUPS = 4
EXPERTS_PER_GROUP = 8
N_EXPERTS = N_GROUPS * EXPERTS_PER_GROUP
EXPERT_HIDDEN = 256
N_MOD = 6
EPS = 1e-6

LANE = 128
SUBLANE = 8
VMEM_LIMIT = 48 * 1024 * 1024
VMEM_LIMIT_HY = 56 * 1024 * 1024

SSD_CONV_CH = SSD_WIDTH + 2 * SSD_GROUPS * SSD_STATE
SSD_XBC0 = SSD_WIDTH
SSD_DT0 = SSD_XBC0 + SSD_CONV_CH
SSD_COLS = SSD_DT0 + LANE
HY_COLS = (HYENA_ORDER + 1) * HYENA_WIDTH
ML_V0 = 2 * ML_WIDTH
ML_O0 = ML_V0 + ML_WIDTH
ML_G0 = ML_O0 + ML_WIDTH
ML_COLS = ML_G0 + LANE
ROUTER_COLS = LANE

F32 = jnp.float32
BF16 = jnp.bfloat16
HI = lax.Precision.HIGHEST


def _params(*sem):
    return pltpu.CompilerParams(dimension_semantics=sem, vmem_limit_bytes=VMEM_LIMIT)


def _silu(x):
    return x * jax.nn.sigmoid(x)


def _softplus(x):
    return jnp.maximum(x, 0.0) + jnp.log(1.0 + jnp.exp(-jnp.abs(x)))


def _log_sigmoid(x):
    return jnp.minimum(x, 0.0) - jnp.log(1.0 + jnp.exp(-jnp.abs(x)))


def _mod_kernel(c_ref, w_ref, b_ref, o_ref):
    o_ref[...] = jnp.dot(_silu(c_ref[...]), w_ref[...], preferred_element_type=F32, precision=HI) + b_ref[...]


def modulation(c_rows, w_mod, b_mod):
    n = w_mod.shape[1]
    tn = 1536
    return pl.pallas_call(
        _mod_kernel,
        out_shape=jax.ShapeDtypeStruct((c_rows.shape[0], n), F32),
        grid=(n // tn,),
        in_specs=[pl.BlockSpec(c_rows.shape, lambda j: (0, 0)),
                  pl.BlockSpec((w_mod.shape[0], tn), lambda j: (0, j)),
                  pl.BlockSpec((1, tn), lambda j: (0, j))],
        out_specs=pl.BlockSpec((c_rows.shape[0], tn), lambda j: (0, j)),
        compiler_params=_params("arbitrary"),
        name="adaln_modulation",
    )(c_rows, w_mod, b_mod.reshape(1, n))


def _modnorm(x, nw, shift, scale):
    y = x * lax.rsqrt(jnp.mean(x * x, axis=-1, keepdims=True) + EPS) * nw
    return y * (1.0 + scale) + shift


def _mod_map(mod, b):
    return (lambda bi, i: (bi, 0, 0)) if mod.shape[0] == b else (lambda bi, i: (0, 0, 0))


def _tok_view(x, col_major):
    b, L, d = x.shape
    return x.reshape(b, L // GRID_W, GRID_W, d) if col_major else x


def _tok_spec(L, d, tm, col_major):
    if col_major:
        assert tm == (L // GRID_W) * SUBLANE
        return pl.BlockSpec((1, L // GRID_W, SUBLANE, d), lambda bi, i: (bi, 0, i, 0))
    return pl.BlockSpec((1, tm, d), lambda bi, i: (bi, i, 0))


def _tok_load(ref, col_major):
    if not col_major:
        return ref[0]
    return jnp.concatenate([ref[0, :, j, :] for j in range(ref.shape[2])], axis=0)


def _tok_store(ref, val, col_major):
    if not col_major:
        ref[0] = val
        return
    rows = ref.shape[1]
    for j in range(ref.shape[2]):
        ref[0, :, j, :] = val[j * rows:(j + 1) * rows]


def _norm_proj_kernel(*refs, col_major, fuse_moe):
    if fuse_moe:
        x_ref, ya_ref, yb_ref, g_ref, nw_ref, sh_ref, sc_ref, w_ref, ssd_ref, hy_ref, ml_ref, xo_ref = refs
    else:
        x_ref, nw_ref, sh_ref, sc_ref, w_ref, ssd_ref, hy_ref, ml_ref = refs
    x = _tok_load(x_ref, col_major)
    if fuse_moe:
        x = x + g_ref[0] * (_tok_load(ya_ref, col_major) + _tok_load(yb_ref, col_major))
        _tok_store(xo_ref, x, col_major)
    h = _modnorm(x, nw_ref[...], sh_ref[0], sc_ref[0])
    u = jnp.dot(h.astype(BF16), w_ref[...], preferred_element_type=F32)
    ssd_ref[0] = u[:, 0:SSD_COLS]
    hy_ref[0] = u[:, SSD_COLS:SSD_COLS + HY_COLS]
    ml_ref[0] = u[:, SSD_COLS + HY_COLS:]


def norm_proj(x, moe, nw, shift, scale, w_bf16, col_major=False, tm=512):
    b, L, d = x.shape
    n = w_bf16.shape[1]
    fuse_moe = moe is not None
    tm = (L // GRID_W) * SUBLANE if col_major else min(tm, L)
    tok = _tok_spec(L, d, tm, col_major)
    row = lambda bi, i: (bi, i, 0)
    const2 = lambda bi, i: (0, 0)
    args, in_specs = [_tok_view(x, col_major)], [tok]
    if fuse_moe:
        ya, yb, gate = moe
        args += [_tok_view(ya, col_major), _tok_view(yb, col_major), gate]
        in_specs += [tok, tok, pl.BlockSpec((1, 1, d), _mod_map(gate, b))]
    args += [nw.reshape(1, d), shift, scale, w_bf16]
    in_specs += [pl.BlockSpec((1, d), const2), pl.BlockSpec((1, 1, d), _mod_map(shift, b)),
                 pl.BlockSpec((1, 1, d), _mod_map(scale, b)), pl.BlockSpec((d, n), const2)]
    out_shape = [jax.ShapeDtypeStruct((b, L, SSD_COLS), F32), jax.ShapeDtypeStruct((b, L, HY_COLS), F32),
                 jax.ShapeDtypeStruct((b, L, ML_COLS), F32)]
    out_specs = [pl.BlockSpec((1, tm, SSD_COLS), row), pl.BlockSpec((1, tm, HY_COLS), row),
                 pl.BlockSpec((1, tm, ML_COLS), row)]
    if fuse_moe:
        out_shape.append(jax.ShapeDtypeStruct(args[0].shape, F32))
        out_specs.append(tok)
    outs = pl.pallas_call(
        functools.partial(_norm_proj_kernel, col_major=col_major, fuse_moe=fuse_moe),
        out_shape=tuple(out_shape),
        grid=(b, L // tm),
        in_specs=in_specs,
        out_specs=tuple(out_specs),
        compiler_params=_params("parallel", "arbitrary"),
        name="norm_in_proj",
    )(*args)
    return (*outs[:3], outs[3].reshape(b, L, d) if fuse_moe else x)


def _out_proj_kernel(ys_ref, yh_ref, ym_ref, x_ref, g_ref, w_ref, o_ref, *, col_major):
    y = jnp.concatenate([ys_ref[0], yh_ref[0], ym_ref[0]], axis=-1).astype(BF16)
    r = _tok_load(x_ref, col_major) + g_ref[0] * jnp.dot(y, w_ref[...], preferred_element_type=F32)
    _tok_store(o_ref, r, col_major)


def out_proj(y_ssd, y_hy, y_ml, x, gate, w_bf16, col_major=False, tm=512):
    b, L, d = x.shape
    tm = (L // GRID_W) * SUBLANE if col_major else min(tm, L)
    row = lambda bi, i: (bi, i, 0)
    tok = _tok_spec(L, d, tm, col_major)
    xv = _tok_view(x, col_major)
    return pl.pallas_call(
        functools.partial(_out_proj_kernel, col_major=col_major),
        out_shape=jax.ShapeDtypeStruct(xv.shape, F32),
        grid=(b, L // tm),
        in_specs=[pl.BlockSpec((1, tm, SSD_WIDTH), row), pl.BlockSpec((1, tm, HYENA_WIDTH), row),
                  pl.BlockSpec((1, tm, ML_WIDTH), row), tok,
                  pl.BlockSpec((1, 1, d), _mod_map(gate, b)),
                  pl.BlockSpec(w_bf16.shape, lambda bi, i: (0, 0))],
        out_specs=tok,
        compiler_params=_params("parallel", "arbitrary"),
        name="out_proj_residual",
    )(y_ssd, y_hy, y_ml, xv, gate, w_bf16).reshape(b, L, d)


def _conv3(xr, prev_row, next_row, cw, cb, q):
    rid = lax.broadcasted_iota(jnp.int32, (q, 1), 0)
    x_prev = jnp.where(rid == 0, prev_row, pltpu.roll(xr, 1, axis=0))
    x_next = jnp.where(rid == q - 1, next_row, pltpu.roll(xr, q - 1, axis=0))
    return x_prev * cw[0:1] + xr * cw[1:2] + x_next * cw[2:3] + cb


def _scan_mask(q, direction):
    li = lax.broadcasted_iota(jnp.int32, (q, q), 0)
    si = lax.broadcasted_iota(jnp.int32, (q, q), 1)
    return (si <= li) if direction == 0 else (si >= li)


def _running_max(x, direction, q):
    rid = lax.broadcasted_iota(jnp.int32, (q, 1), 0)
    s = 1
    while s < q:
        if direction == 0:
            x = jnp.where(rid >= s, jnp.maximum(x, pltpu.roll(x, s, axis=0)), x)
        else:
            x = jnp.where(rid < q - s, jnp.maximum(x, pltpu.roll(x, q - s, axis=0)), x)
        s *= 2
    return x


def _scan_specs(L, q, nc, cols, direction):
    hb = q // SUBLANE
    nrb = L // SUBLANE
    cidx = (lambda j: j) if direction == 0 else (lambda j: nc - 1 - j)
    specs = [pl.BlockSpec((1, q, cols), lambda bi, j: (bi, cidx(j), 0)),
             pl.BlockSpec((1, SUBLANE, cols), lambda bi, j: (bi, jnp.maximum(cidx(j) * hb - 1, 0), 0)),
             pl.BlockSpec((1, SUBLANE, cols), lambda bi, j: (bi, jnp.minimum((cidx(j) + 1) * hb, nrb - 1), 0))]
    return specs, cidx


def _ssd_kernel(*refs, direction, finalize, q, nc):
    if finalize:
        (u_ref, prev_ref, next_ref, yb_ref, init_ref, cw_ref, cb_ref, dtb_ref, a_ref, d_ref, nw_ref,
         y_ref, fin_ref, state_ref) = refs
    else:
        (u_ref, prev_ref, next_ref, init_ref, cw_ref, cb_ref, dtb_ref, a_ref,
         y_ref, fin_ref, state_ref) = refs
    j = pl.program_id(1)
    c = j if direction == 0 else nc - 1 - j

    @pl.when(j == 0)
    def _():
        state_ref[...] = init_ref[0]

    prev_row = jnp.where(c > 0, prev_ref[0, SUBLANE - 1:SUBLANE, SSD_XBC0:SSD_DT0], 0.0)
    next_row = jnp.where(c < nc - 1, next_ref[0, 0:1, SSD_XBC0:SSD_DT0], 0.0)
    xc = _silu(_conv3(u_ref[0, :, SSD_XBC0:SSD_DT0], prev_row, next_row, cw_ref[...], cb_ref[...], q))

    dt = _softplus(u_ref[0, :, SSD_DT0:SSD_COLS] + dtb_ref[...])
    mask = _scan_mask(q, direction)
    cum = jnp.dot(mask.astype(F32), dt * a_ref[...], preferred_element_type=F32, precision=HI)
    cum_t = cum.T
    end = q - 1 if direction == 0 else 0

    ys = []
    for g in range(SSD_GROUPS):
        b0 = SSD_WIDTH + g * SSD_STATE
        c0 = SSD_WIDTH + (SSD_GROUPS + g) * SSD_STATE
        bm_t = xc[:, b0:b0 + SSD_STATE].T
        cm = xc[:, c0:c0 + SSD_STATE].astype(BF16)
        scores = jnp.dot(cm, bm_t.astype(BF16), preferred_element_type=F32)
        for h in range(g * (SSD_HEADS // SSD_GROUPS), (g + 1) * (SSD_HEADS // SSD_GROUPS)):
            hl = direction * SSD_HEADS + h
            col = cum[:, hl:hl + 1]
            row = cum_t[hl:hl + 1, :]
            seg = jnp.exp(jnp.where(mask, col - row, -jnp.inf))
            xdt = (xc[:, h * HEAD_DIM:(h + 1) * HEAD_DIM] * dt[:, hl:hl + 1]).astype(BF16)
            y = jnp.dot((scores * seg).astype(BF16), xdt, preferred_element_type=F32)
            st = state_ref[h]
            y = y + jnp.dot(cm, st.astype(BF16), preferred_element_type=F32) * jnp.exp(col)
            tot = cum[end:end + 1, hl:hl + 1]
            upd = jnp.dot((bm_t * jnp.exp(tot - row)).astype(BF16), xdt, preferred_element_type=F32)
            state_ref[h] = st * jnp.exp(tot) + upd
            ys.append(y)
    y_all = jnp.concatenate(ys, axis=-1)
    if finalize:
        t = (y_all + yb_ref[0] + xc[:, 0:SSD_WIDTH] * d_ref[...]) * _silu(u_ref[0, :, 0:SSD_WIDTH])
        y_all = t * lax.rsqrt(jnp.mean(t * t, axis=-1, keepdims=True) + EPS) * nw_ref[...]
    y_ref[0] = y_all

    @pl.when(j == nc - 1)
    def _():
        fin_ref[0] = state_ref[...]


def ssd_pass(u, y_other, init, sp, direction, q):
    b, L, _ = u.shape
    q = min(q, L)
    nc = L // q
    finalize = y_other is not None
    in_specs, cidx = _scan_specs(L, q, nc, SSD_COLS, direction)
    const2 = lambda bi, j: (0, 0)
    st_spec = pl.BlockSpec((1, SSD_HEADS, HEAD_DIM, SSD_STATE), lambda bi, j: (bi, 0, 0, 0))
    y_spec = pl.BlockSpec((1, q, SSD_WIDTH), lambda bi, j: (bi, cidx(j), 0))
    args = [u, u, u]
    if finalize:
        in_specs.append(y_spec)
        args.append(y_other)
    consts = [sp['cw'], sp['cb'], sp['dtb'], sp['a']] + ([sp['d'], sp['nw']] if finalize else [])
    in_specs += [st_spec] + [pl.BlockSpec(t.shape, const2) for t in consts]
    args += [init] + consts
    return pl.pallas_call(
        functools.partial(_ssd_kernel, direction=direction, finalize=finalize, q=q, nc=nc),
        out_shape=(jax.ShapeDtypeStruct((b, L, SSD_WIDTH), F32),
                   jax.ShapeDtypeStruct((b, SSD_HEADS, HEAD_DIM, SSD_STATE), F32)),
        grid=(b, nc),
        in_specs=in_specs,
        out_specs=(y_spec, st_spec),
        scratch_shapes=[pltpu.VMEM((SSD_HEADS, HEAD_DIM, SSD_STATE), F32)],
        compiler_params=_params("parallel", "arbitrary"),
        name="ssd_scan_%s" % ("fwd" if direction == 0 else "bwd"),
    )(*args)


def ssd_prepare(p):
    pad = lambda v: jnp.pad(v.reshape(1, -1), ((0, 0), (0, LANE - v.size)))
    return dict(cw=p['ssd_conv_w'], cb=p['ssd_conv_b'].reshape(1, -1),
                dtb=pad(p['ssd_dt_bias']), a=pad(-jnp.exp(p['ssd_a_log'])),
                d=jnp.repeat(p['ssd_d'], HEAD_DIM).reshape(1, -1), nw=p['ssd_norm_w'].reshape(1, -1))


def ssd_mixer(u, sp, init_f, init_b, want_y, q=256):
    yb, fin_b = ssd_pass(u, None, init_b, sp, 1, q)
    y, fin_f = ssd_pass(u, yb if want_y else None, init_f, sp, 0, q)
    return y, fin_f, fin_b


def _ml_kernel(*refs, direction, finalize, q, nc):
    if finalize:
        (u_ref, prev_ref, next_ref, hb_ref, s_init_ref, m_init_ref, cw_ref, cb_ref, gb_ref, nw_ref, pool_ref,
         y_ref, s_fin_ref, m_fin_ref, s_ref, m_ref) = refs
    else:
        (u_ref, prev_ref, next_ref, s_init_ref, m_init_ref, cw_ref, cb_ref, gb_ref,
         y_ref, s_fin_ref, m_fin_ref, s_ref, m_ref) = refs
    j = pl.program_id(1)
    c = j if direction == 0 else nc - 1 - j

    @pl.when(j == 0)
    def _():
        s_ref[...] = s_init_ref[0]
        m_ref[...] = m_init_ref[0]

    prev_row = jnp.where(c > 0, prev_ref[0, SUBLANE - 1:SUBLANE, 0:ML_V0], 0.0)
    next_row = jnp.where(c < nc - 1, next_ref[0, 0:1, 0:ML_V0], 0.0)
    qk = _silu(_conv3(u_ref[0, :, 0:ML_V0], prev_row, next_row, cw_ref[...], cb_ref[...], q))
    v = u_ref[0, :, ML_V0:ML_O0]

    gb = u_ref[0, :, ML_G0:ML_COLS] + gb_ref[...]
    mask = _scan_mask(q, direction)
    cum = jnp.dot(mask.astype(F32), _log_sigmoid(gb), preferred_element_type=F32, precision=HI)
    ig = pltpu.roll(gb, ML_HEADS, axis=1)
    end = q - 1 if direction == 0 else 0
    m_prev = m_ref[0:1, :]
    tot = cum[end:end + 1, :]
    w_end = tot - cum + ig
    m_loc = jnp.max(w_end, axis=0, keepdims=True)
    e_end = jnp.exp(w_end - m_loc)
    m_new = jnp.maximum(tot + m_prev, m_loc)
    a_prev = jnp.exp(tot + m_prev - m_new)
    a_loc = jnp.exp(m_loc - m_new)
    inter = cum + m_prev
    rel = ig - cum
    m_t = jnp.maximum(inter, cum + _running_max(rel, direction, q))
    col_a = cum - m_t
    a_inter = jnp.exp(inter - m_t)
    floor = jnp.exp(-m_t)
    rel_t = rel.T
    e_end_t = e_end.T
    k_t = (qk[:, ML_WIDTH:2 * ML_WIDTH] * (HEAD_DIM ** -0.5)).T
    one_col = (lax.broadcasted_iota(jnp.int32, (q, HEAD_DIM), 1) == 0).astype(F32)

    ys = []
    for h in range(ML_HEADS):
        fl = direction * 2 * ML_HEADS + ML_HEADS + h
        qh = qk[:, h * HEAD_DIM:(h + 1) * HEAD_DIM].astype(BF16)
        kh_t = k_t[h * HEAD_DIM:(h + 1) * HEAD_DIM, :]
        v_ext = jnp.concatenate([v[:, h * HEAD_DIM:(h + 1) * HEAD_DIM], one_col], axis=-1).astype(BF16)
        pw = jnp.exp(jnp.where(mask, col_a[:, fl:fl + 1] + rel_t[fl:fl + 1, :], -jnp.inf))
        scores = jnp.dot(qh, kh_t.astype(BF16), preferred_element_type=F32)
        nd = jnp.dot((scores * pw).astype(BF16), v_ext, preferred_element_type=F32)
        st = s_ref[h]
        nd = nd + a_inter[:, fl:fl + 1] * jnp.dot(qh, st.astype(BF16), preferred_element_type=F32)
        den = nd[:, HEAD_DIM:HEAD_DIM + 1]
        ys.append(nd[:, 0:HEAD_DIM] / jnp.maximum(jnp.abs(den), floor[:, fl:fl + 1]))
        upd = jnp.dot((kh_t * e_end_t[fl:fl + 1, :]).astype(BF16), v_ext, preferred_element_type=F32)
        s_ref[h] = a_prev[:, fl:fl + 1] * st + a_loc[:, fl:fl + 1] * upd
    m_ref[...] = jnp.broadcast_to(m_new, m_ref.shape)
    y_all = jnp.concatenate(ys, axis=-1)
    if finalize:
        hs = y_all + hb_ref[0]
        hc = hs - jnp.dot(hs.astype(BF16), pool_ref[...], preferred_element_type=F32)
        var = jnp.dot((hc * hc).astype(BF16), pool_ref[...], preferred_element_type=F32)
        y_all = hc * lax.rsqrt(var + EPS) * nw_ref[...] * jax.nn.sigmoid(u_ref[0, :, ML_O0:ML_G0])
    y_ref[0] = y_all

    @pl.when(j == nc - 1)
    def _():
        s_fin_ref[0] = s_ref[...]
        m_fin_ref[0] = m_ref[...]


def ml_pass(u, h_other, init, mp, direction, q):
    b, L, _ = u.shape
    q = min(q, L)
    nc = L // q
    finalize = h_other is not None
    in_specs, cidx = _scan_specs(L, q, nc, ML_COLS, direction)
    const2 = lambda bi, j: (0, 0)
    s_spec = pl.BlockSpec((1, ML_HEADS, HEAD_DIM, LANE), lambda bi, j: (bi, 0, 0, 0))
    m_spec = pl.BlockSpec((1, SUBLANE, LANE), lambda bi, j: (bi, 0, 0))
    y_spec = pl.BlockSpec((1, q, ML_WIDTH), lambda bi, j: (bi, cidx(j), 0))
    args = [u, u, u]
    if finalize:
        in_specs.append(y_spec)
        args.append(h_other)
    consts = [mp['cw'], mp['cb'], mp['gb']] + ([mp['nw'], mp['pool']] if finalize else [])
    in_specs += [s_spec, m_spec] + [pl.BlockSpec(t.shape, const2) for t in consts]
    args += [init[0], init[1]] + consts
    y, s_fin, m_fin = pl.pallas_call(
        functools.partial(_ml_kernel, direction=direction, finalize=finalize, q=q, nc=nc),
        out_shape=(jax.ShapeDtypeStruct((b, L, ML_WIDTH), F32),
                   jax.ShapeDtypeStruct((b, ML_HEADS, HEAD_DIM, LANE), F32),
                   jax.ShapeDtypeStruct((b, SUBLANE, LANE), F32)),
        grid=(b, nc),
        in_specs=in_specs,
        out_specs=(y_spec, s_spec, m_spec),
        scratch_shapes=[pltpu.VMEM((ML_HEADS, HEAD_DIM, LANE), F32), pltpu.VMEM((SUBLANE, LANE), F32)],
        compiler_params=_params("parallel", "arbitrary"),
        name="mlstm_scan_%s" % ("fwd" if direction == 0 else "bwd"),
    )(*args)
    return y, (s_fin, m_fin)


def ml_prepare(p):
    gb = p['ml_gate_b'].reshape(1, -1)
    head = np.arange(ML_WIDTH) // HEAD_DIM
    pool = jnp.asarray((head[:, None] == head[None, :]) / HEAD_DIM, BF16)
    return dict(cw=p['ml_conv_w'], cb=p['ml_conv_b'].reshape(1, -1),
                gb=jnp.pad(gb, ((0, 0), (0, LANE - gb.shape[1]))), nw=p['ml_norm_w'].reshape(1, -1), pool=pool)


def ml_mixer(u, mp, init_f, init_b, want_y, q=256):
    hb, fin_b = ml_pass(u, None, init_b, mp, 1, q)
    y, fin_f = ml_pass(u, hb if want_y else None, init_f, mp, 0, q)
    return y, fin_f, fin_b


FFT_L = 4096
FFT_N = 2 * FFT_L
FFT_N2 = 128
FFT_N1 = FFT_N // FFT_N2
FFT_N1H = FFT_L // FFT_N2
FFT_K1 = FFT_N1 // 2 + 1
FFT_R = 80
FFT_UNROLL = 8
FFT_PITCH = FFT_N2 + SUBLANE


def _fft_tables():
    n2 = np.arange(FFT_N2)[:, None, None]
    k1 = np.arange(FFT_K1)[None, :, None]
    n1 = np.arange(FFT_N1H)[None, None, :]
    th = 2 * np.pi * (((FFT_N2 * n1 + n2) * k1) % FFT_N) / FFT_N
    f1 = np.zeros((FFT_N2, FFT_R, FFT_N1H))
    f1[:, 0:2 * FFT_K1:2, :] = np.cos(th)
    f1[:, 1:2 * FFT_K1:2, :] = -np.sin(th)
    wgt = np.where((np.arange(FFT_K1) == 0) | (np.arange(FFT_K1) == FFT_N1 // 2), 1.0, 2.0)[None, :, None] / FFT_N
    g1 = np.zeros((FFT_N2, FFT_N1H, FFT_R))
    g1[:, :, 0:2 * FFT_K1:2] = np.transpose(wgt * np.cos(th), (0, 2, 1))
    g1[:, :, 1:2 * FFT_K1:2] = np.transpose(-wgt * np.sin(th), (0, 2, 1))
    ph = 2 * np.pi * ((np.arange(FFT_N2)[:, None] * np.arange(FFT_N2)[None, :]) % FFT_N2) / FFT_N2
    c, s = np.cos(ph), np.sin(ph)
    f2 = np.block([[c, s], [-s, c]])
    as_bf = lambda a: jnp.asarray(a, F32).astype(BF16)
    return as_bf(f1), as_bf(f2), as_bf(f2.T), as_bf(g1)


def _sld(ref, n2, count):
    rows = pl.ds(n2, count, stride=FFT_PITCH)
    return jnp.concatenate([ref[0, rows, :], ref[1, rows, :]], axis=-1)


def _sst(ref, n2, count, val):
    rows = pl.ds(n2, count, stride=FFT_PITCH)
    ref[0, rows, :] = val[:, 0:LANE]
    ref[1, rows, :] = val[:, LANE:2 * LANE]


def _blk_ld(ref, blk, nblk):
    parts = []
    for k in range(nblk):
        rows = pl.ds(pl.multiple_of((blk + k) * FFT_PITCH, SUBLANE), FFT_N2)
        parts.append(jnp.concatenate([ref[0, rows, :], ref[1, rows, :]], axis=-1))
    return parts[0] if nblk == 1 else jnp.concatenate(parts, axis=0)


def _blk_st(ref, blk, nblk, val):
    for k in range(nblk):
        rows = pl.ds(pl.multiple_of((blk + k) * FFT_PITCH, SUBLANE), FFT_N2)
        ref[0, rows, :] = val[k * FFT_N2:(k + 1) * FFT_N2, 0:LANE]
        ref[1, rows, :] = val[k * FFT_N2:(k + 1) * FFT_N2, LANE:2 * LANE]


def _fft_stage1(z_ref, a_ref, f1_ref, n_in=FFT_N1H):
    def body(n2, carry):
        xs = _sld(z_ref, n2, n_in).astype(BF16)
        _sst(a_ref, n2, FFT_R, jnp.dot(f1_ref[n2], xs, preferred_element_type=F32))
        return carry
    lax.fori_loop(0, FFT_N2, body, 0, unroll=FFT_UNROLL)


def _spectrum_kernel(x_ref, inorm_ref, f1_ref, f2_ref, o_ref, z_ref, a_ref):
    for half in range(2):
        for n1 in range(FFT_N1H):
            _blk_st(z_ref, half * FFT_N1H + n1, 1, x_ref[half, n1 * FFT_N2:(n1 + 1) * FFT_N2, :])
    _fft_stage1(z_ref, a_ref, f1_ref, 2 * FFT_N1H)
    inorm = inorm_ref[0]

    def body(k1, carry):
        slab = _blk_ld(a_ref, 2 * k1, 2).astype(BF16)
        o_ref[0, k1] = (jnp.dot(f2_ref[...], slab, preferred_element_type=F32) * inorm).astype(BF16)
        return carry
    lax.fori_loop(0, FFT_K1, body, 0, unroll=3)


def hyena_filter_spectra(sig, inv_norm):
    _, L, c = sig.shape
    f1, f2, _, _ = _fft_tables()
    sign = np.where(np.arange(FFT_R) // 2 % 2 == 0, 1.0, -1.0)[None, :, None].astype(np.float32)
    f1ab = jnp.concatenate([f1, f1 * jnp.asarray(sign, BF16)], axis=2)
    one = pl.Buffered(1)
    return pl.pallas_call(
        _spectrum_kernel,
        out_shape=jax.ShapeDtypeStruct((HYENA_ORDER, FFT_K1, 2 * FFT_N2, c), BF16),
        grid=(HYENA_ORDER,),
        in_specs=[pl.BlockSpec((2, L, c), lambda i: (i, 0, 0)),
                  pl.BlockSpec((1, 1, c), lambda i: (i, 0, 0)),
                  pl.BlockSpec(f1ab.shape, lambda i: (0, 0, 0), pipeline_mode=one),
                  pl.BlockSpec(f2.shape, lambda i: (0, 0), pipeline_mode=one)],
        out_specs=pl.BlockSpec((1, FFT_K1, 2 * FFT_N2, c), lambda i: (i, 0, 0, 0)),
        scratch_shapes=[pltpu.VMEM((2, 2 * FFT_N1H * FFT_PITCH, LANE), F32),
                        pltpu.VMEM((2, FFT_R * FFT_PITCH, LANE), F32)],
        compiler_params=pltpu.CompilerParams(dimension_semantics=("arbitrary",), vmem_limit_bytes=VMEM_LIMIT_HY),
        name="hyena_filter_spectrum",
    )(sig, inv_norm, f1ab, f2)


def _conv3_rows(src, dst, cw, cb, L):
    rows = 2 * FFT_N2
    nchunk = L // rows

    def body(i, carry):
        r0 = pl.multiple_of(i * rows, rows)
        prev_row = jnp.where(i > 0, src[pl.ds(jnp.maximum(r0 - 1, 0), 1), :], 0.0)
        next_row = jnp.where(i < nchunk - 1, src[pl.ds(jnp.minimum(r0 + rows, L - 1), 1), :], 0.0)
        _blk_st(dst, 2 * i, 2, _conv3(src[pl.ds(r0, rows), :], prev_row, next_row, cw, cb, rows))
        return carry
    lax.fori_loop(0, nchunk, body, 0, unroll=2)


def _hyena_kernel(v_ref, g_ref, h_ref, cwv_ref, cbv_ref, cwg_ref, cbg_ref, skip_ref,
                  f1_ref, f2_ref, f2t_ref, g1_ref, o_ref, z_ref, gc_ref, a_ref):
    order = pl.program_id(1)

    @pl.when(order == 0)
    def _():
        _conv3_rows(v_ref.at[0], z_ref, cwv_ref[...], cbv_ref[...], FFT_L)

    _conv3_rows(g_ref.at[0], gc_ref, cwg_ref[0], cbg_ref[0], FFT_L)
    _fft_stage1(z_ref, a_ref, f1_ref)

    def mid(k1, carry):
        x = jnp.dot(f2_ref[...], _blk_ld(a_ref, 2 * k1, 2).astype(BF16), preferred_element_type=F32)
        h = h_ref[0, k1].astype(F32)
        xr, xi, hr, hi = x[:FFT_N2], x[FFT_N2:], h[:FFT_N2], h[FFT_N2:]
        y = jnp.concatenate([xr * hr - xi * hi, xr * hi + xi * hr], axis=0).astype(BF16)
        _blk_st(a_ref, 2 * k1, 2, jnp.dot(f2t_ref[...], y, preferred_element_type=F32))
        return carry
    lax.fori_loop(0, FFT_K1, mid, 0, unroll=3)

    skip = skip_ref[0]

    def last(n2, carry):
        bs = _sld(a_ref, n2, FFT_R).astype(BF16)
        y = jnp.dot(g1_ref[n2], bs, preferred_element_type=F32)
        _sst(z_ref, n2, FFT_N1H, _sld(gc_ref, n2, FFT_N1H) * (y + skip * _sld(z_ref, n2, FFT_N1H)))
        return carry
    lax.fori_loop(0, FFT_N2, last, 0, unroll=FFT_UNROLL)

    @pl.when(order == 1)
    def _():
        for n1 in range(FFT_N1H):
            o_ref[0, n1 * FFT_N2:(n1 + 1) * FFT_N2, :] = _blk_ld(z_ref, n1, 1)


def hyena_long(u, h_spec, cw, cb, skip):
    b, L, _ = u.shape
    c = HYENA_WIDTH
    f1, f2, f2t, g1 = _fft_tables()
    one = pl.Buffered(1)
    cw3 = cw.reshape(3, 3, c).transpose(1, 0, 2)
    cb3 = cb.reshape(3, 1, c)
    return pl.pallas_call(
        _hyena_kernel,
        out_shape=jax.ShapeDtypeStruct((b, L, c), F32),
        grid=(b, 2),
        in_specs=[pl.BlockSpec((1, L, c), lambda bi, o: (bi, 0, 0), pipeline_mode=one),
                  pl.BlockSpec((1, L, c), lambda bi, o: (bi, 0, 1 + o)),
                  pl.BlockSpec((1, FFT_K1, 2 * FFT_N2, c), lambda bi, o: (o, 0, 0, 0)),
                  pl.BlockSpec((3, c), lambda bi, o: (0, 0)),
                  pl.BlockSpec((1, c), lambda bi, o: (0, 0)),
                  pl.BlockSpec((1, 3, c), lambda bi, o: (1 + o, 0, 0)),
                  pl.BlockSpec((1, 1, c), lambda bi, o: (1 + o, 0, 0)),
                  pl.BlockSpec((1, 1, c), lambda bi, o: (o, 0, 0)),
                  pl.BlockSpec(f1.shape, lambda bi, o: (0, 0, 0), pipeline_mode=one),
                  pl.BlockSpec(f2.shape, lambda bi, o: (0, 0), pipeline_mode=one),
                  pl.BlockSpec(f2t.shape, lambda bi, o: (0, 0), pipeline_mode=one),
                  pl.BlockSpec(g1.shape, lambda bi, o: (0, 0, 0), pipeline_mode=one)],
        out_specs=pl.BlockSpec((1, L, c), lambda bi, o: (bi, 0, 0)),
        scratch_shapes=[pltpu.VMEM((2, FFT_N1H * FFT_PITCH, LANE), F32),
                        pltpu.VMEM((2, FFT_N1H * FFT_PITCH, LANE), F32),
                        pltpu.VMEM((2, FFT_R * FFT_PITCH, LANE), F32)],
        compiler_params=pltpu.CompilerParams(dimension_semantics=("parallel", "arbitrary"),
                                             vmem_limit_bytes=VMEM_LIMIT_HY),
        name="hyena_long_conv",
    )(u, u, h_spec, cw3[0], cb3[0], cw3, cb3, skip.reshape(2, 1, c), f1, f2, f2t, g1)


HY_FILT = 2 * HYENA_ORDER * HYENA_WIDTH
HY_HALF = HYENA_ORDER * HYENA_WIDTH


def _filter_kernel(wt_ref, wc_ref, ws_ref, b1_ref, w2_ref, b2_ref, w3_ref, freq_ref, decay_ref,
                   sig_ref, asum_ref, *, L, rows):
    i = pl.program_id(0)
    n = (i * rows + lax.broadcasted_iota(jnp.int32, (rows, 1), 0)).astype(F32)
    band = lax.broadcasted_iota(jnp.int32, (1, HYENA_BANDS), 1).astype(F32)
    bands = 1e-4 + band * ((HYENA_BANDS - 1 - 1e-4) / (HYENA_BANDS - 1))
    freq = freq_ref[...]

    @pl.when(i == 0)
    def _():
        asum_ref[...] = jnp.zeros_like(asum_ref)

    for side, t in enumerate((n, L - n)):
        t_unit = t / float(max(L - 1, 1))
        ang = (2 * math.pi / L) * t * bands
        pre = (t_unit * wt_ref[...] + jnp.dot(jnp.cos(ang), wc_ref[...], preferred_element_type=F32, precision=HI)
               - jnp.dot(jnp.sin(ang), ws_ref[...], preferred_element_type=F32, precision=HI) + b1_ref[...])
        hdn = jnp.sin(freq * pre)
        hdn = jnp.sin(freq * (jnp.dot(hdn, w2_ref[...], preferred_element_type=F32, precision=HI) + b2_ref[...]))
        cols = slice(side * HY_HALF, (side + 1) * HY_HALF)
        val = jnp.dot(hdn, w3_ref[:, cols], preferred_element_type=F32, precision=HI)
        val = val * jnp.exp(-t_unit * jnp.abs(decay_ref[:, cols]))
        if side == 1:
            val = jnp.where(n > 0, val, 0.0)
        for o in range(HYENA_ORDER):
            sig_ref[2 * o + side] = val[:, o * HYENA_WIDTH:(o + 1) * HYENA_WIDTH]
        asum_ref[side:side + 1, :] += jnp.sum(jnp.abs(val), axis=0, keepdims=True)


def hyena_filter_signals(L, p):
    rows = min(L, 512)
    fh = p['hy_pos_w1'].shape[1]
    w1 = p['hy_pos_w1']
    consts = [w1[0:1], w1[1:1 + HYENA_BANDS], w1[1 + HYENA_BANDS:], p['hy_pos_b1'].reshape(1, fh), p['hy_pos_w2'],
              p['hy_pos_b2'].reshape(1, fh), p['hy_pos_w3'], p['hy_freq'].reshape(1, fh),
              p['hy_decay'].reshape(1, HY_FILT)]
    sig, asum = pl.pallas_call(
        functools.partial(_filter_kernel, L=L, rows=rows),
        out_shape=(jax.ShapeDtypeStruct((2 * HYENA_ORDER, L, HYENA_WIDTH), F32),
                   jax.ShapeDtypeStruct((SUBLANE, HY_HALF), F32)),
        grid=(L // rows,),
        in_specs=[pl.BlockSpec(c.shape, lambda i: (0, 0)) for c in consts],
        out_specs=(pl.BlockSpec((2 * HYENA_ORDER, rows, HYENA_WIDTH), lambda i: (0, i, 0)),
                   pl.BlockSpec((SUBLANE, HY_HALF), lambda i: (0, 0))),
        compiler_params=_params("arbitrary"),
        name="hyena_filters",
    )(*consts)
    inv_norm = 1.0 / (asum[0] + asum[1]).reshape(HYENA_ORDER, 1, HYENA_WIDTH)
    return sig, inv_norm


def _rdft_tables(L):
    n_bins = L + 1
    half = -(-n_bins // 16) * 16
    k = np.arange(n_bins)[:, None]
    n = np.arange(L)[None, :]
    th = 2 * np.pi * ((k * n) % (2 * L)) / (2 * L)
    f = np.zeros((2 * half, L))
    f[:n_bins] = np.cos(th)
    f[half:half + n_bins] = -np.sin(th)
    sign = np.where(np.arange(n_bins) % 2 == 0, 1.0, -1.0)[:, None]
    fs = np.zeros_like(f)
    fs[:n_bins] = f[:n_bins] * sign
    fs[half:half + n_bins] = f[half:half + n_bins] * sign
    wgt = np.where((np.arange(n_bins) == 0) | (np.arange(n_bins) == L), 1.0, 2.0)[None, :] / (2 * L)
    g = np.zeros((L, 2 * half))
    g[:, :n_bins] = wgt * np.cos(th).T
    g[:, half:half + n_bins] = -wgt * np.sin(th).T
    as_bf = lambda a: jnp.asarray(a, F32).astype(BF16)
    return as_bf(f), as_bf(fs), as_bf(g), half


def _hyena_ctx_kernel(u_ref, sig_ref, inorm_ref, cw_ref, cb_ref, skip_ref, f_ref, fs_ref, g_ref, o_ref, *, L, half):
    zero_row = jnp.zeros((1, u_ref.shape[2]), F32)
    uc = _conv3(u_ref[0], zero_row, zero_row, cw_ref[...], cb_ref[...], L)
    z = uc[:, 0:HYENA_WIDTH]
    for o in range(HYENA_ORDER):
        h = (jnp.dot(f_ref[...], sig_ref[2 * o].astype(BF16), preferred_element_type=F32)
             + jnp.dot(fs_ref[...], sig_ref[2 * o + 1].astype(BF16), preferred_element_type=F32)) * inorm_ref[o]
        x = jnp.dot(f_ref[...], z.astype(BF16), preferred_element_type=F32)
        xr, xi, hr, hi = x[:half], x[half:], h[:half], h[half:]
        y = jnp.concatenate([xr * hr - xi * hi, xr * hi + xi * hr], axis=0).astype(BF16)
        conv = jnp.dot(g_ref[...], y, preferred_element_type=F32)
        z = uc[:, (o + 1) * HYENA_WIDTH:(o + 2) * HYENA_WIDTH] * (conv + skip_ref[o] * z)
    o_ref[0] = z


def hyena_short(u, sig, inv_norm, cw, cb, skip):
    b, L, cols = u.shape
    f, fs, g, half = _rdft_tables(L)
    const2 = lambda bi: (0, 0)
    const3 = lambda bi: (0, 0, 0)
    return pl.pallas_call(
        functools.partial(_hyena_ctx_kernel, L=L, half=half),
        out_shape=jax.ShapeDtypeStruct((b, L, HYENA_WIDTH), F32),
        grid=(b,),
        in_specs=[pl.BlockSpec((1, L, cols), lambda bi: (bi, 0, 0)),
                  pl.BlockSpec(sig.shape, const3), pl.BlockSpec(inv_norm.shape, const3),
                  pl.BlockSpec(cw.shape, const2), pl.BlockSpec((1, cols), const2),
                  pl.BlockSpec((HYENA_ORDER, 1, HYENA_WIDTH), const3),
                  pl.BlockSpec(f.shape, const2), pl.BlockSpec(fs.shape, const2), pl.BlockSpec(g.shape, const2)],
        out_specs=pl.BlockSpec((1, L, HYENA_WIDTH), lambda bi: (bi, 0, 0)),
        compiler_params=_params("parallel"),
        name="hyena_context",
    )(u, sig, inv_norm, cw, cb.reshape(1, cols), skip.reshape(HYENA_ORDER, 1, HYENA_WIDTH), f, fs, g)


def _split_bf16(a):
    hi = a.astype(BF16)
    return hi, (a - hi.astype(F32)).astype(BF16)


def _norm_router_kernel(*refs, n_lat, two_src):
    if two_src:
        xl_ref, xc_ref, nw_ref, sh_ref, sc_ref, whi_ref, wlo_ref, br_ref, tri_ref, h_ref, lg_ref, cnt_ref = refs
        x = jnp.where(pl.program_id(0) < n_lat, xl_ref[...], xc_ref[...])
    else:
        xl_ref, nw_ref, sh_ref, sc_ref, whi_ref, wlo_ref, br_ref, tri_ref, h_ref, lg_ref, cnt_ref = refs
        x = xl_ref[...]
    h = _modnorm(x, nw_ref[...], sh_ref[0], sc_ref[0])
    h_ref[...] = h
    h_hi, h_lo = _split_bf16(h)
    lg = (jnp.dot(h_hi, whi_ref[...], preferred_element_type=F32)
          + jnp.dot(h_hi, wlo_ref[...], preferred_element_type=F32)
          + jnp.dot(h_lo, whi_ref[...], preferred_element_type=F32)) + br_ref[...]
    lane = lax.broadcasted_iota(jnp.int32, lg.shape, 1)
    first = lambda hit: jnp.min(jnp.where(hit, lane, ROUTER_COLS), axis=-1, keepdims=True)
    gl = jnp.where(lane < N_GROUPS, lg, -jnp.inf)
    gmax = jnp.max(gl, axis=-1, keepdims=True)
    grp = first(gl == gmax)
    grp_p = 1.0 / jnp.sum(jnp.exp(gl - gmax), axis=-1, keepdims=True)
    lo = N_GROUPS + grp * EXPERTS_PER_GROUP
    el = jnp.where((lane >= lo) & (lane < lo + EXPERTS_PER_GROUP), lg, -jnp.inf)
    e1 = jnp.max(el, axis=-1, keepdims=True)
    i1 = first(el == e1)
    el2 = jnp.where(lane == i1, -jnp.inf, el)
    e2 = jnp.max(el2, axis=-1, keepdims=True)
    i2 = first(el2 == e2)
    r = jnp.exp(e2 - e1)
    w1 = grp_p / (1.0 + r)
    w2 = w1 * r
    @pl.when(pl.program_id(0) == 0)
    def _():
        cnt_ref[...] = jnp.zeros_like(cnt_ref)

    hit1, hit2 = lane == i1, lane == i2
    picks = jnp.where(hit1 | hit2, 1.0, 0.0)
    before = cnt_ref[0:1, :] + jnp.dot(tri_ref[...], picks.astype(BF16), preferred_element_type=F32)
    rank1 = jnp.sum(jnp.where(hit1, before, 0.0), axis=-1, keepdims=True)
    rank2 = jnp.sum(jnp.where(hit2, before, 0.0), axis=-1, keepdims=True)
    cnt_ref[...] = jnp.broadcast_to(cnt_ref[0:1, :] + jnp.sum(picks, axis=0, keepdims=True), cnt_ref.shape)
    vals = [(i1 - N_GROUPS).astype(F32), (i2 - N_GROUPS).astype(F32), w1, w2, rank1, rank2]
    out = jnp.zeros(lg.shape, F32)
    for k, val in enumerate(vals):
        out = jnp.where(lane == k, val, out)
    lg_ref[...] = out


def norm_router(xl, xc, nw, mod_l, mod_c, w_router, b_router, tm=512):
    b, L, d = xl.shape
    two_src = xc is not None
    n_lat = b * L // tm
    per_batch = L // tm
    n_ctx = (xc.shape[0] * xc.shape[1]) // tm if two_src else 0
    w_hi, w_lo = _split_bf16(w_router)
    const = lambda i: (0, 0)
    if two_src:
        shift = jnp.concatenate([mod_l[0], mod_c[0]], axis=0)
        scale = jnp.concatenate([mod_l[1], mod_c[1]], axis=0)
        mod_map = lambda i: (jnp.where(i < n_lat, i // per_batch, b), 0, 0)
        srcs = [xl.reshape(b * L, d), xc.reshape(-1, d)]
        src_specs = [pl.BlockSpec((tm, d), lambda i: (jnp.minimum(i, n_lat - 1), 0)),
                     pl.BlockSpec((tm, d), lambda i: (jnp.maximum(i - n_lat, 0), 0))]
    else:
        shift, scale = mod_l
        mod_map = lambda i: (i // per_batch, 0, 0)
        srcs = [xl.reshape(b * L, d)]
        src_specs = [pl.BlockSpec((tm, d), lambda i: (i, 0))]
    n_tok = (n_lat + n_ctx) * tm
    tri = jnp.asarray(np.tril(np.ones((tm, tm)), -1), BF16)
    return pl.pallas_call(
        functools.partial(_norm_router_kernel, n_lat=n_lat, two_src=two_src),
        out_shape=(jax.ShapeDtypeStruct((n_tok, d), F32), jax.ShapeDtypeStruct((n_tok, ROUTER_COLS), F32),
                   jax.ShapeDtypeStruct((SUBLANE, ROUTER_COLS), F32)),
        grid=(n_lat + n_ctx,),
        in_specs=src_specs + [pl.BlockSpec((1, d), const), pl.BlockSpec((1, 1, d), mod_map),
                              pl.BlockSpec((1, 1, d), mod_map), pl.BlockSpec((d, ROUTER_COLS), const),
                              pl.BlockSpec((d, ROUTER_COLS), const), pl.BlockSpec((1, ROUTER_COLS), const),
                              pl.BlockSpec((tm, tm), const)],
        out_specs=(pl.BlockSpec((tm, d), lambda i: (i, 0)), pl.BlockSpec((tm, ROUTER_COLS), lambda i: (i, 0)),
                   pl.BlockSpec((SUBLANE, ROUTER_COLS), const)),
        compiler_params=_params("arbitrary"),
        name="moe_norm_router",
    )(*srcs, nw.reshape(1, d), shift, scale, w_hi, w_lo, b_router, tri)


def _expert_ffn_kernel(te_ref, tv_ref, x_ref, rw_ref, wg_ref, wu_ref, wd_ref, o_ref, wg_s, wu_s, wd_s):
    i = pl.program_id(0)

    @pl.when((i == 0) | (te_ref[i] != te_ref[jnp.maximum(i - 1, 0)]))
    def _():
        wg_s[...] = wg_ref[0].astype(BF16)
        wu_s[...] = wu_ref[0].astype(BF16)
        wd_s[...] = wd_ref[0].astype(BF16)

    @pl.when(tv_ref[i] > 0)
    def _():
        x = x_ref[...].astype(BF16)
        g = jnp.dot(x, wg_s[...], preferred_element_type=F32)
        u = jnp.dot(x, wu_s[...], preferred_element_type=F32)
        hid = _silu(g) * u * rw_ref[:, 0:1]
        o_ref[...] = jnp.dot(hid.astype(BF16), wd_s[...], preferred_element_type=F32)

    @pl.when(tv_ref[i] == 0)
    def _():
        o_ref[...] = jnp.zeros_like(o_ref)


def expert_ffn(x_sorted, row_w, tile_expert, tile_valid, w_gate, w_up, w_down, tm):
    r, d = x_sorted.shape
    f = w_gate.shape[-1]
    grid_spec = pltpu.PrefetchScalarGridSpec(
        num_scalar_prefetch=2,
        grid=(r // tm,),
        in_specs=[pl.BlockSpec((tm, d), lambda i, te, tv: (i, 0)),
                  pl.BlockSpec((tm, LANE), lambda i, te, tv: (i, 0)),
                  pl.BlockSpec((1, d, f), lambda i, te, tv: (te[i], 0, 0)),
                  pl.BlockSpec((1, d, f), lambda i, te, tv: (te[i], 0, 0)),
                  pl.BlockSpec((1, f, d), lambda i, te, tv: (te[i], 0, 0))],
        out_specs=pl.BlockSpec((tm, d), lambda i, te, tv: (i, 0)),
        scratch_shapes=[pltpu.VMEM((d, f), BF16), pltpu.VMEM((d, f), BF16), pltpu.VMEM((f, d), BF16)],
    )
    return pl.pallas_call(
        _expert_ffn_kernel,
        out_shape=jax.ShapeDtypeStruct((r, d), F32),
        grid_spec=grid_spec,
        compiler_params=_params("arbitrary"),
        name="moe_expert_ffn",
    )(tile_expert, tile_valid, x_sorted, row_w, w_gate, w_up, w_down)


def moe_apply(h_tokens, routed, counts_row, w_gate, w_up, w_down, layer, n_lat, tm=256):
    t, d = h_tokens.shape
    e_idx, e_w, rank = routed[:, 0:2].astype(jnp.int32), routed[:, 2:4], routed[:, 4:6].astype(jnp.int32)
    counts = counts_row[0, N_GROUPS:N_GROUPS + N_EXPERTS].astype(jnp.int32)
    n_pairs = 2 * t
    padded = (counts + tm - 1) // tm * tm
    pad_end = jnp.cumsum(padded)
    pad_start = pad_end - padded
    n_rows = n_pairs + N_EXPERTS * tm
    tile_start = jnp.arange(n_rows // tm, dtype=jnp.int32) * tm
    tile_expert = jnp.minimum(jnp.sum((pad_end[None, :] <= tile_start[:, None]).astype(jnp.int32), axis=1),
                              N_EXPERTS - 1)
    tile_valid = (tile_start < pad_end[-1]).astype(jnp.int32)
    onehot = (e_idx[:, :, None] == jnp.arange(N_EXPERTS, dtype=jnp.int32)).astype(jnp.int32)
    pos = jnp.sum(onehot * pad_start, axis=-1) + rank
    j = jnp.arange(tm, dtype=jnp.int32)[None, :]
    fill_key = jnp.where(j < (padded - counts)[:, None], (pad_start + counts)[:, None] + j, n_rows)
    keys = jnp.concatenate([pos.reshape(-1), fill_key.reshape(-1)])
    toks = jnp.concatenate([jnp.arange(n_pairs, dtype=jnp.int32) // 2, jnp.arange(N_EXPERTS * tm, dtype=jnp.int32) % t])
    wts = jnp.concatenate([e_w.reshape(-1), jnp.zeros((N_EXPERTS * tm,), F32)])
    _, row_token, row_w = lax.sort((keys, toks, wts), num_keys=1)
    x_sorted = h_tokens[row_token]
    wg = w_gate.reshape(-1, d, EXPERT_HIDDEN)
    wu = w_up.reshape(-1, d, EXPERT_HIDDEN)
    wd = w_down.reshape(-1, EXPERT_HIDDEN, d)
    row_w = jnp.broadcast_to(row_w[:, None], (n_rows, LANE))
    y_sorted = expert_ffn(x_sorted, row_w, tile_expert + layer * N_EXPERTS, tile_valid, wg, wu, wd, tm)
    lat = (y_sorted[pos[:n_lat, 0]], y_sorted[pos[:n_lat, 1]])
    rest = (y_sorted[pos[n_lat:, 0]], y_sorted[pos[n_lat:, 1]]) if t > n_lat else None
    return lat, rest


def _final_kernel(x_ref, ya_ref, yb_ref, g_ref, w_ref, o_ref):
    x = x_ref[0] + g_ref[0] * (ya_ref[0] + yb_ref[0])
    o_ref[0] = x * lax.rsqrt(jnp.mean(x * x, axis=-1, keepdims=True) + EPS) * w_ref[...]


def final_norm(x, ya, yb, gate, w, tm=512):
    b, L, d = x.shape
    tok = pl.BlockSpec((1, tm, d), lambda bi, i: (bi, i, 0))
    return pl.pallas_call(
        _final_kernel,
        out_shape=jax.ShapeDtypeStruct((b, L, d), F32),
        grid=(b, L // tm),
        in_specs=[tok, tok, tok, pl.BlockSpec((1, 1, d), _mod_map(gate, b)),
                  pl.BlockSpec((1, d), lambda bi, i: (0, 0))],
        out_specs=tok,
        compiler_params=_params("parallel", "arbitrary"),
        name="final_rmsnorm",
    )(x, ya, yb, gate, w.reshape(1, d))


def _regroup_in_weight(w_in):
    sizes = (SSD_WIDTH, SSD_CONV_CH, 2 * SSD_HEADS, HY_COLS, 2 * ML_WIDTH, ML_WIDTH, ML_WIDTH, 4 * ML_HEADS)
    parts, s = [], 0
    for n in sizes:
        parts.append(jnp.pad(w_in[:, s:s + n], ((0, 0), (0, -n % LANE))))
        s += n
    return jnp.concatenate(parts, axis=1).astype(BF16)


def kernel(x, c, ctx, c_ctx, w_mod, b_mod, norm1_w, norm2_w, w_in, w_out, ssd_conv_w, ssd_conv_b, ssd_dt_bias, ssd_a_log, ssd_d, ssd_norm_w, hy_conv_w, hy_conv_b, hy_pos_w1, hy_pos_b1, hy_pos_w2, hy_pos_b2, hy_pos_w3, hy_freq, hy_decay, hy_skip, ml_conv_w, ml_conv_b, ml_gate_b, ml_norm_w, grp_router_w, grp_router_b, exp_router_w, exp_router_b, moe_w_gate, moe_w_up, moe_w_down, final_norm_w):
    layer_params = dict(
        ssd_conv_w=ssd_conv_w, ssd_conv_b=ssd_conv_b, ssd_dt_bias=ssd_dt_bias, ssd_a_log=ssd_a_log,
        ssd_d=ssd_d, ssd_norm_w=ssd_norm_w, hy_conv_w=hy_conv_w, hy_conv_b=hy_conv_b,
        hy_pos_w1=hy_pos_w1, hy_pos_b1=hy_pos_b1, hy_pos_w2=hy_pos_w2, hy_pos_b2=hy_pos_b2,
        hy_pos_w3=hy_pos_w3, hy_freq=hy_freq, hy_decay=hy_decay, hy_skip=hy_skip,
        ml_conv_w=ml_conv_w, ml_conv_b=ml_conv_b, ml_gate_b=ml_gate_b, ml_norm_w=ml_norm_w)
    bsz, seq, d = x.shape
    n_ctx = ctx.shape[1]
    xl, xc = x, ctx
    moe_l = moe_c = None
    ssd0 = jnp.zeros((bsz, SSD_HEADS, HEAD_DIM, SSD_STATE), F32)
    ml0 = (jnp.zeros((bsz, ML_HEADS, HEAD_DIM, LANE), F32), jnp.zeros((bsz, SUBLANE, LANE), F32))
    c_rows = jnp.concatenate([c, c_ctx[None, :], jnp.zeros((SUBLANE - bsz - 1, d), F32)], axis=0)
    for i in range(DEPTH):
        last = i == DEPTH - 1
        p = {name: arr[i] for name, arr in layer_params.items()}
        sp, mp = ssd_prepare(p), ml_prepare(p)
        mod = modulation(c_rows, w_mod[i], b_mod[i]).reshape(SUBLANE, N_MOD, 1, d)
        mod_l = [mod[:bsz, k] for k in range(N_MOD)]
        mod_c = [mod[bsz:bsz + 1, k] for k in range(N_MOD)]
        w_in_p = _regroup_in_weight(w_in[i])
        w_out_b = w_out[i].astype(BF16)
        w_router = jnp.pad(jnp.concatenate([grp_router_w[i], exp_router_w[i]], axis=1),
                           ((0, 0), (0, ROUTER_COLS - N_GROUPS - N_EXPERTS)))
        b_router = jnp.pad(jnp.concatenate([grp_router_b[i], exp_router_b[i]]),
                           (0, ROUTER_COLS - N_GROUPS - N_EXPERTS)).reshape(1, ROUTER_COLS)

        uc_ssd, uc_hy, uc_ml, xc = norm_proj(xc, moe_c, norm1_w[i], mod_c[0], mod_c[1], w_in_p)
        yc_ssd, ssd_f, ssd_b = ssd_mixer(uc_ssd, sp, ssd0, ssd0, not last)
        yc_ml, ml_f, ml_b = ml_mixer(uc_ml, mp, ml0, ml0, not last)
        col_major = i % 2 == 1
        ul_ssd, ul_hy, ul_ml, xl = norm_proj(xl, moe_l, norm1_w[i], mod_l[0], mod_l[1], w_in_p, col_major)
        yl_ssd, _, _ = ssd_mixer(ul_ssd, sp, ssd_f, ssd_b, True)
        yl_ml, _, _ = ml_mixer(ul_ml, mp, ml_f, ml_b, True)
        h_spec = hyena_filter_spectra(*hyena_filter_signals(seq, p))
        yl_hy = hyena_long(ul_hy, h_spec, p['hy_conv_w'], p['hy_conv_b'], p['hy_skip'])
        xl = out_proj(yl_ssd, yl_hy, yl_ml, xl, mod_l[2], w_out_b, col_major)
        if not last:
            sig_c, inorm_c = hyena_filter_signals(n_ctx, p)
            yc_hy = hyena_short(uc_hy, sig_c, inorm_c, p['hy_conv_w'], p['hy_conv_b'], p['hy_skip'])
            xc = out_proj(yc_ssd, yc_hy, yc_ml, xc, mod_c[2], w_out_b)
        h_all, routed, counts_row = norm_router(xl, None if last else xc, norm2_w[i], (mod_l[3], mod_l[4]), (mod_c[3], mod_c[4]),
                                    w_router, b_router)
        lat, rest = moe_apply(h_all, routed, counts_row, moe_w_gate, moe_w_up, moe_w_down, i, bsz * seq)
        moe_l = (lat[0].reshape(bsz, seq, d), lat[1].reshape(bsz, seq, d), mod_l[5])
        if not last:
            moe_c = (rest[0].reshape(bsz, n_ctx, d), rest[1].reshape(bsz, n_ctx, d), mod_c[5])
    return final_norm(xl, *moe_l, final_norm_w)
```

```python
import functools
import math

import jax
import jax.numpy as jnp
import numpy as np
from jax import lax
from jax.experimental import pallas as pl
from jax.experimental.pallas import tpu as pltpu

D_MODEL = 1024
DEPTH = 2
GRID_W = 64
HEAD_DIM = 64
SSD_WIDTH = 384
SSD_HEADS = SSD_WIDTH // HEAD_DIM
SSD_GROUPS = 2
SSD_STATE = 64
HYENA_WIDTH = 256
HYENA_ORDER = 2
HYENA_BANDS = 16
ML_WIDTH = 384
ML_HEADS = ML_WIDTH // HEAD_DIM
N_GROUPS = 4
EXPERTS_PER_GROUP = 8
N_EXPERTS = N_GROUPS * EXPERTS_PER_GROUP
EXPERT_HIDDEN = 256
N_MOD = 6
EPS = 1e-6

LANE = 128
SUBLANE = 8
VMEM_LIMIT = 48 * 1024 * 1024
VMEM_LIMIT_HY = 56 * 1024 * 1024

SSD_CONV_CH = SSD_WIDTH + 2 * SSD_GROUPS * SSD_STATE
SSD_XBC0 = SSD_WIDTH
SSD_DT0 = SSD_XBC0 + SSD_CONV_CH
SSD_COLS = SSD_DT0 + LANE
HY_COLS = (HYENA_ORDER + 1) * HYENA_WIDTH
ML_V0 = 2 * ML_WIDTH
ML_O0 = ML_V0 + ML_WIDTH
ML_G0 = ML_O0 + ML_WIDTH
ML_COLS = ML_G0 + LANE
ROUTER_COLS = LANE

F32 = jnp.float32
BF16 = jnp.bfloat16
HI = lax.Precision.HIGHEST


def _params(*sem):
    return pltpu.CompilerParams(dimension_semantics=sem, vmem_limit_bytes=VMEM_LIMIT)


def _silu(x):
    return x * jax.nn.sigmoid(x)


def _softplus(x):
    return jnp.maximum(x, 0.0) + jnp.log(1.0 + jnp.exp(-jnp.abs(x)))


def _log_sigmoid(x):
    return jnp.minimum(x, 0.0) - jnp.log(1.0 + jnp.exp(-jnp.abs(x)))


def _mod_kernel(c_ref, w_ref, b_ref, o_ref):
    o_ref[...] = jnp.dot(_silu(c_ref[...]), w_ref[...], preferred_element_type=F32, precision=HI) + b_ref[...]


def modulation(c_rows, w_mod, b_mod):
    n = w_mod.shape[1]
    tn = 1536
    return pl.pallas_call(
        _mod_kernel,
        out_shape=jax.ShapeDtypeStruct((c_rows.shape[0], n), F32),
        grid=(n // tn,),
        in_specs=[pl.BlockSpec(c_rows.shape, lambda j: (0, 0)),
                  pl.BlockSpec((w_mod.shape[0], tn), lambda j: (0, j)),
                  pl.BlockSpec((1, tn), lambda j: (0, j))],
        out_specs=pl.BlockSpec((c_rows.shape[0], tn), lambda j: (0, j)),
        compiler_params=_params("arbitrary"),
        name="adaln_modulation",
    )(c_rows, w_mod, b_mod.reshape(1, n))


def _modnorm(x, nw, shift, scale):
    y = x * lax.rsqrt(jnp.mean(x * x, axis=-1, keepdims=True) + EPS) * nw
    return y * (1.0 + scale) + shift


def _mod_map(mod, b):
    return (lambda bi, i: (bi, 0, 0)) if mod.shape[0] == b else (lambda bi, i: (0, 0, 0))


def _tok_view(x, col_major):
    b, L, d = x.shape
    return x.reshape(b, L // GRID_W, GRID_W, d) if col_major else x


def _tok_spec(L, d, tm, col_major):
    if col_major:
        assert tm == (L // GRID_W) * SUBLANE
        return pl.BlockSpec((1, L // GRID_W, SUBLANE, d), lambda bi, i: (bi, 0, i, 0))
    return pl.BlockSpec((1, tm, d), lambda bi, i: (bi, i, 0))


def _tok_load(ref, col_major):
    if not col_major:
        return ref[0]
    return jnp.concatenate([ref[0, :, j, :] for j in range(ref.shape[2])], axis=0)


def _tok_store(ref, val, col_major):
    if not col_major:
        ref[0] = val
        return
    rows = ref.shape[1]
    for j in range(ref.shape[2]):
        ref[0, :, j, :] = val[j * rows:(j + 1) * rows]


def _norm_proj_kernel(*refs, col_major, fuse_moe):
    if fuse_moe:
        x_ref, ya_ref, yb_ref, g_ref, nw_ref, sh_ref, sc_ref, w_ref, ssd_ref, hy_ref, ml_ref, xo_ref = refs
    else:
        x_ref, nw_ref, sh_ref, sc_ref, w_ref, ssd_ref, hy_ref, ml_ref = refs
    x = _tok_load(x_ref, col_major)
    if fuse_moe:
        x = x + g_ref[0] * (_tok_load(ya_ref, col_major) + _tok_load(yb_ref, col_major))
        _tok_store(xo_ref, x, col_major)
    h = _modnorm(x, nw_ref[...], sh_ref[0], sc_ref[0])
    u = jnp.dot(h.astype(BF16), w_ref[...], preferred_element_type=F32)
    ssd_ref[0] = u[:, 0:SSD_COLS]
    hy_ref[0] = u[:, SSD_COLS:SSD_COLS + HY_COLS]
    ml_ref[0] = u[:, SSD_COLS + HY_COLS:]


def norm_proj(x, moe, nw, shift, scale, w_bf16, col_major=False, tm=512):
    b, L, d = x.shape
    n = w_bf16.shape[1]
    fuse_moe = moe is not None
    tm = (L // GRID_W) * SUBLANE if col_major else min(tm, L)
    tok = _tok_spec(L, d, tm, col_major)
    row = lambda bi, i: (bi, i, 0)
    const2 = lambda bi, i: (0, 0)
    args, in_specs = [_tok_view(x, col_major)], [tok]
    if fuse_moe:
        ya, yb, gate = moe
        args += [_tok_view(ya, col_major), _tok_view(yb, col_major), gate]
        in_specs += [tok, tok, pl.BlockSpec((1, 1, d), _mod_map(gate, b))]
    args += [nw.reshape(1, d), shift, scale, w_bf16]
    in_specs += [pl.BlockSpec((1, d), const2), pl.BlockSpec((1, 1, d), _mod_map(shift, b)),
                 pl.BlockSpec((1, 1, d), _mod_map(scale, b)), pl.BlockSpec((d, n), const2)]
    out_shape = [jax.ShapeDtypeStruct((b, L, SSD_COLS), F32), jax.ShapeDtypeStruct((b, L, HY_COLS), F32),
                 jax.ShapeDtypeStruct((b, L, ML_COLS), F32)]
    out_specs = [pl.BlockSpec((1, tm, SSD_COLS), row), pl.BlockSpec((1, tm, HY_COLS), row),
                 pl.BlockSpec((1, tm, ML_COLS), row)]
    if fuse_moe:
        out_shape.append(jax.ShapeDtypeStruct(args[0].shape, F32))
        out_specs.append(tok)
    outs = pl.pallas_call(
        functools.partial(_norm_proj_kernel, col_major=col_major, fuse_moe=fuse_moe),
        out_shape=tuple(out_shape),
        grid=(b, L // tm),
        in_specs=in_specs,
        out_specs=tuple(out_specs),
        compiler_params=_params("parallel", "arbitrary"),
        name="norm_in_proj",
    )(*args)
    return (*outs[:3], outs[3].reshape(b, L, d) if fuse_moe else x)


def _out_proj_kernel(ys_ref, yh_ref, ym_ref, x_ref, g_ref, w_ref, o_ref, *, col_major):
    y = jnp.concatenate([ys_ref[0], yh_ref[0], ym_ref[0]], axis=-1).astype(BF16)
    r = _tok_load(x_ref, col_major) + g_ref[0] * jnp.dot(y, w_ref[...], preferred_element_type=F32)
    _tok_store(o_ref, r, col_major)


def out_proj(y_ssd, y_hy, y_ml, x, gate, w_bf16, col_major=False, tm=512):
    b, L, d = x.shape
    tm = (L // GRID_W) * SUBLANE if col_major else min(tm, L)
    row = lambda bi, i: (bi, i, 0)
    tok = _tok_spec(L, d, tm, col_major)
    xv = _tok_view(x, col_major)
    return pl.pallas_call(
        functools.partial(_out_proj_kernel, col_major=col_major),
        out_shape=jax.ShapeDtypeStruct(xv.shape, F32),
        grid=(b, L // tm),
        in_specs=[pl.BlockSpec((1, tm, SSD_WIDTH), row), pl.BlockSpec((1, tm, HYENA_WIDTH), row),
                  pl.BlockSpec((1, tm, ML_WIDTH), row), tok,
                  pl.BlockSpec((1, 1, d), _mod_map(gate, b)),
                  pl.BlockSpec(w_bf16.shape, lambda bi, i: (0, 0))],
        out_specs=tok,
        compiler_params=_params("parallel", "arbitrary"),
        name="out_proj_residual",
    )(y_ssd, y_hy, y_ml, xv, gate, w_bf16).reshape(b, L, d)


def _conv3(xr, prev_row, next_row, cw, cb, q):
    rid = lax.broadcasted_iota(jnp.int32, (q, 1), 0)
    x_prev = jnp.where(rid == 0, prev_row, pltpu.roll(xr, 1, axis=0))
    x_next = jnp.where(rid == q - 1, next_row, pltpu.roll(xr, q - 1, axis=0))
    return x_prev * cw[0:1] + xr * cw[1:2] + x_next * cw[2:3] + cb


def _scan_mask(q, direction):
    li = lax.broadcasted_iota(jnp.int32, (q, q), 0)
    si = lax.broadcasted_iota(jnp.int32, (q, q), 1)
    return (si <= li) if direction == 0 else (si >= li)


def _running_max(x, direction, q):
    rid = lax.broadcasted_iota(jnp.int32, (q, 1), 0)
    s = 1
    while s < q:
        if direction == 0:
            x = jnp.where(rid >= s, jnp.maximum(x, pltpu.roll(x, s, axis=0)), x)
        else:
            x = jnp.where(rid < q - s, jnp.maximum(x, pltpu.roll(x, q - s, axis=0)), x)
        s *= 2
    return x


def _scan_specs(L, q, nc, cols, direction):
    hb = q // SUBLANE
    nrb = L // SUBLANE
    cidx = (lambda j: j) if direction == 0 else (lambda j: nc - 1 - j)
    specs = [pl.BlockSpec((1, q, cols), lambda bi, j: (bi, cidx(j), 0)),
             pl.BlockSpec((1, SUBLANE, cols), lambda bi, j: (bi, jnp.maximum(cidx(j) * hb - 1, 0), 0)),
             pl.BlockSpec((1, SUBLANE, cols), lambda bi, j: (bi, jnp.minimum((cidx(j) + 1) * hb, nrb - 1), 0))]
    return specs, cidx


def _ssd_kernel(*refs, direction, finalize, q, nc):
    if finalize:
        (u_ref, prev_ref, next_ref, yb_ref, init_ref, cw_ref, cb_ref, dtb_ref, a_ref, d_ref, nw_ref,
         y_ref, fin_ref, state_ref) = refs
    else:
        (u_ref, prev_ref, next_ref, init_ref, cw_ref, cb_ref, dtb_ref, a_ref,
         y_ref, fin_ref, state_ref) = refs
    j = pl.program_id(1)
    c = j if direction == 0 else nc - 1 - j

    @pl.when(j == 0)
    def _():
        state_ref[...] = init_ref[0]

    prev_row = jnp.where(c > 0, prev_ref[0, SUBLANE - 1:SUBLANE, SSD_XBC0:SSD_DT0], 0.0)
    next_row = jnp.where(c < nc - 1, next_ref[0, 0:1, SSD_XBC0:SSD_DT0], 0.0)
    xc = _silu(_conv3(u_ref[0, :, SSD_XBC0:SSD_DT0], prev_row, next_row, cw_ref[...], cb_ref[...], q))

    dt = _softplus(u_ref[0, :, SSD_DT0:SSD_COLS] + dtb_ref[...])
    mask = _scan_mask(q, direction)
    cum = jnp.dot(mask.astype(F32), dt * a_ref[...], preferred_element_type=F32, precision=HI)
    cum_t = cum.T
    end = q - 1 if direction == 0 else 0

    ys = []
    for g in range(SSD_GROUPS):
        b0 = SSD_WIDTH + g * SSD_STATE
        c0 = SSD_WIDTH + (SSD_GROUPS + g) * SSD_STATE
        bm_t = xc[:, b0:b0 + SSD_STATE].T
        cm = xc[:, c0:c0 + SSD_STATE].astype(BF16)
        scores = jnp.dot(cm, bm_t.astype(BF16), preferred_element_type=F32)
        for h in range(g * (SSD_HEADS // SSD_GROUPS), (g + 1) * (SSD_HEADS // SSD_GROUPS)):
            hl = direction * SSD_HEADS + h
            col = cum[:, hl:hl + 1]
            row = cum_t[hl:hl + 1, :]
            seg = jnp.exp(jnp.where(mask, col - row, -jnp.inf))
            xdt = (xc[:, h * HEAD_DIM:(h + 1) * HEAD_DIM] * dt[:, hl:hl + 1]).astype(BF16)
            y = jnp.dot((scores * seg).astype(BF16), xdt, preferred_element_type=F32)
            st = state_ref[h]
            y = y + jnp.dot(cm, st.astype(BF16), preferred_element_type=F32) * jnp.exp(col)
            tot = cum[end:end + 1, hl:hl + 1]
            upd = jnp.dot((bm_t * jnp.exp(tot - row)).astype(BF16), xdt, preferred_element_type=F32)
            state_ref[h] = st * jnp.exp(tot) + upd
            ys.append(y)
    y_all = jnp.concatenate(ys, axis=-1)
    if finalize:
        t = (y_all + yb_ref[0] + xc[:, 0:SSD_WIDTH] * d_ref[...]) * _silu(u_ref[0, :, 0:SSD_WIDTH])
        y_all = t * lax.rsqrt(jnp.mean(t * t, axis=-1, keepdims=True) + EPS) * nw_ref[...]
    y_ref[0] = y_all

    @pl.when(j == nc - 1)
    def _():
        fin_ref[0] = state_ref[...]


def ssd_pass(u, y_other, init, sp, direction, q):
    b, L, _ = u.shape
    q = min(q, L)
    nc = L // q
    finalize = y_other is not None
    in_specs, cidx = _scan_specs(L, q, nc, SSD_COLS, direction)
    const2 = lambda bi, j: (0, 0)
    st_spec = pl.BlockSpec((1, SSD_HEADS, HEAD_DIM, SSD_STATE), lambda bi, j: (bi, 0, 0, 0))
    y_spec = pl.BlockSpec((1, q, SSD_WIDTH), lambda bi, j: (bi, cidx(j), 0))
    args = [u, u, u]
    if finalize:
        in_specs.append(y_spec)
        args.append(y_other)
    consts = [sp['cw'], sp['cb'], sp['dtb'], sp['a']] + ([sp['d'], sp['nw']] if finalize else [])
    in_specs += [st_spec] + [pl.BlockSpec(t.shape, const2) for t in consts]
    args += [init] + consts
    return pl.pallas_call(
        functools.partial(_ssd_kernel, direction=direction, finalize=finalize, q=q, nc=nc),
        out_shape=(jax.ShapeDtypeStruct((b, L, SSD_WIDTH), F32),
                   jax.ShapeDtypeStruct((b, SSD_HEADS, HEAD_DIM, SSD_STATE), F32)),
        grid=(b, nc),
        in_specs=in_specs,
        out_specs=(y_spec, st_spec),
        scratch_shapes=[pltpu.VMEM((SSD_HEADS, HEAD_DIM, SSD_STATE), F32)],
        compiler_params=_params("parallel", "arbitrary"),
        name="ssd_scan_%s" % ("fwd" if direction == 0 else "bwd"),
    )(*args)


def ssd_prepare(p):
    pad = lambda v: jnp.pad(v.reshape(1, -1), ((0, 0), (0, LANE - v.size)))
    return dict(cw=p['ssd_conv_w'], cb=p['ssd_conv_b'].reshape(1, -1),
                dtb=pad(p['ssd_dt_bias']), a=pad(-jnp.exp(p['ssd_a_log'])),
                d=jnp.repeat(p['ssd_d'], HEAD_DIM).reshape(1, -1), nw=p['ssd_norm_w'].reshape(1, -1))


def ssd_mixer(u, sp, init_f, init_b, want_y, q=256):
    yb, fin_b = ssd_pass(u, None, init_b, sp, 1, q)
    y, fin_f = ssd_pass(u, yb if want_y else None, init_f, sp, 0, q)
    return y, fin_f, fin_b


def _ml_kernel(*refs, direction, finalize, q, nc):
    if finalize:
        (u_ref, prev_ref, next_ref, hb_ref, s_init_ref, m_init_ref, cw_ref, cb_ref, gb_ref, nw_ref, pool_ref,
         y_ref, s_fin_ref, m_fin_ref, s_ref, m_ref) = refs
    else:
        (u_ref, prev_ref, next_ref, s_init_ref, m_init_ref, cw_ref, cb_ref, gb_ref,
         y_ref, s_fin_ref, m_fin_ref, s_ref, m_ref) = refs
    j = pl.program_id(1)
    c = j if direction == 0 else nc - 1 - j

    @pl.when(j == 0)
    def _():
        s_ref[...] = s_init_ref[0]
        m_ref[...] = m_init_ref[0]

    prev_row = jnp.where(c > 0, prev_ref[0, SUBLANE - 1:SUBLANE, 0:ML_V0], 0.0)
    next_row = jnp.where(c < nc - 1, next_ref[0, 0:1, 0:ML_V0], 0.0)
    qk = _silu(_conv3(u_ref[0, :, 0:ML_V0], prev_row, next_row, cw_ref[...], cb_ref[...], q))
    v = u_ref[0, :, ML_V0:ML_O0]

    gb = u_ref[0, :, ML_G0:ML_COLS] + gb_ref[...]
    mask = _scan_mask(q, direction)
    cum = jnp.dot(mask.astype(F32), _log_sigmoid(gb), preferred_element_type=F32, precision=HI)
    ig = pltpu.roll(gb, ML_HEADS, axis=1)
    end = q - 1 if direction == 0 else 0
    m_prev = m_ref[0:1, :]
    tot = cum[end:end + 1, :]
    w_end = tot - cum + ig
    m_loc = jnp.max(w_end, axis=0, keepdims=True)
    e_end = jnp.exp(w_end - m_loc)
    m_new = jnp.maximum(tot + m_prev, m_loc)
    a_prev = jnp.exp(tot + m_prev - m_new)
    a_loc = jnp.exp(m_loc - m_new)
    inter = cum + m_prev
    rel = ig - cum
    m_t = jnp.maximum(inter, cum + _running_max(rel, direction, q))
    col_a = cum - m_t
    a_inter = jnp.exp(inter - m_t)
    floor = jnp.exp(-m_t)
    rel_t = rel.T
    e_end_t = e_end.T
    k_t = (qk[:, ML_WIDTH:2 * ML_WIDTH] * (HEAD_DIM ** -0.5)).T
    one_col = (lax.broadcasted_iota(jnp.int32, (q, HEAD_DIM), 1) == 0).astype(F32)

    ys = []
    for h in range(ML_HEADS):
        fl = direction * 2 * ML_HEADS + ML_HEADS + h
        qh = qk[:, h * HEAD_DIM:(h + 1) * HEAD_DIM].astype(BF16)
        kh_t = k_t[h * HEAD_DIM:(h + 1) * HEAD_DIM, :]
        v_ext = jnp.concatenate([v[:, h * HEAD_DIM:(h + 1) * HEAD_DIM], one_col], axis=-1).astype(BF16)
        pw = jnp.exp(jnp.where(mask, col_a[:, fl:fl + 1] + rel_t[fl:fl + 1, :], -jnp.inf))
        scores = jnp.dot(qh, kh_t.astype(BF16), preferred_element_type=F32)
        nd = jnp.dot((scores * pw).astype(BF16), v_ext, preferred_element_type=F32)
        st = s_ref[h]
        nd = nd + a_inter[:, fl:fl + 1] * jnp.dot(qh, st.astype(BF16), preferred_element_type=F32)
        den = nd[:, HEAD_DIM:HEAD_DIM + 1]
        ys.append(nd[:, 0:HEAD_DIM] / jnp.maximum(jnp.abs(den), floor[:, fl:fl + 1]))
        upd = jnp.dot((kh_t * e_end_t[fl:fl + 1, :]).astype(BF16), v_ext, preferred_element_type=F32)
        s_ref[h] = a_prev[:, fl:fl + 1] * st + a_loc[:, fl:fl + 1] * upd
    m_ref[...] = jnp.broadcast_to(m_new, m_ref.shape)
    y_all = jnp.concatenate(ys, axis=-1)
    if finalize:
        hs = y_all + hb_ref[0]
        hc = hs - jnp.dot(hs.astype(BF16), pool_ref[...], preferred_element_type=F32)
        var = jnp.dot((hc * hc).astype(BF16), pool_ref[...], preferred_element_type=F32)
        y_all = hc * lax.rsqrt(var + EPS) * nw_ref[...] * jax.nn.sigmoid(u_ref[0, :, ML_O0:ML_G0])
    y_ref[0] = y_all

    @pl.when(j == nc - 1)
    def _():
        s_fin_ref[0] = s_ref[...]
        m_fin_ref[0] = m_ref[...]


def ml_pass(u, h_other, init, mp, direction, q):
    b, L, _ = u.shape
    q = min(q, L)
    nc = L // q
    finalize = h_other is not None
    in_specs, cidx = _scan_specs(L, q, nc, ML_COLS, direction)
    const2 = lambda bi, j: (0, 0)
    s_spec = pl.BlockSpec((1, ML_HEADS, HEAD_DIM, LANE), lambda bi, j: (bi, 0, 0, 0))
    m_spec = pl.BlockSpec((1, SUBLANE, LANE), lambda bi, j: (bi, 0, 0))
    y_spec = pl.BlockSpec((1, q, ML_WIDTH), lambda bi, j: (bi, cidx(j), 0))
    args = [u, u, u]
    if finalize:
        in_specs.append(y_spec)
        args.append(h_other)
    consts = [mp['cw'], mp['cb'], mp['gb']] + ([mp['nw'], mp['pool']] if finalize else [])
    in_specs += [s_spec, m_spec] + [pl.BlockSpec(t.shape, const2) for t in consts]
    args += [init[0], init[1]] + consts
    y, s_fin, m_fin = pl.pallas_call(
        functools.partial(_ml_kernel, direction=direction, finalize=finalize, q=q, nc=nc),
        out_shape=(jax.ShapeDtypeStruct((b, L, ML_WIDTH), F32),
                   jax.ShapeDtypeStruct((b, ML_HEADS, HEAD_DIM, LANE), F32),
                   jax.ShapeDtypeStruct((b, SUBLANE, LANE), F32)),
        grid=(b, nc),
        in_specs=in_specs,
        out_specs=(y_spec, s_spec, m_spec),
        scratch_shapes=[pltpu.VMEM((ML_HEADS, HEAD_DIM, LANE), F32), pltpu.VMEM((SUBLANE, LANE), F32)],
        compiler_params=_params("parallel", "arbitrary"),
        name="mlstm_scan_%s" % ("fwd" if direction == 0 else "bwd"),
    )(*args)
    return y, (s_fin, m_fin)


def ml_prepare(p):
    gb = p['ml_gate_b'].reshape(1, -1)
    head = np.arange(ML_WIDTH) // HEAD_DIM
    pool = jnp.asarray((head[:, None] == head[None, :]) / HEAD_DIM, BF16)
    return dict(cw=p['ml_conv_w'], cb=p['ml_conv_b'].reshape(1, -1),
                gb=jnp.pad(gb, ((0, 0), (0, LANE - gb.shape[1]))), nw=p['ml_norm_w'].reshape(1, -1), pool=pool)


def ml_mixer(u, mp, init_f, init_b, want_y, q=256):
    hb, fin_b = ml_pass(u, None, init_b, mp, 1, q)
    y, fin_f = ml_pass(u, hb if want_y else None, init_f, mp, 0, q)
    return y, fin_f, fin_b


FFT_L = 4096
FFT_N = 2 * FFT_L
FFT_N2 = 128
FFT_N1 = FFT_N // FFT_N2
FFT_N1H = FFT_L // FFT_N2
FFT_K1 = FFT_N1 // 2 + 1
FFT_R = 80
FFT_UNROLL = 8
FFT_PITCH = FFT_N2 + SUBLANE


def _fft_tables():
    n2 = np.arange(FFT_N2)[:, None, None]
    k1 = np.arange(FFT_K1)[None, :, None]
    n1 = np.arange(FFT_N1H)[None, None, :]
    th = 2 * np.pi * (((FFT_N2 * n1 + n2) * k1) % FFT_N) / FFT_N
    f1 = np.zeros((FFT_N2, FFT_R, FFT_N1H))
    f1[:, 0:2 * FFT_K1:2, :] = np.cos(th)
    f1[:, 1:2 * FFT_K1:2, :] = -np.sin(th)
    wgt = np.where((np.arange(FFT_K1) == 0) | (np.arange(FFT_K1) == FFT_N1 // 2), 1.0, 2.0)[None, :, None] / FFT_N
    g1 = np.zeros((FFT_N2, FFT_N1H, FFT_R))
    g1[:, :, 0:2 * FFT_K1:2] = np.transpose(wgt * np.cos(th), (0, 2, 1))
    g1[:, :, 1:2 * FFT_K1:2] = np.transpose(-wgt * np.sin(th), (0, 2, 1))
    ph = 2 * np.pi * ((np.arange(FFT_N2)[:, None] * np.arange(FFT_N2)[None, :]) % FFT_N2) / FFT_N2
    c, s = np.cos(ph), np.sin(ph)
    f2 = np.block([[c, s], [-s, c]])
    as_bf = lambda a: jnp.asarray(a, F32).astype(BF16)
    return as_bf(f1), as_bf(f2), as_bf(f2.T), as_bf(g1)


def _sld(ref, n2, count):
    rows = pl.ds(n2, count, stride=FFT_PITCH)
    return jnp.concatenate([ref[0, rows, :], ref[1, rows, :]], axis=-1)


def _sst(ref, n2, count, val):
    rows = pl.ds(n2, count, stride=FFT_PITCH)
    ref[0, rows, :] = val[:, 0:LANE]
    ref[1, rows, :] = val[:, LANE:2 * LANE]


def _blk_ld(ref, blk, nblk):
    parts = []
    for k in range(nblk):
        rows = pl.ds(pl.multiple_of((blk + k) * FFT_PITCH, SUBLANE), FFT_N2)
        parts.append(jnp.concatenate([ref[0, rows, :], ref[1, rows, :]], axis=-1))
    return parts[0] if nblk == 1 else jnp.concatenate(parts, axis=0)


def _blk_st(ref, blk, nblk, val):
    for k in range(nblk):
        rows = pl.ds(pl.multiple_of((blk + k) * FFT_PITCH, SUBLANE), FFT_N2)
        ref[0, rows, :] = val[k * FFT_N2:(k + 1) * FFT_N2, 0:LANE]
        ref[1, rows, :] = val[k * FFT_N2:(k + 1) * FFT_N2, LANE:2 * LANE]


def _fft_stage1(z_ref, a_ref, f1_ref, n_in=FFT_N1H):
    def body(n2, carry):
        xs = _sld(z_ref, n2, n_in).astype(BF16)
        _sst(a_ref, n2, FFT_R, jnp.dot(f1_ref[n2], xs, preferred_element_type=F32))
        return carry
    lax.fori_loop(0, FFT_N2, body, 0, unroll=FFT_UNROLL)


def _spectrum_kernel(x_ref, inorm_ref, f1_ref, f2_ref, o_ref, z_ref, a_ref):
    for half in range(2):
        for n1 in range(FFT_N1H):
            _blk_st(z_ref, half * FFT_N1H + n1, 1, x_ref[half, n1 * FFT_N2:(n1 + 1) * FFT_N2, :])
    _fft_stage1(z_ref, a_ref, f1_ref, 2 * FFT_N1H)
    inorm = inorm_ref[0]

    def body(k1, carry):
        slab = _blk_ld(a_ref, 2 * k1, 2).astype(BF16)
        o_ref[0, k1] = (jnp.dot(f2_ref[...], slab, preferred_element_type=F32) * inorm).astype(BF16)
        return carry
    lax.fori_loop(0, FFT_K1, body, 0, unroll=3)


def hyena_filter_spectra(sig, inv_norm):
    _, L, c = sig.shape
    f1, f2, _, _ = _fft_tables()
    sign = np.where(np.arange(FFT_R) // 2 % 2 == 0, 1.0, -1.0)[None, :, None].astype(np.float32)
    f1ab = jnp.concatenate([f1, f1 * jnp.asarray(sign, BF16)], axis=2)
    one = pl.Buffered(1)
    return pl.pallas_call(
        _spectrum_kernel,
        out_shape=jax.ShapeDtypeStruct((HYENA_ORDER, FFT_K1, 2 * FFT_N2, c), BF16),
        grid=(HYENA_ORDER,),
        in_specs=[pl.BlockSpec((2, L, c), lambda i: (i, 0, 0)),
                  pl.BlockSpec((1, 1, c), lambda i: (i, 0, 0)),
                  pl.BlockSpec(f1ab.shape, lambda i: (0, 0, 0), pipeline_mode=one),
                  pl.BlockSpec(f2.shape, lambda i: (0, 0), pipeline_mode=one)],
        out_specs=pl.BlockSpec((1, FFT_K1, 2 * FFT_N2, c), lambda i: (i, 0, 0, 0)),
        scratch_shapes=[pltpu.VMEM((2, 2 * FFT_N1H * FFT_PITCH, LANE), F32),
                        pltpu.VMEM((2, FFT_R * FFT_PITCH, LANE), F32)],
        compiler_params=pltpu.CompilerParams(dimension_semantics=("arbitrary",), vmem_limit_bytes=VMEM_LIMIT_HY),
        name="hyena_filter_spectrum",
    )(sig, inv_norm, f1ab, f2)


def _conv3_rows(src, dst, cw, cb, L):
    rows = 2 * FFT_N2
    nchunk = L // rows

    def body(i, carry):
        r0 = pl.multiple_of(i * rows, rows)
        prev_row = jnp.where(i > 0, src[pl.ds(jnp.maximum(r0 - 1, 0), 1), :], 0.0)
        next_row = jnp.where(i < nchunk - 1, src[pl.ds(jnp.minimum(r0 + rows, L - 1), 1), :], 0.0)
        _blk_st(dst, 2 * i, 2, _conv3(src[pl.ds(r0, rows), :], prev_row, next_row, cw, cb, rows))
        return carry
    lax.fori_loop(0, nchunk, body, 0, unroll=2)


def _hyena_kernel(v_ref, g_ref, h_ref, cwv_ref, cbv_ref, cwg_ref, cbg_ref, skip_ref,
                  f1_ref, f2_ref, f2t_ref, g1_ref, o_ref, z_ref, gc_ref, a_ref):
    order = pl.program_id(1)

    @pl.when(order == 0)
    def _():
        _conv3_rows(v_ref.at[0], z_ref, cwv_ref[...], cbv_ref[...], FFT_L)

    _conv3_rows(g_ref.at[0], gc_ref, cwg_ref[0], cbg_ref[0], FFT_L)
    _fft_stage1(z_ref, a_ref, f1_ref)

    def mid(k1, carry):
        x = jnp.dot(f2_ref[...], _blk_ld(a_ref, 2 * k1, 2).astype(BF16), preferred_element_type=F32)
        h = h_ref[0, k1].astype(F32)
        xr, xi, hr, hi = x[:FFT_N2], x[FFT_N2:], h[:FFT_N2], h[FFT_N2:]
        y = jnp.concatenate([xr * hr - xi * hi, xr * hi + xi * hr], axis=0).astype(BF16)
        _blk_st(a_ref, 2 * k1, 2, jnp.dot(f2t_ref[...], y, preferred_element_type=F32))
        return carry
    lax.fori_loop(0, FFT_K1, mid, 0, unroll=3)

    skip = skip_ref[0]

    def last(n2, carry):
        bs = _sld(a_ref, n2, FFT_R).astype(BF16)
        y = jnp.dot(g1_ref[n2], bs, preferred_element_type=F32)
        _sst(z_ref, n2, FFT_N1H, _sld(gc_ref, n2, FFT_N1H) * (y + skip * _sld(z_ref, n2, FFT_N1H)))
        return carry
    lax.fori_loop(0, FFT_N2, last, 0, unroll=FFT_UNROLL)

    @pl.when(order == 1)
    def _():
        for n1 in range(FFT_N1H):
            o_ref[0, n1 * FFT_N2:(n1 + 1) * FFT_N2, :] = _blk_ld(z_ref, n1, 1)


def hyena_long(u, h_spec, cw, cb, skip):
    b, L, _ = u.shape
    c = HYENA_WIDTH
    f1, f2, f2t, g1 = _fft_tables()
    one = pl.Buffered(1)
    cw3 = cw.reshape(3, 3, c).transpose(1, 0, 2)
    cb3 = cb.reshape(3, 1, c)
    return pl.pallas_call(
        _hyena_kernel,
        out_shape=jax.ShapeDtypeStruct((b, L, c), F32),
        grid=(b, 2),
        in_specs=[pl.BlockSpec((1, L, c), lambda bi, o: (bi, 0, 0), pipeline_mode=one),
                  pl.BlockSpec((1, L, c), lambda bi, o: (bi, 0, 1 + o)),
                  pl.BlockSpec((1, FFT_K1, 2 * FFT_N2, c), lambda bi, o: (o, 0, 0, 0)),
                  pl.BlockSpec((3, c), lambda bi, o: (0, 0)),
                  pl.BlockSpec((1, c), lambda bi, o: (0, 0)),
                  pl.BlockSpec((1, 3, c), lambda bi, o: (1 + o, 0, 0)),
                  pl.BlockSpec((1, 1, c), lambda bi, o: (1 + o, 0, 0)),
                  pl.BlockSpec((1, 1, c), lambda bi, o: (o, 0, 0)),
                  pl.BlockSpec(f1.shape, lambda bi, o: (0, 0, 0), pipeline_mode=one),
                  pl.BlockSpec(f2.shape, lambda bi, o: (0, 0), pipeline_mode=one),
                  pl.BlockSpec(f2t.shape, lambda bi, o: (0, 0), pipeline_mode=one),
                  pl.BlockSpec(g1.shape, lambda bi, o: (0, 0, 0), pipeline_mode=one)],
        out_specs=pl.BlockSpec((1, L, c), lambda bi, o: (bi, 0, 0)),
        scratch_shapes=[pltpu.VMEM((2, FFT_N1H * FFT_PITCH, LANE), F32),
                        pltpu.VMEM((2, FFT_N1H * FFT_PITCH, LANE), F32),
                        pltpu.VMEM((2, FFT_R * FFT_PITCH, LANE), F32)],
        compiler_params=pltpu.CompilerParams(dimension_semantics=("parallel", "arbitrary"),
                                             vmem_limit_bytes=VMEM_LIMIT_HY),
        name="hyena_long_conv",
    )(u, u, h_spec, cw3[0], cb3[0], cw3, cb3, skip.reshape(2, 1, c), f1, f2, f2t, g1)


HY_FILT = 2 * HYENA_ORDER * HYENA_WIDTH
HY_HALF = HYENA_ORDER * HYENA_WIDTH


def _filter_kernel(wt_ref, wc_ref, ws_ref, b1_ref, w2_ref, b2_ref, w3_ref, freq_ref, decay_ref,
                   sig_ref, asum_ref, *, L, rows):
    i = pl.program_id(0)
    n = (i * rows + lax.broadcasted_iota(jnp.int32, (rows, 1), 0)).astype(F32)
    band = lax.broadcasted_iota(jnp.int32, (1, HYENA_BANDS), 1).astype(F32)
    bands = 1e-4 + band * ((HYENA_BANDS - 1 - 1e-4) / (HYENA_BANDS - 1))
    freq = freq_ref[...]

    @pl.when(i == 0)
    def _():
        asum_ref[...] = jnp.zeros_like(asum_ref)

    for side, t in enumerate((n, L - n)):
        t_unit = t / float(max(L - 1, 1))
        ang = (2 * math.pi / L) * t * bands
        pre = (t_unit * wt_ref[...] + jnp.dot(jnp.cos(ang), wc_ref[...], preferred_element_type=F32, precision=HI)
               - jnp.dot(jnp.sin(ang), ws_ref[...], preferred_element_type=F32, precision=HI) + b1_ref[...])
        hdn = jnp.sin(freq * pre)
        hdn = jnp.sin(freq * (jnp.dot(hdn, w2_ref[...], preferred_element_type=F32, precision=HI) + b2_ref[...]))
        cols = slice(side * HY_HALF, (side + 1) * HY_HALF)
        val = jnp.dot(hdn, w3_ref[:, cols], preferred_element_type=F32, precision=HI)
        val = val * jnp.exp(-t_unit * jnp.abs(decay_ref[:, cols]))
        if side == 1:
            val = jnp.where(n > 0, val, 0.0)
        for o in range(HYENA_ORDER):
            sig_ref[2 * o + side] = val[:, o * HYENA_WIDTH:(o + 1) * HYENA_WIDTH]
        asum_ref[side:side + 1, :] += jnp.sum(jnp.abs(val), axis=0, keepdims=True)


def hyena_filter_signals(L, p):
    rows = min(L, 512)
    fh = p['hy_pos_w1'].shape[1]
    w1 = p['hy_pos_w1']
    consts = [w1[0:1], w1[1:1 + HYENA_BANDS], w1[1 + HYENA_BANDS:], p['hy_pos_b1'].reshape(1, fh), p['hy_pos_w2'],
              p['hy_pos_b2'].reshape(1, fh), p['hy_pos_w3'], p['hy_freq'].reshape(1, fh),
              p['hy_decay'].reshape(1, HY_FILT)]
    sig, asum = pl.pallas_call(
        functools.partial(_filter_kernel, L=L, rows=rows),
        out_shape=(jax.ShapeDtypeStruct((2 * HYENA_ORDER, L, HYENA_WIDTH), F32),
                   jax.ShapeDtypeStruct((SUBLANE, HY_HALF), F32)),
        grid=(L // rows,),
        in_specs=[pl.BlockSpec(c.shape, lambda i: (0, 0)) for c in consts],
        out_specs=(pl.BlockSpec((2 * HYENA_ORDER, rows, HYENA_WIDTH), lambda i: (0, i, 0)),
                   pl.BlockSpec((SUBLANE, HY_HALF), lambda i: (0, 0))),
        compiler_params=_params("arbitrary"),
        name="hyena_filters",
    )(*consts)
    inv_norm = 1.0 / (asum[0] + asum[1]).reshape(HYENA_ORDER, 1, HYENA_WIDTH)
    return sig, inv_norm


def _rdft_tables(L):
    n_bins = L + 1
    half = -(-n_bins // 16) * 16
    k = np.arange(n_bins)[:, None]
    n = np.arange(L)[None, :]
    th = 2 * np.pi * ((k * n) % (2 * L)) / (2 * L)
    f = np.zeros((2 * half, L))
    f[:n_bins] = np.cos(th)
    f[half:half + n_bins] = -np.sin(th)
    sign = np.where(np.arange(n_bins) % 2 == 0, 1.0, -1.0)[:, None]
    fs = np.zeros_like(f)
    fs[:n_bins] = f[:n_bins] * sign
    fs[half:half + n_bins] = f[half:half + n_bins] * sign
    wgt = np.where((np.arange(n_bins) == 0) | (np.arange(n_bins) == L), 1.0, 2.0)[None, :] / (2 * L)
    g = np.zeros((L, 2 * half))
    g[:, :n_bins] = wgt * np.cos(th).T
    g[:, half:half + n_bins] = -wgt * np.sin(th).T
    as_bf = lambda a: jnp.asarray(a, F32).astype(BF16)
    return as_bf(f), as_bf(fs), as_bf(g), half


def _hyena_ctx_kernel(u_ref, sig_ref, inorm_ref, cw_ref, cb_ref, skip_ref, f_ref, fs_ref, g_ref, o_ref, *, L, half):
    zero_row = jnp.zeros((1, u_ref.shape[2]), F32)
    uc = _conv3(u_ref[0], zero_row, zero_row, cw_ref[...], cb_ref[...], L)
    z = uc[:, 0:HYENA_WIDTH]
    for o in range(HYENA_ORDER):
        h = (jnp.dot(f_ref[...], sig_ref[2 * o].astype(BF16), preferred_element_type=F32)
             + jnp.dot(fs_ref[...], sig_ref[2 * o + 1].astype(BF16), preferred_element_type=F32)) * inorm_ref[o]
        x = jnp.dot(f_ref[...], z.astype(BF16), preferred_element_type=F32)
        xr, xi, hr, hi = x[:half], x[half:], h[:half], h[half:]
        y = jnp.concatenate([xr * hr - xi * hi, xr * hi + xi * hr], axis=0).astype(BF16)
        conv = jnp.dot(g_ref[...], y, preferred_element_type=F32)
        z = uc[:, (o + 1) * HYENA_WIDTH:(o + 2) * HYENA_WIDTH] * (conv + skip_ref[o] * z)
    o_ref[0] = z


def hyena_short(u, sig, inv_norm, cw, cb, skip):
    b, L, cols = u.shape
    f, fs, g, half = _rdft_tables(L)
    const2 = lambda bi: (0, 0)
    const3 = lambda bi: (0, 0, 0)
    return pl.pallas_call(
        functools.partial(_hyena_ctx_kernel, L=L, half=half),
        out_shape=jax.ShapeDtypeStruct((b, L, HYENA_WIDTH), F32),
        grid=(b,),
        in_specs=[pl.BlockSpec((1, L, cols), lambda bi: (bi, 0, 0)),
                  pl.BlockSpec(sig.shape, const3), pl.BlockSpec(inv_norm.shape, const3),
                  pl.BlockSpec(cw.shape, const2), pl.BlockSpec((1, cols), const2),
                  pl.BlockSpec((HYENA_ORDER, 1, HYENA_WIDTH), const3),
                  pl.BlockSpec(f.shape, const2), pl.BlockSpec(fs.shape, const2), pl.BlockSpec(g.shape, const2)],
        out_specs=pl.BlockSpec((1, L, HYENA_WIDTH), lambda bi: (bi, 0, 0)),
        compiler_params=_params("parallel"),
        name="hyena_context",
    )(u, sig, inv_norm, cw, cb.reshape(1, cols), skip.reshape(HYENA_ORDER, 1, HYENA_WIDTH), f, fs, g)


def _split_bf16(a):
    hi = a.astype(BF16)
    return hi, (a - hi.astype(F32)).astype(BF16)


def _norm_router_kernel(*refs, n_lat, two_src):
    if two_src:
        xl_ref, xc_ref, nw_ref, sh_ref, sc_ref, whi_ref, wlo_ref, br_ref, tri_ref, h_ref, lg_ref, cnt_ref = refs
        x = jnp.where(pl.program_id(0) < n_lat, xl_ref[...], xc_ref[...])
    else:
        xl_ref, nw_ref, sh_ref, sc_ref, whi_ref, wlo_ref, br_ref, tri_ref, h_ref, lg_ref, cnt_ref = refs
        x = xl_ref[...]
    h = _modnorm(x, nw_ref[...], sh_ref[0], sc_ref[0])
    h_ref[...] = h
    h_hi, h_lo = _split_bf16(h)
    lg = (jnp.dot(h_hi, whi_ref[...], preferred_element_type=F32)
          + jnp.dot(h_hi, wlo_ref[...], preferred_element_type=F32)
          + jnp.dot(h_lo, whi_ref[...], preferred_element_type=F32)) + br_ref[...]
    lane = lax.broadcasted_iota(jnp.int32, lg.shape, 1)
    first = lambda hit: jnp.min(jnp.where(hit, lane, ROUTER_COLS), axis=-1, keepdims=True)
    gl = jnp.where(lane < N_GROUPS, lg, -jnp.inf)
    gmax = jnp.max(gl, axis=-1, keepdims=True)
    grp = first(gl == gmax)
    grp_p = 1.0 / jnp.sum(jnp.exp(gl - gmax), axis=-1, keepdims=True)
    lo = N_GROUPS + grp * EXPERTS_PER_GROUP
    el = jnp.where((lane >= lo) & (lane < lo + EXPERTS_PER_GROUP), lg, -jnp.inf)
    e1 = jnp.max(el, axis=-1, keepdims=True)
    i1 = first(el == e1)
    el2 = jnp.where(lane == i1, -jnp.inf, el)
    e2 = jnp.max(el2, axis=-1, keepdims=True)
    i2 = first(el2 == e2)
    r = jnp.exp(e2 - e1)
    w1 = grp_p / (1.0 + r)
    w2 = w1 * r
    @pl.when(pl.program_id(0) == 0)
    def _():
        cnt_ref[...] = jnp.zeros_like(cnt_ref)

    hit1, hit2 = lane == i1, lane == i2
    picks = jnp.where(hit1 | hit2, 1.0, 0.0)
    before = cnt_ref[0:1, :] + jnp.dot(tri_ref[...], picks.astype(BF16), preferred_element_type=F32)
    rank1 = jnp.sum(jnp.where(hit1, before, 0.0), axis=-1, keepdims=True)
    rank2 = jnp.sum(jnp.where(hit2, before, 0.0), axis=-1, keepdims=True)
    cnt_ref[...] = jnp.broadcast_to(cnt_ref[0:1, :] + jnp.sum(picks, axis=0, keepdims=True), cnt_ref.shape)
    vals = [(i1 - N_GROUPS).astype(F32), (i2 - N_GROUPS).astype(F32), w1, w2, rank1, rank2]
    out = jnp.zeros(lg.shape, F32)
    for k, val in enumerate(vals):
        out = jnp.where(lane == k, val, out)
    lg_ref[...] = out


def norm_router(xl, xc, nw, mod_l, mod_c, w_router, b_router, tm=512):
    b, L, d = xl.shape
    two_src = xc is not None
    n_lat = b * L // tm
    per_batch = L // tm
    n_ctx = (xc.shape[0] * xc.shape[1]) // tm if two_src else 0
    w_hi, w_lo = _split_bf16(w_router)
    const = lambda i: (0, 0)
    if two_src:
        shift = jnp.concatenate([mod_l[0], mod_c[0]], axis=0)
        scale = jnp.concatenate([mod_l[1], mod_c[1]], axis=0)
        mod_map = lambda i: (jnp.where(i < n_lat, i // per_batch, b), 0, 0)
        srcs = [xl.reshape(b * L, d), xc.reshape(-1, d)]
        src_specs = [pl.BlockSpec((tm, d), lambda i: (jnp.minimum(i, n_lat - 1), 0)),
                     pl.BlockSpec((tm, d), lambda i: (jnp.maximum(i - n_lat, 0), 0))]
    else:
        shift, scale = mod_l
        mod_map = lambda i: (i // per_batch, 0, 0)
        srcs = [xl.reshape(b * L, d)]
        src_specs = [pl.BlockSpec((tm, d), lambda i: (i, 0))]
    n_tok = (n_lat + n_ctx) * tm
    tri = jnp.asarray(np.tril(np.ones((tm, tm)), -1), BF16)
    return pl.pallas_call(
        functools.partial(_norm_router_kernel, n_lat=n_lat, two_src=two_src),
        out_shape=(jax.ShapeDtypeStruct((n_tok, d), F32), jax.ShapeDtypeStruct((n_tok, ROUTER_COLS), F32),
                   jax.ShapeDtypeStruct((SUBLANE, ROUTER_COLS), F32)),
        grid=(n_lat + n_ctx,),
        in_specs=src_specs + [pl.BlockSpec((1, d), const), pl.BlockSpec((1, 1, d), mod_map),
                              pl.BlockSpec((1, 1, d), mod_map), pl.BlockSpec((d, ROUTER_COLS), const),
                              pl.BlockSpec((d, ROUTER_COLS), const), pl.BlockSpec((1, ROUTER_COLS), const),
                              pl.BlockSpec((tm, tm), const)],
        out_specs=(pl.BlockSpec((tm, d), lambda i: (i, 0)), pl.BlockSpec((tm, ROUTER_COLS), lambda i: (i, 0)),
                   pl.BlockSpec((SUBLANE, ROUTER_COLS), const)),
        compiler_params=_params("arbitrary"),
        name="moe_norm_router",
    )(*srcs, nw.reshape(1, d), shift, scale, w_hi, w_lo, b_router, tri)


def _expert_ffn_kernel(te_ref, tv_ref, x_ref, rw_ref, wg_ref, wu_ref, wd_ref, o_ref):
    i = pl.program_id(0)

    @pl.when(tv_ref[i] > 0)
    def _():
        x = x_ref[...].astype(BF16)
        g = jnp.dot(x, wg_ref[0].astype(BF16), preferred_element_type=F32)
        u = jnp.dot(x, wu_ref[0].astype(BF16), preferred_element_type=F32)
        hid = _silu(g) * u * rw_ref[...]
        o_ref[...] = jnp.dot(hid.astype(BF16), wd_ref[0].astype(BF16), preferred_element_type=F32)

    @pl.when(tv_ref[i] == 0)
    def _():
        o_ref[...] = jnp.zeros_like(o_ref)


def expert_ffn(x_sorted, row_w, tile_expert, tile_valid, w_gate, w_up, w_down, tm):
    r, d = x_sorted.shape
    f = w_gate.shape[-1]
    grid_spec = pltpu.PrefetchScalarGridSpec(
        num_scalar_prefetch=2,
        grid=(r // tm,),
        in_specs=[pl.BlockSpec((tm, d), lambda i, te, tv: (i, 0)),
                  pl.BlockSpec((tm, 1), lambda i, te, tv: (i, 0)),
                  pl.BlockSpec((1, d, f), lambda i, te, tv: (te[i], 0, 0)),
                  pl.BlockSpec((1, d, f), lambda i, te, tv: (te[i], 0, 0)),
                  pl.BlockSpec((1, f, d), lambda i, te, tv: (te[i], 0, 0))],
        out_specs=pl.BlockSpec((tm, d), lambda i, te, tv: (i, 0)),
    )
    return pl.pallas_call(
        _expert_ffn_kernel,
        out_shape=jax.ShapeDtypeStruct((r, d), F32),
        grid_spec=grid_spec,
        compiler_params=_params("arbitrary"),
        name="moe_expert_ffn",
    )(tile_expert, tile_valid, x_sorted, row_w, w_gate, w_up, w_down)


def moe_apply(h_tokens, routed, counts_row, w_gate, w_up, w_down, layer, n_lat, tm=256):
    t, d = h_tokens.shape
    e_idx, e_w, rank = routed[:, 0:2].astype(jnp.int32), routed[:, 2:4], routed[:, 4:6].astype(jnp.int32)
    counts = counts_row[0, N_GROUPS:N_GROUPS + N_EXPERTS].astype(jnp.int32)
    n_pairs = 2 * t
    padded = (counts + tm - 1) // tm * tm
    pad_end = jnp.cumsum(padded)
    pad_start = pad_end - padded
    n_rows = n_pairs + N_EXPERTS * tm
    tile_start = jnp.arange(n_rows // tm, dtype=jnp.int32) * tm
    tile_expert = jnp.minimum(jnp.sum((pad_end[None, :] <= tile_start[:, None]).astype(jnp.int32), axis=1),
                              N_EXPERTS - 1)
    tile_valid = (tile_start < pad_end[-1]).astype(jnp.int32)
    onehot = (e_idx[:, :, None] == jnp.arange(N_EXPERTS, dtype=jnp.int32)).astype(jnp.int32)
    pos = jnp.sum(onehot * pad_start, axis=-1) + rank
    j = jnp.arange(tm, dtype=jnp.int32)[None, :]
    fill_key = jnp.where(j < (padded - counts)[:, None], (pad_start + counts)[:, None] + j, n_rows)
    keys = jnp.concatenate([pos.reshape(-1), fill_key.reshape(-1)])
    toks = jnp.concatenate([jnp.arange(n_pairs, dtype=jnp.int32) // 2, jnp.arange(N_EXPERTS * tm, dtype=jnp.int32) % t])
    wts = jnp.concatenate([e_w.reshape(-1), jnp.zeros((N_EXPERTS * tm,), F32)])
    _, row_token, row_w = lax.sort((keys, toks, wts), num_keys=1)
    x_sorted = h_tokens[row_token]
    wg = w_gate.reshape(-1, d, EXPERT_HIDDEN)
    wu = w_up.reshape(-1, d, EXPERT_HIDDEN)
    wd = w_down.reshape(-1, EXPERT_HIDDEN, d)
    y_sorted = expert_ffn(x_sorted, row_w[:, None], tile_expert + layer * N_EXPERTS, tile_valid, wg, wu, wd, tm)
    lat = (y_sorted[pos[:n_lat, 0]], y_sorted[pos[:n_lat, 1]])
    rest = (y_sorted[pos[n_lat:, 0]], y_sorted[pos[n_lat:, 1]]) if t > n_lat else None
    return lat, rest


def _final_kernel(x_ref, ya_ref, yb_ref, g_ref, w_ref, o_ref):
    x = x_ref[0] + g_ref[0] * (ya_ref[0] + yb_ref[0])
    o_ref[0] = x * lax.rsqrt(jnp.mean(x * x, axis=-1, keepdims=True) + EPS) * w_ref[...]


def final_norm(x, ya, yb, gate, w, tm=512):
    b, L, d = x.shape
    tok = pl.BlockSpec((1, tm, d), lambda bi, i: (bi, i, 0))
    return pl.pallas_call(
        _final_kernel,
        out_shape=jax.ShapeDtypeStruct((b, L, d), F32),
        grid=(b, L // tm),
        in_specs=[tok, tok, tok, pl.BlockSpec((1, 1, d), _mod_map(gate, b)),
                  pl.BlockSpec((1, d), lambda bi, i: (0, 0))],
        out_specs=tok,
        compiler_params=_params("parallel", "arbitrary"),
        name="final_rmsnorm",
    )(x, ya, yb, gate, w.reshape(1, d))


def _regroup_in_weight(w_in):
    sizes = (SSD_WIDTH, SSD_CONV_CH, 2 * SSD_HEADS, HY_COLS, 2 * ML_WIDTH, ML_WIDTH, ML_WIDTH, 4 * ML_HEADS)
    parts, s = [], 0
    for n in sizes:
        parts.append(jnp.pad(w_in[:, s:s + n], ((0, 0), (0, -n % LANE))))
        s += n
    return jnp.concatenate(parts, axis=1).astype(BF16)


def kernel(x, c, ctx, c_ctx, w_mod, b_mod, norm1_w, norm2_w, w_in, w_out, ssd_conv_w, ssd_conv_b, ssd_dt_bias, ssd_a_log, ssd_d, ssd_norm_w, hy_conv_w, hy_conv_b, hy_pos_w1, hy_pos_b1, hy_pos_w2, hy_pos_b2, hy_pos_w3, hy_freq, hy_decay, hy_skip, ml_conv_w, ml_conv_b, ml_gate_b, ml_norm_w, grp_router_w, grp_router_b, exp_router_w, exp_router_b, moe_w_gate, moe_w_up, moe_w_down, final_norm_w):
    layer_params = dict(
        ssd_conv_w=ssd_conv_w, ssd_conv_b=ssd_conv_b, ssd_dt_bias=ssd_dt_bias, ssd_a_log=ssd_a_log,
        ssd_d=ssd_d, ssd_norm_w=ssd_norm_w, hy_conv_w=hy_conv_w, hy_conv_b=hy_conv_b,
        hy_pos_w1=hy_pos_w1, hy_pos_b1=hy_pos_b1, hy_pos_w2=hy_pos_w2, hy_pos_b2=hy_pos_b2,
        hy_pos_w3=hy_pos_w3, hy_freq=hy_freq, hy_decay=hy_decay, hy_skip=hy_skip,
        ml_conv_w=ml_conv_w, ml_conv_b=ml_conv_b, ml_gate_b=ml_gate_b, ml_norm_w=ml_norm_w)
    bsz, seq, d = x.shape
    n_ctx = ctx.shape[1]
    xl, xc = x, ctx
    moe_l = moe_c = None
    ssd0 = jnp.zeros((bsz, SSD_HEADS, HEAD_DIM, SSD_STATE), F32)
    ml0 = (jnp.zeros((bsz, ML_HEADS, HEAD_DIM, LANE), F32), jnp.zeros((bsz, SUBLANE, LANE), F32))
    c_rows = jnp.concatenate([c, c_ctx[None, :], jnp.zeros((SUBLANE - bsz - 1, d), F32)], axis=0)
    for i in range(DEPTH):
        last = i == DEPTH - 1
        p = {name: arr[i] for name, arr in layer_params.items()}
        sp, mp = ssd_prepare(p), ml_prepare(p)
        mod = modulation(c_rows, w_mod[i], b_mod[i]).reshape(SUBLANE, N_MOD, 1, d)
        mod_l = [mod[:bsz, k] for k in range(N_MOD)]
        mod_c = [mod[bsz:bsz + 1, k] for k in range(N_MOD)]
        w_in_p = _regroup_in_weight(w_in[i])
        w_out_b = w_out[i].astype(BF16)
        w_router = jnp.pad(jnp.concatenate([grp_router_w[i], exp_router_w[i]], axis=1),
                           ((0, 0), (0, ROUTER_COLS - N_GROUPS - N_EXPERTS)))
        b_router = jnp.pad(jnp.concatenate([grp_router_b[i], exp_router_b[i]]),
                           (0, ROUTER_COLS - N_GROUPS - N_EXPERTS)).reshape(1, ROUTER_COLS)

        uc_ssd, uc_hy, uc_ml, xc = norm_proj(xc, moe_c, norm1_w[i], mod_c[0], mod_c[1], w_in_p)
        yc_ssd, ssd_f, ssd_b = ssd_mixer(uc_ssd, sp, ssd0, ssd0, not last)
        yc_ml, ml_f, ml_b = ml_mixer(uc_ml, mp, ml0, ml0, not last)
        col_major = i % 2 == 1
        ul_ssd, ul_hy, ul_ml, xl = norm_proj(xl, moe_l, norm1_w[i], mod_l[0], mod_l[1], w_in_p, col_major)
        yl_ssd, _, _ = ssd_mixer(ul_ssd, sp, ssd_f, ssd_b, True)
        yl_ml, _, _ = ml_mixer(ul_ml, mp, ml_f, ml_b, True)
        h_spec = hyena_filter_spectra(*hyena_filter_signals(seq, p))
        yl_hy = hyena_long(ul_hy, h_spec, p['hy_conv_w'], p['hy_conv_b'], p['hy_skip'])
        xl = out_proj(yl_ssd, yl_hy, yl_ml, xl, mod_l[2], w_out_b, col_major)
        if not last:
            sig_c, inorm_c = hyena_filter_signals(n_ctx, p)
            yc_hy = hyena_short(uc_hy, sig_c, inorm_c, p['hy_conv_w'], p['hy_conv_b'], p['hy_skip'])
            xc = out_proj(yc_ssd, yc_hy, yc_ml, xc, mod_c[2], w_out_b)
        h_all, routed, counts_row = norm_router(xl, None if last else xc, norm2_w[i], (mod_l[3], mod_l[4]), (mod_c[3], mod_c[4]),
                                    w_router, b_router)
        lat, rest = moe_apply(h_all, routed, counts_row, moe_w_gate, moe_w_up, moe_w_down, i, bsz * seq)
        moe_l = (lat[0].reshape(bsz, seq, d), lat[1].reshape(bsz, seq, d), mod_l[5])
        if not last:
            moe_c = (rest[0].reshape(bsz, n_ctx, d), rest[1].reshape(bsz, n_ctx, d), mod_c[5])
    return final_norm(xl, *moe_l, final_norm_w)
```

```python
import functools
import math

import jax
import jax.numpy as jnp
import numpy as np
from jax import lax
from jax.experimental import pallas as pl
from jax.experimental.pallas import tpu as pltpu

D_MODEL = 1024
DEPTH = 2
GRID_W = 64
HEAD_DIM = 64
SSD_WIDTH = 384
SSD_HEADS = SSD_WIDTH // HEAD_DIM
SSD_GROUPS = 2
SSD_STATE = 64
HYENA_WIDTH = 256
HYENA_ORDER = 2
HYENA_BANDS = 16
ML_WIDTH = 384
ML_HEADS = ML_WIDTH // HEAD_DIM
N_GROUPS = 4
EXPERTS_PER_GROUP = 8
N_EXPERTS = N_GROUPS * EXPERTS_PER_GROUP
EXPERT_HIDDEN = 256
N_MOD = 6
EPS = 1e-6

LANE = 128
SUBLANE = 8
VMEM_LIMIT = 48 * 1024 * 1024
VMEM_LIMIT_HY = 56 * 1024 * 1024

SSD_CONV_CH = SSD_WIDTH + 2 * SSD_GROUPS * SSD_STATE
SSD_XBC0 = SSD_WIDTH
SSD_DT0 = SSD_XBC0 + SSD_CONV_CH
SSD_COLS = SSD_DT0 + LANE
HY_COLS = (HYENA_ORDER + 1) * HYENA_WIDTH
ML_V0 = 2 * ML_WIDTH
ML_O0 = ML_V0 + ML_WIDTH
ML_G0 = ML_O0 + ML_WIDTH
ML_COLS = ML_G0 + LANE
ROUTER_COLS = LANE

F32 = jnp.float32
BF16 = jnp.bfloat16
HI = lax.Precision.HIGHEST


def _params(*sem):
    return pltpu.CompilerParams(dimension_semantics=sem, vmem_limit_bytes=VMEM_LIMIT)


def _host_bf16(a):
    return jnp.asarray(np.asarray(a, np.float32).astype(BF16))


def _silu(x):
    return x * jax.nn.sigmoid(x)


def _softplus(x):
    return jnp.maximum(x, 0.0) + jnp.log(1.0 + jnp.exp(-jnp.abs(x)))


def _log_sigmoid(x):
    return jnp.minimum(x, 0.0) - jnp.log(1.0 + jnp.exp(-jnp.abs(x)))


def _mod_kernel(c_ref, w_ref, b_ref, o_ref):
    o_ref[...] = jnp.dot(_silu(c_ref[...]), w_ref[...], preferred_element_type=F32, precision=HI) + b_ref[...]


def modulation(c_rows, w_mod, b_mod, layer):
    depth, d, n = w_mod.shape
    tn = 1536
    return pl.pallas_call(
        _mod_kernel,
        out_shape=jax.ShapeDtypeStruct((c_rows.shape[0], n), F32),
        grid=(n // tn,),
        in_specs=[pl.BlockSpec(c_rows.shape, lambda j: (0, 0)),
                  pl.BlockSpec((None, d, tn), lambda j: (layer, 0, j)),
                  pl.BlockSpec((None, 1, tn), lambda j: (layer, 0, j))],
        out_specs=pl.BlockSpec((c_rows.shape[0], tn), lambda j: (0, j)),
        compiler_params=_params("arbitrary"),
        name="adaln_modulation",
    )(c_rows, w_mod, b_mod.reshape(depth, 1, n))


def _modnorm(x, nw, shift, scale):
    y = x * lax.rsqrt(jnp.mean(x * x, axis=-1, keepdims=True) + EPS) * nw
    return y * (1.0 + scale) + shift


def _mod_map(mod, b):
    return (lambda bi, i: (bi, 0, 0)) if mod.shape[0] == b else (lambda bi, i: (0, 0, 0))


def _tok_view(x, col_major):
    b, L, d = x.shape
    return x.reshape(b, L // GRID_W, GRID_W, d) if col_major else x


def _tok_spec(L, d, tm, col_major):
    if col_major:
        assert tm == (L // GRID_W) * SUBLANE
        return pl.BlockSpec((1, L // GRID_W, SUBLANE, d), lambda bi, i: (bi, 0, i, 0))
    return pl.BlockSpec((1, tm, d), lambda bi, i: (bi, i, 0))


def _tok_load(ref, col_major):
    if not col_major:
        return ref[0]
    return jnp.concatenate([ref[0, :, j, :] for j in range(ref.shape[2])], axis=0)


def _tok_store(ref, val, col_major):
    if not col_major:
        ref[0] = val
        return
    rows = ref.shape[1]
    for j in range(ref.shape[2]):
        ref[0, :, j, :] = val[j * rows:(j + 1) * rows]


def _norm_proj_kernel(*refs, col_major, fuse_moe):
    if fuse_moe:
        x_ref, ya_ref, yb_ref, g_ref, nw_ref, sh_ref, sc_ref, w_ref, ssd_ref, hy_ref, ml_ref, xo_ref = refs
    else:
        x_ref, nw_ref, sh_ref, sc_ref, w_ref, ssd_ref, hy_ref, ml_ref = refs
    x = _tok_load(x_ref, col_major)
    if fuse_moe:
        x = x + g_ref[0] * (_tok_load(ya_ref, col_major) + _tok_load(yb_ref, col_major))
        _tok_store(xo_ref, x, col_major)
    h = _modnorm(x, nw_ref[...], sh_ref[0], sc_ref[0])
    u = jnp.dot(h.astype(BF16), w_ref[...], preferred_element_type=F32)
    ssd_ref[0] = u[:, 0:SSD_COLS]
    hy_ref[0] = u[:, SSD_COLS:SSD_COLS + HY_COLS]
    ml_ref[0] = u[:, SSD_COLS + HY_COLS:]


def norm_proj(x, moe, nw, shift, scale, w_all, layer, col_major=False, tm=512):
    b, L, d = x.shape
    n = w_all.shape[2]
    fuse_moe = moe is not None
    tm = (L // GRID_W) * SUBLANE if col_major else min(tm, L)
    tok = _tok_spec(L, d, tm, col_major)
    row = lambda bi, i: (bi, i, 0)
    const2 = lambda bi, i: (0, 0)
    args, in_specs = [_tok_view(x, col_major)], [tok]
    if fuse_moe:
        ya, yb, gate = moe
        args += [_tok_view(ya, col_major), _tok_view(yb, col_major), gate]
        in_specs += [tok, tok, pl.BlockSpec((1, 1, d), _mod_map(gate, b))]
    args += [nw.reshape(1, d), shift, scale, w_all]
    in_specs += [pl.BlockSpec((1, d), const2), pl.BlockSpec((1, 1, d), _mod_map(shift, b)),
                 pl.BlockSpec((1, 1, d), _mod_map(scale, b)),
                 pl.BlockSpec((None, d, n), lambda bi, i: (layer, 0, 0))]
    out_shape = [jax.ShapeDtypeStruct((b, L, SSD_COLS), F32), jax.ShapeDtypeStruct((b, L, HY_COLS), F32),
                 jax.ShapeDtypeStruct((b, L, ML_COLS), F32)]
    out_specs = [pl.BlockSpec((1, tm, SSD_COLS), row), pl.BlockSpec((1, tm, HY_COLS), row),
                 pl.BlockSpec((1, tm, ML_COLS), row)]
    if fuse_moe:
        out_shape.append(jax.ShapeDtypeStruct(args[0].shape, F32))
        out_specs.append(tok)
    outs = pl.pallas_call(
        functools.partial(_norm_proj_kernel, col_major=col_major, fuse_moe=fuse_moe),
        out_shape=tuple(out_shape),
        grid=(b, L // tm),
        in_specs=in_specs,
        out_specs=tuple(out_specs),
        compiler_params=_params("parallel", "arbitrary"),
        name="norm_in_proj",
    )(*args)
    return (*outs[:3], outs[3].reshape(b, L, d) if fuse_moe else x)


def _out_proj_kernel(ys_ref, yh_ref, ym_ref, x_ref, g_ref, w_ref, o_ref, *, col_major):
    y = jnp.concatenate([ys_ref[0], yh_ref[0], ym_ref[0]], axis=-1).astype(BF16)
    r = _tok_load(x_ref, col_major) + g_ref[0] * jnp.dot(y, w_ref[...], preferred_element_type=F32)
    _tok_store(o_ref, r, col_major)


def out_proj(y_ssd, y_hy, y_ml, x, gate, w_bf16, col_major=False, tm=512):
    b, L, d = x.shape
    tm = (L // GRID_W) * SUBLANE if col_major else min(tm, L)
    row = lambda bi, i: (bi, i, 0)
    tok = _tok_spec(L, d, tm, col_major)
    xv = _tok_view(x, col_major)
    return pl.pallas_call(
        functools.partial(_out_proj_kernel, col_major=col_major),
        out_shape=jax.ShapeDtypeStruct(xv.shape, F32),
        grid=(b, L // tm),
        in_specs=[pl.BlockSpec((1, tm, SSD_WIDTH), row), pl.BlockSpec((1, tm, HYENA_WIDTH), row),
                  pl.BlockSpec((1, tm, ML_WIDTH), row), tok,
                  pl.BlockSpec((1, 1, d), _mod_map(gate, b)),
                  pl.BlockSpec(w_bf16.shape, lambda bi, i: (0, 0))],
        out_specs=tok,
        compiler_params=_params("parallel", "arbitrary"),
        name="out_proj_residual",
    )(y_ssd, y_hy, y_ml, xv, gate, w_bf16).reshape(b, L, d)


def _conv3(xr, prev_row, next_row, cw, cb, q):
    rid = lax.broadcasted_iota(jnp.int32, (q, 1), 0)
    x_prev = jnp.where(rid == 0, prev_row, pltpu.roll(xr, 1, axis=0))
    x_next = jnp.where(rid == q - 1, next_row, pltpu.roll(xr, q - 1, axis=0))
    return x_prev * cw[0:1] + xr * cw[1:2] + x_next * cw[2:3] + cb


def _scan_mask(q, direction):
    li = lax.broadcasted_iota(jnp.int32, (q, q), 0)
    si = lax.broadcasted_iota(jnp.int32, (q, q), 1)
    return (si <= li) if direction == 0 else (si >= li)


def _running_max(x, direction, q):
    rid = lax.broadcasted_iota(jnp.int32, (q, 1), 0)
    s = 1
    while s < q:
        if direction == 0:
            x = jnp.where(rid >= s, jnp.maximum(x, pltpu.roll(x, s, axis=0)), x)
        else:
            x = jnp.where(rid < q - s, jnp.maximum(x, pltpu.roll(x, q - s, axis=0)), x)
        s *= 2
    return x


def _scan_specs(L, q, nc, cols, direction):
    hb = q // SUBLANE
    nrb = L // SUBLANE
    cidx = (lambda j: j) if direction == 0 else (lambda j: nc - 1 - j)
    specs = [pl.BlockSpec((1, q, cols), lambda bi, j: (bi, cidx(j), 0)),
             pl.BlockSpec((1, SUBLANE, cols), lambda bi, j: (bi, jnp.maximum(cidx(j) * hb - 1, 0), 0)),
             pl.BlockSpec((1, SUBLANE, cols), lambda bi, j: (bi, jnp.minimum((cidx(j) + 1) * hb, nrb - 1), 0))]
    return specs, cidx


def _ssd_kernel(*refs, direction, finalize, q, nc):
    if finalize:
        (u_ref, prev_ref, next_ref, yb_ref, init_ref, cw_ref, cb_ref, dtb_ref, a_ref, d_ref, nw_ref,
         y_ref, fin_ref, state_ref) = refs
    else:
        (u_ref, prev_ref, next_ref, init_ref, cw_ref, cb_ref, dtb_ref, a_ref,
         y_ref, fin_ref, state_ref) = refs
    j = pl.program_id(1)
    c = j if direction == 0 else nc - 1 - j

    @pl.when(j == 0)
    def _():
        state_ref[...] = init_ref[0]

    prev_row = jnp.where(c > 0, prev_ref[0, SUBLANE - 1:SUBLANE, SSD_XBC0:SSD_DT0], 0.0)
    next_row = jnp.where(c < nc - 1, next_ref[0, 0:1, SSD_XBC0:SSD_DT0], 0.0)
    xc = _silu(_conv3(u_ref[0, :, SSD_XBC0:SSD_DT0], prev_row, next_row, cw_ref[...], cb_ref[...], q))

    dt = _softplus(u_ref[0, :, SSD_DT0:SSD_COLS] + dtb_ref[...])
    mask = _scan_mask(q, direction)
    cum = jnp.dot(mask.astype(F32), dt * a_ref[...], preferred_element_type=F32, precision=HI)
    cum_t = cum.T
    end = q - 1 if direction == 0 else 0

    ys = []
    for g in range(SSD_GROUPS):
        b0 = SSD_WIDTH + g * SSD_STATE
        c0 = SSD_WIDTH + (SSD_GROUPS + g) * SSD_STATE
        bm_t = xc[:, b0:b0 + SSD_STATE].T
        cm = xc[:, c0:c0 + SSD_STATE].astype(BF16)
        scores = jnp.dot(cm, bm_t.astype(BF16), preferred_element_type=F32)
        for h in range(g * (SSD_HEADS // SSD_GROUPS), (g + 1) * (SSD_HEADS // SSD_GROUPS)):
            hl = direction * SSD_HEADS + h
            col = cum[:, hl:hl + 1]
            row = cum_t[hl:hl + 1, :]
            seg = jnp.exp(jnp.where(mask, col - row, -jnp.inf))
            xdt = (xc[:, h * HEAD_DIM:(h + 1) * HEAD_DIM] * dt[:, hl:hl + 1]).astype(BF16)
            y = jnp.dot((scores * seg).astype(BF16), xdt, preferred_element_type=F32)
            st = state_ref[h]
            y = y + jnp.dot(cm, st.astype(BF16), preferred_element_type=F32) * jnp.exp(col)
            tot = cum[end:end + 1, hl:hl + 1]
            upd = jnp.dot((bm_t * jnp.exp(tot - row)).astype(BF16), xdt, preferred_element_type=F32)
            state_ref[h] = st * jnp.exp(tot) + upd
            ys.append(y)
    y_all = jnp.concatenate(ys, axis=-1)
    if finalize:
        t = (y_all + yb_ref[0] + xc[:, 0:SSD_WIDTH] * d_ref[...]) * _silu(u_ref[0, :, 0:SSD_WIDTH])
        y_all = t * lax.rsqrt(jnp.mean(t * t, axis=-1, keepdims=True) + EPS) * nw_ref[...]
    y_ref[0] = y_all

    @pl.when(j == nc - 1)
    def _():
        fin_ref[0] = state_ref[...]


def ssd_pass(u, y_other, init, sp, direction, q):
    b, L, _ = u.shape
    q = min(q, L)
    nc = L // q
    finalize = y_other is not None
    in_specs, cidx = _scan_specs(L, q, nc, SSD_COLS, direction)
    const2 = lambda bi, j: (0, 0)
    st_spec = pl.BlockSpec((1, SSD_HEADS, HEAD_DIM, SSD_STATE), lambda bi, j: (bi, 0, 0, 0))
    y_spec = pl.BlockSpec((1, q, SSD_WIDTH), lambda bi, j: (bi, cidx(j), 0))
    args = [u, u, u]
    if finalize:
        in_specs.append(y_spec)
        args.append(y_other)
    consts = [sp['cw'], sp['cb'], sp['dtb'], sp['a']] + ([sp['d'], sp['nw']] if finalize else [])
    in_specs += [st_spec] + [pl.BlockSpec(t.shape, const2) for t in consts]
    args += [init] + consts
    return pl.pallas_call(
        functools.partial(_ssd_kernel, direction=direction, finalize=finalize, q=q, nc=nc),
        out_shape=(jax.ShapeDtypeStruct((b, L, SSD_WIDTH), F32),
                   jax.ShapeDtypeStruct((b, SSD_HEADS, HEAD_DIM, SSD_STATE), F32)),
        grid=(b, nc),
        in_specs=in_specs,
        out_specs=(y_spec, st_spec),
        scratch_shapes=[pltpu.VMEM((SSD_HEADS, HEAD_DIM, SSD_STATE), F32)],
        compiler_params=_params("parallel", "arbitrary"),
        name="ssd_scan_%s" % ("fwd" if direction == 0 else "bwd"),
    )(*args)


def ssd_prepare(p):
    pad = lambda v: jnp.pad(v.reshape(1, -1), ((0, 0), (0, LANE - v.size)))
    return dict(cw=p['ssd_conv_w'], cb=p['ssd_conv_b'].reshape(1, -1),
                dtb=pad(p['ssd_dt_bias']), a=pad(-jnp.exp(p['ssd_a_log'])),
                d=jnp.repeat(p['ssd_d'], HEAD_DIM).reshape(1, -1), nw=p['ssd_norm_w'].reshape(1, -1))


def ssd_mixer(u, sp, init_f, init_b, want_y, q=256):
    yb, fin_b = ssd_pass(u, None, init_b, sp, 1, q)
    y, fin_f = ssd_pass(u, yb if want_y else None, init_f, sp, 0, q)
    return y, fin_f, fin_b


def _ml_kernel(*refs, direction, finalize, q, nc):
    if finalize:
        (u_ref, prev_ref, next_ref, hb_ref, s_init_ref, m_init_ref, cw_ref, cb_ref, gb_ref, nw_ref, pool_ref,
         y_ref, s_fin_ref, m_fin_ref, s_ref, m_ref) = refs
    else:
        (u_ref, prev_ref, next_ref, s_init_ref, m_init_ref, cw_ref, cb_ref, gb_ref,
         y_ref, s_fin_ref, m_fin_ref, s_ref, m_ref) = refs
    j = pl.program_id(1)
    c = j if direction == 0 else nc - 1 - j

    @pl.when(j == 0)
    def _():
        s_ref[...] = s_init_ref[0]
        m_ref[...] = m_init_ref[0]

    prev_row = jnp.where(c > 0, prev_ref[0, SUBLANE - 1:SUBLANE, 0:ML_V0], 0.0)
    next_row = jnp.where(c < nc - 1, next_ref[0, 0:1, 0:ML_V0], 0.0)
    qk = _silu(_conv3(u_ref[0, :, 0:ML_V0], prev_row, next_row, cw_ref[...], cb_ref[...], q))
    v = u_ref[0, :, ML_V0:ML_O0]

    gb = u_ref[0, :, ML_G0:ML_COLS] + gb_ref[...]
    mask = _scan_mask(q, direction)
    cum = jnp.dot(mask.astype(F32), _log_sigmoid(gb), preferred_element_type=F32, precision=HI)
    ig = pltpu.roll(gb, ML_HEADS, axis=1)
    end = q - 1 if direction == 0 else 0
    m_prev = m_ref[0:1, :]
    tot = cum[end:end + 1, :]
    w_end = tot - cum + ig
    m_loc = jnp.max(w_end, axis=0, keepdims=True)
    e_end = jnp.exp(w_end - m_loc)
    m_new = jnp.maximum(tot + m_prev, m_loc)
    a_prev = jnp.exp(tot + m_prev - m_new)
    a_loc = jnp.exp(m_loc - m_new)
    inter = cum + m_prev
    rel = ig - cum
    m_t = jnp.maximum(inter, cum + _running_max(rel, direction, q))
    col_a = cum - m_t
    a_inter = jnp.exp(inter - m_t)
    floor = jnp.exp(-m_t)
    rel_t = rel.T
    e_end_t = e_end.T
    k_t = (qk[:, ML_WIDTH:2 * ML_WIDTH] * (HEAD_DIM ** -0.5)).T
    one_col = (lax.broadcasted_iota(jnp.int32, (q, HEAD_DIM), 1) == 0).astype(F32)

    ys = []
    for h in range(ML_HEADS):
        fl = direction * 2 * ML_HEADS + ML_HEADS + h
        qh = qk[:, h * HEAD_DIM:(h + 1) * HEAD_DIM].astype(BF16)
        kh_t = k_t[h * HEAD_DIM:(h + 1) * HEAD_DIM, :]
        v_ext = jnp.concatenate([v[:, h * HEAD_DIM:(h + 1) * HEAD_DIM], one_col], axis=-1).astype(BF16)
        pw = jnp.exp(jnp.where(mask, col_a[:, fl:fl + 1] + rel_t[fl:fl + 1, :], -jnp.inf))
        scores = jnp.dot(qh, kh_t.astype(BF16), preferred_element_type=F32)
        nd = jnp.dot((scores * pw).astype(BF16), v_ext, preferred_element_type=F32)
        st = s_ref[h]
        nd = nd + a_inter[:, fl:fl + 1] * jnp.dot(qh, st.astype(BF16), preferred_element_type=F32)
        den = nd[:, HEAD_DIM:HEAD_DIM + 1]
        ys.append(nd[:, 0:HEAD_DIM] / jnp.maximum(jnp.abs(den), floor[:, fl:fl + 1]))
        upd = jnp.dot((kh_t * e_end_t[fl:fl + 1, :]).astype(BF16), v_ext, preferred_element_type=F32)
        s_ref[h] = a_prev[:, fl:fl + 1] * st + a_loc[:, fl:fl + 1] * upd
    m_ref[...] = jnp.broadcast_to(m_new, m_ref.shape)
    y_all = jnp.concatenate(ys, axis=-1)
    if finalize:
        hs = y_all + hb_ref[0]
        hc = hs - jnp.dot(hs.astype(BF16), pool_ref[...], preferred_element_type=F32)
        var = jnp.dot((hc * hc).astype(BF16), pool_ref[...], preferred_element_type=F32)
        y_all = hc * lax.rsqrt(var + EPS) * nw_ref[...] * jax.nn.sigmoid(u_ref[0, :, ML_O0:ML_G0])
    y_ref[0] = y_all

    @pl.when(j == nc - 1)
    def _():
        s_fin_ref[0] = s_ref[...]
        m_fin_ref[0] = m_ref[...]


def ml_pass(u, h_other, init, mp, direction, q):
    b, L, _ = u.shape
    q = min(q, L)
    nc = L // q
    finalize = h_other is not None
    in_specs, cidx = _scan_specs(L, q, nc, ML_COLS, direction)
    const2 = lambda bi, j: (0, 0)
    s_spec = pl.BlockSpec((1, ML_HEADS, HEAD_DIM, LANE), lambda bi, j: (bi, 0, 0, 0))
    m_spec = pl.BlockSpec((1, SUBLANE, LANE), lambda bi, j: (bi, 0, 0))
    y_spec = pl.BlockSpec((1, q, ML_WIDTH), lambda bi, j: (bi, cidx(j), 0))
    args = [u, u, u]
    if finalize:
        in_specs.append(y_spec)
        args.append(h_other)
    consts = [mp['cw'], mp['cb'], mp['gb']] + ([mp['nw'], mp['pool']] if finalize else [])
    in_specs += [s_spec, m_spec] + [pl.BlockSpec(t.shape, const2) for t in consts]
    args += [init[0], init[1]] + consts
    y, s_fin, m_fin = pl.pallas_call(
        functools.partial(_ml_kernel, direction=direction, finalize=finalize, q=q, nc=nc),
        out_shape=(jax.ShapeDtypeStruct((b, L, ML_WIDTH), F32),
                   jax.ShapeDtypeStruct((b, ML_HEADS, HEAD_DIM, LANE), F32),
                   jax.ShapeDtypeStruct((b, SUBLANE, LANE), F32)),
        grid=(b, nc),
        in_specs=in_specs,
        out_specs=(y_spec, s_spec, m_spec),
        scratch_shapes=[pltpu.VMEM((ML_HEADS, HEAD_DIM, LANE), F32), pltpu.VMEM((SUBLANE, LANE), F32)],
        compiler_params=_params("parallel", "arbitrary"),
        name="mlstm_scan_%s" % ("fwd" if direction == 0 else "bwd"),
    )(*args)
    return y, (s_fin, m_fin)


def ml_prepare(p):
    gb = p['ml_gate_b'].reshape(1, -1)
    head = np.arange(ML_WIDTH) // HEAD_DIM
    pool = _host_bf16((head[:, None] == head[None, :]) / HEAD_DIM)
    return dict(cw=p['ml_conv_w'], cb=p['ml_conv_b'].reshape(1, -1),
                gb=jnp.pad(gb, ((0, 0), (0, LANE - gb.shape[1]))), nw=p['ml_norm_w'].reshape(1, -1), pool=pool)


def ml_mixer(u, mp, init_f, init_b, want_y, q=256):
    hb, fin_b = ml_pass(u, None, init_b, mp, 1, q)
    y, fin_f = ml_pass(u, hb if want_y else None, init_f, mp, 0, q)
    return y, fin_f, fin_b


FFT_L = 4096
FFT_N = 2 * FFT_L
FFT_N2 = 128
FFT_N1 = FFT_N // FFT_N2
FFT_N1H = FFT_L // FFT_N2
FFT_K1 = FFT_N1 // 2 + 1
FFT_R = 80
FFT_UNROLL = 8
FFT_PITCH = FFT_N2 + SUBLANE


def _fft_tables():
    n2 = np.arange(FFT_N2)[:, None, None]
    k1 = np.arange(FFT_K1)[None, :, None]
    n1 = np.arange(FFT_N1H)[None, None, :]
    th = 2 * np.pi * (((FFT_N2 * n1 + n2) * k1) % FFT_N) / FFT_N
    f1 = np.zeros((FFT_N2, FFT_R, FFT_N1H))
    f1[:, 0:2 * FFT_K1:2, :] = np.cos(th)
    f1[:, 1:2 * FFT_K1:2, :] = -np.sin(th)
    wgt = np.where((np.arange(FFT_K1) == 0) | (np.arange(FFT_K1) == FFT_N1 // 2), 1.0, 2.0)[None, :, None] / FFT_N
    g1 = np.zeros((FFT_N2, FFT_N1H, FFT_R))
    g1[:, :, 0:2 * FFT_K1:2] = np.transpose(wgt * np.cos(th), (0, 2, 1))
    g1[:, :, 1:2 * FFT_K1:2] = np.transpose(-wgt * np.sin(th), (0, 2, 1))
    ph = 2 * np.pi * ((np.arange(FFT_N2)[:, None] * np.arange(FFT_N2)[None, :]) % FFT_N2) / FFT_N2
    c, s = np.cos(ph), np.sin(ph)
    f2 = np.block([[c, s], [-s, c]])
    sign = np.where(np.arange(FFT_R) // 2 % 2 == 0, 1.0, -1.0)[None, :, None]
    f1ab = np.concatenate([f1, f1 * sign], axis=2)
    return _host_bf16(f1), _host_bf16(f1ab), _host_bf16(f2), _host_bf16(f2.T), _host_bf16(g1)


def _sld(ref, n2, count):
    rows = pl.ds(n2, count, stride=FFT_PITCH)
    return jnp.concatenate([ref[0, rows, :], ref[1, rows, :]], axis=-1)


def _sst(ref, n2, count, val):
    rows = pl.ds(n2, count, stride=FFT_PITCH)
    ref[0, rows, :] = val[:, 0:LANE]
    ref[1, rows, :] = val[:, LANE:2 * LANE]


def _blk_ld(ref, blk, nblk):
    parts = []
    for k in range(nblk):
        rows = pl.ds(pl.multiple_of((blk + k) * FFT_PITCH, SUBLANE), FFT_N2)
        parts.append(jnp.concatenate([ref[0, rows, :], ref[1, rows, :]], axis=-1))
    return parts[0] if nblk == 1 else jnp.concatenate(parts, axis=0)


def _blk_st(ref, blk, nblk, val):
    for k in range(nblk):
        rows = pl.ds(pl.multiple_of((blk + k) * FFT_PITCH, SUBLANE), FFT_N2)
        ref[0, rows, :] = val[k * FFT_N2:(k + 1) * FFT_N2, 0:LANE]
        ref[1, rows, :] = val[k * FFT_N2:(k + 1) * FFT_N2, LANE:2 * LANE]


def _fft_stage1(z_ref, a_ref, f1_ref, n_in=FFT_N1H):
    def body(n2, carry):
        xs = _sld(z_ref, n2, n_in).astype(BF16)
        _sst(a_ref, n2, FFT_R, jnp.dot(f1_ref[n2], xs, preferred_element_type=F32))
        return carry
    lax.fori_loop(0, FFT_N2, body, 0, unroll=FFT_UNROLL)


def _spectrum_kernel(x_ref, inorm_ref, f1_ref, f2_ref, o_ref, z_ref, a_ref):
    for half in range(2):
        for n1 in range(FFT_N1H):
            _blk_st(z_ref, half * FFT_N1H + n1, 1, x_ref[half, n1 * FFT_N2:(n1 + 1) * FFT_N2, :])
    _fft_stage1(z_ref, a_ref, f1_ref, 2 * FFT_N1H)
    inorm = inorm_ref[0]

    def body(k1, carry):
        slab = _blk_ld(a_ref, 2 * k1, 2).astype(BF16)
        o_ref[0, k1] = (jnp.dot(f2_ref[...], slab, preferred_element_type=F32) * inorm).astype(BF16)
        return carry
    lax.fori_loop(0, FFT_K1, body, 0, unroll=3)


def hyena_filter_spectra(sig, inv_norm):
    _, L, c = sig.shape
    _, f1ab, f2, _, _ = _fft_tables()
    one = pl.Buffered(1)
    return pl.pallas_call(
        _spectrum_kernel,
        out_shape=jax.ShapeDtypeStruct((HYENA_ORDER, FFT_K1, 2 * FFT_N2, c), BF16),
        grid=(HYENA_ORDER,),
        in_specs=[pl.BlockSpec((2, L, c), lambda i: (i, 0, 0)),
                  pl.BlockSpec((1, 1, c), lambda i: (i, 0, 0)),
                  pl.BlockSpec(f1ab.shape, lambda i: (0, 0, 0), pipeline_mode=one),
                  pl.BlockSpec(f2.shape, lambda i: (0, 0), pipeline_mode=one)],
        out_specs=pl.BlockSpec((1, FFT_K1, 2 * FFT_N2, c), lambda i: (i, 0, 0, 0)),
        scratch_shapes=[pltpu.VMEM((2, 2 * FFT_N1H * FFT_PITCH, LANE), F32),
                        pltpu.VMEM((2, FFT_R * FFT_PITCH, LANE), F32)],
        compiler_params=pltpu.CompilerParams(dimension_semantics=("arbitrary",), vmem_limit_bytes=VMEM_LIMIT_HY),
        name="hyena_filter_spectrum",
    )(sig, inv_norm, f1ab, f2)


def _conv3_rows(src, dst, cw, cb, L):
    rows = 2 * FFT_N2
    nchunk = L // rows

    def body(i, carry):
        r0 = pl.multiple_of(i * rows, rows)
        prev_row = jnp.where(i > 0, src[pl.ds(jnp.maximum(r0 - 1, 0), 1), :], 0.0)
        next_row = jnp.where(i < nchunk - 1, src[pl.ds(jnp.minimum(r0 + rows, L - 1), 1), :], 0.0)
        _blk_st(dst, 2 * i, 2, _conv3(src[pl.ds(r0, rows), :], prev_row, next_row, cw, cb, rows))
        return carry
    lax.fori_loop(0, nchunk, body, 0, unroll=2)


def _hyena_kernel(v_ref, g_ref, h_ref, cwv_ref, cbv_ref, cwg_ref, cbg_ref, skip_ref,
                  f1_ref, f2_ref, f2t_ref, g1_ref, o_ref, z_ref, gc_ref, a_ref):
    order = pl.program_id(1)

    @pl.when(order == 0)
    def _():
        _conv3_rows(v_ref.at[0], z_ref, cwv_ref[...], cbv_ref[...], FFT_L)

    _conv3_rows(g_ref.at[0], gc_ref, cwg_ref[0], cbg_ref[0], FFT_L)
    _fft_stage1(z_ref, a_ref, f1_ref)

    def mid(k1, carry):
        x = jnp.dot(f2_ref[...], _blk_ld(a_ref, 2 * k1, 2).astype(BF16), preferred_element_type=F32)
        h = h_ref[0, k1].astype(F32)
        xr, xi, hr, hi = x[:FFT_N2], x[FFT_N2:], h[:FFT_N2], h[FFT_N2:]
        y = jnp.concatenate([xr * hr - xi * hi, xr * hi + xi * hr], axis=0).astype(BF16)
        _blk_st(a_ref, 2 * k1, 2, jnp.dot(f2t_ref[...], y, preferred_element_type=F32))
        return carry
    lax.fori_loop(0, FFT_K1, mid, 0, unroll=3)

    skip = skip_ref[0]

    def last(n2, carry):
        bs = _sld(a_ref, n2, FFT_R).astype(BF16)
        y = jnp.dot(g1_ref[n2], bs, preferred_element_type=F32)
        _sst(z_ref, n2, FFT_N1H, _sld(gc_ref, n2, FFT_N1H) * (y + skip * _sld(z_ref, n2, FFT_N1H)))
        return carry
    lax.fori_loop(0, FFT_N2, last, 0, unroll=FFT_UNROLL)

    @pl.when(order == 1)
    def _():
        for n1 in range(FFT_N1H):
            o_ref[0, n1 * FFT_N2:(n1 + 1) * FFT_N2, :] = _blk_ld(z_ref, n1, 1)


def hyena_long(u, h_spec, cw, cb, skip):
    b, L, _ = u.shape
    c = HYENA_WIDTH
    f1, _, f2, f2t, g1 = _fft_tables()
    one = pl.Buffered(1)
    cw3 = cw.reshape(3, 3, c).transpose(1, 0, 2)
    cb3 = cb.reshape(3, 1, c)
    return pl.pallas_call(
        _hyena_kernel,
        out_shape=jax.ShapeDtypeStruct((b, L, c), F32),
        grid=(b, 2),
        in_specs=[pl.BlockSpec((1, L, c), lambda bi, o: (bi, 0, 0), pipeline_mode=one),
                  pl.BlockSpec((1, L, c), lambda bi, o: (bi, 0, 1 + o)),
                  pl.BlockSpec((1, FFT_K1, 2 * FFT_N2, c), lambda bi, o: (o, 0, 0, 0)),
                  pl.BlockSpec((3, c), lambda bi, o: (0, 0)),
                  pl.BlockSpec((1, c), lambda bi, o: (0, 0)),
                  pl.BlockSpec((1, 3, c), lambda bi, o: (1 + o, 0, 0)),
                  pl.BlockSpec((1, 1, c), lambda bi, o: (1 + o, 0, 0)),
                  pl.BlockSpec((1, 1, c), lambda bi, o: (o, 0, 0)),
                  pl.BlockSpec(f1.shape, lambda bi, o: (0, 0, 0), pipeline_mode=one),
                  pl.BlockSpec(f2.shape, lambda bi, o: (0, 0), pipeline_mode=one),
                  pl.BlockSpec(f2t.shape, lambda bi, o: (0, 0), pipeline_mode=one),
                  pl.BlockSpec(g1.shape, lambda bi, o: (0, 0, 0), pipeline_mode=one)],
        out_specs=pl.BlockSpec((1, L, c), lambda bi, o: (bi, 0, 0)),
        scratch_shapes=[pltpu.VMEM((2, FFT_N1H * FFT_PITCH, LANE), F32),
                        pltpu.VMEM((2, FFT_N1H * FFT_PITCH, LANE), F32),
                        pltpu.VMEM((2, FFT_R * FFT_PITCH, LANE), F32)],
        compiler_params=pltpu.CompilerParams(dimension_semantics=("parallel", "arbitrary"),
                                             vmem_limit_bytes=VMEM_LIMIT_HY),
        name="hyena_long_conv",
    )(u, u, h_spec, cw3[0], cb3[0], cw3, cb3, skip.reshape(2, 1, c), f1, f2, f2t, g1)


HY_FILT = 2 * HYENA_ORDER * HYENA_WIDTH
HY_HALF = HYENA_ORDER * HYENA_WIDTH


def _filter_kernel(wt_ref, wc_ref, ws_ref, b1_ref, w2_ref, b2_ref, w3_ref, freq_ref, decay_ref,
                   sig_ref, asum_ref, *, L, rows):
    i = pl.program_id(0)
    n = (i * rows + lax.broadcasted_iota(jnp.int32, (rows, 1), 0)).astype(F32)
    band = lax.broadcasted_iota(jnp.int32, (1, HYENA_BANDS), 1).astype(F32)
    bands = 1e-4 + band * ((HYENA_BANDS - 1 - 1e-4) / (HYENA_BANDS - 1))
    freq = freq_ref[...]

    @pl.when(i == 0)
    def _():
        asum_ref[...] = jnp.zeros_like(asum_ref)

    for side, t in enumerate((n, L - n)):
        t_unit = t / float(max(L - 1, 1))
        ang = (2 * math.pi / L) * t * bands
        pre = (t_unit * wt_ref[...] + jnp.dot(jnp.cos(ang), wc_ref[...], preferred_element_type=F32, precision=HI)
               - jnp.dot(jnp.sin(ang), ws_ref[...], preferred_element_type=F32, precision=HI) + b1_ref[...])
        hdn = jnp.sin(freq * pre)
        hdn = jnp.sin(freq * (jnp.dot(hdn, w2_ref[...], preferred_element_type=F32, precision=HI) + b2_ref[...]))
        cols = slice(side * HY_HALF, (side + 1) * HY_HALF)
        val = jnp.dot(hdn, w3_ref[:, cols], preferred_element_type=F32, precision=HI)
        val = val * jnp.exp(-t_unit * jnp.abs(decay_ref[:, cols]))
        if side == 1:
            val = jnp.where(n > 0, val, 0.0)
        for o in range(HYENA_ORDER):
            sig_ref[2 * o + side] = val[:, o * HYENA_WIDTH:(o + 1) * HYENA_WIDTH]
        asum_ref[side:side + 1, :] += jnp.sum(jnp.abs(val), axis=0, keepdims=True)


def hyena_filter_signals(L, p):
    rows = min(L, 512)
    fh = p['hy_pos_w1'].shape[1]
    w1 = p['hy_pos_w1']
    consts = [w1[0:1], w1[1:1 + HYENA_BANDS], w1[1 + HYENA_BANDS:], p['hy_pos_b1'].reshape(1, fh), p['hy_pos_w2'],
              p['hy_pos_b2'].reshape(1, fh), p['hy_pos_w3'], p['hy_freq'].reshape(1, fh),
              p['hy_decay'].reshape(1, HY_FILT)]
    sig, asum = pl.pallas_call(
        functools.partial(_filter_kernel, L=L, rows=rows),
        out_shape=(jax.ShapeDtypeStruct((2 * HYENA_ORDER, L, HYENA_WIDTH), F32),
                   jax.ShapeDtypeStruct((SUBLANE, HY_HALF), F32)),
        grid=(L // rows,),
        in_specs=[pl.BlockSpec(c.shape, lambda i: (0, 0)) for c in consts],
        out_specs=(pl.BlockSpec((2 * HYENA_ORDER, rows, HYENA_WIDTH), lambda i: (0, i, 0)),
                   pl.BlockSpec((SUBLANE, HY_HALF), lambda i: (0, 0))),
        compiler_params=_params("arbitrary"),
        name="hyena_filters",
    )(*consts)
    inv_norm = 1.0 / (asum[0] + asum[1]).reshape(HYENA_ORDER, 1, HYENA_WIDTH)
    return sig, inv_norm


def _rdft_tables(L):
    n_bins = L + 1
    half = -(-n_bins // 16) * 16
    k = np.arange(n_bins)[:, None]
    n = np.arange(L)[None, :]
    th = 2 * np.pi * ((k * n) % (2 * L)) / (2 * L)
    f = np.zeros((2 * half, L))
    f[:n_bins] = np.cos(th)
    f[half:half + n_bins] = -np.sin(th)
    sign = np.where(np.arange(n_bins) % 2 == 0, 1.0, -1.0)[:, None]
    fs = np.zeros_like(f)
    fs[:n_bins] = f[:n_bins] * sign
    fs[half:half + n_bins] = f[half:half + n_bins] * sign
    wgt = np.where((np.arange(n_bins) == 0) | (np.arange(n_bins) == L), 1.0, 2.0)[None, :] / (2 * L)
    g = np.zeros((L, 2 * half))
    g[:, :n_bins] = wgt * np.cos(th).T
    g[:, half:half + n_bins] = -wgt * np.sin(th).T
    return _host_bf16(f), _host_bf16(fs), _host_bf16(g), half


def _hyena_ctx_kernel(u_ref, sig_ref, inorm_ref, cw_ref, cb_ref, skip_ref, f_ref, fs_ref, g_ref, o_ref, *, L, half):
    zero_row = jnp.zeros((1, u_ref.shape[2]), F32)
    uc = _conv3(u_ref[0], zero_row, zero_row, cw_ref[...], cb_ref[...], L)
    z = uc[:, 0:HYENA_WIDTH]
    for o in range(HYENA_ORDER):
        h = (jnp.dot(f_ref[...], sig_ref[2 * o].astype(BF16), preferred_element_type=F32)
             + jnp.dot(fs_ref[...], sig_ref[2 * o + 1].astype(BF16), preferred_element_type=F32)) * inorm_ref[o]
        x = jnp.dot(f_ref[...], z.astype(BF16), preferred_element_type=F32)
        xr, xi, hr, hi = x[:half], x[half:], h[:half], h[half:]
        y = jnp.concatenate([xr * hr - xi * hi, xr * hi + xi * hr], axis=0).astype(BF16)
        conv = jnp.dot(g_ref[...], y, preferred_element_type=F32)
        z = uc[:, (o + 1) * HYENA_WIDTH:(o + 2) * HYENA_WIDTH] * (conv + skip_ref[o] * z)
    o_ref[0] = z


def hyena_short(u, sig, inv_norm, cw, cb, skip):
    b, L, cols = u.shape
    f, fs, g, half = _rdft_tables(L)
    const2 = lambda bi: (0, 0)
    const3 = lambda bi: (0, 0, 0)
    return pl.pallas_call(
        functools.partial(_hyena_ctx_kernel, L=L, half=half),
        out_shape=jax.ShapeDtypeStruct((b, L, HYENA_WIDTH), F32),
        grid=(b,),
        in_specs=[pl.BlockSpec((1, L, cols), lambda bi: (bi, 0, 0)),
                  pl.BlockSpec(sig.shape, const3), pl.BlockSpec(inv_norm.shape, const3),
                  pl.BlockSpec(cw.shape, const2), pl.BlockSpec((1, cols), const2),
                  pl.BlockSpec((HYENA_ORDER, 1, HYENA_WIDTH), const3),
                  pl.BlockSpec(f.shape, const2), pl.BlockSpec(fs.shape, const2), pl.BlockSpec(g.shape, const2)],
        out_specs=pl.BlockSpec((1, L, HYENA_WIDTH), lambda bi: (bi, 0, 0)),
        compiler_params=_params("parallel"),
        name="hyena_context",
    )(u, sig, inv_norm, cw, cb.reshape(1, cols), skip.reshape(HYENA_ORDER, 1, HYENA_WIDTH), f, fs, g)


def _split_bf16(a):
    hi = a.astype(BF16)
    return hi, (a - hi.astype(F32)).astype(BF16)


def _norm_router_kernel(*refs, n_lat, two_src):
    if two_src:
        xl_ref, xc_ref, nw_ref, sh_ref, sc_ref, whi_ref, wlo_ref, br_ref, tri_ref, h_ref, lg_ref, cnt_ref = refs
        x = jnp.where(pl.program_id(0) < n_lat, xl_ref[...], xc_ref[...])
    else:
        xl_ref, nw_ref, sh_ref, sc_ref, whi_ref, wlo_ref, br_ref, tri_ref, h_ref, lg_ref, cnt_ref = refs
        x = xl_ref[...]
    h = _modnorm(x, nw_ref[...], sh_ref[0], sc_ref[0])
    h_ref[...] = h
    h_hi, h_lo = _split_bf16(h)
    lg = (jnp.dot(h_hi, whi_ref[...], preferred_element_type=F32)
          + jnp.dot(h_hi, wlo_ref[...], preferred_element_type=F32)
          + jnp.dot(h_lo, whi_ref[...], preferred_element_type=F32)) + br_ref[...]
    lane = lax.broadcasted_iota(jnp.int32, lg.shape, 1)
    first = lambda hit: jnp.min(jnp.where(hit, lane, ROUTER_COLS), axis=-1, keepdims=True)
    gl = jnp.where(lane < N_GROUPS, lg, -jnp.inf)
    gmax = jnp.max(gl, axis=-1, keepdims=True)
    grp = first(gl == gmax)
    grp_p = 1.0 / jnp.sum(jnp.exp(gl - gmax), axis=-1, keepdims=True)
    lo = N_GROUPS + grp * EXPERTS_PER_GROUP
    el = jnp.where((lane >= lo) & (lane < lo + EXPERTS_PER_GROUP), lg, -jnp.inf)
    e1 = jnp.max(el, axis=-1, keepdims=True)
    i1 = first(el == e1)
    el2 = jnp.where(lane == i1, -jnp.inf, el)
    e2 = jnp.max(el2, axis=-1, keepdims=True)
    i2 = first(el2 == e2)
    r = jnp.exp(e2 - e1)
    w1 = grp_p / (1.0 + r)
    w2 = w1 * r
    @pl.when(pl.program_id(0) == 0)
    def _():
        cnt_ref[...] = jnp.zeros_like(cnt_ref)

    hit1, hit2 = lane == i1, lane == i2
    picks = jnp.where(hit1 | hit2, 1.0, 0.0)
    before = cnt_ref[0:1, :] + jnp.dot(tri_ref[...], picks.astype(BF16), preferred_element_type=F32)
    rank1 = jnp.sum(jnp.where(hit1, before, 0.0), axis=-1, keepdims=True)
    rank2 = jnp.sum(jnp.where(hit2, before, 0.0), axis=-1, keepdims=True)
    cnt_ref[...] = jnp.broadcast_to(cnt_ref[0:1, :] + jnp.sum(picks, axis=0, keepdims=True), cnt_ref.shape)
    vals = [(i1 - N_GROUPS).astype(F32), (i2 - N_GROUPS).astype(F32), w1, w2, rank1, rank2]
    out = jnp.zeros(lg.shape, F32)
    for k, val in enumerate(vals):
        out = jnp.where(lane == k, val, out)
    lg_ref[...] = out


def norm_router(xl, xc, nw, mod_l, mod_c, w_router, b_router, tm=512):
    b, L, d = xl.shape
    two_src = xc is not None
    n_lat = b * L // tm
    per_batch = L // tm
    n_ctx = (xc.shape[0] * xc.shape[1]) // tm if two_src else 0
    w_hi, w_lo = _split_bf16(w_router)
    const = lambda i: (0, 0)
    if two_src:
        shift = jnp.concatenate([mod_l[0], mod_c[0]], axis=0)
        scale = jnp.concatenate([mod_l[1], mod_c[1]], axis=0)
        mod_map = lambda i: (jnp.where(i < n_lat, i // per_batch, b), 0, 0)
        srcs = [xl.reshape(b * L, d), xc.reshape(-1, d)]
        src_specs = [pl.BlockSpec((tm, d), lambda i: (jnp.minimum(i, n_lat - 1), 0)),
                     pl.BlockSpec((tm, d), lambda i: (jnp.maximum(i - n_lat, 0), 0))]
    else:
        shift, scale = mod_l
        mod_map = lambda i: (i // per_batch, 0, 0)
        srcs = [xl.reshape(b * L, d)]
        src_specs = [pl.BlockSpec((tm, d), lambda i: (i, 0))]
    n_tok = (n_lat + n_ctx) * tm
    tri = _host_bf16(np.tril(np.ones((tm, tm)), -1))
    return pl.pallas_call(
        functools.partial(_norm_router_kernel, n_lat=n_lat, two_src=two_src),
        out_shape=(jax.ShapeDtypeStruct((n_tok, d), F32), jax.ShapeDtypeStruct((n_tok, ROUTER_COLS), F32),
                   jax.ShapeDtypeStruct((SUBLANE, ROUTER_COLS), F32)),
        grid=(n_lat + n_ctx,),
        in_specs=src_specs + [pl.BlockSpec((1, d), const), pl.BlockSpec((1, 1, d), mod_map),
                              pl.BlockSpec((1, 1, d), mod_map), pl.BlockSpec((d, ROUTER_COLS), const),
                              pl.BlockSpec((d, ROUTER_COLS), const), pl.BlockSpec((1, ROUTER_COLS), const),
                              pl.BlockSpec((tm, tm), const)],
        out_specs=(pl.BlockSpec((tm, d), lambda i: (i, 0)), pl.BlockSpec((tm, ROUTER_COLS), lambda i: (i, 0)),
                   pl.BlockSpec((SUBLANE, ROUTER_COLS), const)),
        compiler_params=_params("arbitrary"),
        name="moe_norm_router",
    )(*srcs, nw.reshape(1, d), shift, scale, w_hi, w_lo, b_router, tri)


def _expert_ffn_kernel(te_ref, tv_ref, x_ref, rw_ref, wg_ref, wu_ref, wd_ref, o_ref):
    i = pl.program_id(0)

    @pl.when(tv_ref[i] > 0)
    def _():
        x = x_ref[...].astype(BF16)
        g = jnp.dot(x, wg_ref[0].astype(BF16), preferred_element_type=F32)
        u = jnp.dot(x, wu_ref[0].astype(BF16), preferred_element_type=F32)
        hid = _silu(g) * u * rw_ref[...]
        o_ref[...] = jnp.dot(hid.astype(BF16), wd_ref[0].astype(BF16), preferred_element_type=F32)

    @pl.when(tv_ref[i] == 0)
    def _():
        o_ref[...] = jnp.zeros_like(o_ref)


def expert_ffn(x_sorted, row_w, tile_expert, tile_valid, w_gate, w_up, w_down, tm):
    r, d = x_sorted.shape
    f = w_gate.shape[-1]
    grid_spec = pltpu.PrefetchScalarGridSpec(
        num_scalar_prefetch=2,
        grid=(r // tm,),
        in_specs=[pl.BlockSpec((tm, d), lambda i, te, tv: (i, 0)),
                  pl.BlockSpec((tm, 1), lambda i, te, tv: (i, 0)),
                  pl.BlockSpec((1, d, f), lambda i, te, tv: (te[i], 0, 0)),
                  pl.BlockSpec((1, d, f), lambda i, te, tv: (te[i], 0, 0)),
                  pl.BlockSpec((1, f, d), lambda i, te, tv: (te[i], 0, 0))],
        out_specs=pl.BlockSpec((tm, d), lambda i, te, tv: (i, 0)),
    )
    return pl.pallas_call(
        _expert_ffn_kernel,
        out_shape=jax.ShapeDtypeStruct((r, d), F32),
        grid_spec=grid_spec,
        compiler_params=_params("arbitrary"),
        name="moe_expert_ffn",
    )(tile_expert, tile_valid, x_sorted, row_w, w_gate, w_up, w_down)


def moe_apply(h_tokens, routed, counts_row, w_gate, w_up, w_down, layer, n_lat, tm=256):
    t, d = h_tokens.shape
    e_idx, e_w, rank = routed[:, 0:2].astype(jnp.int32), routed[:, 2:4], routed[:, 4:6].astype(jnp.int32)
    counts = counts_row[0, N_GROUPS:N_GROUPS + N_EXPERTS].astype(jnp.int32)
    n_pairs = 2 * t
    padded = (counts + tm - 1) // tm * tm
    pad_end = jnp.cumsum(padded)
    pad_start = pad_end - padded
    n_rows = n_pairs + N_EXPERTS * tm
    tile_start = jnp.arange(n_rows // tm, dtype=jnp.int32) * tm
    tile_expert = jnp.minimum(jnp.sum((pad_end[None, :] <= tile_start[:, None]).astype(jnp.int32), axis=1),
                              N_EXPERTS - 1)
    tile_valid = (tile_start < pad_end[-1]).astype(jnp.int32)
    onehot = (e_idx[:, :, None] == jnp.arange(N_EXPERTS, dtype=jnp.int32)).astype(jnp.int32)
    pos = jnp.sum(onehot * pad_start, axis=-1) + rank
    j = jnp.arange(tm, dtype=jnp.int32)[None, :]
    fill_key = jnp.where(j < (padded - counts)[:, None], (pad_start + counts)[:, None] + j, n_rows)
    keys = jnp.concatenate([pos.reshape(-1), fill_key.reshape(-1)])
    toks = jnp.concatenate([jnp.arange(n_pairs, dtype=jnp.int32) // 2, jnp.arange(N_EXPERTS * tm, dtype=jnp.int32) % t])
    wts = jnp.concatenate([e_w.reshape(-1), jnp.zeros((N_EXPERTS * tm,), F32)])
    _, row_token, row_w = lax.sort((keys, toks, wts), num_keys=1)
    x_sorted = h_tokens[row_token]
    wg = w_gate.reshape(-1, d, EXPERT_HIDDEN)
    wu = w_up.reshape(-1, d, EXPERT_HIDDEN)
    wd = w_down.reshape(-1, EXPERT_HIDDEN, d)
    y_sorted = expert_ffn(x_sorted, row_w[:, None], tile_expert + layer * N_EXPERTS, tile_valid, wg, wu, wd, tm)
    lat = (y_sorted[pos[:n_lat, 0]], y_sorted[pos[:n_lat, 1]])
    rest = (y_sorted[pos[n_lat:, 0]], y_sorted[pos[n_lat:, 1]]) if t > n_lat else None
    return lat, rest


def _final_kernel(x_ref, ya_ref, yb_ref, g_ref, w_ref, o_ref):
    x = x_ref[0] + g_ref[0] * (ya_ref[0] + yb_ref[0])
    o_ref[0] = x * lax.rsqrt(jnp.mean(x * x, axis=-1, keepdims=True) + EPS) * w_ref[...]


def final_norm(x, ya, yb, gate, w, tm=512):
    b, L, d = x.shape
    tok = pl.BlockSpec((1, tm, d), lambda bi, i: (bi, i, 0))
    return pl.pallas_call(
        _final_kernel,
        out_shape=jax.ShapeDtypeStruct((b, L, d), F32),
        grid=(b, L // tm),
        in_specs=[tok, tok, tok, pl.BlockSpec((1, 1, d), _mod_map(gate, b)),
                  pl.BlockSpec((1, d), lambda bi, i: (0, 0))],
        out_specs=tok,
        compiler_params=_params("parallel", "arbitrary"),
        name="final_rmsnorm",
    )(x, ya, yb, gate, w.reshape(1, d))


IN_SIZES = (SSD_WIDTH, SSD_CONV_CH, 2 * SSD_HEADS, HY_COLS, 2 * ML_WIDTH, ML_WIDTH, ML_WIDTH, 4 * ML_HEADS)


def _regroup_kernel(w_ref, o_ref):
    src = dst = 0
    for n in IN_SIZES:
        pad = -n % LANE
        o_ref[0, :, dst:dst + n] = w_ref[0, :, src:src + n].astype(BF16)
        if pad:
            o_ref[0, :, dst + n:dst + n + pad] = jnp.zeros((o_ref.shape[1], pad), BF16)
        src, dst = src + n, dst + n + pad


def regroup_in_weights(w_in):
    depth, d, n_in = w_in.shape
    n_out = sum(n + (-n % LANE) for n in IN_SIZES)
    one = pl.Buffered(1)
    return pl.pallas_call(
        _regroup_kernel,
        out_shape=jax.ShapeDtypeStruct((depth, d, n_out), BF16),
        grid=(depth,),
        in_specs=[pl.BlockSpec((1, d, n_in), lambda i: (i, 0, 0), pipeline_mode=one)],
        out_specs=pl.BlockSpec((1, d, n_out), lambda i: (i, 0, 0)),
        compiler_params=_params("arbitrary"),
        name="regroup_in_weights",
    )(w_in)


def kernel(x, c, ctx, c_ctx, w_mod, b_mod, norm1_w, norm2_w, w_in, w_out, ssd_conv_w, ssd_conv_b, ssd_dt_bias, ssd_a_log, ssd_d, ssd_norm_w, hy_conv_w, hy_conv_b, hy_pos_w1, hy_pos_b1, hy_pos_w2, hy_pos_b2, hy_pos_w3, hy_freq, hy_decay, hy_skip, ml_conv_w, ml_conv_b, ml_gate_b, ml_norm_w, grp_router_w, grp_router_b, exp_router_w, exp_router_b, moe_w_gate, moe_w_up, moe_w_down, final_norm_w):
    layer_params = dict(
        ssd_conv_w=ssd_conv_w, ssd_conv_b=ssd_conv_b, ssd_dt_bias=ssd_dt_bias, ssd_a_log=ssd_a_log,
        ssd_d=ssd_d, ssd_norm_w=ssd_norm_w, hy_conv_w=hy_conv_w, hy_conv_b=hy_conv_b,
        hy_pos_w1=hy_pos_w1, hy_pos_b1=hy_pos_b1, hy_pos_w2=hy_pos_w2, hy_pos_b2=hy_pos_b2,
        hy_pos_w3=hy_pos_w3, hy_freq=hy_freq, hy_decay=hy_decay, hy_skip=hy_skip,
        ml_conv_w=ml_conv_w, ml_conv_b=ml_conv_b, ml_gate_b=ml_gate_b, ml_norm_w=ml_norm_w)
    bsz, seq, d = x.shape
    n_ctx = ctx.shape[1]
    xl, xc = x, ctx
    moe_l = moe_c = None
    ssd0 = jnp.zeros((bsz, SSD_HEADS, HEAD_DIM, SSD_STATE), F32)
    ml0 = (jnp.zeros((bsz, ML_HEADS, HEAD_DIM, LANE), F32), jnp.zeros((bsz, SUBLANE, LANE), F32))
    c_rows = jnp.concatenate([c, c_ctx[None, :], jnp.zeros((SUBLANE - bsz - 1, d), F32)], axis=0)
    w_in_all = regroup_in_weights(w_in)
    for i in range(DEPTH):
        last = i == DEPTH - 1
        p = {name: arr[i] for name, arr in layer_params.items()}
        sp, mp = ssd_prepare(p), ml_prepare(p)
        mod = modulation(c_rows, w_mod, b_mod, i).reshape(SUBLANE, N_MOD, 1, d)
        mod_l = [mod[:bsz, k] for k in range(N_MOD)]
        mod_c = [mod[bsz:bsz + 1, k] for k in range(N_MOD)]
        w_out_b = w_out[i].astype(BF16)
        w_router = jnp.pad(jnp.concatenate([grp_router_w[i], exp_router_w[i]], axis=1),
                           ((0, 0), (0, ROUTER_COLS - N_GROUPS - N_EXPERTS)))
        b_router = jnp.pad(jnp.concatenate([grp_router_b[i], exp_router_b[i]]),
                           (0, ROUTER_COLS - N_GROUPS - N_EXPERTS)).reshape(1, ROUTER_COLS)

        uc_ssd, uc_hy, uc_ml, xc = norm_proj(xc, moe_c, norm1_w[i], mod_c[0], mod_c[1], w_in_all, i)
        yc_ssd, ssd_f, ssd_b = ssd_mixer(uc_ssd, sp, ssd0, ssd0, not last)
        yc_ml, ml_f, ml_b = ml_mixer(uc_ml, mp, ml0, ml0, not last)
        col_major = i % 2 == 1
        ul_ssd, ul_hy, ul_ml, xl = norm_proj(xl, moe_l, norm1_w[i], mod_l[0], mod_l[1], w_in_all, i, col_major)
        yl_ssd, _, _ = ssd_mixer(ul_ssd, sp, ssd_f, ssd_b, True)
        yl_ml, _, _ = ml_mixer(ul_ml, mp, ml_f, ml_b, True)
        h_spec = hyena_filter_spectra(*hyena_filter_signals(seq, p))
        yl_hy = hyena_long(ul_hy, h_spec, p['hy_conv_w'], p['hy_conv_b'], p['hy_skip'])
        xl = out_proj(yl_ssd, yl_hy, yl_ml, xl, mod_l[2], w_out_b, col_major)
        if not last:
            sig_c, inorm_c = hyena_filter_signals(n_ctx, p)
            yc_hy = hyena_short(uc_hy, sig_c, inorm_c, p['hy_conv_w'], p['hy_conv_b'], p['hy_skip'])
            xc = out_proj(yc_ssd, yc_hy, yc_ml, xc, mod_c[2], w_out_b)
        h_all, routed, counts_row = norm_router(xl, None if last else xc, norm2_w[i], (mod_l[3], mod_l[4]), (mod_c[3], mod_c[4]),
                                    w_router, b_router)
        lat, rest = moe_apply(h_all, routed, counts_row, moe_w_gate, moe_w_up, moe_w_down, i, bsz * seq)
        moe_l = (lat[0].reshape(bsz, seq, d), lat[1].reshape(bsz, seq, d), mod_l[5])
        if not last:
            moe_c = (rest[0].reshape(bsz, n_ctx, d), rest[1].reshape(bsz, n_ctx, d), mod_c[5])
    return final_norm(xl, *moe_l, final_norm_w)
```

```python
import functools
import math

import jax
import jax.numpy as jnp
import numpy as np
from jax import lax
from jax.experimental import pallas as pl
from jax.experimental.pallas import tpu as pltpu

D_MODEL = 1024
DEPTH = 2
GRID_W = 64
HEAD_DIM = 64
SSD_WIDTH = 384
SSD_HEADS = SSD_WIDTH // HEAD_DIM
SSD_GROUPS = 2
SSD_STATE = 64
HYENA_WIDTH = 256
HYENA_ORDER = 2
HYENA_BANDS = 16
ML_WIDTH = 384
ML_HEADS = ML_WIDTH // HEAD_DIM
N_GROUPS = 4
EXPERTS_PER_GROUP = 8
N_EXPERTS = N_GROUPS * EXPERTS_PER_GROUP
EXPERT_HIDDEN = 256
N_MOD = 6
EPS = 1e-6

LANE = 128
SUBLANE = 8
VMEM_LIMIT = 48 * 1024 * 1024
VMEM_LIMIT_HY = 56 * 1024 * 1024

SSD_CONV_CH = SSD_WIDTH + 2 * SSD_GROUPS * SSD_STATE
SSD_XBC0 = SSD_WIDTH
SSD_DT0 = SSD_XBC0 + SSD_CONV_CH
SSD_COLS = SSD_DT0 + LANE
HY_COLS = (HYENA_ORDER + 1) * HYENA_WIDTH
ML_V0 = 2 * ML_WIDTH
ML_O0 = ML_V0 + ML_WIDTH
ML_G0 = ML_O0 + ML_WIDTH
ML_COLS = ML_G0 + LANE
ROUTER_COLS = LANE

F32 = jnp.float32
BF16 = jnp.bfloat16
HI = lax.Precision.HIGHEST


def _params(*sem):
    return pltpu.CompilerParams(dimension_semantics=sem, vmem_limit_bytes=VMEM_LIMIT)


def _host_bf16(a):
    return jnp.asarray(np.asarray(a, np.float32).astype(BF16))


def _silu(x):
    return x * jax.nn.sigmoid(x)


def _softplus(x):
    return jnp.maximum(x, 0.0) + jnp.log(1.0 + jnp.exp(-jnp.abs(x)))


def _log_sigmoid(x):
    return jnp.minimum(x, 0.0) - jnp.log(1.0 + jnp.exp(-jnp.abs(x)))


def _mod_kernel(c_ref, w_ref, b_ref, o_ref):
    o_ref[...] = jnp.dot(_silu(c_ref[...]), w_ref[...], preferred_element_type=F32, precision=HI) + b_ref[...]


def modulation(c_rows, w_mod, b_mod, layer):
    depth, d, n = w_mod.shape
    tn = 1536
    return pl.pallas_call(
        _mod_kernel,
        out_shape=jax.ShapeDtypeStruct((c_rows.shape[0], n), F32),
        grid=(n // tn,),
        in_specs=[pl.BlockSpec(c_rows.shape, lambda j: (0, 0)),
                  pl.BlockSpec((None, d, tn), lambda j: (layer, 0, j)),
                  pl.BlockSpec((None, 1, tn), lambda j: (layer, 0, j))],
        out_specs=pl.BlockSpec((c_rows.shape[0], tn), lambda j: (0, j)),
        compiler_params=_params("arbitrary"),
        name="adaln_modulation",
    )(c_rows, w_mod, b_mod.reshape(depth, 1, n))


def _modnorm(x, nw, shift, scale):
    y = x * lax.rsqrt(jnp.mean(x * x, axis=-1, keepdims=True) + EPS) * nw
    return y * (1.0 + scale) + shift


def _mod_map(mod, b):
    return (lambda bi, i: (bi, 0, 0)) if mod.shape[0] == b else (lambda bi, i: (0, 0, 0))


def _tok_view(x, col_major):
    b, L, d = x.shape
    return x.reshape(b, L // GRID_W, GRID_W, d) if col_major else x


def _tok_spec(L, d, tm, col_major):
    if col_major:
        assert tm == (L // GRID_W) * SUBLANE
        return pl.BlockSpec((1, L // GRID_W, SUBLANE, d), lambda bi, i: (bi, 0, i, 0))
    return pl.BlockSpec((1, tm, d), lambda bi, i: (bi, i, 0))


def _tok_load(ref, col_major):
    if not col_major:
        return ref[0]
    return jnp.concatenate([ref[0, :, j, :] for j in range(ref.shape[2])], axis=0)


def _tok_store(ref, val, col_major):
    if not col_major:
        ref[0] = val
        return
    rows = ref.shape[1]
    for j in range(ref.shape[2]):
        ref[0, :, j, :] = val[j * rows:(j + 1) * rows]


def _moe_mix(ya, yb, routed):
    return routed[:, 2:3] * ya + routed[:, 3:4] * yb


def _norm_proj_kernel(*refs, col_major, fuse_moe):
    if fuse_moe:
        x_ref, ya_ref, yb_ref, rt_ref, g_ref, nw_ref, sh_ref, sc_ref, w_ref, ssd_ref, hy_ref, ml_ref, xo_ref = refs
    else:
        x_ref, nw_ref, sh_ref, sc_ref, w_ref, ssd_ref, hy_ref, ml_ref = refs
    x = _tok_load(x_ref, col_major)
    if fuse_moe:
        x = x + g_ref[0] * _moe_mix(_tok_load(ya_ref, col_major), _tok_load(yb_ref, col_major),
                                    _tok_load(rt_ref, col_major))
        _tok_store(xo_ref, x, col_major)
    h = _modnorm(x, nw_ref[...], sh_ref[0], sc_ref[0])
    u = jnp.dot(h.astype(BF16), w_ref[...], preferred_element_type=F32)
    ssd_ref[0] = u[:, 0:SSD_COLS]
    hy_ref[0] = u[:, SSD_COLS:SSD_COLS + HY_COLS]
    ml_ref[0] = u[:, SSD_COLS + HY_COLS:]


def norm_proj(x, moe, nw, shift, scale, w_all, layer, col_major=False, tm=512):
    b, L, d = x.shape
    n = w_all.shape[2]
    fuse_moe = moe is not None
    tm = (L // GRID_W) * SUBLANE if col_major else min(tm, L)
    tok = _tok_spec(L, d, tm, col_major)
    row = lambda bi, i: (bi, i, 0)
    const2 = lambda bi, i: (0, 0)
    args, in_specs = [_tok_view(x, col_major)], [tok]
    if fuse_moe:
        ya, yb, routed, gate = moe
        args += [_tok_view(ya, col_major), _tok_view(yb, col_major), _tok_view(routed, col_major), gate]
        in_specs += [tok, tok, _tok_spec(L, ROUTER_COLS, tm, col_major), pl.BlockSpec((1, 1, d), _mod_map(gate, b))]
    args += [nw.reshape(1, d), shift, scale, w_all]
    in_specs += [pl.BlockSpec((1, d), const2), pl.BlockSpec((1, 1, d), _mod_map(shift, b)),
                 pl.BlockSpec((1, 1, d), _mod_map(scale, b)),
                 pl.BlockSpec((None, d, n), lambda bi, i: (layer, 0, 0))]
    out_shape = [jax.ShapeDtypeStruct((b, L, SSD_COLS), F32), jax.ShapeDtypeStruct((b, L, HY_COLS), F32),
                 jax.ShapeDtypeStruct((b, L, ML_COLS), F32)]
    out_specs = [pl.BlockSpec((1, tm, SSD_COLS), row), pl.BlockSpec((1, tm, HY_COLS), row),
                 pl.BlockSpec((1, tm, ML_COLS), row)]
    if fuse_moe:
        out_shape.append(jax.ShapeDtypeStruct(args[0].shape, F32))
        out_specs.append(tok)
    outs = pl.pallas_call(
        functools.partial(_norm_proj_kernel, col_major=col_major, fuse_moe=fuse_moe),
        out_shape=tuple(out_shape),
        grid=(b, L // tm),
        in_specs=in_specs,
        out_specs=tuple(out_specs),
        compiler_params=_params("parallel", "arbitrary"),
        name="norm_in_proj",
    )(*args)
    return (*outs[:3], outs[3].reshape(b, L, d) if fuse_moe else x)


def _out_proj_kernel(ys_ref, yh_ref, ym_ref, x_ref, g_ref, w_ref, o_ref, *, col_major):
    y = jnp.concatenate([ys_ref[0], yh_ref[0], ym_ref[0]], axis=-1).astype(BF16)
    r = _tok_load(x_ref, col_major) + g_ref[0] * jnp.dot(y, w_ref[...], preferred_element_type=F32)
    _tok_store(o_ref, r, col_major)


def out_proj(y_ssd, y_hy, y_ml, x, gate, w_bf16, col_major=False, tm=512):
    b, L, d = x.shape
    tm = (L // GRID_W) * SUBLANE if col_major else min(tm, L)
    row = lambda bi, i: (bi, i, 0)
    tok = _tok_spec(L, d, tm, col_major)
    xv = _tok_view(x, col_major)
    return pl.pallas_call(
        functools.partial(_out_proj_kernel, col_major=col_major),
        out_shape=jax.ShapeDtypeStruct(xv.shape, F32),
        grid=(b, L // tm),
        in_specs=[pl.BlockSpec((1, tm, SSD_WIDTH), row), pl.BlockSpec((1, tm, HYENA_WIDTH), row),
                  pl.BlockSpec((1, tm, ML_WIDTH), row), tok,
                  pl.BlockSpec((1, 1, d), _mod_map(gate, b)),
                  pl.BlockSpec(w_bf16.shape, lambda bi, i: (0, 0))],
        out_specs=tok,
        compiler_params=_params("parallel", "arbitrary"),
        name="out_proj_residual",
    )(y_ssd, y_hy, y_ml, xv, gate, w_bf16).reshape(b, L, d)


def _conv3(xr, prev_row, next_row, cw, cb, q):
    rid = lax.broadcasted_iota(jnp.int32, (q, 1), 0)
    x_prev = jnp.where(rid == 0, prev_row, pltpu.roll(xr, 1, axis=0))
    x_next = jnp.where(rid == q - 1, next_row, pltpu.roll(xr, q - 1, axis=0))
    return x_prev * cw[0:1] + xr * cw[1:2] + x_next * cw[2:3] + cb


def _scan_mask(q, direction):
    li = lax.broadcasted_iota(jnp.int32, (q, q), 0)
    si = lax.broadcasted_iota(jnp.int32, (q, q), 1)
    return (si <= li) if direction == 0 else (si >= li)


def _running_max(x, direction, q):
    rid = lax.broadcasted_iota(jnp.int32, (q, 1), 0)
    s = 1
    while s < q:
        if direction == 0:
            x = jnp.where(rid >= s, jnp.maximum(x, pltpu.roll(x, s, axis=0)), x)
        else:
            x = jnp.where(rid < q - s, jnp.maximum(x, pltpu.roll(x, q - s, axis=0)), x)
        s *= 2
    return x


def _scan_specs(L, q, nc, cols, direction):
    hb = q // SUBLANE
    nrb = L // SUBLANE
    cidx = (lambda j: j) if direction == 0 else (lambda j: nc - 1 - j)
    specs = [pl.BlockSpec((1, q, cols), lambda bi, j: (bi, cidx(j), 0)),
             pl.BlockSpec((1, SUBLANE, cols), lambda bi, j: (bi, jnp.maximum(cidx(j) * hb - 1, 0), 0)),
             pl.BlockSpec((1, SUBLANE, cols), lambda bi, j: (bi, jnp.minimum((cidx(j) + 1) * hb, nrb - 1), 0))]
    return specs, cidx


def _ssd_kernel(*refs, direction, finalize, q, nc):
    if finalize:
        (u_ref, prev_ref, next_ref, yb_ref, init_ref, cw_ref, cb_ref, dtb_ref, a_ref, d_ref, nw_ref,
         y_ref, fin_ref, state_ref) = refs
    else:
        (u_ref, prev_ref, next_ref, init_ref, cw_ref, cb_ref, dtb_ref, a_ref,
         y_ref, fin_ref, state_ref) = refs
    j = pl.program_id(1)
    c = j if direction == 0 else nc - 1 - j

    @pl.when(j == 0)
    def _():
        state_ref[...] = init_ref[0]

    prev_row = jnp.where(c > 0, prev_ref[0, SUBLANE - 1:SUBLANE, SSD_XBC0:SSD_DT0], 0.0)
    next_row = jnp.where(c < nc - 1, next_ref[0, 0:1, SSD_XBC0:SSD_DT0], 0.0)
    xc = _silu(_conv3(u_ref[0, :, SSD_XBC0:SSD_DT0], prev_row, next_row, cw_ref[...], cb_ref[...], q))

    dt = _softplus(u_ref[0, :, SSD_DT0:SSD_COLS] + dtb_ref[...])
    mask = _scan_mask(q, direction)
    cum = jnp.dot(mask.astype(F32), dt * a_ref[...], preferred_element_type=F32, precision=HI)
    cum_t = cum.T
    end = q - 1 if direction == 0 else 0

    ys = []
    for g in range(SSD_GROUPS):
        b0 = SSD_WIDTH + g * SSD_STATE
        c0 = SSD_WIDTH + (SSD_GROUPS + g) * SSD_STATE
        bm_t = xc[:, b0:b0 + SSD_STATE].T
        cm = xc[:, c0:c0 + SSD_STATE].astype(BF16)
        scores = jnp.dot(cm, bm_t.astype(BF16), preferred_element_type=F32)
        for h in range(g * (SSD_HEADS // SSD_GROUPS), (g + 1) * (SSD_HEADS // SSD_GROUPS)):
            hl = direction * SSD_HEADS + h
            col = cum[:, hl:hl + 1]
            row = cum_t[hl:hl + 1, :]
            seg = jnp.exp(jnp.where(mask, col - row, -jnp.inf))
            xdt = (xc[:, h * HEAD_DIM:(h + 1) * HEAD_DIM] * dt[:, hl:hl + 1]).astype(BF16)
            y = jnp.dot((scores * seg).astype(BF16), xdt, preferred_element_type=F32)
            st = state_ref[h]
            y = y + jnp.dot(cm, st.astype(BF16), preferred_element_type=F32) * jnp.exp(col)
            tot = cum[end:end + 1, hl:hl + 1]
            upd = jnp.dot((bm_t * jnp.exp(tot - row)).astype(BF16), xdt, preferred_element_type=F32)
            state_ref[h] = st * jnp.exp(tot) + upd
            ys.append(y)
    y_all = jnp.concatenate(ys, axis=-1)
    if finalize:
        t = (y_all + yb_ref[0] + xc[:, 0:SSD_WIDTH] * d_ref[...]) * _silu(u_ref[0, :, 0:SSD_WIDTH])
        y_all = t * lax.rsqrt(jnp.mean(t * t, axis=-1, keepdims=True) + EPS) * nw_ref[...]
    y_ref[0] = y_all

    @pl.when(j == nc - 1)
    def _():
        fin_ref[0] = state_ref[...]


def ssd_pass(u, y_other, init, sp, direction, q):
    b, L, _ = u.shape
    q = min(q, L)
    nc = L // q
    finalize = y_other is not None
    in_specs, cidx = _scan_specs(L, q, nc, SSD_COLS, direction)
    const2 = lambda bi, j: (0, 0)
    st_spec = pl.BlockSpec((1, SSD_HEADS, HEAD_DIM, SSD_STATE), lambda bi, j: (bi, 0, 0, 0))
    y_spec = pl.BlockSpec((1, q, SSD_WIDTH), lambda bi, j: (bi, cidx(j), 0))
    args = [u, u, u]
    if finalize:
        in_specs.append(y_spec)
        args.append(y_other)
    consts = [sp['cw'], sp['cb'], sp['dtb'], sp['a']] + ([sp['d'], sp['nw']] if finalize else [])
    in_specs += [st_spec] + [pl.BlockSpec(t.shape, const2) for t in consts]
    args += [init] + consts
    return pl.pallas_call(
        functools.partial(_ssd_kernel, direction=direction, finalize=finalize, q=q, nc=nc),
        out_shape=(jax.ShapeDtypeStruct((b, L, SSD_WIDTH), F32),
                   jax.ShapeDtypeStruct((b, SSD_HEADS, HEAD_DIM, SSD_STATE), F32)),
        grid=(b, nc),
        in_specs=in_specs,
        out_specs=(y_spec, st_spec),
        scratch_shapes=[pltpu.VMEM((SSD_HEADS, HEAD_DIM, SSD_STATE), F32)],
        compiler_params=_params("parallel", "arbitrary"),
        name="ssd_scan_%s" % ("fwd" if direction == 0 else "bwd"),
    )(*args)


def ssd_prepare(p):
    pad = lambda v: jnp.pad(v.reshape(1, -1), ((0, 0), (0, LANE - v.size)))
    return dict(cw=p['ssd_conv_w'], cb=p['ssd_conv_b'].reshape(1, -1),
                dtb=pad(p['ssd_dt_bias']), a=pad(-jnp.exp(p['ssd_a_log'])),
                d=jnp.repeat(p['ssd_d'], HEAD_DIM).reshape(1, -1), nw=p['ssd_norm_w'].reshape(1, -1))


def ssd_mixer(u, sp, init_f, init_b, want_y, q=256):
    yb, fin_b = ssd_pass(u, None, init_b, sp, 1, q)
    y, fin_f = ssd_pass(u, yb if want_y else None, init_f, sp, 0, q)
    return y, fin_f, fin_b


def _ml_kernel(*refs, direction, finalize, q, nc):
    if finalize:
        (u_ref, prev_ref, next_ref, hb_ref, s_init_ref, m_init_ref, cw_ref, cb_ref, gb_ref, nw_ref, pool_ref,
         y_ref, s_fin_ref, m_fin_ref, s_ref, m_ref) = refs
    else:
        (u_ref, prev_ref, next_ref, s_init_ref, m_init_ref, cw_ref, cb_ref, gb_ref,
         y_ref, s_fin_ref, m_fin_ref, s_ref, m_ref) = refs
    j = pl.program_id(1)
    c = j if direction == 0 else nc - 1 - j

    @pl.when(j == 0)
    def _():
        s_ref[...] = s_init_ref[0]
        m_ref[...] = m_init_ref[0]

    prev_row = jnp.where(c > 0, prev_ref[0, SUBLANE - 1:SUBLANE, 0:ML_V0], 0.0)
    next_row = jnp.where(c < nc - 1, next_ref[0, 0:1, 0:ML_V0], 0.0)
    qk = _silu(_conv3(u_ref[0, :, 0:ML_V0], prev_row, next_row, cw_ref[...], cb_ref[...], q))
    v = u_ref[0, :, ML_V0:ML_O0]

    gb = u_ref[0, :, ML_G0:ML_COLS] + gb_ref[...]
    mask = _scan_mask(q, direction)
    cum = jnp.dot(mask.astype(F32), _log_sigmoid(gb), preferred_element_type=F32, precision=HI)
    ig = pltpu.roll(gb, ML_HEADS, axis=1)
    end = q - 1 if direction == 0 else 0
    m_prev = m_ref[0:1, :]
    tot = cum[end:end + 1, :]
    w_end = tot - cum + ig
    m_loc = jnp.max(w_end, axis=0, keepdims=True)
    e_end = jnp.exp(w_end - m_loc)
    m_new = jnp.maximum(tot + m_prev, m_loc)
    a_prev = jnp.exp(tot + m_prev - m_new)
    a_loc = jnp.exp(m_loc - m_new)
    inter = cum + m_prev
    rel = ig - cum
    m_t = jnp.maximum(inter, cum + _running_max(rel, direction, q))
    col_a = cum - m_t
    a_inter = jnp.exp(inter - m_t)
    floor = jnp.exp(-m_t)
    rel_t = rel.T
    e_end_t = e_end.T
    k_t = (qk[:, ML_WIDTH:2 * ML_WIDTH] * (HEAD_DIM ** -0.5)).T
    one_col = (lax.broadcasted_iota(jnp.int32, (q, HEAD_DIM), 1) == 0).astype(F32)

    ys = []
    for h in range(ML_HEADS):
        fl = direction * 2 * ML_HEADS + ML_HEADS + h
        qh = qk[:, h * HEAD_DIM:(h + 1) * HEAD_DIM].astype(BF16)
        kh_t = k_t[h * HEAD_DIM:(h + 1) * HEAD_DIM, :]
        v_ext = jnp.concatenate([v[:, h * HEAD_DIM:(h + 1) * HEAD_DIM], one_col], axis=-1).astype(BF16)
        pw = jnp.exp(jnp.where(mask, col_a[:, fl:fl + 1] + rel_t[fl:fl + 1, :], -jnp.inf))
        scores = jnp.dot(qh, kh_t.astype(BF16), preferred_element_type=F32)
        nd = jnp.dot((scores * pw).astype(BF16), v_ext, preferred_element_type=F32)
        st = s_ref[h]
        nd = nd + a_inter[:, fl:fl + 1] * jnp.dot(qh, st.astype(BF16), preferred_element_type=F32)
        den = nd[:, HEAD_DIM:HEAD_DIM + 1]
        ys.append(nd[:, 0:HEAD_DIM] / jnp.maximum(jnp.abs(den), floor[:, fl:fl + 1]))
        upd = jnp.dot((kh_t * e_end_t[fl:fl + 1, :]).astype(BF16), v_ext, preferred_element_type=F32)
        s_ref[h] = a_prev[:, fl:fl + 1] * st + a_loc[:, fl:fl + 1] * upd
    m_ref[...] = jnp.broadcast_to(m_new, m_ref.shape)
    y_all = jnp.concatenate(ys, axis=-1)
    if finalize:
        hs = y_all + hb_ref[0]
        hc = hs - jnp.dot(hs.astype(BF16), pool_ref[...], preferred_element_type=F32)
        var = jnp.dot((hc * hc).astype(BF16), pool_ref[...], preferred_element_type=F32)
        y_all = hc * lax.rsqrt(var + EPS) * nw_ref[...] * jax.nn.sigmoid(u_ref[0, :, ML_O0:ML_G0])
    y_ref[0] = y_all

    @pl.when(j == nc - 1)
    def _():
        s_fin_ref[0] = s_ref[...]
        m_fin_ref[0] = m_ref[...]


def ml_pass(u, h_other, init, mp, direction, q):
    b, L, _ = u.shape
    q = min(q, L)
    nc = L // q
    finalize = h_other is not None
    in_specs, cidx = _scan_specs(L, q, nc, ML_COLS, direction)
    const2 = lambda bi, j: (0, 0)
    s_spec = pl.BlockSpec((1, ML_HEADS, HEAD_DIM, LANE), lambda bi, j: (bi, 0, 0, 0))
    m_spec = pl.BlockSpec((1, SUBLANE, LANE), lambda bi, j: (bi, 0, 0))
    y_spec = pl.BlockSpec((1, q, ML_WIDTH), lambda bi, j: (bi, cidx(j), 0))
    args = [u, u, u]
    if finalize:
        in_specs.append(y_spec)
        args.append(h_other)
    consts = [mp['cw'], mp['cb'], mp['gb']] + ([mp['nw'], mp['pool']] if finalize else [])
    in_specs += [s_spec, m_spec] + [pl.BlockSpec(t.shape, const2) for t in consts]
    args += [init[0], init[1]] + consts
    y, s_fin, m_fin = pl.pallas_call(
        functools.partial(_ml_kernel, direction=direction, finalize=finalize, q=q, nc=nc),
        out_shape=(jax.ShapeDtypeStruct((b, L, ML_WIDTH), F32),
                   jax.ShapeDtypeStruct((b, ML_HEADS, HEAD_DIM, LANE), F32),
                   jax.ShapeDtypeStruct((b, SUBLANE, LANE), F32)),
        grid=(b, nc),
        in_specs=in_specs,
        out_specs=(y_spec, s_spec, m_spec),
        scratch_shapes=[pltpu.VMEM((ML_HEADS, HEAD_DIM, LANE), F32), pltpu.VMEM((SUBLANE, LANE), F32)],
        compiler_params=_params("parallel", "arbitrary"),
        name="mlstm_scan_%s" % ("fwd" if direction == 0 else "bwd"),
    )(*args)
    return y, (s_fin, m_fin)


def ml_prepare(p):
    gb = p['ml_gate_b'].reshape(1, -1)
    head = np.arange(ML_WIDTH) // HEAD_DIM
    pool = _host_bf16((head[:, None] == head[None, :]) / HEAD_DIM)
    return dict(cw=p['ml_conv_w'], cb=p['ml_conv_b'].reshape(1, -1),
                gb=jnp.pad(gb, ((0, 0), (0, LANE - gb.shape[1]))), nw=p['ml_norm_w'].reshape(1, -1), pool=pool)


def ml_mixer(u, mp, init_f, init_b, want_y, q=256):
    hb, fin_b = ml_pass(u, None, init_b, mp, 1, q)
    y, fin_f = ml_pass(u, hb if want_y else None, init_f, mp, 0, q)
    return y, fin_f, fin_b


FFT_L = 4096
FFT_N = 2 * FFT_L
FFT_N2 = 128
FFT_N1 = FFT_N // FFT_N2
FFT_N1H = FFT_L // FFT_N2
FFT_K1 = FFT_N1 // 2 + 1
FFT_R = 80
FFT_UNROLL = 8
FFT_PITCH = FFT_N2 + SUBLANE


def _fft_tables():
    n2 = np.arange(FFT_N2)[:, None, None]
    k1 = np.arange(FFT_K1)[None, :, None]
    n1 = np.arange(FFT_N1H)[None, None, :]
    th = 2 * np.pi * (((FFT_N2 * n1 + n2) * k1) % FFT_N) / FFT_N
    f1 = np.zeros((FFT_N2, FFT_R, FFT_N1H))
    f1[:, 0:2 * FFT_K1:2, :] = np.cos(th)
    f1[:, 1:2 * FFT_K1:2, :] = -np.sin(th)
    wgt = np.where((np.arange(FFT_K1) == 0) | (np.arange(FFT_K1) == FFT_N1 // 2), 1.0, 2.0)[None, :, None] / FFT_N
    g1 = np.zeros((FFT_N2, FFT_N1H, FFT_R))
    g1[:, :, 0:2 * FFT_K1:2] = np.transpose(wgt * np.cos(th), (0, 2, 1))
    g1[:, :, 1:2 * FFT_K1:2] = np.transpose(-wgt * np.sin(th), (0, 2, 1))
    ph = 2 * np.pi * ((np.arange(FFT_N2)[:, None] * np.arange(FFT_N2)[None, :]) % FFT_N2) / FFT_N2
    c, s = np.cos(ph), np.sin(ph)
    f2 = np.block([[c, s], [-s, c]])
    sign = np.where(np.arange(FFT_R) // 2 % 2 == 0, 1.0, -1.0)[None, :, None]
    f1ab = np.concatenate([f1, f1 * sign], axis=2)
    return _host_bf16(f1), _host_bf16(f1ab), _host_bf16(f2), _host_bf16(f2.T), _host_bf16(g1)


def _sld(ref, n2, count):
    rows = pl.ds(n2, count, stride=FFT_PITCH)
    return jnp.concatenate([ref[0, rows, :], ref[1, rows, :]], axis=-1)


def _sst(ref, n2, count, val):
    rows = pl.ds(n2, count, stride=FFT_PITCH)
    ref[0, rows, :] = val[:, 0:LANE]
    ref[1, rows, :] = val[:, LANE:2 * LANE]


def _blk_ld(ref, blk, nblk):
    parts = []
    for k in range(nblk):
        rows = pl.ds(pl.multiple_of((blk + k) * FFT_PITCH, SUBLANE), FFT_N2)
        parts.append(jnp.concatenate([ref[0, rows, :], ref[1, rows, :]], axis=-1))
    return parts[0] if nblk == 1 else jnp.concatenate(parts, axis=0)


def _blk_st(ref, blk, nblk, val):
    for k in range(nblk):
        rows = pl.ds(pl.multiple_of((blk + k) * FFT_PITCH, SUBLANE), FFT_N2)
        ref[0, rows, :] = val[k * FFT_N2:(k + 1) * FFT_N2, 0:LANE]
        ref[1, rows, :] = val[k * FFT_N2:(k + 1) * FFT_N2, LANE:2 * LANE]


def _fft_stage1(z_ref, a_ref, f1_ref, n_in=FFT_N1H):
    def body(n2, carry):
        xs = _sld(z_ref, n2, n_in).astype(BF16)
        _sst(a_ref, n2, FFT_R, jnp.dot(f1_ref[n2], xs, preferred_element_type=F32))
        return carry
    lax.fori_loop(0, FFT_N2, body, 0, unroll=FFT_UNROLL)


def _spectrum_kernel(x_ref, inorm_ref, f1_ref, f2_ref, o_ref, z_ref, a_ref):
    for half in range(2):
        for n1 in range(FFT_N1H):
            _blk_st(z_ref, half * FFT_N1H + n1, 1, x_ref[half, n1 * FFT_N2:(n1 + 1) * FFT_N2, :])
    _fft_stage1(z_ref, a_ref, f1_ref, 2 * FFT_N1H)
    inorm = inorm_ref[0]

    def body(k1, carry):
        slab = _blk_ld(a_ref, 2 * k1, 2).astype(BF16)
        o_ref[0, k1] = (jnp.dot(f2_ref[...], slab, preferred_element_type=F32) * inorm).astype(BF16)
        return carry
    lax.fori_loop(0, FFT_K1, body, 0, unroll=3)


def hyena_filter_spectra(sig, inv_norm):
    _, L, c = sig.shape
    _, f1ab, f2, _, _ = _fft_tables()
    one = pl.Buffered(1)
    return pl.pallas_call(
        _spectrum_kernel,
        out_shape=jax.ShapeDtypeStruct((HYENA_ORDER, FFT_K1, 2 * FFT_N2, c), BF16),
        grid=(HYENA_ORDER,),
        in_specs=[pl.BlockSpec((2, L, c), lambda i: (i, 0, 0)),
                  pl.BlockSpec((1, 1, c), lambda i: (i, 0, 0)),
                  pl.BlockSpec(f1ab.shape, lambda i: (0, 0, 0), pipeline_mode=one),
                  pl.BlockSpec(f2.shape, lambda i: (0, 0), pipeline_mode=one)],
        out_specs=pl.BlockSpec((1, FFT_K1, 2 * FFT_N2, c), lambda i: (i, 0, 0, 0)),
        scratch_shapes=[pltpu.VMEM((2, 2 * FFT_N1H * FFT_PITCH, LANE), F32),
                        pltpu.VMEM((2, FFT_R * FFT_PITCH, LANE), F32)],
        compiler_params=pltpu.CompilerParams(dimension_semantics=("arbitrary",), vmem_limit_bytes=VMEM_LIMIT_HY),
        name="hyena_filter_spectrum",
    )(sig, inv_norm, f1ab, f2)


def _conv3_rows(src, dst, cw, cb, L):
    rows = 2 * FFT_N2
    nchunk = L // rows

    def body(i, carry):
        r0 = pl.multiple_of(i * rows, rows)
        prev_row = jnp.where(i > 0, src[pl.ds(jnp.maximum(r0 - 1, 0), 1), :], 0.0)
        next_row = jnp.where(i < nchunk - 1, src[pl.ds(jnp.minimum(r0 + rows, L - 1), 1), :], 0.0)
        _blk_st(dst, 2 * i, 2, _conv3(src[pl.ds(r0, rows), :], prev_row, next_row, cw, cb, rows))
        return carry
    lax.fori_loop(0, nchunk, body, 0, unroll=2)


def _hyena_kernel(v_ref, g_ref, h_ref, cwv_ref, cbv_ref, cwg_ref, cbg_ref, skip_ref,
                  f1_ref, f2_ref, f2t_ref, g1_ref, o_ref, z_ref, gc_ref, a_ref):
    order = pl.program_id(1)

    @pl.when(order == 0)
    def _():
        _conv3_rows(v_ref.at[0], z_ref, cwv_ref[...], cbv_ref[...], FFT_L)

    _conv3_rows(g_ref.at[0], gc_ref, cwg_ref[0], cbg_ref[0], FFT_L)
    _fft_stage1(z_ref, a_ref, f1_ref)

    def mid(k1, carry):
        x = jnp.dot(f2_ref[...], _blk_ld(a_ref, 2 * k1, 2).astype(BF16), preferred_element_type=F32)
        h = h_ref[0, k1].astype(F32)
        xr, xi, hr, hi = x[:FFT_N2], x[FFT_N2:], h[:FFT_N2], h[FFT_N2:]
        y = jnp.concatenate([xr * hr - xi * hi, xr * hi + xi * hr], axis=0).astype(BF16)
        _blk_st(a_ref, 2 * k1, 2, jnp.dot(f2t_ref[...], y, preferred_element_type=F32))
        return carry
    lax.fori_loop(0, FFT_K1, mid, 0, unroll=3)

    skip = skip_ref[0]

    def last(n2, carry):
        bs = _sld(a_ref, n2, FFT_R).astype(BF16)
        y = jnp.dot(g1_ref[n2], bs, preferred_element_type=F32)
        _sst(z_ref, n2, FFT_N1H, _sld(gc_ref, n2, FFT_N1H) * (y + skip * _sld(z_ref, n2, FFT_N1H)))
        return carry
    lax.fori_loop(0, FFT_N2, last, 0, unroll=FFT_UNROLL)

    @pl.when(order == 1)
    def _():
        for n1 in range(FFT_N1H):
            o_ref[0, n1 * FFT_N2:(n1 + 1) * FFT_N2, :] = _blk_ld(z_ref, n1, 1)


def hyena_long(u, h_spec, cw, cb, skip):
    b, L, _ = u.shape
    c = HYENA_WIDTH
    f1, _, f2, f2t, g1 = _fft_tables()
    one = pl.Buffered(1)
    cw3 = cw.reshape(3, 3, c).transpose(1, 0, 2)
    cb3 = cb.reshape(3, 1, c)
    return pl.pallas_call(
        _hyena_kernel,
        out_shape=jax.ShapeDtypeStruct((b, L, c), F32),
        grid=(b, 2),
        in_specs=[pl.BlockSpec((1, L, c), lambda bi, o: (bi, 0, 0), pipeline_mode=one),
                  pl.BlockSpec((1, L, c), lambda bi, o: (bi, 0, 1 + o)),
                  pl.BlockSpec((1, FFT_K1, 2 * FFT_N2, c), lambda bi, o: (o, 0, 0, 0)),
                  pl.BlockSpec((3, c), lambda bi, o: (0, 0)),
                  pl.BlockSpec((1, c), lambda bi, o: (0, 0)),
                  pl.BlockSpec((1, 3, c), lambda bi, o: (1 + o, 0, 0)),
                  pl.BlockSpec((1, 1, c), lambda bi, o: (1 + o, 0, 0)),
                  pl.BlockSpec((1, 1, c), lambda bi, o: (o, 0, 0)),
                  pl.BlockSpec(f1.shape, lambda bi, o: (0, 0, 0), pipeline_mode=one),
                  pl.BlockSpec(f2.shape, lambda bi, o: (0, 0), pipeline_mode=one),
                  pl.BlockSpec(f2t.shape, lambda bi, o: (0, 0), pipeline_mode=one),
                  pl.BlockSpec(g1.shape, lambda bi, o: (0, 0, 0), pipeline_mode=one)],
        out_specs=pl.BlockSpec((1, L, c), lambda bi, o: (bi, 0, 0)),
        scratch_shapes=[pltpu.VMEM((2, FFT_N1H * FFT_PITCH, LANE), F32),
                        pltpu.VMEM((2, FFT_N1H * FFT_PITCH, LANE), F32),
                        pltpu.VMEM((2, FFT_R * FFT_PITCH, LANE), F32)],
        compiler_params=pltpu.CompilerParams(dimension_semantics=("parallel", "arbitrary"),
                                             vmem_limit_bytes=VMEM_LIMIT_HY),
        name="hyena_long_conv",
    )(u, u, h_spec, cw3[0], cb3[0], cw3, cb3, skip.reshape(2, 1, c), f1, f2, f2t, g1)


HY_FILT = 2 * HYENA_ORDER * HYENA_WIDTH
HY_HALF = HYENA_ORDER * HYENA_WIDTH


def _filter_kernel(wt_ref, wc_ref, ws_ref, b1_ref, w2_ref, b2_ref, w3_ref, freq_ref, decay_ref,
                   sig_ref, asum_ref, *, L, rows):
    i = pl.program_id(0)
    n = (i * rows + lax.broadcasted_iota(jnp.int32, (rows, 1), 0)).astype(F32)
    band = lax.broadcasted_iota(jnp.int32, (1, HYENA_BANDS), 1).astype(F32)
    bands = 1e-4 + band * ((HYENA_BANDS - 1 - 1e-4) / (HYENA_BANDS - 1))
    freq = freq_ref[...]

    @pl.when(i == 0)
    def _():
        asum_ref[...] = jnp.zeros_like(asum_ref)

    for side, t in enumerate((n, L - n)):
        t_unit = t / float(max(L - 1, 1))
        ang = (2 * math.pi / L) * t * bands
        pre = (t_unit * wt_ref[...] + jnp.dot(jnp.cos(ang), wc_ref[...], preferred_element_type=F32, precision=HI)
               - jnp.dot(jnp.sin(ang), ws_ref[...], preferred_element_type=F32, precision=HI) + b1_ref[...])
        hdn = jnp.sin(freq * pre)
        hdn = jnp.sin(freq * (jnp.dot(hdn, w2_ref[...], preferred_element_type=F32, precision=HI) + b2_ref[...]))
        cols = slice(side * HY_HALF, (side + 1) * HY_HALF)
        val = jnp.dot(hdn, w3_ref[:, cols], preferred_element_type=F32, precision=HI)
        val = val * jnp.exp(-t_unit * jnp.abs(decay_ref[:, cols]))
        if side == 1:
            val = jnp.where(n > 0, val, 0.0)
        for o in range(HYENA_ORDER):
            sig_ref[2 * o + side] = val[:, o * HYENA_WIDTH:(o + 1) * HYENA_WIDTH]
        asum_ref[side:side + 1, :] += jnp.sum(jnp.abs(val), axis=0, keepdims=True)


def hyena_filter_signals(L, p):
    rows = min(L, 512)
    fh = p['hy_pos_w1'].shape[1]
    w1 = p['hy_pos_w1']
    consts = [w1[0:1], w1[1:1 + HYENA_BANDS], w1[1 + HYENA_BANDS:], p['hy_pos_b1'].reshape(1, fh), p['hy_pos_w2'],
              p['hy_pos_b2'].reshape(1, fh), p['hy_pos_w3'], p['hy_freq'].reshape(1, fh),
              p['hy_decay'].reshape(1, HY_FILT)]
    sig, asum = pl.pallas_call(
        functools.partial(_filter_kernel, L=L, rows=rows),
        out_shape=(jax.ShapeDtypeStruct((2 * HYENA_ORDER, L, HYENA_WIDTH), F32),
                   jax.ShapeDtypeStruct((SUBLANE, HY_HALF), F32)),
        grid=(L // rows,),
        in_specs=[pl.BlockSpec(c.shape, lambda i: (0, 0)) for c in consts],
        out_specs=(pl.BlockSpec((2 * HYENA_ORDER, rows, HYENA_WIDTH), lambda i: (0, i, 0)),
                   pl.BlockSpec((SUBLANE, HY_HALF), lambda i: (0, 0))),
        compiler_params=_params("arbitrary"),
        name="hyena_filters",
    )(*consts)
    inv_norm = 1.0 / (asum[0] + asum[1]).reshape(HYENA_ORDER, 1, HYENA_WIDTH)
    return sig, inv_norm


def _rdft_tables(L):
    n_bins = L + 1
    half = -(-n_bins // 16) * 16
    k = np.arange(n_bins)[:, None]
    n = np.arange(L)[None, :]
    th = 2 * np.pi * ((k * n) % (2 * L)) / (2 * L)
    f = np.zeros((2 * half, L))
    f[:n_bins] = np.cos(th)
    f[half:half + n_bins] = -np.sin(th)
    sign = np.where(np.arange(n_bins) % 2 == 0, 1.0, -1.0)[:, None]
    fs = np.zeros_like(f)
    fs[:n_bins] = f[:n_bins] * sign
    fs[half:half + n_bins] = f[half:half + n_bins] * sign
    wgt = np.where((np.arange(n_bins) == 0) | (np.arange(n_bins) == L), 1.0, 2.0)[None, :] / (2 * L)
    g = np.zeros((L, 2 * half))
    g[:, :n_bins] = wgt * np.cos(th).T
    g[:, half:half + n_bins] = -wgt * np.sin(th).T
    return _host_bf16(f), _host_bf16(fs), _host_bf16(g), half


def _hyena_ctx_kernel(u_ref, sig_ref, inorm_ref, cw_ref, cb_ref, skip_ref, f_ref, fs_ref, g_ref, o_ref, *, L, half):
    zero_row = jnp.zeros((1, u_ref.shape[2]), F32)
    uc = _conv3(u_ref[0], zero_row, zero_row, cw_ref[...], cb_ref[...], L)
    z = uc[:, 0:HYENA_WIDTH]
    for o in range(HYENA_ORDER):
        h = (jnp.dot(f_ref[...], sig_ref[2 * o].astype(BF16), preferred_element_type=F32)
             + jnp.dot(fs_ref[...], sig_ref[2 * o + 1].astype(BF16), preferred_element_type=F32)) * inorm_ref[o]
        x = jnp.dot(f_ref[...], z.astype(BF16), preferred_element_type=F32)
        xr, xi, hr, hi = x[:half], x[half:], h[:half], h[half:]
        y = jnp.concatenate([xr * hr - xi * hi, xr * hi + xi * hr], axis=0).astype(BF16)
        conv = jnp.dot(g_ref[...], y, preferred_element_type=F32)
        z = uc[:, (o + 1) * HYENA_WIDTH:(o + 2) * HYENA_WIDTH] * (conv + skip_ref[o] * z)
    o_ref[0] = z


def hyena_short(u, sig, inv_norm, cw, cb, skip):
    b, L, cols = u.shape
    f, fs, g, half = _rdft_tables(L)
    const2 = lambda bi: (0, 0)
    const3 = lambda bi: (0, 0, 0)
    return pl.pallas_call(
        functools.partial(_hyena_ctx_kernel, L=L, half=half),
        out_shape=jax.ShapeDtypeStruct((b, L, HYENA_WIDTH), F32),
        grid=(b,),
        in_specs=[pl.BlockSpec((1, L, cols), lambda bi: (bi, 0, 0)),
                  pl.BlockSpec(sig.shape, const3), pl.BlockSpec(inv_norm.shape, const3),
                  pl.BlockSpec(cw.shape, const2), pl.BlockSpec((1, cols), const2),
                  pl.BlockSpec((HYENA_ORDER, 1, HYENA_WIDTH), const3),
                  pl.BlockSpec(f.shape, const2), pl.BlockSpec(fs.shape, const2), pl.BlockSpec(g.shape, const2)],
        out_specs=pl.BlockSpec((1, L, HYENA_WIDTH), lambda bi: (bi, 0, 0)),
        compiler_params=_params("parallel"),
        name="hyena_context",
    )(u, sig, inv_norm, cw, cb.reshape(1, cols), skip.reshape(HYENA_ORDER, 1, HYENA_WIDTH), f, fs, g)


def _split_bf16(a):
    hi = a.astype(BF16)
    return hi, (a - hi.astype(F32)).astype(BF16)


def _norm_router_kernel(*refs, n_lat, two_src):
    if two_src:
        xl_ref, xc_ref, nw_ref, sh_ref, sc_ref, whi_ref, wlo_ref, br_ref, tri_ref, h_ref, lg_ref, cnt_ref = refs
        x = jnp.where(pl.program_id(0) < n_lat, xl_ref[...], xc_ref[...])
    else:
        xl_ref, nw_ref, sh_ref, sc_ref, whi_ref, wlo_ref, br_ref, tri_ref, h_ref, lg_ref, cnt_ref = refs
        x = xl_ref[...]
    h = _modnorm(x, nw_ref[...], sh_ref[0], sc_ref[0])
    h_ref[...] = h
    h_hi, h_lo = _split_bf16(h)
    lg = (jnp.dot(h_hi, whi_ref[...], preferred_element_type=F32)
          + jnp.dot(h_hi, wlo_ref[...], preferred_element_type=F32)
          + jnp.dot(h_lo, whi_ref[...], preferred_element_type=F32)) + br_ref[...]
    lane = lax.broadcasted_iota(jnp.int32, lg.shape, 1)
    first = lambda hit: jnp.min(jnp.where(hit, lane, ROUTER_COLS), axis=-1, keepdims=True)
    gl = jnp.where(lane < N_GROUPS, lg, -jnp.inf)
    gmax = jnp.max(gl, axis=-1, keepdims=True)
    grp = first(gl == gmax)
    grp_p = 1.0 / jnp.sum(jnp.exp(gl - gmax), axis=-1, keepdims=True)
    lo = N_GROUPS + grp * EXPERTS_PER_GROUP
    el = jnp.where((lane >= lo) & (lane < lo + EXPERTS_PER_GROUP), lg, -jnp.inf)
    e1 = jnp.max(el, axis=-1, keepdims=True)
    i1 = first(el == e1)
    el2 = jnp.where(lane == i1, -jnp.inf, el)
    e2 = jnp.max(el2, axis=-1, keepdims=True)
    i2 = first(el2 == e2)
    r = jnp.exp(e2 - e1)
    w1 = grp_p / (1.0 + r)
    w2 = w1 * r
    @pl.when(pl.program_id(0) == 0)
    def _():
        cnt_ref[...] = jnp.zeros_like(cnt_ref)

    hit1, hit2 = lane == i1, lane == i2
    picks = jnp.where(hit1 | hit2, 1.0, 0.0)
    before = cnt_ref[0:1, :] + jnp.dot(tri_ref[...], picks.astype(BF16), preferred_element_type=F32)
    rank1 = jnp.sum(jnp.where(hit1, before, 0.0), axis=-1, keepdims=True)
    rank2 = jnp.sum(jnp.where(hit2, before, 0.0), axis=-1, keepdims=True)
    cnt_ref[...] = jnp.broadcast_to(cnt_ref[0:1, :] + jnp.sum(picks, axis=0, keepdims=True), cnt_ref.shape)
    vals = [(i1 - N_GROUPS).astype(F32), (i2 - N_GROUPS).astype(F32), w1, w2, rank1, rank2]
    out = jnp.zeros(lg.shape, F32)
    for k, val in enumerate(vals):
        out = jnp.where(lane == k, val, out)
    lg_ref[...] = out


def norm_router(xl, xc, nw, mod_l, mod_c, w_router, b_router, tm=512):
    b, L, d = xl.shape
    two_src = xc is not None
    n_lat = b * L // tm
    per_batch = L // tm
    n_ctx = (xc.shape[0] * xc.shape[1]) // tm if two_src else 0
    w_hi, w_lo = _split_bf16(w_router)
    const = lambda i: (0, 0)
    if two_src:
        shift = jnp.concatenate([mod_l[0], mod_c[0]], axis=0)
        scale = jnp.concatenate([mod_l[1], mod_c[1]], axis=0)
        mod_map = lambda i: (jnp.where(i < n_lat, i // per_batch, b), 0, 0)
        srcs = [xl.reshape(b * L, d), xc.reshape(-1, d)]
        src_specs = [pl.BlockSpec((tm, d), lambda i: (jnp.minimum(i, n_lat - 1), 0)),
                     pl.BlockSpec((tm, d), lambda i: (jnp.maximum(i - n_lat, 0), 0))]
    else:
        shift, scale = mod_l
        mod_map = lambda i: (i // per_batch, 0, 0)
        srcs = [xl.reshape(b * L, d)]
        src_specs = [pl.BlockSpec((tm, d), lambda i: (i, 0))]
    n_tok = (n_lat + n_ctx) * tm
    tri = _host_bf16(np.tril(np.ones((tm, tm)), -1))
    return pl.pallas_call(
        functools.partial(_norm_router_kernel, n_lat=n_lat, two_src=two_src),
        out_shape=(jax.ShapeDtypeStruct((n_tok, d), F32), jax.ShapeDtypeStruct((n_tok, ROUTER_COLS), F32),
                   jax.ShapeDtypeStruct((SUBLANE, ROUTER_COLS), F32)),
        grid=(n_lat + n_ctx,),
        in_specs=src_specs + [pl.BlockSpec((1, d), const), pl.BlockSpec((1, 1, d), mod_map),
                              pl.BlockSpec((1, 1, d), mod_map), pl.BlockSpec((d, ROUTER_COLS), const),
                              pl.BlockSpec((d, ROUTER_COLS), const), pl.BlockSpec((1, ROUTER_COLS), const),
                              pl.BlockSpec((tm, tm), const)],
        out_specs=(pl.BlockSpec((tm, d), lambda i: (i, 0)), pl.BlockSpec((tm, ROUTER_COLS), lambda i: (i, 0)),
                   pl.BlockSpec((SUBLANE, ROUTER_COLS), const)),
        compiler_params=_params("arbitrary"),
        name="moe_norm_router",
    )(*srcs, nw.reshape(1, d), shift, scale, w_hi, w_lo, b_router, tri)


def _expert_ffn_kernel(te_ref, tv_ref, x_ref, wg_ref, wu_ref, wd_ref, o_ref):
    i = pl.program_id(0)

    @pl.when(tv_ref[i] > 0)
    def _():
        x = x_ref[...].astype(BF16)
        g = jnp.dot(x, wg_ref[0].astype(BF16), preferred_element_type=F32)
        u = jnp.dot(x, wu_ref[0].astype(BF16), preferred_element_type=F32)
        hid = _silu(g) * u
        o_ref[...] = jnp.dot(hid.astype(BF16), wd_ref[0].astype(BF16), preferred_element_type=F32)

    @pl.when(tv_ref[i] == 0)
    def _():
        o_ref[...] = jnp.zeros_like(o_ref)


def expert_ffn(x_sorted, tile_expert, tile_valid, w_gate, w_up, w_down, tm):
    r, d = x_sorted.shape
    f = w_gate.shape[-1]
    grid_spec = pltpu.PrefetchScalarGridSpec(
        num_scalar_prefetch=2,
        grid=(r // tm,),
        in_specs=[pl.BlockSpec((tm, d), lambda i, te, tv: (i, 0)),
                  pl.BlockSpec((1, d, f), lambda i, te, tv: (te[i], 0, 0)),
                  pl.BlockSpec((1, d, f), lambda i, te, tv: (te[i], 0, 0)),
                  pl.BlockSpec((1, f, d), lambda i, te, tv: (te[i], 0, 0))],
        out_specs=pl.BlockSpec((tm, d), lambda i, te, tv: (i, 0)),
    )
    return pl.pallas_call(
        _expert_ffn_kernel,
        out_shape=jax.ShapeDtypeStruct((r, d), F32),
        grid_spec=grid_spec,
        compiler_params=_params("arbitrary"),
        name="moe_expert_ffn",
    )(tile_expert, tile_valid, x_sorted, w_gate, w_up, w_down)


def moe_apply(h_tokens, routed, counts_row, w_gate, w_up, w_down, layer, n_lat, tm=256):
    t, d = h_tokens.shape
    e_idx, rank = routed[:, 0:2].astype(jnp.int32), routed[:, 4:6].astype(jnp.int32)
    counts = counts_row[0, N_GROUPS:N_GROUPS + N_EXPERTS].astype(jnp.int32)
    n_pairs = 2 * t
    padded = (counts + tm - 1) // tm * tm
    pad_end = jnp.cumsum(padded)
    pad_start = pad_end - padded
    n_rows = n_pairs + N_EXPERTS * tm
    tile_start = jnp.arange(n_rows // tm, dtype=jnp.int32) * tm
    tile_expert = jnp.minimum(jnp.sum((pad_end[None, :] <= tile_start[:, None]).astype(jnp.int32), axis=1),
                              N_EXPERTS - 1)
    tile_valid = (tile_start < pad_end[-1]).astype(jnp.int32)
    onehot = (e_idx[:, :, None] == jnp.arange(N_EXPERTS, dtype=jnp.int32)).astype(jnp.int32)
    pos = jnp.sum(onehot * pad_start, axis=-1) + rank
    j = jnp.arange(tm, dtype=jnp.int32)[None, :]
    fill_key = jnp.where(j < (padded - counts)[:, None], (pad_start + counts)[:, None] + j, n_rows)
    keys = jnp.concatenate([pos.reshape(-1), fill_key.reshape(-1)])
    toks = jnp.concatenate([jnp.arange(n_pairs, dtype=jnp.int32) // 2, jnp.arange(N_EXPERTS * tm, dtype=jnp.int32) % t])
    _, row_token = lax.sort((keys, toks), num_keys=1)
    x_sorted = h_tokens[row_token]
    wg = w_gate.reshape(-1, d, EXPERT_HIDDEN)
    wu = w_up.reshape(-1, d, EXPERT_HIDDEN)
    wd = w_down.reshape(-1, EXPERT_HIDDEN, d)
    y_sorted = expert_ffn(x_sorted, tile_expert + layer * N_EXPERTS, tile_valid, wg, wu, wd, tm)
    lat = (y_sorted[pos[:n_lat, 0]], y_sorted[pos[:n_lat, 1]])
    rest = (y_sorted[pos[n_lat:, 0]], y_sorted[pos[n_lat:, 1]]) if t > n_lat else None
    return lat, rest


def _final_kernel(x_ref, ya_ref, yb_ref, rt_ref, g_ref, w_ref, o_ref):
    x = x_ref[0] + g_ref[0] * _moe_mix(ya_ref[0], yb_ref[0], rt_ref[0])
    o_ref[0] = x * lax.rsqrt(jnp.mean(x * x, axis=-1, keepdims=True) + EPS) * w_ref[...]


def final_norm(x, ya, yb, routed, gate, w, tm=512):
    b, L, d = x.shape
    tok = pl.BlockSpec((1, tm, d), lambda bi, i: (bi, i, 0))
    return pl.pallas_call(
        _final_kernel,
        out_shape=jax.ShapeDtypeStruct((b, L, d), F32),
        grid=(b, L // tm),
        in_specs=[tok, tok, tok, pl.BlockSpec((1, tm, ROUTER_COLS), lambda bi, i: (bi, i, 0)),
                  pl.BlockSpec((1, 1, d), _mod_map(gate, b)), pl.BlockSpec((1, d), lambda bi, i: (0, 0))],
        out_specs=tok,
        compiler_params=_params("parallel", "arbitrary"),
        name="final_rmsnorm",
    )(x, ya, yb, routed, gate, w.reshape(1, d))


IN_SIZES = (SSD_WIDTH, SSD_CONV_CH, 2 * SSD_HEADS, HY_COLS, 2 * ML_WIDTH, ML_WIDTH, ML_WIDTH, 4 * ML_HEADS)


def _regroup_kernel(w_ref, o_ref):
    src = dst = 0
    for n in IN_SIZES:
        pad = -n % LANE
        o_ref[0, :, dst:dst + n] = w_ref[0, :, src:src + n].astype(BF16)
        if pad:
            o_ref[0, :, dst + n:dst + n + pad] = jnp.zeros((o_ref.shape[1], pad), BF16)
        src, dst = src + n, dst + n + pad


def regroup_in_weights(w_in):
    depth, d, n_in = w_in.shape
    n_out = sum(n + (-n % LANE) for n in IN_SIZES)
    one = pl.Buffered(1)
    return pl.pallas_call(
        _regroup_kernel,
        out_shape=jax.ShapeDtypeStruct((depth, d, n_out), BF16),
        grid=(depth,),
        in_specs=[pl.BlockSpec((1, d, n_in), lambda i: (i, 0, 0), pipeline_mode=one)],
        out_specs=pl.BlockSpec((1, d, n_out), lambda i: (i, 0, 0)),
        compiler_params=_params("arbitrary"),
        name="regroup_in_weights",
    )(w_in)


def kernel(x, c, ctx, c_ctx, w_mod, b_mod, norm1_w, norm2_w, w_in, w_out, ssd_conv_w, ssd_conv_b, ssd_dt_bias, ssd_a_log, ssd_d, ssd_norm_w, hy_conv_w, hy_conv_b, hy_pos_w1, hy_pos_b1, hy_pos_w2, hy_pos_b2, hy_pos_w3, hy_freq, hy_decay, hy_skip, ml_conv_w, ml_conv_b, ml_gate_b, ml_norm_w, grp_router_w, grp_router_b, exp_router_w, exp_router_b, moe_w_gate, moe_w_up, moe_w_down, final_norm_w):
    layer_params = dict(
        ssd_conv_w=ssd_conv_w, ssd_conv_b=ssd_conv_b, ssd_dt_bias=ssd_dt_bias, ssd_a_log=ssd_a_log,
        ssd_d=ssd_d, ssd_norm_w=ssd_norm_w, hy_conv_w=hy_conv_w, hy_conv_b=hy_conv_b,
        hy_pos_w1=hy_pos_w1, hy_pos_b1=hy_pos_b1, hy_pos_w2=hy_pos_w2, hy_pos_b2=hy_pos_b2,
        hy_pos_w3=hy_pos_w3, hy_freq=hy_freq, hy_decay=hy_decay, hy_skip=hy_skip,
        ml_conv_w=ml_conv_w, ml_conv_b=ml_conv_b, ml_gate_b=ml_gate_b, ml_norm_w=ml_norm_w)
    bsz, seq, d = x.shape
    n_ctx = ctx.shape[1]
    xl, xc = x, ctx
    moe_l = moe_c = None
    ssd0 = jnp.zeros((bsz, SSD_HEADS, HEAD_DIM, SSD_STATE), F32)
    ml0 = (jnp.zeros((bsz, ML_HEADS, HEAD_DIM, LANE), F32), jnp.zeros((bsz, SUBLANE, LANE), F32))
    c_rows = jnp.concatenate([c, c_ctx[None, :], jnp.zeros((SUBLANE - bsz - 1, d), F32)], axis=0)
    w_in_all = regroup_in_weights(w_in)
    for i in range(DEPTH):
        last = i == DEPTH - 1
        p = {name: arr[i] for name, arr in layer_params.items()}
        sp, mp = ssd_prepare(p), ml_prepare(p)
        mod = modulation(c_rows, w_mod, b_mod, i).reshape(SUBLANE, N_MOD, 1, d)
        mod_l = [mod[:bsz, k] for k in range(N_MOD)]
        mod_c = [mod[bsz:bsz + 1, k] for k in range(N_MOD)]
        w_out_b = w_out[i].astype(BF16)
        w_router = jnp.pad(jnp.concatenate([grp_router_w[i], exp_router_w[i]], axis=1),
                           ((0, 0), (0, ROUTER_COLS - N_GROUPS - N_EXPERTS)))
        b_router = jnp.pad(jnp.concatenate([grp_router_b[i], exp_router_b[i]]),
                           (0, ROUTER_COLS - N_GROUPS - N_EXPERTS)).reshape(1, ROUTER_COLS)

        uc_ssd, uc_hy, uc_ml, xc = norm_proj(xc, moe_c, norm1_w[i], mod_c[0], mod_c[1], w_in_all, i)
        yc_ssd, ssd_f, ssd_b = ssd_mixer(uc_ssd, sp, ssd0, ssd0, not last)
        yc_ml, ml_f, ml_b = ml_mixer(uc_ml, mp, ml0, ml0, not last)
        col_major = i % 2 == 1
        ul_ssd, ul_hy, ul_ml, xl = norm_proj(xl, moe_l, norm1_w[i], mod_l[0], mod_l[1], w_in_all, i, col_major)
        yl_ssd, _, _ = ssd_mixer(ul_ssd, sp, ssd_f, ssd_b, True)
        yl_ml, _, _ = ml_mixer(ul_ml, mp, ml_f, ml_b, True)
        h_spec = hyena_filter_spectra(*hyena_filter_signals(seq, p))
        yl_hy = hyena_long(ul_hy, h_spec, p['hy_conv_w'], p['hy_conv_b'], p['hy_skip'])
        xl = out_proj(yl_ssd, yl_hy, yl_ml, xl, mod_l[2], w_out_b, col_major)
        if not last:
            sig_c, inorm_c = hyena_filter_signals(n_ctx, p)
            yc_hy = hyena_short(uc_hy, sig_c, inorm_c, p['hy_conv_w'], p['hy_conv_b'], p['hy_skip'])
            xc = out_proj(yc_ssd, yc_hy, yc_ml, xc, mod_c[2], w_out_b)
        h_all, routed, counts_row = norm_router(xl, None if last else xc, norm2_w[i], (mod_l[3], mod_l[4]), (mod_c[3], mod_c[4]),
                                    w_router, b_router)
        lat, rest = moe_apply(h_all, routed, counts_row, moe_w_gate, moe_w_up, moe_w_down, i, bsz * seq)
        n_lat = bsz * seq
        moe_l = (lat[0].reshape(bsz, seq, d), lat[1].reshape(bsz, seq, d),
                 routed[:n_lat].reshape(bsz, seq, ROUTER_COLS), mod_l[5])
        if not last:
            moe_c = (rest[0].reshape(bsz, n_ctx, d), rest[1].reshape(bsz, n_ctx, d),
                     routed[n_lat:].reshape(bsz, n_ctx, ROUTER_COLS), mod_c[5])
    return final_norm(xl, *moe_l, final_norm_w)
```

```python
import functools
import math

import jax
import jax.numpy as jnp
import numpy as np
from jax import lax
from jax.experimental import pallas as pl
from jax.experimental.pallas import tpu as pltpu

D_MODEL = 1024
DEPTH = 2
GRID_W = 64
HEAD_DIM = 64
SSD_WIDTH = 384
SSD_HEADS = SSD_WIDTH // HEAD_DIM
SSD_GROUPS = 2
SSD_STATE = 64
HYENA_WIDTH = 256
HYENA_ORDER = 2
HYENA_BANDS = 16
ML_WIDTH = 384
ML_HEADS = ML_WIDTH // HEAD_DIM
N_GROUPS = 4
EXPERTS_PER_GROUP = 8
N_EXPERTS = N_GROUPS * EXPERTS_PER_GROUP
EXPERT_HIDDEN = 256
N_MOD = 6
EPS = 1e-6

LANE = 128
SUBLANE = 8
VMEM_LIMIT = 48 * 1024 * 1024
VMEM_LIMIT_HY = 56 * 1024 * 1024

SSD_CONV_CH = SSD_WIDTH + 2 * SSD_GROUPS * SSD_STATE
SSD_XBC0 = SSD_WIDTH
SSD_DT0 = SSD_XBC0 + SSD_CONV_CH
SSD_COLS = SSD_DT0 + LANE
HY_COLS = (HYENA_ORDER + 1) * HYENA_WIDTH
ML_V0 = 2 * ML_WIDTH
ML_O0 = ML_V0 + ML_WIDTH
ML_G0 = ML_O0 + ML_WIDTH
ML_COLS = ML_G0 + LANE
ROUTER_COLS = LANE

F32 = jnp.float32
BF16 = jnp.bfloat16
HI = lax.Precision.HIGHEST


def _params(*sem):
    return pltpu.CompilerParams(dimension_semantics=sem, vmem_limit_bytes=VMEM_LIMIT)


def _host_bf16(a):
    return jnp.asarray(np.asarray(a, np.float32).astype(BF16))


def _silu(x):
    return x * jax.nn.sigmoid(x)


def _softplus(x):
    return jnp.maximum(x, 0.0) + jnp.log(1.0 + jnp.exp(-jnp.abs(x)))


def _log_sigmoid(x):
    return jnp.minimum(x, 0.0) - jnp.log(1.0 + jnp.exp(-jnp.abs(x)))


def _mod_kernel(c_ref, w_ref, b_ref, o_ref):
    o_ref[...] = jnp.dot(_silu(c_ref[...]), w_ref[...], preferred_element_type=F32, precision=HI) + b_ref[...]


def modulation(c_rows, w_mod, b_mod, layer):
    depth, d, n = w_mod.shape
    tn = 1536
    return pl.pallas_call(
        _mod_kernel,
        out_shape=jax.ShapeDtypeStruct((c_rows.shape[0], n), F32),
        grid=(n // tn,),
        in_specs=[pl.BlockSpec(c_rows.shape, lambda j: (0, 0)),
                  pl.BlockSpec((None, d, tn), lambda j: (layer, 0, j)),
                  pl.BlockSpec((None, 1, tn), lambda j: (layer, 0, j))],
        out_specs=pl.BlockSpec((c_rows.shape[0], tn), lambda j: (0, j)),
        compiler_params=_params("arbitrary"),
        name="adaln_modulation",
    )(c_rows, w_mod, b_mod.reshape(depth, 1, n))


def _modnorm(x, nw, shift, scale):
    y = x * lax.rsqrt(jnp.mean(x * x, axis=-1, keepdims=True) + EPS) * nw
    return y * (1.0 + scale) + shift


def _mod_map(mod, b):
    return (lambda bi, i: (bi, 0, 0)) if mod.shape[0] == b else (lambda bi, i: (0, 0, 0))


def _tok_view(x, col_major):
    b, L, d = x.shape
    return x.reshape(b, L // GRID_W, GRID_W, d) if col_major else x


def _tok_spec(L, d, tm, col_major):
    if col_major:
        assert tm == (L // GRID_W) * SUBLANE
        return pl.BlockSpec((1, L // GRID_W, SUBLANE, d), lambda bi, i: (bi, 0, i, 0))
    return pl.BlockSpec((1, tm, d), lambda bi, i: (bi, i, 0))


def _tok_load(ref, col_major):
    if not col_major:
        return ref[0]
    return jnp.concatenate([ref[0, :, j, :] for j in range(ref.shape[2])], axis=0)


def _tok_store(ref, val, col_major):
    if not col_major:
        ref[0] = val
        return
    rows = ref.shape[1]
    for j in range(ref.shape[2]):
        ref[0, :, j, :] = val[j * rows:(j + 1) * rows]


def _moe_mix(ya, yb, routed):
    return routed[:, 2:3] * ya + routed[:, 3:4] * yb


def _norm_proj_kernel(*refs, col_major, fuse_moe):
    if fuse_moe:
        x_ref, ya_ref, yb_ref, rt_ref, g_ref, nw_ref, sh_ref, sc_ref, w_ref, ssd_ref, hy_ref, ml_ref, xo_ref = refs
    else:
        x_ref, nw_ref, sh_ref, sc_ref, w_ref, ssd_ref, hy_ref, ml_ref = refs
    x = _tok_load(x_ref, col_major)
    if fuse_moe:
        x = x + g_ref[0] * _moe_mix(_tok_load(ya_ref, col_major), _tok_load(yb_ref, col_major),
                                    _tok_load(rt_ref, col_major))
        _tok_store(xo_ref, x, col_major)
    h = _modnorm(x, nw_ref[...], sh_ref[0], sc_ref[0])
    u = jnp.dot(h.astype(BF16), w_ref[...], preferred_element_type=F32)
    ssd_ref[0] = u[:, 0:SSD_COLS]
    hy_ref[0] = u[:, SSD_COLS:SSD_COLS + HY_COLS]
    ml_ref[0] = u[:, SSD_COLS + HY_COLS:]


def norm_proj(x, moe, nw, shift, scale, w_all, layer, col_major=False, tm=512):
    b, L, d = x.shape
    n = w_all.shape[2]
    fuse_moe = moe is not None
    tm = (L // GRID_W) * SUBLANE if col_major else min(tm, L)
    tok = _tok_spec(L, d, tm, col_major)
    row = lambda bi, i: (bi, i, 0)
    const2 = lambda bi, i: (0, 0)
    args, in_specs = [_tok_view(x, col_major)], [tok]
    if fuse_moe:
        ya, yb, routed, gate = moe
        args += [_tok_view(ya, col_major), _tok_view(yb, col_major), _tok_view(routed, col_major), gate]
        in_specs += [tok, tok, _tok_spec(L, ROUTER_COLS, tm, col_major), pl.BlockSpec((1, 1, d), _mod_map(gate, b))]
    args += [nw.reshape(1, d), shift, scale, w_all]
    in_specs += [pl.BlockSpec((1, d), const2), pl.BlockSpec((1, 1, d), _mod_map(shift, b)),
                 pl.BlockSpec((1, 1, d), _mod_map(scale, b)),
                 pl.BlockSpec((None, d, n), lambda bi, i: (layer, 0, 0))]
    out_shape = [jax.ShapeDtypeStruct((b, L, SSD_COLS), F32), jax.ShapeDtypeStruct((b, L, HY_COLS), F32),
                 jax.ShapeDtypeStruct((b, L, ML_COLS), F32)]
    out_specs = [pl.BlockSpec((1, tm, SSD_COLS), row), pl.BlockSpec((1, tm, HY_COLS), row),
                 pl.BlockSpec((1, tm, ML_COLS), row)]
    if fuse_moe:
        out_shape.append(jax.ShapeDtypeStruct(args[0].shape, F32))
        out_specs.append(tok)
    outs = pl.pallas_call(
        functools.partial(_norm_proj_kernel, col_major=col_major, fuse_moe=fuse_moe),
        out_shape=tuple(out_shape),
        grid=(b, L // tm),
        in_specs=in_specs,
        out_specs=tuple(out_specs),
        compiler_params=_params("parallel", "arbitrary"),
        name="norm_in_proj",
    )(*args)
    return (*outs[:3], outs[3].reshape(b, L, d) if fuse_moe else x)


def _out_proj_kernel(ys_ref, yh_ref, ym_ref, x_ref, g_ref, w_ref, o_ref, *, col_major):
    y = jnp.concatenate([ys_ref[0], yh_ref[0], ym_ref[0]], axis=-1).astype(BF16)
    r = _tok_load(x_ref, col_major) + g_ref[0] * jnp.dot(y, w_ref[...], preferred_element_type=F32)
    _tok_store(o_ref, r, col_major)


def out_proj(y_ssd, y_hy, y_ml, x, gate, w_bf16, col_major=False, tm=512):
    b, L, d = x.shape
    tm = (L // GRID_W) * SUBLANE if col_major else min(tm, L)
    row = lambda bi, i: (bi, i, 0)
    tok = _tok_spec(L, d, tm, col_major)
    xv = _tok_view(x, col_major)
    return pl.pallas_call(
        functools.partial(_out_proj_kernel, col_major=col_major),
        out_shape=jax.ShapeDtypeStruct(xv.shape, F32),
        grid=(b, L // tm),
        in_specs=[pl.BlockSpec((1, tm, SSD_WIDTH), row), pl.BlockSpec((1, tm, HYENA_WIDTH), row),
                  pl.BlockSpec((1, tm, ML_WIDTH), row), tok,
                  pl.BlockSpec((1, 1, d), _mod_map(gate, b)),
                  pl.BlockSpec(w_bf16.shape, lambda bi, i: (0, 0))],
        out_specs=tok,
        compiler_params=_params("parallel", "arbitrary"),
        name="out_proj_residual",
    )(y_ssd, y_hy, y_ml, xv, gate, w_bf16).reshape(b, L, d)


def _conv3(xr, prev_row, next_row, cw, cb, q):
    rid = lax.broadcasted_iota(jnp.int32, (q, 1), 0)
    x_prev = jnp.where(rid == 0, prev_row, pltpu.roll(xr, 1, axis=0))
    x_next = jnp.where(rid == q - 1, next_row, pltpu.roll(xr, q - 1, axis=0))
    return x_prev * cw[0:1] + xr * cw[1:2] + x_next * cw[2:3] + cb


def _scan_mask(q, direction):
    li = lax.broadcasted_iota(jnp.int32, (q, q), 0)
    si = lax.broadcasted_iota(jnp.int32, (q, q), 1)
    return (si <= li) if direction == 0 else (si >= li)


def _running_max(x, direction, q):
    rid = lax.broadcasted_iota(jnp.int32, (q, 1), 0)
    s = 1
    while s < q:
        if direction == 0:
            x = jnp.where(rid >= s, jnp.maximum(x, pltpu.roll(x, s, axis=0)), x)
        else:
            x = jnp.where(rid < q - s, jnp.maximum(x, pltpu.roll(x, q - s, axis=0)), x)
        s *= 2
    return x


def _scan_specs(L, q, nc, cols, direction):
    hb = q // SUBLANE
    nrb = L // SUBLANE
    cidx = (lambda j: j) if direction == 0 else (lambda j: nc - 1 - j)
    specs = [pl.BlockSpec((1, q, cols), lambda bi, j: (bi, cidx(j), 0)),
             pl.BlockSpec((1, SUBLANE, cols), lambda bi, j: (bi, jnp.maximum(cidx(j) * hb - 1, 0), 0)),
             pl.BlockSpec((1, SUBLANE, cols), lambda bi, j: (bi, jnp.minimum((cidx(j) + 1) * hb, nrb - 1), 0))]
    return specs, cidx


def _ssd_kernel(*refs, direction, finalize, q, nc):
    if finalize:
        (u_ref, prev_ref, next_ref, yb_ref, init_ref, cw_ref, cb_ref, dtb_ref, a_ref, d_ref, nw_ref,
         y_ref, fin_ref, state_ref) = refs
    else:
        (u_ref, prev_ref, next_ref, init_ref, cw_ref, cb_ref, dtb_ref, a_ref,
         y_ref, fin_ref, state_ref) = refs
    j = pl.program_id(1)
    c = j if direction == 0 else nc - 1 - j

    @pl.when(j == 0)
    def _():
        state_ref[...] = init_ref[0]

    prev_row = jnp.where(c > 0, prev_ref[0, SUBLANE - 1:SUBLANE, SSD_XBC0:SSD_DT0], 0.0)
    next_row = jnp.where(c < nc - 1, next_ref[0, 0:1, SSD_XBC0:SSD_DT0], 0.0)
    xc = _silu(_conv3(u_ref[0, :, SSD_XBC0:SSD_DT0], prev_row, next_row, cw_ref[...], cb_ref[...], q))

    dt = _softplus(u_ref[0, :, SSD_DT0:SSD_COLS] + dtb_ref[...])
    mask = _scan_mask(q, direction)
    cum = jnp.dot(mask.astype(F32), dt * a_ref[...], preferred_element_type=F32, precision=HI)
    cum_t = cum.T
    end = q - 1 if direction == 0 else 0

    ys = []
    for g in range(SSD_GROUPS):
        b0 = SSD_WIDTH + g * SSD_STATE
        c0 = SSD_WIDTH + (SSD_GROUPS + g) * SSD_STATE
        bm_t = xc[:, b0:b0 + SSD_STATE].T
        cm = xc[:, c0:c0 + SSD_STATE].astype(BF16)
        scores = jnp.dot(cm, bm_t.astype(BF16), preferred_element_type=F32)
        for h in range(g * (SSD_HEADS // SSD_GROUPS), (g + 1) * (SSD_HEADS // SSD_GROUPS)):
            hl = direction * SSD_HEADS + h
            col = cum[:, hl:hl + 1]
            row = cum_t[hl:hl + 1, :]
            seg = jnp.exp(jnp.where(mask, col - row, -jnp.inf))
            xdt = (xc[:, h * HEAD_DIM:(h + 1) * HEAD_DIM] * dt[:, hl:hl + 1]).astype(BF16)
            y = jnp.dot((scores * seg).astype(BF16), xdt, preferred_element_type=F32)
            st = state_ref[h]
            y = y + jnp.dot(cm, st.astype(BF16), preferred_element_type=F32) * jnp.exp(col)
            tot = cum[end:end + 1, hl:hl + 1]
            upd = jnp.dot((bm_t * jnp.exp(tot - row)).astype(BF16), xdt, preferred_element_type=F32)
            state_ref[h] = st * jnp.exp(tot) + upd
            ys.append(y)
    y_all = jnp.concatenate(ys, axis=-1)
    if finalize:
        t = (y_all + yb_ref[0] + xc[:, 0:SSD_WIDTH] * d_ref[...]) * _silu(u_ref[0, :, 0:SSD_WIDTH])
        y_all = t * lax.rsqrt(jnp.mean(t * t, axis=-1, keepdims=True) + EPS) * nw_ref[...]
    y_ref[0] = y_all

    @pl.when(j == nc - 1)
    def _():
        fin_ref[0] = state_ref[...]


def ssd_pass(u, y_other, init, sp, direction, q):
    b, L, _ = u.shape
    q = min(q, L)
    nc = L // q
    finalize = y_other is not None
    in_specs, cidx = _scan_specs(L, q, nc, SSD_COLS, direction)
    const2 = lambda bi, j: (0, 0)
    st_spec = pl.BlockSpec((1, SSD_HEADS, HEAD_DIM, SSD_STATE), lambda bi, j: (bi, 0, 0, 0))
    y_spec = pl.BlockSpec((1, q, SSD_WIDTH), lambda bi, j: (bi, cidx(j), 0))
    args = [u, u, u]
    if finalize:
        in_specs.append(y_spec)
        args.append(y_other)
    consts = [sp['cw'], sp['cb'], sp['dtb'], sp['a']] + ([sp['d'], sp['nw']] if finalize else [])
    in_specs += [st_spec] + [pl.BlockSpec(t.shape, const2) for t in consts]
    args += [init] + consts
    return pl.pallas_call(
        functools.partial(_ssd_kernel, direction=direction, finalize=finalize, q=q, nc=nc),
        out_shape=(jax.ShapeDtypeStruct((b, L, SSD_WIDTH), F32),
                   jax.ShapeDtypeStruct((b, SSD_HEADS, HEAD_DIM, SSD_STATE), F32)),
        grid=(b, nc),
        in_specs=in_specs,
        out_specs=(y_spec, st_spec),
        scratch_shapes=[pltpu.VMEM((SSD_HEADS, HEAD_DIM, SSD_STATE), F32)],
        compiler_params=_params("parallel", "arbitrary"),
        name="ssd_scan_%s" % ("fwd" if direction == 0 else "bwd"),
    )(*args)


def ssd_prepare(p):
    pad = lambda v: jnp.pad(v.reshape(1, -1), ((0, 0), (0, LANE - v.size)))
    return dict(cw=p['ssd_conv_w'], cb=p['ssd_conv_b'].reshape(1, -1),
                dtb=pad(p['ssd_dt_bias']), a=pad(-jnp.exp(p['ssd_a_log'])),
                d=jnp.repeat(p['ssd_d'], HEAD_DIM).reshape(1, -1), nw=p['ssd_norm_w'].reshape(1, -1))


def ssd_mixer(u, sp, init_f, init_b, want_y, q=256):
    yb, fin_b = ssd_pass(u, None, init_b, sp, 1, q)
    y, fin_f = ssd_pass(u, yb if want_y else None, init_f, sp, 0, q)
    return y, fin_f, fin_b


def _ml_kernel(*refs, direction, finalize, q, nc):
    if finalize:
        (u_ref, prev_ref, next_ref, hb_ref, s_init_ref, m_init_ref, cw_ref, cb_ref, gb_ref, nw_ref, pool_ref,
         y_ref, s_fin_ref, m_fin_ref, s_ref, m_ref) = refs
    else:
        (u_ref, prev_ref, next_ref, s_init_ref, m_init_ref, cw_ref, cb_ref, gb_ref,
         y_ref, s_fin_ref, m_fin_ref, s_ref, m_ref) = refs
    j = pl.program_id(1)
    c = j if direction == 0 else nc - 1 - j

    @pl.when(j == 0)
    def _():
        s_ref[...] = s_init_ref[0]
        m_ref[...] = m_init_ref[0]

    prev_row = jnp.where(c > 0, prev_ref[0, SUBLANE - 1:SUBLANE, 0:ML_V0], 0.0)
    next_row = jnp.where(c < nc - 1, next_ref[0, 0:1, 0:ML_V0], 0.0)
    qk = _silu(_conv3(u_ref[0, :, 0:ML_V0], prev_row, next_row, cw_ref[...], cb_ref[...], q))
    v = u_ref[0, :, ML_V0:ML_O0]

    gb = u_ref[0, :, ML_G0:ML_COLS] + gb_ref[...]
    mask = _scan_mask(q, direction)
    cum = jnp.dot(mask.astype(F32), _log_sigmoid(gb), preferred_element_type=F32, precision=HI)
    ig = pltpu.roll(gb, ML_HEADS, axis=1)
    end = q - 1 if direction == 0 else 0
    m_prev = m_ref[0:1, :]
    tot = cum[end:end + 1, :]
    w_end = tot - cum + ig
    m_loc = jnp.max(w_end, axis=0, keepdims=True)
    e_end = jnp.exp(w_end - m_loc)
    m_new = jnp.maximum(tot + m_prev, m_loc)
    a_prev = jnp.exp(tot + m_prev - m_new)
    a_loc = jnp.exp(m_loc - m_new)
    inter = cum + m_prev
    rel = ig - cum
    m_t = jnp.maximum(inter, cum + _running_max(rel, direction, q))
    col_a = cum - m_t
    a_inter = jnp.exp(inter - m_t)
    floor = jnp.exp(-m_t)
    rel_t = rel.T
    e_end_t = e_end.T
    k_t = (qk[:, ML_WIDTH:2 * ML_WIDTH] * (HEAD_DIM ** -0.5)).T
    one_col = (lax.broadcasted_iota(jnp.int32, (q, HEAD_DIM), 1) == 0).astype(F32)

    ys = []
    for h in range(ML_HEADS):
        fl = direction * 2 * ML_HEADS + ML_HEADS + h
        qh = qk[:, h * HEAD_DIM:(h + 1) * HEAD_DIM].astype(BF16)
        kh_t = k_t[h * HEAD_DIM:(h + 1) * HEAD_DIM, :]
        v_ext = jnp.concatenate([v[:, h * HEAD_DIM:(h + 1) * HEAD_DIM], one_col], axis=-1).astype(BF16)
        pw = jnp.exp(jnp.where(mask, col_a[:, fl:fl + 1] + rel_t[fl:fl + 1, :], -jnp.inf))
        scores = jnp.dot(qh, kh_t.astype(BF16), preferred_element_type=F32)
        nd = jnp.dot((scores * pw).astype(BF16), v_ext, preferred_element_type=F32)
        st = s_ref[h]
        nd = nd + a_inter[:, fl:fl + 1] * jnp.dot(qh, st.astype(BF16), preferred_element_type=F32)
        den = nd[:, HEAD_DIM:HEAD_DIM + 1]
        ys.append(nd[:, 0:HEAD_DIM] / jnp.maximum(jnp.abs(den), floor[:, fl:fl + 1]))
        upd = jnp.dot((kh_t * e_end_t[fl:fl + 1, :]).astype(BF16), v_ext, preferred_element_type=F32)
        s_ref[h] = a_prev[:, fl:fl + 1] * st + a_loc[:, fl:fl + 1] * upd
    m_ref[...] = jnp.broadcast_to(m_new, m_ref.shape)
    y_all = jnp.concatenate(ys, axis=-1)
    if finalize:
        hs = y_all + hb_ref[0]
        hc = hs - jnp.dot(hs.astype(BF16), pool_ref[...], preferred_element_type=F32)
        var = jnp.dot((hc * hc).astype(BF16), pool_ref[...], preferred_element_type=F32)
        y_all = hc * lax.rsqrt(var + EPS) * nw_ref[...] * jax.nn.sigmoid(u_ref[0, :, ML_O0:ML_G0])
    y_ref[0] = y_all

    @pl.when(j == nc - 1)
    def _():
        s_fin_ref[0] = s_ref[...]
        m_fin_ref[0] = m_ref[...]


def ml_pass(u, h_other, init, mp, direction, q):
    b, L, _ = u.shape
    q = min(q, L)
    nc = L // q
    finalize = h_other is not None
    in_specs, cidx = _scan_specs(L, q, nc, ML_COLS, direction)
    const2 = lambda bi, j: (0, 0)
    s_spec = pl.BlockSpec((1, ML_HEADS, HEAD_DIM, LANE), lambda bi, j: (bi, 0, 0, 0))
    m_spec = pl.BlockSpec((1, SUBLANE, LANE), lambda bi, j: (bi, 0, 0))
    y_spec = pl.BlockSpec((1, q, ML_WIDTH), lambda bi, j: (bi, cidx(j), 0))
    args = [u, u, u]
    if finalize:
        in_specs.append(y_spec)
        args.append(h_other)
    consts = [mp['cw'], mp['cb'], mp['gb']] + ([mp['nw'], mp['pool']] if finalize else [])
    in_specs += [s_spec, m_spec] + [pl.BlockSpec(t.shape, const2) for t in consts]
    args += [init[0], init[1]] + consts
    y, s_fin, m_fin = pl.pallas_call(
        functools.partial(_ml_kernel, direction=direction, finalize=finalize, q=q, nc=nc),
        out_shape=(jax.ShapeDtypeStruct((b, L, ML_WIDTH), F32),
                   jax.ShapeDtypeStruct((b, ML_HEADS, HEAD_DIM, LANE), F32),
                   jax.ShapeDtypeStruct((b, SUBLANE, LANE), F32)),
        grid=(b, nc),
        in_specs=in_specs,
        out_specs=(y_spec, s_spec, m_spec),
        scratch_shapes=[pltpu.VMEM((ML_HEADS, HEAD_DIM, LANE), F32), pltpu.VMEM((SUBLANE, LANE), F32)],
        compiler_params=_params("parallel", "arbitrary"),
        name="mlstm_scan_%s" % ("fwd" if direction == 0 else "bwd"),
    )(*args)
    return y, (s_fin, m_fin)


def ml_prepare(p):
    gb = p['ml_gate_b'].reshape(1, -1)
    head = np.arange(ML_WIDTH) // HEAD_DIM
    pool = _host_bf16((head[:, None] == head[None, :]) / HEAD_DIM)
    return dict(cw=p['ml_conv_w'], cb=p['ml_conv_b'].reshape(1, -1),
                gb=jnp.pad(gb, ((0, 0), (0, LANE - gb.shape[1]))), nw=p['ml_norm_w'].reshape(1, -1), pool=pool)


def ml_mixer(u, mp, init_f, init_b, want_y, q=256):
    hb, fin_b = ml_pass(u, None, init_b, mp, 1, q)
    y, fin_f = ml_pass(u, hb if want_y else None, init_f, mp, 0, q)
    return y, fin_f, fin_b


FFT_L = 4096
FFT_N = 2 * FFT_L
FFT_N2 = 128
FFT_N1 = FFT_N // FFT_N2
FFT_N1H = FFT_L // FFT_N2
FFT_K1 = FFT_N1 // 2 + 1
FFT_R = 80
FFT_UNROLL = 16
FFT_K1_UNROLL = 11
FFT_PITCH = FFT_N2 + SUBLANE


def _fft_tables():
    n2 = np.arange(FFT_N2)[:, None, None]
    k1 = np.arange(FFT_K1)[None, :, None]
    n1 = np.arange(FFT_N1H)[None, None, :]
    th = 2 * np.pi * (((FFT_N2 * n1 + n2) * k1) % FFT_N) / FFT_N
    f1 = np.zeros((FFT_N2, FFT_R, FFT_N1H))
    f1[:, 0:2 * FFT_K1:2, :] = np.cos(th)
    f1[:, 1:2 * FFT_K1:2, :] = -np.sin(th)
    wgt = np.where((np.arange(FFT_K1) == 0) | (np.arange(FFT_K1) == FFT_N1 // 2), 1.0, 2.0)[None, :, None] / FFT_N
    g1 = np.zeros((FFT_N2, FFT_N1H, FFT_R))
    g1[:, :, 0:2 * FFT_K1:2] = np.transpose(wgt * np.cos(th), (0, 2, 1))
    g1[:, :, 1:2 * FFT_K1:2] = np.transpose(-wgt * np.sin(th), (0, 2, 1))
    ph = 2 * np.pi * ((np.arange(FFT_N2)[:, None] * np.arange(FFT_N2)[None, :]) % FFT_N2) / FFT_N2
    c, s = np.cos(ph), np.sin(ph)
    f2 = np.block([[c, s], [-s, c]])
    sign = np.where(np.arange(FFT_R) // 2 % 2 == 0, 1.0, -1.0)[None, :, None]
    f1ab = np.concatenate([f1, f1 * sign], axis=2)
    return _host_bf16(f1), _host_bf16(f1ab), _host_bf16(f2), _host_bf16(f2.T), _host_bf16(g1)


def _sld(ref, n2, count):
    rows = pl.ds(n2, count, stride=FFT_PITCH)
    return jnp.concatenate([ref[0, rows, :], ref[1, rows, :]], axis=-1)


def _sst(ref, n2, count, val):
    rows = pl.ds(n2, count, stride=FFT_PITCH)
    ref[0, rows, :] = val[:, 0:LANE]
    ref[1, rows, :] = val[:, LANE:2 * LANE]


def _blk_ld(ref, blk, nblk):
    parts = []
    for k in range(nblk):
        rows = pl.ds(pl.multiple_of((blk + k) * FFT_PITCH, SUBLANE), FFT_N2)
        parts.append(jnp.concatenate([ref[0, rows, :], ref[1, rows, :]], axis=-1))
    return parts[0] if nblk == 1 else jnp.concatenate(parts, axis=0)


def _blk_st(ref, blk, nblk, val):
    for k in range(nblk):
        rows = pl.ds(pl.multiple_of((blk + k) * FFT_PITCH, SUBLANE), FFT_N2)
        ref[0, rows, :] = val[k * FFT_N2:(k + 1) * FFT_N2, 0:LANE]
        ref[1, rows, :] = val[k * FFT_N2:(k + 1) * FFT_N2, LANE:2 * LANE]


def _fft_stage1(z_ref, a_ref, f1_ref, n_in=FFT_N1H):
    def body(n2, carry):
        xs = _sld(z_ref, n2, n_in).astype(BF16)
        _sst(a_ref, n2, FFT_R, jnp.dot(f1_ref[n2], xs, preferred_element_type=F32))
        return carry
    lax.fori_loop(0, FFT_N2, body, 0, unroll=FFT_UNROLL)


def _spectrum_kernel(x_ref, inorm_ref, f1_ref, f2_ref, o_ref, z_ref, a_ref):
    for half in range(2):
        for n1 in range(FFT_N1H):
            _blk_st(z_ref, half * FFT_N1H + n1, 1, x_ref[half, n1 * FFT_N2:(n1 + 1) * FFT_N2, :])
    _fft_stage1(z_ref, a_ref, f1_ref, 2 * FFT_N1H)
    inorm = inorm_ref[0]

    def body(k1, carry):
        slab = _blk_ld(a_ref, 2 * k1, 2).astype(BF16)
        o_ref[0, k1] = (jnp.dot(f2_ref[...], slab, preferred_element_type=F32) * inorm).astype(BF16)
        return carry
    lax.fori_loop(0, FFT_K1, body, 0, unroll=FFT_K1_UNROLL)


def hyena_filter_spectra(sig, inv_norm):
    _, L, c = sig.shape
    _, f1ab, f2, _, _ = _fft_tables()
    one = pl.Buffered(1)
    return pl.pallas_call(
        _spectrum_kernel,
        out_shape=jax.ShapeDtypeStruct((HYENA_ORDER, FFT_K1, 2 * FFT_N2, c), BF16),
        grid=(HYENA_ORDER,),
        in_specs=[pl.BlockSpec((2, L, c), lambda i: (i, 0, 0)),
                  pl.BlockSpec((1, 1, c), lambda i: (i, 0, 0)),
                  pl.BlockSpec(f1ab.shape, lambda i: (0, 0, 0), pipeline_mode=one),
                  pl.BlockSpec(f2.shape, lambda i: (0, 0), pipeline_mode=one)],
        out_specs=pl.BlockSpec((1, FFT_K1, 2 * FFT_N2, c), lambda i: (i, 0, 0, 0)),
        scratch_shapes=[pltpu.VMEM((2, 2 * FFT_N1H * FFT_PITCH, LANE), F32),
                        pltpu.VMEM((2, FFT_R * FFT_PITCH, LANE), F32)],
        compiler_params=pltpu.CompilerParams(dimension_semantics=("arbitrary",), vmem_limit_bytes=VMEM_LIMIT_HY),
        name="hyena_filter_spectrum",
    )(sig, inv_norm, f1ab, f2)


def _conv3_rows(src, dst, cw, cb, L):
    rows = 2 * FFT_N2
    nchunk = L // rows

    def body(i, carry):
        r0 = pl.multiple_of(i * rows, rows)
        prev_row = jnp.where(i > 0, src[pl.ds(jnp.maximum(r0 - 1, 0), 1), :], 0.0)
        next_row = jnp.where(i < nchunk - 1, src[pl.ds(jnp.minimum(r0 + rows, L - 1), 1), :], 0.0)
        _blk_st(dst, 2 * i, 2, _conv3(src[pl.ds(r0, rows), :], prev_row, next_row, cw, cb, rows))
        return carry
    lax.fori_loop(0, nchunk, body, 0, unroll=2)


def _hyena_kernel(v_ref, g_ref, h_ref, cwv_ref, cbv_ref, cwg_ref, cbg_ref, skip_ref,
                  f1_ref, f2_ref, f2t_ref, g1_ref, o_ref, z_ref, gc_ref, a_ref):
    order = pl.program_id(1)

    @pl.when(order == 0)
    def _():
        _conv3_rows(v_ref.at[0], z_ref, cwv_ref[...], cbv_ref[...], FFT_L)

    _conv3_rows(g_ref.at[0], gc_ref, cwg_ref[0], cbg_ref[0], FFT_L)
    _fft_stage1(z_ref, a_ref, f1_ref)

    def mid(k1, carry):
        x = jnp.dot(f2_ref[...], _blk_ld(a_ref, 2 * k1, 2).astype(BF16), preferred_element_type=F32)
        h = h_ref[0, k1].astype(F32)
        xr, xi, hr, hi = x[:FFT_N2], x[FFT_N2:], h[:FFT_N2], h[FFT_N2:]
        y = jnp.concatenate([xr * hr - xi * hi, xr * hi + xi * hr], axis=0).astype(BF16)
        _blk_st(a_ref, 2 * k1, 2, jnp.dot(f2t_ref[...], y, preferred_element_type=F32))
        return carry
    lax.fori_loop(0, FFT_K1, mid, 0, unroll=FFT_K1_UNROLL)

    skip = skip_ref[0]

    def last(n2, carry):
        bs = _sld(a_ref, n2, FFT_R).astype(BF16)
        y = jnp.dot(g1_ref[n2], bs, preferred_element_type=F32)
        _sst(z_ref, n2, FFT_N1H, _sld(gc_ref, n2, FFT_N1H) * (y + skip * _sld(z_ref, n2, FFT_N1H)))
        return carry
    lax.fori_loop(0, FFT_N2, last, 0, unroll=FFT_UNROLL)

    @pl.when(order == 1)
    def _():
        for n1 in range(FFT_N1H):
            o_ref[0, n1 * FFT_N2:(n1 + 1) * FFT_N2, :] = _blk_ld(z_ref, n1, 1)


def hyena_long(u, h_spec, cw, cb, skip):
    b, L, _ = u.shape
    c = HYENA_WIDTH
    f1, _, f2, f2t, g1 = _fft_tables()
    one = pl.Buffered(1)
    cw3 = cw.reshape(3, 3, c).transpose(1, 0, 2)
    cb3 = cb.reshape(3, 1, c)
    return pl.pallas_call(
        _hyena_kernel,
        out_shape=jax.ShapeDtypeStruct((b, L, c), F32),
        grid=(b, 2),
        in_specs=[pl.BlockSpec((1, L, c), lambda bi, o: (bi, 0, 0), pipeline_mode=one),
                  pl.BlockSpec((1, L, c), lambda bi, o: (bi, 0, 1 + o)),
                  pl.BlockSpec((1, FFT_K1, 2 * FFT_N2, c), lambda bi, o: (o, 0, 0, 0)),
                  pl.BlockSpec((3, c), lambda bi, o: (0, 0)),
                  pl.BlockSpec((1, c), lambda bi, o: (0, 0)),
                  pl.BlockSpec((1, 3, c), lambda bi, o: (1 + o, 0, 0)),
                  pl.BlockSpec((1, 1, c), lambda bi, o: (1 + o, 0, 0)),
                  pl.BlockSpec((1, 1, c), lambda bi, o: (o, 0, 0)),
                  pl.BlockSpec(f1.shape, lambda bi, o: (0, 0, 0), pipeline_mode=one),
                  pl.BlockSpec(f2.shape, lambda bi, o: (0, 0), pipeline_mode=one),
                  pl.BlockSpec(f2t.shape, lambda bi, o: (0, 0), pipeline_mode=one),
                  pl.BlockSpec(g1.shape, lambda bi, o: (0, 0, 0), pipeline_mode=one)],
        out_specs=pl.BlockSpec((1, L, c), lambda bi, o: (bi, 0, 0)),
        scratch_shapes=[pltpu.VMEM((2, FFT_N1H * FFT_PITCH, LANE), F32),
                        pltpu.VMEM((2, FFT_N1H * FFT_PITCH, LANE), F32),
                        pltpu.VMEM((2, FFT_R * FFT_PITCH, LANE), F32)],
        compiler_params=pltpu.CompilerParams(dimension_semantics=("parallel", "arbitrary"),
                                             vmem_limit_bytes=VMEM_LIMIT_HY),
        name="hyena_long_conv",
    )(u, u, h_spec, cw3[0], cb3[0], cw3, cb3, skip.reshape(2, 1, c), f1, f2, f2t, g1)


HY_FILT = 2 * HYENA_ORDER * HYENA_WIDTH
HY_HALF = HYENA_ORDER * HYENA_WIDTH


def _filter_kernel(wt_ref, wc_ref, ws_ref, b1_ref, w2_ref, b2_ref, w3f_ref, w3b_ref, freq_ref, decay_ref,
                   sig_ref, asum_ref, *, L, rows):
    i = pl.program_id(0)
    fh = w2_ref.shape[0] // 2
    scale = 1.0 / float(max(L - 1, 1))

    @pl.when(i == 0)
    def _():
        asum_ref[...] = jnp.zeros_like(asum_ref)

    n_lane = (i * rows + lax.broadcasted_iota(jnp.int32, (1, rows), 1)).astype(F32)
    sub = lax.broadcasted_iota(jnp.int32, (2 * HYENA_BANDS, 1), 0)
    bands = 1e-4 + (sub % HYENA_BANDS).astype(F32) * ((HYENA_BANDS - 1 - 1e-4) / (HYENA_BANDS - 1))
    ang_t = (2 * math.pi / L) * jnp.where(sub < HYENA_BANDS, n_lane, L - n_lane) * bands
    cos_f, sin_f = jnp.cos(ang_t).T, jnp.sin(ang_t).T

    n = (i * rows + lax.broadcasted_iota(jnp.int32, (rows, 1), 0)).astype(F32)
    tu_f, tu_b = n * scale, (L - n) * scale
    lane = lax.broadcasted_iota(jnp.int32, (rows, 2 * fh), 1)
    freq = freq_ref[...]
    pre = (jnp.where(lane < fh, tu_f, tu_b) * wt_ref[...]
           + jnp.dot(cos_f, wc_ref[...], preferred_element_type=F32, precision=HI)
           - jnp.dot(sin_f, ws_ref[...], preferred_element_type=F32, precision=HI) + b1_ref[...])
    hdn = jnp.sin(freq * pre)
    hdn = jnp.sin(freq * (jnp.dot(hdn, w2_ref[...], preferred_element_type=F32, precision=HI) + b2_ref[...]))
    for side, (w3_ref, tu) in enumerate(((w3f_ref, tu_f), (w3b_ref, tu_b))):
        cols = slice(side * HY_HALF, (side + 1) * HY_HALF)
        val = jnp.dot(hdn, w3_ref[...], preferred_element_type=F32, precision=HI)
        val = val * jnp.exp(-tu * jnp.abs(decay_ref[:, cols]))
        if side == 1:
            val = jnp.where(n > 0, val, 0.0)
        for o in range(HYENA_ORDER):
            sig_ref[2 * o + side] = val[:, o * HYENA_WIDTH:(o + 1) * HYENA_WIDTH]
        asum_ref[side:side + 1, :] += jnp.sum(jnp.abs(val), axis=0, keepdims=True)


def _block_diag2(w):
    z = jnp.zeros_like(w)
    return jnp.concatenate([jnp.concatenate([w, z], axis=1), jnp.concatenate([z, w], axis=1)], axis=0)


def hyena_filter_signals(L, p):
    rows = min(L, 512)
    fh = p['hy_pos_w1'].shape[1]
    w1, w3 = p['hy_pos_w1'], p['hy_pos_w3']
    twice = lambda v: jnp.tile(v.reshape(1, fh), (1, 2))
    zeros = jnp.zeros((fh, HY_HALF), F32)
    consts = [twice(w1[0]), _block_diag2(w1[1:1 + HYENA_BANDS]), _block_diag2(w1[1 + HYENA_BANDS:]),
              twice(p['hy_pos_b1']), _block_diag2(p['hy_pos_w2']), twice(p['hy_pos_b2']),
              jnp.concatenate([w3[:, :HY_HALF], zeros], axis=0), jnp.concatenate([zeros, w3[:, HY_HALF:]], axis=0),
              twice(p['hy_freq']), p['hy_decay'].reshape(1, HY_FILT)]
    sig, asum = pl.pallas_call(
        functools.partial(_filter_kernel, L=L, rows=rows),
        out_shape=(jax.ShapeDtypeStruct((2 * HYENA_ORDER, L, HYENA_WIDTH), F32),
                   jax.ShapeDtypeStruct((SUBLANE, HY_HALF), F32)),
        grid=(L // rows,),
        in_specs=[pl.BlockSpec(c.shape, lambda i: (0, 0)) for c in consts],
        out_specs=(pl.BlockSpec((2 * HYENA_ORDER, rows, HYENA_WIDTH), lambda i: (0, i, 0)),
                   pl.BlockSpec((SUBLANE, HY_HALF), lambda i: (0, 0))),
        compiler_params=_params("arbitrary"),
        name="hyena_filters",
    )(*consts)
    inv_norm = 1.0 / (asum[0] + asum[1]).reshape(HYENA_ORDER, 1, HYENA_WIDTH)
    return sig, inv_norm


def _rdft_tables(L):
    n_bins = L + 1
    half = -(-n_bins // 16) * 16
    k = np.arange(n_bins)[:, None]
    n = np.arange(L)[None, :]
    th = 2 * np.pi * ((k * n) % (2 * L)) / (2 * L)
    f = np.zeros((2 * half, L))
    f[:n_bins] = np.cos(th)
    f[half:half + n_bins] = -np.sin(th)
    sign = np.where(np.arange(n_bins) % 2 == 0, 1.0, -1.0)[:, None]
    fs = np.zeros_like(f)
    fs[:n_bins] = f[:n_bins] * sign
    fs[half:half + n_bins] = f[half:half + n_bins] * sign
    wgt = np.where((np.arange(n_bins) == 0) | (np.arange(n_bins) == L), 1.0, 2.0)[None, :] / (2 * L)
    g = np.zeros((L, 2 * half))
    g[:, :n_bins] = wgt * np.cos(th).T
    g[:, half:half + n_bins] = -wgt * np.sin(th).T
    return _host_bf16(f), _host_bf16(fs), _host_bf16(g), half


def _hyena_ctx_kernel(u_ref, sig_ref, inorm_ref, cw_ref, cb_ref, skip_ref, f_ref, fs_ref, g_ref, o_ref, *, L, half):
    zero_row = jnp.zeros((1, u_ref.shape[2]), F32)
    uc = _conv3(u_ref[0], zero_row, zero_row, cw_ref[...], cb_ref[...], L)
    z = uc[:, 0:HYENA_WIDTH]
    for o in range(HYENA_ORDER):
        h = (jnp.dot(f_ref[...], sig_ref[2 * o].astype(BF16), preferred_element_type=F32)
             + jnp.dot(fs_ref[...], sig_ref[2 * o + 1].astype(BF16), preferred_element_type=F32)) * inorm_ref[o]
        x = jnp.dot(f_ref[...], z.astype(BF16), preferred_element_type=F32)
        xr, xi, hr, hi = x[:half], x[half:], h[:half], h[half:]
        y = jnp.concatenate([xr * hr - xi * hi, xr * hi + xi * hr], axis=0).astype(BF16)
        conv = jnp.dot(g_ref[...], y, preferred_element_type=F32)
        z = uc[:, (o + 1) * HYENA_WIDTH:(o + 2) * HYENA_WIDTH] * (conv + skip_ref[o] * z)
    o_ref[0] = z


def hyena_short(u, sig, inv_norm, cw, cb, skip):
    b, L, cols = u.shape
    f, fs, g, half = _rdft_tables(L)
    const2 = lambda bi: (0, 0)
    const3 = lambda bi: (0, 0, 0)
    return pl.pallas_call(
        functools.partial(_hyena_ctx_kernel, L=L, half=half),
        out_shape=jax.ShapeDtypeStruct((b, L, HYENA_WIDTH), F32),
        grid=(b,),
        in_specs=[pl.BlockSpec((1, L, cols), lambda bi: (bi, 0, 0)),
                  pl.BlockSpec(sig.shape, const3), pl.BlockSpec(inv_norm.shape, const3),
                  pl.BlockSpec(cw.shape, const2), pl.BlockSpec((1, cols), const2),
                  pl.BlockSpec((HYENA_ORDER, 1, HYENA_WIDTH), const3),
                  pl.BlockSpec(f.shape, const2), pl.BlockSpec(fs.shape, const2), pl.BlockSpec(g.shape, const2)],
        out_specs=pl.BlockSpec((1, L, HYENA_WIDTH), lambda bi: (bi, 0, 0)),
        compiler_params=_params("parallel"),
        name="hyena_context",
    )(u, sig, inv_norm, cw, cb.reshape(1, cols), skip.reshape(HYENA_ORDER, 1, HYENA_WIDTH), f, fs, g)


def _split_bf16(a):
    hi = a.astype(BF16)
    return hi, (a - hi.astype(F32)).astype(BF16)


def _norm_router_kernel(*refs, n_lat, two_src):
    if two_src:
        xl_ref, xc_ref, nw_ref, sh_ref, sc_ref, whi_ref, wlo_ref, br_ref, tri_ref, h_ref, lg_ref, cnt_ref = refs
        x = jnp.where(pl.program_id(0) < n_lat, xl_ref[...], xc_ref[...])
    else:
        xl_ref, nw_ref, sh_ref, sc_ref, whi_ref, wlo_ref, br_ref, tri_ref, h_ref, lg_ref, cnt_ref = refs
        x = xl_ref[...]
    h = _modnorm(x, nw_ref[...], sh_ref[0], sc_ref[0])
    h_ref[...] = h
    h_hi, h_lo = _split_bf16(h)
    lg = (jnp.dot(h_hi, whi_ref[...], preferred_element_type=F32)
          + jnp.dot(h_hi, wlo_ref[...], preferred_element_type=F32)
          + jnp.dot(h_lo, whi_ref[...], preferred_element_type=F32)) + br_ref[...]
    lane = lax.broadcasted_iota(jnp.int32, lg.shape, 1)
    first = lambda hit: jnp.min(jnp.where(hit, lane, ROUTER_COLS), axis=-1, keepdims=True)
    gl = jnp.where(lane < N_GROUPS, lg, -jnp.inf)
    gmax = jnp.max(gl, axis=-1, keepdims=True)
    grp = first(gl == gmax)
    grp_p = 1.0 / jnp.sum(jnp.exp(gl - gmax), axis=-1, keepdims=True)
    lo = N_GROUPS + grp * EXPERTS_PER_GROUP
    el = jnp.where((lane >= lo) & (lane < lo + EXPERTS_PER_GROUP), lg, -jnp.inf)
    e1 = jnp.max(el, axis=-1, keepdims=True)
    i1 = first(el == e1)
    el2 = jnp.where(lane == i1, -jnp.inf, el)
    e2 = jnp.max(el2, axis=-1, keepdims=True)
    i2 = first(el2 == e2)
    r = jnp.exp(e2 - e1)
    w1 = grp_p / (1.0 + r)
    w2 = w1 * r
    @pl.when(pl.program_id(0) == 0)
    def _():
        cnt_ref[...] = jnp.zeros_like(cnt_ref)

    hit1, hit2 = lane == i1, lane == i2
    picks = jnp.where(hit1 | hit2, 1.0, 0.0)
    before = cnt_ref[0:1, :] + jnp.dot(tri_ref[...], picks.astype(BF16), preferred_element_type=F32)
    rank1 = jnp.sum(jnp.where(hit1, before, 0.0), axis=-1, keepdims=True)
    rank2 = jnp.sum(jnp.where(hit2, before, 0.0), axis=-1, keepdims=True)
    cnt_ref[...] = jnp.broadcast_to(cnt_ref[0:1, :] + jnp.sum(picks, axis=0, keepdims=True), cnt_ref.shape)
    vals = [(i1 - N_GROUPS).astype(F32), (i2 - N_GROUPS).astype(F32), w1, w2, rank1, rank2]
    out = jnp.zeros(lg.shape, F32)
    for k, val in enumerate(vals):
        out = jnp.where(lane == k, val, out)
    lg_ref[...] = out


def norm_router(xl, xc, nw, mod_l, mod_c, w_router, b_router, tm=512):
    b, L, d = xl.shape
    two_src = xc is not None
    n_lat = b * L // tm
    per_batch = L // tm
    n_ctx = (xc.shape[0] * xc.shape[1]) // tm if two_src else 0
    w_hi, w_lo = _split_bf16(w_router)
    const = lambda i: (0, 0)
    if two_src:
        shift = jnp.concatenate([mod_l[0], mod_c[0]], axis=0)
        scale = jnp.concatenate([mod_l[1], mod_c[1]], axis=0)
        mod_map = lambda i: (jnp.where(i < n_lat, i // per_batch, b), 0, 0)
        srcs = [xl.reshape(b * L, d), xc.reshape(-1, d)]
        src_specs = [pl.BlockSpec((tm, d), lambda i: (jnp.minimum(i, n_lat - 1), 0)),
                     pl.BlockSpec((tm, d), lambda i: (jnp.maximum(i - n_lat, 0), 0))]
    else:
        shift, scale = mod_l
        mod_map = lambda i: (i // per_batch, 0, 0)
        srcs = [xl.reshape(b * L, d)]
        src_specs = [pl.BlockSpec((tm, d), lambda i: (i, 0))]
    n_tok = (n_lat + n_ctx) * tm
    tri = _host_bf16(np.tril(np.ones((tm, tm)), -1))
    return pl.pallas_call(
        functools.partial(_norm_router_kernel, n_lat=n_lat, two_src=two_src),
        out_shape=(jax.ShapeDtypeStruct((n_tok, d), F32), jax.ShapeDtypeStruct((n_tok, ROUTER_COLS), F32),
                   jax.ShapeDtypeStruct((SUBLANE, ROUTER_COLS), F32)),
        grid=(n_lat + n_ctx,),
        in_specs=src_specs + [pl.BlockSpec((1, d), const), pl.BlockSpec((1, 1, d), mod_map),
                              pl.BlockSpec((1, 1, d), mod_map), pl.BlockSpec((d, ROUTER_COLS), const),
                              pl.BlockSpec((d, ROUTER_COLS), const), pl.BlockSpec((1, ROUTER_COLS), const),
                              pl.BlockSpec((tm, tm), const)],
        out_specs=(pl.BlockSpec((tm, d), lambda i: (i, 0)), pl.BlockSpec((tm, ROUTER_COLS), lambda i: (i, 0)),
                   pl.BlockSpec((SUBLANE, ROUTER_COLS), const)),
        compiler_params=_params("arbitrary"),
        name="moe_norm_router",
    )(*srcs, nw.reshape(1, d), shift, scale, w_hi, w_lo, b_router, tri)


def _expert_ffn_kernel(te_ref, tv_ref, x_ref, wg_ref, wu_ref, wd_ref, o_ref):
    i = pl.program_id(0)

    @pl.when(tv_ref[i] > 0)
    def _():
        x = x_ref[...].astype(BF16)
        g = jnp.dot(x, wg_ref[0].astype(BF16), preferred_element_type=F32)
        u = jnp.dot(x, wu_ref[0].astype(BF16), preferred_element_type=F32)
        hid = _silu(g) * u
        o_ref[...] = jnp.dot(hid.astype(BF16), wd_ref[0].astype(BF16), preferred_element_type=F32)

    @pl.when(tv_ref[i] == 0)
    def _():
        o_ref[...] = jnp.zeros_like(o_ref)


def expert_ffn(x_sorted, tile_expert, tile_valid, w_gate, w_up, w_down, tm):
    r, d = x_sorted.shape
    f = w_gate.shape[-1]
    grid_spec = pltpu.PrefetchScalarGridSpec(
        num_scalar_prefetch=2,
        grid=(r // tm,),
        in_specs=[pl.BlockSpec((tm, d), lambda i, te, tv: (i, 0)),
                  pl.BlockSpec((1, d, f), lambda i, te, tv: (te[i], 0, 0)),
                  pl.BlockSpec((1, d, f), lambda i, te, tv: (te[i], 0, 0)),
                  pl.BlockSpec((1, f, d), lambda i, te, tv: (te[i], 0, 0))],
        out_specs=pl.BlockSpec((tm, d), lambda i, te, tv: (i, 0)),
    )
    return pl.pallas_call(
        _expert_ffn_kernel,
        out_shape=jax.ShapeDtypeStruct((r, d), F32),
        grid_spec=grid_spec,
        compiler_params=_params("arbitrary"),
        name="moe_expert_ffn",
    )(tile_expert, tile_valid, x_sorted, w_gate, w_up, w_down)


def moe_apply(h_tokens, routed, counts_row, w_gate, w_up, w_down, layer, n_lat, tm=256):
    t, d = h_tokens.shape
    e_idx, rank = routed[:, 0:2].astype(jnp.int32), routed[:, 4:6].astype(jnp.int32)
    counts = counts_row[0, N_GROUPS:N_GROUPS + N_EXPERTS].astype(jnp.int32)
    n_pairs = 2 * t
    padded = (counts + tm - 1) // tm * tm
    pad_end = jnp.cumsum(padded)
    pad_start = pad_end - padded
    n_rows = n_pairs + N_EXPERTS * tm
    tile_start = jnp.arange(n_rows // tm, dtype=jnp.int32) * tm
    tile_expert = jnp.minimum(jnp.sum((pad_end[None, :] <= tile_start[:, None]).astype(jnp.int32), axis=1),
                              N_EXPERTS - 1)
    tile_valid = (tile_start < pad_end[-1]).astype(jnp.int32)
    onehot = (e_idx[:, :, None] == jnp.arange(N_EXPERTS, dtype=jnp.int32)).astype(jnp.int32)
    pos = jnp.sum(onehot * pad_start, axis=-1) + rank
    j = jnp.arange(tm, dtype=jnp.int32)[None, :]
    fill_key = jnp.where(j < (padded - counts)[:, None], (pad_start + counts)[:, None] + j, n_rows)
    keys = jnp.concatenate([pos.reshape(-1), fill_key.reshape(-1)])
    toks = jnp.concatenate([jnp.arange(n_pairs, dtype=jnp.int32) // 2, jnp.arange(N_EXPERTS * tm, dtype=jnp.int32) % t])
    _, row_token = lax.sort((keys, toks), num_keys=1)
    x_sorted = h_tokens[row_token]
    wg = w_gate.reshape(-1, d, EXPERT_HIDDEN)
    wu = w_up.reshape(-1, d, EXPERT_HIDDEN)
    wd = w_down.reshape(-1, EXPERT_HIDDEN, d)
    y_sorted = expert_ffn(x_sorted, tile_expert + layer * N_EXPERTS, tile_valid, wg, wu, wd, tm)
    lat = (y_sorted[pos[:n_lat, 0]], y_sorted[pos[:n_lat, 1]])
    rest = (y_sorted[pos[n_lat:, 0]], y_sorted[pos[n_lat:, 1]]) if t > n_lat else None
    return lat, rest


def _final_kernel(x_ref, ya_ref, yb_ref, rt_ref, g_ref, w_ref, o_ref):
    x = x_ref[0] + g_ref[0] * _moe_mix(ya_ref[0], yb_ref[0], rt_ref[0])
    o_ref[0] = x * lax.rsqrt(jnp.mean(x * x, axis=-1, keepdims=True) + EPS) * w_ref[...]


def final_norm(x, ya, yb, routed, gate, w, tm=512):
    b, L, d = x.shape
    tok = pl.BlockSpec((1, tm, d), lambda bi, i: (bi, i, 0))
    return pl.pallas_call(
        _final_kernel,
        out_shape=jax.ShapeDtypeStruct((b, L, d), F32),
        grid=(b, L // tm),
        in_specs=[tok, tok, tok, pl.BlockSpec((1, tm, ROUTER_COLS), lambda bi, i: (bi, i, 0)),
                  pl.BlockSpec((1, 1, d), _mod_map(gate, b)), pl.BlockSpec((1, d), lambda bi, i: (0, 0))],
        out_specs=tok,
        compiler_params=_params("parallel", "arbitrary"),
        name="final_rmsnorm",
    )(x, ya, yb, routed, gate, w.reshape(1, d))


IN_SIZES = (SSD_WIDTH, SSD_CONV_CH, 2 * SSD_HEADS, HY_COLS, 2 * ML_WIDTH, ML_WIDTH, ML_WIDTH, 4 * ML_HEADS)


def _regroup_kernel(w_ref, o_ref):
    src = dst = 0
    for n in IN_SIZES:
        pad = -n % LANE
        o_ref[0, :, dst:dst + n] = w_ref[0, :, src:src + n].astype(BF16)
        if pad:
            o_ref[0, :, dst + n:dst + n + pad] = jnp.zeros((o_ref.shape[1], pad), BF16)
        src, dst = src + n, dst + n + pad


def regroup_in_weights(w_in):
    depth, d, n_in = w_in.shape
    n_out = sum(n + (-n % LANE) for n in IN_SIZES)
    one = pl.Buffered(1)
    return pl.pallas_call(
        _regroup_kernel,
        out_shape=jax.ShapeDtypeStruct((depth, d, n_out), BF16),
        grid=(depth,),
        in_specs=[pl.BlockSpec((1, d, n_in), lambda i: (i, 0, 0), pipeline_mode=one)],
        out_specs=pl.BlockSpec((1, d, n_out), lambda i: (i, 0, 0)),
        compiler_params=_params("arbitrary"),
        name="regroup_in_weights",
    )(w_in)


def kernel(x, c, ctx, c_ctx, w_mod, b_mod, norm1_w, norm2_w, w_in, w_out, ssd_conv_w, ssd_conv_b, ssd_dt_bias, ssd_a_log, ssd_d, ssd_norm_w, hy_conv_w, hy_conv_b, hy_pos_w1, hy_pos_b1, hy_pos_w2, hy_pos_b2, hy_pos_w3, hy_freq, hy_decay, hy_skip, ml_conv_w, ml_conv_b, ml_gate_b, ml_norm_w, grp_router_w, grp_router_b, exp_router_w, exp_router_b, moe_w_gate, moe_w_up, moe_w_down, final_norm_w):
    layer_params = dict(
        ssd_conv_w=ssd_conv_w, ssd_conv_b=ssd_conv_b, ssd_dt_bias=ssd_dt_bias, ssd_a_log=ssd_a_log,
        ssd_d=ssd_d, ssd_norm_w=ssd_norm_w, hy_conv_w=hy_conv_w, hy_conv_b=hy_conv_b,
        hy_pos_w1=hy_pos_w1, hy_pos_b1=hy_pos_b1, hy_pos_w2=hy_pos_w2, hy_pos_b2=hy_pos_b2,
        hy_pos_w3=hy_pos_w3, hy_freq=hy_freq, hy_decay=hy_decay, hy_skip=hy_skip,
        ml_conv_w=ml_conv_w, ml_conv_b=ml_conv_b, ml_gate_b=ml_gate_b, ml_norm_w=ml_norm_w)
    bsz, seq, d = x.shape
    n_ctx = ctx.shape[1]
    xl, xc = x, ctx
    moe_l = moe_c = None
    ssd0 = jnp.zeros((bsz, SSD_HEADS, HEAD_DIM, SSD_STATE), F32)
    ml0 = (jnp.zeros((bsz, ML_HEADS, HEAD_DIM, LANE), F32), jnp.zeros((bsz, SUBLANE, LANE), F32))
    c_rows = jnp.concatenate([c, c_ctx[None, :], jnp.zeros((SUBLANE - bsz - 1, d), F32)], axis=0)
    w_in_all = regroup_in_weights(w_in)
    for i in range(DEPTH):
        last = i == DEPTH - 1
        p = {name: arr[i] for name, arr in layer_params.items()}
        sp, mp = ssd_prepare(p), ml_prepare(p)
        mod = modulation(c_rows, w_mod, b_mod, i).reshape(SUBLANE, N_MOD, 1, d)
        mod_l = [mod[:bsz, k] for k in range(N_MOD)]
        mod_c = [mod[bsz:bsz + 1, k] for k in range(N_MOD)]
        w_out_b = w_out[i].astype(BF16)
        w_router = jnp.pad(jnp.concatenate([grp_router_w[i], exp_router_w[i]], axis=1),
                           ((0, 0), (0, ROUTER_COLS - N_GROUPS - N_EXPERTS)))
        b_router = jnp.pad(jnp.concatenate([grp_router_b[i], exp_router_b[i]]),
                           (0, ROUTER_COLS - N_GROUPS - N_EXPERTS)).reshape(1, ROUTER_COLS)

        uc_ssd, uc_hy, uc_ml, xc = norm_proj(xc, moe_c, norm1_w[i], mod_c[0], mod_c[1], w_in_all, i)
        yc_ssd, ssd_f, ssd_b = ssd_mixer(uc_ssd, sp, ssd0, ssd0, not last)
        yc_ml, ml_f, ml_b = ml_mixer(uc_ml, mp, ml0, ml0, not last)
        col_major = i % 2 == 1
        ul_ssd, ul_hy, ul_ml, xl = norm_proj(xl, moe_l, norm1_w[i], mod_l[0], mod_l[1], w_in_all, i, col_major)
        yl_ssd, _, _ = ssd_mixer(ul_ssd, sp, ssd_f, ssd_b, True)
        yl_ml, _, _ = ml_mixer(ul_ml, mp, ml_f, ml_b, True)
        h_spec = hyena_filter_spectra(*hyena_filter_signals(seq, p))
        yl_hy = hyena_long(ul_hy, h_spec, p['hy_conv_w'], p['hy_conv_b'], p['hy_skip'])
        xl = out_proj(yl_ssd, yl_hy, yl_ml, xl, mod_l[2], w_out_b, col_major)
        if not last:
            sig_c, inorm_c = hyena_filter_signals(n_ctx, p)
            yc_hy = hyena_short(uc_hy, sig_c, inorm_c, p['hy_conv_w'], p['hy_conv_b'], p['hy_skip'])
            xc = out_proj(yc_ssd, yc_hy, yc_ml, xc, mod_c[2], w_out_b)
        h_all, routed, counts_row = norm_router(xl, None if last else xc, norm2_w[i], (mod_l[3], mod_l[4]), (mod_c[3], mod_c[4]),
                                    w_router, b_router)
        lat, rest = moe_apply(h_all, routed, counts_row, moe_w_gate, moe_w_up, moe_w_down, i, bsz * seq)
        n_lat = bsz * seq
        moe_l = (lat[0].reshape(bsz, seq, d), lat[1].reshape(bsz, seq, d),
                 routed[:n_lat].reshape(bsz, seq, ROUTER_COLS), mod_l[5])
        if not last:
            moe_c = (rest[0].reshape(bsz, n_ctx, d), rest[1].reshape(bsz, n_ctx, d),
                     routed[n_lat:].reshape(bsz, n_ctx, ROUTER_COLS), mod_c[5])
    return final_norm(xl, *moe_l, final_norm_w)
```

```python
import functools
import math

import jax
import jax.numpy as jnp
import numpy as np
from jax import lax
from jax.experimental import pallas as pl
from jax.experimental.pallas import tpu as pltpu

D_MODEL = 1024
DEPTH = 2
GRID_W = 64
HEAD_DIM = 64
SSD_WIDTH = 384
SSD_HEADS = SSD_WIDTH // HEAD_DIM
SSD_GROUPS = 2
SSD_STATE = 64
HYENA_WIDTH = 256
HYENA_ORDER = 2
HYENA_BANDS = 16
ML_WIDTH = 384
ML_HEADS = ML_WIDTH // HEAD_DIM
N_GROUPS = 4
EXPERTS_PER_GROUP = 8
N_EXPERTS = N_GROUPS * EXPERTS_PER_GROUP
EXPERT_HIDDEN = 256
N_MOD = 6
EPS = 1e-6

LANE = 128
SUBLANE = 8
VMEM_LIMIT = 48 * 1024 * 1024
VMEM_LIMIT_HY = 56 * 1024 * 1024

SSD_CONV_CH = SSD_WIDTH + 2 * SSD_GROUPS * SSD_STATE
SSD_XBC0 = SSD_WIDTH
SSD_DT0 = SSD_XBC0 + SSD_CONV_CH
SSD_COLS = SSD_DT0 + LANE
HY_COLS = (HYENA_ORDER + 1) * HYENA_WIDTH
ML_V0 = 2 * ML_WIDTH
ML_O0 = ML_V0 + ML_WIDTH
ML_G0 = ML_O0 + ML_WIDTH
ML_COLS = ML_G0 + LANE
ROUTER_COLS = LANE

F32 = jnp.float32
BF16 = jnp.bfloat16
HI = lax.Precision.HIGHEST


def _params(*sem):
    return pltpu.CompilerParams(dimension_semantics=sem, vmem_limit_bytes=VMEM_LIMIT)


def _host_bf16(a):
    return jnp.asarray(np.asarray(a, np.float32).astype(BF16))


def _silu(x):
    return x * jax.nn.sigmoid(x)


def _softplus(x):
    return jnp.maximum(x, 0.0) + jnp.log(1.0 + jnp.exp(-jnp.abs(x)))


def _log_sigmoid(x):
    return jnp.minimum(x, 0.0) - jnp.log(1.0 + jnp.exp(-jnp.abs(x)))


def _mod_kernel(c_ref, w_ref, b_ref, o_ref):
    o_ref[...] = jnp.dot(_silu(c_ref[...]), w_ref[...], preferred_element_type=F32, precision=HI) + b_ref[...]


def modulation(c_rows, w_mod, b_mod, layer):
    depth, d, n = w_mod.shape
    tn = 1536
    return pl.pallas_call(
        _mod_kernel,
        out_shape=jax.ShapeDtypeStruct((c_rows.shape[0], n), F32),
        grid=(n // tn,),
        in_specs=[pl.BlockSpec(c_rows.shape, lambda j: (0, 0)),
                  pl.BlockSpec((None, d, tn), lambda j: (layer, 0, j)),
                  pl.BlockSpec((None, 1, tn), lambda j: (layer, 0, j))],
        out_specs=pl.BlockSpec((c_rows.shape[0], tn), lambda j: (0, j)),
        compiler_params=_params("arbitrary"),
        name="adaln_modulation",
    )(c_rows, w_mod, b_mod.reshape(depth, 1, n))


def _modnorm(x, nw, shift, scale):
    y = x * lax.rsqrt(jnp.mean(x * x, axis=-1, keepdims=True) + EPS) * nw
    return y * (1.0 + scale) + shift


def _mod_map(mod, b):
    return (lambda bi, i: (bi, 0, 0)) if mod.shape[0] == b else (lambda bi, i: (0, 0, 0))


def _tok_view(x, col_major):
    b, L, d = x.shape
    return x.reshape(b, L // GRID_W, GRID_W, d) if col_major else x


def _tok_spec(L, d, tm, col_major):
    if col_major:
        assert tm == (L // GRID_W) * SUBLANE
        return pl.BlockSpec((1, L // GRID_W, SUBLANE, d), lambda bi, i: (bi, 0, i, 0))
    return pl.BlockSpec((1, tm, d), lambda bi, i: (bi, i, 0))


def _tok_load(ref, col_major):
    if not col_major:
        return ref[0]
    return jnp.concatenate([ref[0, :, j, :] for j in range(ref.shape[2])], axis=0)


def _tok_store(ref, val, col_major):
    if not col_major:
        ref[0] = val
        return
    rows = ref.shape[1]
    for j in range(ref.shape[2]):
        ref[0, :, j, :] = val[j * rows:(j + 1) * rows]


def _moe_mix(ya, yb, routed):
    return routed[:, 2:3] * ya + routed[:, 3:4] * yb


def _norm_proj_kernel(*refs, col_major, fuse_moe):
    if fuse_moe:
        x_ref, ya_ref, yb_ref, rt_ref, g_ref, nw_ref, sh_ref, sc_ref, w_ref, ssd_ref, hy_ref, ml_ref, xo_ref = refs
    else:
        x_ref, nw_ref, sh_ref, sc_ref, w_ref, ssd_ref, hy_ref, ml_ref = refs
    x = _tok_load(x_ref, col_major)
    if fuse_moe:
        x = x + g_ref[0] * _moe_mix(_tok_load(ya_ref, col_major), _tok_load(yb_ref, col_major),
                                    _tok_load(rt_ref, col_major))
        _tok_store(xo_ref, x, col_major)
    h = _modnorm(x, nw_ref[...], sh_ref[0], sc_ref[0])
    u = jnp.dot(h.astype(BF16), w_ref[...], preferred_element_type=F32)
    ssd_ref[0] = u[:, 0:SSD_COLS]
    hy_ref[0] = u[:, SSD_COLS:SSD_COLS + HY_COLS]
    ml_ref[0] = u[:, SSD_COLS + HY_COLS:]


def norm_proj(x, moe, nw, shift, scale, w_all, layer, col_major=False, tm=512):
    b, L, d = x.shape
    n = w_all.shape[2]
    fuse_moe = moe is not None
    tm = (L // GRID_W) * SUBLANE if col_major else min(tm, L)
    tok = _tok_spec(L, d, tm, col_major)
    row = lambda bi, i: (bi, i, 0)
    const2 = lambda bi, i: (0, 0)
    args, in_specs = [_tok_view(x, col_major)], [tok]
    if fuse_moe:
        ya, yb, routed, gate = moe
        args += [_tok_view(ya, col_major), _tok_view(yb, col_major), _tok_view(routed, col_major), gate]
        in_specs += [tok, tok, _tok_spec(L, ROUTER_COLS, tm, col_major), pl.BlockSpec((1, 1, d), _mod_map(gate, b))]
    args += [nw.reshape(1, d), shift, scale, w_all]
    in_specs += [pl.BlockSpec((1, d), const2), pl.BlockSpec((1, 1, d), _mod_map(shift, b)),
                 pl.BlockSpec((1, 1, d), _mod_map(scale, b)),
                 pl.BlockSpec((None, d, n), lambda bi, i: (layer, 0, 0))]
    out_shape = [jax.ShapeDtypeStruct((b, L, SSD_COLS), F32), jax.ShapeDtypeStruct((b, L, HY_COLS), F32),
                 jax.ShapeDtypeStruct((b, L, ML_COLS), F32)]
    out_specs = [pl.BlockSpec((1, tm, SSD_COLS), row), pl.BlockSpec((1, tm, HY_COLS), row),
                 pl.BlockSpec((1, tm, ML_COLS), row)]
    if fuse_moe:
        out_shape.append(jax.ShapeDtypeStruct(args[0].shape, F32))
        out_specs.append(tok)
    outs = pl.pallas_call(
        functools.partial(_norm_proj_kernel, col_major=col_major, fuse_moe=fuse_moe),
        out_shape=tuple(out_shape),
        grid=(b, L // tm),
        in_specs=in_specs,
        out_specs=tuple(out_specs),
        compiler_params=_params("parallel", "arbitrary"),
        name="norm_in_proj",
    )(*args)
    return (*outs[:3], outs[3].reshape(b, L, d) if fuse_moe else x)


def _out_proj_kernel(ys_ref, yh_ref, ym_ref, x_ref, g_ref, w_ref, o_ref, *, col_major):
    y = jnp.concatenate([ys_ref[0], yh_ref[0], ym_ref[0]], axis=-1).astype(BF16)
    r = _tok_load(x_ref, col_major) + g_ref[0] * jnp.dot(y, w_ref[...], preferred_element_type=F32)
    _tok_store(o_ref, r, col_major)


def out_proj(y_ssd, y_hy, y_ml, x, gate, w_bf16, col_major=False, tm=512):
    b, L, d = x.shape
    tm = (L // GRID_W) * SUBLANE if col_major else min(tm, L)
    row = lambda bi, i: (bi, i, 0)
    tok = _tok_spec(L, d, tm, col_major)
    xv = _tok_view(x, col_major)
    return pl.pallas_call(
        functools.partial(_out_proj_kernel, col_major=col_major),
        out_shape=jax.ShapeDtypeStruct(xv.shape, F32),
        grid=(b, L // tm),
        in_specs=[pl.BlockSpec((1, tm, SSD_WIDTH), row), pl.BlockSpec((1, tm, HYENA_WIDTH), row),
                  pl.BlockSpec((1, tm, ML_WIDTH), row), tok,
                  pl.BlockSpec((1, 1, d), _mod_map(gate, b)),
                  pl.BlockSpec(w_bf16.shape, lambda bi, i: (0, 0))],
        out_specs=tok,
        compiler_params=_params("parallel", "arbitrary"),
        name="out_proj_residual",
    )(y_ssd, y_hy, y_ml, xv, gate, w_bf16).reshape(b, L, d)


def _conv3(xr, prev_row, next_row, cw, cb, q):
    rid = lax.broadcasted_iota(jnp.int32, (q, 1), 0)
    x_prev = jnp.where(rid == 0, prev_row, pltpu.roll(xr, 1, axis=0))
    x_next = jnp.where(rid == q - 1, next_row, pltpu.roll(xr, q - 1, axis=0))
    return x_prev * cw[0:1] + xr * cw[1:2] + x_next * cw[2:3] + cb


def _masked_scan(mask, x):
    m = mask.astype(BF16)
    hi = x.astype(BF16)
    r1 = x - hi.astype(F32)
    mid = r1.astype(BF16)
    lo = (r1 - mid.astype(F32)).astype(BF16)
    return (jnp.dot(m, hi, preferred_element_type=F32) + jnp.dot(m, mid, preferred_element_type=F32)
            + jnp.dot(m, lo, preferred_element_type=F32))


def _scan_mask(q, direction):
    li = lax.broadcasted_iota(jnp.int32, (q, q), 0)
    si = lax.broadcasted_iota(jnp.int32, (q, q), 1)
    return (si <= li) if direction == 0 else (si >= li)


def _running_max(x, direction, q):
    rid = lax.broadcasted_iota(jnp.int32, (q, 1), 0)
    s = 1
    while s < q:
        if direction == 0:
            x = jnp.where(rid >= s, jnp.maximum(x, pltpu.roll(x, s, axis=0)), x)
        else:
            x = jnp.where(rid < q - s, jnp.maximum(x, pltpu.roll(x, q - s, axis=0)), x)
        s *= 2
    return x


def _scan_specs(L, q, nc, cols, direction):
    hb = q // SUBLANE
    nrb = L // SUBLANE
    cidx = (lambda j: j) if direction == 0 else (lambda j: nc - 1 - j)
    specs = [pl.BlockSpec((1, q, cols), lambda bi, j: (bi, cidx(j), 0)),
             pl.BlockSpec((1, SUBLANE, cols), lambda bi, j: (bi, jnp.maximum(cidx(j) * hb - 1, 0), 0)),
             pl.BlockSpec((1, SUBLANE, cols), lambda bi, j: (bi, jnp.minimum((cidx(j) + 1) * hb, nrb - 1), 0))]
    return specs, cidx


def _ssd_kernel(*refs, direction, finalize, q, nc):
    if finalize:
        (u_ref, prev_ref, next_ref, yb_ref, init_ref, cw_ref, cb_ref, dtb_ref, a_ref, d_ref, nw_ref,
         y_ref, fin_ref, state_ref) = refs
    else:
        (u_ref, prev_ref, next_ref, init_ref, cw_ref, cb_ref, dtb_ref, a_ref,
         y_ref, fin_ref, state_ref) = refs
    j = pl.program_id(1)
    c = j if direction == 0 else nc - 1 - j

    @pl.when(j == 0)
    def _():
        state_ref[...] = init_ref[0]

    prev_row = jnp.where(c > 0, prev_ref[0, SUBLANE - 1:SUBLANE, SSD_XBC0:SSD_DT0], 0.0)
    next_row = jnp.where(c < nc - 1, next_ref[0, 0:1, SSD_XBC0:SSD_DT0], 0.0)
    xc = _silu(_conv3(u_ref[0, :, SSD_XBC0:SSD_DT0], prev_row, next_row, cw_ref[...], cb_ref[...], q))

    dt = _softplus(u_ref[0, :, SSD_DT0:SSD_COLS] + dtb_ref[...])
    mask = _scan_mask(q, direction)
    cum = _masked_scan(mask, dt * a_ref[...])
    cum_t = cum.T
    end = q - 1 if direction == 0 else 0

    ys = []
    for g in range(SSD_GROUPS):
        b0 = SSD_WIDTH + g * SSD_STATE
        c0 = SSD_WIDTH + (SSD_GROUPS + g) * SSD_STATE
        bm_t = xc[:, b0:b0 + SSD_STATE].T
        cm = xc[:, c0:c0 + SSD_STATE].astype(BF16)
        scores = jnp.dot(cm, bm_t.astype(BF16), preferred_element_type=F32)
        for h in range(g * (SSD_HEADS // SSD_GROUPS), (g + 1) * (SSD_HEADS // SSD_GROUPS)):
            hl = direction * SSD_HEADS + h
            col = cum[:, hl:hl + 1]
            row = cum_t[hl:hl + 1, :]
            seg = jnp.exp(jnp.where(mask, col - row, -jnp.inf))
            xdt = (xc[:, h * HEAD_DIM:(h + 1) * HEAD_DIM] * dt[:, hl:hl + 1]).astype(BF16)
            y = jnp.dot((scores * seg).astype(BF16), xdt, preferred_element_type=F32)
            st = state_ref[h]
            y = y + jnp.dot(cm, st.astype(BF16), preferred_element_type=F32) * jnp.exp(col)
            tot = cum[end:end + 1, hl:hl + 1]
            upd = jnp.dot((bm_t * jnp.exp(tot - row)).astype(BF16), xdt, preferred_element_type=F32)
            state_ref[h] = st * jnp.exp(tot) + upd
            ys.append(y)
    y_all = jnp.concatenate(ys, axis=-1)
    if finalize:
        t = (y_all + yb_ref[0] + xc[:, 0:SSD_WIDTH] * d_ref[...]) * _silu(u_ref[0, :, 0:SSD_WIDTH])
        y_all = t * lax.rsqrt(jnp.mean(t * t, axis=-1, keepdims=True) + EPS) * nw_ref[...]
    y_ref[0] = y_all

    @pl.when(j == nc - 1)
    def _():
        fin_ref[0] = state_ref[...]


def ssd_pass(u, y_other, init, sp, direction, q):
    b, L, _ = u.shape
    q = min(q, L)
    nc = L // q
    finalize = y_other is not None
    in_specs, cidx = _scan_specs(L, q, nc, SSD_COLS, direction)
    const2 = lambda bi, j: (0, 0)
    st_spec = pl.BlockSpec((1, SSD_HEADS, HEAD_DIM, SSD_STATE), lambda bi, j: (bi, 0, 0, 0))
    y_spec = pl.BlockSpec((1, q, SSD_WIDTH), lambda bi, j: (bi, cidx(j), 0))
    args = [u, u, u]
    if finalize:
        in_specs.append(y_spec)
        args.append(y_other)
    consts = [sp['cw'], sp['cb'], sp['dtb'], sp['a']] + ([sp['d'], sp['nw']] if finalize else [])
    in_specs += [st_spec] + [pl.BlockSpec(t.shape, const2) for t in consts]
    args += [init] + consts
    return pl.pallas_call(
        functools.partial(_ssd_kernel, direction=direction, finalize=finalize, q=q, nc=nc),
        out_shape=(jax.ShapeDtypeStruct((b, L, SSD_WIDTH), F32),
                   jax.ShapeDtypeStruct((b, SSD_HEADS, HEAD_DIM, SSD_STATE), F32)),
        grid=(b, nc),
        in_specs=in_specs,
        out_specs=(y_spec, st_spec),
        scratch_shapes=[pltpu.VMEM((SSD_HEADS, HEAD_DIM, SSD_STATE), F32)],
        compiler_params=_params("parallel", "arbitrary"),
        name="ssd_scan_%s" % ("fwd" if direction == 0 else "bwd"),
    )(*args)


def ssd_prepare(p):
    pad = lambda v: jnp.pad(v.reshape(1, -1), ((0, 0), (0, LANE - v.size)))
    return dict(cw=p['ssd_conv_w'], cb=p['ssd_conv_b'].reshape(1, -1),
                dtb=pad(p['ssd_dt_bias']), a=pad(-jnp.exp(p['ssd_a_log'])),
                d=jnp.repeat(p['ssd_d'], HEAD_DIM).reshape(1, -1), nw=p['ssd_norm_w'].reshape(1, -1))


def ssd_mixer(u, sp, init_f, init_b, want_y, q=256):
    yb, fin_b = ssd_pass(u, None, init_b, sp, 1, q)
    y, fin_f = ssd_pass(u, yb if want_y else None, init_f, sp, 0, q)
    return y, fin_f, fin_b


def _ml_kernel(*refs, direction, finalize, q, nc):
    if finalize:
        (u_ref, prev_ref, next_ref, hb_ref, s_init_ref, m_init_ref, cw_ref, cb_ref, gb_ref, nw_ref, pool_ref,
         y_ref, s_fin_ref, m_fin_ref, s_ref, m_ref) = refs
    else:
        (u_ref, prev_ref, next_ref, s_init_ref, m_init_ref, cw_ref, cb_ref, gb_ref,
         y_ref, s_fin_ref, m_fin_ref, s_ref, m_ref) = refs
    j = pl.program_id(1)
    c = j if direction == 0 else nc - 1 - j

    @pl.when(j == 0)
    def _():
        s_ref[...] = s_init_ref[0]
        m_ref[...] = m_init_ref[0]

    prev_row = jnp.where(c > 0, prev_ref[0, SUBLANE - 1:SUBLANE, 0:ML_V0], 0.0)
    next_row = jnp.where(c < nc - 1, next_ref[0, 0:1, 0:ML_V0], 0.0)
    qk = _silu(_conv3(u_ref[0, :, 0:ML_V0], prev_row, next_row, cw_ref[...], cb_ref[...], q))
    v = u_ref[0, :, ML_V0:ML_O0]

    gb = u_ref[0, :, ML_G0:ML_COLS] + gb_ref[...]
    mask = _scan_mask(q, direction)
    cum = _masked_scan(mask, _log_sigmoid(gb))
    ig = pltpu.roll(gb, ML_HEADS, axis=1)
    end = q - 1 if direction == 0 else 0
    m_prev = m_ref[0:1, :]
    tot = cum[end:end + 1, :]
    w_end = tot - cum + ig
    m_loc = jnp.max(w_end, axis=0, keepdims=True)
    e_end = jnp.exp(w_end - m_loc)
    m_new = jnp.maximum(tot + m_prev, m_loc)
    a_prev = jnp.exp(tot + m_prev - m_new)
    a_loc = jnp.exp(m_loc - m_new)
    inter = cum + m_prev
    rel = ig - cum
    m_t = jnp.maximum(inter, cum + _running_max(rel, direction, q))
    col_a = cum - m_t
    a_inter = jnp.exp(inter - m_t)
    floor = jnp.exp(-m_t)
    rel_t = rel.T
    e_end_t = e_end.T
    k_t = (qk[:, ML_WIDTH:2 * ML_WIDTH] * (HEAD_DIM ** -0.5)).T
    one_col = (lax.broadcasted_iota(jnp.int32, (q, HEAD_DIM), 1) == 0).astype(F32)

    ys = []
    for h in range(ML_HEADS):
        fl = direction * 2 * ML_HEADS + ML_HEADS + h
        qh = qk[:, h * HEAD_DIM:(h + 1) * HEAD_DIM].astype(BF16)
        kh_t = k_t[h * HEAD_DIM:(h + 1) * HEAD_DIM, :]
        v_ext = jnp.concatenate([v[:, h * HEAD_DIM:(h + 1) * HEAD_DIM], one_col], axis=-1).astype(BF16)
        pw = jnp.exp(jnp.where(mask, col_a[:, fl:fl + 1] + rel_t[fl:fl + 1, :], -jnp.inf))
        scores = jnp.dot(qh, kh_t.astype(BF16), preferred_element_type=F32)
        nd = jnp.dot((scores * pw).astype(BF16), v_ext, preferred_element_type=F32)
        st = s_ref[h]
        nd = nd + a_inter[:, fl:fl + 1] * jnp.dot(qh, st.astype(BF16), preferred_element_type=F32)
        den = nd[:, HEAD_DIM:HEAD_DIM + 1]
        ys.append(nd[:, 0:HEAD_DIM] / jnp.maximum(jnp.abs(den), floor[:, fl:fl + 1]))
        upd = jnp.dot((kh_t * e_end_t[fl:fl + 1, :]).astype(BF16), v_ext, preferred_element_type=F32)
        s_ref[h] = a_prev[:, fl:fl + 1] * st + a_loc[:, fl:fl + 1] * upd
    m_ref[...] = jnp.broadcast_to(m_new, m_ref.shape)
    y_all = jnp.concatenate(ys, axis=-1)
    if finalize:
        hs = y_all + hb_ref[0]
        hc = hs - jnp.dot(hs.astype(BF16), pool_ref[...], preferred_element_type=F32)
        var = jnp.dot((hc * hc).astype(BF16), pool_ref[...], preferred_element_type=F32)
        y_all = hc * lax.rsqrt(var + EPS) * nw_ref[...] * jax.nn.sigmoid(u_ref[0, :, ML_O0:ML_G0])
    y_ref[0] = y_all

    @pl.when(j == nc - 1)
    def _():
        s_fin_ref[0] = s_ref[...]
        m_fin_ref[0] = m_ref[...]


def ml_pass(u, h_other, init, mp, direction, q):
    b, L, _ = u.shape
    q = min(q, L)
    nc = L // q
    finalize = h_other is not None
    in_specs, cidx = _scan_specs(L, q, nc, ML_COLS, direction)
    const2 = lambda bi, j: (0, 0)
    s_spec = pl.BlockSpec((1, ML_HEADS, HEAD_DIM, LANE), lambda bi, j: (bi, 0, 0, 0))
    m_spec = pl.BlockSpec((1, SUBLANE, LANE), lambda bi, j: (bi, 0, 0))
    y_spec = pl.BlockSpec((1, q, ML_WIDTH), lambda bi, j: (bi, cidx(j), 0))
    args = [u, u, u]
    if finalize:
        in_specs.append(y_spec)
        args.append(h_other)
    consts = [mp['cw'], mp['cb'], mp['gb']] + ([mp['nw'], mp['pool']] if finalize else [])
    in_specs += [s_spec, m_spec] + [pl.BlockSpec(t.shape, const2) for t in consts]
    args += [init[0], init[1]] + consts
    y, s_fin, m_fin = pl.pallas_call(
        functools.partial(_ml_kernel, direction=direction, finalize=finalize, q=q, nc=nc),
        out_shape=(jax.ShapeDtypeStruct((b, L, ML_WIDTH), F32),
                   jax.ShapeDtypeStruct((b, ML_HEADS, HEAD_DIM, LANE), F32),
                   jax.ShapeDtypeStruct((b, SUBLANE, LANE), F32)),
        grid=(b, nc),
        in_specs=in_specs,
        out_specs=(y_spec, s_spec, m_spec),
        scratch_shapes=[pltpu.VMEM((ML_HEADS, HEAD_DIM, LANE), F32), pltpu.VMEM((SUBLANE, LANE), F32)],
        compiler_params=_params("parallel", "arbitrary"),
        name="mlstm_scan_%s" % ("fwd" if direction == 0 else "bwd"),
    )(*args)
    return y, (s_fin, m_fin)


def ml_prepare(p):
    gb = p['ml_gate_b'].reshape(1, -1)
    head = np.arange(ML_WIDTH) // HEAD_DIM
    pool = _host_bf16((head[:, None] == head[None, :]) / HEAD_DIM)
    return dict(cw=p['ml_conv_w'], cb=p['ml_conv_b'].reshape(1, -1),
                gb=jnp.pad(gb, ((0, 0), (0, LANE - gb.shape[1]))), nw=p['ml_norm_w'].reshape(1, -1), pool=pool)


def ml_mixer(u, mp, init_f, init_b, want_y, q=256):
    hb, fin_b = ml_pass(u, None, init_b, mp, 1, q)
    y, fin_f = ml_pass(u, hb if want_y else None, init_f, mp, 0, q)
    return y, fin_f, fin_b


FFT_L = 4096
FFT_N = 2 * FFT_L
FFT_N2 = 128
FFT_N1 = FFT_N // FFT_N2
FFT_N1H = FFT_L // FFT_N2
FFT_K1 = FFT_N1 // 2 + 1
FFT_R = 80
FFT_UNROLL = 16
FFT_K1_UNROLL = 11
FFT_PITCH = FFT_N2 + SUBLANE


def _fft_tables():
    n2 = np.arange(FFT_N2)[:, None, None]
    k1 = np.arange(FFT_K1)[None, :, None]
    n1 = np.arange(FFT_N1H)[None, None, :]
    th = 2 * np.pi * (((FFT_N2 * n1 + n2) * k1) % FFT_N) / FFT_N
    f1 = np.zeros((FFT_N2, FFT_R, FFT_N1H))
    f1[:, 0:2 * FFT_K1:2, :] = np.cos(th)
    f1[:, 1:2 * FFT_K1:2, :] = -np.sin(th)
    wgt = np.where((np.arange(FFT_K1) == 0) | (np.arange(FFT_K1) == FFT_N1 // 2), 1.0, 2.0)[None, :, None] / FFT_N
    g1 = np.zeros((FFT_N2, FFT_N1H, FFT_R))
    g1[:, :, 0:2 * FFT_K1:2] = np.transpose(wgt * np.cos(th), (0, 2, 1))
    g1[:, :, 1:2 * FFT_K1:2] = np.transpose(-wgt * np.sin(th), (0, 2, 1))
    ph = 2 * np.pi * ((np.arange(FFT_N2)[:, None] * np.arange(FFT_N2)[None, :]) % FFT_N2) / FFT_N2
    c, s = np.cos(ph), np.sin(ph)
    f2 = np.block([[c, s], [-s, c]])
    sign = np.where(np.arange(FFT_R) // 2 % 2 == 0, 1.0, -1.0)[None, :, None]
    f1ab = np.concatenate([f1, f1 * sign], axis=2)
    return _host_bf16(f1), _host_bf16(f1ab), _host_bf16(f2), _host_bf16(f2.T), _host_bf16(g1)


def _sld(ref, n2, count):
    rows = pl.ds(n2, count, stride=FFT_PITCH)
    return jnp.concatenate([ref[0, rows, :], ref[1, rows, :]], axis=-1)


def _sst(ref, n2, count, val):
    rows = pl.ds(n2, count, stride=FFT_PITCH)
    ref[0, rows, :] = val[:, 0:LANE]
    ref[1, rows, :] = val[:, LANE:2 * LANE]


def _blk_ld(ref, blk, nblk):
    parts = []
    for k in range(nblk):
        rows = pl.ds(pl.multiple_of((blk + k) * FFT_PITCH, SUBLANE), FFT_N2)
        parts.append(jnp.concatenate([ref[0, rows, :], ref[1, rows, :]], axis=-1))
    return parts[0] if nblk == 1 else jnp.concatenate(parts, axis=0)


def _blk_st(ref, blk, nblk, val):
    for k in range(nblk):
        rows = pl.ds(pl.multiple_of((blk + k) * FFT_PITCH, SUBLANE), FFT_N2)
        ref[0, rows, :] = val[k * FFT_N2:(k + 1) * FFT_N2, 0:LANE]
        ref[1, rows, :] = val[k * FFT_N2:(k + 1) * FFT_N2, LANE:2 * LANE]


def _fft_stage1(z_ref, a_ref, f1_ref, n_in=FFT_N1H):
    def body(n2, carry):
        xs = _sld(z_ref, n2, n_in).astype(BF16)
        _sst(a_ref, n2, FFT_R, jnp.dot(f1_ref[n2], xs, preferred_element_type=F32))
        return carry
    lax.fori_loop(0, FFT_N2, body, 0, unroll=FFT_UNROLL)


def _spectrum_kernel(x_ref, inorm_ref, f1_ref, f2_ref, o_ref, z_ref, a_ref):
    for half in range(2):
        for n1 in range(FFT_N1H):
            _blk_st(z_ref, half * FFT_N1H + n1, 1, x_ref[half, n1 * FFT_N2:(n1 + 1) * FFT_N2, :])
    _fft_stage1(z_ref, a_ref, f1_ref, 2 * FFT_N1H)
    inorm = inorm_ref[0]

    def body(k1, carry):
        slab = _blk_ld(a_ref, 2 * k1, 2).astype(BF16)
        o_ref[0, k1] = (jnp.dot(f2_ref[...], slab, preferred_element_type=F32) * inorm).astype(BF16)
        return carry
    lax.fori_loop(0, FFT_K1, body, 0, unroll=FFT_K1_UNROLL)


def hyena_filter_spectra(sig, inv_norm):
    _, L, c = sig.shape
    _, f1ab, f2, _, _ = _fft_tables()
    one = pl.Buffered(1)
    return pl.pallas_call(
        _spectrum_kernel,
        out_shape=jax.ShapeDtypeStruct((HYENA_ORDER, FFT_K1, 2 * FFT_N2, c), BF16),
        grid=(HYENA_ORDER,),
        in_specs=[pl.BlockSpec((2, L, c), lambda i: (i, 0, 0)),
                  pl.BlockSpec((1, 1, c), lambda i: (i, 0, 0)),
                  pl.BlockSpec(f1ab.shape, lambda i: (0, 0, 0), pipeline_mode=one),
                  pl.BlockSpec(f2.shape, lambda i: (0, 0), pipeline_mode=one)],
        out_specs=pl.BlockSpec((1, FFT_K1, 2 * FFT_N2, c), lambda i: (i, 0, 0, 0)),
        scratch_shapes=[pltpu.VMEM((2, 2 * FFT_N1H * FFT_PITCH, LANE), F32),
                        pltpu.VMEM((2, FFT_R * FFT_PITCH, LANE), F32)],
        compiler_params=pltpu.CompilerParams(dimension_semantics=("arbitrary",), vmem_limit_bytes=VMEM_LIMIT_HY),
        name="hyena_filter_spectrum",
    )(sig, inv_norm, f1ab, f2)


def _conv3_rows(src, dst, cw, cb, L):
    rows = 2 * FFT_N2
    nchunk = L // rows

    def body(i, carry):
        r0 = pl.multiple_of(i * rows, rows)
        prev_row = jnp.where(i > 0, src[pl.ds(jnp.maximum(r0 - 1, 0), 1), :], 0.0)
        next_row = jnp.where(i < nchunk - 1, src[pl.ds(jnp.minimum(r0 + rows, L - 1), 1), :], 0.0)
        _blk_st(dst, 2 * i, 2, _conv3(src[pl.ds(r0, rows), :], prev_row, next_row, cw, cb, rows))
        return carry
    lax.fori_loop(0, nchunk, body, 0, unroll=2)


def _hyena_kernel(v_ref, g_ref, h_ref, cwv_ref, cbv_ref, cwg_ref, cbg_ref, skip_ref,
                  f1_ref, f2_ref, f2t_ref, g1_ref, o_ref, z_ref, gc_ref, a_ref):
    order = pl.program_id(1)

    @pl.when(order == 0)
    def _():
        _conv3_rows(v_ref.at[0], z_ref, cwv_ref[...], cbv_ref[...], FFT_L)

    _conv3_rows(g_ref.at[0], gc_ref, cwg_ref[0], cbg_ref[0], FFT_L)
    _fft_stage1(z_ref, a_ref, f1_ref)

    def mid(k1, carry):
        x = jnp.dot(f2_ref[...], _blk_ld(a_ref, 2 * k1, 2).astype(BF16), preferred_element_type=F32)
        h = h_ref[0, k1].astype(F32)
        xr, xi, hr, hi = x[:FFT_N2], x[FFT_N2:], h[:FFT_N2], h[FFT_N2:]
        y = jnp.concatenate([xr * hr - xi * hi, xr * hi + xi * hr], axis=0).astype(BF16)
        _blk_st(a_ref, 2 * k1, 2, jnp.dot(f2t_ref[...], y, preferred_element_type=F32))
        return carry
    lax.fori_loop(0, FFT_K1, mid, 0, unroll=FFT_K1_UNROLL)

    skip = skip_ref[0]

    def last(n2, carry):
        bs = _sld(a_ref, n2, FFT_R).astype(BF16)
        y = jnp.dot(g1_ref[n2], bs, preferred_element_type=F32)
        _sst(z_ref, n2, FFT_N1H, _sld(gc_ref, n2, FFT_N1H) * (y + skip * _sld(z_ref, n2, FFT_N1H)))
        return carry
    lax.fori_loop(0, FFT_N2, last, 0, unroll=FFT_UNROLL)

    @pl.when(order == 1)
    def _():
        for n1 in range(FFT_N1H):
            o_ref[0, n1 * FFT_N2:(n1 + 1) * FFT_N2, :] = _blk_ld(z_ref, n1, 1)


def hyena_long(u, h_spec, cw, cb, skip):
    b, L, _ = u.shape
    c = HYENA_WIDTH
    f1, _, f2, f2t, g1 = _fft_tables()
    one = pl.Buffered(1)
    cw3 = cw.reshape(3, 3, c).transpose(1, 0, 2)
    cb3 = cb.reshape(3, 1, c)
    return pl.pallas_call(
        _hyena_kernel,
        out_shape=jax.ShapeDtypeStruct((b, L, c), F32),
        grid=(b, 2),
        in_specs=[pl.BlockSpec((1, L, c), lambda bi, o: (bi, 0, 0), pipeline_mode=one),
                  pl.BlockSpec((1, L, c), lambda bi, o: (bi, 0, 1 + o)),
                  pl.BlockSpec((1, FFT_K1, 2 * FFT_N2, c), lambda bi, o: (o, 0, 0, 0)),
                  pl.BlockSpec((3, c), lambda bi, o: (0, 0)),
                  pl.BlockSpec((1, c), lambda bi, o: (0, 0)),
                  pl.BlockSpec((1, 3, c), lambda bi, o: (1 + o, 0, 0)),
                  pl.BlockSpec((1, 1, c), lambda bi, o: (1 + o, 0, 0)),
                  pl.BlockSpec((1, 1, c), lambda bi, o: (o, 0, 0)),
                  pl.BlockSpec(f1.shape, lambda bi, o: (0, 0, 0), pipeline_mode=one),
                  pl.BlockSpec(f2.shape, lambda bi, o: (0, 0), pipeline_mode=one),
                  pl.BlockSpec(f2t.shape, lambda bi, o: (0, 0), pipeline_mode=one),
                  pl.BlockSpec(g1.shape, lambda bi, o: (0, 0, 0), pipeline_mode=one)],
        out_specs=pl.BlockSpec((1, L, c), lambda bi, o: (bi, 0, 0)),
        scratch_shapes=[pltpu.VMEM((2, FFT_N1H * FFT_PITCH, LANE), F32),
                        pltpu.VMEM((2, FFT_N1H * FFT_PITCH, LANE), F32),
                        pltpu.VMEM((2, FFT_R * FFT_PITCH, LANE), F32)],
        compiler_params=pltpu.CompilerParams(dimension_semantics=("parallel", "arbitrary"),
                                             vmem_limit_bytes=VMEM_LIMIT_HY),
        name="hyena_long_conv",
    )(u, u, h_spec, cw3[0], cb3[0], cw3, cb3, skip.reshape(2, 1, c), f1, f2, f2t, g1)


HY_FILT = 2 * HYENA_ORDER * HYENA_WIDTH
HY_HALF = HYENA_ORDER * HYENA_WIDTH


def _filter_kernel(wt_ref, wc_ref, ws_ref, b1_ref, w2_ref, b2_ref, w3f_ref, w3b_ref, freq_ref, decay_ref,
                   sig_ref, asum_ref, *, L, rows):
    i = pl.program_id(0)
    fh = w2_ref.shape[0] // 2
    scale = 1.0 / float(max(L - 1, 1))

    @pl.when(i == 0)
    def _():
        asum_ref[...] = jnp.zeros_like(asum_ref)

    n_lane = (i * rows + lax.broadcasted_iota(jnp.int32, (1, rows), 1)).astype(F32)
    sub = lax.broadcasted_iota(jnp.int32, (2 * HYENA_BANDS, 1), 0)
    bands = 1e-4 + (sub % HYENA_BANDS).astype(F32) * ((HYENA_BANDS - 1 - 1e-4) / (HYENA_BANDS - 1))
    ang_t = (2 * math.pi / L) * jnp.where(sub < HYENA_BANDS, n_lane, L - n_lane) * bands
    cos_f, sin_f = jnp.cos(ang_t).T, jnp.sin(ang_t).T

    n = (i * rows + lax.broadcasted_iota(jnp.int32, (rows, 1), 0)).astype(F32)
    tu_f, tu_b = n * scale, (L - n) * scale
    lane = lax.broadcasted_iota(jnp.int32, (rows, 2 * fh), 1)
    freq = freq_ref[...]
    pre = (jnp.where(lane < fh, tu_f, tu_b) * wt_ref[...]
           + jnp.dot(cos_f, wc_ref[...], preferred_element_type=F32, precision=HI)
           - jnp.dot(sin_f, ws_ref[...], preferred_element_type=F32, precision=HI) + b1_ref[...])
    hdn = jnp.sin(freq * pre)
    hdn = jnp.sin(freq * (jnp.dot(hdn, w2_ref[...], preferred_element_type=F32, precision=HI) + b2_ref[...]))
    for side, (w3_ref, tu) in enumerate(((w3f_ref, tu_f), (w3b_ref, tu_b))):
        cols = slice(side * HY_HALF, (side + 1) * HY_HALF)
        val = jnp.dot(hdn, w3_ref[...], preferred_element_type=F32, precision=HI)
        val = val * jnp.exp(-tu * jnp.abs(decay_ref[:, cols]))
        if side == 1:
            val = jnp.where(n > 0, val, 0.0)
        for o in range(HYENA_ORDER):
            sig_ref[2 * o + side] = val[:, o * HYENA_WIDTH:(o + 1) * HYENA_WIDTH]
        asum_ref[side:side + 1, :] += jnp.sum(jnp.abs(val), axis=0, keepdims=True)


def _block_diag2(w):
    z = jnp.zeros_like(w)
    return jnp.concatenate([jnp.concatenate([w, z], axis=1), jnp.concatenate([z, w], axis=1)], axis=0)


def hyena_filter_signals(L, p):
    rows = min(L, 512)
    fh = p['hy_pos_w1'].shape[1]
    w1, w3 = p['hy_pos_w1'], p['hy_pos_w3']
    twice = lambda v: jnp.tile(v.reshape(1, fh), (1, 2))
    zeros = jnp.zeros((fh, HY_HALF), F32)
    consts = [twice(w1[0]), _block_diag2(w1[1:1 + HYENA_BANDS]), _block_diag2(w1[1 + HYENA_BANDS:]),
              twice(p['hy_pos_b1']), _block_diag2(p['hy_pos_w2']), twice(p['hy_pos_b2']),
              jnp.concatenate([w3[:, :HY_HALF], zeros], axis=0), jnp.concatenate([zeros, w3[:, HY_HALF:]], axis=0),
              twice(p['hy_freq']), p['hy_decay'].reshape(1, HY_FILT)]
    sig, asum = pl.pallas_call(
        functools.partial(_filter_kernel, L=L, rows=rows),
        out_shape=(jax.ShapeDtypeStruct((2 * HYENA_ORDER, L, HYENA_WIDTH), F32),
                   jax.ShapeDtypeStruct((SUBLANE, HY_HALF), F32)),
        grid=(L // rows,),
        in_specs=[pl.BlockSpec(c.shape, lambda i: (0, 0)) for c in consts],
        out_specs=(pl.BlockSpec((2 * HYENA_ORDER, rows, HYENA_WIDTH), lambda i: (0, i, 0)),
                   pl.BlockSpec((SUBLANE, HY_HALF), lambda i: (0, 0))),
        compiler_params=_params("arbitrary"),
        name="hyena_filters",
    )(*consts)
    inv_norm = 1.0 / (asum[0] + asum[1]).reshape(HYENA_ORDER, 1, HYENA_WIDTH)
    return sig, inv_norm


def _rdft_tables(L):
    n_bins = L + 1
    half = -(-n_bins // 16) * 16
    k = np.arange(n_bins)[:, None]
    n = np.arange(L)[None, :]
    th = 2 * np.pi * ((k * n) % (2 * L)) / (2 * L)
    f = np.zeros((2 * half, L))
    f[:n_bins] = np.cos(th)
    f[half:half + n_bins] = -np.sin(th)
    sign = np.where(np.arange(n_bins) % 2 == 0, 1.0, -1.0)[:, None]
    fs = np.zeros_like(f)
    fs[:n_bins] = f[:n_bins] * sign
    fs[half:half + n_bins] = f[half:half + n_bins] * sign
    wgt = np.where((np.arange(n_bins) == 0) | (np.arange(n_bins) == L), 1.0, 2.0)[None, :] / (2 * L)
    g = np.zeros((L, 2 * half))
    g[:, :n_bins] = wgt * np.cos(th).T
    g[:, half:half + n_bins] = -wgt * np.sin(th).T
    return _host_bf16(f), _host_bf16(fs), _host_bf16(g), half


def _hyena_ctx_kernel(u_ref, sig_ref, inorm_ref, cw_ref, cb_ref, skip_ref, f_ref, fs_ref, g_ref, o_ref, *, L, half):
    zero_row = jnp.zeros((1, u_ref.shape[2]), F32)
    uc = _conv3(u_ref[0], zero_row, zero_row, cw_ref[...], cb_ref[...], L)
    z = uc[:, 0:HYENA_WIDTH]
    for o in range(HYENA_ORDER):
        h = (jnp.dot(f_ref[...], sig_ref[2 * o].astype(BF16), preferred_element_type=F32)
             + jnp.dot(fs_ref[...], sig_ref[2 * o + 1].astype(BF16), preferred_element_type=F32)) * inorm_ref[o]
        x = jnp.dot(f_ref[...], z.astype(BF16), preferred_element_type=F32)
        xr, xi, hr, hi = x[:half], x[half:], h[:half], h[half:]
        y = jnp.concatenate([xr * hr - xi * hi, xr * hi + xi * hr], axis=0).astype(BF16)
        conv = jnp.dot(g_ref[...], y, preferred_element_type=F32)
        z = uc[:, (o + 1) * HYENA_WIDTH:(o + 2) * HYENA_WIDTH] * (conv + skip_ref[o] * z)
    o_ref[0] = z


def hyena_short(u, sig, inv_norm, cw, cb, skip):
    b, L, cols = u.shape
    f, fs, g, half = _rdft_tables(L)
    const2 = lambda bi: (0, 0)
    const3 = lambda bi: (0, 0, 0)
    return pl.pallas_call(
        functools.partial(_hyena_ctx_kernel, L=L, half=half),
        out_shape=jax.ShapeDtypeStruct((b, L, HYENA_WIDTH), F32),
        grid=(b,),
        in_specs=[pl.BlockSpec((1, L, cols), lambda bi: (bi, 0, 0)),
                  pl.BlockSpec(sig.shape, const3), pl.BlockSpec(inv_norm.shape, const3),
                  pl.BlockSpec(cw.shape, const2), pl.BlockSpec((1, cols), const2),
                  pl.BlockSpec((HYENA_ORDER, 1, HYENA_WIDTH), const3),
                  pl.BlockSpec(f.shape, const2), pl.BlockSpec(fs.shape, const2), pl.BlockSpec(g.shape, const2)],
        out_specs=pl.BlockSpec((1, L, HYENA_WIDTH), lambda bi: (bi, 0, 0)),
        compiler_params=_params("parallel"),
        name="hyena_context",
    )(u, sig, inv_norm, cw, cb.reshape(1, cols), skip.reshape(HYENA_ORDER, 1, HYENA_WIDTH), f, fs, g)


def _split_bf16(a):
    hi = a.astype(BF16)
    return hi, (a - hi.astype(F32)).astype(BF16)


def _norm_router_kernel(*refs, n_lat, two_src):
    if two_src:
        xl_ref, xc_ref, nw_ref, sh_ref, sc_ref, whi_ref, wlo_ref, br_ref, tri_ref, h_ref, lg_ref, cnt_ref = refs
        x = jnp.where(pl.program_id(0) < n_lat, xl_ref[...], xc_ref[...])
    else:
        xl_ref, nw_ref, sh_ref, sc_ref, whi_ref, wlo_ref, br_ref, tri_ref, h_ref, lg_ref, cnt_ref = refs
        x = xl_ref[...]
    h = _modnorm(x, nw_ref[...], sh_ref[0], sc_ref[0])
    h_ref[...] = h
    h_hi, h_lo = _split_bf16(h)
    lg = (jnp.dot(h_hi, whi_ref[...], preferred_element_type=F32)
          + jnp.dot(h_hi, wlo_ref[...], preferred_element_type=F32)
          + jnp.dot(h_lo, whi_ref[...], preferred_element_type=F32)) + br_ref[...]
    lane = lax.broadcasted_iota(jnp.int32, lg.shape, 1)
    first = lambda hit: jnp.min(jnp.where(hit, lane, ROUTER_COLS), axis=-1, keepdims=True)
    gl = jnp.where(lane < N_GROUPS, lg, -jnp.inf)
    gmax = jnp.max(gl, axis=-1, keepdims=True)
    grp = first(gl == gmax)
    grp_p = 1.0 / jnp.sum(jnp.exp(gl - gmax), axis=-1, keepdims=True)
    lo = N_GROUPS + grp * EXPERTS_PER_GROUP
    el = jnp.where((lane >= lo) & (lane < lo + EXPERTS_PER_GROUP), lg, -jnp.inf)
    e1 = jnp.max(el, axis=-1, keepdims=True)
    i1 = first(el == e1)
    el2 = jnp.where(lane == i1, -jnp.inf, el)
    e2 = jnp.max(el2, axis=-1, keepdims=True)
    i2 = first(el2 == e2)
    r = jnp.exp(e2 - e1)
    w1 = grp_p / (1.0 + r)
    w2 = w1 * r
    @pl.when(pl.program_id(0) == 0)
    def _():
        cnt_ref[...] = jnp.zeros_like(cnt_ref)

    hit1, hit2 = lane == i1, lane == i2
    picks = jnp.where(hit1 | hit2, 1.0, 0.0)
    before = cnt_ref[0:1, :] + jnp.dot(tri_ref[...], picks.astype(BF16), preferred_element_type=F32)
    rank1 = jnp.sum(jnp.where(hit1, before, 0.0), axis=-1, keepdims=True)
    rank2 = jnp.sum(jnp.where(hit2, before, 0.0), axis=-1, keepdims=True)
    cnt_ref[...] = jnp.broadcast_to(cnt_ref[0:1, :] + jnp.sum(picks, axis=0, keepdims=True), cnt_ref.shape)
    vals = [(i1 - N_GROUPS).astype(F32), (i2 - N_GROUPS).astype(F32), w1, w2, rank1, rank2]
    out = jnp.zeros(lg.shape, F32)
    for k, val in enumerate(vals):
        out = jnp.where(lane == k, val, out)
    lg_ref[...] = out


def norm_router(xl, xc, nw, mod_l, mod_c, w_router, b_router, tm=512):
    b, L, d = xl.shape
    two_src = xc is not None
    n_lat = b * L // tm
    per_batch = L // tm
    n_ctx = (xc.shape[0] * xc.shape[1]) // tm if two_src else 0
    w_hi, w_lo = _split_bf16(w_router)
    const = lambda i: (0, 0)
    if two_src:
        shift = jnp.concatenate([mod_l[0], mod_c[0]], axis=0)
        scale = jnp.concatenate([mod_l[1], mod_c[1]], axis=0)
        mod_map = lambda i: (jnp.where(i < n_lat, i // per_batch, b), 0, 0)
        srcs = [xl.reshape(b * L, d), xc.reshape(-1, d)]
        src_specs = [pl.BlockSpec((tm, d), lambda i: (jnp.minimum(i, n_lat - 1), 0)),
                     pl.BlockSpec((tm, d), lambda i: (jnp.maximum(i - n_lat, 0), 0))]
    else:
        shift, scale = mod_l
        mod_map = lambda i: (i // per_batch, 0, 0)
        srcs = [xl.reshape(b * L, d)]
        src_specs = [pl.BlockSpec((tm, d), lambda i: (i, 0))]
    n_tok = (n_lat + n_ctx) * tm
    tri = _host_bf16(np.tril(np.ones((tm, tm)), -1))
    return pl.pallas_call(
        functools.partial(_norm_router_kernel, n_lat=n_lat, two_src=two_src),
        out_shape=(jax.ShapeDtypeStruct((n_tok, d), F32), jax.ShapeDtypeStruct((n_tok, ROUTER_COLS), F32),
                   jax.ShapeDtypeStruct((SUBLANE, ROUTER_COLS), F32)),
        grid=(n_lat + n_ctx,),
        in_specs=src_specs + [pl.BlockSpec((1, d), const), pl.BlockSpec((1, 1, d), mod_map),
                              pl.BlockSpec((1, 1, d), mod_map), pl.BlockSpec((d, ROUTER_COLS), const),
                              pl.BlockSpec((d, ROUTER_COLS), const), pl.BlockSpec((1, ROUTER_COLS), const),
                              pl.BlockSpec((tm, tm), const)],
        out_specs=(pl.BlockSpec((tm, d), lambda i: (i, 0)), pl.BlockSpec((tm, ROUTER_COLS), lambda i: (i, 0)),
                   pl.BlockSpec((SUBLANE, ROUTER_COLS), const)),
        compiler_params=_params("arbitrary"),
        name="moe_norm_router",
    )(*srcs, nw.reshape(1, d), shift, scale, w_hi, w_lo, b_router, tri)


def _expert_ffn_kernel(te_ref, tv_ref, x_ref, wg_ref, wu_ref, wd_ref, o_ref):
    i = pl.program_id(0)

    @pl.when(tv_ref[i] > 0)
    def _():
        x = x_ref[...].astype(BF16)
        g = jnp.dot(x, wg_ref[0].astype(BF16), preferred_element_type=F32)
        u = jnp.dot(x, wu_ref[0].astype(BF16), preferred_element_type=F32)
        hid = _silu(g) * u
        o_ref[...] = jnp.dot(hid.astype(BF16), wd_ref[0].astype(BF16), preferred_element_type=F32)

    @pl.when(tv_ref[i] == 0)
    def _():
        o_ref[...] = jnp.zeros_like(o_ref)


def expert_ffn(x_sorted, tile_expert, tile_valid, w_gate, w_up, w_down, tm):
    r, d = x_sorted.shape
    f = w_gate.shape[-1]
    grid_spec = pltpu.PrefetchScalarGridSpec(
        num_scalar_prefetch=2,
        grid=(r // tm,),
        in_specs=[pl.BlockSpec((tm, d), lambda i, te, tv: (i, 0)),
                  pl.BlockSpec((1, d, f), lambda i, te, tv: (te[i], 0, 0)),
                  pl.BlockSpec((1, d, f), lambda i, te, tv: (te[i], 0, 0)),
                  pl.BlockSpec((1, f, d), lambda i, te, tv: (te[i], 0, 0))],
        out_specs=pl.BlockSpec((tm, d), lambda i, te, tv: (i, 0)),
    )
    return pl.pallas_call(
        _expert_ffn_kernel,
        out_shape=jax.ShapeDtypeStruct((r, d), F32),
        grid_spec=grid_spec,
        compiler_params=_params("arbitrary"),
        name="moe_expert_ffn",
    )(tile_expert, tile_valid, x_sorted, w_gate, w_up, w_down)


def moe_apply(h_tokens, routed, counts_row, w_gate, w_up, w_down, layer, n_lat, tm=256):
    t, d = h_tokens.shape
    e_idx, rank = routed[:, 0:2].astype(jnp.int32), routed[:, 4:6].astype(jnp.int32)
    counts = counts_row[0, N_GROUPS:N_GROUPS + N_EXPERTS].astype(jnp.int32)
    n_pairs = 2 * t
    padded = (counts + tm - 1) // tm * tm
    pad_end = jnp.cumsum(padded)
    pad_start = pad_end - padded
    n_rows = n_pairs + N_EXPERTS * tm
    tile_start = jnp.arange(n_rows // tm, dtype=jnp.int32) * tm
    tile_expert = jnp.minimum(jnp.sum((pad_end[None, :] <= tile_start[:, None]).astype(jnp.int32), axis=1),
                              N_EXPERTS - 1)
    tile_valid = (tile_start < pad_end[-1]).astype(jnp.int32)
    onehot = (e_idx[:, :, None] == jnp.arange(N_EXPERTS, dtype=jnp.int32)).astype(jnp.int32)
    pos = jnp.sum(onehot * pad_start, axis=-1) + rank
    j = jnp.arange(tm, dtype=jnp.int32)[None, :]
    fill_key = jnp.where(j < (padded - counts)[:, None], (pad_start + counts)[:, None] + j, n_rows)
    keys = jnp.concatenate([pos.reshape(-1), fill_key.reshape(-1)])
    toks = jnp.concatenate([jnp.arange(n_pairs, dtype=jnp.int32) // 2, jnp.arange(N_EXPERTS * tm, dtype=jnp.int32) % t])
    _, row_token = lax.sort((keys, toks), num_keys=1)
    x_sorted = h_tokens[row_token]
    wg = w_gate.reshape(-1, d, EXPERT_HIDDEN)
    wu = w_up.reshape(-1, d, EXPERT_HIDDEN)
    wd = w_down.reshape(-1, EXPERT_HIDDEN, d)
    y_sorted = expert_ffn(x_sorted, tile_expert + layer * N_EXPERTS, tile_valid, wg, wu, wd, tm)
    lat = (y_sorted[pos[:n_lat, 0]], y_sorted[pos[:n_lat, 1]])
    rest = (y_sorted[pos[n_lat:, 0]], y_sorted[pos[n_lat:, 1]]) if t > n_lat else None
    return lat, rest


def _final_kernel(x_ref, ya_ref, yb_ref, rt_ref, g_ref, w_ref, o_ref):
    x = x_ref[0] + g_ref[0] * _moe_mix(ya_ref[0], yb_ref[0], rt_ref[0])
    o_ref[0] = x * lax.rsqrt(jnp.mean(x * x, axis=-1, keepdims=True) + EPS) * w_ref[...]


def final_norm(x, ya, yb, routed, gate, w, tm=512):
    b, L, d = x.shape
    tok = pl.BlockSpec((1, tm, d), lambda bi, i: (bi, i, 0))
    return pl.pallas_call(
        _final_kernel,
        out_shape=jax.ShapeDtypeStruct((b, L, d), F32),
        grid=(b, L // tm),
        in_specs=[tok, tok, tok, pl.BlockSpec((1, tm, ROUTER_COLS), lambda bi, i: (bi, i, 0)),
                  pl.BlockSpec((1, 1, d), _mod_map(gate, b)), pl.BlockSpec((1, d), lambda bi, i: (0, 0))],
        out_specs=tok,
        compiler_params=_params("parallel", "arbitrary"),
        name="final_rmsnorm",
    )(x, ya, yb, routed, gate, w.reshape(1, d))


IN_SIZES = (SSD_WIDTH, SSD_CONV_CH, 2 * SSD_HEADS, HY_COLS, 2 * ML_WIDTH, ML_WIDTH, ML_WIDTH, 4 * ML_HEADS)


def _regroup_kernel(w_ref, o_ref):
    src = dst = 0
    for n in IN_SIZES:
        pad = -n % LANE
        o_ref[0, :, dst:dst + n] = w_ref[0, :, src:src + n].astype(BF16)
        if pad:
            o_ref[0, :, dst + n:dst + n + pad] = jnp.zeros((o_ref.shape[1], pad), BF16)
        src, dst = src + n, dst + n + pad


def regroup_in_weights(w_in):
    depth, d, n_in = w_in.shape
    n_out = sum(n + (-n % LANE) for n in IN_SIZES)
    one = pl.Buffered(1)
    return pl.pallas_call(
        _regroup_kernel,
        out_shape=jax.ShapeDtypeStruct((depth, d, n_out), BF16),
        grid=(depth,),
        in_specs=[pl.BlockSpec((1, d, n_in), lambda i: (i, 0, 0), pipeline_mode=one)],
        out_specs=pl.BlockSpec((1, d, n_out), lambda i: (i, 0, 0)),
        compiler_params=_params("arbitrary"),
        name="regroup_in_weights",
    )(w_in)


def kernel(x, c, ctx, c_ctx, w_mod, b_mod, norm1_w, norm2_w, w_in, w_out, ssd_conv_w, ssd_conv_b, ssd_dt_bias, ssd_a_log, ssd_d, ssd_norm_w, hy_conv_w, hy_conv_b, hy_pos_w1, hy_pos_b1, hy_pos_w2, hy_pos_b2, hy_pos_w3, hy_freq, hy_decay, hy_skip, ml_conv_w, ml_conv_b, ml_gate_b, ml_norm_w, grp_router_w, grp_router_b, exp_router_w, exp_router_b, moe_w_gate, moe_w_up, moe_w_down, final_norm_w):
    layer_params = dict(
        ssd_conv_w=ssd_conv_w, ssd_conv_b=ssd_conv_b, ssd_dt_bias=ssd_dt_bias, ssd_a_log=ssd_a_log,
        ssd_d=ssd_d, ssd_norm_w=ssd_norm_w, hy_conv_w=hy_conv_w, hy_conv_b=hy_conv_b,
        hy_pos_w1=hy_pos_w1, hy_pos_b1=hy_pos_b1, hy_pos_w2=hy_pos_w2, hy_pos_b2=hy_pos_b2,
        hy_pos_w3=hy_pos_w3, hy_freq=hy_freq, hy_decay=hy_decay, hy_skip=hy_skip,
        ml_conv_w=ml_conv_w, ml_conv_b=ml_conv_b, ml_gate_b=ml_gate_b, ml_norm_w=ml_norm_w)
    bsz, seq, d = x.shape
    n_ctx = ctx.shape[1]
    xl, xc = x, ctx
    moe_l = moe_c = None
    ssd0 = jnp.zeros((bsz, SSD_HEADS, HEAD_DIM, SSD_STATE), F32)
    ml0 = (jnp.zeros((bsz, ML_HEADS, HEAD_DIM, LANE), F32), jnp.zeros((bsz, SUBLANE, LANE), F32))
    c_rows = jnp.concatenate([c, c_ctx[None, :], jnp.zeros((SUBLANE - bsz - 1, d), F32)], axis=0)
    w_in_all = regroup_in_weights(w_in)
    for i in range(DEPTH):
        last = i == DEPTH - 1
        p = {name: arr[i] for name, arr in layer_params.items()}
        sp, mp = ssd_prepare(p), ml_prepare(p)
        mod = modulation(c_rows, w_mod, b_mod, i).reshape(SUBLANE, N_MOD, 1, d)
        mod_l = [mod[:bsz, k] for k in range(N_MOD)]
        mod_c = [mod[bsz:bsz + 1, k] for k in range(N_MOD)]
        w_out_b = w_out[i].astype(BF16)
        w_router = jnp.pad(jnp.concatenate([grp_router_w[i], exp_router_w[i]], axis=1),
                           ((0, 0), (0, ROUTER_COLS - N_GROUPS - N_EXPERTS)))
        b_router = jnp.pad(jnp.concatenate([grp_router_b[i], exp_router_b[i]]),
                           (0, ROUTER_COLS - N_GROUPS - N_EXPERTS)).reshape(1, ROUTER_COLS)

        uc_ssd, uc_hy, uc_ml, xc = norm_proj(xc, moe_c, norm1_w[i], mod_c[0], mod_c[1], w_in_all, i)
        yc_ssd, ssd_f, ssd_b = ssd_mixer(uc_ssd, sp, ssd0, ssd0, not last)
        yc_ml, ml_f, ml_b = ml_mixer(uc_ml, mp, ml0, ml0, not last)
        col_major = i % 2 == 1
        ul_ssd, ul_hy, ul_ml, xl = norm_proj(xl, moe_l, norm1_w[i], mod_l[0], mod_l[1], w_in_all, i, col_major)
        yl_ssd, _, _ = ssd_mixer(ul_ssd, sp, ssd_f, ssd_b, True)
        yl_ml, _, _ = ml_mixer(ul_ml, mp, ml_f, ml_b, True)
        h_spec = hyena_filter_spectra(*hyena_filter_signals(seq, p))
        yl_hy = hyena_long(ul_hy, h_spec, p['hy_conv_w'], p['hy_conv_b'], p['hy_skip'])
        xl = out_proj(yl_ssd, yl_hy, yl_ml, xl, mod_l[2], w_out_b, col_major)
        if not last:
            sig_c, inorm_c = hyena_filter_signals(n_ctx, p)
            yc_hy = hyena_short(uc_hy, sig_c, inorm_c, p['hy_conv_w'], p['hy_conv_b'], p['hy_skip'])
            xc = out_proj(yc_ssd, yc_hy, yc_ml, xc, mod_c[2], w_out_b)
        h_all, routed, counts_row = norm_router(xl, None if last else xc, norm2_w[i], (mod_l[3], mod_l[4]), (mod_c[3], mod_c[4]),
                                    w_router, b_router)
        lat, rest = moe_apply(h_all, routed, counts_row, moe_w_gate, moe_w_up, moe_w_down, i, bsz * seq)
        n_lat = bsz * seq
        moe_l = (lat[0].reshape(bsz, seq, d), lat[1].reshape(bsz, seq, d),
                 routed[:n_lat].reshape(bsz, seq, ROUTER_COLS), mod_l[5])
        if not last:
            moe_c = (rest[0].reshape(bsz, n_ctx, d), rest[1].reshape(bsz, n_ctx, d),
                     routed[n_lat:].reshape(bsz, n_ctx, ROUTER_COLS), mod_c[5])
    return final_norm(xl, *moe_l, final_norm_w)
```

```python
import functools
import math

import jax
import jax.numpy as jnp
import numpy as np
from jax import lax
from jax.experimental import pallas as pl
from jax.experimental.pallas import tpu as pltpu

D_MODEL = 1024
DEPTH = 2
GRID_W = 64
HEAD_DIM = 64
SSD_WIDTH = 384
SSD_HEADS = SSD_WIDTH // HEAD_DIM
SSD_GROUPS = 2
SSD_STATE = 64
HYENA_WIDTH = 256
HYENA_ORDER = 2
HYENA_BANDS = 16
ML_WIDTH = 384
ML_HEADS = ML_WIDTH // HEAD_DIM
N_GROUPS = 4
EXPERTS_PER_GROUP = 8
N_EXPERTS = N_GROUPS * EXPERTS_PER_GROUP
EXPERT_HIDDEN = 256
N_MOD = 6
EPS = 1e-6

LANE = 128
SUBLANE = 8
VMEM_LIMIT = 48 * 1024 * 1024
VMEM_LIMIT_HY = 56 * 1024 * 1024

SSD_CONV_CH = SSD_WIDTH + 2 * SSD_GROUPS * SSD_STATE
SSD_XBC0 = SSD_WIDTH
SSD_DT0 = SSD_XBC0 + SSD_CONV_CH
SSD_COLS = SSD_DT0 + LANE
HY_COLS = (HYENA_ORDER + 1) * HYENA_WIDTH
ML_V0 = 2 * ML_WIDTH
ML_O0 = ML_V0 + ML_WIDTH
ML_G0 = ML_O0 + ML_WIDTH
ML_COLS = ML_G0 + LANE
ROUTER_COLS = LANE

F32 = jnp.float32
BF16 = jnp.bfloat16
HI = lax.Precision.HIGHEST


def _params(*sem):
    return pltpu.CompilerParams(dimension_semantics=sem, vmem_limit_bytes=VMEM_LIMIT)


def _host_bf16(a):
    return jnp.asarray(np.asarray(a, np.float32).astype(BF16))


def _silu(x):
    return x * jax.nn.sigmoid(x)


def _softplus(x):
    return jnp.maximum(x, 0.0) + jnp.log(1.0 + jnp.exp(-jnp.abs(x)))


def _log_sigmoid(x):
    return jnp.minimum(x, 0.0) - jnp.log(1.0 + jnp.exp(-jnp.abs(x)))


def _mod_kernel(c_ref, w_ref, b_ref, o_ref):
    o_ref[...] = jnp.dot(_silu(c_ref[...]), w_ref[...], preferred_element_type=F32, precision=HI) + b_ref[...]


def modulation(c_rows, w_mod, b_mod, layer):
    depth, d, n = w_mod.shape
    tn = 1536
    return pl.pallas_call(
        _mod_kernel,
        out_shape=jax.ShapeDtypeStruct((c_rows.shape[0], n), F32),
        grid=(n // tn,),
        in_specs=[pl.BlockSpec(c_rows.shape, lambda j: (0, 0)),
                  pl.BlockSpec((None, d, tn), lambda j: (layer, 0, j)),
                  pl.BlockSpec((None, 1, tn), lambda j: (layer, 0, j))],
        out_specs=pl.BlockSpec((c_rows.shape[0], tn), lambda j: (0, j)),
        compiler_params=_params("arbitrary"),
        name="adaln_modulation",
    )(c_rows, w_mod, b_mod.reshape(depth, 1, n))


def _modnorm(x, nw, shift, scale):
    y = x * lax.rsqrt(jnp.mean(x * x, axis=-1, keepdims=True) + EPS) * nw
    return y * (1.0 + scale) + shift


def _mod_map(mod, b):
    return (lambda bi, i: (bi, 0, 0)) if mod.shape[0] == b else (lambda bi, i: (0, 0, 0))


def _tok_view(x, col_major):
    b, L, d = x.shape
    return x.reshape(b, L // GRID_W, GRID_W, d) if col_major else x


def _tok_spec(L, d, tm, col_major):
    if col_major:
        assert tm == (L // GRID_W) * SUBLANE
        return pl.BlockSpec((1, L // GRID_W, SUBLANE, d), lambda bi, i: (bi, 0, i, 0))
    return pl.BlockSpec((1, tm, d), lambda bi, i: (bi, i, 0))


def _tok_load(ref, col_major):
    if not col_major:
        return ref[0]
    return jnp.concatenate([ref[0, :, j, :] for j in range(ref.shape[2])], axis=0)


def _tok_store(ref, val, col_major):
    if not col_major:
        ref[0] = val
        return
    rows = ref.shape[1]
    for j in range(ref.shape[2]):
        ref[0, :, j, :] = val[j * rows:(j + 1) * rows]


def _moe_mix(ya, yb, routed):
    return routed[:, 2:3] * ya + routed[:, 3:4] * yb


def _norm_proj_kernel(*refs, col_major, fuse_moe):
    if fuse_moe:
        x_ref, ya_ref, yb_ref, rt_ref, g_ref, nw_ref, sh_ref, sc_ref, w_ref, ssd_ref, hy_ref, ml_ref, xo_ref = refs
    else:
        x_ref, nw_ref, sh_ref, sc_ref, w_ref, ssd_ref, hy_ref, ml_ref = refs
    x = _tok_load(x_ref, col_major)
    if fuse_moe:
        x = x + g_ref[0] * _moe_mix(_tok_load(ya_ref, col_major), _tok_load(yb_ref, col_major),
                                    _tok_load(rt_ref, col_major))
        _tok_store(xo_ref, x, col_major)
    h = _modnorm(x, nw_ref[...], sh_ref[0], sc_ref[0])
    u = jnp.dot(h.astype(BF16), w_ref[...], preferred_element_type=F32)
    ssd_ref[0] = u[:, 0:SSD_COLS]
    hy_ref[0] = u[:, SSD_COLS:SSD_COLS + HY_COLS]
    ml_ref[0] = u[:, SSD_COLS + HY_COLS:]


def norm_proj(x, moe, nw, shift, scale, w_all, layer, col_major=False, tm=512):
    b, L, d = x.shape
    n = w_all.shape[2]
    fuse_moe = moe is not None
    tm = (L // GRID_W) * SUBLANE if col_major else min(tm, L)
    tok = _tok_spec(L, d, tm, col_major)
    row = lambda bi, i: (bi, i, 0)
    const2 = lambda bi, i: (0, 0)
    args, in_specs = [_tok_view(x, col_major)], [tok]
    if fuse_moe:
        ya, yb, routed, gate = moe
        args += [_tok_view(ya, col_major), _tok_view(yb, col_major), _tok_view(routed, col_major), gate]
        in_specs += [tok, tok, _tok_spec(L, ROUTER_COLS, tm, col_major), pl.BlockSpec((1, 1, d), _mod_map(gate, b))]
    args += [nw.reshape(1, d), shift, scale, w_all]
    in_specs += [pl.BlockSpec((1, d), const2), pl.BlockSpec((1, 1, d), _mod_map(shift, b)),
                 pl.BlockSpec((1, 1, d), _mod_map(scale, b)),
                 pl.BlockSpec((None, d, n), lambda bi, i: (layer, 0, 0))]
    out_shape = [jax.ShapeDtypeStruct((b, L, SSD_COLS), F32), jax.ShapeDtypeStruct((b, L, HY_COLS), F32),
                 jax.ShapeDtypeStruct((b, L, ML_COLS), F32)]
    out_specs = [pl.BlockSpec((1, tm, SSD_COLS), row), pl.BlockSpec((1, tm, HY_COLS), row),
                 pl.BlockSpec((1, tm, ML_COLS), row)]
    if fuse_moe:
        out_shape.append(jax.ShapeDtypeStruct(args[0].shape, F32))
        out_specs.append(tok)
    outs = pl.pallas_call(
        functools.partial(_norm_proj_kernel, col_major=col_major, fuse_moe=fuse_moe),
        out_shape=tuple(out_shape),
        grid=(b, L // tm),
        in_specs=in_specs,
        out_specs=tuple(out_specs),
        compiler_params=_params("parallel", "arbitrary"),
        name="norm_in_proj",
    )(*args)
    return (*outs[:3], outs[3].reshape(b, L, d) if fuse_moe else x)


def _out_proj_kernel(ys_ref, yh_ref, ym_ref, x_ref, g_ref, w_ref, o_ref, *, col_major):
    y = jnp.concatenate([ys_ref[0], yh_ref[0], ym_ref[0]], axis=-1).astype(BF16)
    r = _tok_load(x_ref, col_major) + g_ref[0] * jnp.dot(y, w_ref[...], preferred_element_type=F32)
    _tok_store(o_ref, r, col_major)


def out_proj(y_ssd, y_hy, y_ml, x, gate, w_bf16, col_major=False, tm=512):
    b, L, d = x.shape
    tm = (L // GRID_W) * SUBLANE if col_major else min(tm, L)
    row = lambda bi, i: (bi, i, 0)
    tok = _tok_spec(L, d, tm, col_major)
    xv = _tok_view(x, col_major)
    return pl.pallas_call(
        functools.partial(_out_proj_kernel, col_major=col_major),
        out_shape=jax.ShapeDtypeStruct(xv.shape, F32),
        grid=(b, L // tm),
        in_specs=[pl.BlockSpec((1, tm, SSD_WIDTH), row), pl.BlockSpec((1, tm, HYENA_WIDTH), row),
                  pl.BlockSpec((1, tm, ML_WIDTH), row), tok,
                  pl.BlockSpec((1, 1, d), _mod_map(gate, b)),
                  pl.BlockSpec(w_bf16.shape, lambda bi, i: (0, 0))],
        out_specs=tok,
        compiler_params=_params("parallel", "arbitrary"),
        name="out_proj_residual",
    )(y_ssd, y_hy, y_ml, xv, gate, w_bf16).reshape(b, L, d)


def _conv3(xr, prev_row, next_row, cw, cb, q):
    rid = lax.broadcasted_iota(jnp.int32, (q, 1), 0)
    x_prev = jnp.where(rid == 0, prev_row, pltpu.roll(xr, 1, axis=0))
    x_next = jnp.where(rid == q - 1, next_row, pltpu.roll(xr, q - 1, axis=0))
    return x_prev * cw[0:1] + xr * cw[1:2] + x_next * cw[2:3] + cb


def _masked_scan(mask, x):
    m = mask.astype(BF16)
    hi = x.astype(BF16)
    r1 = x - hi.astype(F32)
    mid = r1.astype(BF16)
    lo = (r1 - mid.astype(F32)).astype(BF16)
    return (jnp.dot(m, hi, preferred_element_type=F32) + jnp.dot(m, mid, preferred_element_type=F32)
            + jnp.dot(m, lo, preferred_element_type=F32))


def _scan_mask(q, direction):
    li = lax.broadcasted_iota(jnp.int32, (q, q), 0)
    si = lax.broadcasted_iota(jnp.int32, (q, q), 1)
    return (si <= li) if direction == 0 else (si >= li)


def _running_max(x, direction, q):
    rid = lax.broadcasted_iota(jnp.int32, (q, 1), 0)
    s = 1
    while s < q:
        if direction == 0:
            x = jnp.where(rid >= s, jnp.maximum(x, pltpu.roll(x, s, axis=0)), x)
        else:
            x = jnp.where(rid < q - s, jnp.maximum(x, pltpu.roll(x, q - s, axis=0)), x)
        s *= 2
    return x


def _scan_specs(L, q, nc, cols, direction):
    hb = q // SUBLANE
    nrb = L // SUBLANE
    cidx = (lambda j: j) if direction == 0 else (lambda j: nc - 1 - j)
    specs = [pl.BlockSpec((1, q, cols), lambda bi, j: (bi, cidx(j), 0)),
             pl.BlockSpec((1, SUBLANE, cols), lambda bi, j: (bi, jnp.maximum(cidx(j) * hb - 1, 0), 0)),
             pl.BlockSpec((1, SUBLANE, cols), lambda bi, j: (bi, jnp.minimum((cidx(j) + 1) * hb, nrb - 1), 0))]
    return specs, cidx


def _ssd_kernel(*refs, direction, finalize, q, nc):
    if finalize:
        (u_ref, prev_ref, next_ref, yb_ref, init_ref, cw_ref, cb_ref, dtb_ref, a_ref, d_ref, nw_ref,
         y_ref, fin_ref, state_ref) = refs
    else:
        (u_ref, prev_ref, next_ref, init_ref, cw_ref, cb_ref, dtb_ref, a_ref,
         y_ref, fin_ref, state_ref) = refs
    j = pl.program_id(1)
    c = j if direction == 0 else nc - 1 - j

    @pl.when(j == 0)
    def _():
        state_ref[...] = init_ref[0]

    prev_row = jnp.where(c > 0, prev_ref[0, SUBLANE - 1:SUBLANE, SSD_XBC0:SSD_DT0], 0.0)
    next_row = jnp.where(c < nc - 1, next_ref[0, 0:1, SSD_XBC0:SSD_DT0], 0.0)
    xc = _silu(_conv3(u_ref[0, :, SSD_XBC0:SSD_DT0], prev_row, next_row, cw_ref[...], cb_ref[...], q))

    dt = _softplus(u_ref[0, :, SSD_DT0:SSD_COLS] + dtb_ref[...])
    mask = _scan_mask(q, direction)
    cum = _masked_scan(mask, dt * a_ref[...])
    cum_t = cum.T
    end = q - 1 if direction == 0 else 0

    ys = []
    for g in range(SSD_GROUPS):
        b0 = SSD_WIDTH + g * SSD_STATE
        c0 = SSD_WIDTH + (SSD_GROUPS + g) * SSD_STATE
        bm_t = xc[:, b0:b0 + SSD_STATE].T
        cm = xc[:, c0:c0 + SSD_STATE].astype(BF16)
        scores = jnp.dot(cm, bm_t.astype(BF16), preferred_element_type=F32)
        for h in range(g * (SSD_HEADS // SSD_GROUPS), (g + 1) * (SSD_HEADS // SSD_GROUPS)):
            hl = direction * SSD_HEADS + h
            col = cum[:, hl:hl + 1]
            row = cum_t[hl:hl + 1, :]
            seg = jnp.exp(jnp.where(mask, col - row, -jnp.inf))
            xdt = (xc[:, h * HEAD_DIM:(h + 1) * HEAD_DIM] * dt[:, hl:hl + 1]).astype(BF16)
            y = jnp.dot((scores * seg).astype(BF16), xdt, preferred_element_type=F32)
            st = state_ref[h]
            y = y + jnp.dot(cm, st.astype(BF16), preferred_element_type=F32) * jnp.exp(col)
            tot = cum[end:end + 1, hl:hl + 1]
            upd = jnp.dot((bm_t * jnp.exp(tot - row)).astype(BF16), xdt, preferred_element_type=F32)
            state_ref[h] = st * jnp.exp(tot) + upd
            ys.append(y)
    y_all = jnp.concatenate(ys, axis=-1)
    if finalize:
        t = (y_all + yb_ref[0] + xc[:, 0:SSD_WIDTH] * d_ref[...]) * _silu(u_ref[0, :, 0:SSD_WIDTH])
        y_all = t * lax.rsqrt(jnp.mean(t * t, axis=-1, keepdims=True) + EPS) * nw_ref[...]
    y_ref[0] = y_all

    @pl.when(j == nc - 1)
    def _():
        fin_ref[0] = state_ref[...]


def ssd_pass(u, y_other, init, sp, direction, q):
    b, L, _ = u.shape
    q = min(q, L)
    nc = L // q
    finalize = y_other is not None
    in_specs, cidx = _scan_specs(L, q, nc, SSD_COLS, direction)
    const2 = lambda bi, j: (0, 0)
    st_spec = pl.BlockSpec((1, SSD_HEADS, HEAD_DIM, SSD_STATE), lambda bi, j: (bi, 0, 0, 0))
    y_spec = pl.BlockSpec((1, q, SSD_WIDTH), lambda bi, j: (bi, cidx(j), 0))
    args = [u, u, u]
    if finalize:
        in_specs.append(y_spec)
        args.append(y_other)
    consts = [sp['cw'], sp['cb'], sp['dtb'], sp['a']] + ([sp['d'], sp['nw']] if finalize else [])
    in_specs += [st_spec] + [pl.BlockSpec(t.shape, const2) for t in consts]
    args += [init] + consts
    return pl.pallas_call(
        functools.partial(_ssd_kernel, direction=direction, finalize=finalize, q=q, nc=nc),
        out_shape=(jax.ShapeDtypeStruct((b, L, SSD_WIDTH), F32),
                   jax.ShapeDtypeStruct((b, SSD_HEADS, HEAD_DIM, SSD_STATE), F32)),
        grid=(b, nc),
        in_specs=in_specs,
        out_specs=(y_spec, st_spec),
        scratch_shapes=[pltpu.VMEM((SSD_HEADS, HEAD_DIM, SSD_STATE), F32)],
        compiler_params=_params("parallel", "arbitrary"),
        name="ssd_scan_%s" % ("fwd" if direction == 0 else "bwd"),
    )(*args)


def ssd_prepare(p):
    pad = lambda v: jnp.pad(v.reshape(1, -1), ((0, 0), (0, LANE - v.size)))
    return dict(cw=p['ssd_conv_w'], cb=p['ssd_conv_b'].reshape(1, -1),
                dtb=pad(p['ssd_dt_bias']), a=pad(-jnp.exp(p['ssd_a_log'])),
                d=jnp.repeat(p['ssd_d'], HEAD_DIM).reshape(1, -1), nw=p['ssd_norm_w'].reshape(1, -1))


def ssd_mixer(u, sp, init_f, init_b, want_y, q=256):
    yb, fin_b = ssd_pass(u, None, init_b, sp, 1, q)
    y, fin_f = ssd_pass(u, yb if want_y else None, init_f, sp, 0, q)
    return y, fin_f, fin_b


def _ml_kernel(*refs, direction, finalize, q, nc):
    if finalize:
        (u_ref, prev_ref, next_ref, hb_ref, s_init_ref, m_init_ref, cw_ref, cb_ref, gb_ref, nw_ref, pool_ref,
         y_ref, s_fin_ref, m_fin_ref, s_ref, m_ref) = refs
    else:
        (u_ref, prev_ref, next_ref, s_init_ref, m_init_ref, cw_ref, cb_ref, gb_ref,
         y_ref, s_fin_ref, m_fin_ref, s_ref, m_ref) = refs
    j = pl.program_id(1)
    c = j if direction == 0 else nc - 1 - j

    @pl.when(j == 0)
    def _():
        s_ref[...] = s_init_ref[0]
        m_ref[...] = m_init_ref[0]

    prev_row = jnp.where(c > 0, prev_ref[0, SUBLANE - 1:SUBLANE, 0:ML_V0], 0.0)
    next_row = jnp.where(c < nc - 1, next_ref[0, 0:1, 0:ML_V0], 0.0)
    qk = _silu(_conv3(u_ref[0, :, 0:ML_V0], prev_row, next_row, cw_ref[...], cb_ref[...], q))
    v = u_ref[0, :, ML_V0:ML_O0]

    gb = u_ref[0, :, ML_G0:ML_COLS] + gb_ref[...]
    mask = _scan_mask(q, direction)
    cum = jnp.dot(mask.astype(F32), _log_sigmoid(gb), preferred_element_type=F32, precision=HI)
    ig = pltpu.roll(gb, ML_HEADS, axis=1)
    end = q - 1 if direction == 0 else 0
    m_prev = m_ref[0:1, :]
    tot = cum[end:end + 1, :]
    w_end = tot - cum + ig
    m_loc = jnp.max(w_end, axis=0, keepdims=True)
    e_end = jnp.exp(w_end - m_loc)
    m_new = jnp.maximum(tot + m_prev, m_loc)
    a_prev = jnp.exp(tot + m_prev - m_new)
    a_loc = jnp.exp(m_loc - m_new)
    inter = cum + m_prev
    rel = ig - cum
    m_t = jnp.maximum(inter, cum + _running_max(rel, direction, q))
    col_a = cum - m_t
    a_inter = jnp.exp(inter - m_t)
    floor = jnp.exp(-m_t)
    rel_t = rel.T
    e_end_t = e_end.T
    k_t = (qk[:, ML_WIDTH:2 * ML_WIDTH] * (HEAD_DIM ** -0.5)).T
    one_col = (lax.broadcasted_iota(jnp.int32, (q, HEAD_DIM), 1) == 0).astype(F32)

    ys = []
    for h in range(ML_HEADS):
        fl = direction * 2 * ML_HEADS + ML_HEADS + h
        qh = qk[:, h * HEAD_DIM:(h + 1) * HEAD_DIM].astype(BF16)
        kh_t = k_t[h * HEAD_DIM:(h + 1) * HEAD_DIM, :]
        v_ext = jnp.concatenate([v[:, h * HEAD_DIM:(h + 1) * HEAD_DIM], one_col], axis=-1).astype(BF16)
        pw = jnp.exp(jnp.where(mask, col_a[:, fl:fl + 1] + rel_t[fl:fl + 1, :], -jnp.inf))
        scores = jnp.dot(qh, kh_t.astype(BF16), preferred_element_type=F32)
        nd = jnp.dot((scores * pw).astype(BF16), v_ext, preferred_element_type=F32)
        st = s_ref[h]
        nd = nd + a_inter[:, fl:fl + 1] * jnp.dot(qh, st.astype(BF16), preferred_element_type=F32)
        den = nd[:, HEAD_DIM:HEAD_DIM + 1]
        ys.append(nd[:, 0:HEAD_DIM] / jnp.maximum(jnp.abs(den), floor[:, fl:fl + 1]))
        upd = jnp.dot((kh_t * e_end_t[fl:fl + 1, :]).astype(BF16), v_ext, preferred_element_type=F32)
        s_ref[h] = a_prev[:, fl:fl + 1] * st + a_loc[:, fl:fl + 1] * upd
    m_ref[...] = jnp.broadcast_to(m_new, m_ref.shape)
    y_all = jnp.concatenate(ys, axis=-1)
    if finalize:
        hs = y_all + hb_ref[0]
        hc = hs - jnp.dot(hs.astype(BF16), pool_ref[...], preferred_element_type=F32)
        var = jnp.dot((hc * hc).astype(BF16), pool_ref[...], preferred_element_type=F32)
        y_all = hc * lax.rsqrt(var + EPS) * nw_ref[...] * jax.nn.sigmoid(u_ref[0, :, ML_O0:ML_G0])
    y_ref[0] = y_all

    @pl.when(j == nc - 1)
    def _():
        s_fin_ref[0] = s_ref[...]
        m_fin_ref[0] = m_ref[...]


def ml_pass(u, h_other, init, mp, direction, q):
    b, L, _ = u.shape
    q = min(q, L)
    nc = L // q
    finalize = h_other is not None
    in_specs, cidx = _scan_specs(L, q, nc, ML_COLS, direction)
    const2 = lambda bi, j: (0, 0)
    s_spec = pl.BlockSpec((1, ML_HEADS, HEAD_DIM, LANE), lambda bi, j: (bi, 0, 0, 0))
    m_spec = pl.BlockSpec((1, SUBLANE, LANE), lambda bi, j: (bi, 0, 0))
    y_spec = pl.BlockSpec((1, q, ML_WIDTH), lambda bi, j: (bi, cidx(j), 0))
    args = [u, u, u]
    if finalize:
        in_specs.append(y_spec)
        args.append(h_other)
    consts = [mp['cw'], mp['cb'], mp['gb']] + ([mp['nw'], mp['pool']] if finalize else [])
    in_specs += [s_spec, m_spec] + [pl.BlockSpec(t.shape, const2) for t in consts]
    args += [init[0], init[1]] + consts
    y, s_fin, m_fin = pl.pallas_call(
        functools.partial(_ml_kernel, direction=direction, finalize=finalize, q=q, nc=nc),
        out_shape=(jax.ShapeDtypeStruct((b, L, ML_WIDTH), F32),
                   jax.ShapeDtypeStruct((b, ML_HEADS, HEAD_DIM, LANE), F32),
                   jax.ShapeDtypeStruct((b, SUBLANE, LANE), F32)),
        grid=(b, nc),
        in_specs=in_specs,
        out_specs=(y_spec, s_spec, m_spec),
        scratch_shapes=[pltpu.VMEM((ML_HEADS, HEAD_DIM, LANE), F32), pltpu.VMEM((SUBLANE, LANE), F32)],
        compiler_params=_params("parallel", "arbitrary"),
        name="mlstm_scan_%s" % ("fwd" if direction == 0 else "bwd"),
    )(*args)
    return y, (s_fin, m_fin)


def ml_prepare(p):
    gb = p['ml_gate_b'].reshape(1, -1)
    head = np.arange(ML_WIDTH) // HEAD_DIM
    pool = _host_bf16((head[:, None] == head[None, :]) / HEAD_DIM)
    return dict(cw=p['ml_conv_w'], cb=p['ml_conv_b'].reshape(1, -1),
                gb=jnp.pad(gb, ((0, 0), (0, LANE - gb.shape[1]))), nw=p['ml_norm_w'].reshape(1, -1), pool=pool)


def ml_mixer(u, mp, init_f, init_b, want_y, q=256):
    hb, fin_b = ml_pass(u, None, init_b, mp, 1, q)
    y, fin_f = ml_pass(u, hb if want_y else None, init_f, mp, 0, q)
    return y, fin_f, fin_b


FFT_L = 4096
FFT_N = 2 * FFT_L
FFT_N2 = 128
FFT_N1 = FFT_N // FFT_N2
FFT_N1H = FFT_L // FFT_N2
FFT_K1 = FFT_N1 // 2 + 1
FFT_R = 80
FFT_UNROLL = 16
FFT_K1_UNROLL = 11
FFT_PITCH = FFT_N2 + SUBLANE


def _fft_tables():
    n2 = np.arange(FFT_N2)[:, None, None]
    k1 = np.arange(FFT_K1)[None, :, None]
    n1 = np.arange(FFT_N1H)[None, None, :]
    th = 2 * np.pi * (((FFT_N2 * n1 + n2) * k1) % FFT_N) / FFT_N
    f1 = np.zeros((FFT_N2, FFT_R, FFT_N1H))
    f1[:, 0:2 * FFT_K1:2, :] = np.cos(th)
    f1[:, 1:2 * FFT_K1:2, :] = -np.sin(th)
    wgt = np.where((np.arange(FFT_K1) == 0) | (np.arange(FFT_K1) == FFT_N1 // 2), 1.0, 2.0)[None, :, None] / FFT_N
    g1 = np.zeros((FFT_N2, FFT_N1H, FFT_R))
    g1[:, :, 0:2 * FFT_K1:2] = np.transpose(wgt * np.cos(th), (0, 2, 1))
    g1[:, :, 1:2 * FFT_K1:2] = np.transpose(-wgt * np.sin(th), (0, 2, 1))
    ph = 2 * np.pi * ((np.arange(FFT_N2)[:, None] * np.arange(FFT_N2)[None, :]) % FFT_N2) / FFT_N2
    c, s = np.cos(ph), np.sin(ph)
    f2 = np.block([[c, s], [-s, c]])
    sign = np.where(np.arange(FFT_R) // 2 % 2 == 0, 1.0, -1.0)[None, :, None]
    f1ab = np.concatenate([f1, f1 * sign], axis=2)
    return _host_bf16(f1), _host_bf16(f1ab), _host_bf16(f2), _host_bf16(f2.T), _host_bf16(g1)


def _sld(ref, n2, count):
    rows = pl.ds(n2, count, stride=FFT_PITCH)
    return jnp.concatenate([ref[0, rows, :], ref[1, rows, :]], axis=-1)


def _sst(ref, n2, count, val):
    rows = pl.ds(n2, count, stride=FFT_PITCH)
    ref[0, rows, :] = val[:, 0:LANE]
    ref[1, rows, :] = val[:, LANE:2 * LANE]


def _blk_ld(ref, blk, nblk):
    parts = []
    for k in range(nblk):
        rows = pl.ds(pl.multiple_of((blk + k) * FFT_PITCH, SUBLANE), FFT_N2)
        parts.append(jnp.concatenate([ref[0, rows, :], ref[1, rows, :]], axis=-1))
    return parts[0] if nblk == 1 else jnp.concatenate(parts, axis=0)


def _blk_st(ref, blk, nblk, val):
    for k in range(nblk):
        rows = pl.ds(pl.multiple_of((blk + k) * FFT_PITCH, SUBLANE), FFT_N2)
        ref[0, rows, :] = val[k * FFT_N2:(k + 1) * FFT_N2, 0:LANE]
        ref[1, rows, :] = val[k * FFT_N2:(k + 1) * FFT_N2, LANE:2 * LANE]


def _fft_stage1(z_ref, a_ref, f1_ref, n_in=FFT_N1H):
    def body(n2, carry):
        xs = _sld(z_ref, n2, n_in).astype(BF16)
        _sst(a_ref, n2, FFT_R, jnp.dot(f1_ref[n2], xs, preferred_element_type=F32))
        return carry
    lax.fori_loop(0, FFT_N2, body, 0, unroll=FFT_UNROLL)


def _spectrum_kernel(x_ref, inorm_ref, f1_ref, f2_ref, o_ref, z_ref, a_ref):
    for half in range(2):
        for n1 in range(FFT_N1H):
            _blk_st(z_ref, half * FFT_N1H + n1, 1, x_ref[half, n1 * FFT_N2:(n1 + 1) * FFT_N2, :])
    _fft_stage1(z_ref, a_ref, f1_ref, 2 * FFT_N1H)
    inorm = inorm_ref[0]

    def body(k1, carry):
        slab = _blk_ld(a_ref, 2 * k1, 2).astype(BF16)
        o_ref[0, k1] = (jnp.dot(f2_ref[...], slab, preferred_element_type=F32) * inorm).astype(BF16)
        return carry
    lax.fori_loop(0, FFT_K1, body, 0, unroll=FFT_K1_UNROLL)


def hyena_filter_spectra(sig, inv_norm):
    _, L, c = sig.shape
    _, f1ab, f2, _, _ = _fft_tables()
    one = pl.Buffered(1)
    return pl.pallas_call(
        _spectrum_kernel,
        out_shape=jax.ShapeDtypeStruct((HYENA_ORDER, FFT_K1, 2 * FFT_N2, c), BF16),
        grid=(HYENA_ORDER,),
        in_specs=[pl.BlockSpec((2, L, c), lambda i: (i, 0, 0)),
                  pl.BlockSpec((1, 1, c), lambda i: (i, 0, 0)),
                  pl.BlockSpec(f1ab.shape, lambda i: (0, 0, 0), pipeline_mode=one),
                  pl.BlockSpec(f2.shape, lambda i: (0, 0), pipeline_mode=one)],
        out_specs=pl.BlockSpec((1, FFT_K1, 2 * FFT_N2, c), lambda i: (i, 0, 0, 0)),
        scratch_shapes=[pltpu.VMEM((2, 2 * FFT_N1H * FFT_PITCH, LANE), F32),
                        pltpu.VMEM((2, FFT_R * FFT_PITCH, LANE), F32)],
        compiler_params=pltpu.CompilerParams(dimension_semantics=("arbitrary",), vmem_limit_bytes=VMEM_LIMIT_HY),
        name="hyena_filter_spectrum",
    )(sig, inv_norm, f1ab, f2)


def _conv3_rows(src, dst, cw, cb, L):
    rows = 2 * FFT_N2
    nchunk = L // rows

    def body(i, carry):
        r0 = pl.multiple_of(i * rows, rows)
        prev_row = jnp.where(i > 0, src[pl.ds(jnp.maximum(r0 - 1, 0), 1), :], 0.0)
        next_row = jnp.where(i < nchunk - 1, src[pl.ds(jnp.minimum(r0 + rows, L - 1), 1), :], 0.0)
        _blk_st(dst, 2 * i, 2, _conv3(src[pl.ds(r0, rows), :], prev_row, next_row, cw, cb, rows))
        return carry
    lax.fori_loop(0, nchunk, body, 0, unroll=2)


def _hyena_kernel(v_ref, g_ref, h_ref, cwv_ref, cbv_ref, cwg_ref, cbg_ref, skip_ref,
                  f1_ref, f2_ref, f2t_ref, g1_ref, o_ref, z_ref, gc_ref, a_ref):
    order = pl.program_id(1)

    @pl.when(order == 0)
    def _():
        _conv3_rows(v_ref.at[0], z_ref, cwv_ref[...], cbv_ref[...], FFT_L)

    _conv3_rows(g_ref.at[0], gc_ref, cwg_ref[0], cbg_ref[0], FFT_L)
    _fft_stage1(z_ref, a_ref, f1_ref)

    def mid(k1, carry):
        x = jnp.dot(f2_ref[...], _blk_ld(a_ref, 2 * k1, 2).astype(BF16), preferred_element_type=F32)
        h = h_ref[0, k1].astype(F32)
        xr, xi, hr, hi = x[:FFT_N2], x[FFT_N2:], h[:FFT_N2], h[FFT_N2:]
        y = jnp.concatenate([xr * hr - xi * hi, xr * hi + xi * hr], axis=0).astype(BF16)
        _blk_st(a_ref, 2 * k1, 2, jnp.dot(f2t_ref[...], y, preferred_element_type=F32))
        return carry
    lax.fori_loop(0, FFT_K1, mid, 0, unroll=FFT_K1_UNROLL)

    skip = skip_ref[0]

    def last(n2, carry):
        bs = _sld(a_ref, n2, FFT_R).astype(BF16)
        y = jnp.dot(g1_ref[n2], bs, preferred_element_type=F32)
        _sst(z_ref, n2, FFT_N1H, _sld(gc_ref, n2, FFT_N1H) * (y + skip * _sld(z_ref, n2, FFT_N1H)))
        return carry
    lax.fori_loop(0, FFT_N2, last, 0, unroll=FFT_UNROLL)

    @pl.when(order == 1)
    def _():
        for n1 in range(FFT_N1H):
            o_ref[0, n1 * FFT_N2:(n1 + 1) * FFT_N2, :] = _blk_ld(z_ref, n1, 1)


def hyena_long(u, h_spec, cw, cb, skip):
    b, L, _ = u.shape
    c = HYENA_WIDTH
    f1, _, f2, f2t, g1 = _fft_tables()
    one = pl.Buffered(1)
    cw3 = cw.reshape(3, 3, c).transpose(1, 0, 2)
    cb3 = cb.reshape(3, 1, c)
    return pl.pallas_call(
        _hyena_kernel,
        out_shape=jax.ShapeDtypeStruct((b, L, c), F32),
        grid=(b, 2),
        in_specs=[pl.BlockSpec((1, L, c), lambda bi, o: (bi, 0, 0), pipeline_mode=one),
                  pl.BlockSpec((1, L, c), lambda bi, o: (bi, 0, 1 + o)),
                  pl.BlockSpec((1, FFT_K1, 2 * FFT_N2, c), lambda bi, o: (o, 0, 0, 0)),
                  pl.BlockSpec((3, c), lambda bi, o: (0, 0)),
                  pl.BlockSpec((1, c), lambda bi, o: (0, 0)),
                  pl.BlockSpec((1, 3, c), lambda bi, o: (1 + o, 0, 0)),
                  pl.BlockSpec((1, 1, c), lambda bi, o: (1 + o, 0, 0)),
                  pl.BlockSpec((1, 1, c), lambda bi, o: (o, 0, 0)),
                  pl.BlockSpec(f1.shape, lambda bi, o: (0, 0, 0), pipeline_mode=one),
                  pl.BlockSpec(f2.shape, lambda bi, o: (0, 0), pipeline_mode=one),
                  pl.BlockSpec(f2t.shape, lambda bi, o: (0, 0), pipeline_mode=one),
                  pl.BlockSpec(g1.shape, lambda bi, o: (0, 0, 0), pipeline_mode=one)],
        out_specs=pl.BlockSpec((1, L, c), lambda bi, o: (bi, 0, 0)),
        scratch_shapes=[pltpu.VMEM((2, FFT_N1H * FFT_PITCH, LANE), F32),
                        pltpu.VMEM((2, FFT_N1H * FFT_PITCH, LANE), F32),
                        pltpu.VMEM((2, FFT_R * FFT_PITCH, LANE), F32)],
        compiler_params=pltpu.CompilerParams(dimension_semantics=("parallel", "arbitrary"),
                                             vmem_limit_bytes=VMEM_LIMIT_HY),
        name="hyena_long_conv",
    )(u, u, h_spec, cw3[0], cb3[0], cw3, cb3, skip.reshape(2, 1, c), f1, f2, f2t, g1)


HY_FILT = 2 * HYENA_ORDER * HYENA_WIDTH
HY_HALF = HYENA_ORDER * HYENA_WIDTH


def _filter_kernel(wt_ref, wc_ref, ws_ref, b1_ref, w2_ref, b2_ref, w3f_ref, w3b_ref, freq_ref, decay_ref,
                   sig_ref, asum_ref, *, L, rows):
    i = pl.program_id(0)
    fh = w2_ref.shape[0] // 2
    scale = 1.0 / float(max(L - 1, 1))

    @pl.when(i == 0)
    def _():
        asum_ref[...] = jnp.zeros_like(asum_ref)

    n_lane = (i * rows + lax.broadcasted_iota(jnp.int32, (1, rows), 1)).astype(F32)
    sub = lax.broadcasted_iota(jnp.int32, (2 * HYENA_BANDS, 1), 0)
    bands = 1e-4 + (sub % HYENA_BANDS).astype(F32) * ((HYENA_BANDS - 1 - 1e-4) / (HYENA_BANDS - 1))
    ang_t = (2 * math.pi / L) * jnp.where(sub < HYENA_BANDS, n_lane, L - n_lane) * bands
    cos_f, sin_f = jnp.cos(ang_t).T, jnp.sin(ang_t).T

    n = (i * rows + lax.broadcasted_iota(jnp.int32, (rows, 1), 0)).astype(F32)
    tu_f, tu_b = n * scale, (L - n) * scale
    lane = lax.broadcasted_iota(jnp.int32, (rows, 2 * fh), 1)
    freq = freq_ref[...]
    pre = (jnp.where(lane < fh, tu_f, tu_b) * wt_ref[...]
           + jnp.dot(cos_f, wc_ref[...], preferred_element_type=F32, precision=HI)
           - jnp.dot(sin_f, ws_ref[...], preferred_element_type=F32, precision=HI) + b1_ref[...])
    hdn = jnp.sin(freq * pre)
    hdn = jnp.sin(freq * (jnp.dot(hdn, w2_ref[...], preferred_element_type=F32, precision=HI) + b2_ref[...]))
    for side, (w3_ref, tu) in enumerate(((w3f_ref, tu_f), (w3b_ref, tu_b))):
        cols = slice(side * HY_HALF, (side + 1) * HY_HALF)
        val = jnp.dot(hdn, w3_ref[...], preferred_element_type=F32, precision=HI)
        val = val * jnp.exp(-tu * jnp.abs(decay_ref[:, cols]))
        if side == 1:
            val = jnp.where(n > 0, val, 0.0)
        for o in range(HYENA_ORDER):
            sig_ref[2 * o + side] = val[:, o * HYENA_WIDTH:(o + 1) * HYENA_WIDTH]
        asum_ref[side:side + 1, :] += jnp.sum(jnp.abs(val), axis=0, keepdims=True)


def _block_diag2(w):
    z = jnp.zeros_like(w)
    return jnp.concatenate([jnp.concatenate([w, z], axis=1), jnp.concatenate([z, w], axis=1)], axis=0)


def hyena_filter_signals(L, p):
    rows = min(L, 512)
    fh = p['hy_pos_w1'].shape[1]
    w1, w3 = p['hy_pos_w1'], p['hy_pos_w3']
    twice = lambda v: jnp.tile(v.reshape(1, fh), (1, 2))
    zeros = jnp.zeros((fh, HY_HALF), F32)
    consts = [twice(w1[0]), _block_diag2(w1[1:1 + HYENA_BANDS]), _block_diag2(w1[1 + HYENA_BANDS:]),
              twice(p['hy_pos_b1']), _block_diag2(p['hy_pos_w2']), twice(p['hy_pos_b2']),
              jnp.concatenate([w3[:, :HY_HALF], zeros], axis=0), jnp.concatenate([zeros, w3[:, HY_HALF:]], axis=0),
              twice(p['hy_freq']), p['hy_decay'].reshape(1, HY_FILT)]
    sig, asum = pl.pallas_call(
        functools.partial(_filter_kernel, L=L, rows=rows),
        out_shape=(jax.ShapeDtypeStruct((2 * HYENA_ORDER, L, HYENA_WIDTH), F32),
                   jax.ShapeDtypeStruct((SUBLANE, HY_HALF), F32)),
        grid=(L // rows,),
        in_specs=[pl.BlockSpec(c.shape, lambda i: (0, 0)) for c in consts],
        out_specs=(pl.BlockSpec((2 * HYENA_ORDER, rows, HYENA_WIDTH), lambda i: (0, i, 0)),
                   pl.BlockSpec((SUBLANE, HY_HALF), lambda i: (0, 0))),
        compiler_params=_params("arbitrary"),
        name="hyena_filters",
    )(*consts)
    inv_norm = 1.0 / (asum[0] + asum[1]).reshape(HYENA_ORDER, 1, HYENA_WIDTH)
    return sig, inv_norm


def _rdft_tables(L):
    n_bins = L + 1
    half = -(-n_bins // 16) * 16
    k = np.arange(n_bins)[:, None]
    n = np.arange(L)[None, :]
    th = 2 * np.pi * ((k * n) % (2 * L)) / (2 * L)
    f = np.zeros((2 * half, L))
    f[:n_bins] = np.cos(th)
    f[half:half + n_bins] = -np.sin(th)
    sign = np.where(np.arange(n_bins) % 2 == 0, 1.0, -1.0)[:, None]
    fs = np.zeros_like(f)
    fs[:n_bins] = f[:n_bins] * sign
    fs[half:half + n_bins] = f[half:half + n_bins] * sign
    wgt = np.where((np.arange(n_bins) == 0) | (np.arange(n_bins) == L), 1.0, 2.0)[None, :] / (2 * L)
    g = np.zeros((L, 2 * half))
    g[:, :n_bins] = wgt * np.cos(th).T
    g[:, half:half + n_bins] = -wgt * np.sin(th).T
    return _host_bf16(f), _host_bf16(fs), _host_bf16(g), half


def _hyena_ctx_kernel(u_ref, sig_ref, inorm_ref, cw_ref, cb_ref, skip_ref, f_ref, fs_ref, g_ref, o_ref, *, L, half):
    zero_row = jnp.zeros((1, u_ref.shape[2]), F32)
    uc = _conv3(u_ref[0], zero_row, zero_row, cw_ref[...], cb_ref[...], L)
    z = uc[:, 0:HYENA_WIDTH]
    for o in range(HYENA_ORDER):
        h = (jnp.dot(f_ref[...], sig_ref[2 * o].astype(BF16), preferred_element_type=F32)
             + jnp.dot(fs_ref[...], sig_ref[2 * o + 1].astype(BF16), preferred_element_type=F32)) * inorm_ref[o]
        x = jnp.dot(f_ref[...], z.astype(BF16), preferred_element_type=F32)
        xr, xi, hr, hi = x[:half], x[half:], h[:half], h[half:]
        y = jnp.concatenate([xr * hr - xi * hi, xr * hi + xi * hr], axis=0).astype(BF16)
        conv = jnp.dot(g_ref[...], y, preferred_element_type=F32)
        z = uc[:, (o + 1) * HYENA_WIDTH:(o + 2) * HYENA_WIDTH] * (conv + skip_ref[o] * z)
    o_ref[0] = z


def hyena_short(u, sig, inv_norm, cw, cb, skip):
    b, L, cols = u.shape
    f, fs, g, half = _rdft_tables(L)
    const2 = lambda bi: (0, 0)
    const3 = lambda bi: (0, 0, 0)
    return pl.pallas_call(
        functools.partial(_hyena_ctx_kernel, L=L, half=half),
        out_shape=jax.ShapeDtypeStruct((b, L, HYENA_WIDTH), F32),
        grid=(b,),
        in_specs=[pl.BlockSpec((1, L, cols), lambda bi: (bi, 0, 0)),
                  pl.BlockSpec(sig.shape, const3), pl.BlockSpec(inv_norm.shape, const3),
                  pl.BlockSpec(cw.shape, const2), pl.BlockSpec((1, cols), const2),
                  pl.BlockSpec((HYENA_ORDER, 1, HYENA_WIDTH), const3),
                  pl.BlockSpec(f.shape, const2), pl.BlockSpec(fs.shape, const2), pl.BlockSpec(g.shape, const2)],
        out_specs=pl.BlockSpec((1, L, HYENA_WIDTH), lambda bi: (bi, 0, 0)),
        compiler_params=_params("parallel"),
        name="hyena_context",
    )(u, sig, inv_norm, cw, cb.reshape(1, cols), skip.reshape(HYENA_ORDER, 1, HYENA_WIDTH), f, fs, g)


def _split_bf16(a):
    hi = a.astype(BF16)
    return hi, (a - hi.astype(F32)).astype(BF16)


def _norm_router_kernel(*refs, n_lat, two_src):
    if two_src:
        xl_ref, xc_ref, nw_ref, sh_ref, sc_ref, whi_ref, wlo_ref, br_ref, tri_ref, h_ref, lg_ref, cnt_ref = refs
        x = jnp.where(pl.program_id(0) < n_lat, xl_ref[...], xc_ref[...])
    else:
        xl_ref, nw_ref, sh_ref, sc_ref, whi_ref, wlo_ref, br_ref, tri_ref, h_ref, lg_ref, cnt_ref = refs
        x = xl_ref[...]
    h = _modnorm(x, nw_ref[...], sh_ref[0], sc_ref[0])
    h_ref[...] = h
    h_hi, h_lo = _split_bf16(h)
    lg = (jnp.dot(h_hi, whi_ref[...], preferred_element_type=F32)
          + jnp.dot(h_hi, wlo_ref[...], preferred_element_type=F32)
          + jnp.dot(h_lo, whi_ref[...], preferred_element_type=F32)) + br_ref[...]
    lane = lax.broadcasted_iota(jnp.int32, lg.shape, 1)
    first = lambda hit: jnp.min(jnp.where(hit, lane, ROUTER_COLS), axis=-1, keepdims=True)
    gl = jnp.where(lane < N_GROUPS, lg, -jnp.inf)
    gmax = jnp.max(gl, axis=-1, keepdims=True)
    grp = first(gl == gmax)
    grp_p = 1.0 / jnp.sum(jnp.exp(gl - gmax), axis=-1, keepdims=True)
    lo = N_GROUPS + grp * EXPERTS_PER_GROUP
    el = jnp.where((lane >= lo) & (lane < lo + EXPERTS_PER_GROUP), lg, -jnp.inf)
    e1 = jnp.max(el, axis=-1, keepdims=True)
    i1 = first(el == e1)
    el2 = jnp.where(lane == i1, -jnp.inf, el)
    e2 = jnp.max(el2, axis=-1, keepdims=True)
    i2 = first(el2 == e2)
    r = jnp.exp(e2 - e1)
    w1 = grp_p / (1.0 + r)
    w2 = w1 * r
    @pl.when(pl.program_id(0) == 0)
    def _():
        cnt_ref[...] = jnp.zeros_like(cnt_ref)

    hit1, hit2 = lane == i1, lane == i2
    picks = jnp.where(hit1 | hit2, 1.0, 0.0)
    before = cnt_ref[0:1, :] + jnp.dot(tri_ref[...], picks.astype(BF16), preferred_element_type=F32)
    rank1 = jnp.sum(jnp.where(hit1, before, 0.0), axis=-1, keepdims=True)
    rank2 = jnp.sum(jnp.where(hit2, before, 0.0), axis=-1, keepdims=True)
    cnt_ref[...] = jnp.broadcast_to(cnt_ref[0:1, :] + jnp.sum(picks, axis=0, keepdims=True), cnt_ref.shape)
    vals = [(i1 - N_GROUPS).astype(F32), (i2 - N_GROUPS).astype(F32), w1, w2, rank1, rank2]
    out = jnp.zeros(lg.shape, F32)
    for k, val in enumerate(vals):
        out = jnp.where(lane == k, val, out)
    lg_ref[...] = out


def norm_router(xl, xc, nw, mod_l, mod_c, w_router, b_router, tm=512):
    b, L, d = xl.shape
    two_src = xc is not None
    n_lat = b * L // tm
    per_batch = L // tm
    n_ctx = (xc.shape[0] * xc.shape[1]) // tm if two_src else 0
    w_hi, w_lo = _split_bf16(w_router)
    const = lambda i: (0, 0)
    if two_src:
        shift = jnp.concatenate([mod_l[0], mod_c[0]], axis=0)
        scale = jnp.concatenate([mod_l[1], mod_c[1]], axis=0)
        mod_map = lambda i: (jnp.where(i < n_lat, i // per_batch, b), 0, 0)
        srcs = [xl.reshape(b * L, d), xc.reshape(-1, d)]
        src_specs = [pl.BlockSpec((tm, d), lambda i: (jnp.minimum(i, n_lat - 1), 0)),
                     pl.BlockSpec((tm, d), lambda i: (jnp.maximum(i - n_lat, 0), 0))]
    else:
        shift, scale = mod_l
        mod_map = lambda i: (i // per_batch, 0, 0)
        srcs = [xl.reshape(b * L, d)]
        src_specs = [pl.BlockSpec((tm, d), lambda i: (i, 0))]
    n_tok = (n_lat + n_ctx) * tm
    tri = _host_bf16(np.tril(np.ones((tm, tm)), -1))
    return pl.pallas_call(
        functools.partial(_norm_router_kernel, n_lat=n_lat, two_src=two_src),
        out_shape=(jax.ShapeDtypeStruct((n_tok, d), F32), jax.ShapeDtypeStruct((n_tok, ROUTER_COLS), F32),
                   jax.ShapeDtypeStruct((SUBLANE, ROUTER_COLS), F32)),
        grid=(n_lat + n_ctx,),
        in_specs=src_specs + [pl.BlockSpec((1, d), const), pl.BlockSpec((1, 1, d), mod_map),
                              pl.BlockSpec((1, 1, d), mod_map), pl.BlockSpec((d, ROUTER_COLS), const),
                              pl.BlockSpec((d, ROUTER_COLS), const), pl.BlockSpec((1, ROUTER_COLS), const),
                              pl.BlockSpec((tm, tm), const)],
        out_specs=(pl.BlockSpec((tm, d), lambda i: (i, 0)), pl.BlockSpec((tm, ROUTER_COLS), lambda i: (i, 0)),
                   pl.BlockSpec((SUBLANE, ROUTER_COLS), const)),
        compiler_params=_params("arbitrary"),
        name="moe_norm_router",
    )(*srcs, nw.reshape(1, d), shift, scale, w_hi, w_lo, b_router, tri)


def _expert_ffn_kernel(te_ref, tv_ref, x_ref, wg_ref, wu_ref, wd_ref, o_ref):
    i = pl.program_id(0)

    @pl.when(tv_ref[i] > 0)
    def _():
        x = x_ref[...].astype(BF16)
        g = jnp.dot(x, wg_ref[0].astype(BF16), preferred_element_type=F32)
        u = jnp.dot(x, wu_ref[0].astype(BF16), preferred_element_type=F32)
        hid = _silu(g) * u
        o_ref[...] = jnp.dot(hid.astype(BF16), wd_ref[0].astype(BF16), preferred_element_type=F32)

    @pl.when(tv_ref[i] == 0)
    def _():
        o_ref[...] = jnp.zeros_like(o_ref)


def expert_ffn(x_sorted, tile_expert, tile_valid, w_gate, w_up, w_down, tm):
    r, d = x_sorted.shape
    f = w_gate.shape[-1]
    grid_spec = pltpu.PrefetchScalarGridSpec(
        num_scalar_prefetch=2,
        grid=(r // tm,),
        in_specs=[pl.BlockSpec((tm, d), lambda i, te, tv: (i, 0)),
                  pl.BlockSpec((1, d, f), lambda i, te, tv: (te[i], 0, 0)),
                  pl.BlockSpec((1, d, f), lambda i, te, tv: (te[i], 0, 0)),
                  pl.BlockSpec((1, f, d), lambda i, te, tv: (te[i], 0, 0))],
        out_specs=pl.BlockSpec((tm, d), lambda i, te, tv: (i, 0)),
    )
    return pl.pallas_call(
        _expert_ffn_kernel,
        out_shape=jax.ShapeDtypeStruct((r, d), F32),
        grid_spec=grid_spec,
        compiler_params=_params("arbitrary"),
        name="moe_expert_ffn",
    )(tile_expert, tile_valid, x_sorted, w_gate, w_up, w_down)


def moe_apply(h_tokens, routed, counts_row, w_gate, w_up, w_down, layer, n_lat, tm=256):
    t, d = h_tokens.shape
    e_idx, rank = routed[:, 0:2].astype(jnp.int32), routed[:, 4:6].astype(jnp.int32)
    counts = counts_row[0, N_GROUPS:N_GROUPS + N_EXPERTS].astype(jnp.int32)
    n_pairs = 2 * t
    padded = (counts + tm - 1) // tm * tm
    pad_end = jnp.cumsum(padded)
    pad_start = pad_end - padded
    n_rows = n_pairs + N_EXPERTS * tm
    tile_start = jnp.arange(n_rows // tm, dtype=jnp.int32) * tm
    tile_expert = jnp.minimum(jnp.sum((pad_end[None, :] <= tile_start[:, None]).astype(jnp.int32), axis=1),
                              N_EXPERTS - 1)
    tile_valid = (tile_start < pad_end[-1]).astype(jnp.int32)
    onehot = (e_idx[:, :, None] == jnp.arange(N_EXPERTS, dtype=jnp.int32)).astype(jnp.int32)
    pos = jnp.sum(onehot * pad_start, axis=-1) + rank
    j = jnp.arange(tm, dtype=jnp.int32)[None, :]
    fill_key = jnp.where(j < (padded - counts)[:, None], (pad_start + counts)[:, None] + j, n_rows)
    keys = jnp.concatenate([pos.reshape(-1), fill_key.reshape(-1)])
    toks = jnp.concatenate([jnp.arange(n_pairs, dtype=jnp.int32) // 2, jnp.arange(N_EXPERTS * tm, dtype=jnp.int32) % t])
    _, row_token = lax.sort((keys, toks), num_keys=1)
    x_sorted = h_tokens[row_token]
    wg = w_gate.reshape(-1, d, EXPERT_HIDDEN)
    wu = w_up.reshape(-1, d, EXPERT_HIDDEN)
    wd = w_down.reshape(-1, EXPERT_HIDDEN, d)
    y_sorted = expert_ffn(x_sorted, tile_expert + layer * N_EXPERTS, tile_valid, wg, wu, wd, tm)
    lat = (y_sorted[pos[:n_lat, 0]], y_sorted[pos[:n_lat, 1]])
    rest = (y_sorted[pos[n_lat:, 0]], y_sorted[pos[n_lat:, 1]]) if t > n_lat else None
    return lat, rest


def _final_kernel(x_ref, ya_ref, yb_ref, rt_ref, g_ref, w_ref, o_ref):
    x = x_ref[0] + g_ref[0] * _moe_mix(ya_ref[0], yb_ref[0], rt_ref[0])
    o_ref[0] = x * lax.rsqrt(jnp.mean(x * x, axis=-1, keepdims=True) + EPS) * w_ref[...]


def final_norm(x, ya, yb, routed, gate, w, tm=512):
    b, L, d = x.shape
    tok = pl.BlockSpec((1, tm, d), lambda bi, i: (bi, i, 0))
    return pl.pallas_call(
        _final_kernel,
        out_shape=jax.ShapeDtypeStruct((b, L, d), F32),
        grid=(b, L // tm),
        in_specs=[tok, tok, tok, pl.BlockSpec((1, tm, ROUTER_COLS), lambda bi, i: (bi, i, 0)),
                  pl.BlockSpec((1, 1, d), _mod_map(gate, b)), pl.BlockSpec((1, d), lambda bi, i: (0, 0))],
        out_specs=tok,
        compiler_params=_params("parallel", "arbitrary"),
        name="final_rmsnorm",
    )(x, ya, yb, routed, gate, w.reshape(1, d))


IN_SIZES = (SSD_WIDTH, SSD_CONV_CH, 2 * SSD_HEADS, HY_COLS, 2 * ML_WIDTH, ML_WIDTH, ML_WIDTH, 4 * ML_HEADS)


def _regroup_kernel(w_ref, o_ref):
    src = dst = 0
    for n in IN_SIZES:
        pad = -n % LANE
        o_ref[0, :, dst:dst + n] = w_ref[0, :, src:src + n].astype(BF16)
        if pad:
            o_ref[0, :, dst + n:dst + n + pad] = jnp.zeros((o_ref.shape[1], pad), BF16)
        src, dst = src + n, dst + n + pad


def regroup_in_weights(w_in):
    depth, d, n_in = w_in.shape
    n_out = sum(n + (-n % LANE) for n in IN_SIZES)
    one = pl.Buffered(1)
    return pl.pallas_call(
        _regroup_kernel,
        out_shape=jax.ShapeDtypeStruct((depth, d, n_out), BF16),
        grid=(depth,),
        in_specs=[pl.BlockSpec((1, d, n_in), lambda i: (i, 0, 0), pipeline_mode=one)],
        out_specs=pl.BlockSpec((1, d, n_out), lambda i: (i, 0, 0)),
        compiler_params=_params("arbitrary"),
        name="regroup_in_weights",
    )(w_in)


def kernel(x, c, ctx, c_ctx, w_mod, b_mod, norm1_w, norm2_w, w_in, w_out, ssd_conv_w, ssd_conv_b, ssd_dt_bias, ssd_a_log, ssd_d, ssd_norm_w, hy_conv_w, hy_conv_b, hy_pos_w1, hy_pos_b1, hy_pos_w2, hy_pos_b2, hy_pos_w3, hy_freq, hy_decay, hy_skip, ml_conv_w, ml_conv_b, ml_gate_b, ml_norm_w, grp_router_w, grp_router_b, exp_router_w, exp_router_b, moe_w_gate, moe_w_up, moe_w_down, final_norm_w):
    layer_params = dict(
        ssd_conv_w=ssd_conv_w, ssd_conv_b=ssd_conv_b, ssd_dt_bias=ssd_dt_bias, ssd_a_log=ssd_a_log,
        ssd_d=ssd_d, ssd_norm_w=ssd_norm_w, hy_conv_w=hy_conv_w, hy_conv_b=hy_conv_b,
        hy_pos_w1=hy_pos_w1, hy_pos_b1=hy_pos_b1, hy_pos_w2=hy_pos_w2, hy_pos_b2=hy_pos_b2,
        hy_pos_w3=hy_pos_w3, hy_freq=hy_freq, hy_decay=hy_decay, hy_skip=hy_skip,
        ml_conv_w=ml_conv_w, ml_conv_b=ml_conv_b, ml_gate_b=ml_gate_b, ml_norm_w=ml_norm_w)
    bsz, seq, d = x.shape
    n_ctx = ctx.shape[1]
    xl, xc = x, ctx
    moe_l = moe_c = None
    ssd0 = jnp.zeros((bsz, SSD_HEADS, HEAD_DIM, SSD_STATE), F32)
    ml0 = (jnp.zeros((bsz, ML_HEADS, HEAD_DIM, LANE), F32), jnp.zeros((bsz, SUBLANE, LANE), F32))
    c_rows = jnp.concatenate([c, c_ctx[None, :], jnp.zeros((SUBLANE - bsz - 1, d), F32)], axis=0)
    w_in_all = regroup_in_weights(w_in)

    def layer_prep(i):
        p = {name: arr[i] for name, arr in layer_params.items()}
        mod = modulation(c_rows, w_mod, b_mod, i).reshape(SUBLANE, N_MOD, 1, d)
        pad = ROUTER_COLS - N_GROUPS - N_EXPERTS
        return dict(
            p=p, sp=ssd_prepare(p), mp=ml_prepare(p),
            mod_l=[mod[:bsz, k] for k in range(N_MOD)],
            mod_c=[mod[bsz:bsz + 1, k] for k in range(N_MOD)],
            w_out=w_out[i].astype(BF16),
            w_router=jnp.pad(jnp.concatenate([grp_router_w[i], exp_router_w[i]], axis=1), ((0, 0), (0, pad))),
            b_router=jnp.pad(jnp.concatenate([grp_router_b[i], exp_router_b[i]]), (0, pad)).reshape(1, ROUTER_COLS),
            h_spec=hyena_filter_spectra(*hyena_filter_signals(seq, p)),
            ctx_filt=hyena_filter_signals(n_ctx, p) if i < DEPTH - 1 else None)

    prep = layer_prep(0)
    for i in range(DEPTH):
        last = i == DEPTH - 1
        p, sp, mp, mod_l, mod_c, w_out_b = prep['p'], prep['sp'], prep['mp'], prep['mod_l'], prep['mod_c'], prep['w_out']

        uc_ssd, uc_hy, uc_ml, xc = norm_proj(xc, moe_c, norm1_w[i], mod_c[0], mod_c[1], w_in_all, i)
        yc_ssd, ssd_f, ssd_b = ssd_mixer(uc_ssd, sp, ssd0, ssd0, not last)
        yc_ml, ml_f, ml_b = ml_mixer(uc_ml, mp, ml0, ml0, not last)
        col_major = i % 2 == 1
        ul_ssd, ul_hy, ul_ml, xl = norm_proj(xl, moe_l, norm1_w[i], mod_l[0], mod_l[1], w_in_all, i, col_major)
        yl_ssd, _, _ = ssd_mixer(ul_ssd, sp, ssd_f, ssd_b, True)
        yl_ml, _, _ = ml_mixer(ul_ml, mp, ml_f, ml_b, True)
        yl_hy = hyena_long(ul_hy, prep['h_spec'], p['hy_conv_w'], p['hy_conv_b'], p['hy_skip'])
        xl = out_proj(yl_ssd, yl_hy, yl_ml, xl, mod_l[2], w_out_b, col_major)
        if not last:
            yc_hy = hyena_short(uc_hy, *prep['ctx_filt'], p['hy_conv_w'], p['hy_conv_b'], p['hy_skip'])
            xc = out_proj(yc_ssd, yc_hy, yc_ml, xc, mod_c[2], w_out_b)
        h_all, routed, counts_row = norm_router(xl, None if last else xc, norm2_w[i], (mod_l[3], mod_l[4]),
                                                (mod_c[3], mod_c[4]), prep['w_router'], prep['b_router'])
        if not last:
            prep = layer_prep(i + 1)
        lat, rest = moe_apply(h_all, routed, counts_row, moe_w_gate, moe_w_up, moe_w_down, i, bsz * seq)
        n_lat = bsz * seq
        moe_l = (lat[0].reshape(bsz, seq, d), lat[1].reshape(bsz, seq, d),
                 routed[:n_lat].reshape(bsz, seq, ROUTER_COLS), mod_l[5])
        if not last:
            moe_c = (rest[0].reshape(bsz, n_ctx, d), rest[1].reshape(bsz, n_ctx, d),
                     routed[n_lat:].reshape(bsz, n_ctx, ROUTER_COLS), mod_c[5])
    return final_norm(xl, *moe_l, final_norm_w)
```

```python
import functools
import math

import jax
import jax.numpy as jnp
import numpy as np
from jax import lax
from jax.experimental import pallas as pl
from jax.experimental.pallas import tpu as pltpu

D_MODEL = 1024
DEPTH = 2
GRID_W = 64
HEAD_DIM = 64
SSD_WIDTH = 384
SSD_HEADS = SSD_WIDTH // HEAD_DIM
SSD_GROUPS = 2
SSD_STATE = 64
HYENA_WIDTH = 256
HYENA_ORDER = 2
HYENA_BANDS = 16
ML_WIDTH = 384
ML_HEADS = ML_WIDTH // HEAD_DIM
N_GROUPS = 4
EXPERTS_PER_GROUP = 8
N_EXPERTS = N_GROUPS * EXPERTS_PER_GROUP
EXPERT_HIDDEN = 256
N_MOD = 6
EPS = 1e-6

LANE = 128
SUBLANE = 8
VMEM_LIMIT = 48 * 1024 * 1024
VMEM_LIMIT_HY = 56 * 1024 * 1024

SSD_CONV_CH = SSD_WIDTH + 2 * SSD_GROUPS * SSD_STATE
SSD_XBC0 = SSD_WIDTH
SSD_DT0 = SSD_XBC0 + SSD_CONV_CH
SSD_COLS = SSD_DT0 + LANE
HY_COLS = (HYENA_ORDER + 1) * HYENA_WIDTH
ML_V0 = 2 * ML_WIDTH
ML_O0 = ML_V0 + ML_WIDTH
ML_G0 = ML_O0 + ML_WIDTH
ML_COLS = ML_G0 + LANE
ROUTER_COLS = LANE

F32 = jnp.float32
BF16 = jnp.bfloat16
HI = lax.Precision.HIGHEST


def _params(*sem):
    return pltpu.CompilerParams(dimension_semantics=sem, vmem_limit_bytes=VMEM_LIMIT)


def _host_bf16(a):
    return jnp.asarray(np.asarray(a, np.float32).astype(BF16))


def _silu(x):
    return x * jax.nn.sigmoid(x)


def _softplus(x):
    return jnp.maximum(x, 0.0) + jnp.log(1.0 + jnp.exp(-jnp.abs(x)))


def _log_sigmoid(x):
    return jnp.minimum(x, 0.0) - jnp.log(1.0 + jnp.exp(-jnp.abs(x)))


def _mod_kernel(c_ref, w_ref, b_ref, o_ref):
    o_ref[...] = jnp.dot(_silu(c_ref[...]), w_ref[...], preferred_element_type=F32, precision=HI) + b_ref[...]


def modulation(c_rows, w_mod, b_mod, layer):
    depth, d, n = w_mod.shape
    tn = 1536
    return pl.pallas_call(
        _mod_kernel,
        out_shape=jax.ShapeDtypeStruct((c_rows.shape[0], n), F32),
        grid=(n // tn,),
        in_specs=[pl.BlockSpec(c_rows.shape, lambda j: (0, 0)),
                  pl.BlockSpec((None, d, tn), lambda j: (layer, 0, j)),
                  pl.BlockSpec((None, 1, tn), lambda j: (layer, 0, j))],
        out_specs=pl.BlockSpec((c_rows.shape[0], tn), lambda j: (0, j)),
        compiler_params=_params("arbitrary"),
        name="adaln_modulation",
    )(c_rows, w_mod, b_mod.reshape(depth, 1, n))


def _modnorm(x, nw, shift, scale):
    y = x * lax.rsqrt(jnp.mean(x * x, axis=-1, keepdims=True) + EPS) * nw
    return y * (1.0 + scale) + shift


def _mod_map(mod, b):
    return (lambda bi, i: (bi, 0, 0)) if mod.shape[0] == b else (lambda bi, i: (0, 0, 0))


def _tok_view(x, col_major):
    b, L, d = x.shape
    return x.reshape(b, L // GRID_W, GRID_W, d) if col_major else x


def _tok_spec(L, d, tm, col_major):
    if col_major:
        assert tm == (L // GRID_W) * SUBLANE
        return pl.BlockSpec((1, L // GRID_W, SUBLANE, d), lambda bi, i: (bi, 0, i, 0))
    return pl.BlockSpec((1, tm, d), lambda bi, i: (bi, i, 0))


def _tok_load(ref, col_major):
    if not col_major:
        return ref[0]
    return jnp.concatenate([ref[0, :, j, :] for j in range(ref.shape[2])], axis=0)


def _tok_store(ref, val, col_major):
    if not col_major:
        ref[0] = val
        return
    rows = ref.shape[1]
    for j in range(ref.shape[2]):
        ref[0, :, j, :] = val[j * rows:(j + 1) * rows]


def _moe_mix(ya, yb, routed):
    return routed[:, 2:3] * ya + routed[:, 3:4] * yb


def _norm_proj_kernel(*refs, col_major, fuse_moe):
    if fuse_moe:
        x_ref, ya_ref, yb_ref, rt_ref, g_ref, nw_ref, sh_ref, sc_ref, w_ref, ssd_ref, hy_ref, ml_ref, xo_ref = refs
    else:
        x_ref, nw_ref, sh_ref, sc_ref, w_ref, ssd_ref, hy_ref, ml_ref = refs
    x = _tok_load(x_ref, col_major)
    if fuse_moe:
        x = x + g_ref[0] * _moe_mix(_tok_load(ya_ref, col_major), _tok_load(yb_ref, col_major),
                                    _tok_load(rt_ref, col_major))
        _tok_store(xo_ref, x, col_major)
    h = _modnorm(x, nw_ref[...], sh_ref[0], sc_ref[0])
    u = jnp.dot(h.astype(BF16), w_ref[...], preferred_element_type=F32)
    ssd_ref[0] = u[:, 0:SSD_COLS]
    hy_ref[0] = u[:, SSD_COLS:SSD_COLS + HY_COLS]
    ml_ref[0] = u[:, SSD_COLS + HY_COLS:]


def norm_proj(x, moe, nw, shift, scale, w_all, layer, col_major=False, tm=512):
    b, L, d = x.shape
    n = w_all.shape[2]
    fuse_moe = moe is not None
    tm = (L // GRID_W) * SUBLANE if col_major else min(tm, L)
    tok = _tok_spec(L, d, tm, col_major)
    row = lambda bi, i: (bi, i, 0)
    const2 = lambda bi, i: (0, 0)
    args, in_specs = [_tok_view(x, col_major)], [tok]
    if fuse_moe:
        ya, yb, routed, gate = moe
        args += [_tok_view(ya, col_major), _tok_view(yb, col_major), _tok_view(routed, col_major), gate]
        in_specs += [tok, tok, _tok_spec(L, ROUTER_COLS, tm, col_major), pl.BlockSpec((1, 1, d), _mod_map(gate, b))]
    args += [nw.reshape(1, d), shift, scale, w_all]
    in_specs += [pl.BlockSpec((1, d), const2), pl.BlockSpec((1, 1, d), _mod_map(shift, b)),
                 pl.BlockSpec((1, 1, d), _mod_map(scale, b)),
                 pl.BlockSpec((None, d, n), lambda bi, i: (layer, 0, 0))]
    out_shape = [jax.ShapeDtypeStruct((b, L, SSD_COLS), F32), jax.ShapeDtypeStruct((b, L, HY_COLS), F32),
                 jax.ShapeDtypeStruct((b, L, ML_COLS), F32)]
    out_specs = [pl.BlockSpec((1, tm, SSD_COLS), row), pl.BlockSpec((1, tm, HY_COLS), row),
                 pl.BlockSpec((1, tm, ML_COLS), row)]
    if fuse_moe:
        out_shape.append(jax.ShapeDtypeStruct(args[0].shape, F32))
        out_specs.append(tok)
    outs = pl.pallas_call(
        functools.partial(_norm_proj_kernel, col_major=col_major, fuse_moe=fuse_moe),
        out_shape=tuple(out_shape),
        grid=(b, L // tm),
        in_specs=in_specs,
        out_specs=tuple(out_specs),
        compiler_params=_params("parallel", "arbitrary"),
        name="norm_in_proj",
    )(*args)
    return (*outs[:3], outs[3].reshape(b, L, d) if fuse_moe else x)


def _out_proj_kernel(ys_ref, yh_ref, ym_ref, x_ref, g_ref, w_ref, o_ref, *, col_major):
    y = jnp.concatenate([ys_ref[0], yh_ref[0], ym_ref[0]], axis=-1).astype(BF16)
    r = _tok_load(x_ref, col_major) + g_ref[0] * jnp.dot(y, w_ref[...], preferred_element_type=F32)
    _tok_store(o_ref, r, col_major)


def out_proj(y_ssd, y_hy, y_ml, x, gate, w_bf16, col_major=False, scan_off=0, tm=512):
    b, L, d = x.shape
    tm = (L // GRID_W) * SUBLANE if col_major else min(tm, L)
    row = lambda bi, i: (bi, i, 0)
    scan_row = lambda bi, i: (bi, i + scan_off, 0)
    tok = _tok_spec(L, d, tm, col_major)
    xv = _tok_view(x, col_major)
    return pl.pallas_call(
        functools.partial(_out_proj_kernel, col_major=col_major),
        out_shape=jax.ShapeDtypeStruct(xv.shape, F32),
        grid=(b, L // tm),
        in_specs=[pl.BlockSpec((1, tm, SSD_WIDTH), scan_row), pl.BlockSpec((1, tm, HYENA_WIDTH), row),
                  pl.BlockSpec((1, tm, ML_WIDTH), scan_row), tok,
                  pl.BlockSpec((1, 1, d), _mod_map(gate, b)),
                  pl.BlockSpec(w_bf16.shape, lambda bi, i: (0, 0))],
        out_specs=tok,
        compiler_params=_params("parallel", "arbitrary"),
        name="out_proj_residual",
    )(y_ssd, y_hy, y_ml, xv, gate, w_bf16).reshape(b, L, d)


def _conv3(xr, prev_row, next_row, cw, cb, q):
    rid = lax.broadcasted_iota(jnp.int32, (q, 1), 0)
    x_prev = jnp.where(rid == 0, prev_row, pltpu.roll(xr, 1, axis=0))
    x_next = jnp.where(rid == q - 1, next_row, pltpu.roll(xr, q - 1, axis=0))
    return x_prev * cw[0:1] + xr * cw[1:2] + x_next * cw[2:3] + cb


def _masked_scan(mask, x):
    m = mask.astype(BF16)
    hi = x.astype(BF16)
    r1 = x - hi.astype(F32)
    mid = r1.astype(BF16)
    lo = (r1 - mid.astype(F32)).astype(BF16)
    return (jnp.dot(m, hi, preferred_element_type=F32) + jnp.dot(m, mid, preferred_element_type=F32)
            + jnp.dot(m, lo, preferred_element_type=F32))


def _scan_mask(q, direction):
    li = lax.broadcasted_iota(jnp.int32, (q, q), 0)
    si = lax.broadcasted_iota(jnp.int32, (q, q), 1)
    return (si <= li) if direction == 0 else (si >= li)


def _running_max(x, direction, q):
    rid = lax.broadcasted_iota(jnp.int32, (q, 1), 0)
    s = 1
    while s < q:
        if direction == 0:
            x = jnp.where(rid >= s, jnp.maximum(x, pltpu.roll(x, s, axis=0)), x)
        else:
            x = jnp.where(rid < q - s, jnp.maximum(x, pltpu.roll(x, q - s, axis=0)), x)
        s *= 2
    return x


def _scan_specs(L, q, nc, cols, direction):
    hb = q // SUBLANE
    nrb = L // SUBLANE
    order = (lambda j: j) if direction == 0 else (lambda j: nc - 1 - j)
    cidx = lambda j: order(jnp.maximum(j - 1, 0))
    specs = [pl.BlockSpec((1, q, cols), lambda bi, j: (bi, cidx(j), 0)),
             pl.BlockSpec((1, SUBLANE, cols), lambda bi, j: (bi, jnp.maximum(cidx(j) * hb - 1, 0), 0)),
             pl.BlockSpec((1, SUBLANE, cols), lambda bi, j: (bi, jnp.minimum((cidx(j) + 1) * hb, nrb - 1), 0)),
             pl.BlockSpec((1, q, cols), lambda bi, j: (bi, 0, 0))]
    yidx = lambda j: jnp.where(j == 0, nc, cidx(j))
    return specs, yidx


def _scan_inputs(u_ref, prev_ref, next_ref, uc_ref, lo, hi, direction, nc):
    j = pl.program_id(1)
    jm = jnp.maximum(j - 1, 0)
    c = jm if direction == 0 else nc - 1 - jm
    is_ctx = j == 0
    x = jnp.where(is_ctx, uc_ref[0, :, lo:hi], u_ref[0, :, lo:hi])
    prev_row = jnp.where(is_ctx | (c == 0), 0.0, prev_ref[0, SUBLANE - 1:SUBLANE, lo:hi])
    next_row = jnp.where(is_ctx | (c == nc - 1), 0.0, next_ref[0, 0:1, lo:hi])
    return x, prev_row, next_row


def _ssd_kernel(*refs, direction, finalize, q, nc):
    if finalize:
        u_ref, prev_ref, next_ref, uc_ref, yb_ref, cw_ref, cb_ref, dtb_ref, a_ref, d_ref, nw_ref, y_ref, state_ref = refs
    else:
        u_ref, prev_ref, next_ref, uc_ref, cw_ref, cb_ref, dtb_ref, a_ref, y_ref, state_ref = refs

    @pl.when(pl.program_id(1) == 0)
    def _():
        state_ref[...] = jnp.zeros_like(state_ref)

    xr, prev_row, next_row = _scan_inputs(u_ref, prev_ref, next_ref, uc_ref, SSD_XBC0, SSD_DT0, direction, nc)
    xc = _silu(_conv3(xr, prev_row, next_row, cw_ref[...], cb_ref[...], q))
    u_rest, _, _ = _scan_inputs(u_ref, prev_ref, next_ref, uc_ref, SSD_DT0, SSD_COLS, direction, nc)

    dt = _softplus(u_rest + dtb_ref[...])
    mask = _scan_mask(q, direction)
    cum = _masked_scan(mask, dt * a_ref[...])
    cum_t = cum.T
    end = q - 1 if direction == 0 else 0

    ys = []
    for g in range(SSD_GROUPS):
        b0 = SSD_WIDTH + g * SSD_STATE
        c0 = SSD_WIDTH + (SSD_GROUPS + g) * SSD_STATE
        bm_t = xc[:, b0:b0 + SSD_STATE].T
        cm = xc[:, c0:c0 + SSD_STATE].astype(BF16)
        scores = jnp.dot(cm, bm_t.astype(BF16), preferred_element_type=F32)
        for h in range(g * (SSD_HEADS // SSD_GROUPS), (g + 1) * (SSD_HEADS // SSD_GROUPS)):
            hl = direction * SSD_HEADS + h
            col = cum[:, hl:hl + 1]
            row = cum_t[hl:hl + 1, :]
            seg = jnp.exp(jnp.where(mask, col - row, -jnp.inf))
            xdt = (xc[:, h * HEAD_DIM:(h + 1) * HEAD_DIM] * dt[:, hl:hl + 1]).astype(BF16)
            y = jnp.dot((scores * seg).astype(BF16), xdt, preferred_element_type=F32)
            st = state_ref[h]
            y = y + jnp.dot(cm, st.astype(BF16), preferred_element_type=F32) * jnp.exp(col)
            tot = cum[end:end + 1, hl:hl + 1]
            upd = jnp.dot((bm_t * jnp.exp(tot - row)).astype(BF16), xdt, preferred_element_type=F32)
            state_ref[h] = st * jnp.exp(tot) + upd
            ys.append(y)
    y_all = jnp.concatenate(ys, axis=-1)
    if finalize:
        z, _, _ = _scan_inputs(u_ref, prev_ref, next_ref, uc_ref, 0, SSD_WIDTH, direction, nc)
        t = (y_all + yb_ref[0] + xc[:, 0:SSD_WIDTH] * d_ref[...]) * _silu(z)
        y_all = t * lax.rsqrt(jnp.mean(t * t, axis=-1, keepdims=True) + EPS) * nw_ref[...]
    y_ref[0] = y_all


def ssd_pass(u, u_ctx, y_other, sp, direction, q):
    b, L, _ = u.shape
    nc = L // q
    finalize = y_other is not None
    in_specs, yidx = _scan_specs(L, q, nc, SSD_COLS, direction)
    const2 = lambda bi, j: (0, 0)
    y_spec = pl.BlockSpec((1, q, SSD_WIDTH), lambda bi, j: (bi, yidx(j), 0))
    args = [u, u, u, u_ctx]
    if finalize:
        in_specs.append(y_spec)
        args.append(y_other)
    consts = [sp['cw'], sp['cb'], sp['dtb'], sp['a']] + ([sp['d'], sp['nw']] if finalize else [])
    in_specs += [pl.BlockSpec(t.shape, const2) for t in consts]
    args += consts
    return pl.pallas_call(
        functools.partial(_ssd_kernel, direction=direction, finalize=finalize, q=q, nc=nc),
        out_shape=jax.ShapeDtypeStruct((b, L + q, SSD_WIDTH), F32),
        grid=(b, nc + 1),
        in_specs=in_specs,
        out_specs=y_spec,
        scratch_shapes=[pltpu.VMEM((SSD_HEADS, HEAD_DIM, SSD_STATE), F32)],
        compiler_params=_params("parallel", "arbitrary"),
        name="ssd_scan_%s" % ("fwd" if direction == 0 else "bwd"),
    )(*args)


def ssd_prepare(p):
    pad = lambda v: jnp.pad(v.reshape(1, -1), ((0, 0), (0, LANE - v.size)))
    return dict(cw=p['ssd_conv_w'], cb=p['ssd_conv_b'].reshape(1, -1),
                dtb=pad(p['ssd_dt_bias']), a=pad(-jnp.exp(p['ssd_a_log'])),
                d=jnp.repeat(p['ssd_d'], HEAD_DIM).reshape(1, -1), nw=p['ssd_norm_w'].reshape(1, -1))


def ssd_mixer(u, u_ctx, sp, q=256):
    assert u_ctx.shape[1] == q
    return ssd_pass(u, u_ctx, ssd_pass(u, u_ctx, None, sp, 1, q), sp, 0, q)


def _ml_kernel(*refs, direction, finalize, q, nc):
    if finalize:
        u_ref, prev_ref, next_ref, uc_ref, hb_ref, cw_ref, cb_ref, gb_ref, nw_ref, pool_ref, y_ref, s_ref, m_ref = refs
    else:
        u_ref, prev_ref, next_ref, uc_ref, cw_ref, cb_ref, gb_ref, y_ref, s_ref, m_ref = refs

    @pl.when(pl.program_id(1) == 0)
    def _():
        s_ref[...] = jnp.zeros_like(s_ref)
        m_ref[...] = jnp.zeros_like(m_ref)

    xr, prev_row, next_row = _scan_inputs(u_ref, prev_ref, next_ref, uc_ref, 0, ML_V0, direction, nc)
    qk = _silu(_conv3(xr, prev_row, next_row, cw_ref[...], cb_ref[...], q))
    rest, _, _ = _scan_inputs(u_ref, prev_ref, next_ref, uc_ref, ML_V0, ML_COLS, direction, nc)
    v = rest[:, 0:ML_WIDTH]

    gb = rest[:, ML_G0 - ML_V0:] + gb_ref[...]
    mask = _scan_mask(q, direction)
    cum = jnp.dot(mask.astype(F32), _log_sigmoid(gb), preferred_element_type=F32, precision=HI)
    ig = pltpu.roll(gb, ML_HEADS, axis=1)
    end = q - 1 if direction == 0 else 0
    m_prev = m_ref[0:1, :]
    tot = cum[end:end + 1, :]
    w_end = tot - cum + ig
    m_loc = jnp.max(w_end, axis=0, keepdims=True)
    e_end = jnp.exp(w_end - m_loc)
    m_new = jnp.maximum(tot + m_prev, m_loc)
    a_prev = jnp.exp(tot + m_prev - m_new)
    a_loc = jnp.exp(m_loc - m_new)
    inter = cum + m_prev
    rel = ig - cum
    m_t = jnp.maximum(inter, cum + _running_max(rel, direction, q))
    col_a = cum - m_t
    a_inter = jnp.exp(inter - m_t)
    floor = jnp.exp(-m_t)
    rel_t = rel.T
    e_end_t = e_end.T
    k_t = (qk[:, ML_WIDTH:2 * ML_WIDTH] * (HEAD_DIM ** -0.5)).T
    one_col = (lax.broadcasted_iota(jnp.int32, (q, HEAD_DIM), 1) == 0).astype(F32)

    ys = []
    for h in range(ML_HEADS):
        fl = direction * 2 * ML_HEADS + ML_HEADS + h
        qh = qk[:, h * HEAD_DIM:(h + 1) * HEAD_DIM].astype(BF16)
        kh_t = k_t[h * HEAD_DIM:(h + 1) * HEAD_DIM, :]
        v_ext = jnp.concatenate([v[:, h * HEAD_DIM:(h + 1) * HEAD_DIM], one_col], axis=-1).astype(BF16)
        pw = jnp.exp(jnp.where(mask, col_a[:, fl:fl + 1] + rel_t[fl:fl + 1, :], -jnp.inf))
        scores = jnp.dot(qh, kh_t.astype(BF16), preferred_element_type=F32)
        nd = jnp.dot((scores * pw).astype(BF16), v_ext, preferred_element_type=F32)
        st = s_ref[h]
        nd = nd + a_inter[:, fl:fl + 1] * jnp.dot(qh, st.astype(BF16), preferred_element_type=F32)
        den = nd[:, HEAD_DIM:HEAD_DIM + 1]
        ys.append(nd[:, 0:HEAD_DIM] / jnp.maximum(jnp.abs(den), floor[:, fl:fl + 1]))
        upd = jnp.dot((kh_t * e_end_t[fl:fl + 1, :]).astype(BF16), v_ext, preferred_element_type=F32)
        s_ref[h] = a_prev[:, fl:fl + 1] * st + a_loc[:, fl:fl + 1] * upd
    m_ref[...] = jnp.broadcast_to(m_new, m_ref.shape)
    y_all = jnp.concatenate(ys, axis=-1)
    if finalize:
        hs = y_all + hb_ref[0]
        hc = hs - jnp.dot(hs.astype(BF16), pool_ref[...], preferred_element_type=F32)
        var = jnp.dot((hc * hc).astype(BF16), pool_ref[...], preferred_element_type=F32)
        y_all = hc * lax.rsqrt(var + EPS) * nw_ref[...] * jax.nn.sigmoid(rest[:, ML_O0 - ML_V0:ML_G0 - ML_V0])
    y_ref[0] = y_all


def ml_pass(u, u_ctx, h_other, mp, direction, q):
    b, L, _ = u.shape
    nc = L // q
    finalize = h_other is not None
    in_specs, yidx = _scan_specs(L, q, nc, ML_COLS, direction)
    const2 = lambda bi, j: (0, 0)
    y_spec = pl.BlockSpec((1, q, ML_WIDTH), lambda bi, j: (bi, yidx(j), 0))
    args = [u, u, u, u_ctx]
    if finalize:
        in_specs.append(y_spec)
        args.append(h_other)
    consts = [mp['cw'], mp['cb'], mp['gb']] + ([mp['nw'], mp['pool']] if finalize else [])
    in_specs += [pl.BlockSpec(t.shape, const2) for t in consts]
    args += consts
    return pl.pallas_call(
        functools.partial(_ml_kernel, direction=direction, finalize=finalize, q=q, nc=nc),
        out_shape=jax.ShapeDtypeStruct((b, L + q, ML_WIDTH), F32),
        grid=(b, nc + 1),
        in_specs=in_specs,
        out_specs=y_spec,
        scratch_shapes=[pltpu.VMEM((ML_HEADS, HEAD_DIM, LANE), F32), pltpu.VMEM((SUBLANE, LANE), F32)],
        compiler_params=_params("parallel", "arbitrary"),
        name="mlstm_scan_%s" % ("fwd" if direction == 0 else "bwd"),
    )(*args)


def ml_prepare(p):
    gb = p['ml_gate_b'].reshape(1, -1)
    head = np.arange(ML_WIDTH) // HEAD_DIM
    pool = _host_bf16((head[:, None] == head[None, :]) / HEAD_DIM)
    return dict(cw=p['ml_conv_w'], cb=p['ml_conv_b'].reshape(1, -1),
                gb=jnp.pad(gb, ((0, 0), (0, LANE - gb.shape[1]))), nw=p['ml_norm_w'].reshape(1, -1), pool=pool)


def ml_mixer(u, u_ctx, mp, q=256):
    assert u_ctx.shape[1] == q
    return ml_pass(u, u_ctx, ml_pass(u, u_ctx, None, mp, 1, q), mp, 0, q)


FFT_L = 4096
FFT_N = 2 * FFT_L
FFT_N2 = 128
FFT_N1 = FFT_N // FFT_N2
FFT_N1H = FFT_L // FFT_N2
FFT_K1 = FFT_N1 // 2 + 1
FFT_R = 80
FFT_UNROLL = 16
FFT_K1_UNROLL = 11
FFT_PITCH = FFT_N2 + SUBLANE


def _fft_tables():
    n2 = np.arange(FFT_N2)[:, None, None]
    k1 = np.arange(FFT_K1)[None, :, None]
    n1 = np.arange(FFT_N1H)[None, None, :]
    th = 2 * np.pi * (((FFT_N2 * n1 + n2) * k1) % FFT_N) / FFT_N
    f1 = np.zeros((FFT_N2, FFT_R, FFT_N1H))
    f1[:, 0:2 * FFT_K1:2, :] = np.cos(th)
    f1[:, 1:2 * FFT_K1:2, :] = -np.sin(th)
    wgt = np.where((np.arange(FFT_K1) == 0) | (np.arange(FFT_K1) == FFT_N1 // 2), 1.0, 2.0)[None, :, None] / FFT_N
    g1 = np.zeros((FFT_N2, FFT_N1H, FFT_R))
    g1[:, :, 0:2 * FFT_K1:2] = np.transpose(wgt * np.cos(th), (0, 2, 1))
    g1[:, :, 1:2 * FFT_K1:2] = np.transpose(-wgt * np.sin(th), (0, 2, 1))
    ph = 2 * np.pi * ((np.arange(FFT_N2)[:, None] * np.arange(FFT_N2)[None, :]) % FFT_N2) / FFT_N2
    c, s = np.cos(ph), np.sin(ph)
    f2 = np.block([[c, s], [-s, c]])
    sign = np.where(np.arange(FFT_R) // 2 % 2 == 0, 1.0, -1.0)[None, :, None]
    f1ab = np.concatenate([f1, f1 * sign], axis=2)
    return _host_bf16(f1), _host_bf16(f1ab), _host_bf16(f2), _host_bf16(f2.T), _host_bf16(g1)


def _sld(ref, n2, count):
    rows = pl.ds(n2, count, stride=FFT_PITCH)
    return jnp.concatenate([ref[0, rows, :], ref[1, rows, :]], axis=-1)


def _sst(ref, n2, count, val):
    rows = pl.ds(n2, count, stride=FFT_PITCH)
    ref[0, rows, :] = val[:, 0:LANE]
    ref[1, rows, :] = val[:, LANE:2 * LANE]


def _blk_ld(ref, blk, nblk):
    parts = []
    for k in range(nblk):
        rows = pl.ds(pl.multiple_of((blk + k) * FFT_PITCH, SUBLANE), FFT_N2)
        parts.append(jnp.concatenate([ref[0, rows, :], ref[1, rows, :]], axis=-1))
    return parts[0] if nblk == 1 else jnp.concatenate(parts, axis=0)


def _blk_st(ref, blk, nblk, val):
    for k in range(nblk):
        rows = pl.ds(pl.multiple_of((blk + k) * FFT_PITCH, SUBLANE), FFT_N2)
        ref[0, rows, :] = val[k * FFT_N2:(k + 1) * FFT_N2, 0:LANE]
        ref[1, rows, :] = val[k * FFT_N2:(k + 1) * FFT_N2, LANE:2 * LANE]


def _fft_stage1(z_ref, a_ref, f1_ref, n_in=FFT_N1H):
    def body(n2, carry):
        xs = _sld(z_ref, n2, n_in).astype(BF16)
        _sst(a_ref, n2, FFT_R, jnp.dot(f1_ref[n2], xs, preferred_element_type=F32))
        return carry
    lax.fori_loop(0, FFT_N2, body, 0, unroll=FFT_UNROLL)


def _spectrum_kernel(x_ref, inorm_ref, f1_ref, f2_ref, o_ref, z_ref, a_ref):
    for half in range(2):
        for n1 in range(FFT_N1H):
            _blk_st(z_ref, half * FFT_N1H + n1, 1, x_ref[half, n1 * FFT_N2:(n1 + 1) * FFT_N2, :])
    _fft_stage1(z_ref, a_ref, f1_ref, 2 * FFT_N1H)
    inorm = inorm_ref[0]

    def body(k1, carry):
        slab = _blk_ld(a_ref, 2 * k1, 2).astype(BF16)
        o_ref[0, k1] = (jnp.dot(f2_ref[...], slab, preferred_element_type=F32) * inorm).astype(BF16)
        return carry
    lax.fori_loop(0, FFT_K1, body, 0, unroll=FFT_K1_UNROLL)


def hyena_filter_spectra(sig, inv_norm):
    _, L, c = sig.shape
    _, f1ab, f2, _, _ = _fft_tables()
    one = pl.Buffered(1)
    return pl.pallas_call(
        _spectrum_kernel,
        out_shape=jax.ShapeDtypeStruct((HYENA_ORDER, FFT_K1, 2 * FFT_N2, c), BF16),
        grid=(HYENA_ORDER,),
        in_specs=[pl.BlockSpec((2, L, c), lambda i: (i, 0, 0)),
                  pl.BlockSpec((1, 1, c), lambda i: (i, 0, 0)),
                  pl.BlockSpec(f1ab.shape, lambda i: (0, 0, 0), pipeline_mode=one),
                  pl.BlockSpec(f2.shape, lambda i: (0, 0), pipeline_mode=one)],
        out_specs=pl.BlockSpec((1, FFT_K1, 2 * FFT_N2, c), lambda i: (i, 0, 0, 0)),
        scratch_shapes=[pltpu.VMEM((2, 2 * FFT_N1H * FFT_PITCH, LANE), F32),
                        pltpu.VMEM((2, FFT_R * FFT_PITCH, LANE), F32)],
        compiler_params=pltpu.CompilerParams(dimension_semantics=("arbitrary",), vmem_limit_bytes=VMEM_LIMIT_HY),
        name="hyena_filter_spectrum",
    )(sig, inv_norm, f1ab, f2)


def _conv3_rows(src, dst, cw, cb, L):
    rows = 2 * FFT_N2
    nchunk = L // rows

    def body(i, carry):
        r0 = pl.multiple_of(i * rows, rows)
        prev_row = jnp.where(i > 0, src[pl.ds(jnp.maximum(r0 - 1, 0), 1), :], 0.0)
        next_row = jnp.where(i < nchunk - 1, src[pl.ds(jnp.minimum(r0 + rows, L - 1), 1), :], 0.0)
        _blk_st(dst, 2 * i, 2, _conv3(src[pl.ds(r0, rows), :], prev_row, next_row, cw, cb, rows))
        return carry
    lax.fori_loop(0, nchunk, body, 0, unroll=2)


def _hyena_kernel(v_ref, g_ref, h_ref, cwv_ref, cbv_ref, cwg_ref, cbg_ref, skip_ref,
                  f1_ref, f2_ref, f2t_ref, g1_ref, o_ref, z_ref, gc_ref, a_ref):
    order = pl.program_id(1)

    @pl.when(order == 0)
    def _():
        _conv3_rows(v_ref.at[0], z_ref, cwv_ref[...], cbv_ref[...], FFT_L)

    _conv3_rows(g_ref.at[0], gc_ref, cwg_ref[0], cbg_ref[0], FFT_L)
    _fft_stage1(z_ref, a_ref, f1_ref)

    def mid(k1, carry):
        x = jnp.dot(f2_ref[...], _blk_ld(a_ref, 2 * k1, 2).astype(BF16), preferred_element_type=F32)
        h = h_ref[0, k1].astype(F32)
        xr, xi, hr, hi = x[:FFT_N2], x[FFT_N2:], h[:FFT_N2], h[FFT_N2:]
        y = jnp.concatenate([xr * hr - xi * hi, xr * hi + xi * hr], axis=0).astype(BF16)
        _blk_st(a_ref, 2 * k1, 2, jnp.dot(f2t_ref[...], y, preferred_element_type=F32))
        return carry
    lax.fori_loop(0, FFT_K1, mid, 0, unroll=FFT_K1_UNROLL)

    skip = skip_ref[0]

    def last(n2, carry):
        bs = _sld(a_ref, n2, FFT_R).astype(BF16)
        y = jnp.dot(g1_ref[n2], bs, preferred_element_type=F32)
        _sst(z_ref, n2, FFT_N1H, _sld(gc_ref, n2, FFT_N1H) * (y + skip * _sld(z_ref, n2, FFT_N1H)))
        return carry
    lax.fori_loop(0, FFT_N2, last, 0, unroll=FFT_UNROLL)

    @pl.when(order == 1)
    def _():
        for n1 in range(FFT_N1H):
            o_ref[0, n1 * FFT_N2:(n1 + 1) * FFT_N2, :] = _blk_ld(z_ref, n1, 1)


def hyena_long(u, h_spec, cw, cb, skip):
    b, L, _ = u.shape
    c = HYENA_WIDTH
    f1, _, f2, f2t, g1 = _fft_tables()
    one = pl.Buffered(1)
    cw3 = cw.reshape(3, 3, c).transpose(1, 0, 2)
    cb3 = cb.reshape(3, 1, c)
    return pl.pallas_call(
        _hyena_kernel,
        out_shape=jax.ShapeDtypeStruct((b, L, c), F32),
        grid=(b, 2),
        in_specs=[pl.BlockSpec((1, L, c), lambda bi, o: (bi, 0, 0), pipeline_mode=one),
                  pl.BlockSpec((1, L, c), lambda bi, o: (bi, 0, 1 + o)),
                  pl.BlockSpec((1, FFT_K1, 2 * FFT_N2, c), lambda bi, o: (o, 0, 0, 0)),
                  pl.BlockSpec((3, c), lambda bi, o: (0, 0)),
                  pl.BlockSpec((1, c), lambda bi, o: (0, 0)),
                  pl.BlockSpec((1, 3, c), lambda bi, o: (1 + o, 0, 0)),
                  pl.BlockSpec((1, 1, c), lambda bi, o: (1 + o, 0, 0)),
                  pl.BlockSpec((1, 1, c), lambda bi, o: (o, 0, 0)),
                  pl.BlockSpec(f1.shape, lambda bi, o: (0, 0, 0), pipeline_mode=one),
                  pl.BlockSpec(f2.shape, lambda bi, o: (0, 0), pipeline_mode=one),
                  pl.BlockSpec(f2t.shape, lambda bi, o: (0, 0), pipeline_mode=one),
                  pl.BlockSpec(g1.shape, lambda bi, o: (0, 0, 0), pipeline_mode=one)],
        out_specs=pl.BlockSpec((1, L, c), lambda bi, o: (bi, 0, 0)),
        scratch_shapes=[pltpu.VMEM((2, FFT_N1H * FFT_PITCH, LANE), F32),
                        pltpu.VMEM((2, FFT_N1H * FFT_PITCH, LANE), F32),
                        pltpu.VMEM((2, FFT_R * FFT_PITCH, LANE), F32)],
        compiler_params=pltpu.CompilerParams(dimension_semantics=("parallel", "arbitrary"),
                                             vmem_limit_bytes=VMEM_LIMIT_HY),
        name="hyena_long_conv",
    )(u, u, h_spec, cw3[0], cb3[0], cw3, cb3, skip.reshape(2, 1, c), f1, f2, f2t, g1)


HY_FILT = 2 * HYENA_ORDER * HYENA_WIDTH
HY_HALF = HYENA_ORDER * HYENA_WIDTH


def _filter_kernel(wt_ref, wc_ref, ws_ref, b1_ref, w2_ref, b2_ref, w3f_ref, w3b_ref, freq_ref, decay_ref,
                   sig_ref, asum_ref, *, L, rows):
    i = pl.program_id(0)
    fh = w2_ref.shape[0] // 2
    scale = 1.0 / float(max(L - 1, 1))

    @pl.when(i == 0)
    def _():
        asum_ref[...] = jnp.zeros_like(asum_ref)

    n_lane = (i * rows + lax.broadcasted_iota(jnp.int32, (1, rows), 1)).astype(F32)
    sub = lax.broadcasted_iota(jnp.int32, (2 * HYENA_BANDS, 1), 0)
    bands = 1e-4 + (sub % HYENA_BANDS).astype(F32) * ((HYENA_BANDS - 1 - 1e-4) / (HYENA_BANDS - 1))
    ang_t = (2 * math.pi / L) * jnp.where(sub < HYENA_BANDS, n_lane, L - n_lane) * bands
    cos_f, sin_f = jnp.cos(ang_t).T, jnp.sin(ang_t).T

    n = (i * rows + lax.broadcasted_iota(jnp.int32, (rows, 1), 0)).astype(F32)
    tu_f, tu_b = n * scale, (L - n) * scale
    lane = lax.broadcasted_iota(jnp.int32, (rows, 2 * fh), 1)
    freq = freq_ref[...]
    pre = (jnp.where(lane < fh, tu_f, tu_b) * wt_ref[...]
           + jnp.dot(cos_f, wc_ref[...], preferred_element_type=F32, precision=HI)
           - jnp.dot(sin_f, ws_ref[...], preferred_element_type=F32, precision=HI) + b1_ref[...])
    hdn = jnp.sin(freq * pre)
    hdn = jnp.sin(freq * (jnp.dot(hdn, w2_ref[...], preferred_element_type=F32, precision=HI) + b2_ref[...]))
    for side, (w3_ref, tu) in enumerate(((w3f_ref, tu_f), (w3b_ref, tu_b))):
        cols = slice(side * HY_HALF, (side + 1) * HY_HALF)
        val = jnp.dot(hdn, w3_ref[...], preferred_element_type=F32, precision=HI)
        val = val * jnp.exp(-tu * jnp.abs(decay_ref[:, cols]))
        if side == 1:
            val = jnp.where(n > 0, val, 0.0)
        for o in range(HYENA_ORDER):
            sig_ref[2 * o + side] = val[:, o * HYENA_WIDTH:(o + 1) * HYENA_WIDTH]
        asum_ref[side:side + 1, :] += jnp.sum(jnp.abs(val), axis=0, keepdims=True)


def _block_diag2(w):
    z = jnp.zeros_like(w)
    return jnp.concatenate([jnp.concatenate([w, z], axis=1), jnp.concatenate([z, w], axis=1)], axis=0)


def hyena_filter_signals(L, p):
    rows = min(L, 512)
    fh = p['hy_pos_w1'].shape[1]
    w1, w3 = p['hy_pos_w1'], p['hy_pos_w3']
    twice = lambda v: jnp.tile(v.reshape(1, fh), (1, 2))
    zeros = jnp.zeros((fh, HY_HALF), F32)
    consts = [twice(w1[0]), _block_diag2(w1[1:1 + HYENA_BANDS]), _block_diag2(w1[1 + HYENA_BANDS:]),
              twice(p['hy_pos_b1']), _block_diag2(p['hy_pos_w2']), twice(p['hy_pos_b2']),
              jnp.concatenate([w3[:, :HY_HALF], zeros], axis=0), jnp.concatenate([zeros, w3[:, HY_HALF:]], axis=0),
              twice(p['hy_freq']), p['hy_decay'].reshape(1, HY_FILT)]
    sig, asum = pl.pallas_call(
        functools.partial(_filter_kernel, L=L, rows=rows),
        out_shape=(jax.ShapeDtypeStruct((2 * HYENA_ORDER, L, HYENA_WIDTH), F32),
                   jax.ShapeDtypeStruct((SUBLANE, HY_HALF), F32)),
        grid=(L // rows,),
        in_specs=[pl.BlockSpec(c.shape, lambda i: (0, 0)) for c in consts],
        out_specs=(pl.BlockSpec((2 * HYENA_ORDER, rows, HYENA_WIDTH), lambda i: (0, i, 0)),
                   pl.BlockSpec((SUBLANE, HY_HALF), lambda i: (0, 0))),
        compiler_params=_params("arbitrary"),
        name="hyena_filters",
    )(*consts)
    inv_norm = 1.0 / (asum[0] + asum[1]).reshape(HYENA_ORDER, 1, HYENA_WIDTH)
    return sig, inv_norm


def _rdft_tables(L):
    n_bins = L + 1
    half = -(-n_bins // 16) * 16
    k = np.arange(n_bins)[:, None]
    n = np.arange(L)[None, :]
    th = 2 * np.pi * ((k * n) % (2 * L)) / (2 * L)
    f = np.zeros((2 * half, L))
    f[:n_bins] = np.cos(th)
    f[half:half + n_bins] = -np.sin(th)
    sign = np.where(np.arange(n_bins) % 2 == 0, 1.0, -1.0)[:, None]
    fs = np.zeros_like(f)
    fs[:n_bins] = f[:n_bins] * sign
    fs[half:half + n_bins] = f[half:half + n_bins] * sign
    wgt = np.where((np.arange(n_bins) == 0) | (np.arange(n_bins) == L), 1.0, 2.0)[None, :] / (2 * L)
    g = np.zeros((L, 2 * half))
    g[:, :n_bins] = wgt * np.cos(th).T
    g[:, half:half + n_bins] = -wgt * np.sin(th).T
    return _host_bf16(f), _host_bf16(fs), _host_bf16(g), half


def _hyena_ctx_kernel(u_ref, sig_ref, inorm_ref, cw_ref, cb_ref, skip_ref, f_ref, fs_ref, g_ref, o_ref, *, L, half):
    zero_row = jnp.zeros((1, u_ref.shape[2]), F32)
    uc = _conv3(u_ref[0], zero_row, zero_row, cw_ref[...], cb_ref[...], L)
    z = uc[:, 0:HYENA_WIDTH]
    for o in range(HYENA_ORDER):
        h = (jnp.dot(f_ref[...], sig_ref[2 * o].astype(BF16), preferred_element_type=F32)
             + jnp.dot(fs_ref[...], sig_ref[2 * o + 1].astype(BF16), preferred_element_type=F32)) * inorm_ref[o]
        x = jnp.dot(f_ref[...], z.astype(BF16), preferred_element_type=F32)
        xr, xi, hr, hi = x[:half], x[half:], h[:half], h[half:]
        y = jnp.concatenate([xr * hr - xi * hi, xr * hi + xi * hr], axis=0).astype(BF16)
        conv = jnp.dot(g_ref[...], y, preferred_element_type=F32)
        z = uc[:, (o + 1) * HYENA_WIDTH:(o + 2) * HYENA_WIDTH] * (conv + skip_ref[o] * z)
    o_ref[0] = z


def hyena_short(u, sig, inv_norm, cw, cb, skip):
    b, L, cols = u.shape
    f, fs, g, half = _rdft_tables(L)
    const2 = lambda bi: (0, 0)
    const3 = lambda bi: (0, 0, 0)
    return pl.pallas_call(
        functools.partial(_hyena_ctx_kernel, L=L, half=half),
        out_shape=jax.ShapeDtypeStruct((b, L, HYENA_WIDTH), F32),
        grid=(b,),
        in_specs=[pl.BlockSpec((1, L, cols), lambda bi: (bi, 0, 0)),
                  pl.BlockSpec(sig.shape, const3), pl.BlockSpec(inv_norm.shape, const3),
                  pl.BlockSpec(cw.shape, const2), pl.BlockSpec((1, cols), const2),
                  pl.BlockSpec((HYENA_ORDER, 1, HYENA_WIDTH), const3),
                  pl.BlockSpec(f.shape, const2), pl.BlockSpec(fs.shape, const2), pl.BlockSpec(g.shape, const2)],
        out_specs=pl.BlockSpec((1, L, HYENA_WIDTH), lambda bi: (bi, 0, 0)),
        compiler_params=_params("parallel"),
        name="hyena_context",
    )(u, sig, inv_norm, cw, cb.reshape(1, cols), skip.reshape(HYENA_ORDER, 1, HYENA_WIDTH), f, fs, g)


def _split_bf16(a):
    hi = a.astype(BF16)
    return hi, (a - hi.astype(F32)).astype(BF16)


def _norm_router_kernel(*refs, n_lat, two_src):
    if two_src:
        xl_ref, xc_ref, nw_ref, sh_ref, sc_ref, whi_ref, wlo_ref, br_ref, tri_ref, h_ref, lg_ref, cnt_ref = refs
        x = jnp.where(pl.program_id(0) < n_lat, xl_ref[...], xc_ref[...])
    else:
        xl_ref, nw_ref, sh_ref, sc_ref, whi_ref, wlo_ref, br_ref, tri_ref, h_ref, lg_ref, cnt_ref = refs
        x = xl_ref[...]
    h = _modnorm(x, nw_ref[...], sh_ref[0], sc_ref[0])
    h_ref[...] = h
    h_hi, h_lo = _split_bf16(h)
    lg = (jnp.dot(h_hi, whi_ref[...], preferred_element_type=F32)
          + jnp.dot(h_hi, wlo_ref[...], preferred_element_type=F32)
          + jnp.dot(h_lo, whi_ref[...], preferred_element_type=F32)) + br_ref[...]
    lane = lax.broadcasted_iota(jnp.int32, lg.shape, 1)
    first = lambda hit: jnp.min(jnp.where(hit, lane, ROUTER_COLS), axis=-1, keepdims=True)
    gl = jnp.where(lane < N_GROUPS, lg, -jnp.inf)
    gmax = jnp.max(gl, axis=-1, keepdims=True)
    grp = first(gl == gmax)
    grp_p = 1.0 / jnp.sum(jnp.exp(gl - gmax), axis=-1, keepdims=True)
    lo = N_GROUPS + grp * EXPERTS_PER_GROUP
    el = jnp.where((lane >= lo) & (lane < lo + EXPERTS_PER_GROUP), lg, -jnp.inf)
    e1 = jnp.max(el, axis=-1, keepdims=True)
    i1 = first(el == e1)
    el2 = jnp.where(lane == i1, -jnp.inf, el)
    e2 = jnp.max(el2, axis=-1, keepdims=True)
    i2 = first(el2 == e2)
    r = jnp.exp(e2 - e1)
    w1 = grp_p / (1.0 + r)
    w2 = w1 * r
    @pl.when(pl.program_id(0) == 0)
    def _():
        cnt_ref[...] = jnp.zeros_like(cnt_ref)

    hit1, hit2 = lane == i1, lane == i2
    picks = jnp.where(hit1 | hit2, 1.0, 0.0)
    before = cnt_ref[0:1, :] + jnp.dot(tri_ref[...], picks.astype(BF16), preferred_element_type=F32)
    rank1 = jnp.sum(jnp.where(hit1, before, 0.0), axis=-1, keepdims=True)
    rank2 = jnp.sum(jnp.where(hit2, before, 0.0), axis=-1, keepdims=True)
    cnt_ref[...] = jnp.broadcast_to(cnt_ref[0:1, :] + jnp.sum(picks, axis=0, keepdims=True), cnt_ref.shape)
    vals = [(i1 - N_GROUPS).astype(F32), (i2 - N_GROUPS).astype(F32), w1, w2, rank1, rank2]
    out = jnp.zeros(lg.shape, F32)
    for k, val in enumerate(vals):
        out = jnp.where(lane == k, val, out)
    lg_ref[...] = out


def norm_router(xl, xc, nw, mod_l, mod_c, w_router, b_router, tm=512):
    b, L, d = xl.shape
    two_src = xc is not None
    n_lat = b * L // tm
    per_batch = L // tm
    n_ctx = (xc.shape[0] * xc.shape[1]) // tm if two_src else 0
    w_hi, w_lo = _split_bf16(w_router)
    const = lambda i: (0, 0)
    if two_src:
        shift = jnp.concatenate([mod_l[0], mod_c[0]], axis=0)
        scale = jnp.concatenate([mod_l[1], mod_c[1]], axis=0)
        mod_map = lambda i: (jnp.where(i < n_lat, i // per_batch, b), 0, 0)
        srcs = [xl.reshape(b * L, d), xc.reshape(-1, d)]
        src_specs = [pl.BlockSpec((tm, d), lambda i: (jnp.minimum(i, n_lat - 1), 0)),
                     pl.BlockSpec((tm, d), lambda i: (jnp.maximum(i - n_lat, 0), 0))]
    else:
        shift, scale = mod_l
        mod_map = lambda i: (i // per_batch, 0, 0)
        srcs = [xl.reshape(b * L, d)]
        src_specs = [pl.BlockSpec((tm, d), lambda i: (i, 0))]
    n_tok = (n_lat + n_ctx) * tm
    tri = _host_bf16(np.tril(np.ones((tm, tm)), -1))
    return pl.pallas_call(
        functools.partial(_norm_router_kernel, n_lat=n_lat, two_src=two_src),
        out_shape=(jax.ShapeDtypeStruct((n_tok, d), F32), jax.ShapeDtypeStruct((n_tok, ROUTER_COLS), F32),
                   jax.ShapeDtypeStruct((SUBLANE, ROUTER_COLS), F32)),
        grid=(n_lat + n_ctx,),
        in_specs=src_specs + [pl.BlockSpec((1, d), const), pl.BlockSpec((1, 1, d), mod_map),
                              pl.BlockSpec((1, 1, d), mod_map), pl.BlockSpec((d, ROUTER_COLS), const),
                              pl.BlockSpec((d, ROUTER_COLS), const), pl.BlockSpec((1, ROUTER_COLS), const),
                              pl.BlockSpec((tm, tm), const)],
        out_specs=(pl.BlockSpec((tm, d), lambda i: (i, 0)), pl.BlockSpec((tm, ROUTER_COLS), lambda i: (i, 0)),
                   pl.BlockSpec((SUBLANE, ROUTER_COLS), const)),
        compiler_params=_params("arbitrary"),
        name="moe_norm_router",
    )(*srcs, nw.reshape(1, d), shift, scale, w_hi, w_lo, b_router, tri)


def _expert_ffn_kernel(te_ref, tv_ref, x_ref, wg_ref, wu_ref, wd_ref, o_ref):
    i = pl.program_id(0)

    @pl.when(tv_ref[i] > 0)
    def _():
        x = x_ref[...].astype(BF16)
        g = jnp.dot(x, wg_ref[0].astype(BF16), preferred_element_type=F32)
        u = jnp.dot(x, wu_ref[0].astype(BF16), preferred_element_type=F32)
        hid = _silu(g) * u
        o_ref[...] = jnp.dot(hid.astype(BF16), wd_ref[0].astype(BF16), preferred_element_type=F32)

    @pl.when(tv_ref[i] == 0)
    def _():
        o_ref[...] = jnp.zeros_like(o_ref)


def expert_ffn(x_sorted, tile_expert, tile_valid, w_gate, w_up, w_down, tm):
    r, d = x_sorted.shape
    f = w_gate.shape[-1]
    grid_spec = pltpu.PrefetchScalarGridSpec(
        num_scalar_prefetch=2,
        grid=(r // tm,),
        in_specs=[pl.BlockSpec((tm, d), lambda i, te, tv: (i, 0)),
                  pl.BlockSpec((1, d, f), lambda i, te, tv: (te[i], 0, 0)),
                  pl.BlockSpec((1, d, f), lambda i, te, tv: (te[i], 0, 0)),
                  pl.BlockSpec((1, f, d), lambda i, te, tv: (te[i], 0, 0))],
        out_specs=pl.BlockSpec((tm, d), lambda i, te, tv: (i, 0)),
    )
    return pl.pallas_call(
        _expert_ffn_kernel,
        out_shape=jax.ShapeDtypeStruct((r, d), F32),
        grid_spec=grid_spec,
        compiler_params=_params("arbitrary"),
        name="moe_expert_ffn",
    )(tile_expert, tile_valid, x_sorted, w_gate, w_up, w_down)


def moe_apply(h_tokens, routed, counts_row, w_gate, w_up, w_down, layer, n_lat, tm=256):
    t, d = h_tokens.shape
    e_idx, rank = routed[:, 0:2].astype(jnp.int32), routed[:, 4:6].astype(jnp.int32)
    counts = counts_row[0, N_GROUPS:N_GROUPS + N_EXPERTS].astype(jnp.int32)
    n_pairs = 2 * t
    padded = (counts + tm - 1) // tm * tm
    pad_end = jnp.cumsum(padded)
    pad_start = pad_end - padded
    n_rows = n_pairs + N_EXPERTS * tm
    tile_start = jnp.arange(n_rows // tm, dtype=jnp.int32) * tm
    tile_expert = jnp.minimum(jnp.sum((pad_end[None, :] <= tile_start[:, None]).astype(jnp.int32), axis=1),
                              N_EXPERTS - 1)
    tile_valid = (tile_start < pad_end[-1]).astype(jnp.int32)
    onehot = (e_idx[:, :, None] == jnp.arange(N_EXPERTS, dtype=jnp.int32)).astype(jnp.int32)
    pos = jnp.sum(onehot * pad_start, axis=-1) + rank
    j = jnp.arange(tm, dtype=jnp.int32)[None, :]
    fill_key = jnp.where(j < (padded - counts)[:, None], (pad_start + counts)[:, None] + j, n_rows)
    keys = jnp.concatenate([pos.reshape(-1), fill_key.reshape(-1)])
    toks = jnp.concatenate([jnp.arange(n_pairs, dtype=jnp.int32) // 2, jnp.arange(N_EXPERTS * tm, dtype=jnp.int32) % t])
    _, row_token = lax.sort((keys, toks), num_keys=1)
    x_sorted = h_tokens[row_token]
    wg = w_gate.reshape(-1, d, EXPERT_HIDDEN)
    wu = w_up.reshape(-1, d, EXPERT_HIDDEN)
    wd = w_down.reshape(-1, EXPERT_HIDDEN, d)
    y_sorted = expert_ffn(x_sorted, tile_expert + layer * N_EXPERTS, tile_valid, wg, wu, wd, tm)
    lat = (y_sorted[pos[:n_lat, 0]], y_sorted[pos[:n_lat, 1]])
    rest = (y_sorted[pos[n_lat:, 0]], y_sorted[pos[n_lat:, 1]]) if t > n_lat else None
    return lat, rest


def _final_kernel(x_ref, ya_ref, yb_ref, rt_ref, g_ref, w_ref, o_ref):
    x = x_ref[0] + g_ref[0] * _moe_mix(ya_ref[0], yb_ref[0], rt_ref[0])
    o_ref[0] = x * lax.rsqrt(jnp.mean(x * x, axis=-1, keepdims=True) + EPS) * w_ref[...]


def final_norm(x, ya, yb, routed, gate, w, tm=512):
    b, L, d = x.shape
    tok = pl.BlockSpec((1, tm, d), lambda bi, i: (bi, i, 0))
    return pl.pallas_call(
        _final_kernel,
        out_shape=jax.ShapeDtypeStruct((b, L, d), F32),
        grid=(b, L // tm),
        in_specs=[tok, tok, tok, pl.BlockSpec((1, tm, ROUTER_COLS), lambda bi, i: (bi, i, 0)),
                  pl.BlockSpec((1, 1, d), _mod_map(gate, b)), pl.BlockSpec((1, d), lambda bi, i: (0, 0))],
        out_specs=tok,
        compiler_params=_params("parallel", "arbitrary"),
        name="final_rmsnorm",
    )(x, ya, yb, routed, gate, w.reshape(1, d))


IN_SIZES = (SSD_WIDTH, SSD_CONV_CH, 2 * SSD_HEADS, HY_COLS, 2 * ML_WIDTH, ML_WIDTH, ML_WIDTH, 4 * ML_HEADS)


def _regroup_kernel(w_ref, o_ref):
    src = dst = 0
    for n in IN_SIZES:
        pad = -n % LANE
        o_ref[0, :, dst:dst + n] = w_ref[0, :, src:src + n].astype(BF16)
        if pad:
            o_ref[0, :, dst + n:dst + n + pad] = jnp.zeros((o_ref.shape[1], pad), BF16)
        src, dst = src + n, dst + n + pad


def regroup_in_weights(w_in):
    depth, d, n_in = w_in.shape
    n_out = sum(n + (-n % LANE) for n in IN_SIZES)
    one = pl.Buffered(1)
    return pl.pallas_call(
        _regroup_kernel,
        out_shape=jax.ShapeDtypeStruct((depth, d, n_out), BF16),
        grid=(depth,),
        in_specs=[pl.BlockSpec((1, d, n_in), lambda i: (i, 0, 0), pipeline_mode=one)],
        out_specs=pl.BlockSpec((1, d, n_out), lambda i: (i, 0, 0)),
        compiler_params=_params("arbitrary"),
        name="regroup_in_weights",
    )(w_in)


def kernel(x, c, ctx, c_ctx, w_mod, b_mod, norm1_w, norm2_w, w_in, w_out, ssd_conv_w, ssd_conv_b, ssd_dt_bias, ssd_a_log, ssd_d, ssd_norm_w, hy_conv_w, hy_conv_b, hy_pos_w1, hy_pos_b1, hy_pos_w2, hy_pos_b2, hy_pos_w3, hy_freq, hy_decay, hy_skip, ml_conv_w, ml_conv_b, ml_gate_b, ml_norm_w, grp_router_w, grp_router_b, exp_router_w, exp_router_b, moe_w_gate, moe_w_up, moe_w_down, final_norm_w):
    layer_params = dict(
        ssd_conv_w=ssd_conv_w, ssd_conv_b=ssd_conv_b, ssd_dt_bias=ssd_dt_bias, ssd_a_log=ssd_a_log,
        ssd_d=ssd_d, ssd_norm_w=ssd_norm_w, hy_conv_w=hy_conv_w, hy_conv_b=hy_conv_b,
        hy_pos_w1=hy_pos_w1, hy_pos_b1=hy_pos_b1, hy_pos_w2=hy_pos_w2, hy_pos_b2=hy_pos_b2,
        hy_pos_w3=hy_pos_w3, hy_freq=hy_freq, hy_decay=hy_decay, hy_skip=hy_skip,
        ml_conv_w=ml_conv_w, ml_conv_b=ml_conv_b, ml_gate_b=ml_gate_b, ml_norm_w=ml_norm_w)
    bsz, seq, d = x.shape
    n_ctx = ctx.shape[1]
    xl, xc = x, ctx
    moe_l = moe_c = None
    c_rows = jnp.concatenate([c, c_ctx[None, :], jnp.zeros((SUBLANE - bsz - 1, d), F32)], axis=0)
    w_in_all = regroup_in_weights(w_in)

    def layer_prep(i):
        p = {name: arr[i] for name, arr in layer_params.items()}
        mod = modulation(c_rows, w_mod, b_mod, i).reshape(SUBLANE, N_MOD, 1, d)
        pad = ROUTER_COLS - N_GROUPS - N_EXPERTS
        return dict(
            p=p, sp=ssd_prepare(p), mp=ml_prepare(p),
            mod_l=[mod[:bsz, k] for k in range(N_MOD)],
            mod_c=[mod[bsz:bsz + 1, k] for k in range(N_MOD)],
            w_out=w_out[i].astype(BF16),
            w_router=jnp.pad(jnp.concatenate([grp_router_w[i], exp_router_w[i]], axis=1), ((0, 0), (0, pad))),
            b_router=jnp.pad(jnp.concatenate([grp_router_b[i], exp_router_b[i]]), (0, pad)).reshape(1, ROUTER_COLS),
            h_spec=hyena_filter_spectra(*hyena_filter_signals(seq, p)),
            ctx_filt=hyena_filter_signals(n_ctx, p) if i < DEPTH - 1 else None)

    prep = layer_prep(0)
    for i in range(DEPTH):
        last = i == DEPTH - 1
        p, sp, mp, mod_l, mod_c, w_out_b = prep['p'], prep['sp'], prep['mp'], prep['mod_l'], prep['mod_c'], prep['w_out']

        uc_ssd, uc_hy, uc_ml, xc = norm_proj(xc, moe_c, norm1_w[i], mod_c[0], mod_c[1], w_in_all, i)
        col_major = i % 2 == 1
        ul_ssd, ul_hy, ul_ml, xl = norm_proj(xl, moe_l, norm1_w[i], mod_l[0], mod_l[1], w_in_all, i, col_major)
        y_ssd = ssd_mixer(ul_ssd, uc_ssd, sp)
        y_ml = ml_mixer(ul_ml, uc_ml, mp)
        yl_hy = hyena_long(ul_hy, prep['h_spec'], p['hy_conv_w'], p['hy_conv_b'], p['hy_skip'])
        xl = out_proj(y_ssd, yl_hy, y_ml, xl, mod_l[2], w_out_b, col_major)
        if not last:
            yc_hy = hyena_short(uc_hy, *prep['ctx_filt'], p['hy_conv_w'], p['hy_conv_b'], p['hy_skip'])
            xc = out_proj(y_ssd, yc_hy, y_ml, xc, mod_c[2], w_out_b, scan_off=seq // n_ctx)
        h_all, routed, counts_row = norm_router(xl, None if last else xc, norm2_w[i], (mod_l[3], mod_l[4]),
                                                (mod_c[3], mod_c[4]), prep['w_router'], prep['b_router'])
        if not last:
            prep = layer_prep(i + 1)
        lat, rest = moe_apply(h_all, routed, counts_row, moe_w_gate, moe_w_up, moe_w_down, i, bsz * seq)
        n_lat = bsz * seq
        moe_l = (lat[0].reshape(bsz, seq, d), lat[1].reshape(bsz, seq, d),
                 routed[:n_lat].reshape(bsz, seq, ROUTER_COLS), mod_l[5])
        if not last:
            moe_c = (rest[0].reshape(bsz, n_ctx, d), rest[1].reshape(bsz, n_ctx, d),
                     routed[n_lat:].reshape(bsz, n_ctx, ROUTER_COLS), mod_c[5])
    return final_norm(xl, *moe_l, final_norm_w)
```

```python
import functools
import math

import jax
import jax.numpy as jnp
import numpy as np
from jax import lax
from jax.experimental import pallas as pl
from jax.experimental.pallas import tpu as pltpu

D_MODEL = 1024
DEPTH = 2
GRID_W = 64
HEAD_DIM = 64
SSD_WIDTH = 384
SSD_HEADS = SSD_WIDTH // HEAD_DIM
SSD_GROUPS = 2
SSD_STATE = 64
HYENA_WIDTH = 256
HYENA_ORDER = 2
HYENA_BANDS = 16
ML_WIDTH = 384
ML_HEADS = ML_WIDTH // HEAD_DIM
N_GROUPS = 4
EXPERTS_PER_GROUP = 8
N_EXPERTS = N_GROUPS * EXPERTS_PER_GROUP
EXPERT_HIDDEN = 256
N_MOD = 6
EPS = 1e-6

LANE = 128
SUBLANE = 8
VMEM_LIMIT = 48 * 1024 * 1024
VMEM_LIMIT_HY = 56 * 1024 * 1024

TOKEN_TILE = 512
EXPERT_TILE = 256
MOD_TILE = 1536
SCAN_CHUNK = 256

SSD_CONV_CH = SSD_WIDTH + 2 * SSD_GROUPS * SSD_STATE
SSD_XBC0 = SSD_WIDTH
SSD_DT0 = SSD_XBC0 + SSD_CONV_CH
SSD_COLS = SSD_DT0 + LANE
HY_COLS = (HYENA_ORDER + 1) * HYENA_WIDTH
ML_V0 = 2 * ML_WIDTH
ML_O0 = ML_V0 + ML_WIDTH
ML_G0 = ML_O0 + ML_WIDTH
ML_COLS = ML_G0 + LANE
ROUTER_COLS = LANE

F32 = jnp.float32
BF16 = jnp.bfloat16
HI = lax.Precision.HIGHEST


def _params(*sem):
    return pltpu.CompilerParams(dimension_semantics=sem, vmem_limit_bytes=VMEM_LIMIT)


def _host_bf16(a):
    return jnp.asarray(np.asarray(a, np.float32).astype(BF16))


def _silu(x):
    return x * jax.nn.sigmoid(x)


def _softplus(x):
    return jnp.maximum(x, 0.0) + jnp.log(1.0 + jnp.exp(-jnp.abs(x)))


def _log_sigmoid(x):
    return jnp.minimum(x, 0.0) - jnp.log(1.0 + jnp.exp(-jnp.abs(x)))


def _mod_kernel(c_ref, w_ref, b_ref, o_ref):
    o_ref[...] = jnp.dot(_silu(c_ref[...]), w_ref[...], preferred_element_type=F32, precision=HI) + b_ref[...]


def modulation(c_rows, w_mod, b_mod, layer):
    depth, d, n = w_mod.shape
    tn = MOD_TILE
    return pl.pallas_call(
        _mod_kernel,
        out_shape=jax.ShapeDtypeStruct((c_rows.shape[0], n), F32),
        grid=(n // tn,),
        in_specs=[pl.BlockSpec(c_rows.shape, lambda j: (0, 0)),
                  pl.BlockSpec((None, d, tn), lambda j: (layer, 0, j)),
                  pl.BlockSpec((None, 1, tn), lambda j: (layer, 0, j))],
        out_specs=pl.BlockSpec((c_rows.shape[0], tn), lambda j: (0, j)),
        compiler_params=_params("arbitrary"),
        name="adaln_modulation",
    )(c_rows, w_mod, b_mod.reshape(depth, 1, n))


def _modnorm(x, nw, shift, scale):
    y = x * lax.rsqrt(jnp.mean(x * x, axis=-1, keepdims=True) + EPS) * nw
    return y * (1.0 + scale) + shift


def _mod_map(mod, b):
    return (lambda bi, i: (bi, 0, 0)) if mod.shape[0] == b else (lambda bi, i: (0, 0, 0))


def _tok_view(x, col_major):
    b, L, d = x.shape
    return x.reshape(b, L // GRID_W, GRID_W, d) if col_major else x


def _tok_spec(L, d, tm, col_major):
    if col_major:
        assert tm == (L // GRID_W) * SUBLANE
        return pl.BlockSpec((1, L // GRID_W, SUBLANE, d), lambda bi, i: (bi, 0, i, 0))
    return pl.BlockSpec((1, tm, d), lambda bi, i: (bi, i, 0))


def _tok_load(ref, col_major):
    if not col_major:
        return ref[0]
    return jnp.concatenate([ref[0, :, j, :] for j in range(ref.shape[2])], axis=0)


def _tok_store(ref, val, col_major):
    if not col_major:
        ref[0] = val
        return
    rows = ref.shape[1]
    for j in range(ref.shape[2]):
        ref[0, :, j, :] = val[j * rows:(j + 1) * rows]


def _moe_mix(ya, yb, routed):
    return routed[:, 2:3] * ya + routed[:, 3:4] * yb


def _norm_proj_kernel(*refs, col_major, fuse_moe):
    if fuse_moe:
        x_ref, ya_ref, yb_ref, rt_ref, g_ref, nw_ref, sh_ref, sc_ref, w_ref, ssd_ref, hy_ref, ml_ref, xo_ref = refs
    else:
        x_ref, nw_ref, sh_ref, sc_ref, w_ref, ssd_ref, hy_ref, ml_ref = refs
    x = _tok_load(x_ref, col_major)
    if fuse_moe:
        x = x + g_ref[0] * _moe_mix(_tok_load(ya_ref, col_major), _tok_load(yb_ref, col_major),
                                    _tok_load(rt_ref, col_major))
        _tok_store(xo_ref, x, col_major)
    h = _modnorm(x, nw_ref[...], sh_ref[0], sc_ref[0])
    u = jnp.dot(h.astype(BF16), w_ref[...], preferred_element_type=F32)
    ssd_ref[0] = u[:, 0:SSD_COLS]
    hy_ref[0] = u[:, SSD_COLS:SSD_COLS + HY_COLS]
    ml_ref[0] = u[:, SSD_COLS + HY_COLS:]


def norm_proj(x, moe, nw, shift, scale, w_all, layer, col_major=False, tm=TOKEN_TILE):
    b, L, d = x.shape
    n = w_all.shape[2]
    fuse_moe = moe is not None
    tm = (L // GRID_W) * SUBLANE if col_major else min(tm, L)
    tok = _tok_spec(L, d, tm, col_major)
    row = lambda bi, i: (bi, i, 0)
    const2 = lambda bi, i: (0, 0)
    args, in_specs = [_tok_view(x, col_major)], [tok]
    if fuse_moe:
        ya, yb, routed, gate = moe
        args += [_tok_view(ya, col_major), _tok_view(yb, col_major), _tok_view(routed, col_major), gate]
        in_specs += [tok, tok, _tok_spec(L, ROUTER_COLS, tm, col_major), pl.BlockSpec((1, 1, d), _mod_map(gate, b))]
    args += [nw.reshape(1, d), shift, scale, w_all]
    in_specs += [pl.BlockSpec((1, d), const2), pl.BlockSpec((1, 1, d), _mod_map(shift, b)),
                 pl.BlockSpec((1, 1, d), _mod_map(scale, b)),
                 pl.BlockSpec((None, d, n), lambda bi, i: (layer, 0, 0))]
    out_shape = [jax.ShapeDtypeStruct((b, L, SSD_COLS), F32), jax.ShapeDtypeStruct((b, L, HY_COLS), F32),
                 jax.ShapeDtypeStruct((b, L, ML_COLS), F32)]
    out_specs = [pl.BlockSpec((1, tm, SSD_COLS), row), pl.BlockSpec((1, tm, HY_COLS), row),
                 pl.BlockSpec((1, tm, ML_COLS), row)]
    if fuse_moe:
        out_shape.append(jax.ShapeDtypeStruct(args[0].shape, F32))
        out_specs.append(tok)
    outs = pl.pallas_call(
        functools.partial(_norm_proj_kernel, col_major=col_major, fuse_moe=fuse_moe),
        out_shape=tuple(out_shape),
        grid=(b, L // tm),
        in_specs=in_specs,
        out_specs=tuple(out_specs),
        compiler_params=_params("parallel", "arbitrary"),
        name="norm_in_proj",
    )(*args)
    return (*outs[:3], outs[3].reshape(b, L, d) if fuse_moe else x)


def _out_proj_kernel(ys_ref, yh_ref, ym_ref, x_ref, g_ref, w_ref, o_ref, *, col_major):
    y = jnp.concatenate([ys_ref[0], yh_ref[0], ym_ref[0]], axis=-1).astype(BF16)
    r = _tok_load(x_ref, col_major) + g_ref[0] * jnp.dot(y, w_ref[...], preferred_element_type=F32)
    _tok_store(o_ref, r, col_major)


def out_proj(y_ssd, y_hy, y_ml, x, gate, w_bf16, col_major=False, scan_off=0, tm=TOKEN_TILE):
    b, L, d = x.shape
    tm = (L // GRID_W) * SUBLANE if col_major else min(tm, L)
    row = lambda bi, i: (bi, i, 0)
    scan_row = lambda bi, i: (bi, i + scan_off, 0)
    tok = _tok_spec(L, d, tm, col_major)
    xv = _tok_view(x, col_major)
    return pl.pallas_call(
        functools.partial(_out_proj_kernel, col_major=col_major),
        out_shape=jax.ShapeDtypeStruct(xv.shape, F32),
        grid=(b, L // tm),
        in_specs=[pl.BlockSpec((1, tm, SSD_WIDTH), scan_row), pl.BlockSpec((1, tm, HYENA_WIDTH), row),
                  pl.BlockSpec((1, tm, ML_WIDTH), scan_row), tok,
                  pl.BlockSpec((1, 1, d), _mod_map(gate, b)),
                  pl.BlockSpec(w_bf16.shape, lambda bi, i: (0, 0))],
        out_specs=tok,
        compiler_params=_params("parallel", "arbitrary"),
        name="out_proj_residual",
    )(y_ssd, y_hy, y_ml, xv, gate, w_bf16).reshape(b, L, d)


def _conv3(xr, prev_row, next_row, cw, cb, q):
    rid = lax.broadcasted_iota(jnp.int32, (q, 1), 0)
    x_prev = jnp.where(rid == 0, prev_row, pltpu.roll(xr, 1, axis=0))
    x_next = jnp.where(rid == q - 1, next_row, pltpu.roll(xr, q - 1, axis=0))
    return x_prev * cw[0:1] + xr * cw[1:2] + x_next * cw[2:3] + cb


def _masked_scan(mask, x):
    m = mask.astype(BF16)
    hi = x.astype(BF16)
    r1 = x - hi.astype(F32)
    mid = r1.astype(BF16)
    lo = (r1 - mid.astype(F32)).astype(BF16)
    return (jnp.dot(m, hi, preferred_element_type=F32) + jnp.dot(m, mid, preferred_element_type=F32)
            + jnp.dot(m, lo, preferred_element_type=F32))


def _scan_mask(q, direction):
    li = lax.broadcasted_iota(jnp.int32, (q, q), 0)
    si = lax.broadcasted_iota(jnp.int32, (q, q), 1)
    return (si <= li) if direction == 0 else (si >= li)


def _running_max(x, direction, q):
    rid = lax.broadcasted_iota(jnp.int32, (q, 1), 0)
    s = 1
    while s < q:
        if direction == 0:
            x = jnp.where(rid >= s, jnp.maximum(x, pltpu.roll(x, s, axis=0)), x)
        else:
            x = jnp.where(rid < q - s, jnp.maximum(x, pltpu.roll(x, q - s, axis=0)), x)
        s *= 2
    return x


def _scan_specs(L, q, nc, cols, direction):
    hb = q // SUBLANE
    nrb = L // SUBLANE
    order = (lambda j: j) if direction == 0 else (lambda j: nc - 1 - j)
    cidx = lambda j: order(jnp.maximum(j - 1, 0))
    specs = [pl.BlockSpec((1, q, cols), lambda bi, j: (bi, cidx(j), 0)),
             pl.BlockSpec((1, SUBLANE, cols), lambda bi, j: (bi, jnp.maximum(cidx(j) * hb - 1, 0), 0)),
             pl.BlockSpec((1, SUBLANE, cols), lambda bi, j: (bi, jnp.minimum((cidx(j) + 1) * hb, nrb - 1), 0)),
             pl.BlockSpec((1, q, cols), lambda bi, j: (bi, 0, 0))]
    yidx = lambda j: jnp.where(j == 0, nc, cidx(j))
    return specs, yidx


def _scan_inputs(u_ref, prev_ref, next_ref, uc_ref, lo, hi, direction, nc):
    j = pl.program_id(1)
    jm = jnp.maximum(j - 1, 0)
    c = jm if direction == 0 else nc - 1 - jm
    is_ctx = j == 0
    x = jnp.where(is_ctx, uc_ref[0, :, lo:hi], u_ref[0, :, lo:hi])
    prev_row = jnp.where(is_ctx | (c == 0), 0.0, prev_ref[0, SUBLANE - 1:SUBLANE, lo:hi])
    next_row = jnp.where(is_ctx | (c == nc - 1), 0.0, next_ref[0, 0:1, lo:hi])
    return x, prev_row, next_row


def _ssd_kernel(*refs, direction, finalize, q, nc):
    if finalize:
        u_ref, prev_ref, next_ref, uc_ref, yb_ref, cw_ref, cb_ref, dtb_ref, a_ref, d_ref, nw_ref, y_ref, state_ref = refs
    else:
        u_ref, prev_ref, next_ref, uc_ref, cw_ref, cb_ref, dtb_ref, a_ref, y_ref, state_ref = refs

    @pl.when(pl.program_id(1) == 0)
    def _():
        state_ref[...] = jnp.zeros_like(state_ref)

    xr, prev_row, next_row = _scan_inputs(u_ref, prev_ref, next_ref, uc_ref, SSD_XBC0, SSD_DT0, direction, nc)
    xc = _silu(_conv3(xr, prev_row, next_row, cw_ref[...], cb_ref[...], q))
    u_rest, _, _ = _scan_inputs(u_ref, prev_ref, next_ref, uc_ref, SSD_DT0, SSD_COLS, direction, nc)

    dt = _softplus(u_rest + dtb_ref[...])
    mask = _scan_mask(q, direction)
    cum = _masked_scan(mask, dt * a_ref[...])
    cum_t = cum.T
    end = q - 1 if direction == 0 else 0

    ys = []
    for g in range(SSD_GROUPS):
        b0 = SSD_WIDTH + g * SSD_STATE
        c0 = SSD_WIDTH + (SSD_GROUPS + g) * SSD_STATE
        bm_t = xc[:, b0:b0 + SSD_STATE].T
        cm = xc[:, c0:c0 + SSD_STATE].astype(BF16)
        scores = jnp.dot(cm, bm_t.astype(BF16), preferred_element_type=F32)
        for h in range(g * (SSD_HEADS // SSD_GROUPS), (g + 1) * (SSD_HEADS // SSD_GROUPS)):
            hl = direction * SSD_HEADS + h
            col = cum[:, hl:hl + 1]
            row = cum_t[hl:hl + 1, :]
            seg = jnp.exp(jnp.where(mask, col - row, -jnp.inf))
            xdt = (xc[:, h * HEAD_DIM:(h + 1) * HEAD_DIM] * dt[:, hl:hl + 1]).astype(BF16)
            y = jnp.dot((scores * seg).astype(BF16), xdt, preferred_element_type=F32)
            st = state_ref[h]
            y = y + jnp.dot(cm, st.astype(BF16), preferred_element_type=F32) * jnp.exp(col)
            tot = cum[end:end + 1, hl:hl + 1]
            upd = jnp.dot((bm_t * jnp.exp(tot - row)).astype(BF16), xdt, preferred_element_type=F32)
            state_ref[h] = st * jnp.exp(tot) + upd
            ys.append(y)
    y_all = jnp.concatenate(ys, axis=-1)
    if finalize:
        z, _, _ = _scan_inputs(u_ref, prev_ref, next_ref, uc_ref, 0, SSD_WIDTH, direction, nc)
        t = (y_all + yb_ref[0] + xc[:, 0:SSD_WIDTH] * d_ref[...]) * _silu(z)
        y_all = t * lax.rsqrt(jnp.mean(t * t, axis=-1, keepdims=True) + EPS) * nw_ref[...]
    y_ref[0] = y_all


def ssd_pass(u, u_ctx, y_other, sp, direction, q):
    b, L, _ = u.shape
    nc = L // q
    finalize = y_other is not None
    in_specs, yidx = _scan_specs(L, q, nc, SSD_COLS, direction)
    const2 = lambda bi, j: (0, 0)
    y_spec = pl.BlockSpec((1, q, SSD_WIDTH), lambda bi, j: (bi, yidx(j), 0))
    args = [u, u, u, u_ctx]
    if finalize:
        in_specs.append(y_spec)
        args.append(y_other)
    consts = [sp['cw'], sp['cb'], sp['dtb'], sp['a']] + ([sp['d'], sp['nw']] if finalize else [])
    in_specs += [pl.BlockSpec(t.shape, const2) for t in consts]
    args += consts
    return pl.pallas_call(
        functools.partial(_ssd_kernel, direction=direction, finalize=finalize, q=q, nc=nc),
        out_shape=jax.ShapeDtypeStruct((b, L + q, SSD_WIDTH), F32),
        grid=(b, nc + 1),
        in_specs=in_specs,
        out_specs=y_spec,
        scratch_shapes=[pltpu.VMEM((SSD_HEADS, HEAD_DIM, SSD_STATE), F32)],
        compiler_params=_params("parallel", "arbitrary"),
        name="ssd_scan_%s" % ("fwd" if direction == 0 else "bwd"),
    )(*args)


def ssd_prepare(p):
    pad = lambda v: jnp.pad(v.reshape(1, -1), ((0, 0), (0, LANE - v.size)))
    return dict(cw=p['ssd_conv_w'], cb=p['ssd_conv_b'].reshape(1, -1),
                dtb=pad(p['ssd_dt_bias']), a=pad(-jnp.exp(p['ssd_a_log'])),
                d=jnp.repeat(p['ssd_d'], HEAD_DIM).reshape(1, -1), nw=p['ssd_norm_w'].reshape(1, -1))


def ssd_mixer(u, u_ctx, sp, q=SCAN_CHUNK):
    assert u_ctx.shape[1] == q
    return ssd_pass(u, u_ctx, ssd_pass(u, u_ctx, None, sp, 1, q), sp, 0, q)


def _ml_kernel(*refs, direction, finalize, q, nc):
    if finalize:
        u_ref, prev_ref, next_ref, uc_ref, hb_ref, cw_ref, cb_ref, gb_ref, nw_ref, pool_ref, y_ref, s_ref, m_ref = refs
    else:
        u_ref, prev_ref, next_ref, uc_ref, cw_ref, cb_ref, gb_ref, y_ref, s_ref, m_ref = refs

    @pl.when(pl.program_id(1) == 0)
    def _():
        s_ref[...] = jnp.zeros_like(s_ref)
        m_ref[...] = jnp.zeros_like(m_ref)

    xr, prev_row, next_row = _scan_inputs(u_ref, prev_ref, next_ref, uc_ref, 0, ML_V0, direction, nc)
    qk = _silu(_conv3(xr, prev_row, next_row, cw_ref[...], cb_ref[...], q))
    rest, _, _ = _scan_inputs(u_ref, prev_ref, next_ref, uc_ref, ML_V0, ML_COLS, direction, nc)
    v = rest[:, 0:ML_WIDTH]

    gb = rest[:, ML_G0 - ML_V0:] + gb_ref[...]
    mask = _scan_mask(q, direction)
    cum = jnp.dot(mask.astype(F32), _log_sigmoid(gb), preferred_element_type=F32, precision=HI)
    ig = pltpu.roll(gb, ML_HEADS, axis=1)
    end = q - 1 if direction == 0 else 0
    m_prev = m_ref[0:1, :]
    tot = cum[end:end + 1, :]
    w_end = tot - cum + ig
    m_loc = jnp.max(w_end, axis=0, keepdims=True)
    e_end = jnp.exp(w_end - m_loc)
    m_new = jnp.maximum(tot + m_prev, m_loc)
    a_prev = jnp.exp(tot + m_prev - m_new)
    a_loc = jnp.exp(m_loc - m_new)
    inter = cum + m_prev
    rel = ig - cum
    m_t = jnp.maximum(inter, cum + _running_max(rel, direction, q))
    col_a = cum - m_t
    a_inter = jnp.exp(inter - m_t)
    floor = jnp.exp(-m_t)
    rel_t = rel.T
    e_end_t = e_end.T
    k_t = (qk[:, ML_WIDTH:2 * ML_WIDTH] * (HEAD_DIM ** -0.5)).T
    one_col = (lax.broadcasted_iota(jnp.int32, (q, HEAD_DIM), 1) == 0).astype(F32)

    ys = []
    for h in range(ML_HEADS):
        fl = direction * 2 * ML_HEADS + ML_HEADS + h
        qh = qk[:, h * HEAD_DIM:(h + 1) * HEAD_DIM].astype(BF16)
        kh_t = k_t[h * HEAD_DIM:(h + 1) * HEAD_DIM, :]
        v_ext = jnp.concatenate([v[:, h * HEAD_DIM:(h + 1) * HEAD_DIM], one_col], axis=-1).astype(BF16)
        pw = jnp.exp(jnp.where(mask, col_a[:, fl:fl + 1] + rel_t[fl:fl + 1, :], -jnp.inf))
        scores = jnp.dot(qh, kh_t.astype(BF16), preferred_element_type=F32)
        nd = jnp.dot((scores * pw).astype(BF16), v_ext, preferred_element_type=F32)
        st = s_ref[h]
        nd = nd + a_inter[:, fl:fl + 1] * jnp.dot(qh, st.astype(BF16), preferred_element_type=F32)
        den = nd[:, HEAD_DIM:HEAD_DIM + 1]
        ys.append(nd[:, 0:HEAD_DIM] / jnp.maximum(jnp.abs(den), floor[:, fl:fl + 1]))
        upd = jnp.dot((kh_t * e_end_t[fl:fl + 1, :]).astype(BF16), v_ext, preferred_element_type=F32)
        s_ref[h] = a_prev[:, fl:fl + 1] * st + a_loc[:, fl:fl + 1] * upd
    m_ref[...] = jnp.broadcast_to(m_new, m_ref.shape)
    y_all = jnp.concatenate(ys, axis=-1)
    if finalize:
        hs = y_all + hb_ref[0]
        hc = hs - jnp.dot(hs.astype(BF16), pool_ref[...], preferred_element_type=F32)
        var = jnp.dot((hc * hc).astype(BF16), pool_ref[...], preferred_element_type=F32)
        y_all = hc * lax.rsqrt(var + EPS) * nw_ref[...] * jax.nn.sigmoid(rest[:, ML_O0 - ML_V0:ML_G0 - ML_V0])
    y_ref[0] = y_all


def ml_pass(u, u_ctx, h_other, mp, direction, q):
    b, L, _ = u.shape
    nc = L // q
    finalize = h_other is not None
    in_specs, yidx = _scan_specs(L, q, nc, ML_COLS, direction)
    const2 = lambda bi, j: (0, 0)
    y_spec = pl.BlockSpec((1, q, ML_WIDTH), lambda bi, j: (bi, yidx(j), 0))
    args = [u, u, u, u_ctx]
    if finalize:
        in_specs.append(y_spec)
        args.append(h_other)
    consts = [mp['cw'], mp['cb'], mp['gb']] + ([mp['nw'], mp['pool']] if finalize else [])
    in_specs += [pl.BlockSpec(t.shape, const2) for t in consts]
    args += consts
    return pl.pallas_call(
        functools.partial(_ml_kernel, direction=direction, finalize=finalize, q=q, nc=nc),
        out_shape=jax.ShapeDtypeStruct((b, L + q, ML_WIDTH), F32),
        grid=(b, nc + 1),
        in_specs=in_specs,
        out_specs=y_spec,
        scratch_shapes=[pltpu.VMEM((ML_HEADS, HEAD_DIM, LANE), F32), pltpu.VMEM((SUBLANE, LANE), F32)],
        compiler_params=_params("parallel", "arbitrary"),
        name="mlstm_scan_%s" % ("fwd" if direction == 0 else "bwd"),
    )(*args)


def ml_prepare(p):
    gb = p['ml_gate_b'].reshape(1, -1)
    head = np.arange(ML_WIDTH) // HEAD_DIM
    pool = _host_bf16((head[:, None] == head[None, :]) / HEAD_DIM)
    return dict(cw=p['ml_conv_w'], cb=p['ml_conv_b'].reshape(1, -1),
                gb=jnp.pad(gb, ((0, 0), (0, LANE - gb.shape[1]))), nw=p['ml_norm_w'].reshape(1, -1), pool=pool)


def ml_mixer(u, u_ctx, mp, q=SCAN_CHUNK):
    assert u_ctx.shape[1] == q
    return ml_pass(u, u_ctx, ml_pass(u, u_ctx, None, mp, 1, q), mp, 0, q)


FFT_L = 4096
FFT_N = 2 * FFT_L
FFT_N2 = 128
FFT_N1 = FFT_N // FFT_N2
FFT_N1H = FFT_L // FFT_N2
FFT_K1 = FFT_N1 // 2 + 1
FFT_R = 80
FFT_UNROLL = 16
FFT_K1_UNROLL = 11
FFT_PITCH = FFT_N2 + SUBLANE


def _fft_tables():
    n2 = np.arange(FFT_N2)[:, None, None]
    k1 = np.arange(FFT_K1)[None, :, None]
    n1 = np.arange(FFT_N1H)[None, None, :]
    th = 2 * np.pi * (((FFT_N2 * n1 + n2) * k1) % FFT_N) / FFT_N
    f1 = np.zeros((FFT_N2, FFT_R, FFT_N1H))
    f1[:, 0:2 * FFT_K1:2, :] = np.cos(th)
    f1[:, 1:2 * FFT_K1:2, :] = -np.sin(th)
    wgt = np.where((np.arange(FFT_K1) == 0) | (np.arange(FFT_K1) == FFT_N1 // 2), 1.0, 2.0)[None, :, None] / FFT_N
    g1 = np.zeros((FFT_N2, FFT_N1H, FFT_R))
    g1[:, :, 0:2 * FFT_K1:2] = np.transpose(wgt * np.cos(th), (0, 2, 1))
    g1[:, :, 1:2 * FFT_K1:2] = np.transpose(-wgt * np.sin(th), (0, 2, 1))
    ph = 2 * np.pi * ((np.arange(FFT_N2)[:, None] * np.arange(FFT_N2)[None, :]) % FFT_N2) / FFT_N2
    c, s = np.cos(ph), np.sin(ph)
    f2 = np.block([[c, s], [-s, c]])
    sign = np.where(np.arange(FFT_R) // 2 % 2 == 0, 1.0, -1.0)[None, :, None]
    f1ab = np.concatenate([f1, f1 * sign], axis=2)
    return _host_bf16(f1), _host_bf16(f1ab), _host_bf16(f2), _host_bf16(f2.T), _host_bf16(g1)


def _sld(ref, n2, count):
    rows = pl.ds(n2, count, stride=FFT_PITCH)
    return jnp.concatenate([ref[0, rows, :], ref[1, rows, :]], axis=-1)


def _sst(ref, n2, count, val):
    rows = pl.ds(n2, count, stride=FFT_PITCH)
    ref[0, rows, :] = val[:, 0:LANE]
    ref[1, rows, :] = val[:, LANE:2 * LANE]


def _blk_ld(ref, blk, nblk):
    parts = []
    for k in range(nblk):
        rows = pl.ds(pl.multiple_of((blk + k) * FFT_PITCH, SUBLANE), FFT_N2)
        parts.append(jnp.concatenate([ref[0, rows, :], ref[1, rows, :]], axis=-1))
    return parts[0] if nblk == 1 else jnp.concatenate(parts, axis=0)


def _blk_st(ref, blk, nblk, val):
    for k in range(nblk):
        rows = pl.ds(pl.multiple_of((blk + k) * FFT_PITCH, SUBLANE), FFT_N2)
        ref[0, rows, :] = val[k * FFT_N2:(k + 1) * FFT_N2, 0:LANE]
        ref[1, rows, :] = val[k * FFT_N2:(k + 1) * FFT_N2, LANE:2 * LANE]


def _fft_stage1(z_ref, a_ref, f1_ref, n_in=FFT_N1H):
    def body(n2, carry):
        xs = _sld(z_ref, n2, n_in).astype(BF16)
        _sst(a_ref, n2, FFT_R, jnp.dot(f1_ref[n2], xs, preferred_element_type=F32))
        return carry
    lax.fori_loop(0, FFT_N2, body, 0, unroll=FFT_UNROLL)


def _spectrum_kernel(x_ref, inorm_ref, f1_ref, f2_ref, o_ref, z_ref, a_ref):
    for half in range(2):
        for n1 in range(FFT_N1H):
            _blk_st(z_ref, half * FFT_N1H + n1, 1, x_ref[half, n1 * FFT_N2:(n1 + 1) * FFT_N2, :])
    _fft_stage1(z_ref, a_ref, f1_ref, 2 * FFT_N1H)
    inorm = inorm_ref[0]

    def body(k1, carry):
        slab = _blk_ld(a_ref, 2 * k1, 2).astype(BF16)
        o_ref[0, k1] = (jnp.dot(f2_ref[...], slab, preferred_element_type=F32) * inorm).astype(BF16)
        return carry
    lax.fori_loop(0, FFT_K1, body, 0, unroll=FFT_K1_UNROLL)


def hyena_filter_spectra(sig, inv_norm):
    _, L, c = sig.shape
    _, f1ab, f2, _, _ = _fft_tables()
    one = pl.Buffered(1)
    return pl.pallas_call(
        _spectrum_kernel,
        out_shape=jax.ShapeDtypeStruct((HYENA_ORDER, FFT_K1, 2 * FFT_N2, c), BF16),
        grid=(HYENA_ORDER,),
        in_specs=[pl.BlockSpec((2, L, c), lambda i: (i, 0, 0)),
                  pl.BlockSpec((1, 1, c), lambda i: (i, 0, 0)),
                  pl.BlockSpec(f1ab.shape, lambda i: (0, 0, 0), pipeline_mode=one),
                  pl.BlockSpec(f2.shape, lambda i: (0, 0), pipeline_mode=one)],
        out_specs=pl.BlockSpec((1, FFT_K1, 2 * FFT_N2, c), lambda i: (i, 0, 0, 0)),
        scratch_shapes=[pltpu.VMEM((2, 2 * FFT_N1H * FFT_PITCH, LANE), F32),
                        pltpu.VMEM((2, FFT_R * FFT_PITCH, LANE), F32)],
        compiler_params=pltpu.CompilerParams(dimension_semantics=("arbitrary",), vmem_limit_bytes=VMEM_LIMIT_HY),
        name="hyena_filter_spectrum",
    )(sig, inv_norm, f1ab, f2)


def _conv3_rows(src, dst, cw, cb, L):
    rows = 2 * FFT_N2
    nchunk = L // rows

    def body(i, carry):
        r0 = pl.multiple_of(i * rows, rows)
        prev_row = jnp.where(i > 0, src[pl.ds(jnp.maximum(r0 - 1, 0), 1), :], 0.0)
        next_row = jnp.where(i < nchunk - 1, src[pl.ds(jnp.minimum(r0 + rows, L - 1), 1), :], 0.0)
        _blk_st(dst, 2 * i, 2, _conv3(src[pl.ds(r0, rows), :], prev_row, next_row, cw, cb, rows))
        return carry
    lax.fori_loop(0, nchunk, body, 0, unroll=2)


def _hyena_kernel(v_ref, g_ref, h_ref, cwv_ref, cbv_ref, cwg_ref, cbg_ref, skip_ref,
                  f1_ref, f2_ref, f2t_ref, g1_ref, o_ref, z_ref, gc_ref, a_ref):
    order = pl.program_id(1)

    @pl.when(order == 0)
    def _():
        _conv3_rows(v_ref.at[0], z_ref, cwv_ref[...], cbv_ref[...], FFT_L)

    _conv3_rows(g_ref.at[0], gc_ref, cwg_ref[0], cbg_ref[0], FFT_L)
    _fft_stage1(z_ref, a_ref, f1_ref)

    def mid(k1, carry):
        x = jnp.dot(f2_ref[...], _blk_ld(a_ref, 2 * k1, 2).astype(BF16), preferred_element_type=F32)
        h = h_ref[0, k1].astype(F32)
        xr, xi, hr, hi = x[:FFT_N2], x[FFT_N2:], h[:FFT_N2], h[FFT_N2:]
        y = jnp.concatenate([xr * hr - xi * hi, xr * hi + xi * hr], axis=0).astype(BF16)
        _blk_st(a_ref, 2 * k1, 2, jnp.dot(f2t_ref[...], y, preferred_element_type=F32))
        return carry
    lax.fori_loop(0, FFT_K1, mid, 0, unroll=FFT_K1_UNROLL)

    skip = skip_ref[0]

    def last(n2, carry):
        bs = _sld(a_ref, n2, FFT_R).astype(BF16)
        y = jnp.dot(g1_ref[n2], bs, preferred_element_type=F32)
        _sst(z_ref, n2, FFT_N1H, _sld(gc_ref, n2, FFT_N1H) * (y + skip * _sld(z_ref, n2, FFT_N1H)))
        return carry
    lax.fori_loop(0, FFT_N2, last, 0, unroll=FFT_UNROLL)

    @pl.when(order == 1)
    def _():
        for n1 in range(FFT_N1H):
            o_ref[0, n1 * FFT_N2:(n1 + 1) * FFT_N2, :] = _blk_ld(z_ref, n1, 1)


def hyena_long(u, h_spec, cw, cb, skip):
    b, L, _ = u.shape
    c = HYENA_WIDTH
    f1, _, f2, f2t, g1 = _fft_tables()
    one = pl.Buffered(1)
    cw3 = cw.reshape(3, 3, c).transpose(1, 0, 2)
    cb3 = cb.reshape(3, 1, c)
    return pl.pallas_call(
        _hyena_kernel,
        out_shape=jax.ShapeDtypeStruct((b, L, c), F32),
        grid=(b, 2),
        in_specs=[pl.BlockSpec((1, L, c), lambda bi, o: (bi, 0, 0), pipeline_mode=one),
                  pl.BlockSpec((1, L, c), lambda bi, o: (bi, 0, 1 + o)),
                  pl.BlockSpec((1, FFT_K1, 2 * FFT_N2, c), lambda bi, o: (o, 0, 0, 0)),
                  pl.BlockSpec((3, c), lambda bi, o: (0, 0)),
                  pl.BlockSpec((1, c), lambda bi, o: (0, 0)),
                  pl.BlockSpec((1, 3, c), lambda bi, o: (1 + o, 0, 0)),
                  pl.BlockSpec((1, 1, c), lambda bi, o: (1 + o, 0, 0)),
                  pl.BlockSpec((1, 1, c), lambda bi, o: (o, 0, 0)),
                  pl.BlockSpec(f1.shape, lambda bi, o: (0, 0, 0), pipeline_mode=one),
                  pl.BlockSpec(f2.shape, lambda bi, o: (0, 0), pipeline_mode=one),
                  pl.BlockSpec(f2t.shape, lambda bi, o: (0, 0), pipeline_mode=one),
                  pl.BlockSpec(g1.shape, lambda bi, o: (0, 0, 0), pipeline_mode=one)],
        out_specs=pl.BlockSpec((1, L, c), lambda bi, o: (bi, 0, 0)),
        scratch_shapes=[pltpu.VMEM((2, FFT_N1H * FFT_PITCH, LANE), F32),
                        pltpu.VMEM((2, FFT_N1H * FFT_PITCH, LANE), F32),
                        pltpu.VMEM((2, FFT_R * FFT_PITCH, LANE), F32)],
        compiler_params=pltpu.CompilerParams(dimension_semantics=("parallel", "arbitrary"),
                                             vmem_limit_bytes=VMEM_LIMIT_HY),
        name="hyena_long_conv",
    )(u, u, h_spec, cw3[0], cb3[0], cw3, cb3, skip.reshape(2, 1, c), f1, f2, f2t, g1)


HY_FILT = 2 * HYENA_ORDER * HYENA_WIDTH
HY_HALF = HYENA_ORDER * HYENA_WIDTH


def _filter_kernel(wt_ref, wc_ref, ws_ref, b1_ref, w2_ref, b2_ref, w3f_ref, w3b_ref, freq_ref, decay_ref,
                   sig_ref, asum_ref, *, L, rows):
    i = pl.program_id(0)
    fh = w2_ref.shape[0] // 2
    scale = 1.0 / float(max(L - 1, 1))

    @pl.when(i == 0)
    def _():
        asum_ref[...] = jnp.zeros_like(asum_ref)

    n_lane = (i * rows + lax.broadcasted_iota(jnp.int32, (1, rows), 1)).astype(F32)
    sub = lax.broadcasted_iota(jnp.int32, (2 * HYENA_BANDS, 1), 0)
    bands = 1e-4 + (sub % HYENA_BANDS).astype(F32) * ((HYENA_BANDS - 1 - 1e-4) / (HYENA_BANDS - 1))
    ang_t = (2 * math.pi / L) * jnp.where(sub < HYENA_BANDS, n_lane, L - n_lane) * bands
    cos_f, sin_f = jnp.cos(ang_t).T, jnp.sin(ang_t).T

    n = (i * rows + lax.broadcasted_iota(jnp.int32, (rows, 1), 0)).astype(F32)
    tu_f, tu_b = n * scale, (L - n) * scale
    lane = lax.broadcasted_iota(jnp.int32, (rows, 2 * fh), 1)
    freq = freq_ref[...]
    pre = (jnp.where(lane < fh, tu_f, tu_b) * wt_ref[...]
           + jnp.dot(cos_f, wc_ref[...], preferred_element_type=F32, precision=HI)
           - jnp.dot(sin_f, ws_ref[...], preferred_element_type=F32, precision=HI) + b1_ref[...])
    hdn = jnp.sin(freq * pre)
    hdn = jnp.sin(freq * (jnp.dot(hdn, w2_ref[...], preferred_element_type=F32, precision=HI) + b2_ref[...]))
    for side, (w3_ref, tu) in enumerate(((w3f_ref, tu_f), (w3b_ref, tu_b))):
        cols = slice(side * HY_HALF, (side + 1) * HY_HALF)
        val = jnp.dot(hdn, w3_ref[...], preferred_element_type=F32, precision=HI)
        val = val * jnp.exp(-tu * jnp.abs(decay_ref[:, cols]))
        if side == 1:
            val = jnp.where(n > 0, val, 0.0)
        for o in range(HYENA_ORDER):
            sig_ref[2 * o + side] = val[:, o * HYENA_WIDTH:(o + 1) * HYENA_WIDTH]
        asum_ref[side:side + 1, :] += jnp.sum(jnp.abs(val), axis=0, keepdims=True)


def _block_diag2(w):
    z = jnp.zeros_like(w)
    return jnp.concatenate([jnp.concatenate([w, z], axis=1), jnp.concatenate([z, w], axis=1)], axis=0)


def hyena_filter_signals(L, p):
    rows = min(L, TOKEN_TILE)
    fh = p['hy_pos_w1'].shape[1]
    w1, w3 = p['hy_pos_w1'], p['hy_pos_w3']
    twice = lambda v: jnp.tile(v.reshape(1, fh), (1, 2))
    zeros = jnp.zeros((fh, HY_HALF), F32)
    consts = [twice(w1[0]), _block_diag2(w1[1:1 + HYENA_BANDS]), _block_diag2(w1[1 + HYENA_BANDS:]),
              twice(p['hy_pos_b1']), _block_diag2(p['hy_pos_w2']), twice(p['hy_pos_b2']),
              jnp.concatenate([w3[:, :HY_HALF], zeros], axis=0), jnp.concatenate([zeros, w3[:, HY_HALF:]], axis=0),
              twice(p['hy_freq']), p['hy_decay'].reshape(1, HY_FILT)]
    sig, asum = pl.pallas_call(
        functools.partial(_filter_kernel, L=L, rows=rows),
        out_shape=(jax.ShapeDtypeStruct((2 * HYENA_ORDER, L, HYENA_WIDTH), F32),
                   jax.ShapeDtypeStruct((SUBLANE, HY_HALF), F32)),
        grid=(L // rows,),
        in_specs=[pl.BlockSpec(c.shape, lambda i: (0, 0)) for c in consts],
        out_specs=(pl.BlockSpec((2 * HYENA_ORDER, rows, HYENA_WIDTH), lambda i: (0, i, 0)),
                   pl.BlockSpec((SUBLANE, HY_HALF), lambda i: (0, 0))),
        compiler_params=_params("arbitrary"),
        name="hyena_filters",
    )(*consts)
    inv_norm = 1.0 / (asum[0] + asum[1]).reshape(HYENA_ORDER, 1, HYENA_WIDTH)
    return sig, inv_norm


def _rdft_tables(L):
    n_bins = L + 1
    half = -(-n_bins // 16) * 16
    k = np.arange(n_bins)[:, None]
    n = np.arange(L)[None, :]
    th = 2 * np.pi * ((k * n) % (2 * L)) / (2 * L)
    f = np.zeros((2 * half, L))
    f[:n_bins] = np.cos(th)
    f[half:half + n_bins] = -np.sin(th)
    sign = np.where(np.arange(n_bins) % 2 == 0, 1.0, -1.0)[:, None]
    fs = np.zeros_like(f)
    fs[:n_bins] = f[:n_bins] * sign
    fs[half:half + n_bins] = f[half:half + n_bins] * sign
    wgt = np.where((np.arange(n_bins) == 0) | (np.arange(n_bins) == L), 1.0, 2.0)[None, :] / (2 * L)
    g = np.zeros((L, 2 * half))
    g[:, :n_bins] = wgt * np.cos(th).T
    g[:, half:half + n_bins] = -wgt * np.sin(th).T
    return _host_bf16(f), _host_bf16(fs), _host_bf16(g), half


def _hyena_ctx_kernel(u_ref, sig_ref, inorm_ref, cw_ref, cb_ref, skip_ref, f_ref, fs_ref, g_ref, o_ref, *, L, half):
    zero_row = jnp.zeros((1, u_ref.shape[2]), F32)
    uc = _conv3(u_ref[0], zero_row, zero_row, cw_ref[...], cb_ref[...], L)
    z = uc[:, 0:HYENA_WIDTH]
    for o in range(HYENA_ORDER):
        h = (jnp.dot(f_ref[...], sig_ref[2 * o].astype(BF16), preferred_element_type=F32)
             + jnp.dot(fs_ref[...], sig_ref[2 * o + 1].astype(BF16), preferred_element_type=F32)) * inorm_ref[o]
        x = jnp.dot(f_ref[...], z.astype(BF16), preferred_element_type=F32)
        xr, xi, hr, hi = x[:half], x[half:], h[:half], h[half:]
        y = jnp.concatenate([xr * hr - xi * hi, xr * hi + xi * hr], axis=0).astype(BF16)
        conv = jnp.dot(g_ref[...], y, preferred_element_type=F32)
        z = uc[:, (o + 1) * HYENA_WIDTH:(o + 2) * HYENA_WIDTH] * (conv + skip_ref[o] * z)
    o_ref[0] = z


def hyena_short(u, sig, inv_norm, cw, cb, skip):
    b, L, cols = u.shape
    f, fs, g, half = _rdft_tables(L)
    const2 = lambda bi: (0, 0)
    const3 = lambda bi: (0, 0, 0)
    return pl.pallas_call(
        functools.partial(_hyena_ctx_kernel, L=L, half=half),
        out_shape=jax.ShapeDtypeStruct((b, L, HYENA_WIDTH), F32),
        grid=(b,),
        in_specs=[pl.BlockSpec((1, L, cols), lambda bi: (bi, 0, 0)),
                  pl.BlockSpec(sig.shape, const3), pl.BlockSpec(inv_norm.shape, const3),
                  pl.BlockSpec(cw.shape, const2), pl.BlockSpec((1, cols), const2),
                  pl.BlockSpec((HYENA_ORDER, 1, HYENA_WIDTH), const3),
                  pl.BlockSpec(f.shape, const2), pl.BlockSpec(fs.shape, const2), pl.BlockSpec(g.shape, const2)],
        out_specs=pl.BlockSpec((1, L, HYENA_WIDTH), lambda bi: (bi, 0, 0)),
        compiler_params=_params("parallel"),
        name="hyena_context",
    )(u, sig, inv_norm, cw, cb.reshape(1, cols), skip.reshape(HYENA_ORDER, 1, HYENA_WIDTH), f, fs, g)


def _split_bf16(a):
    hi = a.astype(BF16)
    return hi, (a - hi.astype(F32)).astype(BF16)


def _norm_router_kernel(*refs, n_lat, two_src):
    if two_src:
        xl_ref, xc_ref, nw_ref, sh_ref, sc_ref, whi_ref, wlo_ref, br_ref, tri_ref, h_ref, lg_ref, cnt_ref = refs
        x = jnp.where(pl.program_id(0) < n_lat, xl_ref[...], xc_ref[...])
    else:
        xl_ref, nw_ref, sh_ref, sc_ref, whi_ref, wlo_ref, br_ref, tri_ref, h_ref, lg_ref, cnt_ref = refs
        x = xl_ref[...]
    h = _modnorm(x, nw_ref[...], sh_ref[0], sc_ref[0])
    h_ref[...] = h
    h_hi, h_lo = _split_bf16(h)
    lg = (jnp.dot(h_hi, whi_ref[...], preferred_element_type=F32)
          + jnp.dot(h_hi, wlo_ref[...], preferred_element_type=F32)
          + jnp.dot(h_lo, whi_ref[...], preferred_element_type=F32)) + br_ref[...]
    lane = lax.broadcasted_iota(jnp.int32, lg.shape, 1)
    first = lambda hit: jnp.min(jnp.where(hit, lane, ROUTER_COLS), axis=-1, keepdims=True)
    gl = jnp.where(lane < N_GROUPS, lg, -jnp.inf)
    gmax = jnp.max(gl, axis=-1, keepdims=True)
    grp = first(gl == gmax)
    grp_p = 1.0 / jnp.sum(jnp.exp(gl - gmax), axis=-1, keepdims=True)
    lo = N_GROUPS + grp * EXPERTS_PER_GROUP
    el = jnp.where((lane >= lo) & (lane < lo + EXPERTS_PER_GROUP), lg, -jnp.inf)
    e1 = jnp.max(el, axis=-1, keepdims=True)
    i1 = first(el == e1)
    el2 = jnp.where(lane == i1, -jnp.inf, el)
    e2 = jnp.max(el2, axis=-1, keepdims=True)
    i2 = first(el2 == e2)
    r = jnp.exp(e2 - e1)
    w1 = grp_p / (1.0 + r)
    w2 = w1 * r
    @pl.when(pl.program_id(0) == 0)
    def _():
        cnt_ref[...] = jnp.zeros_like(cnt_ref)

    hit1, hit2 = lane == i1, lane == i2
    picks = jnp.where(hit1 | hit2, 1.0, 0.0)
    before = cnt_ref[0:1, :] + jnp.dot(tri_ref[...], picks.astype(BF16), preferred_element_type=F32)
    rank1 = jnp.sum(jnp.where(hit1, before, 0.0), axis=-1, keepdims=True)
    rank2 = jnp.sum(jnp.where(hit2, before, 0.0), axis=-1, keepdims=True)
    cnt_ref[...] = jnp.broadcast_to(cnt_ref[0:1, :] + jnp.sum(picks, axis=0, keepdims=True), cnt_ref.shape)
    vals = [(i1 - N_GROUPS).astype(F32), (i2 - N_GROUPS).astype(F32), w1, w2, rank1, rank2]
    out = jnp.zeros(lg.shape, F32)
    for k, val in enumerate(vals):
        out = jnp.where(lane == k, val, out)
    lg_ref[...] = out


def norm_router(xl, xc, nw, mod_l, mod_c, w_router, b_router, tm=TOKEN_TILE):
    b, L, d = xl.shape
    two_src = xc is not None
    n_lat = b * L // tm
    per_batch = L // tm
    n_ctx = (xc.shape[0] * xc.shape[1]) // tm if two_src else 0
    w_hi, w_lo = _split_bf16(w_router)
    const = lambda i: (0, 0)
    if two_src:
        shift = jnp.concatenate([mod_l[0], mod_c[0]], axis=0)
        scale = jnp.concatenate([mod_l[1], mod_c[1]], axis=0)
        mod_map = lambda i: (jnp.where(i < n_lat, i // per_batch, b), 0, 0)
        srcs = [xl.reshape(b * L, d), xc.reshape(-1, d)]
        src_specs = [pl.BlockSpec((tm, d), lambda i: (jnp.minimum(i, n_lat - 1), 0)),
                     pl.BlockSpec((tm, d), lambda i: (jnp.maximum(i - n_lat, 0), 0))]
    else:
        shift, scale = mod_l
        mod_map = lambda i: (i // per_batch, 0, 0)
        srcs = [xl.reshape(b * L, d)]
        src_specs = [pl.BlockSpec((tm, d), lambda i: (i, 0))]
    n_tok = (n_lat + n_ctx) * tm
    tri = _host_bf16(np.tril(np.ones((tm, tm)), -1))
    return pl.pallas_call(
        functools.partial(_norm_router_kernel, n_lat=n_lat, two_src=two_src),
        out_shape=(jax.ShapeDtypeStruct((n_tok, d), F32), jax.ShapeDtypeStruct((n_tok, ROUTER_COLS), F32),
                   jax.ShapeDtypeStruct((SUBLANE, ROUTER_COLS), F32)),
        grid=(n_lat + n_ctx,),
        in_specs=src_specs + [pl.BlockSpec((1, d), const), pl.BlockSpec((1, 1, d), mod_map),
                              pl.BlockSpec((1, 1, d), mod_map), pl.BlockSpec((d, ROUTER_COLS), const),
                              pl.BlockSpec((d, ROUTER_COLS), const), pl.BlockSpec((1, ROUTER_COLS), const),
                              pl.BlockSpec((tm, tm), const)],
        out_specs=(pl.BlockSpec((tm, d), lambda i: (i, 0)), pl.BlockSpec((tm, ROUTER_COLS), lambda i: (i, 0)),
                   pl.BlockSpec((SUBLANE, ROUTER_COLS), const)),
        compiler_params=_params("arbitrary"),
        name="moe_norm_router",
    )(*srcs, nw.reshape(1, d), shift, scale, w_hi, w_lo, b_router, tri)


def _expert_ffn_kernel(te_ref, x_ref, wg_ref, wu_ref, wd_ref, o_ref):
    x = x_ref[...].astype(BF16)
    g = jnp.dot(x, wg_ref[0].astype(BF16), preferred_element_type=F32)
    u = jnp.dot(x, wu_ref[0].astype(BF16), preferred_element_type=F32)
    o_ref[...] = jnp.dot((_silu(g) * u).astype(BF16), wd_ref[0].astype(BF16), preferred_element_type=F32)


def expert_ffn(x_sorted, tile_expert, w_gate, w_up, w_down, tm):
    r, d = x_sorted.shape
    f = w_gate.shape[-1]
    grid_spec = pltpu.PrefetchScalarGridSpec(
        num_scalar_prefetch=1,
        grid=(r // tm,),
        in_specs=[pl.BlockSpec((tm, d), lambda i, te: (i, 0)),
                  pl.BlockSpec((1, d, f), lambda i, te: (te[i], 0, 0)),
                  pl.BlockSpec((1, d, f), lambda i, te: (te[i], 0, 0)),
                  pl.BlockSpec((1, f, d), lambda i, te: (te[i], 0, 0))],
        out_specs=pl.BlockSpec((tm, d), lambda i, te: (i, 0)),
    )
    return pl.pallas_call(
        _expert_ffn_kernel,
        out_shape=jax.ShapeDtypeStruct((r, d), F32),
        grid_spec=grid_spec,
        compiler_params=_params("arbitrary"),
        name="moe_expert_ffn",
    )(tile_expert, x_sorted, w_gate, w_up, w_down)


def moe_apply(h_tokens, routed, counts_row, w_gate, w_up, w_down, layer, n_lat, tm=EXPERT_TILE):
    t, d = h_tokens.shape
    e_idx, rank = routed[:, 0:2].astype(jnp.int32), routed[:, 4:6].astype(jnp.int32)
    counts = counts_row[0, N_GROUPS:N_GROUPS + N_EXPERTS].astype(jnp.int32)
    n_pairs = 2 * t
    padded = (counts + tm - 1) // tm * tm
    pad_end = jnp.cumsum(padded)
    pad_start = pad_end - padded
    n_rows = n_pairs + N_EXPERTS * tm
    tile_start = jnp.arange(n_rows // tm, dtype=jnp.int32) * tm
    tile_expert = jnp.minimum(jnp.sum((pad_end[None, :] <= tile_start[:, None]).astype(jnp.int32), axis=1),
                              N_EXPERTS - 1)
    onehot = (e_idx[:, :, None] == jnp.arange(N_EXPERTS, dtype=jnp.int32)).astype(jnp.int32)
    pos = jnp.sum(onehot * pad_start, axis=-1) + rank
    j = jnp.arange(tm, dtype=jnp.int32)[None, :]
    fill_key = jnp.where(j < (padded - counts)[:, None], (pad_start + counts)[:, None] + j, n_rows)
    keys = jnp.concatenate([pos.reshape(-1), fill_key.reshape(-1)])
    toks = jnp.concatenate([jnp.arange(n_pairs, dtype=jnp.int32) // 2, jnp.arange(N_EXPERTS * tm, dtype=jnp.int32) % t])
    _, row_token = lax.sort((keys, toks), num_keys=1)
    x_sorted = h_tokens[row_token]
    wg = w_gate.reshape(-1, d, EXPERT_HIDDEN)
    wu = w_up.reshape(-1, d, EXPERT_HIDDEN)
    wd = w_down.reshape(-1, EXPERT_HIDDEN, d)
    y_sorted = expert_ffn(x_sorted, tile_expert + layer * N_EXPERTS, wg, wu, wd, tm)
    lat = (y_sorted[pos[:n_lat, 0]], y_sorted[pos[:n_lat, 1]])
    rest = (y_sorted[pos[n_lat:, 0]], y_sorted[pos[n_lat:, 1]]) if t > n_lat else None
    return lat, rest


def _final_kernel(x_ref, ya_ref, yb_ref, rt_ref, g_ref, w_ref, o_ref):
    x = x_ref[0] + g_ref[0] * _moe_mix(ya_ref[0], yb_ref[0], rt_ref[0])
    o_ref[0] = x * lax.rsqrt(jnp.mean(x * x, axis=-1, keepdims=True) + EPS) * w_ref[...]


def final_norm(x, ya, yb, routed, gate, w, tm=TOKEN_TILE):
    b, L, d = x.shape
    tok = pl.BlockSpec((1, tm, d), lambda bi, i: (bi, i, 0))
    return pl.pallas_call(
        _final_kernel,
        out_shape=jax.ShapeDtypeStruct((b, L, d), F32),
        grid=(b, L // tm),
        in_specs=[tok, tok, tok, pl.BlockSpec((1, tm, ROUTER_COLS), lambda bi, i: (bi, i, 0)),
                  pl.BlockSpec((1, 1, d), _mod_map(gate, b)), pl.BlockSpec((1, d), lambda bi, i: (0, 0))],
        out_specs=tok,
        compiler_params=_params("parallel", "arbitrary"),
        name="final_rmsnorm",
    )(x, ya, yb, routed, gate, w.reshape(1, d))


IN_SIZES = (SSD_WIDTH, SSD_CONV_CH, 2 * SSD_HEADS, HY_COLS, 2 * ML_WIDTH, ML_WIDTH, ML_WIDTH, 4 * ML_HEADS)


def _regroup_kernel(w_ref, o_ref):
    src = dst = 0
    for n in IN_SIZES:
        pad = -n % LANE
        o_ref[0, :, dst:dst + n] = w_ref[0, :, src:src + n].astype(BF16)
        if pad:
            o_ref[0, :, dst + n:dst + n + pad] = jnp.zeros((o_ref.shape[1], pad), BF16)
        src, dst = src + n, dst + n + pad


def regroup_in_weights(w_in):
    depth, d, n_in = w_in.shape
    n_out = sum(n + (-n % LANE) for n in IN_SIZES)
    one = pl.Buffered(1)
    return pl.pallas_call(
        _regroup_kernel,
        out_shape=jax.ShapeDtypeStruct((depth, d, n_out), BF16),
        grid=(depth,),
        in_specs=[pl.BlockSpec((1, d, n_in), lambda i: (i, 0, 0), pipeline_mode=one)],
        out_specs=pl.BlockSpec((1, d, n_out), lambda i: (i, 0, 0)),
        compiler_params=_params("arbitrary"),
        name="regroup_in_weights",
    )(w_in)


def kernel(x, c, ctx, c_ctx, w_mod, b_mod, norm1_w, norm2_w, w_in, w_out, ssd_conv_w, ssd_conv_b, ssd_dt_bias, ssd_a_log, ssd_d, ssd_norm_w, hy_conv_w, hy_conv_b, hy_pos_w1, hy_pos_b1, hy_pos_w2, hy_pos_b2, hy_pos_w3, hy_freq, hy_decay, hy_skip, ml_conv_w, ml_conv_b, ml_gate_b, ml_norm_w, grp_router_w, grp_router_b, exp_router_w, exp_router_b, moe_w_gate, moe_w_up, moe_w_down, final_norm_w):
    layer_params = dict(
        ssd_conv_w=ssd_conv_w, ssd_conv_b=ssd_conv_b, ssd_dt_bias=ssd_dt_bias, ssd_a_log=ssd_a_log,
        ssd_d=ssd_d, ssd_norm_w=ssd_norm_w, hy_conv_w=hy_conv_w, hy_conv_b=hy_conv_b,
        hy_pos_w1=hy_pos_w1, hy_pos_b1=hy_pos_b1, hy_pos_w2=hy_pos_w2, hy_pos_b2=hy_pos_b2,
        hy_pos_w3=hy_pos_w3, hy_freq=hy_freq, hy_decay=hy_decay, hy_skip=hy_skip,
        ml_conv_w=ml_conv_w, ml_conv_b=ml_conv_b, ml_gate_b=ml_gate_b, ml_norm_w=ml_norm_w)
    bsz, seq, d = x.shape
    n_ctx = ctx.shape[1]
    xl, xc = x, ctx
    moe_l = moe_c = None
    c_rows = jnp.concatenate([c, c_ctx[None, :], jnp.zeros((SUBLANE - bsz - 1, d), F32)], axis=0)
    w_in_all = regroup_in_weights(w_in)

    def layer_prep(i):
        p = {name: arr[i] for name, arr in layer_params.items()}
        mod = modulation(c_rows, w_mod, b_mod, i).reshape(SUBLANE, N_MOD, 1, d)
        pad = ROUTER_COLS - N_GROUPS - N_EXPERTS
        return dict(
            p=p, sp=ssd_prepare(p), mp=ml_prepare(p),
            mod_l=[mod[:bsz, k] for k in range(N_MOD)],
            mod_c=[mod[bsz:bsz + 1, k] for k in range(N_MOD)],
            w_out=w_out[i].astype(BF16),
            w_router=jnp.pad(jnp.concatenate([grp_router_w[i], exp_router_w[i]], axis=1), ((0, 0), (0, pad))),
            b_router=jnp.pad(jnp.concatenate([grp_router_b[i], exp_router_b[i]]), (0, pad)).reshape(1, ROUTER_COLS),
            h_spec=hyena_filter_spectra(*hyena_filter_signals(seq, p)),
            ctx_filt=hyena_filter_signals(n_ctx, p) if i < DEPTH - 1 else None)

    prep = layer_prep(0)
    for i in range(DEPTH):
        last = i == DEPTH - 1
        p, sp, mp, mod_l, mod_c, w_out_b = prep['p'], prep['sp'], prep['mp'], prep['mod_l'], prep['mod_c'], prep['w_out']

        uc_ssd, uc_hy, uc_ml, xc = norm_proj(xc, moe_c, norm1_w[i], mod_c[0], mod_c[1], w_in_all, i)
        col_major = i % 2 == 1
        ul_ssd, ul_hy, ul_ml, xl = norm_proj(xl, moe_l, norm1_w[i], mod_l[0], mod_l[1], w_in_all, i, col_major)
        y_ssd = ssd_mixer(ul_ssd, uc_ssd, sp)
        y_ml = ml_mixer(ul_ml, uc_ml, mp)
        yl_hy = hyena_long(ul_hy, prep['h_spec'], p['hy_conv_w'], p['hy_conv_b'], p['hy_skip'])
        xl = out_proj(y_ssd, yl_hy, y_ml, xl, mod_l[2], w_out_b, col_major)
        if not last:
            yc_hy = hyena_short(uc_hy, *prep['ctx_filt'], p['hy_conv_w'], p['hy_conv_b'], p['hy_skip'])
            xc = out_proj(y_ssd, yc_hy, y_ml, xc, mod_c[2], w_out_b, scan_off=seq // n_ctx)
        h_all, routed, counts_row = norm_router(xl, None if last else xc, norm2_w[i], (mod_l[3], mod_l[4]),
                                                (mod_c[3], mod_c[4]), prep['w_router'], prep['b_router'])
        if not last:
            prep = layer_prep(i + 1)
        lat, rest = moe_apply(h_all, routed, counts_row, moe_w_gate, moe_w_up, moe_w_down, i, bsz * seq)
        n_lat = bsz * seq
        moe_l = (lat[0].reshape(bsz, seq, d), lat[1].reshape(bsz, seq, d),
                 routed[:n_lat].reshape(bsz, seq, ROUTER_COLS), mod_l[5])
        if not last:
            moe_c = (rest[0].reshape(bsz, n_ctx, d), rest[1].reshape(bsz, n_ctx, d),
                     routed[n_lat:].reshape(bsz, n_ctx, ROUTER_COLS), mod_c[5])
    return final_norm(xl, *moe_l, final_norm_w)
```

```python
import functools
import math

import jax
import jax.numpy as jnp
import numpy as np
from jax import lax
from jax.experimental import pallas as pl
from jax.experimental.pallas import tpu as pltpu

D_MODEL = 1024
DEPTH = 2
GRID_W = 64
HEAD_DIM = 64
SSD_WIDTH = 384
SSD_HEADS = SSD_WIDTH // HEAD_DIM
SSD_GROUPS = 2
SSD_STATE = 64
HYENA_WIDTH = 256
HYENA_ORDER = 2
HYENA_BANDS = 16
ML_WIDTH = 384
ML_HEADS = ML_WIDTH // HEAD_DIM
N_GROUPS = 4
EXPERTS_PER_GROUP = 8
N_EXPERTS = N_GROUPS * EXPERTS_PER_GROUP
EXPERT_HIDDEN = 256
N_MOD = 6
EPS = 1e-6

LANE = 128
SUBLANE = 8
VMEM_LIMIT = 48 * 1024 * 1024
VMEM_LIMIT_HY = 56 * 1024 * 1024

TOKEN_TILE = 512
EXPERT_TILE = 256
MOD_TILE = 1536
SCAN_CHUNK = 256

SSD_CONV_CH = SSD_WIDTH + 2 * SSD_GROUPS * SSD_STATE
SSD_XBC0 = SSD_WIDTH
SSD_DT0 = SSD_XBC0 + SSD_CONV_CH
SSD_COLS = SSD_DT0 + LANE
HY_COLS = (HYENA_ORDER + 1) * HYENA_WIDTH
ML_V0 = 2 * ML_WIDTH
ML_O0 = ML_V0 + ML_WIDTH
ML_G0 = ML_O0 + ML_WIDTH
ML_COLS = ML_G0 + LANE
ROUTER_COLS = LANE

F32 = jnp.float32
BF16 = jnp.bfloat16
HI = lax.Precision.HIGHEST


def _params(*sem):
    return pltpu.CompilerParams(dimension_semantics=sem, vmem_limit_bytes=VMEM_LIMIT)


def _table_bf16(a):
    return jnp.asarray(np.asarray(a, np.float32)).astype(BF16)


def _silu(x):
    return x * jax.nn.sigmoid(x)


def _softplus(x):
    return jnp.maximum(x, 0.0) + jnp.log(1.0 + jnp.exp(-jnp.abs(x)))


def _log_sigmoid(x):
    return jnp.minimum(x, 0.0) - jnp.log(1.0 + jnp.exp(-jnp.abs(x)))


def _mod_kernel(c_ref, w_ref, b_ref, o_ref):
    o_ref[...] = jnp.dot(_silu(c_ref[...]), w_ref[...], preferred_element_type=F32, precision=HI) + b_ref[...]


def modulation(c_rows, w_mod, b_mod, layer):
    depth, d, n = w_mod.shape
    tn = MOD_TILE
    return pl.pallas_call(
        _mod_kernel,
        out_shape=jax.ShapeDtypeStruct((c_rows.shape[0], n), F32),
        grid=(n // tn,),
        in_specs=[pl.BlockSpec(c_rows.shape, lambda j: (0, 0)),
                  pl.BlockSpec((None, d, tn), lambda j: (layer, 0, j)),
                  pl.BlockSpec((None, 1, tn), lambda j: (layer, 0, j))],
        out_specs=pl.BlockSpec((c_rows.shape[0], tn), lambda j: (0, j)),
        compiler_params=_params("arbitrary"),
        name="adaln_modulation",
    )(c_rows, w_mod, b_mod.reshape(depth, 1, n))


def _modnorm(x, nw, shift, scale):
    y = x * lax.rsqrt(jnp.mean(x * x, axis=-1, keepdims=True) + EPS) * nw
    return y * (1.0 + scale) + shift


def _mod_map(mod, b):
    return (lambda bi, i: (bi, 0, 0)) if mod.shape[0] == b else (lambda bi, i: (0, 0, 0))


def _tok_view(x, col_major):
    b, L, d = x.shape
    return x.reshape(b, L // GRID_W, GRID_W, d) if col_major else x


def _tok_spec(L, d, tm, col_major):
    if col_major:
        assert tm == (L // GRID_W) * SUBLANE
        return pl.BlockSpec((1, L // GRID_W, SUBLANE, d), lambda bi, i: (bi, 0, i, 0))
    return pl.BlockSpec((1, tm, d), lambda bi, i: (bi, i, 0))


def _tok_load(ref, col_major):
    if not col_major:
        return ref[0]
    return jnp.concatenate([ref[0, :, j, :] for j in range(ref.shape[2])], axis=0)


def _tok_store(ref, val, col_major):
    if not col_major:
        ref[0] = val
        return
    rows = ref.shape[1]
    for j in range(ref.shape[2]):
        ref[0, :, j, :] = val[j * rows:(j + 1) * rows]


def _moe_mix(ya, yb, routed):
    return routed[:, 2:3] * ya + routed[:, 3:4] * yb


def _norm_proj_kernel(*refs, col_major, fuse_moe):
    if fuse_moe:
        x_ref, ya_ref, yb_ref, rt_ref, g_ref, nw_ref, sh_ref, sc_ref, w_ref, ssd_ref, hy_ref, ml_ref, xo_ref = refs
    else:
        x_ref, nw_ref, sh_ref, sc_ref, w_ref, ssd_ref, hy_ref, ml_ref = refs
    x = _tok_load(x_ref, col_major)
    if fuse_moe:
        x = x + g_ref[0] * _moe_mix(_tok_load(ya_ref, col_major), _tok_load(yb_ref, col_major),
                                    _tok_load(rt_ref, col_major))
        _tok_store(xo_ref, x, col_major)
    h = _modnorm(x, nw_ref[...], sh_ref[0], sc_ref[0])
    u = jnp.dot(h.astype(BF16), w_ref[...], preferred_element_type=F32)
    ssd_ref[0] = u[:, 0:SSD_COLS]
    hy_ref[0] = u[:, SSD_COLS:SSD_COLS + HY_COLS]
    ml_ref[0] = u[:, SSD_COLS + HY_COLS:]


def norm_proj(x, moe, nw, shift, scale, w_all, layer, col_major=False, tm=TOKEN_TILE):
    b, L, d = x.shape
    n = w_all.shape[2]
    fuse_moe = moe is not None
    tm = (L // GRID_W) * SUBLANE if col_major else min(tm, L)
    tok = _tok_spec(L, d, tm, col_major)
    row = lambda bi, i: (bi, i, 0)
    const2 = lambda bi, i: (0, 0)
    args, in_specs = [_tok_view(x, col_major)], [tok]
    if fuse_moe:
        ya, yb, routed, gate = moe
        args += [_tok_view(ya, col_major), _tok_view(yb, col_major), _tok_view(routed, col_major), gate]
        in_specs += [tok, tok, _tok_spec(L, ROUTER_COLS, tm, col_major), pl.BlockSpec((1, 1, d), _mod_map(gate, b))]
    args += [nw.reshape(1, d), shift, scale, w_all]
    in_specs += [pl.BlockSpec((1, d), const2), pl.BlockSpec((1, 1, d), _mod_map(shift, b)),
                 pl.BlockSpec((1, 1, d), _mod_map(scale, b)),
                 pl.BlockSpec((None, d, n), lambda bi, i: (layer, 0, 0))]
    out_shape = [jax.ShapeDtypeStruct((b, L, SSD_COLS), F32), jax.ShapeDtypeStruct((b, L, HY_COLS), F32),
                 jax.ShapeDtypeStruct((b, L, ML_COLS), F32)]
    out_specs = [pl.BlockSpec((1, tm, SSD_COLS), row), pl.BlockSpec((1, tm, HY_COLS), row),
                 pl.BlockSpec((1, tm, ML_COLS), row)]
    if fuse_moe:
        out_shape.append(jax.ShapeDtypeStruct(args[0].shape, F32))
        out_specs.append(tok)
    outs = pl.pallas_call(
        functools.partial(_norm_proj_kernel, col_major=col_major, fuse_moe=fuse_moe),
        out_shape=tuple(out_shape),
        grid=(b, L // tm),
        in_specs=in_specs,
        out_specs=tuple(out_specs),
        compiler_params=_params("parallel", "arbitrary"),
        name="norm_in_proj",
    )(*args)
    return (*outs[:3], outs[3].reshape(b, L, d) if fuse_moe else x)


def _out_proj_kernel(ys_ref, yh_ref, ym_ref, x_ref, g_ref, w_ref, o_ref, *, col_major):
    y = jnp.concatenate([ys_ref[0], yh_ref[0], ym_ref[0]], axis=-1).astype(BF16)
    r = _tok_load(x_ref, col_major) + g_ref[0] * jnp.dot(y, w_ref[...], preferred_element_type=F32)
    _tok_store(o_ref, r, col_major)


def out_proj(y_ssd, y_hy, y_ml, x, gate, w_bf16, col_major=False, scan_off=0, tm=TOKEN_TILE):
    b, L, d = x.shape
    tm = (L // GRID_W) * SUBLANE if col_major else min(tm, L)
    row = lambda bi, i: (bi, i, 0)
    scan_row = lambda bi, i: (bi, i + scan_off, 0)
    tok = _tok_spec(L, d, tm, col_major)
    xv = _tok_view(x, col_major)
    return pl.pallas_call(
        functools.partial(_out_proj_kernel, col_major=col_major),
        out_shape=jax.ShapeDtypeStruct(xv.shape, F32),
        grid=(b, L // tm),
        in_specs=[pl.BlockSpec((1, tm, SSD_WIDTH), scan_row), pl.BlockSpec((1, tm, HYENA_WIDTH), row),
                  pl.BlockSpec((1, tm, ML_WIDTH), scan_row), tok,
                  pl.BlockSpec((1, 1, d), _mod_map(gate, b)),
                  pl.BlockSpec(w_bf16.shape, lambda bi, i: (0, 0))],
        out_specs=tok,
        compiler_params=_params("parallel", "arbitrary"),
        name="out_proj_residual",
    )(y_ssd, y_hy, y_ml, xv, gate, w_bf16).reshape(b, L, d)


def _conv3(xr, prev_row, next_row, cw, cb, q):
    rid = lax.broadcasted_iota(jnp.int32, (q, 1), 0)
    x_prev = jnp.where(rid == 0, prev_row, pltpu.roll(xr, 1, axis=0))
    x_next = jnp.where(rid == q - 1, next_row, pltpu.roll(xr, q - 1, axis=0))
    return x_prev * cw[0:1] + xr * cw[1:2] + x_next * cw[2:3] + cb


def _masked_scan(mask, x):
    m = mask.astype(BF16)
    hi = x.astype(BF16)
    r1 = x - hi.astype(F32)
    mid = r1.astype(BF16)
    lo = (r1 - mid.astype(F32)).astype(BF16)
    return (jnp.dot(m, hi, preferred_element_type=F32) + jnp.dot(m, mid, preferred_element_type=F32)
            + jnp.dot(m, lo, preferred_element_type=F32))


def _scan_mask(q, direction):
    li = lax.broadcasted_iota(jnp.int32, (q, q), 0)
    si = lax.broadcasted_iota(jnp.int32, (q, q), 1)
    return (si <= li) if direction == 0 else (si >= li)


def _running_max(x, direction, q):
    rid = lax.broadcasted_iota(jnp.int32, (q, 1), 0)
    s = 1
    while s < q:
        if direction == 0:
            x = jnp.where(rid >= s, jnp.maximum(x, pltpu.roll(x, s, axis=0)), x)
        else:
            x = jnp.where(rid < q - s, jnp.maximum(x, pltpu.roll(x, q - s, axis=0)), x)
        s *= 2
    return x


def _scan_specs(L, q, nc, cols, direction):
    hb = q // SUBLANE
    nrb = L // SUBLANE
    order = (lambda j: j) if direction == 0 else (lambda j: nc - 1 - j)
    cidx = lambda j: order(jnp.maximum(j - 1, 0))
    specs = [pl.BlockSpec((1, q, cols), lambda bi, j: (bi, cidx(j), 0)),
             pl.BlockSpec((1, SUBLANE, cols), lambda bi, j: (bi, jnp.maximum(cidx(j) * hb - 1, 0), 0)),
             pl.BlockSpec((1, SUBLANE, cols), lambda bi, j: (bi, jnp.minimum((cidx(j) + 1) * hb, nrb - 1), 0)),
             pl.BlockSpec((1, q, cols), lambda bi, j: (bi, 0, 0))]
    yidx = lambda j: jnp.where(j == 0, nc, cidx(j))
    return specs, yidx


def _scan_inputs(u_ref, prev_ref, next_ref, uc_ref, lo, hi, direction, nc):
    j = pl.program_id(1)
    jm = jnp.maximum(j - 1, 0)
    c = jm if direction == 0 else nc - 1 - jm
    is_ctx = j == 0
    x = jnp.where(is_ctx, uc_ref[0, :, lo:hi], u_ref[0, :, lo:hi])
    prev_row = jnp.where(is_ctx | (c == 0), 0.0, prev_ref[0, SUBLANE - 1:SUBLANE, lo:hi])
    next_row = jnp.where(is_ctx | (c == nc - 1), 0.0, next_ref[0, 0:1, lo:hi])
    return x, prev_row, next_row


def _ssd_kernel(*refs, direction, finalize, q, nc):
    if finalize:
        u_ref, prev_ref, next_ref, uc_ref, yb_ref, cw_ref, cb_ref, dtb_ref, a_ref, d_ref, nw_ref, y_ref, state_ref = refs
    else:
        u_ref, prev_ref, next_ref, uc_ref, cw_ref, cb_ref, dtb_ref, a_ref, y_ref, state_ref = refs

    @pl.when(pl.program_id(1) == 0)
    def _():
        state_ref[...] = jnp.zeros_like(state_ref)

    xr, prev_row, next_row = _scan_inputs(u_ref, prev_ref, next_ref, uc_ref, SSD_XBC0, SSD_DT0, direction, nc)
    xc = _silu(_conv3(xr, prev_row, next_row, cw_ref[...], cb_ref[...], q))
    u_rest, _, _ = _scan_inputs(u_ref, prev_ref, next_ref, uc_ref, SSD_DT0, SSD_COLS, direction, nc)

    dt = _softplus(u_rest + dtb_ref[...])
    mask = _scan_mask(q, direction)
    cum = _masked_scan(mask, dt * a_ref[...])
    cum_t = cum.T
    end = q - 1 if direction == 0 else 0

    ys = []
    for g in range(SSD_GROUPS):
        b0 = SSD_WIDTH + g * SSD_STATE
        c0 = SSD_WIDTH + (SSD_GROUPS + g) * SSD_STATE
        bm_t = xc[:, b0:b0 + SSD_STATE].T
        cm = xc[:, c0:c0 + SSD_STATE].astype(BF16)
        scores = jnp.dot(cm, bm_t.astype(BF16), preferred_element_type=F32)
        for h in range(g * (SSD_HEADS // SSD_GROUPS), (g + 1) * (SSD_HEADS // SSD_GROUPS)):
            hl = direction * SSD_HEADS + h
            col = cum[:, hl:hl + 1]
            row = cum_t[hl:hl + 1, :]
            seg = jnp.exp(jnp.where(mask, col - row, -jnp.inf))
            xdt = (xc[:, h * HEAD_DIM:(h + 1) * HEAD_DIM] * dt[:, hl:hl + 1]).astype(BF16)
            y = jnp.dot((scores * seg).astype(BF16), xdt, preferred_element_type=F32)
            st = state_ref[h]
            y = y + jnp.dot(cm, st.astype(BF16), preferred_element_type=F32) * jnp.exp(col)
            tot = cum[end:end + 1, hl:hl + 1]
            upd = jnp.dot((bm_t * jnp.exp(tot - row)).astype(BF16), xdt, preferred_element_type=F32)
            state_ref[h] = st * jnp.exp(tot) + upd
            ys.append(y)
    y_all = jnp.concatenate(ys, axis=-1)
    if finalize:
        z, _, _ = _scan_inputs(u_ref, prev_ref, next_ref, uc_ref, 0, SSD_WIDTH, direction, nc)
        t = (y_all + yb_ref[0] + xc[:, 0:SSD_WIDTH] * d_ref[...]) * _silu(z)
        y_all = t * lax.rsqrt(jnp.mean(t * t, axis=-1, keepdims=True) + EPS) * nw_ref[...]
    y_ref[0] = y_all


def ssd_pass(u, u_ctx, y_other, sp, direction, q):
    b, L, _ = u.shape
    nc = L // q
    finalize = y_other is not None
    in_specs, yidx = _scan_specs(L, q, nc, SSD_COLS, direction)
    const2 = lambda bi, j: (0, 0)
    y_spec = pl.BlockSpec((1, q, SSD_WIDTH), lambda bi, j: (bi, yidx(j), 0))
    args = [u, u, u, u_ctx]
    if finalize:
        in_specs.append(y_spec)
        args.append(y_other)
    consts = [sp['cw'], sp['cb'], sp['dtb'], sp['a']] + ([sp['d'], sp['nw']] if finalize else [])
    in_specs += [pl.BlockSpec(t.shape, const2) for t in consts]
    args += consts
    return pl.pallas_call(
        functools.partial(_ssd_kernel, direction=direction, finalize=finalize, q=q, nc=nc),
        out_shape=jax.ShapeDtypeStruct((b, L + q, SSD_WIDTH), F32),
        grid=(b, nc + 1),
        in_specs=in_specs,
        out_specs=y_spec,
        scratch_shapes=[pltpu.VMEM((SSD_HEADS, HEAD_DIM, SSD_STATE), F32)],
        compiler_params=_params("parallel", "arbitrary"),
        name="ssd_scan_%s" % ("fwd" if direction == 0 else "bwd"),
    )(*args)


def ssd_prepare(p):
    pad = lambda v: jnp.pad(v.reshape(1, -1), ((0, 0), (0, LANE - v.size)))
    return dict(cw=p['ssd_conv_w'], cb=p['ssd_conv_b'].reshape(1, -1),
                dtb=pad(p['ssd_dt_bias']), a=pad(-jnp.exp(p['ssd_a_log'])),
                d=jnp.repeat(p['ssd_d'], HEAD_DIM).reshape(1, -1), nw=p['ssd_norm_w'].reshape(1, -1))


def ssd_mixer(u, u_ctx, sp, q=SCAN_CHUNK):
    assert u_ctx.shape[1] == q
    return ssd_pass(u, u_ctx, ssd_pass(u, u_ctx, None, sp, 1, q), sp, 0, q)


def _ml_kernel(*refs, direction, finalize, q, nc):
    if finalize:
        u_ref, prev_ref, next_ref, uc_ref, hb_ref, cw_ref, cb_ref, gb_ref, nw_ref, pool_ref, y_ref, s_ref, m_ref = refs
    else:
        u_ref, prev_ref, next_ref, uc_ref, cw_ref, cb_ref, gb_ref, y_ref, s_ref, m_ref = refs

    @pl.when(pl.program_id(1) == 0)
    def _():
        s_ref[...] = jnp.zeros_like(s_ref)
        m_ref[...] = jnp.zeros_like(m_ref)

    xr, prev_row, next_row = _scan_inputs(u_ref, prev_ref, next_ref, uc_ref, 0, ML_V0, direction, nc)
    qk = _silu(_conv3(xr, prev_row, next_row, cw_ref[...], cb_ref[...], q))
    rest, _, _ = _scan_inputs(u_ref, prev_ref, next_ref, uc_ref, ML_V0, ML_COLS, direction, nc)
    v = rest[:, 0:ML_WIDTH]

    gb = rest[:, ML_G0 - ML_V0:] + gb_ref[...]
    mask = _scan_mask(q, direction)
    cum = jnp.dot(mask.astype(F32), _log_sigmoid(gb), preferred_element_type=F32, precision=HI)
    ig = pltpu.roll(gb, ML_HEADS, axis=1)
    end = q - 1 if direction == 0 else 0
    m_prev = m_ref[0:1, :]
    tot = cum[end:end + 1, :]
    w_end = tot - cum + ig
    m_loc = jnp.max(w_end, axis=0, keepdims=True)
    e_end = jnp.exp(w_end - m_loc)
    m_new = jnp.maximum(tot + m_prev, m_loc)
    a_prev = jnp.exp(tot + m_prev - m_new)
    a_loc = jnp.exp(m_loc - m_new)
    inter = cum + m_prev
    rel = ig - cum
    m_t = jnp.maximum(inter, cum + _running_max(rel, direction, q))
    col_a = cum - m_t
    a_inter = jnp.exp(inter - m_t)
    floor = jnp.exp(-m_t)
    rel_t = rel.T
    e_end_t = e_end.T
    k_t = (qk[:, ML_WIDTH:2 * ML_WIDTH] * (HEAD_DIM ** -0.5)).T
    one_col = (lax.broadcasted_iota(jnp.int32, (q, HEAD_DIM), 1) == 0).astype(F32)

    ys = []
    for h in range(ML_HEADS):
        fl = direction * 2 * ML_HEADS + ML_HEADS + h
        qh = qk[:, h * HEAD_DIM:(h + 1) * HEAD_DIM].astype(BF16)
        kh_t = k_t[h * HEAD_DIM:(h + 1) * HEAD_DIM, :]
        v_ext = jnp.concatenate([v[:, h * HEAD_DIM:(h + 1) * HEAD_DIM], one_col], axis=-1).astype(BF16)
        pw = jnp.exp(jnp.where(mask, col_a[:, fl:fl + 1] + rel_t[fl:fl + 1, :], -jnp.inf))
        scores = jnp.dot(qh, kh_t.astype(BF16), preferred_element_type=F32)
        nd = jnp.dot((scores * pw).astype(BF16), v_ext, preferred_element_type=F32)
        st = s_ref[h]
        nd = nd + a_inter[:, fl:fl + 1] * jnp.dot(qh, st.astype(BF16), preferred_element_type=F32)
        den = nd[:, HEAD_DIM:HEAD_DIM + 1]
        ys.append(nd[:, 0:HEAD_DIM] / jnp.maximum(jnp.abs(den), floor[:, fl:fl + 1]))
        upd = jnp.dot((kh_t * e_end_t[fl:fl + 1, :]).astype(BF16), v_ext, preferred_element_type=F32)
        s_ref[h] = a_prev[:, fl:fl + 1] * st + a_loc[:, fl:fl + 1] * upd
    m_ref[...] = jnp.broadcast_to(m_new, m_ref.shape)
    y_all = jnp.concatenate(ys, axis=-1)
    if finalize:
        hs = y_all + hb_ref[0]
        hc = hs - jnp.dot(hs.astype(BF16), pool_ref[...], preferred_element_type=F32)
        var = jnp.dot((hc * hc).astype(BF16), pool_ref[...], preferred_element_type=F32)
        y_all = hc * lax.rsqrt(var + EPS) * nw_ref[...] * jax.nn.sigmoid(rest[:, ML_O0 - ML_V0:ML_G0 - ML_V0])
    y_ref[0] = y_all


def ml_pass(u, u_ctx, h_other, mp, direction, q):
    b, L, _ = u.shape
    nc = L // q
    finalize = h_other is not None
    in_specs, yidx = _scan_specs(L, q, nc, ML_COLS, direction)
    const2 = lambda bi, j: (0, 0)
    y_spec = pl.BlockSpec((1, q, ML_WIDTH), lambda bi, j: (bi, yidx(j), 0))
    args = [u, u, u, u_ctx]
    if finalize:
        in_specs.append(y_spec)
        args.append(h_other)
    consts = [mp['cw'], mp['cb'], mp['gb']] + ([mp['nw'], mp['pool']] if finalize else [])
    in_specs += [pl.BlockSpec(t.shape, const2) for t in consts]
    args += consts
    return pl.pallas_call(
        functools.partial(_ml_kernel, direction=direction, finalize=finalize, q=q, nc=nc),
        out_shape=jax.ShapeDtypeStruct((b, L + q, ML_WIDTH), F32),
        grid=(b, nc + 1),
        in_specs=in_specs,
        out_specs=y_spec,
        scratch_shapes=[pltpu.VMEM((ML_HEADS, HEAD_DIM, LANE), F32), pltpu.VMEM((SUBLANE, LANE), F32)],
        compiler_params=_params("parallel", "arbitrary"),
        name="mlstm_scan_%s" % ("fwd" if direction == 0 else "bwd"),
    )(*args)


def ml_prepare(p):
    gb = p['ml_gate_b'].reshape(1, -1)
    head = np.arange(ML_WIDTH) // HEAD_DIM
    pool = _table_bf16((head[:, None] == head[None, :]) / HEAD_DIM)
    return dict(cw=p['ml_conv_w'], cb=p['ml_conv_b'].reshape(1, -1),
                gb=jnp.pad(gb, ((0, 0), (0, LANE - gb.shape[1]))), nw=p['ml_norm_w'].reshape(1, -1), pool=pool)


def ml_mixer(u, u_ctx, mp, q=SCAN_CHUNK):
    assert u_ctx.shape[1] == q
    return ml_pass(u, u_ctx, ml_pass(u, u_ctx, None, mp, 1, q), mp, 0, q)


FFT_L = 4096
FFT_N = 2 * FFT_L
FFT_N2 = 128
FFT_N1 = FFT_N // FFT_N2
FFT_N1H = FFT_L // FFT_N2
FFT_K1 = FFT_N1 // 2 + 1
FFT_R = 80
FFT_UNROLL = 16
FFT_K1_UNROLL = 11
FFT_PITCH = FFT_N2 + SUBLANE


def _fft_tables():
    n2 = np.arange(FFT_N2)[:, None, None]
    k1 = np.arange(FFT_K1)[None, :, None]
    n1 = np.arange(FFT_N1H)[None, None, :]
    th = 2 * np.pi * (((FFT_N2 * n1 + n2) * k1) % FFT_N) / FFT_N
    f1 = np.zeros((FFT_N2, FFT_R, FFT_N1H))
    f1[:, 0:2 * FFT_K1:2, :] = np.cos(th)
    f1[:, 1:2 * FFT_K1:2, :] = -np.sin(th)
    wgt = np.where((np.arange(FFT_K1) == 0) | (np.arange(FFT_K1) == FFT_N1 // 2), 1.0, 2.0)[None, :, None] / FFT_N
    g1 = np.zeros((FFT_N2, FFT_N1H, FFT_R))
    g1[:, :, 0:2 * FFT_K1:2] = np.transpose(wgt * np.cos(th), (0, 2, 1))
    g1[:, :, 1:2 * FFT_K1:2] = np.transpose(-wgt * np.sin(th), (0, 2, 1))
    ph = 2 * np.pi * ((np.arange(FFT_N2)[:, None] * np.arange(FFT_N2)[None, :]) % FFT_N2) / FFT_N2
    c, s = np.cos(ph), np.sin(ph)
    f2 = np.block([[c, s], [-s, c]])
    sign = np.where(np.arange(FFT_R) // 2 % 2 == 0, 1.0, -1.0)[None, :, None]
    f1ab = np.concatenate([f1, f1 * sign], axis=2)
    return _table_bf16(f1), _table_bf16(f1ab), _table_bf16(f2), _table_bf16(f2.T), _table_bf16(g1)


def _sld(ref, n2, count):
    rows = pl.ds(n2, count, stride=FFT_PITCH)
    return jnp.concatenate([ref[0, rows, :], ref[1, rows, :]], axis=-1)


def _sst(ref, n2, count, val):
    rows = pl.ds(n2, count, stride=FFT_PITCH)
    ref[0, rows, :] = val[:, 0:LANE]
    ref[1, rows, :] = val[:, LANE:2 * LANE]


def _blk_ld(ref, blk, nblk):
    parts = []
    for k in range(nblk):
        rows = pl.ds(pl.multiple_of((blk + k) * FFT_PITCH, SUBLANE), FFT_N2)
        parts.append(jnp.concatenate([ref[0, rows, :], ref[1, rows, :]], axis=-1))
    return parts[0] if nblk == 1 else jnp.concatenate(parts, axis=0)


def _blk_st(ref, blk, nblk, val):
    for k in range(nblk):
        rows = pl.ds(pl.multiple_of((blk + k) * FFT_PITCH, SUBLANE), FFT_N2)
        ref[0, rows, :] = val[k * FFT_N2:(k + 1) * FFT_N2, 0:LANE]
        ref[1, rows, :] = val[k * FFT_N2:(k + 1) * FFT_N2, LANE:2 * LANE]


def _fft_stage1(z_ref, a_ref, f1_ref, n_in=FFT_N1H):
    def body(n2, carry):
        xs = _sld(z_ref, n2, n_in).astype(BF16)
        _sst(a_ref, n2, FFT_R, jnp.dot(f1_ref[n2], xs, preferred_element_type=F32))
        return carry
    lax.fori_loop(0, FFT_N2, body, 0, unroll=FFT_UNROLL)


def _spectrum_kernel(x_ref, inorm_ref, f1_ref, f2_ref, o_ref, z_ref, a_ref):
    for half in range(2):
        for n1 in range(FFT_N1H):
            _blk_st(z_ref, half * FFT_N1H + n1, 1, x_ref[half, n1 * FFT_N2:(n1 + 1) * FFT_N2, :])
    _fft_stage1(z_ref, a_ref, f1_ref, 2 * FFT_N1H)
    inorm = inorm_ref[0]

    def body(k1, carry):
        slab = _blk_ld(a_ref, 2 * k1, 2).astype(BF16)
        o_ref[0, k1] = (jnp.dot(f2_ref[...], slab, preferred_element_type=F32) * inorm).astype(BF16)
        return carry
    lax.fori_loop(0, FFT_K1, body, 0, unroll=FFT_K1_UNROLL)


def hyena_filter_spectra(sig, inv_norm):
    _, L, c = sig.shape
    _, f1ab, f2, _, _ = _fft_tables()
    one = pl.Buffered(1)
    return pl.pallas_call(
        _spectrum_kernel,
        out_shape=jax.ShapeDtypeStruct((HYENA_ORDER, FFT_K1, 2 * FFT_N2, c), BF16),
        grid=(HYENA_ORDER,),
        in_specs=[pl.BlockSpec((2, L, c), lambda i: (i, 0, 0)),
                  pl.BlockSpec((1, 1, c), lambda i: (i, 0, 0)),
                  pl.BlockSpec(f1ab.shape, lambda i: (0, 0, 0), pipeline_mode=one),
                  pl.BlockSpec(f2.shape, lambda i: (0, 0), pipeline_mode=one)],
        out_specs=pl.BlockSpec((1, FFT_K1, 2 * FFT_N2, c), lambda i: (i, 0, 0, 0)),
        scratch_shapes=[pltpu.VMEM((2, 2 * FFT_N1H * FFT_PITCH, LANE), F32),
                        pltpu.VMEM((2, FFT_R * FFT_PITCH, LANE), F32)],
        compiler_params=pltpu.CompilerParams(dimension_semantics=("arbitrary",), vmem_limit_bytes=VMEM_LIMIT_HY),
        name="hyena_filter_spectrum",
    )(sig, inv_norm, f1ab, f2)


def _conv3_rows(src, dst, cw, cb, L):
    rows = 2 * FFT_N2
    nchunk = L // rows

    def body(i, carry):
        r0 = pl.multiple_of(i * rows, rows)
        prev_row = jnp.where(i > 0, src[pl.ds(jnp.maximum(r0 - 1, 0), 1), :], 0.0)
        next_row = jnp.where(i < nchunk - 1, src[pl.ds(jnp.minimum(r0 + rows, L - 1), 1), :], 0.0)
        _blk_st(dst, 2 * i, 2, _conv3(src[pl.ds(r0, rows), :], prev_row, next_row, cw, cb, rows))
        return carry
    lax.fori_loop(0, nchunk, body, 0, unroll=2)


def _hyena_kernel(v_ref, g_ref, h_ref, cwv_ref, cbv_ref, cwg_ref, cbg_ref, skip_ref,
                  f1_ref, f2_ref, f2t_ref, g1_ref, o_ref, z_ref, gc_ref, a_ref):
    order = pl.program_id(1)

    @pl.when(order == 0)
    def _():
        _conv3_rows(v_ref.at[0], z_ref, cwv_ref[...], cbv_ref[...], FFT_L)

    _conv3_rows(g_ref.at[0], gc_ref, cwg_ref[0], cbg_ref[0], FFT_L)
    _fft_stage1(z_ref, a_ref, f1_ref)

    def mid(k1, carry):
        x = jnp.dot(f2_ref[...], _blk_ld(a_ref, 2 * k1, 2).astype(BF16), preferred_element_type=F32)
        h = h_ref[0, k1].astype(F32)
        xr, xi, hr, hi = x[:FFT_N2], x[FFT_N2:], h[:FFT_N2], h[FFT_N2:]
        y = jnp.concatenate([xr * hr - xi * hi, xr * hi + xi * hr], axis=0).astype(BF16)
        _blk_st(a_ref, 2 * k1, 2, jnp.dot(f2t_ref[...], y, preferred_element_type=F32))
        return carry
    lax.fori_loop(0, FFT_K1, mid, 0, unroll=FFT_K1_UNROLL)

    skip = skip_ref[0]

    def last(n2, carry):
        bs = _sld(a_ref, n2, FFT_R).astype(BF16)
        y = jnp.dot(g1_ref[n2], bs, preferred_element_type=F32)
        _sst(z_ref, n2, FFT_N1H, _sld(gc_ref, n2, FFT_N1H) * (y + skip * _sld(z_ref, n2, FFT_N1H)))
        return carry
    lax.fori_loop(0, FFT_N2, last, 0, unroll=FFT_UNROLL)

    @pl.when(order == 1)
    def _():
        for n1 in range(FFT_N1H):
            o_ref[0, n1 * FFT_N2:(n1 + 1) * FFT_N2, :] = _blk_ld(z_ref, n1, 1)


def hyena_long(u, h_spec, cw, cb, skip):
    b, L, _ = u.shape
    c = HYENA_WIDTH
    f1, _, f2, f2t, g1 = _fft_tables()
    one = pl.Buffered(1)
    cw3 = cw.reshape(3, 3, c).transpose(1, 0, 2)
    cb3 = cb.reshape(3, 1, c)
    return pl.pallas_call(
        _hyena_kernel,
        out_shape=jax.ShapeDtypeStruct((b, L, c), F32),
        grid=(b, 2),
        in_specs=[pl.BlockSpec((1, L, c), lambda bi, o: (bi, 0, 0), pipeline_mode=one),
                  pl.BlockSpec((1, L, c), lambda bi, o: (bi, 0, 1 + o)),
                  pl.BlockSpec((1, FFT_K1, 2 * FFT_N2, c), lambda bi, o: (o, 0, 0, 0)),
                  pl.BlockSpec((3, c), lambda bi, o: (0, 0)),
                  pl.BlockSpec((1, c), lambda bi, o: (0, 0)),
                  pl.BlockSpec((1, 3, c), lambda bi, o: (1 + o, 0, 0)),
                  pl.BlockSpec((1, 1, c), lambda bi, o: (1 + o, 0, 0)),
                  pl.BlockSpec((1, 1, c), lambda bi, o: (o, 0, 0)),
                  pl.BlockSpec(f1.shape, lambda bi, o: (0, 0, 0), pipeline_mode=one),
                  pl.BlockSpec(f2.shape, lambda bi, o: (0, 0), pipeline_mode=one),
                  pl.BlockSpec(f2t.shape, lambda bi, o: (0, 0), pipeline_mode=one),
                  pl.BlockSpec(g1.shape, lambda bi, o: (0, 0, 0), pipeline_mode=one)],
        out_specs=pl.BlockSpec((1, L, c), lambda bi, o: (bi, 0, 0)),
        scratch_shapes=[pltpu.VMEM((2, FFT_N1H * FFT_PITCH, LANE), F32),
                        pltpu.VMEM((2, FFT_N1H * FFT_PITCH, LANE), F32),
                        pltpu.VMEM((2, FFT_R * FFT_PITCH, LANE), F32)],
        compiler_params=pltpu.CompilerParams(dimension_semantics=("parallel", "arbitrary"),
                                             vmem_limit_bytes=VMEM_LIMIT_HY),
        name="hyena_long_conv",
    )(u, u, h_spec, cw3[0], cb3[0], cw3, cb3, skip.reshape(2, 1, c), f1, f2, f2t, g1)


HY_FILT = 2 * HYENA_ORDER * HYENA_WIDTH
HY_HALF = HYENA_ORDER * HYENA_WIDTH


def _filter_kernel(wt_ref, wc_ref, ws_ref, b1_ref, w2_ref, b2_ref, w3f_ref, w3b_ref, freq_ref, decay_ref,
                   sig_ref, asum_ref, *, L, rows):
    i = pl.program_id(0)
    fh = w2_ref.shape[0] // 2
    scale = 1.0 / float(max(L - 1, 1))

    @pl.when(i == 0)
    def _():
        asum_ref[...] = jnp.zeros_like(asum_ref)

    n_lane = (i * rows + lax.broadcasted_iota(jnp.int32, (1, rows), 1)).astype(F32)
    sub = lax.broadcasted_iota(jnp.int32, (2 * HYENA_BANDS, 1), 0)
    bands = 1e-4 + (sub % HYENA_BANDS).astype(F32) * ((HYENA_BANDS - 1 - 1e-4) / (HYENA_BANDS - 1))
    ang_t = (2 * math.pi / L) * jnp.where(sub < HYENA_BANDS, n_lane, L - n_lane) * bands
    cos_f, sin_f = jnp.cos(ang_t).T, jnp.sin(ang_t).T

    n = (i * rows + lax.broadcasted_iota(jnp.int32, (rows, 1), 0)).astype(F32)
    tu_f, tu_b = n * scale, (L - n) * scale
    lane = lax.broadcasted_iota(jnp.int32, (rows, 2 * fh), 1)
    freq = freq_ref[...]
    pre = (jnp.where(lane < fh, tu_f, tu_b) * wt_ref[...]
           + jnp.dot(cos_f, wc_ref[...], preferred_element_type=F32, precision=HI)
           - jnp.dot(sin_f, ws_ref[...], preferred_element_type=F32, precision=HI) + b1_ref[...])
    hdn = jnp.sin(freq * pre)
    hdn = jnp.sin(freq * (jnp.dot(hdn, w2_ref[...], preferred_element_type=F32, precision=HI) + b2_ref[...]))
    for side, (w3_ref, tu) in enumerate(((w3f_ref, tu_f), (w3b_ref, tu_b))):
        cols = slice(side * HY_HALF, (side + 1) * HY_HALF)
        val = jnp.dot(hdn, w3_ref[...], preferred_element_type=F32, precision=HI)
        val = val * jnp.exp(-tu * jnp.abs(decay_ref[:, cols]))
        if side == 1:
            val = jnp.where(n > 0, val, 0.0)
        for o in range(HYENA_ORDER):
            sig_ref[2 * o + side] = val[:, o * HYENA_WIDTH:(o + 1) * HYENA_WIDTH]
        asum_ref[side:side + 1, :] += jnp.sum(jnp.abs(val), axis=0, keepdims=True)


def _block_diag2(w):
    z = jnp.zeros_like(w)
    return jnp.concatenate([jnp.concatenate([w, z], axis=1), jnp.concatenate([z, w], axis=1)], axis=0)


def hyena_filter_signals(L, p):
    rows = min(L, TOKEN_TILE)
    fh = p['hy_pos_w1'].shape[1]
    w1, w3 = p['hy_pos_w1'], p['hy_pos_w3']
    twice = lambda v: jnp.tile(v.reshape(1, fh), (1, 2))
    zeros = jnp.zeros((fh, HY_HALF), F32)
    consts = [twice(w1[0]), _block_diag2(w1[1:1 + HYENA_BANDS]), _block_diag2(w1[1 + HYENA_BANDS:]),
              twice(p['hy_pos_b1']), _block_diag2(p['hy_pos_w2']), twice(p['hy_pos_b2']),
              jnp.concatenate([w3[:, :HY_HALF], zeros], axis=0), jnp.concatenate([zeros, w3[:, HY_HALF:]], axis=0),
              twice(p['hy_freq']), p['hy_decay'].reshape(1, HY_FILT)]
    sig, asum = pl.pallas_call(
        functools.partial(_filter_kernel, L=L, rows=rows),
        out_shape=(jax.ShapeDtypeStruct((2 * HYENA_ORDER, L, HYENA_WIDTH), F32),
                   jax.ShapeDtypeStruct((SUBLANE, HY_HALF), F32)),
        grid=(L // rows,),
        in_specs=[pl.BlockSpec(c.shape, lambda i: (0, 0)) for c in consts],
        out_specs=(pl.BlockSpec((2 * HYENA_ORDER, rows, HYENA_WIDTH), lambda i: (0, i, 0)),
                   pl.BlockSpec((SUBLANE, HY_HALF), lambda i: (0, 0))),
        compiler_params=_params("arbitrary"),
        name="hyena_filters",
    )(*consts)
    inv_norm = 1.0 / (asum[0] + asum[1]).reshape(HYENA_ORDER, 1, HYENA_WIDTH)
    return sig, inv_norm


def _rdft_tables(L):
    n_bins = L + 1
    half = -(-n_bins // 16) * 16
    k = np.arange(n_bins)[:, None]
    n = np.arange(L)[None, :]
    th = 2 * np.pi * ((k * n) % (2 * L)) / (2 * L)
    f = np.zeros((2 * half, L))
    f[:n_bins] = np.cos(th)
    f[half:half + n_bins] = -np.sin(th)
    sign = np.where(np.arange(n_bins) % 2 == 0, 1.0, -1.0)[:, None]
    fs = np.zeros_like(f)
    fs[:n_bins] = f[:n_bins] * sign
    fs[half:half + n_bins] = f[half:half + n_bins] * sign
    wgt = np.where((np.arange(n_bins) == 0) | (np.arange(n_bins) == L), 1.0, 2.0)[None, :] / (2 * L)
    g = np.zeros((L, 2 * half))
    g[:, :n_bins] = wgt * np.cos(th).T
    g[:, half:half + n_bins] = -wgt * np.sin(th).T
    return _table_bf16(f), _table_bf16(fs), _table_bf16(g), half


def _hyena_ctx_kernel(u_ref, sig_ref, inorm_ref, cw_ref, cb_ref, skip_ref, f_ref, fs_ref, g_ref, o_ref, *, L, half):
    zero_row = jnp.zeros((1, u_ref.shape[2]), F32)
    uc = _conv3(u_ref[0], zero_row, zero_row, cw_ref[...], cb_ref[...], L)
    z = uc[:, 0:HYENA_WIDTH]
    for o in range(HYENA_ORDER):
        h = (jnp.dot(f_ref[...], sig_ref[2 * o].astype(BF16), preferred_element_type=F32)
             + jnp.dot(fs_ref[...], sig_ref[2 * o + 1].astype(BF16), preferred_element_type=F32)) * inorm_ref[o]
        x = jnp.dot(f_ref[...], z.astype(BF16), preferred_element_type=F32)
        xr, xi, hr, hi = x[:half], x[half:], h[:half], h[half:]
        y = jnp.concatenate([xr * hr - xi * hi, xr * hi + xi * hr], axis=0).astype(BF16)
        conv = jnp.dot(g_ref[...], y, preferred_element_type=F32)
        z = uc[:, (o + 1) * HYENA_WIDTH:(o + 2) * HYENA_WIDTH] * (conv + skip_ref[o] * z)
    o_ref[0] = z


def hyena_short(u, sig, inv_norm, cw, cb, skip):
    b, L, cols = u.shape
    f, fs, g, half = _rdft_tables(L)
    const2 = lambda bi: (0, 0)
    const3 = lambda bi: (0, 0, 0)
    return pl.pallas_call(
        functools.partial(_hyena_ctx_kernel, L=L, half=half),
        out_shape=jax.ShapeDtypeStruct((b, L, HYENA_WIDTH), F32),
        grid=(b,),
        in_specs=[pl.BlockSpec((1, L, cols), lambda bi: (bi, 0, 0)),
                  pl.BlockSpec(sig.shape, const3), pl.BlockSpec(inv_norm.shape, const3),
                  pl.BlockSpec(cw.shape, const2), pl.BlockSpec((1, cols), const2),
                  pl.BlockSpec((HYENA_ORDER, 1, HYENA_WIDTH), const3),
                  pl.BlockSpec(f.shape, const2), pl.BlockSpec(fs.shape, const2), pl.BlockSpec(g.shape, const2)],
        out_specs=pl.BlockSpec((1, L, HYENA_WIDTH), lambda bi: (bi, 0, 0)),
        compiler_params=_params("parallel"),
        name="hyena_context",
    )(u, sig, inv_norm, cw, cb.reshape(1, cols), skip.reshape(HYENA_ORDER, 1, HYENA_WIDTH), f, fs, g)


def _split_bf16(a):
    hi = a.astype(BF16)
    return hi, (a - hi.astype(F32)).astype(BF16)


def _norm_router_kernel(*refs, n_lat, two_src):
    if two_src:
        xl_ref, xc_ref, nw_ref, sh_ref, sc_ref, whi_ref, wlo_ref, br_ref, tri_ref, h_ref, lg_ref, cnt_ref = refs
        x = jnp.where(pl.program_id(0) < n_lat, xl_ref[...], xc_ref[...])
    else:
        xl_ref, nw_ref, sh_ref, sc_ref, whi_ref, wlo_ref, br_ref, tri_ref, h_ref, lg_ref, cnt_ref = refs
        x = xl_ref[...]
    h = _modnorm(x, nw_ref[...], sh_ref[0], sc_ref[0])
    h_ref[...] = h
    h_hi, h_lo = _split_bf16(h)
    lg = (jnp.dot(h_hi, whi_ref[...], preferred_element_type=F32)
          + jnp.dot(h_hi, wlo_ref[...], preferred_element_type=F32)
          + jnp.dot(h_lo, whi_ref[...], preferred_element_type=F32)) + br_ref[...]
    lane = lax.broadcasted_iota(jnp.int32, lg.shape, 1)
    first = lambda hit: jnp.min(jnp.where(hit, lane, ROUTER_COLS), axis=-1, keepdims=True)
    gl = jnp.where(lane < N_GROUPS, lg, -jnp.inf)
    gmax = jnp.max(gl, axis=-1, keepdims=True)
    grp = first(gl == gmax)
    grp_p = 1.0 / jnp.sum(jnp.exp(gl - gmax), axis=-1, keepdims=True)
    lo = N_GROUPS + grp * EXPERTS_PER_GROUP
    el = jnp.where((lane >= lo) & (lane < lo + EXPERTS_PER_GROUP), lg, -jnp.inf)
    e1 = jnp.max(el, axis=-1, keepdims=True)
    i1 = first(el == e1)
    el2 = jnp.where(lane == i1, -jnp.inf, el)
    e2 = jnp.max(el2, axis=-1, keepdims=True)
    i2 = first(el2 == e2)
    r = jnp.exp(e2 - e1)
    w1 = grp_p / (1.0 + r)
    w2 = w1 * r
    @pl.when(pl.program_id(0) == 0)
    def _():
        cnt_ref[...] = jnp.zeros_like(cnt_ref)

    hit1, hit2 = lane == i1, lane == i2
    picks = jnp.where(hit1 | hit2, 1.0, 0.0)
    before = cnt_ref[0:1, :] + jnp.dot(tri_ref[...], picks.astype(BF16), preferred_element_type=F32)
    rank1 = jnp.sum(jnp.where(hit1, before, 0.0), axis=-1, keepdims=True)
    rank2 = jnp.sum(jnp.where(hit2, before, 0.0), axis=-1, keepdims=True)
    cnt_ref[...] = jnp.broadcast_to(cnt_ref[0:1, :] + jnp.sum(picks, axis=0, keepdims=True), cnt_ref.shape)
    vals = [(i1 - N_GROUPS).astype(F32), (i2 - N_GROUPS).astype(F32), w1, w2, rank1, rank2]
    out = jnp.zeros(lg.shape, F32)
    for k, val in enumerate(vals):
        out = jnp.where(lane == k, val, out)
    lg_ref[...] = out


def norm_router(xl, xc, nw, mod_l, mod_c, w_router, b_router, tm=TOKEN_TILE):
    b, L, d = xl.shape
    two_src = xc is not None
    n_lat = b * L // tm
    per_batch = L // tm
    n_ctx = (xc.shape[0] * xc.shape[1]) // tm if two_src else 0
    w_hi, w_lo = _split_bf16(w_router)
    const = lambda i: (0, 0)
    if two_src:
        shift = jnp.concatenate([mod_l[0], mod_c[0]], axis=0)
        scale = jnp.concatenate([mod_l[1], mod_c[1]], axis=0)
        mod_map = lambda i: (jnp.where(i < n_lat, i // per_batch, b), 0, 0)
        srcs = [xl.reshape(b * L, d), xc.reshape(-1, d)]
        src_specs = [pl.BlockSpec((tm, d), lambda i: (jnp.minimum(i, n_lat - 1), 0)),
                     pl.BlockSpec((tm, d), lambda i: (jnp.maximum(i - n_lat, 0), 0))]
    else:
        shift, scale = mod_l
        mod_map = lambda i: (i // per_batch, 0, 0)
        srcs = [xl.reshape(b * L, d)]
        src_specs = [pl.BlockSpec((tm, d), lambda i: (i, 0))]
    n_tok = (n_lat + n_ctx) * tm
    tri = _table_bf16(np.tril(np.ones((tm, tm)), -1))
    return pl.pallas_call(
        functools.partial(_norm_router_kernel, n_lat=n_lat, two_src=two_src),
        out_shape=(jax.ShapeDtypeStruct((n_tok, d), F32), jax.ShapeDtypeStruct((n_tok, ROUTER_COLS), F32),
                   jax.ShapeDtypeStruct((SUBLANE, ROUTER_COLS), F32)),
        grid=(n_lat + n_ctx,),
        in_specs=src_specs + [pl.BlockSpec((1, d), const), pl.BlockSpec((1, 1, d), mod_map),
                              pl.BlockSpec((1, 1, d), mod_map), pl.BlockSpec((d, ROUTER_COLS), const),
                              pl.BlockSpec((d, ROUTER_COLS), const), pl.BlockSpec((1, ROUTER_COLS), const),
                              pl.BlockSpec((tm, tm), const)],
        out_specs=(pl.BlockSpec((tm, d), lambda i: (i, 0)), pl.BlockSpec((tm, ROUTER_COLS), lambda i: (i, 0)),
                   pl.BlockSpec((SUBLANE, ROUTER_COLS), const)),
        compiler_params=_params("arbitrary"),
        name="moe_norm_router",
    )(*srcs, nw.reshape(1, d), shift, scale, w_hi, w_lo, b_router, tri)


def _expert_ffn_kernel(te_ref, x_ref, wg_ref, wu_ref, wd_ref, o_ref):
    x = x_ref[...].astype(BF16)
    g = jnp.dot(x, wg_ref[0].astype(BF16), preferred_element_type=F32)
    u = jnp.dot(x, wu_ref[0].astype(BF16), preferred_element_type=F32)
    o_ref[...] = jnp.dot((_silu(g) * u).astype(BF16), wd_ref[0].astype(BF16), preferred_element_type=F32)


def expert_ffn(x_sorted, tile_expert, w_gate, w_up, w_down, tm):
    r, d = x_sorted.shape
    f = w_gate.shape[-1]
    grid_spec = pltpu.PrefetchScalarGridSpec(
        num_scalar_prefetch=1,
        grid=(r // tm,),
        in_specs=[pl.BlockSpec((tm, d), lambda i, te: (i, 0)),
                  pl.BlockSpec((1, d, f), lambda i, te: (te[i], 0, 0)),
                  pl.BlockSpec((1, d, f), lambda i, te: (te[i], 0, 0)),
                  pl.BlockSpec((1, f, d), lambda i, te: (te[i], 0, 0))],
        out_specs=pl.BlockSpec((tm, d), lambda i, te: (i, 0)),
    )
    return pl.pallas_call(
        _expert_ffn_kernel,
        out_shape=jax.ShapeDtypeStruct((r, d), F32),
        grid_spec=grid_spec,
        compiler_params=_params("arbitrary"),
        name="moe_expert_ffn",
    )(tile_expert, x_sorted, w_gate, w_up, w_down)


def moe_apply(h_tokens, routed, counts_row, w_gate, w_up, w_down, layer, n_lat, tm=EXPERT_TILE):
    t, d = h_tokens.shape
    e_idx, rank = routed[:, 0:2].astype(jnp.int32), routed[:, 4:6].astype(jnp.int32)
    counts = counts_row[0, N_GROUPS:N_GROUPS + N_EXPERTS].astype(jnp.int32)
    n_pairs = 2 * t
    padded = (counts + tm - 1) // tm * tm
    pad_end = jnp.cumsum(padded)
    pad_start = pad_end - padded
    n_rows = n_pairs + N_EXPERTS * tm
    tile_start = jnp.arange(n_rows // tm, dtype=jnp.int32) * tm
    tile_expert = jnp.minimum(jnp.sum((pad_end[None, :] <= tile_start[:, None]).astype(jnp.int32), axis=1),
                              N_EXPERTS - 1)
    onehot = (e_idx[:, :, None] == jnp.arange(N_EXPERTS, dtype=jnp.int32)).astype(jnp.int32)
    pos = jnp.sum(onehot * pad_start, axis=-1) + rank
    j = jnp.arange(tm, dtype=jnp.int32)[None, :]
    fill_key = jnp.where(j < (padded - counts)[:, None], (pad_start + counts)[:, None] + j, n_rows)
    keys = jnp.concatenate([pos.reshape(-1), fill_key.reshape(-1)])
    toks = jnp.concatenate([jnp.arange(n_pairs, dtype=jnp.int32) // 2, jnp.arange(N_EXPERTS * tm, dtype=jnp.int32) % t])
    _, row_token = lax.sort((keys, toks), num_keys=1)
    x_sorted = h_tokens[row_token]
    wg = w_gate.reshape(-1, d, EXPERT_HIDDEN)
    wu = w_up.reshape(-1, d, EXPERT_HIDDEN)
    wd = w_down.reshape(-1, EXPERT_HIDDEN, d)
    y_sorted = expert_ffn(x_sorted, tile_expert + layer * N_EXPERTS, wg, wu, wd, tm)
    lat = (y_sorted[pos[:n_lat, 0]], y_sorted[pos[:n_lat, 1]])
    rest = (y_sorted[pos[n_lat:, 0]], y_sorted[pos[n_lat:, 1]]) if t > n_lat else None
    return lat, rest


def _final_kernel(x_ref, ya_ref, yb_ref, rt_ref, g_ref, w_ref, o_ref):
    x = x_ref[0] + g_ref[0] * _moe_mix(ya_ref[0], yb_ref[0], rt_ref[0])
    o_ref[0] = x * lax.rsqrt(jnp.mean(x * x, axis=-1, keepdims=True) + EPS) * w_ref[...]


def final_norm(x, ya, yb, routed, gate, w, tm=TOKEN_TILE):
    b, L, d = x.shape
    tok = pl.BlockSpec((1, tm, d), lambda bi, i: (bi, i, 0))
    return pl.pallas_call(
        _final_kernel,
        out_shape=jax.ShapeDtypeStruct((b, L, d), F32),
        grid=(b, L // tm),
        in_specs=[tok, tok, tok, pl.BlockSpec((1, tm, ROUTER_COLS), lambda bi, i: (bi, i, 0)),
                  pl.BlockSpec((1, 1, d), _mod_map(gate, b)), pl.BlockSpec((1, d), lambda bi, i: (0, 0))],
        out_specs=tok,
        compiler_params=_params("parallel", "arbitrary"),
        name="final_rmsnorm",
    )(x, ya, yb, routed, gate, w.reshape(1, d))


IN_SIZES = (SSD_WIDTH, SSD_CONV_CH, 2 * SSD_HEADS, HY_COLS, 2 * ML_WIDTH, ML_WIDTH, ML_WIDTH, 4 * ML_HEADS)


def _regroup_kernel(w_ref, o_ref):
    src = dst = 0
    for n in IN_SIZES:
        pad = -n % LANE
        o_ref[0, :, dst:dst + n] = w_ref[0, :, src:src + n].astype(BF16)
        if pad:
            o_ref[0, :, dst + n:dst + n + pad] = jnp.zeros((o_ref.shape[1], pad), BF16)
        src, dst = src + n, dst + n + pad


def regroup_in_weights(w_in):
    depth, d, n_in = w_in.shape
    n_out = sum(n + (-n % LANE) for n in IN_SIZES)
    one = pl.Buffered(1)
    return pl.pallas_call(
        _regroup_kernel,
        out_shape=jax.ShapeDtypeStruct((depth, d, n_out), BF16),
        grid=(depth,),
        in_specs=[pl.BlockSpec((1, d, n_in), lambda i: (i, 0, 0), pipeline_mode=one)],
        out_specs=pl.BlockSpec((1, d, n_out), lambda i: (i, 0, 0)),
        compiler_params=_params("arbitrary"),
        name="regroup_in_weights",
    )(w_in)


def kernel(x, c, ctx, c_ctx, w_mod, b_mod, norm1_w, norm2_w, w_in, w_out, ssd_conv_w, ssd_conv_b, ssd_dt_bias, ssd_a_log, ssd_d, ssd_norm_w, hy_conv_w, hy_conv_b, hy_pos_w1, hy_pos_b1, hy_pos_w2, hy_pos_b2, hy_pos_w3, hy_freq, hy_decay, hy_skip, ml_conv_w, ml_conv_b, ml_gate_b, ml_norm_w, grp_router_w, grp_router_b, exp_router_w, exp_router_b, moe_w_gate, moe_w_up, moe_w_down, final_norm_w):
    layer_params = dict(
        ssd_conv_w=ssd_conv_w, ssd_conv_b=ssd_conv_b, ssd_dt_bias=ssd_dt_bias, ssd_a_log=ssd_a_log,
        ssd_d=ssd_d, ssd_norm_w=ssd_norm_w, hy_conv_w=hy_conv_w, hy_conv_b=hy_conv_b,
        hy_pos_w1=hy_pos_w1, hy_pos_b1=hy_pos_b1, hy_pos_w2=hy_pos_w2, hy_pos_b2=hy_pos_b2,
        hy_pos_w3=hy_pos_w3, hy_freq=hy_freq, hy_decay=hy_decay, hy_skip=hy_skip,
        ml_conv_w=ml_conv_w, ml_conv_b=ml_conv_b, ml_gate_b=ml_gate_b, ml_norm_w=ml_norm_w)
    bsz, seq, d = x.shape
    n_ctx = ctx.shape[1]
    xl, xc = x, ctx
    moe_l = moe_c = None
    c_rows = jnp.concatenate([c, c_ctx[None, :], jnp.zeros((SUBLANE - bsz - 1, d), F32)], axis=0)
    w_in_all = regroup_in_weights(w_in)

    def layer_prep(i):
        p = {name: arr[i] for name, arr in layer_params.items()}
        mod = modulation(c_rows, w_mod, b_mod, i).reshape(SUBLANE, N_MOD, 1, d)
        pad = ROUTER_COLS - N_GROUPS - N_EXPERTS
        return dict(
            p=p, sp=ssd_prepare(p), mp=ml_prepare(p),
            mod_l=[mod[:bsz, k] for k in range(N_MOD)],
            mod_c=[mod[bsz:bsz + 1, k] for k in range(N_MOD)],
            w_out=w_out[i].astype(BF16),
            w_router=jnp.pad(jnp.concatenate([grp_router_w[i], exp_router_w[i]], axis=1), ((0, 0), (0, pad))),
            b_router=jnp.pad(jnp.concatenate([grp_router_b[i], exp_router_b[i]]), (0, pad)).reshape(1, ROUTER_COLS),
            h_spec=hyena_filter_spectra(*hyena_filter_signals(seq, p)),
            ctx_filt=hyena_filter_signals(n_ctx, p) if i < DEPTH - 1 else None)

    prep = layer_prep(0)
    for i in range(DEPTH):
        last = i == DEPTH - 1
        p, sp, mp, mod_l, mod_c, w_out_b = prep['p'], prep['sp'], prep['mp'], prep['mod_l'], prep['mod_c'], prep['w_out']

        uc_ssd, uc_hy, uc_ml, xc = norm_proj(xc, moe_c, norm1_w[i], mod_c[0], mod_c[1], w_in_all, i)
        col_major = i % 2 == 1
        ul_ssd, ul_hy, ul_ml, xl = norm_proj(xl, moe_l, norm1_w[i], mod_l[0], mod_l[1], w_in_all, i, col_major)
        y_ssd = ssd_mixer(ul_ssd, uc_ssd, sp)
        y_ml = ml_mixer(ul_ml, uc_ml, mp)
        yl_hy = hyena_long(ul_hy, prep['h_spec'], p['hy_conv_w'], p['hy_conv_b'], p['hy_skip'])
        xl = out_proj(y_ssd, yl_hy, y_ml, xl, mod_l[2], w_out_b, col_major)
        if not last:
            yc_hy = hyena_short(uc_hy, *prep['ctx_filt'], p['hy_conv_w'], p['hy_conv_b'], p['hy_skip'])
            xc = out_proj(y_ssd, yc_hy, y_ml, xc, mod_c[2], w_out_b, scan_off=seq // n_ctx)
        h_all, routed, counts_row = norm_router(xl, None if last else xc, norm2_w[i], (mod_l[3], mod_l[4]),
                                                (mod_c[3], mod_c[4]), prep['w_router'], prep['b_router'])
        if not last:
            prep = layer_prep(i + 1)
        lat, rest = moe_apply(h_all, routed, counts_row, moe_w_gate, moe_w_up, moe_w_down, i, bsz * seq)
        n_lat = bsz * seq
        moe_l = (lat[0].reshape(bsz, seq, d), lat[1].reshape(bsz, seq, d),
                 routed[:n_lat].reshape(bsz, seq, ROUTER_COLS), mod_l[5])
        if not last:
            moe_c = (rest[0].reshape(bsz, n_ctx, d), rest[1].reshape(bsz, n_ctx, d),
                     routed[n_lat:].reshape(bsz, n_ctx, ROUTER_COLS), mod_c[5])
    return final_norm(xl, *moe_l, final_norm_w)
```

```python
import functools
import math

import jax
import jax.numpy as jnp
import numpy as np
from jax import lax
from jax.experimental import pallas as pl
from jax.experimental.pallas import tpu as pltpu

D_MODEL = 1024
DEPTH = 2
GRID_W = 64
HEAD_DIM = 64
SSD_WIDTH = 384
SSD_HEADS = SSD_WIDTH // HEAD_DIM
SSD_GROUPS = 2
SSD_STATE = 64
HYENA_WIDTH = 256
HYENA_ORDER = 2
HYENA_BANDS = 16
ML_WIDTH = 384
ML_HEADS = ML_WIDTH // HEAD_DIM
N_GROUPS = 4
EXPERTS_PER_GROUP = 8
N_EXPERTS = N_GROUPS * EXPERTS_PER_GROUP
EXPERT_HIDDEN = 256
N_MOD = 6
EPS = 1e-6

LANE = 128
SUBLANE = 8
VMEM_LIMIT = 48 * 1024 * 1024
VMEM_LIMIT_HY = 56 * 1024 * 1024

TOKEN_TILE = 512
EXPERT_TILE = 256
MOD_TILE = 1536
SCAN_CHUNK = 256

SSD_CONV_CH = SSD_WIDTH + 2 * SSD_GROUPS * SSD_STATE
SSD_XBC0 = SSD_WIDTH
SSD_DT0 = SSD_XBC0 + SSD_CONV_CH
SSD_COLS = SSD_DT0 + LANE
HY_COLS = (HYENA_ORDER + 1) * HYENA_WIDTH
ML_V0 = 2 * ML_WIDTH
ML_O0 = ML_V0 + ML_WIDTH
ML_G0 = ML_O0 + ML_WIDTH
ML_COLS = ML_G0 + LANE
ROUTER_COLS = LANE

F32 = jnp.float32
BF16 = jnp.bfloat16
HI = lax.Precision.HIGHEST


def _params(*sem):
    return pltpu.CompilerParams(dimension_semantics=sem, vmem_limit_bytes=VMEM_LIMIT)


def _table_bf16(a):
    return jnp.asarray(np.asarray(a, np.float32)).astype(BF16)


def _silu(x):
    return x * jax.nn.sigmoid(x)


def _softplus(x):
    return jnp.maximum(x, 0.0) + jnp.log(1.0 + jnp.exp(-jnp.abs(x)))


def _log_sigmoid(x):
    return jnp.minimum(x, 0.0) - jnp.log(1.0 + jnp.exp(-jnp.abs(x)))


def _mod_kernel(c_ref, w_ref, b_ref, o_ref):
    o_ref[...] = jnp.dot(_silu(c_ref[...]), w_ref[...], preferred_element_type=F32, precision=HI) + b_ref[...]


def modulation(c_rows, w_mod, b_mod, layer):
    depth, d, n = w_mod.shape
    tn = MOD_TILE
    return pl.pallas_call(
        _mod_kernel,
        out_shape=jax.ShapeDtypeStruct((c_rows.shape[0], n), F32),
        grid=(n // tn,),
        in_specs=[pl.BlockSpec(c_rows.shape, lambda j: (0, 0)),
                  pl.BlockSpec((None, d, tn), lambda j: (layer, 0, j)),
                  pl.BlockSpec((None, 1, tn), lambda j: (layer, 0, j))],
        out_specs=pl.BlockSpec((c_rows.shape[0], tn), lambda j: (0, j)),
        compiler_params=_params("arbitrary"),
        name="adaln_modulation",
    )(c_rows, w_mod, b_mod.reshape(depth, 1, n))


def _modnorm(x, nw, shift, scale):
    y = x * lax.rsqrt(jnp.mean(x * x, axis=-1, keepdims=True) + EPS) * nw
    return y * (1.0 + scale) + shift


def _mod_map(mod, b):
    return (lambda bi, i: (bi, 0, 0)) if mod.shape[0] == b else (lambda bi, i: (0, 0, 0))


def _tok_view(x, col_major):
    b, L, d = x.shape
    return x.reshape(b, L // GRID_W, GRID_W, d) if col_major else x


def _tok_spec(L, d, tm, col_major):
    if col_major:
        assert tm == (L // GRID_W) * SUBLANE
        return pl.BlockSpec((1, L // GRID_W, SUBLANE, d), lambda bi, i: (bi, 0, i, 0))
    return pl.BlockSpec((1, tm, d), lambda bi, i: (bi, i, 0))


def _tok_load(ref, col_major):
    if not col_major:
        return ref[0]
    return jnp.concatenate([ref[0, :, j, :] for j in range(ref.shape[2])], axis=0)


def _tok_store(ref, val, col_major):
    if not col_major:
        ref[0] = val
        return
    rows = ref.shape[1]
    for j in range(ref.shape[2]):
        ref[0, :, j, :] = val[j * rows:(j + 1) * rows]


def _moe_mix(ya, yb, routed):
    return routed[:, 2:3] * ya + routed[:, 3:4] * yb


def _norm_proj_kernel(*refs, col_major, fuse_moe):
    if fuse_moe:
        x_ref, ya_ref, yb_ref, rt_ref, g_ref, nw_ref, sh_ref, sc_ref, w_ref, ssd_ref, hy_ref, ml_ref, xo_ref = refs
    else:
        x_ref, nw_ref, sh_ref, sc_ref, w_ref, ssd_ref, hy_ref, ml_ref = refs
    x = _tok_load(x_ref, col_major)
    if fuse_moe:
        x = x + g_ref[0] * _moe_mix(_tok_load(ya_ref, col_major), _tok_load(yb_ref, col_major),
                                    _tok_load(rt_ref, col_major))
        _tok_store(xo_ref, x, col_major)
    h = _modnorm(x, nw_ref[...], sh_ref[0], sc_ref[0])
    u = jnp.dot(h.astype(BF16), w_ref[...], preferred_element_type=F32)
    ssd_ref[0] = u[:, 0:SSD_COLS]
    hy_ref[0] = u[:, SSD_COLS:SSD_COLS + HY_COLS]
    ml_ref[0] = u[:, SSD_COLS + HY_COLS:]


def norm_proj(x, moe, nw, shift, scale, w_all, layer, col_major=False, tm=TOKEN_TILE):
    b, L, d = x.shape
    n = w_all.shape[2]
    fuse_moe = moe is not None
    tm = (L // GRID_W) * SUBLANE if col_major else min(tm, L)
    tok = _tok_spec(L, d, tm, col_major)
    row = lambda bi, i: (bi, i, 0)
    const2 = lambda bi, i: (0, 0)
    args, in_specs = [_tok_view(x, col_major)], [tok]
    if fuse_moe:
        ya, yb, routed, gate = moe
        args += [_tok_view(ya, col_major), _tok_view(yb, col_major), _tok_view(routed, col_major), gate]
        in_specs += [tok, tok, _tok_spec(L, ROUTER_COLS, tm, col_major), pl.BlockSpec((1, 1, d), _mod_map(gate, b))]
    args += [nw.reshape(1, d), shift, scale, w_all]
    in_specs += [pl.BlockSpec((1, d), const2), pl.BlockSpec((1, 1, d), _mod_map(shift, b)),
                 pl.BlockSpec((1, 1, d), _mod_map(scale, b)),
                 pl.BlockSpec((None, d, n), lambda bi, i: (layer, 0, 0))]
    out_shape = [jax.ShapeDtypeStruct((b, L, SSD_COLS), F32), jax.ShapeDtypeStruct((b, L, HY_COLS), F32),
                 jax.ShapeDtypeStruct((b, L, ML_COLS), F32)]
    out_specs = [pl.BlockSpec((1, tm, SSD_COLS), row), pl.BlockSpec((1, tm, HY_COLS), row),
                 pl.BlockSpec((1, tm, ML_COLS), row)]
    if fuse_moe:
        out_shape.append(jax.ShapeDtypeStruct(args[0].shape, F32))
        out_specs.append(tok)
    outs = pl.pallas_call(
        functools.partial(_norm_proj_kernel, col_major=col_major, fuse_moe=fuse_moe),
        out_shape=tuple(out_shape),
        grid=(b, L // tm),
        in_specs=in_specs,
        out_specs=tuple(out_specs),
        compiler_params=_params("parallel", "arbitrary"),
        name="norm_in_proj",
    )(*args)
    return (*outs[:3], outs[3].reshape(b, L, d) if fuse_moe else x)


def _out_proj_kernel(ys_ref, yh_ref, ym_ref, x_ref, g_ref, w_ref, o_ref, *, col_major):
    y = jnp.concatenate([ys_ref[0], yh_ref[0], ym_ref[0]], axis=-1).astype(BF16)
    r = _tok_load(x_ref, col_major) + g_ref[0] * jnp.dot(y, w_ref[...], preferred_element_type=F32)
    _tok_store(o_ref, r, col_major)


def out_proj(y_ssd, y_hy, y_ml, x, gate, w_bf16, col_major=False, scan_off=0, tm=TOKEN_TILE):
    b, L, d = x.shape
    tm = (L // GRID_W) * SUBLANE if col_major else min(tm, L)
    row = lambda bi, i: (bi, i, 0)
    scan_row = lambda bi, i: (bi, i + scan_off, 0)
    tok = _tok_spec(L, d, tm, col_major)
    xv = _tok_view(x, col_major)
    return pl.pallas_call(
        functools.partial(_out_proj_kernel, col_major=col_major),
        out_shape=jax.ShapeDtypeStruct(xv.shape, F32),
        grid=(b, L // tm),
        in_specs=[pl.BlockSpec((1, tm, SSD_WIDTH), scan_row), pl.BlockSpec((1, tm, HYENA_WIDTH), row),
                  pl.BlockSpec((1, tm, ML_WIDTH), scan_row), tok,
                  pl.BlockSpec((1, 1, d), _mod_map(gate, b)),
                  pl.BlockSpec(w_bf16.shape, lambda bi, i: (0, 0))],
        out_specs=tok,
        compiler_params=_params("parallel", "arbitrary"),
        name="out_proj_residual",
    )(y_ssd, y_hy, y_ml, xv, gate, w_bf16).reshape(b, L, d)


def _conv3(xr, prev_row, next_row, cw, cb, q):
    rid = lax.broadcasted_iota(jnp.int32, (q, 1), 0)
    x_prev = jnp.where(rid == 0, prev_row, pltpu.roll(xr, 1, axis=0))
    x_next = jnp.where(rid == q - 1, next_row, pltpu.roll(xr, q - 1, axis=0))
    return x_prev * cw[0:1] + xr * cw[1:2] + x_next * cw[2:3] + cb


def _masked_scan(mask, x):
    m = mask.astype(BF16)
    hi = x.astype(BF16)
    r1 = x - hi.astype(F32)
    mid = r1.astype(BF16)
    lo = (r1 - mid.astype(F32)).astype(BF16)
    return (jnp.dot(m, hi, preferred_element_type=F32) + jnp.dot(m, mid, preferred_element_type=F32)
            + jnp.dot(m, lo, preferred_element_type=F32))


def _scan_mask(q, direction):
    li = lax.broadcasted_iota(jnp.int32, (q, q), 0)
    si = lax.broadcasted_iota(jnp.int32, (q, q), 1)
    return (si <= li) if direction == 0 else (si >= li)


def _running_max(x, direction, q):
    rid = lax.broadcasted_iota(jnp.int32, (q, 1), 0)
    s = 1
    while s < q:
        if direction == 0:
            x = jnp.where(rid >= s, jnp.maximum(x, pltpu.roll(x, s, axis=0)), x)
        else:
            x = jnp.where(rid < q - s, jnp.maximum(x, pltpu.roll(x, q - s, axis=0)), x)
        s *= 2
    return x


def _scan_specs(L, q, nc, cols, direction):
    hb = q // SUBLANE
    nrb = L // SUBLANE
    order = (lambda j: j) if direction == 0 else (lambda j: nc - 1 - j)
    cidx = lambda j: order(jnp.maximum(j - 1, 0))
    specs = [pl.BlockSpec((1, q, cols), lambda bi, j: (bi, cidx(j), 0)),
             pl.BlockSpec((1, SUBLANE, cols), lambda bi, j: (bi, jnp.maximum(cidx(j) * hb - 1, 0), 0)),
             pl.BlockSpec((1, SUBLANE, cols), lambda bi, j: (bi, jnp.minimum((cidx(j) + 1) * hb, nrb - 1), 0)),
             pl.BlockSpec((1, q, cols), lambda bi, j: (bi, 0, 0))]
    yidx = lambda j: jnp.where(j == 0, nc, cidx(j))
    return specs, yidx


def _scan_inputs(u_ref, prev_ref, next_ref, uc_ref, lo, hi, direction, nc):
    j = pl.program_id(1)
    jm = jnp.maximum(j - 1, 0)
    c = jm if direction == 0 else nc - 1 - jm
    is_ctx = j == 0
    x = jnp.where(is_ctx, uc_ref[0, :, lo:hi], u_ref[0, :, lo:hi])
    prev_row = jnp.where(is_ctx | (c == 0), 0.0, prev_ref[0, SUBLANE - 1:SUBLANE, lo:hi])
    next_row = jnp.where(is_ctx | (c == nc - 1), 0.0, next_ref[0, 0:1, lo:hi])
    return x, prev_row, next_row


def _ssd_kernel(*refs, direction, finalize, q, nc):
    if finalize:
        u_ref, prev_ref, next_ref, uc_ref, yb_ref, cw_ref, cb_ref, dtb_ref, a_ref, d_ref, nw_ref, y_ref, state_ref = refs
    else:
        u_ref, prev_ref, next_ref, uc_ref, cw_ref, cb_ref, dtb_ref, a_ref, y_ref, state_ref = refs

    @pl.when(pl.program_id(1) == 0)
    def _():
        state_ref[...] = jnp.zeros_like(state_ref)

    xr, prev_row, next_row = _scan_inputs(u_ref, prev_ref, next_ref, uc_ref, SSD_XBC0, SSD_DT0, direction, nc)
    xc = _silu(_conv3(xr, prev_row, next_row, cw_ref[...], cb_ref[...], q))
    u_rest, _, _ = _scan_inputs(u_ref, prev_ref, next_ref, uc_ref, SSD_DT0, SSD_COLS, direction, nc)

    dt = _softplus(u_rest + dtb_ref[...])
    mask = _scan_mask(q, direction)
    cum = _masked_scan(mask, dt * a_ref[...])
    cum_t = cum.T
    end = q - 1 if direction == 0 else 0

    ys = []
    for g in range(SSD_GROUPS):
        b0 = SSD_WIDTH + g * SSD_STATE
        c0 = SSD_WIDTH + (SSD_GROUPS + g) * SSD_STATE
        bm_t = xc[:, b0:b0 + SSD_STATE].T
        cm = xc[:, c0:c0 + SSD_STATE].astype(BF16)
        scores = jnp.dot(cm, bm_t.astype(BF16), preferred_element_type=F32)
        for h in range(g * (SSD_HEADS // SSD_GROUPS), (g + 1) * (SSD_HEADS // SSD_GROUPS)):
            hl = direction * SSD_HEADS + h
            col = cum[:, hl:hl + 1]
            row = cum_t[hl:hl + 1, :]
            seg = jnp.exp(jnp.where(mask, col - row, -jnp.inf))
            xdt = (xc[:, h * HEAD_DIM:(h + 1) * HEAD_DIM] * dt[:, hl:hl + 1]).astype(BF16)
            y = jnp.dot((scores * seg).astype(BF16), xdt, preferred_element_type=F32)
            st = state_ref[h]
            y = y + jnp.dot(cm, st.astype(BF16), preferred_element_type=F32) * jnp.exp(col)
            tot = cum[end:end + 1, hl:hl + 1]
            upd = jnp.dot((bm_t * jnp.exp(tot - row)).astype(BF16), xdt, preferred_element_type=F32)
            state_ref[h] = st * jnp.exp(tot) + upd
            ys.append(y)
    y_all = jnp.concatenate(ys, axis=-1)
    if finalize:
        z, _, _ = _scan_inputs(u_ref, prev_ref, next_ref, uc_ref, 0, SSD_WIDTH, direction, nc)
        t = (y_all + yb_ref[0] + xc[:, 0:SSD_WIDTH] * d_ref[...]) * _silu(z)
        y_all = t * lax.rsqrt(jnp.mean(t * t, axis=-1, keepdims=True) + EPS) * nw_ref[...]
    y_ref[0] = y_all


def ssd_pass(u, u_ctx, y_other, sp, direction, q):
    b, L, _ = u.shape
    nc = L // q
    finalize = y_other is not None
    in_specs, yidx = _scan_specs(L, q, nc, SSD_COLS, direction)
    const2 = lambda bi, j: (0, 0)
    y_spec = pl.BlockSpec((1, q, SSD_WIDTH), lambda bi, j: (bi, yidx(j), 0))
    args = [u, u, u, u_ctx]
    if finalize:
        in_specs.append(y_spec)
        args.append(y_other)
    consts = [sp['cw'], sp['cb'], sp['dtb'], sp['a']] + ([sp['d'], sp['nw']] if finalize else [])
    in_specs += [pl.BlockSpec(t.shape, const2) for t in consts]
    args += consts
    return pl.pallas_call(
        functools.partial(_ssd_kernel, direction=direction, finalize=finalize, q=q, nc=nc),
        out_shape=jax.ShapeDtypeStruct((b, L + q, SSD_WIDTH), F32),
        grid=(b, nc + 1),
        in_specs=in_specs,
        out_specs=y_spec,
        scratch_shapes=[pltpu.VMEM((SSD_HEADS, HEAD_DIM, SSD_STATE), F32)],
        compiler_params=_params("parallel", "arbitrary"),
        name="ssd_scan_%s" % ("fwd" if direction == 0 else "bwd"),
    )(*args)


def ssd_prepare(p):
    pad = lambda v: jnp.pad(v.reshape(1, -1), ((0, 0), (0, LANE - v.size)))
    return dict(cw=p['ssd_conv_w'], cb=p['ssd_conv_b'].reshape(1, -1),
                dtb=pad(p['ssd_dt_bias']), a=pad(-jnp.exp(p['ssd_a_log'])),
                d=jnp.repeat(p['ssd_d'], HEAD_DIM).reshape(1, -1), nw=p['ssd_norm_w'].reshape(1, -1))


def ssd_mixer(u, u_ctx, sp, q=SCAN_CHUNK):
    assert u_ctx.shape[1] == q
    return ssd_pass(u, u_ctx, ssd_pass(u, u_ctx, None, sp, 1, q), sp, 0, q)


def _ml_kernel(*refs, direction, finalize, q, nc):
    if finalize:
        u_ref, prev_ref, next_ref, uc_ref, hb_ref, cw_ref, cb_ref, gb_ref, nw_ref, pool_ref, y_ref, s_ref, m_ref = refs
    else:
        u_ref, prev_ref, next_ref, uc_ref, cw_ref, cb_ref, gb_ref, y_ref, s_ref, m_ref = refs

    @pl.when(pl.program_id(1) == 0)
    def _():
        s_ref[...] = jnp.zeros_like(s_ref)
        m_ref[...] = jnp.zeros_like(m_ref)

    xr, prev_row, next_row = _scan_inputs(u_ref, prev_ref, next_ref, uc_ref, 0, ML_V0, direction, nc)
    qk = _silu(_conv3(xr, prev_row, next_row, cw_ref[...], cb_ref[...], q))
    rest, _, _ = _scan_inputs(u_ref, prev_ref, next_ref, uc_ref, ML_V0, ML_COLS, direction, nc)
    v = rest[:, 0:ML_WIDTH]

    gb = rest[:, ML_G0 - ML_V0:] + gb_ref[...]
    mask = _scan_mask(q, direction)
    cum = jnp.dot(mask.astype(F32), _log_sigmoid(gb), preferred_element_type=F32, precision=HI)
    ig = pltpu.roll(gb, ML_HEADS, axis=1)
    end = q - 1 if direction == 0 else 0
    m_prev = m_ref[0:1, :]
    tot = cum[end:end + 1, :]
    w_end = tot - cum + ig
    m_loc = jnp.max(w_end, axis=0, keepdims=True)
    e_end = jnp.exp(w_end - m_loc)
    m_new = jnp.maximum(tot + m_prev, m_loc)
    a_prev = jnp.exp(tot + m_prev - m_new)
    a_loc = jnp.exp(m_loc - m_new)
    inter = cum + m_prev
    rel = ig - cum
    m_t = jnp.maximum(inter, cum + _running_max(rel, direction, q))
    col_a = cum - m_t
    a_inter = jnp.exp(inter - m_t)
    floor = jnp.exp(-m_t)
    rel_t = rel.T
    e_end_t = e_end.T
    k_t = (qk[:, ML_WIDTH:2 * ML_WIDTH] * (HEAD_DIM ** -0.5)).T
    one_col = (lax.broadcasted_iota(jnp.int32, (q, HEAD_DIM), 1) == 0).astype(F32)

    ys = []
    for h in range(ML_HEADS):
        fl = direction * 2 * ML_HEADS + ML_HEADS + h
        qh = qk[:, h * HEAD_DIM:(h + 1) * HEAD_DIM].astype(BF16)
        kh_t = k_t[h * HEAD_DIM:(h + 1) * HEAD_DIM, :]
        v_ext = jnp.concatenate([v[:, h * HEAD_DIM:(h + 1) * HEAD_DIM], one_col], axis=-1).astype(BF16)
        pw = jnp.exp(jnp.where(mask, col_a[:, fl:fl + 1] + rel_t[fl:fl + 1, :], -jnp.inf))
        scores = jnp.dot(qh, kh_t.astype(BF16), preferred_element_type=F32)
        nd = jnp.dot((scores * pw).astype(BF16), v_ext, preferred_element_type=F32)
        st = s_ref[h]
        nd = nd + a_inter[:, fl:fl + 1] * jnp.dot(qh, st.astype(BF16), preferred_element_type=F32)
        den = nd[:, HEAD_DIM:HEAD_DIM + 1]
        ys.append(nd[:, 0:HEAD_DIM] / jnp.maximum(jnp.abs(den), floor[:, fl:fl + 1]))
        upd = jnp.dot((kh_t * e_end_t[fl:fl + 1, :]).astype(BF16), v_ext, preferred_element_type=F32)
        s_ref[h] = a_prev[:, fl:fl + 1] * st + a_loc[:, fl:fl + 1] * upd
    m_ref[...] = jnp.broadcast_to(m_new, m_ref.shape)
    y_all = jnp.concatenate(ys, axis=-1)
    if finalize:
        hs = y_all + hb_ref[0]
        hc = hs - jnp.dot(hs.astype(BF16), pool_ref[...], preferred_element_type=F32)
        var = jnp.dot((hc * hc).astype(BF16), pool_ref[...], preferred_element_type=F32)
        y_all = hc * lax.rsqrt(var + EPS) * nw_ref[...] * jax.nn.sigmoid(rest[:, ML_O0 - ML_V0:ML_G0 - ML_V0])
    y_ref[0] = y_all


def ml_pass(u, u_ctx, h_other, mp, direction, q):
    b, L, _ = u.shape
    nc = L // q
    finalize = h_other is not None
    in_specs, yidx = _scan_specs(L, q, nc, ML_COLS, direction)
    const2 = lambda bi, j: (0, 0)
    y_spec = pl.BlockSpec((1, q, ML_WIDTH), lambda bi, j: (bi, yidx(j), 0))
    args = [u, u, u, u_ctx]
    if finalize:
        in_specs.append(y_spec)
        args.append(h_other)
    consts = [mp['cw'], mp['cb'], mp['gb']] + ([mp['nw'], mp['pool']] if finalize else [])
    in_specs += [pl.BlockSpec(t.shape, const2) for t in consts]
    args += consts
    return pl.pallas_call(
        functools.partial(_ml_kernel, direction=direction, finalize=finalize, q=q, nc=nc),
        out_shape=jax.ShapeDtypeStruct((b, L + q, ML_WIDTH), F32),
        grid=(b, nc + 1),
        in_specs=in_specs,
        out_specs=y_spec,
        scratch_shapes=[pltpu.VMEM((ML_HEADS, HEAD_DIM, LANE), F32), pltpu.VMEM((SUBLANE, LANE), F32)],
        compiler_params=_params("parallel", "arbitrary"),
        name="mlstm_scan_%s" % ("fwd" if direction == 0 else "bwd"),
    )(*args)


def ml_prepare(p):
    gb = p['ml_gate_b'].reshape(1, -1)
    head = np.arange(ML_WIDTH) // HEAD_DIM
    pool = _table_bf16((head[:, None] == head[None, :]) / HEAD_DIM)
    return dict(cw=p['ml_conv_w'], cb=p['ml_conv_b'].reshape(1, -1),
                gb=jnp.pad(gb, ((0, 0), (0, LANE - gb.shape[1]))), nw=p['ml_norm_w'].reshape(1, -1), pool=pool)


def ml_mixer(u, u_ctx, mp, q=SCAN_CHUNK):
    assert u_ctx.shape[1] == q
    return ml_pass(u, u_ctx, ml_pass(u, u_ctx, None, mp, 1, q), mp, 0, q)


FFT_L = 4096
FFT_N = 2 * FFT_L
FFT_N2 = 128
FFT_N1 = FFT_N // FFT_N2
FFT_N1H = FFT_L // FFT_N2
FFT_K1 = FFT_N1 // 2 + 1
FFT_R = 80
FFT_UNROLL = 16
FFT_K1_UNROLL = 11
FFT_PITCH = FFT_N2 + SUBLANE


def _fft_tables():
    n2 = np.arange(FFT_N2)[:, None, None]
    k1 = np.arange(FFT_K1)[None, :, None]
    n1 = np.arange(FFT_N1H)[None, None, :]
    th = 2 * np.pi * (((FFT_N2 * n1 + n2) * k1) % FFT_N) / FFT_N
    f1 = np.zeros((FFT_N2, FFT_R, FFT_N1H))
    f1[:, 0:2 * FFT_K1:2, :] = np.cos(th)
    f1[:, 1:2 * FFT_K1:2, :] = -np.sin(th)
    wgt = np.where((np.arange(FFT_K1) == 0) | (np.arange(FFT_K1) == FFT_N1 // 2), 1.0, 2.0)[None, :, None] / FFT_N
    g1 = np.zeros((FFT_N2, FFT_N1H, FFT_R))
    g1[:, :, 0:2 * FFT_K1:2] = np.transpose(wgt * np.cos(th), (0, 2, 1))
    g1[:, :, 1:2 * FFT_K1:2] = np.transpose(-wgt * np.sin(th), (0, 2, 1))
    ph = 2 * np.pi * ((np.arange(FFT_N2)[:, None] * np.arange(FFT_N2)[None, :]) % FFT_N2) / FFT_N2
    c, s = np.cos(ph), np.sin(ph)
    f2 = np.block([[c, s], [-s, c]])
    sign = np.where(np.arange(FFT_R) // 2 % 2 == 0, 1.0, -1.0)[None, :, None]
    f1ab = np.concatenate([f1, f1 * sign], axis=2)
    return _table_bf16(f1), _table_bf16(f1ab), _table_bf16(f2), _table_bf16(f2.T), _table_bf16(g1)


def _sld(ref, n2, count):
    rows = pl.ds(n2, count, stride=FFT_PITCH)
    return jnp.concatenate([ref[0, rows, :], ref[1, rows, :]], axis=-1)


def _sst(ref, n2, count, val):
    rows = pl.ds(n2, count, stride=FFT_PITCH)
    ref[0, rows, :] = val[:, 0:LANE]
    ref[1, rows, :] = val[:, LANE:2 * LANE]


def _blk_ld(ref, blk, nblk):
    parts = []
    for k in range(nblk):
        rows = pl.ds(pl.multiple_of((blk + k) * FFT_PITCH, SUBLANE), FFT_N2)
        parts.append(jnp.concatenate([ref[0, rows, :], ref[1, rows, :]], axis=-1))
    return parts[0] if nblk == 1 else jnp.concatenate(parts, axis=0)


def _blk_st(ref, blk, nblk, val):
    for k in range(nblk):
        rows = pl.ds(pl.multiple_of((blk + k) * FFT_PITCH, SUBLANE), FFT_N2)
        ref[0, rows, :] = val[k * FFT_N2:(k + 1) * FFT_N2, 0:LANE]
        ref[1, rows, :] = val[k * FFT_N2:(k + 1) * FFT_N2, LANE:2 * LANE]


def _fft_stage1(z_ref, a_ref, f1_ref, n_in=FFT_N1H):
    def body(n2, carry):
        xs = _sld(z_ref, n2, n_in).astype(BF16)
        _sst(a_ref, n2, FFT_R, jnp.dot(f1_ref[n2], xs, preferred_element_type=F32))
        return carry
    lax.fori_loop(0, FFT_N2, body, 0, unroll=FFT_UNROLL)


def _spectrum_kernel(x_ref, inorm_ref, f1_ref, f2_ref, o_ref, z_ref, a_ref):
    for half in range(2):
        for n1 in range(FFT_N1H):
            _blk_st(z_ref, half * FFT_N1H + n1, 1, x_ref[half, n1 * FFT_N2:(n1 + 1) * FFT_N2, :])
    _fft_stage1(z_ref, a_ref, f1_ref, 2 * FFT_N1H)
    inorm = inorm_ref[0]

    def body(k1, carry):
        slab = _blk_ld(a_ref, 2 * k1, 2).astype(BF16)
        o_ref[0, k1] = (jnp.dot(f2_ref[...], slab, preferred_element_type=F32) * inorm).astype(BF16)
        return carry
    lax.fori_loop(0, FFT_K1, body, 0, unroll=FFT_K1_UNROLL)


def hyena_filter_spectra(sig, inv_norm):
    _, L, c = sig.shape
    _, f1ab, f2, _, _ = _fft_tables()
    one = pl.Buffered(1)
    return pl.pallas_call(
        _spectrum_kernel,
        out_shape=jax.ShapeDtypeStruct((HYENA_ORDER, FFT_K1, 2 * FFT_N2, c), BF16),
        grid=(HYENA_ORDER,),
        in_specs=[pl.BlockSpec((2, L, c), lambda i: (i, 0, 0)),
                  pl.BlockSpec((1, 1, c), lambda i: (i, 0, 0)),
                  pl.BlockSpec(f1ab.shape, lambda i: (0, 0, 0), pipeline_mode=one),
                  pl.BlockSpec(f2.shape, lambda i: (0, 0), pipeline_mode=one)],
        out_specs=pl.BlockSpec((1, FFT_K1, 2 * FFT_N2, c), lambda i: (i, 0, 0, 0)),
        scratch_shapes=[pltpu.VMEM((2, 2 * FFT_N1H * FFT_PITCH, LANE), F32),
                        pltpu.VMEM((2, FFT_R * FFT_PITCH, LANE), F32)],
        compiler_params=pltpu.CompilerParams(dimension_semantics=("arbitrary",), vmem_limit_bytes=VMEM_LIMIT_HY),
        name="hyena_filter_spectrum",
    )(sig, inv_norm, f1ab, f2)


def _conv3_rows(src, dst, cw, cb, L):
    rows = 2 * FFT_N2
    nchunk = L // rows

    def body(i, carry):
        r0 = pl.multiple_of(i * rows, rows)
        prev_row = jnp.where(i > 0, src[pl.ds(jnp.maximum(r0 - 1, 0), 1), :], 0.0)
        next_row = jnp.where(i < nchunk - 1, src[pl.ds(jnp.minimum(r0 + rows, L - 1), 1), :], 0.0)
        _blk_st(dst, 2 * i, 2, _conv3(src[pl.ds(r0, rows), :], prev_row, next_row, cw, cb, rows))
        return carry
    lax.fori_loop(0, nchunk, body, 0, unroll=2)


def _hyena_kernel(v_ref, g_ref, h_ref, cwv_ref, cbv_ref, cwg_ref, cbg_ref, skip_ref,
                  f1_ref, f2_ref, f2t_ref, g1_ref, o_ref, z_ref, gc_ref, a_ref):
    order = pl.program_id(1)

    @pl.when(order == 0)
    def _():
        _conv3_rows(v_ref.at[0], z_ref, cwv_ref[...], cbv_ref[...], FFT_L)

    _conv3_rows(g_ref.at[0], gc_ref, cwg_ref[0], cbg_ref[0], FFT_L)
    _fft_stage1(z_ref, a_ref, f1_ref)

    def mid(k1, carry):
        x = jnp.dot(f2_ref[...], _blk_ld(a_ref, 2 * k1, 2).astype(BF16), preferred_element_type=F32)
        h = h_ref[0, k1].astype(F32)
        xr, xi, hr, hi = x[:FFT_N2], x[FFT_N2:], h[:FFT_N2], h[FFT_N2:]
        y = jnp.concatenate([xr * hr - xi * hi, xr * hi + xi * hr], axis=0).astype(BF16)
        _blk_st(a_ref, 2 * k1, 2, jnp.dot(f2t_ref[...], y, preferred_element_type=F32))
        return carry
    lax.fori_loop(0, FFT_K1, mid, 0, unroll=FFT_K1_UNROLL)

    skip = skip_ref[0]

    def last(n2, carry):
        bs = _sld(a_ref, n2, FFT_R).astype(BF16)
        y = jnp.dot(g1_ref[n2], bs, preferred_element_type=F32)
        _sst(z_ref, n2, FFT_N1H, _sld(gc_ref, n2, FFT_N1H) * (y + skip * _sld(z_ref, n2, FFT_N1H)))
        return carry
    lax.fori_loop(0, FFT_N2, last, 0, unroll=FFT_UNROLL)

    @pl.when(order == 1)
    def _():
        for n1 in range(FFT_N1H):
            o_ref[0, n1 * FFT_N2:(n1 + 1) * FFT_N2, :] = _blk_ld(z_ref, n1, 1)


def hyena_long(u, h_spec, cw, cb, skip):
    b, L, _ = u.shape
    c = HYENA_WIDTH
    f1, _, f2, f2t, g1 = _fft_tables()
    one = pl.Buffered(1)
    cw3 = cw.reshape(3, 3, c).transpose(1, 0, 2)
    cb3 = cb.reshape(3, 1, c)
    return pl.pallas_call(
        _hyena_kernel,
        out_shape=jax.ShapeDtypeStruct((b, L, c), F32),
        grid=(b, 2),
        in_specs=[pl.BlockSpec((1, L, c), lambda bi, o: (bi, 0, 0), pipeline_mode=one),
                  pl.BlockSpec((1, L, c), lambda bi, o: (bi, 0, 1 + o)),
                  pl.BlockSpec((1, FFT_K1, 2 * FFT_N2, c), lambda bi, o: (o, 0, 0, 0)),
                  pl.BlockSpec((3, c), lambda bi, o: (0, 0)),
                  pl.BlockSpec((1, c), lambda bi, o: (0, 0)),
                  pl.BlockSpec((1, 3, c), lambda bi, o: (1 + o, 0, 0)),
                  pl.BlockSpec((1, 1, c), lambda bi, o: (1 + o, 0, 0)),
                  pl.BlockSpec((1, 1, c), lambda bi, o: (o, 0, 0)),
                  pl.BlockSpec(f1.shape, lambda bi, o: (0, 0, 0), pipeline_mode=one),
                  pl.BlockSpec(f2.shape, lambda bi, o: (0, 0), pipeline_mode=one),
                  pl.BlockSpec(f2t.shape, lambda bi, o: (0, 0), pipeline_mode=one),
                  pl.BlockSpec(g1.shape, lambda bi, o: (0, 0, 0), pipeline_mode=one)],
        out_specs=pl.BlockSpec((1, L, c), lambda bi, o: (bi, 0, 0)),
        scratch_shapes=[pltpu.VMEM((2, FFT_N1H * FFT_PITCH, LANE), F32),
                        pltpu.VMEM((2, FFT_N1H * FFT_PITCH, LANE), F32),
                        pltpu.VMEM((2, FFT_R * FFT_PITCH, LANE), F32)],
        compiler_params=pltpu.CompilerParams(dimension_semantics=("parallel", "arbitrary"),
                                             vmem_limit_bytes=VMEM_LIMIT_HY),
        name="hyena_long_conv",
    )(u, u, h_spec, cw3[0], cb3[0], cw3, cb3, skip.reshape(2, 1, c), f1, f2, f2t, g1)


HY_FILT = 2 * HYENA_ORDER * HYENA_WIDTH
HY_HALF = HYENA_ORDER * HYENA_WIDTH


def _filter_kernel(wt_ref, wc_ref, ws_ref, b1_ref, w2_ref, b2_ref, w3f_ref, w3b_ref, freq_ref, decay_ref,
                   sig_ref, asum_ref, *, L, rows):
    i = pl.program_id(0)
    fh = w2_ref.shape[0] // 2
    scale = 1.0 / float(max(L - 1, 1))

    @pl.when(i == 0)
    def _():
        asum_ref[...] = jnp.zeros_like(asum_ref)

    n_lane = (i * rows + lax.broadcasted_iota(jnp.int32, (1, rows), 1)).astype(F32)
    sub = lax.broadcasted_iota(jnp.int32, (2 * HYENA_BANDS, 1), 0)
    bands = 1e-4 + (sub % HYENA_BANDS).astype(F32) * ((HYENA_BANDS - 1 - 1e-4) / (HYENA_BANDS - 1))
    ang_t = (2 * math.pi / L) * jnp.where(sub < HYENA_BANDS, n_lane, L - n_lane) * bands
    cos_f, sin_f = jnp.cos(ang_t).T, jnp.sin(ang_t).T

    n = (i * rows + lax.broadcasted_iota(jnp.int32, (rows, 1), 0)).astype(F32)
    tu_f, tu_b = n * scale, (L - n) * scale
    lane = lax.broadcasted_iota(jnp.int32, (rows, 2 * fh), 1)
    freq = freq_ref[...]
    pre = (jnp.where(lane < fh, tu_f, tu_b) * wt_ref[...]
           + _dot3(cos_f, wc_ref[...]) - _dot3(sin_f, ws_ref[...]) + b1_ref[...])
    hdn = jnp.sin(freq * pre)
    hdn = jnp.sin(freq * (_dot3(hdn, w2_ref[...]) + b2_ref[...]))
    for side, (w3_ref, tu) in enumerate(((w3f_ref, tu_f), (w3b_ref, tu_b))):
        cols = slice(side * HY_HALF, (side + 1) * HY_HALF)
        val = _dot3(hdn, w3_ref[...])
        val = val * jnp.exp(-tu * jnp.abs(decay_ref[:, cols]))
        if side == 1:
            val = jnp.where(n > 0, val, 0.0)
        for o in range(HYENA_ORDER):
            sig_ref[2 * o + side] = val[:, o * HYENA_WIDTH:(o + 1) * HYENA_WIDTH]
        asum_ref[side:side + 1, :] += jnp.sum(jnp.abs(val), axis=0, keepdims=True)


def _block_diag2(w):
    z = jnp.zeros_like(w)
    return jnp.concatenate([jnp.concatenate([w, z], axis=1), jnp.concatenate([z, w], axis=1)], axis=0)


def hyena_filter_signals(L, p):
    rows = min(L, TOKEN_TILE)
    fh = p['hy_pos_w1'].shape[1]
    w1, w3 = p['hy_pos_w1'], p['hy_pos_w3']
    twice = lambda v: jnp.tile(v.reshape(1, fh), (1, 2))
    zeros = jnp.zeros((fh, HY_HALF), F32)
    consts = [twice(w1[0]), _block_diag2(w1[1:1 + HYENA_BANDS]), _block_diag2(w1[1 + HYENA_BANDS:]),
              twice(p['hy_pos_b1']), _block_diag2(p['hy_pos_w2']), twice(p['hy_pos_b2']),
              jnp.concatenate([w3[:, :HY_HALF], zeros], axis=0), jnp.concatenate([zeros, w3[:, HY_HALF:]], axis=0),
              twice(p['hy_freq']), p['hy_decay'].reshape(1, HY_FILT)]
    sig, asum = pl.pallas_call(
        functools.partial(_filter_kernel, L=L, rows=rows),
        out_shape=(jax.ShapeDtypeStruct((2 * HYENA_ORDER, L, HYENA_WIDTH), F32),
                   jax.ShapeDtypeStruct((SUBLANE, HY_HALF), F32)),
        grid=(L // rows,),
        in_specs=[pl.BlockSpec(c.shape, lambda i: (0, 0)) for c in consts],
        out_specs=(pl.BlockSpec((2 * HYENA_ORDER, rows, HYENA_WIDTH), lambda i: (0, i, 0)),
                   pl.BlockSpec((SUBLANE, HY_HALF), lambda i: (0, 0))),
        compiler_params=_params("arbitrary"),
        name="hyena_filters",
    )(*consts)
    inv_norm = 1.0 / (asum[0] + asum[1]).reshape(HYENA_ORDER, 1, HYENA_WIDTH)
    return sig, inv_norm


def _rdft_tables(L):
    n_bins = L + 1
    half = -(-n_bins // 16) * 16
    k = np.arange(n_bins)[:, None]
    n = np.arange(L)[None, :]
    th = 2 * np.pi * ((k * n) % (2 * L)) / (2 * L)
    f = np.zeros((2 * half, L))
    f[:n_bins] = np.cos(th)
    f[half:half + n_bins] = -np.sin(th)
    sign = np.where(np.arange(n_bins) % 2 == 0, 1.0, -1.0)[:, None]
    fs = np.zeros_like(f)
    fs[:n_bins] = f[:n_bins] * sign
    fs[half:half + n_bins] = f[half:half + n_bins] * sign
    wgt = np.where((np.arange(n_bins) == 0) | (np.arange(n_bins) == L), 1.0, 2.0)[None, :] / (2 * L)
    g = np.zeros((L, 2 * half))
    g[:, :n_bins] = wgt * np.cos(th).T
    g[:, half:half + n_bins] = -wgt * np.sin(th).T
    return _table_bf16(f), _table_bf16(fs), _table_bf16(g), half


def _hyena_ctx_kernel(u_ref, sig_ref, inorm_ref, cw_ref, cb_ref, skip_ref, f_ref, fs_ref, g_ref, o_ref, *, L, half):
    zero_row = jnp.zeros((1, u_ref.shape[2]), F32)
    uc = _conv3(u_ref[0], zero_row, zero_row, cw_ref[...], cb_ref[...], L)
    z = uc[:, 0:HYENA_WIDTH]
    for o in range(HYENA_ORDER):
        h = (jnp.dot(f_ref[...], sig_ref[2 * o].astype(BF16), preferred_element_type=F32)
             + jnp.dot(fs_ref[...], sig_ref[2 * o + 1].astype(BF16), preferred_element_type=F32)) * inorm_ref[o]
        x = jnp.dot(f_ref[...], z.astype(BF16), preferred_element_type=F32)
        xr, xi, hr, hi = x[:half], x[half:], h[:half], h[half:]
        y = jnp.concatenate([xr * hr - xi * hi, xr * hi + xi * hr], axis=0).astype(BF16)
        conv = jnp.dot(g_ref[...], y, preferred_element_type=F32)
        z = uc[:, (o + 1) * HYENA_WIDTH:(o + 2) * HYENA_WIDTH] * (conv + skip_ref[o] * z)
    o_ref[0] = z


def hyena_short(u, sig, inv_norm, cw, cb, skip):
    b, L, cols = u.shape
    f, fs, g, half = _rdft_tables(L)
    const2 = lambda bi: (0, 0)
    const3 = lambda bi: (0, 0, 0)
    return pl.pallas_call(
        functools.partial(_hyena_ctx_kernel, L=L, half=half),
        out_shape=jax.ShapeDtypeStruct((b, L, HYENA_WIDTH), F32),
        grid=(b,),
        in_specs=[pl.BlockSpec((1, L, cols), lambda bi: (bi, 0, 0)),
                  pl.BlockSpec(sig.shape, const3), pl.BlockSpec(inv_norm.shape, const3),
                  pl.BlockSpec(cw.shape, const2), pl.BlockSpec((1, cols), const2),
                  pl.BlockSpec((HYENA_ORDER, 1, HYENA_WIDTH), const3),
                  pl.BlockSpec(f.shape, const2), pl.BlockSpec(fs.shape, const2), pl.BlockSpec(g.shape, const2)],
        out_specs=pl.BlockSpec((1, L, HYENA_WIDTH), lambda bi: (bi, 0, 0)),
        compiler_params=_params("parallel"),
        name="hyena_context",
    )(u, sig, inv_norm, cw, cb.reshape(1, cols), skip.reshape(HYENA_ORDER, 1, HYENA_WIDTH), f, fs, g)


def _split_bf16(a):
    hi = a.astype(BF16)
    return hi, (a - hi.astype(F32)).astype(BF16)


def _dot3(a, b):
    a_hi, a_lo = _split_bf16(a)
    b_hi, b_lo = _split_bf16(b)
    return (jnp.dot(a_hi, b_hi, preferred_element_type=F32) + jnp.dot(a_hi, b_lo, preferred_element_type=F32)
            + jnp.dot(a_lo, b_hi, preferred_element_type=F32))


def _norm_router_kernel(*refs, n_lat, two_src):
    if two_src:
        xl_ref, xc_ref, nw_ref, sh_ref, sc_ref, whi_ref, wlo_ref, br_ref, tri_ref, h_ref, lg_ref, cnt_ref = refs
        x = jnp.where(pl.program_id(0) < n_lat, xl_ref[...], xc_ref[...])
    else:
        xl_ref, nw_ref, sh_ref, sc_ref, whi_ref, wlo_ref, br_ref, tri_ref, h_ref, lg_ref, cnt_ref = refs
        x = xl_ref[...]
    h = _modnorm(x, nw_ref[...], sh_ref[0], sc_ref[0])
    h_ref[...] = h
    h_hi, h_lo = _split_bf16(h)
    lg = (jnp.dot(h_hi, whi_ref[...], preferred_element_type=F32)
          + jnp.dot(h_hi, wlo_ref[...], preferred_element_type=F32)
          + jnp.dot(h_lo, whi_ref[...], preferred_element_type=F32)) + br_ref[...]
    lane = lax.broadcasted_iota(jnp.int32, lg.shape, 1)
    first = lambda hit: jnp.min(jnp.where(hit, lane, ROUTER_COLS), axis=-1, keepdims=True)
    gl = jnp.where(lane < N_GROUPS, lg, -jnp.inf)
    gmax = jnp.max(gl, axis=-1, keepdims=True)
    grp = first(gl == gmax)
    grp_p = 1.0 / jnp.sum(jnp.exp(gl - gmax), axis=-1, keepdims=True)
    lo = N_GROUPS + grp * EXPERTS_PER_GROUP
    el = jnp.where((lane >= lo) & (lane < lo + EXPERTS_PER_GROUP), lg, -jnp.inf)
    e1 = jnp.max(el, axis=-1, keepdims=True)
    i1 = first(el == e1)
    el2 = jnp.where(lane == i1, -jnp.inf, el)
    e2 = jnp.max(el2, axis=-1, keepdims=True)
    i2 = first(el2 == e2)
    r = jnp.exp(e2 - e1)
    w1 = grp_p / (1.0 + r)
    w2 = w1 * r
    @pl.when(pl.program_id(0) == 0)
    def _():
        cnt_ref[...] = jnp.zeros_like(cnt_ref)

    hit1, hit2 = lane == i1, lane == i2
    picks = jnp.where(hit1 | hit2, 1.0, 0.0)
    before = cnt_ref[0:1, :] + jnp.dot(tri_ref[...], picks.astype(BF16), preferred_element_type=F32)
    rank1 = jnp.sum(jnp.where(hit1, before, 0.0), axis=-1, keepdims=True)
    rank2 = jnp.sum(jnp.where(hit2, before, 0.0), axis=-1, keepdims=True)
    cnt_ref[...] = jnp.broadcast_to(cnt_ref[0:1, :] + jnp.sum(picks, axis=0, keepdims=True), cnt_ref.shape)
    vals = [(i1 - N_GROUPS).astype(F32), (i2 - N_GROUPS).astype(F32), w1, w2, rank1, rank2]
    out = jnp.zeros(lg.shape, F32)
    for k, val in enumerate(vals):
        out = jnp.where(lane == k, val, out)
    lg_ref[...] = out


def norm_router(xl, xc, nw, mod_l, mod_c, w_router, b_router, tm=TOKEN_TILE):
    b, L, d = xl.shape
    two_src = xc is not None
    n_lat = b * L // tm
    per_batch = L // tm
    n_ctx = (xc.shape[0] * xc.shape[1]) // tm if two_src else 0
    w_hi, w_lo = _split_bf16(w_router)
    const = lambda i: (0, 0)
    if two_src:
        shift = jnp.concatenate([mod_l[0], mod_c[0]], axis=0)
        scale = jnp.concatenate([mod_l[1], mod_c[1]], axis=0)
        mod_map = lambda i: (jnp.where(i < n_lat, i // per_batch, b), 0, 0)
        srcs = [xl.reshape(b * L, d), xc.reshape(-1, d)]
        src_specs = [pl.BlockSpec((tm, d), lambda i: (jnp.minimum(i, n_lat - 1), 0)),
                     pl.BlockSpec((tm, d), lambda i: (jnp.maximum(i - n_lat, 0), 0))]
    else:
        shift, scale = mod_l
        mod_map = lambda i: (i // per_batch, 0, 0)
        srcs = [xl.reshape(b * L, d)]
        src_specs = [pl.BlockSpec((tm, d), lambda i: (i, 0))]
    n_tok = (n_lat + n_ctx) * tm
    tri = _table_bf16(np.tril(np.ones((tm, tm)), -1))
    return pl.pallas_call(
        functools.partial(_norm_router_kernel, n_lat=n_lat, two_src=two_src),
        out_shape=(jax.ShapeDtypeStruct((n_tok, d), F32), jax.ShapeDtypeStruct((n_tok, ROUTER_COLS), F32),
                   jax.ShapeDtypeStruct((SUBLANE, ROUTER_COLS), F32)),
        grid=(n_lat + n_ctx,),
        in_specs=src_specs + [pl.BlockSpec((1, d), const), pl.BlockSpec((1, 1, d), mod_map),
                              pl.BlockSpec((1, 1, d), mod_map), pl.BlockSpec((d, ROUTER_COLS), const),
                              pl.BlockSpec((d, ROUTER_COLS), const), pl.BlockSpec((1, ROUTER_COLS), const),
                              pl.BlockSpec((tm, tm), const)],
        out_specs=(pl.BlockSpec((tm, d), lambda i: (i, 0)), pl.BlockSpec((tm, ROUTER_COLS), lambda i: (i, 0)),
                   pl.BlockSpec((SUBLANE, ROUTER_COLS), const)),
        compiler_params=_params("arbitrary"),
        name="moe_norm_router",
    )(*srcs, nw.reshape(1, d), shift, scale, w_hi, w_lo, b_router, tri)


def _expert_ffn_kernel(te_ref, x_ref, wg_ref, wu_ref, wd_ref, o_ref):
    x = x_ref[...].astype(BF16)
    g = jnp.dot(x, wg_ref[0].astype(BF16), preferred_element_type=F32)
    u = jnp.dot(x, wu_ref[0].astype(BF16), preferred_element_type=F32)
    o_ref[...] = jnp.dot((_silu(g) * u).astype(BF16), wd_ref[0].astype(BF16), preferred_element_type=F32)


def expert_ffn(x_sorted, tile_expert, w_gate, w_up, w_down, tm):
    r, d = x_sorted.shape
    f = w_gate.shape[-1]
    grid_spec = pltpu.PrefetchScalarGridSpec(
        num_scalar_prefetch=1,
        grid=(r // tm,),
        in_specs=[pl.BlockSpec((tm, d), lambda i, te: (i, 0)),
                  pl.BlockSpec((1, d, f), lambda i, te: (te[i], 0, 0)),
                  pl.BlockSpec((1, d, f), lambda i, te: (te[i], 0, 0)),
                  pl.BlockSpec((1, f, d), lambda i, te: (te[i], 0, 0))],
        out_specs=pl.BlockSpec((tm, d), lambda i, te: (i, 0)),
    )
    return pl.pallas_call(
        _expert_ffn_kernel,
        out_shape=jax.ShapeDtypeStruct((r, d), F32),
        grid_spec=grid_spec,
        compiler_params=_params("arbitrary"),
        name="moe_expert_ffn",
    )(tile_expert, x_sorted, w_gate, w_up, w_down)


def moe_apply(h_tokens, routed, counts_row, w_gate, w_up, w_down, layer, n_lat, tm=EXPERT_TILE):
    t, d = h_tokens.shape
    e_idx, rank = routed[:, 0:2].astype(jnp.int32), routed[:, 4:6].astype(jnp.int32)
    counts = counts_row[0, N_GROUPS:N_GROUPS + N_EXPERTS].astype(jnp.int32)
    n_pairs = 2 * t
    padded = (counts + tm - 1) // tm * tm
    pad_end = jnp.cumsum(padded)
    pad_start = pad_end - padded
    n_rows = n_pairs + N_EXPERTS * tm
    tile_start = jnp.arange(n_rows // tm, dtype=jnp.int32) * tm
    tile_expert = jnp.minimum(jnp.sum((pad_end[None, :] <= tile_start[:, None]).astype(jnp.int32), axis=1),
                              N_EXPERTS - 1)
    onehot = (e_idx[:, :, None] == jnp.arange(N_EXPERTS, dtype=jnp.int32)).astype(jnp.int32)
    pos = jnp.sum(onehot * pad_start, axis=-1) + rank
    j = jnp.arange(tm, dtype=jnp.int32)[None, :]
    fill_key = jnp.where(j < (padded - counts)[:, None], (pad_start + counts)[:, None] + j, n_rows)
    keys = jnp.concatenate([pos.reshape(-1), fill_key.reshape(-1)])
    toks = jnp.concatenate([jnp.arange(n_pairs, dtype=jnp.int32) // 2, jnp.arange(N_EXPERTS * tm, dtype=jnp.int32) % t])
    _, row_token = lax.sort((keys, toks), num_keys=1)
    x_sorted = h_tokens[row_token]
    wg = w_gate.reshape(-1, d, EXPERT_HIDDEN)
    wu = w_up.reshape(-1, d, EXPERT_HIDDEN)
    wd = w_down.reshape(-1, EXPERT_HIDDEN, d)
    y_sorted = expert_ffn(x_sorted, tile_expert + layer * N_EXPERTS, wg, wu, wd, tm)
    lat = (y_sorted[pos[:n_lat, 0]], y_sorted[pos[:n_lat, 1]])
    rest = (y_sorted[pos[n_lat:, 0]], y_sorted[pos[n_lat:, 1]]) if t > n_lat else None
    return lat, rest


def _final_kernel(x_ref, ya_ref, yb_ref, rt_ref, g_ref, w_ref, o_ref):
    x = x_ref[0] + g_ref[0] * _moe_mix(ya_ref[0], yb_ref[0], rt_ref[0])
    o_ref[0] = x * lax.rsqrt(jnp.mean(x * x, axis=-1, keepdims=True) + EPS) * w_ref[...]


def final_norm(x, ya, yb, routed, gate, w, tm=TOKEN_TILE):
    b, L, d = x.shape
    tok = pl.BlockSpec((1, tm, d), lambda bi, i: (bi, i, 0))
    return pl.pallas_call(
        _final_kernel,
        out_shape=jax.ShapeDtypeStruct((b, L, d), F32),
        grid=(b, L // tm),
        in_specs=[tok, tok, tok, pl.BlockSpec((1, tm, ROUTER_COLS), lambda bi, i: (bi, i, 0)),
                  pl.BlockSpec((1, 1, d), _mod_map(gate, b)), pl.BlockSpec((1, d), lambda bi, i: (0, 0))],
        out_specs=tok,
        compiler_params=_params("parallel", "arbitrary"),
        name="final_rmsnorm",
    )(x, ya, yb, routed, gate, w.reshape(1, d))


IN_SIZES = (SSD_WIDTH, SSD_CONV_CH, 2 * SSD_HEADS, HY_COLS, 2 * ML_WIDTH, ML_WIDTH, ML_WIDTH, 4 * ML_HEADS)


def _regroup_kernel(w_ref, o_ref):
    src = dst = 0
    for n in IN_SIZES:
        pad = -n % LANE
        o_ref[0, :, dst:dst + n] = w_ref[0, :, src:src + n].astype(BF16)
        if pad:
            o_ref[0, :, dst + n:dst + n + pad] = jnp.zeros((o_ref.shape[1], pad), BF16)
        src, dst = src + n, dst + n + pad


def regroup_in_weights(w_in):
    depth, d, n_in = w_in.shape
    n_out = sum(n + (-n % LANE) for n in IN_SIZES)
    one = pl.Buffered(1)
    return pl.pallas_call(
        _regroup_kernel,
        out_shape=jax.ShapeDtypeStruct((depth, d, n_out), BF16),
        grid=(depth,),
        in_specs=[pl.BlockSpec((1, d, n_in), lambda i: (i, 0, 0), pipeline_mode=one)],
        out_specs=pl.BlockSpec((1, d, n_out), lambda i: (i, 0, 0)),
        compiler_params=_params("arbitrary"),
        name="regroup_in_weights",
    )(w_in)


def kernel(x, c, ctx, c_ctx, w_mod, b_mod, norm1_w, norm2_w, w_in, w_out, ssd_conv_w, ssd_conv_b, ssd_dt_bias, ssd_a_log, ssd_d, ssd_norm_w, hy_conv_w, hy_conv_b, hy_pos_w1, hy_pos_b1, hy_pos_w2, hy_pos_b2, hy_pos_w3, hy_freq, hy_decay, hy_skip, ml_conv_w, ml_conv_b, ml_gate_b, ml_norm_w, grp_router_w, grp_router_b, exp_router_w, exp_router_b, moe_w_gate, moe_w_up, moe_w_down, final_norm_w):
    layer_params = dict(
        ssd_conv_w=ssd_conv_w, ssd_conv_b=ssd_conv_b, ssd_dt_bias=ssd_dt_bias, ssd_a_log=ssd_a_log,
        ssd_d=ssd_d, ssd_norm_w=ssd_norm_w, hy_conv_w=hy_conv_w, hy_conv_b=hy_conv_b,
        hy_pos_w1=hy_pos_w1, hy_pos_b1=hy_pos_b1, hy_pos_w2=hy_pos_w2, hy_pos_b2=hy_pos_b2,
        hy_pos_w3=hy_pos_w3, hy_freq=hy_freq, hy_decay=hy_decay, hy_skip=hy_skip,
        ml_conv_w=ml_conv_w, ml_conv_b=ml_conv_b, ml_gate_b=ml_gate_b, ml_norm_w=ml_norm_w)
    bsz, seq, d = x.shape
    n_ctx = ctx.shape[1]
    xl, xc = x, ctx
    moe_l = moe_c = None
    c_rows = jnp.concatenate([c, c_ctx[None, :], jnp.zeros((SUBLANE - bsz - 1, d), F32)], axis=0)
    w_in_all = regroup_in_weights(w_in)

    def layer_prep(i):
        p = {name: arr[i] for name, arr in layer_params.items()}
        mod = modulation(c_rows, w_mod, b_mod, i).reshape(SUBLANE, N_MOD, 1, d)
        pad = ROUTER_COLS - N_GROUPS - N_EXPERTS
        return dict(
            p=p, sp=ssd_prepare(p), mp=ml_prepare(p),
            mod_l=[mod[:bsz, k] for k in range(N_MOD)],
            mod_c=[mod[bsz:bsz + 1, k] for k in range(N_MOD)],
            w_out=w_out[i].astype(BF16),
            w_router=jnp.pad(jnp.concatenate([grp_router_w[i], exp_router_w[i]], axis=1), ((0, 0), (0, pad))),
            b_router=jnp.pad(jnp.concatenate([grp_router_b[i], exp_router_b[i]]), (0, pad)).reshape(1, ROUTER_COLS),
            h_spec=hyena_filter_spectra(*hyena_filter_signals(seq, p)),
            ctx_filt=hyena_filter_signals(n_ctx, p) if i < DEPTH - 1 else None)

    prep = layer_prep(0)
    for i in range(DEPTH):
        last = i == DEPTH - 1
        p, sp, mp, mod_l, mod_c, w_out_b = prep['p'], prep['sp'], prep['mp'], prep['mod_l'], prep['mod_c'], prep['w_out']

        uc_ssd, uc_hy, uc_ml, xc = norm_proj(xc, moe_c, norm1_w[i], mod_c[0], mod_c[1], w_in_all, i)
        col_major = i % 2 == 1
        ul_ssd, ul_hy, ul_ml, xl = norm_proj(xl, moe_l, norm1_w[i], mod_l[0], mod_l[1], w_in_all, i, col_major)
        y_ssd = ssd_mixer(ul_ssd, uc_ssd, sp)
        y_ml = ml_mixer(ul_ml, uc_ml, mp)
        yl_hy = hyena_long(ul_hy, prep['h_spec'], p['hy_conv_w'], p['hy_conv_b'], p['hy_skip'])
        xl = out_proj(y_ssd, yl_hy, y_ml, xl, mod_l[2], w_out_b, col_major)
        if not last:
            yc_hy = hyena_short(uc_hy, *prep['ctx_filt'], p['hy_conv_w'], p['hy_conv_b'], p['hy_skip'])
            xc = out_proj(y_ssd, yc_hy, y_ml, xc, mod_c[2], w_out_b, scan_off=seq // n_ctx)
        h_all, routed, counts_row = norm_router(xl, None if last else xc, norm2_w[i], (mod_l[3], mod_l[4]),
                                                (mod_c[3], mod_c[4]), prep['w_router'], prep['b_router'])
        if not last:
            prep = layer_prep(i + 1)
        lat, rest = moe_apply(h_all, routed, counts_row, moe_w_gate, moe_w_up, moe_w_down, i, bsz * seq)
        n_lat = bsz * seq
        moe_l = (lat[0].reshape(bsz, seq, d), lat[1].reshape(bsz, seq, d),
                 routed[:n_lat].reshape(bsz, seq, ROUTER_COLS), mod_l[5])
        if not last:
            moe_c = (rest[0].reshape(bsz, n_ctx, d), rest[1].reshape(bsz, n_ctx, d),
                     routed[n_lat:].reshape(bsz, n_ctx, ROUTER_COLS), mod_c[5])
    return final_norm(xl, *moe_l, final_norm_w)
```

```python
import functools
import math

import jax
import jax.numpy as jnp
import numpy as np
from jax import lax
from jax.experimental import pallas as pl
from jax.experimental.pallas import tpu as pltpu

D_MODEL = 1024
DEPTH = 2
GRID_W = 64
HEAD_DIM = 64
SSD_WIDTH = 384
SSD_HEADS = SSD_WIDTH // HEAD_DIM
SSD_GROUPS = 2
SSD_STATE = 64
HYENA_WIDTH = 256
HYENA_ORDER = 2
HYENA_BANDS = 16
ML_WIDTH = 384
ML_HEADS = ML_WIDTH // HEAD_DIM
N_GROUPS = 4
EXPERTS_PER_GROUP = 8
N_EXPERTS = N_GROUPS * EXPERTS_PER_GROUP
EXPERT_HIDDEN = 256
N_MOD = 6
EPS = 1e-6

LANE = 128
SUBLANE = 8
VMEM_LIMIT = 48 * 1024 * 1024
VMEM_LIMIT_HY = 56 * 1024 * 1024

TOKEN_TILE = 512
EXPERT_TILE = 256
MOD_TILE = 1536
SCAN_CHUNK = 256

SSD_CONV_CH = SSD_WIDTH + 2 * SSD_GROUPS * SSD_STATE
SSD_XBC0 = SSD_WIDTH
SSD_DT0 = SSD_XBC0 + SSD_CONV_CH
SSD_COLS = SSD_DT0 + LANE
HY_COLS = (HYENA_ORDER + 1) * HYENA_WIDTH
ML_V0 = 2 * ML_WIDTH
ML_O0 = ML_V0 + ML_WIDTH
ML_G0 = ML_O0 + ML_WIDTH
ML_COLS = ML_G0 + LANE
ROUTER_COLS = LANE

F32 = jnp.float32
BF16 = jnp.bfloat16
HI = lax.Precision.HIGHEST


def _params(*sem):
    return pltpu.CompilerParams(dimension_semantics=sem, vmem_limit_bytes=VMEM_LIMIT)


def _table_bf16(a):
    return jnp.asarray(np.asarray(a, np.float32)).astype(BF16)


def _silu(x):
    return x * jax.nn.sigmoid(x)


def _softplus(x):
    return jnp.maximum(x, 0.0) + jnp.log(1.0 + jnp.exp(-jnp.abs(x)))


def _log_sigmoid(x):
    return jnp.minimum(x, 0.0) - jnp.log(1.0 + jnp.exp(-jnp.abs(x)))


def _mod_kernel(c_ref, w_ref, b_ref, o_ref):
    o_ref[...] = jnp.dot(_silu(c_ref[...]), w_ref[...], preferred_element_type=F32, precision=HI) + b_ref[...]


def modulation(c_rows, w_mod, b_mod, layer):
    depth, d, n = w_mod.shape
    tn = MOD_TILE
    return pl.pallas_call(
        _mod_kernel,
        out_shape=jax.ShapeDtypeStruct((c_rows.shape[0], n), F32),
        grid=(n // tn,),
        in_specs=[pl.BlockSpec(c_rows.shape, lambda j: (0, 0)),
                  pl.BlockSpec((None, d, tn), lambda j: (layer, 0, j)),
                  pl.BlockSpec((None, 1, tn), lambda j: (layer, 0, j))],
        out_specs=pl.BlockSpec((c_rows.shape[0], tn), lambda j: (0, j)),
        compiler_params=_params("arbitrary"),
        name="adaln_modulation",
    )(c_rows, w_mod, b_mod.reshape(depth, 1, n))


def _modnorm(x, nw, shift, scale):
    y = x * lax.rsqrt(jnp.mean(x * x, axis=-1, keepdims=True) + EPS) * nw
    return y * (1.0 + scale) + shift


def _mod_map(mod, b):
    return (lambda bi, i: (bi, 0, 0)) if mod.shape[0] == b else (lambda bi, i: (0, 0, 0))


def _tok_view(x, col_major):
    *lead, L, d = x.shape
    return x.reshape(*lead, L // GRID_W, GRID_W, d) if col_major else x


def _tok_spec(L, d, tm, col_major, pick=None):
    if col_major:
        assert tm == (L // GRID_W) * SUBLANE
        shape, index = (1, L // GRID_W, SUBLANE, d), (lambda bi, i: (bi, 0, i, 0))
    else:
        shape, index = (1, tm, d), (lambda bi, i: (bi, i, 0))
    if pick is None:
        return pl.BlockSpec(shape, index)
    return pl.BlockSpec((None,) + shape, lambda bi, i: (pick,) + index(bi, i))


def _tok_load(ref, col_major):
    if not col_major:
        return ref[0]
    return jnp.concatenate([ref[0, :, j, :] for j in range(ref.shape[2])], axis=0)


def _tok_store(ref, val, col_major):
    if not col_major:
        ref[0] = val
        return
    rows = ref.shape[1]
    for j in range(ref.shape[2]):
        ref[0, :, j, :] = val[j * rows:(j + 1) * rows]


def _moe_mix(ya, yb, routed):
    return routed[:, 2:3] * ya + routed[:, 3:4] * yb


def _norm_proj_kernel(*refs, col_major, fuse_moe):
    if fuse_moe:
        x_ref, ya_ref, yb_ref, rt_ref, g_ref, nw_ref, sh_ref, sc_ref, w_ref, ssd_ref, hy_ref, ml_ref, xo_ref = refs
    else:
        x_ref, nw_ref, sh_ref, sc_ref, w_ref, ssd_ref, hy_ref, ml_ref = refs
    x = _tok_load(x_ref, col_major)
    if fuse_moe:
        x = x + g_ref[0] * _moe_mix(_tok_load(ya_ref, col_major), _tok_load(yb_ref, col_major),
                                    _tok_load(rt_ref, col_major))
        _tok_store(xo_ref, x, col_major)
    h = _modnorm(x, nw_ref[...], sh_ref[0], sc_ref[0])
    u = jnp.dot(h.astype(BF16), w_ref[...], preferred_element_type=F32)
    ssd_ref[0] = u[:, 0:SSD_COLS]
    hy_ref[0] = u[:, SSD_COLS:SSD_COLS + HY_COLS]
    ml_ref[0] = u[:, SSD_COLS + HY_COLS:]


def norm_proj(x, moe, nw, shift, scale, w_all, layer, col_major=False, tm=TOKEN_TILE):
    b, L, d = x.shape
    n = w_all.shape[2]
    fuse_moe = moe is not None
    tm = (L // GRID_W) * SUBLANE if col_major else min(tm, L)
    tok = _tok_spec(L, d, tm, col_major)
    row = lambda bi, i: (bi, i, 0)
    const2 = lambda bi, i: (0, 0)
    args, in_specs = [_tok_view(x, col_major)], [tok]
    if fuse_moe:
        y2, routed, gate = moe
        args += [_tok_view(y2, col_major), _tok_view(y2, col_major), _tok_view(routed, col_major), gate]
        in_specs += [_tok_spec(L, d, tm, col_major, pick=0), _tok_spec(L, d, tm, col_major, pick=1),
                     _tok_spec(L, ROUTER_COLS, tm, col_major), pl.BlockSpec((1, 1, d), _mod_map(gate, b))]
    args += [nw.reshape(1, d), shift, scale, w_all]
    in_specs += [pl.BlockSpec((1, d), const2), pl.BlockSpec((1, 1, d), _mod_map(shift, b)),
                 pl.BlockSpec((1, 1, d), _mod_map(scale, b)),
                 pl.BlockSpec((None, d, n), lambda bi, i: (layer, 0, 0))]
    out_shape = [jax.ShapeDtypeStruct((b, L, SSD_COLS), F32), jax.ShapeDtypeStruct((b, L, HY_COLS), F32),
                 jax.ShapeDtypeStruct((b, L, ML_COLS), F32)]
    out_specs = [pl.BlockSpec((1, tm, SSD_COLS), row), pl.BlockSpec((1, tm, HY_COLS), row),
                 pl.BlockSpec((1, tm, ML_COLS), row)]
    if fuse_moe:
        out_shape.append(jax.ShapeDtypeStruct(args[0].shape, F32))
        out_specs.append(tok)
    outs = pl.pallas_call(
        functools.partial(_norm_proj_kernel, col_major=col_major, fuse_moe=fuse_moe),
        out_shape=tuple(out_shape),
        grid=(b, L // tm),
        in_specs=in_specs,
        out_specs=tuple(out_specs),
        compiler_params=_params("parallel", "arbitrary"),
        name="norm_in_proj",
    )(*args)
    return (*outs[:3], outs[3].reshape(b, L, d) if fuse_moe else x)


def _out_proj_kernel(ys_ref, yh_ref, ym_ref, x_ref, g_ref, w_ref, o_ref, *, col_major):
    y = jnp.concatenate([ys_ref[0], yh_ref[0], ym_ref[0]], axis=-1).astype(BF16)
    r = _tok_load(x_ref, col_major) + g_ref[0] * jnp.dot(y, w_ref[...], preferred_element_type=F32)
    _tok_store(o_ref, r, col_major)


def out_proj(y_ssd, y_hy, y_ml, x, gate, w_bf16, col_major=False, scan_off=0, tm=TOKEN_TILE):
    b, L, d = x.shape
    tm = (L // GRID_W) * SUBLANE if col_major else min(tm, L)
    row = lambda bi, i: (bi, i, 0)
    scan_row = lambda bi, i: (bi, i + scan_off, 0)
    tok = _tok_spec(L, d, tm, col_major)
    xv = _tok_view(x, col_major)
    return pl.pallas_call(
        functools.partial(_out_proj_kernel, col_major=col_major),
        out_shape=jax.ShapeDtypeStruct(xv.shape, F32),
        grid=(b, L // tm),
        in_specs=[pl.BlockSpec((1, tm, SSD_WIDTH), scan_row), pl.BlockSpec((1, tm, HYENA_WIDTH), row),
                  pl.BlockSpec((1, tm, ML_WIDTH), scan_row), tok,
                  pl.BlockSpec((1, 1, d), _mod_map(gate, b)),
                  pl.BlockSpec(w_bf16.shape, lambda bi, i: (0, 0))],
        out_specs=tok,
        compiler_params=_params("parallel", "arbitrary"),
        name="out_proj_residual",
    )(y_ssd, y_hy, y_ml, xv, gate, w_bf16).reshape(b, L, d)


def _conv3(xr, prev_row, next_row, cw, cb, q):
    rid = lax.broadcasted_iota(jnp.int32, (q, 1), 0)
    x_prev = jnp.where(rid == 0, prev_row, pltpu.roll(xr, 1, axis=0))
    x_next = jnp.where(rid == q - 1, next_row, pltpu.roll(xr, q - 1, axis=0))
    return x_prev * cw[0:1] + xr * cw[1:2] + x_next * cw[2:3] + cb


def _masked_scan(mask, x):
    m = mask.astype(BF16)
    hi = x.astype(BF16)
    r1 = x - hi.astype(F32)
    mid = r1.astype(BF16)
    lo = (r1 - mid.astype(F32)).astype(BF16)
    return (jnp.dot(m, hi, preferred_element_type=F32) + jnp.dot(m, mid, preferred_element_type=F32)
            + jnp.dot(m, lo, preferred_element_type=F32))


def _scan_mask(q, direction):
    li = lax.broadcasted_iota(jnp.int32, (q, q), 0)
    si = lax.broadcasted_iota(jnp.int32, (q, q), 1)
    return (si <= li) if direction == 0 else (si >= li)


def _running_max(x, direction, q):
    rid = lax.broadcasted_iota(jnp.int32, (q, 1), 0)
    s = 1
    while s < q:
        if direction == 0:
            x = jnp.where(rid >= s, jnp.maximum(x, pltpu.roll(x, s, axis=0)), x)
        else:
            x = jnp.where(rid < q - s, jnp.maximum(x, pltpu.roll(x, q - s, axis=0)), x)
        s *= 2
    return x


def _scan_specs(L, q, nc, cols, direction):
    hb = q // SUBLANE
    nrb = L // SUBLANE
    order = (lambda j: j) if direction == 0 else (lambda j: nc - 1 - j)
    cidx = lambda j: order(jnp.maximum(j - 1, 0))
    specs = [pl.BlockSpec((1, q, cols), lambda bi, j: (bi, cidx(j), 0)),
             pl.BlockSpec((1, SUBLANE, cols), lambda bi, j: (bi, jnp.maximum(cidx(j) * hb - 1, 0), 0)),
             pl.BlockSpec((1, SUBLANE, cols), lambda bi, j: (bi, jnp.minimum((cidx(j) + 1) * hb, nrb - 1), 0)),
             pl.BlockSpec((1, q, cols), lambda bi, j: (bi, 0, 0))]
    yidx = lambda j: jnp.where(j == 0, nc, cidx(j))
    return specs, yidx


def _scan_inputs(u_ref, prev_ref, next_ref, uc_ref, lo, hi, direction, nc):
    j = pl.program_id(1)
    jm = jnp.maximum(j - 1, 0)
    c = jm if direction == 0 else nc - 1 - jm
    is_ctx = j == 0
    x = jnp.where(is_ctx, uc_ref[0, :, lo:hi], u_ref[0, :, lo:hi])
    prev_row = jnp.where(is_ctx | (c == 0), 0.0, prev_ref[0, SUBLANE - 1:SUBLANE, lo:hi])
    next_row = jnp.where(is_ctx | (c == nc - 1), 0.0, next_ref[0, 0:1, lo:hi])
    return x, prev_row, next_row


def _ssd_kernel(*refs, direction, finalize, q, nc):
    if finalize:
        u_ref, prev_ref, next_ref, uc_ref, yb_ref, cw_ref, cb_ref, dtb_ref, a_ref, d_ref, nw_ref, y_ref, state_ref = refs
    else:
        u_ref, prev_ref, next_ref, uc_ref, cw_ref, cb_ref, dtb_ref, a_ref, y_ref, state_ref = refs

    @pl.when(pl.program_id(1) == 0)
    def _():
        state_ref[...] = jnp.zeros_like(state_ref)

    xr, prev_row, next_row = _scan_inputs(u_ref, prev_ref, next_ref, uc_ref, SSD_XBC0, SSD_DT0, direction, nc)
    xc = _silu(_conv3(xr, prev_row, next_row, cw_ref[...], cb_ref[...], q))
    u_rest, _, _ = _scan_inputs(u_ref, prev_ref, next_ref, uc_ref, SSD_DT0, SSD_COLS, direction, nc)

    dt = _softplus(u_rest + dtb_ref[...])
    mask = _scan_mask(q, direction)
    cum = _masked_scan(mask, dt * a_ref[...])
    cum_t = cum.T
    end = q - 1 if direction == 0 else 0

    ys = []
    for g in range(SSD_GROUPS):
        b0 = SSD_WIDTH + g * SSD_STATE
        c0 = SSD_WIDTH + (SSD_GROUPS + g) * SSD_STATE
        bm_t = xc[:, b0:b0 + SSD_STATE].T
        cm = xc[:, c0:c0 + SSD_STATE].astype(BF16)
        scores = jnp.dot(cm, bm_t.astype(BF16), preferred_element_type=F32)
        for h in range(g * (SSD_HEADS // SSD_GROUPS), (g + 1) * (SSD_HEADS // SSD_GROUPS)):
            hl = direction * SSD_HEADS + h
            col = cum[:, hl:hl + 1]
            row = cum_t[hl:hl + 1, :]
            seg = jnp.exp(jnp.where(mask, col - row, -jnp.inf))
            xdt = (xc[:, h * HEAD_DIM:(h + 1) * HEAD_DIM] * dt[:, hl:hl + 1]).astype(BF16)
            y = jnp.dot((scores * seg).astype(BF16), xdt, preferred_element_type=F32)
            st = state_ref[h]
            y = y + jnp.dot(cm, st.astype(BF16), preferred_element_type=F32) * jnp.exp(col)
            tot = cum[end:end + 1, hl:hl + 1]
            upd = jnp.dot((bm_t * jnp.exp(tot - row)).astype(BF16), xdt, preferred_element_type=F32)
            state_ref[h] = st * jnp.exp(tot) + upd
            ys.append(y)
    y_all = jnp.concatenate(ys, axis=-1)
    if finalize:
        z, _, _ = _scan_inputs(u_ref, prev_ref, next_ref, uc_ref, 0, SSD_WIDTH, direction, nc)
        t = (y_all + yb_ref[0] + xc[:, 0:SSD_WIDTH] * d_ref[...]) * _silu(z)
        y_all = t * lax.rsqrt(jnp.mean(t * t, axis=-1, keepdims=True) + EPS) * nw_ref[...]
    y_ref[0] = y_all


def ssd_pass(u, u_ctx, y_other, sp, direction, q):
    b, L, _ = u.shape
    nc = L // q
    finalize = y_other is not None
    in_specs, yidx = _scan_specs(L, q, nc, SSD_COLS, direction)
    const2 = lambda bi, j: (0, 0)
    y_spec = pl.BlockSpec((1, q, SSD_WIDTH), lambda bi, j: (bi, yidx(j), 0))
    args = [u, u, u, u_ctx]
    if finalize:
        in_specs.append(y_spec)
        args.append(y_other)
    consts = [sp['cw'], sp['cb'], sp['dtb'], sp['a']] + ([sp['d'], sp['nw']] if finalize else [])
    in_specs += [pl.BlockSpec(t.shape, const2) for t in consts]
    args += consts
    return pl.pallas_call(
        functools.partial(_ssd_kernel, direction=direction, finalize=finalize, q=q, nc=nc),
        out_shape=jax.ShapeDtypeStruct((b, L + q, SSD_WIDTH), F32),
        grid=(b, nc + 1),
        in_specs=in_specs,
        out_specs=y_spec,
        scratch_shapes=[pltpu.VMEM((SSD_HEADS, HEAD_DIM, SSD_STATE), F32)],
        compiler_params=_params("parallel", "arbitrary"),
        name="ssd_scan_%s" % ("fwd" if direction == 0 else "bwd"),
    )(*args)


def ssd_prepare(p):
    pad = lambda v: jnp.pad(v.reshape(1, -1), ((0, 0), (0, LANE - v.size)))
    return dict(cw=p['ssd_conv_w'], cb=p['ssd_conv_b'].reshape(1, -1),
                dtb=pad(p['ssd_dt_bias']), a=pad(-jnp.exp(p['ssd_a_log'])),
                d=jnp.repeat(p['ssd_d'], HEAD_DIM).reshape(1, -1), nw=p['ssd_norm_w'].reshape(1, -1))


def ssd_mixer(u, u_ctx, sp, q=SCAN_CHUNK):
    assert u_ctx.shape[1] == q
    return ssd_pass(u, u_ctx, ssd_pass(u, u_ctx, None, sp, 1, q), sp, 0, q)


def _ml_kernel(*refs, direction, finalize, q, nc):
    if finalize:
        u_ref, prev_ref, next_ref, uc_ref, hb_ref, cw_ref, cb_ref, gb_ref, nw_ref, pool_ref, y_ref, s_ref, m_ref = refs
    else:
        u_ref, prev_ref, next_ref, uc_ref, cw_ref, cb_ref, gb_ref, y_ref, s_ref, m_ref = refs

    @pl.when(pl.program_id(1) == 0)
    def _():
        s_ref[...] = jnp.zeros_like(s_ref)
        m_ref[...] = jnp.zeros_like(m_ref)

    xr, prev_row, next_row = _scan_inputs(u_ref, prev_ref, next_ref, uc_ref, 0, ML_V0, direction, nc)
    qk = _silu(_conv3(xr, prev_row, next_row, cw_ref[...], cb_ref[...], q))
    rest, _, _ = _scan_inputs(u_ref, prev_ref, next_ref, uc_ref, ML_V0, ML_COLS, direction, nc)
    v = rest[:, 0:ML_WIDTH]

    gb = rest[:, ML_G0 - ML_V0:] + gb_ref[...]
    mask = _scan_mask(q, direction)
    cum = jnp.dot(mask.astype(F32), _log_sigmoid(gb), preferred_element_type=F32, precision=HI)
    ig = pltpu.roll(gb, ML_HEADS, axis=1)
    end = q - 1 if direction == 0 else 0
    m_prev = m_ref[0:1, :]
    tot = cum[end:end + 1, :]
    w_end = tot - cum + ig
    m_loc = jnp.max(w_end, axis=0, keepdims=True)
    e_end = jnp.exp(w_end - m_loc)
    m_new = jnp.maximum(tot + m_prev, m_loc)
    a_prev = jnp.exp(tot + m_prev - m_new)
    a_loc = jnp.exp(m_loc - m_new)
    inter = cum + m_prev
    rel = ig - cum
    m_t = jnp.maximum(inter, cum + _running_max(rel, direction, q))
    col_a = cum - m_t
    a_inter = jnp.exp(inter - m_t)
    floor = jnp.exp(-m_t)
    rel_t = rel.T
    e_end_t = e_end.T
    k_t = (qk[:, ML_WIDTH:2 * ML_WIDTH] * (HEAD_DIM ** -0.5)).T
    one_col = (lax.broadcasted_iota(jnp.int32, (q, HEAD_DIM), 1) == 0).astype(F32)

    ys = []
    for h in range(ML_HEADS):
        fl = direction * 2 * ML_HEADS + ML_HEADS + h
        qh = qk[:, h * HEAD_DIM:(h + 1) * HEAD_DIM].astype(BF16)
        kh_t = k_t[h * HEAD_DIM:(h + 1) * HEAD_DIM, :]
        v_ext = jnp.concatenate([v[:, h * HEAD_DIM:(h + 1) * HEAD_DIM], one_col], axis=-1).astype(BF16)
        pw = jnp.exp(jnp.where(mask, col_a[:, fl:fl + 1] + rel_t[fl:fl + 1, :], -jnp.inf))
        scores = jnp.dot(qh, kh_t.astype(BF16), preferred_element_type=F32)
        nd = jnp.dot((scores * pw).astype(BF16), v_ext, preferred_element_type=F32)
        st = s_ref[h]
        nd = nd + a_inter[:, fl:fl + 1] * jnp.dot(qh, st.astype(BF16), preferred_element_type=F32)
        den = nd[:, HEAD_DIM:HEAD_DIM + 1]
        ys.append(nd[:, 0:HEAD_DIM] / jnp.maximum(jnp.abs(den), floor[:, fl:fl + 1]))
        upd = jnp.dot((kh_t * e_end_t[fl:fl + 1, :]).astype(BF16), v_ext, preferred_element_type=F32)
        s_ref[h] = a_prev[:, fl:fl + 1] * st + a_loc[:, fl:fl + 1] * upd
    m_ref[...] = jnp.broadcast_to(m_new, m_ref.shape)
    y_all = jnp.concatenate(ys, axis=-1)
    if finalize:
        hs = y_all + hb_ref[0]
        hc = hs - jnp.dot(hs.astype(BF16), pool_ref[...], preferred_element_type=F32)
        var = jnp.dot((hc * hc).astype(BF16), pool_ref[...], preferred_element_type=F32)
        y_all = hc * lax.rsqrt(var + EPS) * nw_ref[...] * jax.nn.sigmoid(rest[:, ML_O0 - ML_V0:ML_G0 - ML_V0])
    y_ref[0] = y_all


def ml_pass(u, u_ctx, h_other, mp, direction, q):
    b, L, _ = u.shape
    nc = L // q
    finalize = h_other is not None
    in_specs, yidx = _scan_specs(L, q, nc, ML_COLS, direction)
    const2 = lambda bi, j: (0, 0)
    y_spec = pl.BlockSpec((1, q, ML_WIDTH), lambda bi, j: (bi, yidx(j), 0))
    args = [u, u, u, u_ctx]
    if finalize:
        in_specs.append(y_spec)
        args.append(h_other)
    consts = [mp['cw'], mp['cb'], mp['gb']] + ([mp['nw'], mp['pool']] if finalize else [])
    in_specs += [pl.BlockSpec(t.shape, const2) for t in consts]
    args += consts
    return pl.pallas_call(
        functools.partial(_ml_kernel, direction=direction, finalize=finalize, q=q, nc=nc),
        out_shape=jax.ShapeDtypeStruct((b, L + q, ML_WIDTH), F32),
        grid=(b, nc + 1),
        in_specs=in_specs,
        out_specs=y_spec,
        scratch_shapes=[pltpu.VMEM((ML_HEADS, HEAD_DIM, LANE), F32), pltpu.VMEM((SUBLANE, LANE), F32)],
        compiler_params=_params("parallel", "arbitrary"),
        name="mlstm_scan_%s" % ("fwd" if direction == 0 else "bwd"),
    )(*args)


def ml_prepare(p):
    gb = p['ml_gate_b'].reshape(1, -1)
    head = np.arange(ML_WIDTH) // HEAD_DIM
    pool = _table_bf16((head[:, None] == head[None, :]) / HEAD_DIM)
    return dict(cw=p['ml_conv_w'], cb=p['ml_conv_b'].reshape(1, -1),
                gb=jnp.pad(gb, ((0, 0), (0, LANE - gb.shape[1]))), nw=p['ml_norm_w'].reshape(1, -1), pool=pool)


def ml_mixer(u, u_ctx, mp, q=SCAN_CHUNK):
    assert u_ctx.shape[1] == q
    return ml_pass(u, u_ctx, ml_pass(u, u_ctx, None, mp, 1, q), mp, 0, q)


FFT_L = 4096
FFT_N = 2 * FFT_L
FFT_N2 = 128
FFT_N1 = FFT_N // FFT_N2
FFT_N1H = FFT_L // FFT_N2
FFT_K1 = FFT_N1 // 2 + 1
FFT_R = 80
FFT_UNROLL = 16
FFT_K1_UNROLL = 11
FFT_PITCH = FFT_N2 + SUBLANE


def _fft_tables():
    n2 = np.arange(FFT_N2)[:, None, None]
    k1 = np.arange(FFT_K1)[None, :, None]
    n1 = np.arange(FFT_N1H)[None, None, :]
    th = 2 * np.pi * (((FFT_N2 * n1 + n2) * k1) % FFT_N) / FFT_N
    f1 = np.zeros((FFT_N2, FFT_R, FFT_N1H))
    f1[:, 0:2 * FFT_K1:2, :] = np.cos(th)
    f1[:, 1:2 * FFT_K1:2, :] = -np.sin(th)
    wgt = np.where((np.arange(FFT_K1) == 0) | (np.arange(FFT_K1) == FFT_N1 // 2), 1.0, 2.0)[None, :, None] / FFT_N
    g1 = np.zeros((FFT_N2, FFT_N1H, FFT_R))
    g1[:, :, 0:2 * FFT_K1:2] = np.transpose(wgt * np.cos(th), (0, 2, 1))
    g1[:, :, 1:2 * FFT_K1:2] = np.transpose(-wgt * np.sin(th), (0, 2, 1))
    ph = 2 * np.pi * ((np.arange(FFT_N2)[:, None] * np.arange(FFT_N2)[None, :]) % FFT_N2) / FFT_N2
    c, s = np.cos(ph), np.sin(ph)
    f2 = np.block([[c, s], [-s, c]])
    sign = np.where(np.arange(FFT_R) // 2 % 2 == 0, 1.0, -1.0)[None, :, None]
    f1ab = np.concatenate([f1, f1 * sign], axis=2)
    return _table_bf16(f1), _table_bf16(f1ab), _table_bf16(f2), _table_bf16(f2.T), _table_bf16(g1)


def _sld(ref, n2, count):
    rows = pl.ds(n2, count, stride=FFT_PITCH)
    return jnp.concatenate([ref[0, rows, :], ref[1, rows, :]], axis=-1)


def _sst(ref, n2, count, val):
    rows = pl.ds(n2, count, stride=FFT_PITCH)
    ref[0, rows, :] = val[:, 0:LANE]
    ref[1, rows, :] = val[:, LANE:2 * LANE]


def _blk_ld(ref, blk, nblk):
    parts = []
    for k in range(nblk):
        rows = pl.ds(pl.multiple_of((blk + k) * FFT_PITCH, SUBLANE), FFT_N2)
        parts.append(jnp.concatenate([ref[0, rows, :], ref[1, rows, :]], axis=-1))
    return parts[0] if nblk == 1 else jnp.concatenate(parts, axis=0)


def _blk_st(ref, blk, nblk, val):
    for k in range(nblk):
        rows = pl.ds(pl.multiple_of((blk + k) * FFT_PITCH, SUBLANE), FFT_N2)
        ref[0, rows, :] = val[k * FFT_N2:(k + 1) * FFT_N2, 0:LANE]
        ref[1, rows, :] = val[k * FFT_N2:(k + 1) * FFT_N2, LANE:2 * LANE]


def _fft_stage1(z_ref, a_ref, f1_ref, n_in=FFT_N1H):
    def body(n2, carry):
        xs = _sld(z_ref, n2, n_in).astype(BF16)
        _sst(a_ref, n2, FFT_R, jnp.dot(f1_ref[n2], xs, preferred_element_type=F32))
        return carry
    lax.fori_loop(0, FFT_N2, body, 0, unroll=FFT_UNROLL)


def _spectrum_kernel(x_ref, inorm_ref, f1_ref, f2_ref, o_ref, z_ref, a_ref):
    for half in range(2):
        for n1 in range(FFT_N1H):
            _blk_st(z_ref, half * FFT_N1H + n1, 1, x_ref[half, n1 * FFT_N2:(n1 + 1) * FFT_N2, :])
    _fft_stage1(z_ref, a_ref, f1_ref, 2 * FFT_N1H)
    inorm = inorm_ref[0]

    def body(k1, carry):
        slab = _blk_ld(a_ref, 2 * k1, 2).astype(BF16)
        o_ref[0, k1] = (jnp.dot(f2_ref[...], slab, preferred_element_type=F32) * inorm).astype(BF16)
        return carry
    lax.fori_loop(0, FFT_K1, body, 0, unroll=FFT_K1_UNROLL)


def hyena_filter_spectra(sig, inv_norm):
    _, L, c = sig.shape
    _, f1ab, f2, _, _ = _fft_tables()
    one = pl.Buffered(1)
    return pl.pallas_call(
        _spectrum_kernel,
        out_shape=jax.ShapeDtypeStruct((HYENA_ORDER, FFT_K1, 2 * FFT_N2, c), BF16),
        grid=(HYENA_ORDER,),
        in_specs=[pl.BlockSpec((2, L, c), lambda i: (i, 0, 0)),
                  pl.BlockSpec((1, 1, c), lambda i: (i, 0, 0)),
                  pl.BlockSpec(f1ab.shape, lambda i: (0, 0, 0), pipeline_mode=one),
                  pl.BlockSpec(f2.shape, lambda i: (0, 0), pipeline_mode=one)],
        out_specs=pl.BlockSpec((1, FFT_K1, 2 * FFT_N2, c), lambda i: (i, 0, 0, 0)),
        scratch_shapes=[pltpu.VMEM((2, 2 * FFT_N1H * FFT_PITCH, LANE), F32),
                        pltpu.VMEM((2, FFT_R * FFT_PITCH, LANE), F32)],
        compiler_params=pltpu.CompilerParams(dimension_semantics=("arbitrary",), vmem_limit_bytes=VMEM_LIMIT_HY),
        name="hyena_filter_spectrum",
    )(sig, inv_norm, f1ab, f2)


def _conv3_rows(src, dst, cw, cb, L):
    rows = 2 * FFT_N2
    nchunk = L // rows

    def body(i, carry):
        r0 = pl.multiple_of(i * rows, rows)
        prev_row = jnp.where(i > 0, src[pl.ds(jnp.maximum(r0 - 1, 0), 1), :], 0.0)
        next_row = jnp.where(i < nchunk - 1, src[pl.ds(jnp.minimum(r0 + rows, L - 1), 1), :], 0.0)
        _blk_st(dst, 2 * i, 2, _conv3(src[pl.ds(r0, rows), :], prev_row, next_row, cw, cb, rows))
        return carry
    lax.fori_loop(0, nchunk, body, 0, unroll=2)


def _hyena_kernel(v_ref, g_ref, h_ref, cwv_ref, cbv_ref, cwg_ref, cbg_ref, skip_ref,
                  f1_ref, f2_ref, f2t_ref, g1_ref, o_ref, z_ref, gc_ref, a_ref):
    order = pl.program_id(1)

    @pl.when(order == 0)
    def _():
        _conv3_rows(v_ref.at[0], z_ref, cwv_ref[...], cbv_ref[...], FFT_L)

    _conv3_rows(g_ref.at[0], gc_ref, cwg_ref[0], cbg_ref[0], FFT_L)
    _fft_stage1(z_ref, a_ref, f1_ref)

    def mid(k1, carry):
        x = jnp.dot(f2_ref[...], _blk_ld(a_ref, 2 * k1, 2).astype(BF16), preferred_element_type=F32)
        h = h_ref[0, k1].astype(F32)
        xr, xi, hr, hi = x[:FFT_N2], x[FFT_N2:], h[:FFT_N2], h[FFT_N2:]
        y = jnp.concatenate([xr * hr - xi * hi, xr * hi + xi * hr], axis=0).astype(BF16)
        _blk_st(a_ref, 2 * k1, 2, jnp.dot(f2t_ref[...], y, preferred_element_type=F32))
        return carry
    lax.fori_loop(0, FFT_K1, mid, 0, unroll=FFT_K1_UNROLL)

    skip = skip_ref[0]

    def last(n2, carry):
        bs = _sld(a_ref, n2, FFT_R).astype(BF16)
        y = jnp.dot(g1_ref[n2], bs, preferred_element_type=F32)
        _sst(z_ref, n2, FFT_N1H, _sld(gc_ref, n2, FFT_N1H) * (y + skip * _sld(z_ref, n2, FFT_N1H)))
        return carry
    lax.fori_loop(0, FFT_N2, last, 0, unroll=FFT_UNROLL)

    @pl.when(order == 1)
    def _():
        for n1 in range(FFT_N1H):
            o_ref[0, n1 * FFT_N2:(n1 + 1) * FFT_N2, :] = _blk_ld(z_ref, n1, 1)


def hyena_long(u, h_spec, cw, cb, skip):
    b, L, _ = u.shape
    c = HYENA_WIDTH
    f1, _, f2, f2t, g1 = _fft_tables()
    one = pl.Buffered(1)
    cw3 = cw.reshape(3, 3, c).transpose(1, 0, 2)
    cb3 = cb.reshape(3, 1, c)
    return pl.pallas_call(
        _hyena_kernel,
        out_shape=jax.ShapeDtypeStruct((b, L, c), F32),
        grid=(b, 2),
        in_specs=[pl.BlockSpec((1, L, c), lambda bi, o: (bi, 0, 0), pipeline_mode=one),
                  pl.BlockSpec((1, L, c), lambda bi, o: (bi, 0, 1 + o)),
                  pl.BlockSpec((1, FFT_K1, 2 * FFT_N2, c), lambda bi, o: (o, 0, 0, 0)),
                  pl.BlockSpec((3, c), lambda bi, o: (0, 0)),
                  pl.BlockSpec((1, c), lambda bi, o: (0, 0)),
                  pl.BlockSpec((1, 3, c), lambda bi, o: (1 + o, 0, 0)),
                  pl.BlockSpec((1, 1, c), lambda bi, o: (1 + o, 0, 0)),
                  pl.BlockSpec((1, 1, c), lambda bi, o: (o, 0, 0)),
                  pl.BlockSpec(f1.shape, lambda bi, o: (0, 0, 0), pipeline_mode=one),
                  pl.BlockSpec(f2.shape, lambda bi, o: (0, 0), pipeline_mode=one),
                  pl.BlockSpec(f2t.shape, lambda bi, o: (0, 0), pipeline_mode=one),
                  pl.BlockSpec(g1.shape, lambda bi, o: (0, 0, 0), pipeline_mode=one)],
        out_specs=pl.BlockSpec((1, L, c), lambda bi, o: (bi, 0, 0)),
        scratch_shapes=[pltpu.VMEM((2, FFT_N1H * FFT_PITCH, LANE), F32),
                        pltpu.VMEM((2, FFT_N1H * FFT_PITCH, LANE), F32),
                        pltpu.VMEM((2, FFT_R * FFT_PITCH, LANE), F32)],
        compiler_params=pltpu.CompilerParams(dimension_semantics=("parallel", "arbitrary"),
                                             vmem_limit_bytes=VMEM_LIMIT_HY),
        name="hyena_long_conv",
    )(u, u, h_spec, cw3[0], cb3[0], cw3, cb3, skip.reshape(2, 1, c), f1, f2, f2t, g1)


HY_FILT = 2 * HYENA_ORDER * HYENA_WIDTH
HY_HALF = HYENA_ORDER * HYENA_WIDTH


def _filter_kernel(wt_ref, wc_ref, ws_ref, b1_ref, w2_ref, b2_ref, w3f_ref, w3b_ref, freq_ref, decay_ref,
                   sig_ref, asum_ref, *, L, rows):
    i = pl.program_id(0)
    fh = w2_ref.shape[0] // 2
    scale = 1.0 / float(max(L - 1, 1))

    @pl.when(i == 0)
    def _():
        asum_ref[...] = jnp.zeros_like(asum_ref)

    n_lane = (i * rows + lax.broadcasted_iota(jnp.int32, (1, rows), 1)).astype(F32)
    sub = lax.broadcasted_iota(jnp.int32, (2 * HYENA_BANDS, 1), 0)
    bands = 1e-4 + (sub % HYENA_BANDS).astype(F32) * ((HYENA_BANDS - 1 - 1e-4) / (HYENA_BANDS - 1))
    ang_t = (2 * math.pi / L) * jnp.where(sub < HYENA_BANDS, n_lane, L - n_lane) * bands
    cos_f, sin_f = jnp.cos(ang_t).T, jnp.sin(ang_t).T

    n = (i * rows + lax.broadcasted_iota(jnp.int32, (rows, 1), 0)).astype(F32)
    tu_f, tu_b = n * scale, (L - n) * scale
    lane = lax.broadcasted_iota(jnp.int32, (rows, 2 * fh), 1)
    freq = freq_ref[...]
    pre = (jnp.where(lane < fh, tu_f, tu_b) * wt_ref[...]
           + _dot3(cos_f, wc_ref[...]) - _dot3(sin_f, ws_ref[...]) + b1_ref[...])
    hdn = jnp.sin(freq * pre)
    hdn = jnp.sin(freq * (_dot3(hdn, w2_ref[...]) + b2_ref[...]))
    for side, (w3_ref, tu) in enumerate(((w3f_ref, tu_f), (w3b_ref, tu_b))):
        cols = slice(side * HY_HALF, (side + 1) * HY_HALF)
        val = _dot3(hdn, w3_ref[...])
        val = val * jnp.exp(-tu * jnp.abs(decay_ref[:, cols]))
        if side == 1:
            val = jnp.where(n > 0, val, 0.0)
        for o in range(HYENA_ORDER):
            sig_ref[2 * o + side] = val[:, o * HYENA_WIDTH:(o + 1) * HYENA_WIDTH]
        asum_ref[side:side + 1, :] += jnp.sum(jnp.abs(val), axis=0, keepdims=True)


def _block_diag2(w):
    z = jnp.zeros_like(w)
    return jnp.concatenate([jnp.concatenate([w, z], axis=1), jnp.concatenate([z, w], axis=1)], axis=0)


def hyena_filter_signals(L, p):
    rows = min(L, TOKEN_TILE)
    fh = p['hy_pos_w1'].shape[1]
    w1, w3 = p['hy_pos_w1'], p['hy_pos_w3']
    twice = lambda v: jnp.tile(v.reshape(1, fh), (1, 2))
    zeros = jnp.zeros((fh, HY_HALF), F32)
    consts = [twice(w1[0]), _block_diag2(w1[1:1 + HYENA_BANDS]), _block_diag2(w1[1 + HYENA_BANDS:]),
              twice(p['hy_pos_b1']), _block_diag2(p['hy_pos_w2']), twice(p['hy_pos_b2']),
              jnp.concatenate([w3[:, :HY_HALF], zeros], axis=0), jnp.concatenate([zeros, w3[:, HY_HALF:]], axis=0),
              twice(p['hy_freq']), p['hy_decay'].reshape(1, HY_FILT)]
    sig, asum = pl.pallas_call(
        functools.partial(_filter_kernel, L=L, rows=rows),
        out_shape=(jax.ShapeDtypeStruct((2 * HYENA_ORDER, L, HYENA_WIDTH), F32),
                   jax.ShapeDtypeStruct((SUBLANE, HY_HALF), F32)),
        grid=(L // rows,),
        in_specs=[pl.BlockSpec(c.shape, lambda i: (0, 0)) for c in consts],
        out_specs=(pl.BlockSpec((2 * HYENA_ORDER, rows, HYENA_WIDTH), lambda i: (0, i, 0)),
                   pl.BlockSpec((SUBLANE, HY_HALF), lambda i: (0, 0))),
        compiler_params=_params("arbitrary"),
        name="hyena_filters",
    )(*consts)
    inv_norm = 1.0 / (asum[0] + asum[1]).reshape(HYENA_ORDER, 1, HYENA_WIDTH)
    return sig, inv_norm


def _rdft_tables(L):
    n_bins = L + 1
    half = -(-n_bins // 16) * 16
    k = np.arange(n_bins)[:, None]
    n = np.arange(L)[None, :]
    th = 2 * np.pi * ((k * n) % (2 * L)) / (2 * L)
    f = np.zeros((2 * half, L))
    f[:n_bins] = np.cos(th)
    f[half:half + n_bins] = -np.sin(th)
    sign = np.where(np.arange(n_bins) % 2 == 0, 1.0, -1.0)[:, None]
    fs = np.zeros_like(f)
    fs[:n_bins] = f[:n_bins] * sign
    fs[half:half + n_bins] = f[half:half + n_bins] * sign
    wgt = np.where((np.arange(n_bins) == 0) | (np.arange(n_bins) == L), 1.0, 2.0)[None, :] / (2 * L)
    g = np.zeros((L, 2 * half))
    g[:, :n_bins] = wgt * np.cos(th).T
    g[:, half:half + n_bins] = -wgt * np.sin(th).T
    return _table_bf16(f), _table_bf16(fs), _table_bf16(g), half


def _hyena_ctx_kernel(u_ref, sig_ref, inorm_ref, cw_ref, cb_ref, skip_ref, f_ref, fs_ref, g_ref, o_ref, *, L, half):
    zero_row = jnp.zeros((1, u_ref.shape[2]), F32)
    uc = _conv3(u_ref[0], zero_row, zero_row, cw_ref[...], cb_ref[...], L)
    z = uc[:, 0:HYENA_WIDTH]
    for o in range(HYENA_ORDER):
        h = (jnp.dot(f_ref[...], sig_ref[2 * o].astype(BF16), preferred_element_type=F32)
             + jnp.dot(fs_ref[...], sig_ref[2 * o + 1].astype(BF16), preferred_element_type=F32)) * inorm_ref[o]
        x = jnp.dot(f_ref[...], z.astype(BF16), preferred_element_type=F32)
        xr, xi, hr, hi = x[:half], x[half:], h[:half], h[half:]
        y = jnp.concatenate([xr * hr - xi * hi, xr * hi + xi * hr], axis=0).astype(BF16)
        conv = jnp.dot(g_ref[...], y, preferred_element_type=F32)
        z = uc[:, (o + 1) * HYENA_WIDTH:(o + 2) * HYENA_WIDTH] * (conv + skip_ref[o] * z)
    o_ref[0] = z


def hyena_short(u, sig, inv_norm, cw, cb, skip):
    b, L, cols = u.shape
    f, fs, g, half = _rdft_tables(L)
    const2 = lambda bi: (0, 0)
    const3 = lambda bi: (0, 0, 0)
    return pl.pallas_call(
        functools.partial(_hyena_ctx_kernel, L=L, half=half),
        out_shape=jax.ShapeDtypeStruct((b, L, HYENA_WIDTH), F32),
        grid=(b,),
        in_specs=[pl.BlockSpec((1, L, cols), lambda bi: (bi, 0, 0)),
                  pl.BlockSpec(sig.shape, const3), pl.BlockSpec(inv_norm.shape, const3),
                  pl.BlockSpec(cw.shape, const2), pl.BlockSpec((1, cols), const2),
                  pl.BlockSpec((HYENA_ORDER, 1, HYENA_WIDTH), const3),
                  pl.BlockSpec(f.shape, const2), pl.BlockSpec(fs.shape, const2), pl.BlockSpec(g.shape, const2)],
        out_specs=pl.BlockSpec((1, L, HYENA_WIDTH), lambda bi: (bi, 0, 0)),
        compiler_params=_params("parallel"),
        name="hyena_context",
    )(u, sig, inv_norm, cw, cb.reshape(1, cols), skip.reshape(HYENA_ORDER, 1, HYENA_WIDTH), f, fs, g)


def _split_bf16(a):
    hi = a.astype(BF16)
    return hi, (a - hi.astype(F32)).astype(BF16)


def _dot3(a, b):
    a_hi, a_lo = _split_bf16(a)
    b_hi, b_lo = _split_bf16(b)
    return (jnp.dot(a_hi, b_hi, preferred_element_type=F32) + jnp.dot(a_hi, b_lo, preferred_element_type=F32)
            + jnp.dot(a_lo, b_hi, preferred_element_type=F32))


def _norm_router_kernel(*refs, n_lat, two_src):
    if two_src:
        xl_ref, xc_ref, nw_ref, sh_ref, sc_ref, whi_ref, wlo_ref, br_ref, tri_ref, h_ref, lg_ref, cnt_ref = refs
        x = jnp.where(pl.program_id(0) < n_lat, xl_ref[...], xc_ref[...])
    else:
        xl_ref, nw_ref, sh_ref, sc_ref, whi_ref, wlo_ref, br_ref, tri_ref, h_ref, lg_ref, cnt_ref = refs
        x = xl_ref[...]
    h = _modnorm(x, nw_ref[...], sh_ref[0], sc_ref[0])
    h_ref[...] = h
    h_hi, h_lo = _split_bf16(h)
    lg = (jnp.dot(h_hi, whi_ref[...], preferred_element_type=F32)
          + jnp.dot(h_hi, wlo_ref[...], preferred_element_type=F32)
          + jnp.dot(h_lo, whi_ref[...], preferred_element_type=F32)) + br_ref[...]
    lane = lax.broadcasted_iota(jnp.int32, lg.shape, 1)
    first = lambda hit: jnp.min(jnp.where(hit, lane, ROUTER_COLS), axis=-1, keepdims=True)
    gl = jnp.where(lane < N_GROUPS, lg, -jnp.inf)
    gmax = jnp.max(gl, axis=-1, keepdims=True)
    grp = first(gl == gmax)
    grp_p = 1.0 / jnp.sum(jnp.exp(gl - gmax), axis=-1, keepdims=True)
    lo = N_GROUPS + grp * EXPERTS_PER_GROUP
    el = jnp.where((lane >= lo) & (lane < lo + EXPERTS_PER_GROUP), lg, -jnp.inf)
    e1 = jnp.max(el, axis=-1, keepdims=True)
    i1 = first(el == e1)
    el2 = jnp.where(lane == i1, -jnp.inf, el)
    e2 = jnp.max(el2, axis=-1, keepdims=True)
    i2 = first(el2 == e2)
    r = jnp.exp(e2 - e1)
    w1 = grp_p / (1.0 + r)
    w2 = w1 * r
    @pl.when(pl.program_id(0) == 0)
    def _():
        cnt_ref[...] = jnp.zeros_like(cnt_ref)

    hit1, hit2 = lane == i1, lane == i2
    picks = jnp.where(hit1 | hit2, 1.0, 0.0)
    before = cnt_ref[0:1, :] + jnp.dot(tri_ref[...], picks.astype(BF16), preferred_element_type=F32)
    rank1 = jnp.sum(jnp.where(hit1, before, 0.0), axis=-1, keepdims=True)
    rank2 = jnp.sum(jnp.where(hit2, before, 0.0), axis=-1, keepdims=True)
    cnt_ref[...] = jnp.broadcast_to(cnt_ref[0:1, :] + jnp.sum(picks, axis=0, keepdims=True), cnt_ref.shape)
    vals = [(i1 - N_GROUPS).astype(F32), (i2 - N_GROUPS).astype(F32), w1, w2, rank1, rank2]
    out = jnp.zeros(lg.shape, F32)
    for k, val in enumerate(vals):
        out = jnp.where(lane == k, val, out)
    lg_ref[...] = out


def norm_router(xl, xc, nw, mod_l, mod_c, w_router, b_router, tm=TOKEN_TILE):
    b, L, d = xl.shape
    two_src = xc is not None
    n_lat = b * L // tm
    per_batch = L // tm
    n_ctx = (xc.shape[0] * xc.shape[1]) // tm if two_src else 0
    w_hi, w_lo = _split_bf16(w_router)
    const = lambda i: (0, 0)
    if two_src:
        shift = jnp.concatenate([mod_l[0], mod_c[0]], axis=0)
        scale = jnp.concatenate([mod_l[1], mod_c[1]], axis=0)
        mod_map = lambda i: (jnp.where(i < n_lat, i // per_batch, b), 0, 0)
        srcs = [xl.reshape(b * L, d), xc.reshape(-1, d)]
        src_specs = [pl.BlockSpec((tm, d), lambda i: (jnp.minimum(i, n_lat - 1), 0)),
                     pl.BlockSpec((tm, d), lambda i: (jnp.maximum(i - n_lat, 0), 0))]
    else:
        shift, scale = mod_l
        mod_map = lambda i: (i // per_batch, 0, 0)
        srcs = [xl.reshape(b * L, d)]
        src_specs = [pl.BlockSpec((tm, d), lambda i: (i, 0))]
    n_tok = (n_lat + n_ctx) * tm
    tri = _table_bf16(np.tril(np.ones((tm, tm)), -1))
    return pl.pallas_call(
        functools.partial(_norm_router_kernel, n_lat=n_lat, two_src=two_src),
        out_shape=(jax.ShapeDtypeStruct((n_tok, d), F32), jax.ShapeDtypeStruct((n_tok, ROUTER_COLS), F32),
                   jax.ShapeDtypeStruct((SUBLANE, ROUTER_COLS), F32)),
        grid=(n_lat + n_ctx,),
        in_specs=src_specs + [pl.BlockSpec((1, d), const), pl.BlockSpec((1, 1, d), mod_map),
                              pl.BlockSpec((1, 1, d), mod_map), pl.BlockSpec((d, ROUTER_COLS), const),
                              pl.BlockSpec((d, ROUTER_COLS), const), pl.BlockSpec((1, ROUTER_COLS), const),
                              pl.BlockSpec((tm, tm), const)],
        out_specs=(pl.BlockSpec((tm, d), lambda i: (i, 0)), pl.BlockSpec((tm, ROUTER_COLS), lambda i: (i, 0)),
                   pl.BlockSpec((SUBLANE, ROUTER_COLS), const)),
        compiler_params=_params("arbitrary"),
        name="moe_norm_router",
    )(*srcs, nw.reshape(1, d), shift, scale, w_hi, w_lo, b_router, tri)


def _expert_ffn_kernel(te_ref, x_ref, wg_ref, wu_ref, wd_ref, o_ref):
    x = x_ref[...].astype(BF16)
    g = jnp.dot(x, wg_ref[0].astype(BF16), preferred_element_type=F32)
    u = jnp.dot(x, wu_ref[0].astype(BF16), preferred_element_type=F32)
    o_ref[...] = jnp.dot((_silu(g) * u).astype(BF16), wd_ref[0].astype(BF16), preferred_element_type=F32)


def expert_ffn(x_sorted, tile_expert, w_gate, w_up, w_down, tm):
    r, d = x_sorted.shape
    f = w_gate.shape[-1]
    grid_spec = pltpu.PrefetchScalarGridSpec(
        num_scalar_prefetch=1,
        grid=(r // tm,),
        in_specs=[pl.BlockSpec((tm, d), lambda i, te: (i, 0)),
                  pl.BlockSpec((1, d, f), lambda i, te: (te[i], 0, 0)),
                  pl.BlockSpec((1, d, f), lambda i, te: (te[i], 0, 0)),
                  pl.BlockSpec((1, f, d), lambda i, te: (te[i], 0, 0))],
        out_specs=pl.BlockSpec((tm, d), lambda i, te: (i, 0)),
    )
    return pl.pallas_call(
        _expert_ffn_kernel,
        out_shape=jax.ShapeDtypeStruct((r, d), F32),
        grid_spec=grid_spec,
        compiler_params=_params("arbitrary"),
        name="moe_expert_ffn",
    )(tile_expert, x_sorted, w_gate, w_up, w_down)


def moe_apply(h_tokens, routed, counts_row, w_gate, w_up, w_down, layer, n_lat, tm=EXPERT_TILE):
    t, d = h_tokens.shape
    e_idx, rank = routed[:, 0:2].astype(jnp.int32), routed[:, 4:6].astype(jnp.int32)
    counts = counts_row[0, N_GROUPS:N_GROUPS + N_EXPERTS].astype(jnp.int32)
    n_pairs = 2 * t
    padded = (counts + tm - 1) // tm * tm
    pad_end = jnp.cumsum(padded)
    pad_start = pad_end - padded
    n_rows = n_pairs + N_EXPERTS * tm
    tile_start = jnp.arange(n_rows // tm, dtype=jnp.int32) * tm
    tile_expert = jnp.minimum(jnp.sum((pad_end[None, :] <= tile_start[:, None]).astype(jnp.int32), axis=1),
                              N_EXPERTS - 1)
    onehot = (e_idx[:, :, None] == jnp.arange(N_EXPERTS, dtype=jnp.int32)).astype(jnp.int32)
    pos = jnp.sum(onehot * pad_start, axis=-1) + rank
    j = jnp.arange(tm, dtype=jnp.int32)[None, :]
    fill_key = jnp.where(j < (padded - counts)[:, None], (pad_start + counts)[:, None] + j, n_rows)
    keys = jnp.concatenate([pos.reshape(-1), fill_key.reshape(-1)])
    toks = jnp.concatenate([jnp.arange(n_pairs, dtype=jnp.int32) // 2, jnp.arange(N_EXPERTS * tm, dtype=jnp.int32) % t])
    _, row_token = lax.sort((keys, toks), num_keys=1)
    x_sorted = h_tokens[row_token]
    wg = w_gate.reshape(-1, d, EXPERT_HIDDEN)
    wu = w_up.reshape(-1, d, EXPERT_HIDDEN)
    wd = w_down.reshape(-1, EXPERT_HIDDEN, d)
    y_sorted = expert_ffn(x_sorted, tile_expert + layer * N_EXPERTS, wg, wu, wd, tm)
    both = lambda p: y_sorted[jnp.concatenate([p[:, 0], p[:, 1]])].reshape(2, p.shape[0], d)
    return both(pos[:n_lat]), (both(pos[n_lat:]) if t > n_lat else None)


def _final_kernel(x_ref, ya_ref, yb_ref, rt_ref, g_ref, w_ref, o_ref):
    x = x_ref[0] + g_ref[0] * _moe_mix(ya_ref[0], yb_ref[0], rt_ref[0])
    o_ref[0] = x * lax.rsqrt(jnp.mean(x * x, axis=-1, keepdims=True) + EPS) * w_ref[...]


def final_norm(x, y2, routed, gate, w, tm=TOKEN_TILE):
    b, L, d = x.shape
    tok = pl.BlockSpec((1, tm, d), lambda bi, i: (bi, i, 0))
    return pl.pallas_call(
        _final_kernel,
        out_shape=jax.ShapeDtypeStruct((b, L, d), F32),
        grid=(b, L // tm),
        in_specs=[tok, _tok_spec(L, d, tm, False, pick=0), _tok_spec(L, d, tm, False, pick=1),
                  pl.BlockSpec((1, tm, ROUTER_COLS), lambda bi, i: (bi, i, 0)),
                  pl.BlockSpec((1, 1, d), _mod_map(gate, b)), pl.BlockSpec((1, d), lambda bi, i: (0, 0))],
        out_specs=tok,
        compiler_params=_params("parallel", "arbitrary"),
        name="final_rmsnorm",
    )(x, y2, y2, routed, gate, w.reshape(1, d))


IN_SIZES = (SSD_WIDTH, SSD_CONV_CH, 2 * SSD_HEADS, HY_COLS, 2 * ML_WIDTH, ML_WIDTH, ML_WIDTH, 4 * ML_HEADS)


def _regroup_kernel(w_ref, o_ref):
    src = dst = 0
    for n in IN_SIZES:
        pad = -n % LANE
        o_ref[0, :, dst:dst + n] = w_ref[0, :, src:src + n].astype(BF16)
        if pad:
            o_ref[0, :, dst + n:dst + n + pad] = jnp.zeros((o_ref.shape[1], pad), BF16)
        src, dst = src + n, dst + n + pad


def regroup_in_weights(w_in):
    depth, d, n_in = w_in.shape
    n_out = sum(n + (-n % LANE) for n in IN_SIZES)
    one = pl.Buffered(1)
    return pl.pallas_call(
        _regroup_kernel,
        out_shape=jax.ShapeDtypeStruct((depth, d, n_out), BF16),
        grid=(depth,),
        in_specs=[pl.BlockSpec((1, d, n_in), lambda i: (i, 0, 0), pipeline_mode=one)],
        out_specs=pl.BlockSpec((1, d, n_out), lambda i: (i, 0, 0)),
        compiler_params=_params("arbitrary"),
        name="regroup_in_weights",
    )(w_in)


def kernel(x, c, ctx, c_ctx, w_mod, b_mod, norm1_w, norm2_w, w_in, w_out, ssd_conv_w, ssd_conv_b, ssd_dt_bias, ssd_a_log, ssd_d, ssd_norm_w, hy_conv_w, hy_conv_b, hy_pos_w1, hy_pos_b1, hy_pos_w2, hy_pos_b2, hy_pos_w3, hy_freq, hy_decay, hy_skip, ml_conv_w, ml_conv_b, ml_gate_b, ml_norm_w, grp_router_w, grp_router_b, exp_router_w, exp_router_b, moe_w_gate, moe_w_up, moe_w_down, final_norm_w):
    layer_params = dict(
        ssd_conv_w=ssd_conv_w, ssd_conv_b=ssd_conv_b, ssd_dt_bias=ssd_dt_bias, ssd_a_log=ssd_a_log,
        ssd_d=ssd_d, ssd_norm_w=ssd_norm_w, hy_conv_w=hy_conv_w, hy_conv_b=hy_conv_b,
        hy_pos_w1=hy_pos_w1, hy_pos_b1=hy_pos_b1, hy_pos_w2=hy_pos_w2, hy_pos_b2=hy_pos_b2,
        hy_pos_w3=hy_pos_w3, hy_freq=hy_freq, hy_decay=hy_decay, hy_skip=hy_skip,
        ml_conv_w=ml_conv_w, ml_conv_b=ml_conv_b, ml_gate_b=ml_gate_b, ml_norm_w=ml_norm_w)
    bsz, seq, d = x.shape
    n_ctx = ctx.shape[1]
    xl, xc = x, ctx
    moe_l = moe_c = None
    c_rows = jnp.concatenate([c, c_ctx[None, :], jnp.zeros((SUBLANE - bsz - 1, d), F32)], axis=0)
    w_in_all = regroup_in_weights(w_in)

    def layer_prep(i):
        p = {name: arr[i] for name, arr in layer_params.items()}
        mod = modulation(c_rows, w_mod, b_mod, i).reshape(SUBLANE, N_MOD, 1, d)
        pad = ROUTER_COLS - N_GROUPS - N_EXPERTS
        return dict(
            p=p, sp=ssd_prepare(p), mp=ml_prepare(p),
            mod_l=[mod[:bsz, k] for k in range(N_MOD)],
            mod_c=[mod[bsz:bsz + 1, k] for k in range(N_MOD)],
            w_out=w_out[i].astype(BF16),
            w_router=jnp.pad(jnp.concatenate([grp_router_w[i], exp_router_w[i]], axis=1), ((0, 0), (0, pad))),
            b_router=jnp.pad(jnp.concatenate([grp_router_b[i], exp_router_b[i]]), (0, pad)).reshape(1, ROUTER_COLS),
            h_spec=hyena_filter_spectra(*hyena_filter_signals(seq, p)),
            ctx_filt=hyena_filter_signals(n_ctx, p) if i < DEPTH - 1 else None)

    prep = layer_prep(0)
    for i in range(DEPTH):
        last = i == DEPTH - 1
        p, sp, mp, mod_l, mod_c, w_out_b = prep['p'], prep['sp'], prep['mp'], prep['mod_l'], prep['mod_c'], prep['w_out']

        uc_ssd, uc_hy, uc_ml, xc = norm_proj(xc, moe_c, norm1_w[i], mod_c[0], mod_c[1], w_in_all, i)
        col_major = i % 2 == 1
        ul_ssd, ul_hy, ul_ml, xl = norm_proj(xl, moe_l, norm1_w[i], mod_l[0], mod_l[1], w_in_all, i, col_major)
        y_ssd = ssd_mixer(ul_ssd, uc_ssd, sp)
        y_ml = ml_mixer(ul_ml, uc_ml, mp)
        yl_hy = hyena_long(ul_hy, prep['h_spec'], p['hy_conv_w'], p['hy_conv_b'], p['hy_skip'])
        xl = out_proj(y_ssd, yl_hy, y_ml, xl, mod_l[2], w_out_b, col_major)
        if not last:
            yc_hy = hyena_short(uc_hy, *prep['ctx_filt'], p['hy_conv_w'], p['hy_conv_b'], p['hy_skip'])
            xc = out_proj(y_ssd, yc_hy, y_ml, xc, mod_c[2], w_out_b, scan_off=seq // n_ctx)
        h_all, routed, counts_row = norm_router(xl, None if last else xc, norm2_w[i], (mod_l[3], mod_l[4]),
                                                (mod_c[3], mod_c[4]), prep['w_router'], prep['b_router'])
        if not last:
            prep = layer_prep(i + 1)
        lat, rest = moe_apply(h_all, routed, counts_row, moe_w_gate, moe_w_up, moe_w_down, i, bsz * seq)
        n_lat = bsz * seq
        moe_l = (lat.reshape(2, bsz, seq, d), routed[:n_lat].reshape(bsz, seq, ROUTER_COLS), mod_l[5])
        if not last:
            moe_c = (rest.reshape(2, bsz, n_ctx, d), routed[n_lat:].reshape(bsz, n_ctx, ROUTER_COLS), mod_c[5])
    return final_norm(xl, *moe_l, final_norm_w)
```

```python
import functools
import math

import jax
import jax.numpy as jnp
import numpy as np
from jax import lax
from jax.experimental import pallas as pl
from jax.experimental.pallas import tpu as pltpu

D_MODEL = 1024
DEPTH = 2
GRID_W = 64
HEAD_DIM = 64
SSD_WIDTH = 384
SSD_HEADS = SSD_WIDTH // HEAD_DIM
SSD_GROUPS = 2
SSD_STATE = 64
HYENA_WIDTH = 256
HYENA_ORDER = 2
HYENA_BANDS = 16
ML_WIDTH = 384
ML_HEADS = ML_WIDTH // HEAD_DIM
N_GROUPS = 4
EXPERTS_PER_GROUP = 8
N_EXPERTS = N_GROUPS * EXPERTS_PER_GROUP
EXPERT_HIDDEN = 256
N_MOD = 6
EPS = 1e-6

LANE = 128
SUBLANE = 8
VMEM_LIMIT = 48 * 1024 * 1024
VMEM_LIMIT_HY = 56 * 1024 * 1024

TOKEN_TILE = 512
EXPERT_TILE = 256
MOD_TILE = 1536
SCAN_CHUNK = 256

SSD_CONV_CH = SSD_WIDTH + 2 * SSD_GROUPS * SSD_STATE
SSD_XBC0 = SSD_WIDTH
SSD_DT0 = SSD_XBC0 + SSD_CONV_CH
SSD_COLS = SSD_DT0 + LANE
HY_COLS = (HYENA_ORDER + 1) * HYENA_WIDTH
ML_V0 = 2 * ML_WIDTH
ML_O0 = ML_V0 + ML_WIDTH
ML_G0 = ML_O0 + ML_WIDTH
ML_COLS = ML_G0 + LANE
ROUTER_COLS = LANE

F32 = jnp.float32
BF16 = jnp.bfloat16
HI = lax.Precision.HIGHEST


def _params(*sem):
    return pltpu.CompilerParams(dimension_semantics=sem, vmem_limit_bytes=VMEM_LIMIT)


def _table_bf16(a):
    return jnp.asarray(np.asarray(a, np.float32)).astype(BF16)


def _silu(x):
    return x * jax.nn.sigmoid(x)


def _softplus(x):
    return jnp.maximum(x, 0.0) + jnp.log(1.0 + jnp.exp(-jnp.abs(x)))


def _log_sigmoid(x):
    return jnp.minimum(x, 0.0) - jnp.log(1.0 + jnp.exp(-jnp.abs(x)))


def _mod_kernel(c_ref, w_ref, b_ref, o_ref):
    o_ref[...] = jnp.dot(_silu(c_ref[...]), w_ref[...], preferred_element_type=F32, precision=HI) + b_ref[...]


def modulation(c_rows, w_mod, b_mod, layer):
    depth, d, n = w_mod.shape
    tn = MOD_TILE
    return pl.pallas_call(
        _mod_kernel,
        out_shape=jax.ShapeDtypeStruct((c_rows.shape[0], n), F32),
        grid=(n // tn,),
        in_specs=[pl.BlockSpec(c_rows.shape, lambda j: (0, 0)),
                  pl.BlockSpec((None, d, tn), lambda j: (layer, 0, j)),
                  pl.BlockSpec((None, 1, tn), lambda j: (layer, 0, j))],
        out_specs=pl.BlockSpec((c_rows.shape[0], tn), lambda j: (0, j)),
        compiler_params=_params("arbitrary"),
        name="adaln_modulation",
    )(c_rows, w_mod, b_mod.reshape(depth, 1, n))


def _modnorm(x, nw, shift, scale):
    y = x * lax.rsqrt(jnp.mean(x * x, axis=-1, keepdims=True) + EPS) * nw
    return y * (1.0 + scale) + shift


def _mod_map(mod, b):
    return (lambda bi, i: (bi, 0, 0)) if mod.shape[0] == b else (lambda bi, i: (0, 0, 0))


def _tok_view(x, col_major):
    *lead, L, d = x.shape
    return x.reshape(*lead, L // GRID_W, GRID_W, d) if col_major else x


def _tok_spec(L, d, tm, col_major, pick=None):
    if col_major:
        assert tm == (L // GRID_W) * SUBLANE
        shape, index = (1, L // GRID_W, SUBLANE, d), (lambda bi, i: (bi, 0, i, 0))
    else:
        shape, index = (1, tm, d), (lambda bi, i: (bi, i, 0))
    if pick is None:
        return pl.BlockSpec(shape, index)
    return pl.BlockSpec((None,) + shape, lambda bi, i: (pick,) + index(bi, i))


def _tok_load(ref, col_major):
    if not col_major:
        return ref[0]
    return jnp.concatenate([ref[0, :, j, :] for j in range(ref.shape[2])], axis=0)


def _tok_store(ref, val, col_major):
    if not col_major:
        ref[0] = val
        return
    rows = ref.shape[1]
    for j in range(ref.shape[2]):
        ref[0, :, j, :] = val[j * rows:(j + 1) * rows]


def _moe_mix(ya, yb, routed):
    return routed[:, 2:3] * ya + routed[:, 3:4] * yb


def _norm_proj_kernel(*refs, col_major, fuse_moe):
    if fuse_moe:
        x_ref, ya_ref, yb_ref, rt_ref, g_ref, nw_ref, sh_ref, sc_ref, w_ref, ssd_ref, hy_ref, ml_ref, xo_ref = refs
    else:
        x_ref, nw_ref, sh_ref, sc_ref, w_ref, ssd_ref, hy_ref, ml_ref = refs
    x = _tok_load(x_ref, col_major)
    if fuse_moe:
        x = x + g_ref[0] * _moe_mix(_tok_load(ya_ref, col_major), _tok_load(yb_ref, col_major),
                                    _tok_load(rt_ref, col_major))
        _tok_store(xo_ref, x, col_major)
    h = _modnorm(x, nw_ref[...], sh_ref[0], sc_ref[0])
    u = jnp.dot(h.astype(BF16), w_ref[...], preferred_element_type=F32)
    ssd_ref[0] = u[:, 0:SSD_COLS]
    hy_ref[0] = u[:, SSD_COLS:SSD_COLS + HY_COLS]
    ml_ref[0] = u[:, SSD_COLS + HY_COLS:]


def norm_proj(x, moe, nw, shift, scale, w_all, layer, col_major=False, tm=TOKEN_TILE):
    b, L, d = x.shape
    n = w_all.shape[2]
    fuse_moe = moe is not None
    tm = (L // GRID_W) * SUBLANE if col_major else min(tm, L)
    tok = _tok_spec(L, d, tm, col_major)
    row = lambda bi, i: (bi, i, 0)
    const2 = lambda bi, i: (0, 0)
    args, in_specs = [_tok_view(x, col_major)], [tok]
    if fuse_moe:
        y2, routed, gate = moe
        args += [_tok_view(y2, col_major), _tok_view(y2, col_major), _tok_view(routed, col_major), gate]
        in_specs += [_tok_spec(L, d, tm, col_major, pick=0), _tok_spec(L, d, tm, col_major, pick=1),
                     _tok_spec(L, ROUTER_COLS, tm, col_major), pl.BlockSpec((1, 1, d), _mod_map(gate, b))]
    args += [nw.reshape(1, d), shift, scale, w_all]
    in_specs += [pl.BlockSpec((1, d), const2), pl.BlockSpec((1, 1, d), _mod_map(shift, b)),
                 pl.BlockSpec((1, 1, d), _mod_map(scale, b)),
                 pl.BlockSpec((None, d, n), lambda bi, i: (layer, 0, 0))]
    out_shape = [jax.ShapeDtypeStruct((b, L, SSD_COLS), F32), jax.ShapeDtypeStruct((b, L, HY_COLS), F32),
                 jax.ShapeDtypeStruct((b, L, ML_COLS), F32)]
    out_specs = [pl.BlockSpec((1, tm, SSD_COLS), row), pl.BlockSpec((1, tm, HY_COLS), row),
                 pl.BlockSpec((1, tm, ML_COLS), row)]
    if fuse_moe:
        out_shape.append(jax.ShapeDtypeStruct(args[0].shape, F32))
        out_specs.append(tok)
    outs = pl.pallas_call(
        functools.partial(_norm_proj_kernel, col_major=col_major, fuse_moe=fuse_moe),
        out_shape=tuple(out_shape),
        grid=(b, L // tm),
        in_specs=in_specs,
        out_specs=tuple(out_specs),
        compiler_params=_params("parallel", "arbitrary"),
        name="norm_in_proj",
    )(*args)
    return (*outs[:3], outs[3].reshape(b, L, d) if fuse_moe else x)


def _out_proj_kernel(ys_ref, yh_ref, ym_ref, x_ref, g_ref, w_ref, o_ref, *, col_major):
    y = jnp.concatenate([ys_ref[0], yh_ref[0], ym_ref[0]], axis=-1).astype(BF16)
    r = _tok_load(x_ref, col_major) + g_ref[0] * jnp.dot(y, w_ref[...], preferred_element_type=F32)
    _tok_store(o_ref, r, col_major)


def out_proj(y_ssd, y_hy, y_ml, x, gate, w_bf16, col_major=False, scan_off=0, tm=TOKEN_TILE):
    b, L, d = x.shape
    tm = (L // GRID_W) * SUBLANE if col_major else min(tm, L)
    row = lambda bi, i: (bi, i, 0)
    scan_row = lambda bi, i: (bi, i + scan_off, 0)
    tok = _tok_spec(L, d, tm, col_major)
    xv = _tok_view(x, col_major)
    return pl.pallas_call(
        functools.partial(_out_proj_kernel, col_major=col_major),
        out_shape=jax.ShapeDtypeStruct(xv.shape, F32),
        grid=(b, L // tm),
        in_specs=[pl.BlockSpec((1, tm, SSD_WIDTH), scan_row), pl.BlockSpec((1, tm, HYENA_WIDTH), row),
                  pl.BlockSpec((1, tm, ML_WIDTH), scan_row), tok,
                  pl.BlockSpec((1, 1, d), _mod_map(gate, b)),
                  pl.BlockSpec(w_bf16.shape, lambda bi, i: (0, 0))],
        out_specs=tok,
        compiler_params=_params("parallel", "arbitrary"),
        name="out_proj_residual",
    )(y_ssd, y_hy, y_ml, xv, gate, w_bf16).reshape(b, L, d)


def _conv3(xr, prev_row, next_row, cw, cb, q):
    rid = lax.broadcasted_iota(jnp.int32, (q, 1), 0)
    x_prev = jnp.where(rid == 0, prev_row, pltpu.roll(xr, 1, axis=0))
    x_next = jnp.where(rid == q - 1, next_row, pltpu.roll(xr, q - 1, axis=0))
    return x_prev * cw[0:1] + xr * cw[1:2] + x_next * cw[2:3] + cb


def _masked_scan(mask, x):
    m = mask.astype(BF16)
    hi = x.astype(BF16)
    r1 = x - hi.astype(F32)
    mid = r1.astype(BF16)
    lo = (r1 - mid.astype(F32)).astype(BF16)
    return (jnp.dot(m, hi, preferred_element_type=F32) + jnp.dot(m, mid, preferred_element_type=F32)
            + jnp.dot(m, lo, preferred_element_type=F32))


def _scan_mask(q, direction):
    li = lax.broadcasted_iota(jnp.int32, (q, q), 0)
    si = lax.broadcasted_iota(jnp.int32, (q, q), 1)
    return (si <= li) if direction == 0 else (si >= li)


def _running_max(x, direction, q):
    rid = lax.broadcasted_iota(jnp.int32, (q, 1), 0)
    s = 1
    while s < q:
        if direction == 0:
            x = jnp.where(rid >= s, jnp.maximum(x, pltpu.roll(x, s, axis=0)), x)
        else:
            x = jnp.where(rid < q - s, jnp.maximum(x, pltpu.roll(x, q - s, axis=0)), x)
        s *= 2
    return x


def _scan_specs(L, q, nc, cols, direction):
    hb = q // SUBLANE
    nrb = L // SUBLANE
    order = (lambda j: j) if direction == 0 else (lambda j: nc - 1 - j)
    cidx = lambda j: order(jnp.maximum(j - 1, 0))
    specs = [pl.BlockSpec((1, q, cols), lambda bi, j: (bi, cidx(j), 0)),
             pl.BlockSpec((1, SUBLANE, cols), lambda bi, j: (bi, jnp.maximum(cidx(j) * hb - 1, 0), 0)),
             pl.BlockSpec((1, SUBLANE, cols), lambda bi, j: (bi, jnp.minimum((cidx(j) + 1) * hb, nrb - 1), 0)),
             pl.BlockSpec((1, q, cols), lambda bi, j: (bi, 0, 0))]
    yidx = lambda j: jnp.where(j == 0, nc, cidx(j))
    return specs, yidx


def _scan_inputs(u_ref, prev_ref, next_ref, uc_ref, lo, hi, direction, nc):
    j = pl.program_id(1)
    jm = jnp.maximum(j - 1, 0)
    c = jm if direction == 0 else nc - 1 - jm
    is_ctx = j == 0
    x = jnp.where(is_ctx, uc_ref[0, :, lo:hi], u_ref[0, :, lo:hi])
    prev_row = jnp.where(is_ctx | (c == 0), 0.0, prev_ref[0, SUBLANE - 1:SUBLANE, lo:hi])
    next_row = jnp.where(is_ctx | (c == nc - 1), 0.0, next_ref[0, 0:1, lo:hi])
    return x, prev_row, next_row


def _ssd_kernel(*refs, direction, finalize, q, nc):
    if finalize:
        u_ref, prev_ref, next_ref, uc_ref, yb_ref, cw_ref, cb_ref, dtb_ref, a_ref, d_ref, nw_ref, y_ref, state_ref = refs
    else:
        u_ref, prev_ref, next_ref, uc_ref, cw_ref, cb_ref, dtb_ref, a_ref, y_ref, state_ref = refs

    @pl.when(pl.program_id(1) == 0)
    def _():
        state_ref[...] = jnp.zeros_like(state_ref)

    xr, prev_row, next_row = _scan_inputs(u_ref, prev_ref, next_ref, uc_ref, SSD_XBC0, SSD_DT0, direction, nc)
    xc = _silu(_conv3(xr, prev_row, next_row, cw_ref[...], cb_ref[...], q))
    u_rest, _, _ = _scan_inputs(u_ref, prev_ref, next_ref, uc_ref, SSD_DT0, SSD_COLS, direction, nc)

    dt = _softplus(u_rest + dtb_ref[...])
    mask = _scan_mask(q, direction)
    cum = _masked_scan(mask, dt * a_ref[...])
    cum_t = cum.T
    end = q - 1 if direction == 0 else 0

    ys = []
    for g in range(SSD_GROUPS):
        b0 = SSD_WIDTH + g * SSD_STATE
        c0 = SSD_WIDTH + (SSD_GROUPS + g) * SSD_STATE
        bm_t = xc[:, b0:b0 + SSD_STATE].T
        cm = xc[:, c0:c0 + SSD_STATE].astype(BF16)
        scores = jnp.dot(cm, bm_t.astype(BF16), preferred_element_type=F32)
        for h in range(g * (SSD_HEADS // SSD_GROUPS), (g + 1) * (SSD_HEADS // SSD_GROUPS)):
            hl = direction * SSD_HEADS + h
            col = cum[:, hl:hl + 1]
            row = cum_t[hl:hl + 1, :]
            seg = jnp.exp(jnp.where(mask, col - row, -jnp.inf))
            xdt = (xc[:, h * HEAD_DIM:(h + 1) * HEAD_DIM] * dt[:, hl:hl + 1]).astype(BF16)
            y = jnp.dot((scores * seg).astype(BF16), xdt, preferred_element_type=F32)
            st = state_ref[h]
            y = y + jnp.dot(cm, st.astype(BF16), preferred_element_type=F32) * jnp.exp(col)
            tot = cum[end:end + 1, hl:hl + 1]
            upd = jnp.dot((bm_t * jnp.exp(tot - row)).astype(BF16), xdt, preferred_element_type=F32)
            state_ref[h] = st * jnp.exp(tot) + upd
            ys.append(y)
    y_all = jnp.concatenate(ys, axis=-1)
    if finalize:
        z, _, _ = _scan_inputs(u_ref, prev_ref, next_ref, uc_ref, 0, SSD_WIDTH, direction, nc)
        t = (y_all + yb_ref[0] + xc[:, 0:SSD_WIDTH] * d_ref[...]) * _silu(z)
        y_all = t * lax.rsqrt(jnp.mean(t * t, axis=-1, keepdims=True) + EPS) * nw_ref[...]
    y_ref[0] = y_all


def ssd_pass(u, u_ctx, y_other, sp, direction, q):
    b, L, _ = u.shape
    nc = L // q
    finalize = y_other is not None
    in_specs, yidx = _scan_specs(L, q, nc, SSD_COLS, direction)
    const2 = lambda bi, j: (0, 0)
    y_spec = pl.BlockSpec((1, q, SSD_WIDTH), lambda bi, j: (bi, yidx(j), 0))
    args = [u, u, u, u_ctx]
    if finalize:
        in_specs.append(y_spec)
        args.append(y_other)
    consts = [sp['cw'], sp['cb'], sp['dtb'], sp['a']] + ([sp['d'], sp['nw']] if finalize else [])
    in_specs += [pl.BlockSpec(t.shape, const2) for t in consts]
    args += consts
    return pl.pallas_call(
        functools.partial(_ssd_kernel, direction=direction, finalize=finalize, q=q, nc=nc),
        out_shape=jax.ShapeDtypeStruct((b, L + q, SSD_WIDTH), F32),
        grid=(b, nc + 1),
        in_specs=in_specs,
        out_specs=y_spec,
        scratch_shapes=[pltpu.VMEM((SSD_HEADS, HEAD_DIM, SSD_STATE), F32)],
        compiler_params=_params("parallel", "arbitrary"),
        name="ssd_scan_%s" % ("fwd" if direction == 0 else "bwd"),
    )(*args)


def ssd_prepare(p):
    pad = lambda v: jnp.pad(v.reshape(1, -1), ((0, 0), (0, LANE - v.size)))
    return dict(cw=p['ssd_conv_w'], cb=p['ssd_conv_b'].reshape(1, -1),
                dtb=pad(p['ssd_dt_bias']), a=pad(-jnp.exp(p['ssd_a_log'])),
                d=jnp.repeat(p['ssd_d'], HEAD_DIM).reshape(1, -1), nw=p['ssd_norm_w'].reshape(1, -1))


def ssd_mixer(u, u_ctx, sp, q=SCAN_CHUNK):
    assert u_ctx.shape[1] == q
    return ssd_pass(u, u_ctx, ssd_pass(u, u_ctx, None, sp, 1, q), sp, 0, q)


def _ml_kernel(*refs, direction, finalize, q, nc):
    if finalize:
        u_ref, prev_ref, next_ref, uc_ref, hb_ref, cw_ref, cb_ref, gb_ref, nw_ref, pool_ref, y_ref, s_ref, m_ref = refs
    else:
        u_ref, prev_ref, next_ref, uc_ref, cw_ref, cb_ref, gb_ref, y_ref, s_ref, m_ref = refs

    @pl.when(pl.program_id(1) == 0)
    def _():
        s_ref[...] = jnp.zeros_like(s_ref)
        m_ref[...] = jnp.zeros_like(m_ref)

    xr, prev_row, next_row = _scan_inputs(u_ref, prev_ref, next_ref, uc_ref, 0, ML_V0, direction, nc)
    qk = _silu(_conv3(xr, prev_row, next_row, cw_ref[...], cb_ref[...], q))
    rest, _, _ = _scan_inputs(u_ref, prev_ref, next_ref, uc_ref, ML_V0, ML_COLS, direction, nc)
    v = rest[:, 0:ML_WIDTH]

    gb = rest[:, ML_G0 - ML_V0:] + gb_ref[...]
    mask = _scan_mask(q, direction)
    cum = jnp.dot(mask.astype(F32), _log_sigmoid(gb), preferred_element_type=F32, precision=HI)
    ig = pltpu.roll(gb, ML_HEADS, axis=1)
    end = q - 1 if direction == 0 else 0
    m_prev = m_ref[0:1, :]
    tot = cum[end:end + 1, :]
    w_end = tot - cum + ig
    m_loc = jnp.max(w_end, axis=0, keepdims=True)
    e_end = jnp.exp(w_end - m_loc)
    m_new = jnp.maximum(tot + m_prev, m_loc)
    a_prev = jnp.exp(tot + m_prev - m_new)
    a_loc = jnp.exp(m_loc - m_new)
    inter = cum + m_prev
    rel = ig - cum
    m_t = jnp.maximum(inter, cum + _running_max(rel, direction, q))
    col_a = cum - m_t
    a_inter = jnp.exp(inter - m_t)
    floor = jnp.exp(-m_t)
    rel_t = rel.T
    e_end_t = e_end.T
    k_t = (qk[:, ML_WIDTH:2 * ML_WIDTH] * (HEAD_DIM ** -0.5)).T
    one_col = (lax.broadcasted_iota(jnp.int32, (q, HEAD_DIM), 1) == 0).astype(F32)

    ys = []
    for h in range(ML_HEADS):
        fl = direction * 2 * ML_HEADS + ML_HEADS + h
        qh = qk[:, h * HEAD_DIM:(h + 1) * HEAD_DIM].astype(BF16)
        kh_t = k_t[h * HEAD_DIM:(h + 1) * HEAD_DIM, :]
        v_ext = jnp.concatenate([v[:, h * HEAD_DIM:(h + 1) * HEAD_DIM], one_col], axis=-1).astype(BF16)
        pw = jnp.exp(jnp.where(mask, col_a[:, fl:fl + 1] + rel_t[fl:fl + 1, :], -jnp.inf))
        scores = jnp.dot(qh, kh_t.astype(BF16), preferred_element_type=F32)
        nd = jnp.dot((scores * pw).astype(BF16), v_ext, preferred_element_type=F32)
        st = s_ref[h]
        nd = nd + a_inter[:, fl:fl + 1] * jnp.dot(qh, st.astype(BF16), preferred_element_type=F32)
        den = nd[:, HEAD_DIM:HEAD_DIM + 1]
        ys.append(nd[:, 0:HEAD_DIM] / jnp.maximum(jnp.abs(den), floor[:, fl:fl + 1]))
        upd = jnp.dot((kh_t * e_end_t[fl:fl + 1, :]).astype(BF16), v_ext, preferred_element_type=F32)
        s_ref[h] = a_prev[:, fl:fl + 1] * st + a_loc[:, fl:fl + 1] * upd
    m_ref[...] = jnp.broadcast_to(m_new, m_ref.shape)
    y_all = jnp.concatenate(ys, axis=-1)
    if finalize:
        hs = y_all + hb_ref[0]
        hc = hs - jnp.dot(hs.astype(BF16), pool_ref[...], preferred_element_type=F32)
        var = jnp.dot((hc * hc).astype(BF16), pool_ref[...], preferred_element_type=F32)
        y_all = hc * lax.rsqrt(var + EPS) * nw_ref[...] * jax.nn.sigmoid(rest[:, ML_O0 - ML_V0:ML_G0 - ML_V0])
    y_ref[0] = y_all


def ml_pass(u, u_ctx, h_other, mp, direction, q):
    b, L, _ = u.shape
    nc = L // q
    finalize = h_other is not None
    in_specs, yidx = _scan_specs(L, q, nc, ML_COLS, direction)
    const2 = lambda bi, j: (0, 0)
    y_spec = pl.BlockSpec((1, q, ML_WIDTH), lambda bi, j: (bi, yidx(j), 0))
    args = [u, u, u, u_ctx]
    if finalize:
        in_specs.append(y_spec)
        args.append(h_other)
    consts = [mp['cw'], mp['cb'], mp['gb']] + ([mp['nw'], mp['pool']] if finalize else [])
    in_specs += [pl.BlockSpec(t.shape, const2) for t in consts]
    args += consts
    return pl.pallas_call(
        functools.partial(_ml_kernel, direction=direction, finalize=finalize, q=q, nc=nc),
        out_shape=jax.ShapeDtypeStruct((b, L + q, ML_WIDTH), F32),
        grid=(b, nc + 1),
        in_specs=in_specs,
        out_specs=y_spec,
        scratch_shapes=[pltpu.VMEM((ML_HEADS, HEAD_DIM, LANE), F32), pltpu.VMEM((SUBLANE, LANE), F32)],
        compiler_params=_params("parallel", "arbitrary"),
        name="mlstm_scan_%s" % ("fwd" if direction == 0 else "bwd"),
    )(*args)


def ml_prepare(p):
    gb = p['ml_gate_b'].reshape(1, -1)
    head = np.arange(ML_WIDTH) // HEAD_DIM
    pool = _table_bf16((head[:, None] == head[None, :]) / HEAD_DIM)
    return dict(cw=p['ml_conv_w'], cb=p['ml_conv_b'].reshape(1, -1),
                gb=jnp.pad(gb, ((0, 0), (0, LANE - gb.shape[1]))), nw=p['ml_norm_w'].reshape(1, -1), pool=pool)


def ml_mixer(u, u_ctx, mp, q=SCAN_CHUNK):
    assert u_ctx.shape[1] == q
    return ml_pass(u, u_ctx, ml_pass(u, u_ctx, None, mp, 1, q), mp, 0, q)


FFT_L = 4096
FFT_N = 2 * FFT_L
FFT_N2 = 128
FFT_N1 = FFT_N // FFT_N2
FFT_N1H = FFT_L // FFT_N2
FFT_K1 = FFT_N1 // 2 + 1
FFT_R = 80
FFT_UNROLL = 16
FFT_K1_UNROLL = 11
FFT_PITCH = FFT_N2 + SUBLANE


def _fft_tables():
    n2 = np.arange(FFT_N2)[:, None, None]
    k1 = np.arange(FFT_K1)[None, :, None]
    n1 = np.arange(FFT_N1H)[None, None, :]
    th = 2 * np.pi * (((FFT_N2 * n1 + n2) * k1) % FFT_N) / FFT_N
    f1 = np.zeros((FFT_N2, FFT_R, FFT_N1H))
    f1[:, 0:2 * FFT_K1:2, :] = np.cos(th)
    f1[:, 1:2 * FFT_K1:2, :] = -np.sin(th)
    wgt = np.where((np.arange(FFT_K1) == 0) | (np.arange(FFT_K1) == FFT_N1 // 2), 1.0, 2.0)[None, :, None] / FFT_N
    g1 = np.zeros((FFT_N2, FFT_N1H, FFT_R))
    g1[:, :, 0:2 * FFT_K1:2] = np.transpose(wgt * np.cos(th), (0, 2, 1))
    g1[:, :, 1:2 * FFT_K1:2] = np.transpose(-wgt * np.sin(th), (0, 2, 1))
    ph = 2 * np.pi * ((np.arange(FFT_N2)[:, None] * np.arange(FFT_N2)[None, :]) % FFT_N2) / FFT_N2
    c, s = np.cos(ph), np.sin(ph)
    f2 = np.block([[c, s], [-s, c]])
    sign = np.where(np.arange(FFT_R) // 2 % 2 == 0, 1.0, -1.0)[None, :, None]
    f1ab = np.concatenate([f1, f1 * sign], axis=2)
    return _table_bf16(f1), _table_bf16(f1ab), _table_bf16(f2), _table_bf16(f2.T), _table_bf16(g1)


def _sld(ref, n2, count):
    rows = pl.ds(n2, count, stride=FFT_PITCH)
    return jnp.concatenate([ref[0, rows, :], ref[1, rows, :]], axis=-1)


def _sst(ref, n2, count, val):
    rows = pl.ds(n2, count, stride=FFT_PITCH)
    ref[0, rows, :] = val[:, 0:LANE]
    ref[1, rows, :] = val[:, LANE:2 * LANE]


def _blk_ld(ref, blk, nblk):
    parts = []
    for k in range(nblk):
        rows = pl.ds(pl.multiple_of((blk + k) * FFT_PITCH, SUBLANE), FFT_N2)
        parts.append(jnp.concatenate([ref[0, rows, :], ref[1, rows, :]], axis=-1))
    return parts[0] if nblk == 1 else jnp.concatenate(parts, axis=0)


def _blk_st(ref, blk, nblk, val):
    for k in range(nblk):
        rows = pl.ds(pl.multiple_of((blk + k) * FFT_PITCH, SUBLANE), FFT_N2)
        ref[0, rows, :] = val[k * FFT_N2:(k + 1) * FFT_N2, 0:LANE]
        ref[1, rows, :] = val[k * FFT_N2:(k + 1) * FFT_N2, LANE:2 * LANE]


def _fft_stage1(z_ref, a_ref, f1_ref, n_in=FFT_N1H):
    def body(n2, carry):
        xs = _sld(z_ref, n2, n_in).astype(BF16)
        _sst(a_ref, n2, FFT_R, jnp.dot(f1_ref[n2], xs, preferred_element_type=F32))
        return carry
    lax.fori_loop(0, FFT_N2, body, 0, unroll=FFT_UNROLL)


def _spectrum_kernel(x_ref, inorm_ref, f1_ref, f2_ref, o_ref, z_ref, a_ref):
    for half in range(2):
        for n1 in range(FFT_N1H):
            _blk_st(z_ref, half * FFT_N1H + n1, 1, x_ref[half, n1 * FFT_N2:(n1 + 1) * FFT_N2, :])
    _fft_stage1(z_ref, a_ref, f1_ref, 2 * FFT_N1H)
    inorm = inorm_ref[0]

    def body(k1, carry):
        slab = _blk_ld(a_ref, 2 * k1, 2).astype(BF16)
        o_ref[0, k1] = (jnp.dot(f2_ref[...], slab, preferred_element_type=F32) * inorm).astype(BF16)
        return carry
    lax.fori_loop(0, FFT_K1, body, 0, unroll=FFT_K1_UNROLL)


def hyena_filter_spectra(sig, inv_norm):
    _, L, c = sig.shape
    _, f1ab, f2, _, _ = _fft_tables()
    one = pl.Buffered(1)
    return pl.pallas_call(
        _spectrum_kernel,
        out_shape=jax.ShapeDtypeStruct((HYENA_ORDER, FFT_K1, 2 * FFT_N2, c), BF16),
        grid=(HYENA_ORDER,),
        in_specs=[pl.BlockSpec((2, L, c), lambda i: (i, 0, 0)),
                  pl.BlockSpec((1, 1, c), lambda i: (i, 0, 0)),
                  pl.BlockSpec(f1ab.shape, lambda i: (0, 0, 0), pipeline_mode=one),
                  pl.BlockSpec(f2.shape, lambda i: (0, 0), pipeline_mode=one)],
        out_specs=pl.BlockSpec((1, FFT_K1, 2 * FFT_N2, c), lambda i: (i, 0, 0, 0)),
        scratch_shapes=[pltpu.VMEM((2, 2 * FFT_N1H * FFT_PITCH, LANE), F32),
                        pltpu.VMEM((2, FFT_R * FFT_PITCH, LANE), F32)],
        compiler_params=pltpu.CompilerParams(dimension_semantics=("arbitrary",), vmem_limit_bytes=VMEM_LIMIT_HY),
        name="hyena_filter_spectrum",
    )(sig, inv_norm, f1ab, f2)


def _conv3_rows(src, dst, cw, cb, L):
    rows = 2 * FFT_N2
    nchunk = L // rows

    def body(i, carry):
        r0 = pl.multiple_of(i * rows, rows)
        prev_row = jnp.where(i > 0, src[pl.ds(jnp.maximum(r0 - 1, 0), 1), :], 0.0)
        next_row = jnp.where(i < nchunk - 1, src[pl.ds(jnp.minimum(r0 + rows, L - 1), 1), :], 0.0)
        _blk_st(dst, 2 * i, 2, _conv3(src[pl.ds(r0, rows), :], prev_row, next_row, cw, cb, rows))
        return carry
    lax.fori_loop(0, nchunk, body, 0, unroll=2)


def _hyena_kernel(v_ref, g_ref, h_ref, cwv_ref, cbv_ref, cwg_ref, cbg_ref, skip_ref,
                  f1_ref, f2_ref, f2t_ref, g1_ref, o_ref, z_ref, gc_ref, a_ref):
    order = pl.program_id(1)

    @pl.when(order == 0)
    def _():
        _conv3_rows(v_ref.at[0], z_ref, cwv_ref[...], cbv_ref[...], FFT_L)

    _conv3_rows(g_ref.at[0], gc_ref, cwg_ref[0], cbg_ref[0], FFT_L)
    _fft_stage1(z_ref, a_ref, f1_ref)

    def mid(k1, carry):
        x = jnp.dot(f2_ref[...], _blk_ld(a_ref, 2 * k1, 2).astype(BF16), preferred_element_type=F32)
        h = h_ref[0, k1].astype(F32)
        xr, xi, hr, hi = x[:FFT_N2], x[FFT_N2:], h[:FFT_N2], h[FFT_N2:]
        y = jnp.concatenate([xr * hr - xi * hi, xr * hi + xi * hr], axis=0).astype(BF16)
        _blk_st(a_ref, 2 * k1, 2, jnp.dot(f2t_ref[...], y, preferred_element_type=F32))
        return carry
    lax.fori_loop(0, FFT_K1, mid, 0, unroll=FFT_K1_UNROLL)

    skip = skip_ref[0]

    def last(n2, carry):
        bs = _sld(a_ref, n2, FFT_R).astype(BF16)
        y = jnp.dot(g1_ref[n2], bs, preferred_element_type=F32)
        _sst(z_ref, n2, FFT_N1H, _sld(gc_ref, n2, FFT_N1H) * (y + skip * _sld(z_ref, n2, FFT_N1H)))
        return carry
    lax.fori_loop(0, FFT_N2, last, 0, unroll=FFT_UNROLL)

    @pl.when(order == 1)
    def _():
        for n1 in range(FFT_N1H):
            o_ref[0, n1 * FFT_N2:(n1 + 1) * FFT_N2, :] = _blk_ld(z_ref, n1, 1)


def hyena_long(u, h_spec, cw, cb, skip):
    b, L, _ = u.shape
    c = HYENA_WIDTH
    f1, _, f2, f2t, g1 = _fft_tables()
    one = pl.Buffered(1)
    cw3 = cw.reshape(3, 3, c).transpose(1, 0, 2)
    cb3 = cb.reshape(3, 1, c)
    return pl.pallas_call(
        _hyena_kernel,
        out_shape=jax.ShapeDtypeStruct((b, L, c), F32),
        grid=(b, 2),
        in_specs=[pl.BlockSpec((1, L, c), lambda bi, o: (bi, 0, 0), pipeline_mode=one),
                  pl.BlockSpec((1, L, c), lambda bi, o: (bi, 0, 1 + o)),
                  pl.BlockSpec((1, FFT_K1, 2 * FFT_N2, c), lambda bi, o: (o, 0, 0, 0)),
                  pl.BlockSpec((3, c), lambda bi, o: (0, 0)),
                  pl.BlockSpec((1, c), lambda bi, o: (0, 0)),
                  pl.BlockSpec((1, 3, c), lambda bi, o: (1 + o, 0, 0)),
                  pl.BlockSpec((1, 1, c), lambda bi, o: (1 + o, 0, 0)),
                  pl.BlockSpec((1, 1, c), lambda bi, o: (o, 0, 0)),
                  pl.BlockSpec(f1.shape, lambda bi, o: (0, 0, 0), pipeline_mode=one),
                  pl.BlockSpec(f2.shape, lambda bi, o: (0, 0), pipeline_mode=one),
                  pl.BlockSpec(f2t.shape, lambda bi, o: (0, 0), pipeline_mode=one),
                  pl.BlockSpec(g1.shape, lambda bi, o: (0, 0, 0), pipeline_mode=one)],
        out_specs=pl.BlockSpec((1, L, c), lambda bi, o: (bi, 0, 0)),
        scratch_shapes=[pltpu.VMEM((2, FFT_N1H * FFT_PITCH, LANE), F32),
                        pltpu.VMEM((2, FFT_N1H * FFT_PITCH, LANE), F32),
                        pltpu.VMEM((2, FFT_R * FFT_PITCH, LANE), F32)],
        compiler_params=pltpu.CompilerParams(dimension_semantics=("parallel", "arbitrary"),
                                             vmem_limit_bytes=VMEM_LIMIT_HY),
        name="hyena_long_conv",
    )(u, u, h_spec, cw3[0], cb3[0], cw3, cb3, skip.reshape(2, 1, c), f1, f2, f2t, g1)


HY_FILT = 2 * HYENA_ORDER * HYENA_WIDTH
HY_HALF = HYENA_ORDER * HYENA_WIDTH


def _filter_kernel(wt_ref, wc_ref, ws_ref, b1_ref, w2_ref, b2_ref, w3f_ref, w3b_ref, freq_ref, decay_ref,
                   sig_ref, asum_ref, *, L, rows):
    i = pl.program_id(0)
    fh = w2_ref.shape[0] // 2
    scale = 1.0 / float(max(L - 1, 1))

    @pl.when(i == 0)
    def _():
        asum_ref[...] = jnp.zeros_like(asum_ref)

    n_lane = (i * rows + lax.broadcasted_iota(jnp.int32, (1, rows), 1)).astype(F32)
    sub = lax.broadcasted_iota(jnp.int32, (2 * HYENA_BANDS, 1), 0)
    bands = 1e-4 + (sub % HYENA_BANDS).astype(F32) * ((HYENA_BANDS - 1 - 1e-4) / (HYENA_BANDS - 1))
    ang_t = (2 * math.pi / L) * jnp.where(sub < HYENA_BANDS, n_lane, L - n_lane) * bands
    cos_f, sin_f = jnp.cos(ang_t).T, jnp.sin(ang_t).T

    n = (i * rows + lax.broadcasted_iota(jnp.int32, (rows, 1), 0)).astype(F32)
    tu_f, tu_b = n * scale, (L - n) * scale
    lane = lax.broadcasted_iota(jnp.int32, (rows, 2 * fh), 1)
    freq = freq_ref[...]
    pre = (jnp.where(lane < fh, tu_f, tu_b) * wt_ref[...]
           + _dot3(cos_f, wc_ref[...]) - _dot3(sin_f, ws_ref[...]) + b1_ref[...])
    hdn = jnp.sin(freq * pre)
    hdn = jnp.sin(freq * (_dot3(hdn, w2_ref[...]) + b2_ref[...]))
    for side, (w3_ref, tu) in enumerate(((w3f_ref, tu_f), (w3b_ref, tu_b))):
        cols = slice(side * HY_HALF, (side + 1) * HY_HALF)
        val = _dot3(hdn, w3_ref[...])
        val = val * jnp.exp(-tu * jnp.abs(decay_ref[:, cols]))
        if side == 1:
            val = jnp.where(n > 0, val, 0.0)
        for o in range(HYENA_ORDER):
            sig_ref[2 * o + side] = val[:, o * HYENA_WIDTH:(o + 1) * HYENA_WIDTH]
        asum_ref[side:side + 1, :] += jnp.sum(jnp.abs(val), axis=0, keepdims=True)


def _block_diag2(w):
    z = jnp.zeros_like(w)
    return jnp.concatenate([jnp.concatenate([w, z], axis=1), jnp.concatenate([z, w], axis=1)], axis=0)


def hyena_filter_signals(L, p):
    rows = min(L, TOKEN_TILE)
    fh = p['hy_pos_w1'].shape[1]
    w1, w3 = p['hy_pos_w1'], p['hy_pos_w3']
    twice = lambda v: jnp.tile(v.reshape(1, fh), (1, 2))
    zeros = jnp.zeros((fh, HY_HALF), F32)
    consts = [twice(w1[0]), _block_diag2(w1[1:1 + HYENA_BANDS]), _block_diag2(w1[1 + HYENA_BANDS:]),
              twice(p['hy_pos_b1']), _block_diag2(p['hy_pos_w2']), twice(p['hy_pos_b2']),
              jnp.concatenate([w3[:, :HY_HALF], zeros], axis=0), jnp.concatenate([zeros, w3[:, HY_HALF:]], axis=0),
              twice(p['hy_freq']), p['hy_decay'].reshape(1, HY_FILT)]
    sig, asum = pl.pallas_call(
        functools.partial(_filter_kernel, L=L, rows=rows),
        out_shape=(jax.ShapeDtypeStruct((2 * HYENA_ORDER, L, HYENA_WIDTH), F32),
                   jax.ShapeDtypeStruct((SUBLANE, HY_HALF), F32)),
        grid=(L // rows,),
        in_specs=[pl.BlockSpec(c.shape, lambda i: (0, 0)) for c in consts],
        out_specs=(pl.BlockSpec((2 * HYENA_ORDER, rows, HYENA_WIDTH), lambda i: (0, i, 0)),
                   pl.BlockSpec((SUBLANE, HY_HALF), lambda i: (0, 0))),
        compiler_params=_params("arbitrary"),
        name="hyena_filters",
    )(*consts)
    inv_norm = 1.0 / (asum[0] + asum[1]).reshape(HYENA_ORDER, 1, HYENA_WIDTH)
    return sig, inv_norm


def _rdft_tables(L):
    n_bins = L + 1
    half = -(-n_bins // 16) * 16
    k = np.arange(n_bins)[:, None]
    n = np.arange(L)[None, :]
    th = 2 * np.pi * ((k * n) % (2 * L)) / (2 * L)
    f = np.zeros((2 * half, L))
    f[:n_bins] = np.cos(th)
    f[half:half + n_bins] = -np.sin(th)
    sign = np.where(np.arange(n_bins) % 2 == 0, 1.0, -1.0)[:, None]
    fs = np.zeros_like(f)
    fs[:n_bins] = f[:n_bins] * sign
    fs[half:half + n_bins] = f[half:half + n_bins] * sign
    wgt = np.where((np.arange(n_bins) == 0) | (np.arange(n_bins) == L), 1.0, 2.0)[None, :] / (2 * L)
    g = np.zeros((L, 2 * half))
    g[:, :n_bins] = wgt * np.cos(th).T
    g[:, half:half + n_bins] = -wgt * np.sin(th).T
    return _table_bf16(f), _table_bf16(fs), _table_bf16(g), half


def _hyena_ctx_kernel(u_ref, sig_ref, inorm_ref, cw_ref, cb_ref, skip_ref, f_ref, fs_ref, g_ref, o_ref, *, L, half):
    zero_row = jnp.zeros((1, u_ref.shape[2]), F32)
    uc = _conv3(u_ref[0], zero_row, zero_row, cw_ref[...], cb_ref[...], L)
    z = uc[:, 0:HYENA_WIDTH]
    for o in range(HYENA_ORDER):
        h = (jnp.dot(f_ref[...], sig_ref[2 * o].astype(BF16), preferred_element_type=F32)
             + jnp.dot(fs_ref[...], sig_ref[2 * o + 1].astype(BF16), preferred_element_type=F32)) * inorm_ref[o]
        x = jnp.dot(f_ref[...], z.astype(BF16), preferred_element_type=F32)
        xr, xi, hr, hi = x[:half], x[half:], h[:half], h[half:]
        y = jnp.concatenate([xr * hr - xi * hi, xr * hi + xi * hr], axis=0).astype(BF16)
        conv = jnp.dot(g_ref[...], y, preferred_element_type=F32)
        z = uc[:, (o + 1) * HYENA_WIDTH:(o + 2) * HYENA_WIDTH] * (conv + skip_ref[o] * z)
    o_ref[0] = z


def hyena_short(u, sig, inv_norm, cw, cb, skip):
    b, L, cols = u.shape
    f, fs, g, half = _rdft_tables(L)
    const2 = lambda bi: (0, 0)
    const3 = lambda bi: (0, 0, 0)
    return pl.pallas_call(
        functools.partial(_hyena_ctx_kernel, L=L, half=half),
        out_shape=jax.ShapeDtypeStruct((b, L, HYENA_WIDTH), F32),
        grid=(b,),
        in_specs=[pl.BlockSpec((1, L, cols), lambda bi: (bi, 0, 0)),
                  pl.BlockSpec(sig.shape, const3), pl.BlockSpec(inv_norm.shape, const3),
                  pl.BlockSpec(cw.shape, const2), pl.BlockSpec((1, cols), const2),
                  pl.BlockSpec((HYENA_ORDER, 1, HYENA_WIDTH), const3),
                  pl.BlockSpec(f.shape, const2), pl.BlockSpec(fs.shape, const2), pl.BlockSpec(g.shape, const2)],
        out_specs=pl.BlockSpec((1, L, HYENA_WIDTH), lambda bi: (bi, 0, 0)),
        compiler_params=_params("parallel"),
        name="hyena_context",
    )(u, sig, inv_norm, cw, cb.reshape(1, cols), skip.reshape(HYENA_ORDER, 1, HYENA_WIDTH), f, fs, g)


def _split_bf16(a):
    hi = a.astype(BF16)
    return hi, (a - hi.astype(F32)).astype(BF16)


def _dot3(a, b):
    a_hi, a_lo = _split_bf16(a)
    b_hi, b_lo = _split_bf16(b)
    return (jnp.dot(a_hi, b_hi, preferred_element_type=F32) + jnp.dot(a_hi, b_lo, preferred_element_type=F32)
            + jnp.dot(a_lo, b_hi, preferred_element_type=F32))


def _norm_router_kernel(*refs, n_lat, two_src):
    if two_src:
        xl_ref, xc_ref, nw_ref, sh_ref, sc_ref, whi_ref, wlo_ref, br_ref, tri_ref, h_ref, lg_ref, cnt_ref = refs
        x = jnp.where(pl.program_id(0) < n_lat, xl_ref[...], xc_ref[...])
    else:
        xl_ref, nw_ref, sh_ref, sc_ref, whi_ref, wlo_ref, br_ref, tri_ref, h_ref, lg_ref, cnt_ref = refs
        x = xl_ref[...]
    h = _modnorm(x, nw_ref[...], sh_ref[0], sc_ref[0])
    h_ref[...] = h
    h_hi, h_lo = _split_bf16(h)
    lg = (jnp.dot(h_hi, whi_ref[...], preferred_element_type=F32)
          + jnp.dot(h_hi, wlo_ref[...], preferred_element_type=F32)
          + jnp.dot(h_lo, whi_ref[...], preferred_element_type=F32)) + br_ref[...]
    lane = lax.broadcasted_iota(jnp.int32, lg.shape, 1)
    first = lambda hit: jnp.min(jnp.where(hit, lane, ROUTER_COLS), axis=-1, keepdims=True)
    gl = jnp.where(lane < N_GROUPS, lg, -jnp.inf)
    gmax = jnp.max(gl, axis=-1, keepdims=True)
    grp = first(gl == gmax)
    grp_p = 1.0 / jnp.sum(jnp.exp(gl - gmax), axis=-1, keepdims=True)
    lo = N_GROUPS + grp * EXPERTS_PER_GROUP
    el = jnp.where((lane >= lo) & (lane < lo + EXPERTS_PER_GROUP), lg, -jnp.inf)
    e1 = jnp.max(el, axis=-1, keepdims=True)
    i1 = first(el == e1)
    el2 = jnp.where(lane == i1, -jnp.inf, el)
    e2 = jnp.max(el2, axis=-1, keepdims=True)
    i2 = first(el2 == e2)
    r = jnp.exp(e2 - e1)
    w1 = grp_p / (1.0 + r)
    w2 = w1 * r
    @pl.when(pl.program_id(0) == 0)
    def _():
        cnt_ref[...] = jnp.zeros_like(cnt_ref)

    hit1, hit2 = lane == i1, lane == i2
    picks = jnp.where(hit1 | hit2, 1.0, 0.0)
    before = cnt_ref[0:1, :] + jnp.dot(tri_ref[...], picks.astype(BF16), preferred_element_type=F32)
    rank1 = jnp.sum(jnp.where(hit1, before, 0.0), axis=-1, keepdims=True)
    rank2 = jnp.sum(jnp.where(hit2, before, 0.0), axis=-1, keepdims=True)
    cnt_ref[...] = jnp.broadcast_to(cnt_ref[0:1, :] + jnp.sum(picks, axis=0, keepdims=True), cnt_ref.shape)
    vals = [(i1 - N_GROUPS).astype(F32), (i2 - N_GROUPS).astype(F32), w1, w2, rank1, rank2]
    out = jnp.zeros(lg.shape, F32)
    for k, val in enumerate(vals):
        out = jnp.where(lane == k, val, out)
    lg_ref[...] = out


def norm_router(xl, xc, nw, mod_l, mod_c, w_router, b_router, tm=TOKEN_TILE):
    b, L, d = xl.shape
    two_src = xc is not None
    n_lat = b * L // tm
    per_batch = L // tm
    n_ctx = (xc.shape[0] * xc.shape[1]) // tm if two_src else 0
    w_hi, w_lo = _split_bf16(w_router)
    const = lambda i: (0, 0)
    if two_src:
        shift = jnp.concatenate([mod_l[0], mod_c[0]], axis=0)
        scale = jnp.concatenate([mod_l[1], mod_c[1]], axis=0)
        mod_map = lambda i: (jnp.where(i < n_lat, i // per_batch, b), 0, 0)
        srcs = [xl.reshape(b * L, d), xc.reshape(-1, d)]
        src_specs = [pl.BlockSpec((tm, d), lambda i: (jnp.minimum(i, n_lat - 1), 0)),
                     pl.BlockSpec((tm, d), lambda i: (jnp.maximum(i - n_lat, 0), 0))]
    else:
        shift, scale = mod_l
        mod_map = lambda i: (i // per_batch, 0, 0)
        srcs = [xl.reshape(b * L, d)]
        src_specs = [pl.BlockSpec((tm, d), lambda i: (i, 0))]
    n_tok = (n_lat + n_ctx) * tm
    tri = _table_bf16(np.tril(np.ones((tm, tm)), -1))
    return pl.pallas_call(
        functools.partial(_norm_router_kernel, n_lat=n_lat, two_src=two_src),
        out_shape=(jax.ShapeDtypeStruct((n_tok, d), F32), jax.ShapeDtypeStruct((n_tok, ROUTER_COLS), F32),
                   jax.ShapeDtypeStruct((SUBLANE, ROUTER_COLS), F32)),
        grid=(n_lat + n_ctx,),
        in_specs=src_specs + [pl.BlockSpec((1, d), const), pl.BlockSpec((1, 1, d), mod_map),
                              pl.BlockSpec((1, 1, d), mod_map), pl.BlockSpec((d, ROUTER_COLS), const),
                              pl.BlockSpec((d, ROUTER_COLS), const), pl.BlockSpec((1, ROUTER_COLS), const),
                              pl.BlockSpec((tm, tm), const)],
        out_specs=(pl.BlockSpec((tm, d), lambda i: (i, 0)), pl.BlockSpec((tm, ROUTER_COLS), lambda i: (i, 0)),
                   pl.BlockSpec((SUBLANE, ROUTER_COLS), const)),
        compiler_params=_params("arbitrary"),
        name="moe_norm_router",
    )(*srcs, nw.reshape(1, d), shift, scale, w_hi, w_lo, b_router, tri)


def _expert_ffn_kernel(te_ref, x_ref, wg_ref, wu_ref, wd_ref, o_ref):
    x = x_ref[...].astype(BF16)
    g = jnp.dot(x, wg_ref[0].astype(BF16), preferred_element_type=F32)
    u = jnp.dot(x, wu_ref[0].astype(BF16), preferred_element_type=F32)
    o_ref[...] = jnp.dot((_silu(g) * u).astype(BF16), wd_ref[0].astype(BF16), preferred_element_type=F32)


def expert_ffn(x_sorted, tile_expert, w_gate, w_up, w_down, tm):
    r, d = x_sorted.shape
    f = w_gate.shape[-1]
    grid_spec = pltpu.PrefetchScalarGridSpec(
        num_scalar_prefetch=1,
        grid=(r // tm,),
        in_specs=[pl.BlockSpec((tm, d), lambda i, te: (i, 0)),
                  pl.BlockSpec((1, d, f), lambda i, te: (te[i], 0, 0)),
                  pl.BlockSpec((1, d, f), lambda i, te: (te[i], 0, 0)),
                  pl.BlockSpec((1, f, d), lambda i, te: (te[i], 0, 0))],
        out_specs=pl.BlockSpec((tm, d), lambda i, te: (i, 0)),
    )
    return pl.pallas_call(
        _expert_ffn_kernel,
        out_shape=jax.ShapeDtypeStruct((r, d), F32),
        grid_spec=grid_spec,
        compiler_params=_params("arbitrary"),
        name="moe_expert_ffn",
    )(tile_expert, x_sorted, w_gate, w_up, w_down)


def moe_apply(h_tokens, routed, counts_row, w_gate, w_up, w_down, layer, n_lat, tm=EXPERT_TILE):
    t, d = h_tokens.shape
    rec = routed[:, 0:SUBLANE].T.astype(jnp.int32)
    e_idx, rank = rec[0:2], rec[4:6]
    counts = counts_row[0, N_GROUPS:N_GROUPS + N_EXPERTS].astype(jnp.int32)
    n_pairs = 2 * t
    padded = (counts + tm - 1) // tm * tm
    pad_end = jnp.cumsum(padded)
    pad_start = pad_end - padded
    n_rows = n_pairs + N_EXPERTS * tm
    tile_start = jnp.arange(n_rows // tm, dtype=jnp.int32) * tm
    tile_expert = jnp.minimum(jnp.sum((pad_end[None, :] <= tile_start[:, None]).astype(jnp.int32), axis=1),
                              N_EXPERTS - 1)
    onehot = (e_idx[None, :, :] == jnp.arange(N_EXPERTS, dtype=jnp.int32)[:, None, None]).astype(jnp.int32)
    pos = jnp.sum(onehot * pad_start[:, None, None], axis=0) + rank
    j = jnp.arange(tm, dtype=jnp.int32)[None, :]
    fill_key = jnp.where(j < (padded - counts)[:, None], (pad_start + counts)[:, None] + j, n_rows)
    keys = jnp.concatenate([pos.reshape(-1), fill_key.reshape(-1)])
    toks = jnp.concatenate([jnp.arange(n_pairs, dtype=jnp.int32) % t, jnp.arange(N_EXPERTS * tm, dtype=jnp.int32) % t])
    _, row_token = lax.sort((keys, toks), num_keys=1)
    x_sorted = h_tokens[row_token]
    wg = w_gate.reshape(-1, d, EXPERT_HIDDEN)
    wu = w_up.reshape(-1, d, EXPERT_HIDDEN)
    wd = w_down.reshape(-1, EXPERT_HIDDEN, d)
    y_sorted = expert_ffn(x_sorted, tile_expert + layer * N_EXPERTS, wg, wu, wd, tm)
    both = lambda p: y_sorted[p.reshape(-1)].reshape(2, p.shape[1], d)
    return both(pos[:, :n_lat]), (both(pos[:, n_lat:]) if t > n_lat else None)


def _final_kernel(x_ref, ya_ref, yb_ref, rt_ref, g_ref, w_ref, o_ref):
    x = x_ref[0] + g_ref[0] * _moe_mix(ya_ref[0], yb_ref[0], rt_ref[0])
    o_ref[0] = x * lax.rsqrt(jnp.mean(x * x, axis=-1, keepdims=True) + EPS) * w_ref[...]


def final_norm(x, y2, routed, gate, w, tm=TOKEN_TILE):
    b, L, d = x.shape
    tok = pl.BlockSpec((1, tm, d), lambda bi, i: (bi, i, 0))
    return pl.pallas_call(
        _final_kernel,
        out_shape=jax.ShapeDtypeStruct((b, L, d), F32),
        grid=(b, L // tm),
        in_specs=[tok, _tok_spec(L, d, tm, False, pick=0), _tok_spec(L, d, tm, False, pick=1),
                  pl.BlockSpec((1, tm, ROUTER_COLS), lambda bi, i: (bi, i, 0)),
                  pl.BlockSpec((1, 1, d), _mod_map(gate, b)), pl.BlockSpec((1, d), lambda bi, i: (0, 0))],
        out_specs=tok,
        compiler_params=_params("parallel", "arbitrary"),
        name="final_rmsnorm",
    )(x, y2, y2, routed, gate, w.reshape(1, d))


IN_SIZES = (SSD_WIDTH, SSD_CONV_CH, 2 * SSD_HEADS, HY_COLS, 2 * ML_WIDTH, ML_WIDTH, ML_WIDTH, 4 * ML_HEADS)


def _regroup_kernel(w_ref, o_ref):
    src = dst = 0
    for n in IN_SIZES:
        pad = -n % LANE
        o_ref[0, :, dst:dst + n] = w_ref[0, :, src:src + n].astype(BF16)
        if pad:
            o_ref[0, :, dst + n:dst + n + pad] = jnp.zeros((o_ref.shape[1], pad), BF16)
        src, dst = src + n, dst + n + pad


def regroup_in_weights(w_in):
    depth, d, n_in = w_in.shape
    n_out = sum(n + (-n % LANE) for n in IN_SIZES)
    one = pl.Buffered(1)
    return pl.pallas_call(
        _regroup_kernel,
        out_shape=jax.ShapeDtypeStruct((depth, d, n_out), BF16),
        grid=(depth,),
        in_specs=[pl.BlockSpec((1, d, n_in), lambda i: (i, 0, 0), pipeline_mode=one)],
        out_specs=pl.BlockSpec((1, d, n_out), lambda i: (i, 0, 0)),
        compiler_params=_params("arbitrary"),
        name="regroup_in_weights",
    )(w_in)


def kernel(x, c, ctx, c_ctx, w_mod, b_mod, norm1_w, norm2_w, w_in, w_out, ssd_conv_w, ssd_conv_b, ssd_dt_bias, ssd_a_log, ssd_d, ssd_norm_w, hy_conv_w, hy_conv_b, hy_pos_w1, hy_pos_b1, hy_pos_w2, hy_pos_b2, hy_pos_w3, hy_freq, hy_decay, hy_skip, ml_conv_w, ml_conv_b, ml_gate_b, ml_norm_w, grp_router_w, grp_router_b, exp_router_w, exp_router_b, moe_w_gate, moe_w_up, moe_w_down, final_norm_w):
    layer_params = dict(
        ssd_conv_w=ssd_conv_w, ssd_conv_b=ssd_conv_b, ssd_dt_bias=ssd_dt_bias, ssd_a_log=ssd_a_log,
        ssd_d=ssd_d, ssd_norm_w=ssd_norm_w, hy_conv_w=hy_conv_w, hy_conv_b=hy_conv_b,
        hy_pos_w1=hy_pos_w1, hy_pos_b1=hy_pos_b1, hy_pos_w2=hy_pos_w2, hy_pos_b2=hy_pos_b2,
        hy_pos_w3=hy_pos_w3, hy_freq=hy_freq, hy_decay=hy_decay, hy_skip=hy_skip,
        ml_conv_w=ml_conv_w, ml_conv_b=ml_conv_b, ml_gate_b=ml_gate_b, ml_norm_w=ml_norm_w)
    bsz, seq, d = x.shape
    n_ctx = ctx.shape[1]
    xl, xc = x, ctx
    moe_l = moe_c = None
    c_rows = jnp.concatenate([c, c_ctx[None, :], jnp.zeros((SUBLANE - bsz - 1, d), F32)], axis=0)
    w_in_all = regroup_in_weights(w_in)

    def layer_prep(i):
        p = {name: arr[i] for name, arr in layer_params.items()}
        mod = modulation(c_rows, w_mod, b_mod, i).reshape(SUBLANE, N_MOD, 1, d)
        pad = ROUTER_COLS - N_GROUPS - N_EXPERTS
        return dict(
            p=p, sp=ssd_prepare(p), mp=ml_prepare(p),
            mod_l=[mod[:bsz, k] for k in range(N_MOD)],
            mod_c=[mod[bsz:bsz + 1, k] for k in range(N_MOD)],
            w_out=w_out[i].astype(BF16),
            w_router=jnp.pad(jnp.concatenate([grp_router_w[i], exp_router_w[i]], axis=1), ((0, 0), (0, pad))),
            b_router=jnp.pad(jnp.concatenate([grp_router_b[i], exp_router_b[i]]), (0, pad)).reshape(1, ROUTER_COLS),
            h_spec=hyena_filter_spectra(*hyena_filter_signals(seq, p)),
            ctx_filt=hyena_filter_signals(n_ctx, p) if i < DEPTH - 1 else None)

    prep = layer_prep(0)
    for i in range(DEPTH):
        last = i == DEPTH - 1
        p, sp, mp, mod_l, mod_c, w_out_b = prep['p'], prep['sp'], prep['mp'], prep['mod_l'], prep['mod_c'], prep['w_out']

        uc_ssd, uc_hy, uc_ml, xc = norm_proj(xc, moe_c, norm1_w[i], mod_c[0], mod_c[1], w_in_all, i)
        col_major = i % 2 == 1
        ul_ssd, ul_hy, ul_ml, xl = norm_proj(xl, moe_l, norm1_w[i], mod_l[0], mod_l[1], w_in_all, i, col_major)
        y_ssd = ssd_mixer(ul_ssd, uc_ssd, sp)
        y_ml = ml_mixer(ul_ml, uc_ml, mp)
        yl_hy = hyena_long(ul_hy, prep['h_spec'], p['hy_conv_w'], p['hy_conv_b'], p['hy_skip'])
        xl = out_proj(y_ssd, yl_hy, y_ml, xl, mod_l[2], w_out_b, col_major)
        if not last:
            yc_hy = hyena_short(uc_hy, *prep['ctx_filt'], p['hy_conv_w'], p['hy_conv_b'], p['hy_skip'])
            xc = out_proj(y_ssd, yc_hy, y_ml, xc, mod_c[2], w_out_b, scan_off=seq // n_ctx)
        h_all, routed, counts_row = norm_router(xl, None if last else xc, norm2_w[i], (mod_l[3], mod_l[4]),
                                                (mod_c[3], mod_c[4]), prep['w_router'], prep['b_router'])
        if not last:
            prep = layer_prep(i + 1)
        lat, rest = moe_apply(h_all, routed, counts_row, moe_w_gate, moe_w_up, moe_w_down, i, bsz * seq)
        n_lat = bsz * seq
        moe_l = (lat.reshape(2, bsz, seq, d), routed[:n_lat].reshape(bsz, seq, ROUTER_COLS), mod_l[5])
        if not last:
            moe_c = (rest.reshape(2, bsz, n_ctx, d), routed[n_lat:].reshape(bsz, n_ctx, ROUTER_COLS), mod_c[5])
    return final_norm(xl, *moe_l, final_norm_w)
```
